```python
import functools
import jax, jax.numpy as jnp
from jax import lax
import numpy as np


D_MODEL = 1024
BATCH = 8
SEQ = 16384
DEPTH = 2

N_MIXERS = 2
N_SUB = 3
EPS = 1e-6

HG_KDIM = 128
HG_HEADS = D_MODEL // HG_KDIM
HG_VDIM = D_MODEL // HG_HEADS
HG_F = HG_HEADS * HG_KDIM
HG_CHUNK = 64

GM_FFN = 6 * D_MODEL
GM_HALF = GM_FFN // 2
GM_GROUPS = 8
GM_GDIM = GM_HALF // GM_GROUPS
GM_CHUNK = 128

D_FF = ((8 * D_MODEL // 3 + 127) // 128) * 128

N_HGRN = (DEPTH + 1) // 2
N_GMLP = DEPTH // 2

kernel_name = "hybrid_hgrn2_gmlp_macaron_adaln"


def rms_norm(h, g):
    hf = h.astype(jnp.float32)
    y = hf * lax.rsqrt(jnp.mean(hf * hf, axis=-1, keepdims=True) + EPS)
    return (y * g.astype(jnp.float32)).astype(h.dtype)


def layer_norm(h, g, b):
    hf = h.astype(jnp.float32)
    mu = jnp.mean(hf, axis=-1, keepdims=True)
    d = hf - mu
    y = d * lax.rsqrt(jnp.mean(d * d, axis=-1, keepdims=True) + EPS)
    return (y * g.astype(jnp.float32) + b.astype(jnp.float32)).astype(h.dtype)


def swiglu(h, w_in, w_out):
    a, b = jnp.split(h @ w_in, 2, axis=-1)
    return (jax.nn.silu(a) * b) @ w_out


def hgrn2_mix(h, w_in, w_out, out_norm, lb):
    B, S, _ = h.shape
    f32 = jnp.float32
    proj = h @ w_in
    q, f, i, g = jnp.split(proj, [HG_F, 2 * HG_F, 2 * HG_F + D_MODEL], axis=-1)
    q = jax.nn.silu(q.astype(f32))
    fx = f.astype(f32)
    log_f = jnp.logaddexp(jnp.log(lb), jnp.log1p(-lb) + jax.nn.log_sigmoid(fx))
    k = (1.0 - lb) * jax.nn.sigmoid(-fx)
    nc = S // HG_CHUNK

    def to_chunks(t, d):
        return t.reshape(B, nc, HG_CHUNK, HG_HEADS, d).transpose(1, 0, 3, 2, 4)

    qc = to_chunks(q, HG_KDIM)
    kc = to_chunks(k, HG_KDIM)
    vc = to_chunks(i.astype(f32), HG_VDIM)
    bc = jnp.cumsum(to_chunks(log_f, HG_KDIM), axis=3)
    causal = jnp.tril(jnp.ones((HG_CHUNK, HG_CHUNK), bool))[:, :, None]

    def step(state, inp):
        qb, kb, vb, bb = inp
        o_inter = jnp.einsum('bhtk,bhkv->bhtv', qb * jnp.exp(bb), state)
        diff = bb[:, :, :, None, :] - bb[:, :, None, :, :]
        decay = jnp.exp(jnp.where(causal, diff, -jnp.inf))
        scores = jnp.einsum('bhtk,bhtsk,bhsk->bhts', qb, decay, kb)
        o_intra = jnp.einsum('bhts,bhsv->bhtv', scores, vb)
        b_last = bb[:, :, -1:, :]
        k_dec = kb * jnp.exp(b_last - bb)
        new_state = state * jnp.exp(b_last[:, :, 0, :, None]) + jnp.einsum('bhsk,bhsv->bhkv', k_dec, vb)
        return new_state, o_inter + o_intra

    state0 = jnp.zeros((B, HG_HEADS, HG_KDIM, HG_VDIM), f32)
    _, o = lax.scan(step, state0, (qc, kc, vc, bc))
    o = o.transpose(1, 0, 3, 2, 4).reshape(B, S, HG_HEADS, HG_VDIM)
    gate = jax.nn.silu(g.astype(f32)).reshape(B, S, HG_HEADS, HG_VDIM)
    o = rms_norm(o, out_norm) * gate
    return o.reshape(B, S, D_MODEL).astype(h.dtype) @ w_out


def gmlp_mix(h, w_in, b_in, ln_g, ln_b, w_s, b_s, w_out):
    B, S, _ = h.shape
    z = jax.nn.gelu(h @ w_in + b_in)
    u, v = jnp.split(z, 2, axis=-1)
    v = layer_norm(v, ln_g, ln_b)
    nc = S // GM_CHUNK
    vc = v.reshape(B, nc, GM_CHUNK, GM_GROUPS, GM_GDIM)
    ws = w_s * jnp.tril(jnp.ones((GM_CHUNK, GM_CHUNK), w_s.dtype))[None]
    vm = jnp.einsum('gts,bnsgc->bntgc', ws, vc) + b_s.T[None, None, :, :, None]
    return (u * vm.reshape(B, S, GM_HALF)) @ w_out


def sublayer(x, fn, pre_g, post_g, shift, scale, gate, res_w):
    h = rms_norm(x, pre_g) * (1.0 + scale) + shift
    return x + res_w * gate * rms_norm(fn(h), post_g)


def _fwd_setup_inputs(seed: int = 0) -> dict:
    key = jax.random.key(seed)
    ks = jax.random.split(key, 24)
    f32 = jnp.float32

    def nrm(k, shape, s):
        return s * jax.random.normal(k, shape, f32)

    return {
        "x": nrm(ks[0], (BATCH, SEQ, D_MODEL), 1.0),
        "c": nrm(ks[1], (BATCH, D_MODEL), 1.0),
        "ada_w": nrm(ks[2], (DEPTH, D_MODEL, 3 * N_SUB * D_MODEL), 0.5 * D_MODEL ** -0.5),
        "ada_b": nrm(ks[3], (DEPTH, 3 * N_SUB * D_MODEL), 0.02),
        "norm_pre": 1.0 + nrm(ks[4], (DEPTH, N_SUB, D_MODEL), 0.02),
        "norm_post": 1.0 + nrm(ks[5], (DEPTH, N_SUB, D_MODEL), 0.02),
        "ffn_w_in": nrm(ks[6], (DEPTH, 2, D_MODEL, 2 * D_FF), D_MODEL ** -0.5),
        "ffn_w_out": nrm(ks[7], (DEPTH, 2, D_FF, D_MODEL), D_FF ** -0.5),
        "hg_w_in": nrm(ks[8], (N_HGRN, D_MODEL, 2 * HG_F + 2 * D_MODEL), D_MODEL ** -0.5),
        "hg_w_out": nrm(ks[9], (N_HGRN, D_MODEL, D_MODEL), D_MODEL ** -0.5),
        "hg_out_norm": 1.0 + nrm(ks[10], (N_HGRN, HG_VDIM), 0.02),
        "hg_lb": nrm(ks[11], (DEPTH + 1, HG_F), 0.1),
        "gm_w_in": nrm(ks[12], (N_GMLP, D_MODEL, GM_FFN), D_MODEL ** -0.5),
        "gm_b_in": nrm(ks[13], (N_GMLP, GM_FFN), 0.02),
        "gm_ln_g": 1.0 + nrm(ks[14], (N_GMLP, GM_HALF), 0.02),
        "gm_ln_b": nrm(ks[15], (N_GMLP, GM_HALF), 0.02),
        "gm_w_s": nrm(ks[16], (N_GMLP, GM_GROUPS, GM_CHUNK, GM_CHUNK), GM_CHUNK ** -0.5),
        "gm_b_s": 1.0 + nrm(ks[17], (N_GMLP, GM_GROUPS, GM_CHUNK), 0.02),
        "gm_w_out": nrm(ks[18], (N_GMLP, GM_HALF, D_MODEL), GM_HALF ** -0.5),
    }


def _fwd_reference(x, c, ada_w, ada_b, norm_pre, norm_post, ffn_w_in, ffn_w_out,
              hg_w_in, hg_w_out, hg_out_norm, hg_lb,
              gm_w_in, gm_b_in, gm_ln_g, gm_ln_b, gm_w_s, gm_b_s, gm_w_out):
    B = x.shape[0]
    lb_all = jnp.cumsum(jax.nn.softmax(hg_lb.astype(jnp.float32), axis=0), axis=0)
    cond = jax.nn.silu(c)
    for i in range(DEPTH):
        mod = (cond @ ada_w[i] + ada_b[i]).reshape(B, 3 * N_SUB, D_MODEL)[:, :, None, :]
        j = i // N_MIXERS
        if i % N_MIXERS == 0:
            mixer = functools.partial(hgrn2_mix, w_in=hg_w_in[j], w_out=hg_w_out[j],
                                      out_norm=hg_out_norm[j], lb=lb_all[i])
        else:
            mixer = functools.partial(gmlp_mix, w_in=gm_w_in[j], b_in=gm_b_in[j], ln_g=gm_ln_g[j],
                                      ln_b=gm_ln_b[j], w_s=gm_w_s[j], b_s=gm_b_s[j], w_out=gm_w_out[j])
        fns = (functools.partial(swiglu, w_in=ffn_w_in[i, 0], w_out=ffn_w_out[i, 0]),
               mixer,
               functools.partial(swiglu, w_in=ffn_w_in[i, 1], w_out=ffn_w_out[i, 1]))
        res_ws = (0.5, 1.0, 0.5)
        for s in range(N_SUB):
            x = sublayer(x, fns[s], norm_pre[i, s], norm_post[i, s],
                         mod[:, 3 * s], mod[:, 3 * s + 1], mod[:, 3 * s + 2], res_ws[s])
    return x


import jax as _jax
import jax.numpy as _jnp

TWIN_FORMAT = 'train_step'
FWD_PARAMS = ['x', 'c', 'ada_w', 'ada_b', 'norm_pre', 'norm_post', 'ffn_w_in', 'ffn_w_out', 'hg_w_in', 'hg_w_out', 'hg_out_norm', 'hg_lb', 'gm_w_in', 'gm_b_in', 'gm_ln_g', 'gm_ln_b', 'gm_w_s', 'gm_b_s', 'gm_w_out']
TWIN_WEIGHTS = ['ada_w', 'ada_b', 'norm_pre', 'norm_post', 'ffn_w_in', 'ffn_w_out', 'hg_w_in', 'hg_w_out', 'hg_out_norm', 'hg_lb', 'gm_w_in', 'gm_b_in', 'gm_ln_g', 'gm_ln_b', 'gm_w_s', 'gm_b_s', 'gm_w_out']
TWIN_DIFF_INPUT = 'x'
TWIN_INPUTS = ['x', 'c', 'ada_w', 'ada_b', 'norm_pre', 'norm_post', 'ffn_w_in', 'ffn_w_out', 'hg_w_in', 'hg_w_out', 'hg_out_norm', 'hg_lb', 'gm_w_in', 'gm_b_in', 'gm_ln_g', 'gm_ln_b', 'gm_w_s', 'gm_b_s', 'gm_w_out', 'loss_target', 'm_ada_w', 'm_ada_b', 'm_norm_pre', 'm_norm_post', 'm_ffn_w_in', 'm_ffn_w_out', 'm_hg_w_in', 'm_hg_w_out', 'm_hg_out_norm', 'm_hg_lb', 'm_gm_w_in', 'm_gm_b_in', 'm_gm_ln_g', 'm_gm_ln_b', 'm_gm_w_s', 'm_gm_b_s', 'm_gm_w_out', 'v_ada_w', 'v_ada_b', 'v_norm_pre', 'v_norm_post', 'v_ffn_w_in', 'v_ffn_w_out', 'v_hg_w_in', 'v_hg_w_out', 'v_hg_out_norm', 'v_hg_lb', 'v_gm_w_in', 'v_gm_b_in', 'v_gm_ln_g', 'v_gm_ln_b', 'v_gm_w_s', 'v_gm_b_s', 'v_gm_w_out']
TWIN_OUTPUTS = ['loss', 'grad_x', 'grad_ada_w', 'grad_ada_b', 'grad_norm_pre', 'grad_norm_post', 'grad_ffn_w_in', 'grad_ffn_w_out', 'grad_hg_w_in', 'grad_hg_w_out', 'grad_hg_out_norm', 'grad_hg_lb', 'grad_gm_w_in', 'grad_gm_b_in', 'grad_gm_ln_g', 'grad_gm_ln_b', 'grad_gm_w_s', 'grad_gm_b_s', 'grad_gm_w_out', 'delta_ada_w', 'delta_ada_b', 'delta_norm_pre', 'delta_norm_post', 'delta_ffn_w_in', 'delta_ffn_w_out', 'delta_hg_w_in', 'delta_hg_w_out', 'delta_hg_out_norm', 'delta_hg_lb', 'delta_gm_w_in', 'delta_gm_b_in', 'delta_gm_ln_g', 'delta_gm_ln_b', 'delta_gm_w_s', 'delta_gm_b_s', 'delta_gm_w_out', 'new_m_ada_w', 'new_m_ada_b', 'new_m_norm_pre', 'new_m_norm_post', 'new_m_ffn_w_in', 'new_m_ffn_w_out', 'new_m_hg_w_in', 'new_m_hg_w_out', 'new_m_hg_out_norm', 'new_m_hg_lb', 'new_m_gm_w_in', 'new_m_gm_b_in', 'new_m_gm_ln_g', 'new_m_gm_ln_b', 'new_m_gm_w_s', 'new_m_gm_b_s', 'new_m_gm_w_out', 'new_v_ada_w', 'new_v_ada_b', 'new_v_norm_pre', 'new_v_norm_post', 'new_v_ffn_w_in', 'new_v_ffn_w_out', 'new_v_hg_w_in', 'new_v_hg_w_out', 'new_v_hg_out_norm', 'new_v_hg_lb', 'new_v_gm_w_in', 'new_v_gm_b_in', 'new_v_gm_ln_g', 'new_v_gm_ln_b', 'new_v_gm_w_s', 'new_v_gm_b_s', 'new_v_gm_w_out']
TWIN_LEAF_KINDS = {'loss': 'loss', 'grad_x': 'grad_x', 'grad_ada_w': 'grad_w', 'grad_ada_b': 'grad_w', 'grad_norm_pre': 'grad_w', 'grad_norm_post': 'grad_w', 'grad_ffn_w_in': 'grad_w', 'grad_ffn_w_out': 'grad_w', 'grad_hg_w_in': 'grad_w', 'grad_hg_w_out': 'grad_w', 'grad_hg_out_norm': 'grad_w', 'grad_hg_lb': 'grad_w', 'grad_gm_w_in': 'grad_w', 'grad_gm_b_in': 'grad_w', 'grad_gm_ln_g': 'grad_w', 'grad_gm_ln_b': 'grad_w', 'grad_gm_w_s': 'grad_w', 'grad_gm_b_s': 'grad_w', 'grad_gm_w_out': 'grad_w', 'delta_ada_w': 'delta_w', 'delta_ada_b': 'delta_w', 'delta_norm_pre': 'delta_w', 'delta_norm_post': 'delta_w', 'delta_ffn_w_in': 'delta_w', 'delta_ffn_w_out': 'delta_w', 'delta_hg_w_in': 'delta_w', 'delta_hg_w_out': 'delta_w', 'delta_hg_out_norm': 'delta_w', 'delta_hg_lb': 'delta_w', 'delta_gm_w_in': 'delta_w', 'delta_gm_b_in': 'delta_w', 'delta_gm_ln_g': 'delta_w', 'delta_gm_ln_b': 'delta_w', 'delta_gm_w_s': 'delta_w', 'delta_gm_b_s': 'delta_w', 'delta_gm_w_out': 'delta_w', 'new_m_ada_w': 'new_m', 'new_m_ada_b': 'new_m', 'new_m_norm_pre': 'new_m', 'new_m_norm_post': 'new_m', 'new_m_ffn_w_in': 'new_m', 'new_m_ffn_w_out': 'new_m', 'new_m_hg_w_in': 'new_m', 'new_m_hg_w_out': 'new_m', 'new_m_hg_out_norm': 'new_m', 'new_m_hg_lb': 'new_m', 'new_m_gm_w_in': 'new_m', 'new_m_gm_b_in': 'new_m', 'new_m_gm_ln_g': 'new_m', 'new_m_gm_ln_b': 'new_m', 'new_m_gm_w_s': 'new_m', 'new_m_gm_b_s': 'new_m', 'new_m_gm_w_out': 'new_m', 'new_v_ada_w': 'new_v', 'new_v_ada_b': 'new_v', 'new_v_norm_pre': 'new_v', 'new_v_norm_post': 'new_v', 'new_v_ffn_w_in': 'new_v', 'new_v_ffn_w_out': 'new_v', 'new_v_hg_w_in': 'new_v', 'new_v_hg_w_out': 'new_v', 'new_v_hg_out_norm': 'new_v', 'new_v_hg_lb': 'new_v', 'new_v_gm_w_in': 'new_v', 'new_v_gm_b_in': 'new_v', 'new_v_gm_ln_g': 'new_v', 'new_v_gm_ln_b': 'new_v', 'new_v_gm_w_s': 'new_v', 'new_v_gm_b_s': 'new_v', 'new_v_gm_w_out': 'new_v'}


def _forward(args):
    return _fwd_reference(*[args[k] for k in FWD_PARAMS])


def _output_shape():
    def fwd():
        inp = _fwd_setup_inputs(0)
        return _fwd_reference(*[inp[k] for k in FWD_PARAMS])
    out = _jax.eval_shape(fwd)
    return out.shape, out.dtype

N_MICROBATCH = 1
ADAM_LR = 0.001
ADAM_B1 = 0.9
ADAM_B2 = 0.999
ADAM_EPS = 1e-08
ADAM_WD = 0.01
ADAM_STEP = 10
PER_EXAMPLE_BATCH_AXIS = {'x': 0, 'c': 0, 'loss_target': 0}
SHARED_INPUTS = []
_WEIGHT_DTYPES = {'ada_w': _jnp.float32, 'ada_b': _jnp.float32, 'norm_pre': _jnp.float32, 'norm_post': _jnp.float32, 'ffn_w_in': _jnp.float32, 'ffn_w_out': _jnp.float32, 'hg_w_in': _jnp.float32, 'hg_w_out': _jnp.float32, 'hg_out_norm': _jnp.float32, 'hg_lb': _jnp.float32, 'gm_w_in': _jnp.float32, 'gm_b_in': _jnp.float32, 'gm_ln_g': _jnp.float32, 'gm_ln_b': _jnp.float32, 'gm_w_s': _jnp.float32, 'gm_b_s': _jnp.float32, 'gm_w_out': _jnp.float32}
MOMENT_SCALE = {'ada_w': 2.839525e+00, 'ada_b': 6.305272e+00, 'norm_pre': 1.861963e-01, 'norm_post': 7.950867e+00, 'ffn_w_in': 6.742642e-02, 'ffn_w_out': 1.219069e-01, 'hg_w_in': 2.086307e-01, 'hg_w_out': 3.615436e-01, 'hg_out_norm': 8.683968e-01, 'hg_lb': 8.643015e-03, 'gm_w_in': 1.421391e-01, 'gm_b_in': 4.763665e-01, 'gm_ln_g': 5.776889e-02, 'gm_ln_b': 5.693348e-02, 'gm_w_s': 9.287664e-02, 'gm_b_s': 1.587826e-01, 'gm_w_out': 7.342342e-01}


def _to_microbatches(a, axis):
    t = _jnp.moveaxis(a, axis, 0)
    t = t.reshape((N_MICROBATCH, t.shape[0] // N_MICROBATCH) + t.shape[1:])
    return _jnp.moveaxis(t, 1, axis + 1)


def setup_inputs(seed: int = 0) -> dict:
    inp = _fwd_setup_inputs(seed)
    key = _jax.random.fold_in(_jax.random.key(seed), 7919)
    shape, _ = _output_shape()
    out = dict(inp)
    out["loss_target"] = _jax.random.normal(_jax.random.fold_in(key, 0), shape, _jnp.float32)
    for i, name in enumerate(TWIN_WEIGHTS):
        w = inp[name].astype(_jnp.float32)
        if MOMENT_SCALE is None:
            s = _jnp.sqrt(_jnp.mean(_jnp.square(w)) + 1e-30)
        else:
            s = MOMENT_SCALE[name]
        km, kv = _jax.random.split(_jax.random.fold_in(key, i + 1))
        out[name] = w
        out["m_" + name] = s * _jax.random.normal(km, w.shape, _jnp.float32)
        out["v_" + name] = (s * s) * _jax.random.uniform(kv, w.shape, _jnp.float32, 0.5, 1.5)
    if N_MICROBATCH > 1:
        for name, axis in PER_EXAMPLE_BATCH_AXIS.items():
            out[name] = _to_microbatches(out[name], axis)
    return {'x': out['x'], 'c': out['c'], 'ada_w': out['ada_w'], 'ada_b': out['ada_b'], 'norm_pre': out['norm_pre'], 'norm_post': out['norm_post'], 'ffn_w_in': out['ffn_w_in'], 'ffn_w_out': out['ffn_w_out'], 'hg_w_in': out['hg_w_in'], 'hg_w_out': out['hg_w_out'], 'hg_out_norm': out['hg_out_norm'], 'hg_lb': out['hg_lb'], 'gm_w_in': out['gm_w_in'], 'gm_b_in': out['gm_b_in'], 'gm_ln_g': out['gm_ln_g'], 'gm_ln_b': out['gm_ln_b'], 'gm_w_s': out['gm_w_s'], 'gm_b_s': out['gm_b_s'], 'gm_w_out': out['gm_w_out'], 'loss_target': out['loss_target'], 'm_ada_w': out['m_ada_w'], 'm_ada_b': out['m_ada_b'], 'm_norm_pre': out['m_norm_pre'], 'm_norm_post': out['m_norm_post'], 'm_ffn_w_in': out['m_ffn_w_in'], 'm_ffn_w_out': out['m_ffn_w_out'], 'm_hg_w_in': out['m_hg_w_in'], 'm_hg_w_out': out['m_hg_w_out'], 'm_hg_out_norm': out['m_hg_out_norm'], 'm_hg_lb': out['m_hg_lb'], 'm_gm_w_in': out['m_gm_w_in'], 'm_gm_b_in': out['m_gm_b_in'], 'm_gm_ln_g': out['m_gm_ln_g'], 'm_gm_ln_b': out['m_gm_ln_b'], 'm_gm_w_s': out['m_gm_w_s'], 'm_gm_b_s': out['m_gm_b_s'], 'm_gm_w_out': out['m_gm_w_out'], 'v_ada_w': out['v_ada_w'], 'v_ada_b': out['v_ada_b'], 'v_norm_pre': out['v_norm_pre'], 'v_norm_post': out['v_norm_post'], 'v_ffn_w_in': out['v_ffn_w_in'], 'v_ffn_w_out': out['v_ffn_w_out'], 'v_hg_w_in': out['v_hg_w_in'], 'v_hg_w_out': out['v_hg_w_out'], 'v_hg_out_norm': out['v_hg_out_norm'], 'v_hg_lb': out['v_hg_lb'], 'v_gm_w_in': out['v_gm_w_in'], 'v_gm_b_in': out['v_gm_b_in'], 'v_gm_ln_g': out['v_gm_ln_g'], 'v_gm_ln_b': out['v_gm_ln_b'], 'v_gm_w_s': out['v_gm_w_s'], 'v_gm_b_s': out['v_gm_b_s'], 'v_gm_w_out': out['v_gm_w_out']}


def _loss(weights, diff, rest, loss_target):
    with _jax.named_scope("forward"):
        args = {**rest, TWIN_DIFF_INPUT: diff, **{k: w.astype(_WEIGHT_DTYPES[k]) for k, w in weights.items()}}
        y = _forward(args)
    with _jax.named_scope("loss_head"):
        err = _jnp.square(y.astype(_jnp.float32) - loss_target)
        return 0.5 * _jnp.sum(_jnp.mean(err, axis=-1)) if err.ndim else 0.5 * err


def _adamw(w, g, m, v):
    m = ADAM_B1 * m + (1.0 - ADAM_B1) * g
    v = ADAM_B2 * v + (1.0 - ADAM_B2) * _jnp.square(g)
    m_hat = m / (1.0 - ADAM_B1 ** ADAM_STEP)
    v_hat = v / (1.0 - ADAM_B2 ** ADAM_STEP)
    delta = -ADAM_LR * (m_hat / (_jnp.sqrt(v_hat) + ADAM_EPS) + ADAM_WD * w)
    return delta, m, v


def reference(x, c, ada_w, ada_b, norm_pre, norm_post, ffn_w_in, ffn_w_out, hg_w_in, hg_w_out, hg_out_norm, hg_lb, gm_w_in, gm_b_in, gm_ln_g, gm_ln_b, gm_w_s, gm_b_s, gm_w_out, loss_target, m_ada_w, m_ada_b, m_norm_pre, m_norm_post, m_ffn_w_in, m_ffn_w_out, m_hg_w_in, m_hg_w_out, m_hg_out_norm, m_hg_lb, m_gm_w_in, m_gm_b_in, m_gm_ln_g, m_gm_ln_b, m_gm_w_s, m_gm_b_s, m_gm_w_out, v_ada_w, v_ada_b, v_norm_pre, v_norm_post, v_ffn_w_in, v_ffn_w_out, v_hg_w_in, v_hg_w_out, v_hg_out_norm, v_hg_lb, v_gm_w_in, v_gm_b_in, v_gm_ln_g, v_gm_ln_b, v_gm_w_s, v_gm_b_s, v_gm_w_out):
    given = dict(x=x, c=c, ada_w=ada_w, ada_b=ada_b, norm_pre=norm_pre, norm_post=norm_post, ffn_w_in=ffn_w_in, ffn_w_out=ffn_w_out, hg_w_in=hg_w_in, hg_w_out=hg_w_out, hg_out_norm=hg_out_norm, hg_lb=hg_lb, gm_w_in=gm_w_in, gm_b_in=gm_b_in, gm_ln_g=gm_ln_g, gm_ln_b=gm_ln_b, gm_w_s=gm_w_s, gm_b_s=gm_b_s, gm_w_out=gm_w_out, loss_target=loss_target, m_ada_w=m_ada_w, m_ada_b=m_ada_b, m_norm_pre=m_norm_pre, m_norm_post=m_norm_post, m_ffn_w_in=m_ffn_w_in, m_ffn_w_out=m_ffn_w_out, m_hg_w_in=m_hg_w_in, m_hg_w_out=m_hg_w_out, m_hg_out_norm=m_hg_out_norm, m_hg_lb=m_hg_lb, m_gm_w_in=m_gm_w_in, m_gm_b_in=m_gm_b_in, m_gm_ln_g=m_gm_ln_g, m_gm_ln_b=m_gm_ln_b, m_gm_w_s=m_gm_w_s, m_gm_b_s=m_gm_b_s, m_gm_w_out=m_gm_w_out, v_ada_w=v_ada_w, v_ada_b=v_ada_b, v_norm_pre=v_norm_pre, v_norm_post=v_norm_post, v_ffn_w_in=v_ffn_w_in, v_ffn_w_out=v_ffn_w_out, v_hg_w_in=v_hg_w_in, v_hg_w_out=v_hg_w_out, v_hg_out_norm=v_hg_out_norm, v_hg_lb=v_hg_lb, v_gm_w_in=v_gm_w_in, v_gm_b_in=v_gm_b_in, v_gm_ln_g=v_gm_ln_g, v_gm_ln_b=v_gm_ln_b, v_gm_w_s=v_gm_w_s, v_gm_b_s=v_gm_b_s, v_gm_w_out=v_gm_w_out)
    weights = {n: given[n] for n in TWIN_WEIGHTS}
    shared = {n: given[n] for n in SHARED_INPUTS}
    per_example = {n: given[n] for n in ['x', 'c']}
    grad_fn = _jax.value_and_grad(_loss, argnums=(0, 1))

    def one_microbatch(ex, loss_target):
        ex = dict(ex)
        diff = ex.pop(TWIN_DIFF_INPUT)
        return grad_fn(weights, diff, {**shared, **ex}, loss_target)

    if N_MICROBATCH == 1:
        loss, (grad_w, grad_x) = one_microbatch(per_example, given["loss_target"])
    else:
        def body(carry, xs):
            loss_sum, grad_sum = carry
            l_k, (gw_k, gx_k) = one_microbatch(xs[0], xs[1])
            with _jax.named_scope("update"):
                return (loss_sum + l_k, _jax.tree.map(_jnp.add, grad_sum, gw_k)), gx_k

        init = (_jnp.zeros((), _jnp.float32), _jax.tree.map(_jnp.zeros_like, weights))
        (loss, grad_w), grad_x = _jax.lax.scan(body, init, (per_example, given["loss_target"]))
    with _jax.named_scope("update"):
        delta_w, new_m, new_v = {}, {}, {}
        for n in TWIN_WEIGHTS:
            delta_w[n], new_m[n], new_v[n] = _adamw(weights[n], grad_w[n], given["m_" + n], given["v_" + n])
    return (loss, grad_x, *[grad_w[n] for n in TWIN_WEIGHTS], *[delta_w[n] for n in TWIN_WEIGHTS],
            *[new_m[n] for n in TWIN_WEIGHTS], *[new_v[n] for n in TWIN_WEIGHTS])
```

```python
import functools
import math

import jax
import jax.numpy as jnp
from jax import lax
from jax.experimental import pallas as pl
from jax.experimental.pallas import tpu as pltpu

F32 = jnp.float32
BF16 = jnp.bfloat16
NDEV = 8
D_MODEL = 1024
EPS = 1e-6
HG_CHUNK = 64
HG_HEAD = 128
HG_HEADS = 8
GM_CHUNK = 128
GM_GDIM = 384
GM_GROUPS = 8
ADAM_LR = 0.001
ADAM_B1 = 0.9
ADAM_B2 = 0.999
ADAM_EPS = 1e-08
ADAM_WD = 0.01
ADAM_STEP = 10
VMEM_LIMIT = 56 * 2 ** 20

NN = (((1,), (0,)), ((), ()))
NT = (((1,), (1,)), ((), ()))
TN = (((0,), (0,)), ((), ()))
MESH = pl.DeviceIdType.MESH
ANY = pl.BlockSpec(memory_space=pl.ANY)


def _dot(a, b, dims=NN, precision=None):
    return lax.dot_general(a, b, dims, preferred_element_type=F32, precision=precision)


def _params(*sem):
    return pltpu.CompilerParams(dimension_semantics=sem, vmem_limit_bytes=VMEM_LIMIT)


def _sigmoid(x):
    return 1.0 / (1.0 + jnp.exp(-x))


def _gelu(x):
    c = math.sqrt(2.0 / math.pi)
    return 0.5 * x * (1.0 + jnp.tanh(c * (x + 0.044715 * x * x * x)))


def _gelu_grad(x):
    c = math.sqrt(2.0 / math.pi)
    t = jnp.tanh(c * (x + 0.044715 * x * x * x))
    return 0.5 * (1.0 + t) + 0.5 * x * (1.0 - t * t) * c * (1.0 + 3.0 * 0.044715 * x * x)


def _colsum(x):
    return jnp.sum(x, axis=0, keepdims=True)


def _rowmean(x):
    return jnp.mean(x, axis=-1, keepdims=True)


def _all_gather(shard, name):
    def body(x_ref, out_ref, send_sems, recv_sems, local_sem):
        x, y, c = lax.axis_index("x"), lax.axis_index("y"), lax.axis_index("c")
        me, sibling = (x, y, c), (x, y, 1 - c)
        chips = [(1 - x, y), (x, 1 - y), (1 - x, 1 - y)]

        def slot(p):
            return out_ref.at[4 * p[0] + 2 * p[1] + p[2]]

        def copy(k, block, to, src=None):
            return pltpu.make_async_remote_copy(
                src_ref=slot(block) if src is None else src, dst_ref=slot(block),
                send_sem=send_sems.at[k], recv_sem=recv_sems.at[k],
                device_id=to, device_id_type=MESH)

        mine = pltpu.make_async_copy(x_ref, slot(me), local_sem)
        mine.start()
        first = [copy(0, me, sibling, src=x_ref)]
        first += [copy(1 + j, me, (*chip, c), src=x_ref) for j, chip in enumerate(chips)]
        for cp in first:
            cp.start()
        passed = [copy(4 + j, (*chip, c), sibling) for j, chip in enumerate(chips)]
        for j, chip in enumerate(chips):
            copy(1 + j, (*chip, c), me).wait_recv()
            passed[j].start()
        copy(0, sibling, me).wait_recv()
        for j, chip in enumerate(chips):
            copy(4 + j, (*chip, 1 - c), me).wait_recv()
        for cp in first + passed:
            cp.wait_send()
        mine.wait()

    return pl.pallas_call(
        body, name=name,
        out_shape=jax.ShapeDtypeStruct((NDEV,) + shard.shape, shard.dtype),
        in_specs=[ANY], out_specs=ANY,
        scratch_shapes=[pltpu.SemaphoreType.DMA((7,)), pltpu.SemaphoreType.DMA((7,)),
                        pltpu.SemaphoreType.DMA(())],
    )(shard)


def _scatter_blocks(parts, name):
    n = len(parts)
    blk = parts[0].shape[1:]

    def body(*refs):
        g_refs = refs[:n]
        recv_ref = refs[n]
        send_sems, recv_sems, local_sems = refs[n + 1:]
        x, y, c = lax.axis_index("x"), lax.axis_index("y"), lax.axis_index("c")
        me = 4 * x + 2 * y + c
        peers = []
        for k in range(1, NDEV):
            px = 1 - x if k & 4 else x
            py = 1 - y if k & 2 else y
            pc = 1 - c if k & 1 else c
            peers.append((px, py, pc))

        def copy(a, k):
            p = peers[k]
            pid = 4 * p[0] + 2 * p[1] + p[2]
            return (pltpu.make_async_remote_copy(
                        src_ref=g_refs[a].at[pid], dst_ref=recv_ref.at[me, a],
                        send_sem=send_sems.at[a, k], recv_sem=recv_sems.at[a, k],
                        device_id=p, device_id_type=MESH),
                    pltpu.make_async_remote_copy(
                        src_ref=g_refs[a].at[pid], dst_ref=recv_ref.at[pid, a],
                        send_sem=send_sems.at[a, k], recv_sem=recv_sems.at[a, k],
                        device_id=p, device_id_type=MESH))

        local = [pltpu.make_async_copy(g_refs[a].at[me], recv_ref.at[me, a], local_sems.at[a])
                 for a in range(n)]
        for cp in local:
            cp.start()
        for a in range(n):
            for k in range(NDEV - 1):
                copy(a, k)[0].start()
        for a in range(n):
            for k in range(NDEV - 1):
                send, recv = copy(a, k)
                send.wait_send()
                recv.wait_recv()
        for cp in local:
            cp.wait()

    return pl.pallas_call(
        body, name=name,
        out_shape=jax.ShapeDtypeStruct((NDEV, n) + blk, parts[0].dtype),
        in_specs=[ANY] * n, out_specs=ANY,
        scratch_shapes=[pltpu.SemaphoreType.DMA((n, NDEV - 1)), pltpu.SemaphoreType.DMA((n, NDEV - 1)),
                        pltpu.SemaphoreType.DMA((n,))],
    )(*parts)


def _mm_blocks(a, w, name, *, transpose_w=False, bias=None, out_dtype=BF16, tm=1024):
    T, K = a.shape
    J = w.shape[0]
    n = w.shape[1] if transpose_w else w.shape[2]
    tm = min(tm, T)
    dims = NT if transpose_w else NN

    def body(*refs):
        if bias is None:
            a_ref, w_ref, o_ref = refs
            o_ref[...] = _dot(a_ref[...], w_ref[...], dims).astype(out_dtype)
        else:
            a_ref, w_ref, b_ref, o_ref = refs
            o_ref[...] = (_dot(a_ref[...], w_ref[...], dims) + b_ref[...]).astype(out_dtype)

    in_specs = [pl.BlockSpec((tm, K), lambda j, m: (m, 0)),
                pl.BlockSpec((None,) + w.shape[1:], lambda j, m: (j, 0, 0))]
    args = [a, w]
    if bias is not None:
        in_specs.append(pl.BlockSpec((None, 1, n), lambda j, m: (j, 0, 0)))
        args.append(bias)
    return pl.pallas_call(
        body, name=name, grid=(J, T // tm),
        out_shape=jax.ShapeDtypeStruct((J, T, n), out_dtype),
        in_specs=in_specs,
        out_specs=pl.BlockSpec((None, tm, n), lambda j, m: (j, m, 0)),
        compiler_params=_params("parallel", "parallel"),
    )(*args)


def _mm_sum(a, w, name, *, transpose_w=False, out_dtype=F32, tm=512):
    J, T, n = a.shape
    N = w.shape[1] if transpose_w else w.shape[2]
    tm = min(tm, T)
    dims = NT if transpose_w else NN

    def body(a_ref, w_ref, o_ref):
        acc = _dot(a_ref[0], w_ref[0], dims)
        for j in range(1, J):
            acc += _dot(a_ref[j], w_ref[j], dims)
        o_ref[...] = acc.astype(out_dtype)

    return pl.pallas_call(
        body, name=name, grid=(T // tm,),
        out_shape=jax.ShapeDtypeStruct((T, N), out_dtype),
        in_specs=[pl.BlockSpec((J, tm, n), lambda m: (0, m, 0)),
                  pl.BlockSpec(w.shape, lambda m: (0, 0, 0))],
        out_specs=pl.BlockSpec((tm, N), lambda m: (m, 0)),
        compiler_params=_params("parallel"),
    )(a, w)


def _mm_wgrad(xs, ys, name, *, tt=512):
    Jx, T, P = xs.shape
    Jy, _, Q = ys.shape
    J = max(Jx, Jy)
    tt = min(tt, T)
    nt = T // tt

    def body(x_ref, y_ref, o_ref, acc_ref):
        t = pl.program_id(1)

        @pl.when(t == 0)
        def _():
            acc_ref[...] = jnp.zeros_like(acc_ref)

        acc_ref[...] += _dot(x_ref[...], y_ref[...], TN)

        @pl.when(t == nt - 1)
        def _():
            o_ref[...] = acc_ref[...].astype(BF16)

    return pl.pallas_call(
        body, name=name, grid=(J, nt),
        out_shape=jax.ShapeDtypeStruct((J, P, Q), BF16),
        in_specs=[pl.BlockSpec((None, tt, P), (lambda j, t: (j, t, 0)) if Jx > 1 else (lambda j, t: (0, t, 0))),
                  pl.BlockSpec((None, tt, Q), (lambda j, t: (j, t, 0)) if Jy > 1 else (lambda j, t: (0, t, 0)))],
        out_specs=pl.BlockSpec((None, P, Q), lambda j, t: (j, 0, 0)),
        scratch_shapes=[pltpu.VMEM((P, Q), F32)],
        compiler_params=_params("parallel", "arbitrary"),
    )(xs, ys)


def _ffn_in(h, wg, name, *, tm=1024):
    T, K = h.shape
    n = wg.shape[-1]
    tm = min(tm, T)

    def body(h_ref, w_ref, ab_ref, g_ref):
        hh = h_ref[...]
        a = _dot(hh, w_ref[0])
        b = _dot(hh, w_ref[1])
        ab_ref[0] = a.astype(BF16)
        ab_ref[1] = b.astype(BF16)
        g_ref[...] = (a * _sigmoid(a) * b).astype(BF16)

    return pl.pallas_call(
        body, name=name, grid=(4, T // tm),
        out_shape=(jax.ShapeDtypeStruct((2, 4, T, n), BF16), jax.ShapeDtypeStruct((4, T, n), BF16)),
        in_specs=[pl.BlockSpec((tm, K), lambda j, m: (m, 0)),
                  pl.BlockSpec((2, None, K, n), lambda j, m: (0, j, 0, 0))],
        out_specs=(pl.BlockSpec((2, None, tm, n), lambda j, m: (0, j, m, 0)),
                   pl.BlockSpec((None, tm, n), lambda j, m: (j, m, 0))),
        compiler_params=_params("parallel", "parallel"),
    )(h, wg)


def _ffn_dgate(dy, wo4, ab, name, *, tm=1024):
    T, N = dy.shape
    n = wo4.shape[1]
    tm = min(tm, T)

    def body(dy_ref, w_ref, ab_ref, dab_ref):
        dg = _dot(dy_ref[...], w_ref[...], NT)
        a = ab_ref[0].astype(F32)
        b = ab_ref[1].astype(F32)
        s = _sigmoid(a)
        dab_ref[0] = (dg * b * (s * (1.0 + a * (1.0 - s)))).astype(BF16)
        dab_ref[1] = (dg * (a * s)).astype(BF16)

    return pl.pallas_call(
        body, name=name, grid=(4, T // tm),
        out_shape=jax.ShapeDtypeStruct((2, 4, T, n), BF16),
        in_specs=[pl.BlockSpec((tm, N), lambda j, m: (m, 0)),
                  pl.BlockSpec((None, n, N), lambda j, m: (j, 0, 0)),
                  pl.BlockSpec((2, None, tm, n), lambda j, m: (0, j, m, 0))],
        out_specs=pl.BlockSpec((2, None, tm, n), lambda j, m: (0, j, m, 0)),
        compiler_params=_params("parallel", "parallel"),
    )(dy, wo4, ab)


def _row_spec(tm, d):
    return pl.BlockSpec((tm, d), lambda m: (m, 0))


def _vec_spec(d):
    return pl.BlockSpec((1, d), lambda m: (0, 0))


def _prenorm_fwd(x, pre_g, scale, shift, name, *, tm=512):
    T, d = x.shape
    tm = min(tm, T)

    def body(x_ref, g_ref, sc_ref, sh_ref, h_ref):
        xv = x_ref[...]
        r = lax.rsqrt(_rowmean(xv * xv) + EPS)
        h_ref[...] = (xv * r * g_ref[...] * (1.0 + sc_ref[...]) + sh_ref[...]).astype(BF16)

    return pl.pallas_call(
        body, name=name, grid=(T // tm,),
        out_shape=jax.ShapeDtypeStruct((T, d), BF16),
        in_specs=[_row_spec(tm, d), _vec_spec(d), _vec_spec(d), _vec_spec(d)],
        out_specs=_row_spec(tm, d),
        compiler_params=_params("parallel"),
    )(x, pre_g, scale, shift)


def _postnorm_fwd(x, y, post_g, gate, res_w, name, *, tm=512):
    T, d = x.shape
    tm = min(tm, T)

    def body(x_ref, y_ref, g_ref, gate_ref, o_ref):
        yv = y_ref[...]
        r = lax.rsqrt(_rowmean(yv * yv) + EPS)
        o_ref[...] = x_ref[...] + res_w * gate_ref[...] * (yv * r * g_ref[...])

    return pl.pallas_call(
        body, name=name, grid=(T // tm,),
        out_shape=jax.ShapeDtypeStruct((T, d), F32),
        in_specs=[_row_spec(tm, d), _row_spec(tm, d), _vec_spec(d), _vec_spec(d)],
        out_specs=_row_spec(tm, d),
        compiler_params=_params("parallel"),
    )(x, y, post_g, gate)


def _postnorm_bwd(dxo, y, post_g, gate, res_w, name, *, tm=512):
    T, d = y.shape
    tm = min(tm, T)

    def body(dxo_ref, y_ref, g_ref, gate_ref, dy_ref, dgate_ref, dg_ref):
        @pl.when(pl.program_id(0) == 0)
        def _():
            dgate_ref[...] = jnp.zeros_like(dgate_ref)
            dg_ref[...] = jnp.zeros_like(dg_ref)

        yv = y_ref[...]
        r = lax.rsqrt(_rowmean(yv * yv) + EPS)
        yh = yv * r
        dr = dxo_ref[...] * res_w
        dgate_ref[...] += _colsum(dr * (yh * g_ref[...]))
        dp = dr * gate_ref[...]
        dg_ref[...] += _colsum(dp * yh)
        dyh = dp * g_ref[...]
        dy_ref[...] = (r * (dyh - yh * _rowmean(dyh * yh))).astype(BF16)

    return pl.pallas_call(
        body, name=name, grid=(T // tm,),
        out_shape=(jax.ShapeDtypeStruct((T, d), BF16), jax.ShapeDtypeStruct((1, d), F32),
                   jax.ShapeDtypeStruct((1, d), F32)),
        in_specs=[_row_spec(tm, d), _row_spec(tm, d), _vec_spec(d), _vec_spec(d)],
        out_specs=(_row_spec(tm, d), _vec_spec(d), _vec_spec(d)),
        compiler_params=_params("arbitrary"),
    )(dxo, y, post_g, gate)


def _prenorm_bwd(dxo, dh, x, pre_g, scale, name, *, tm=512):
    T, d = x.shape
    tm = min(tm, T)

    def body(dxo_ref, dh_ref, x_ref, g_ref, sc_ref, dx_ref, dsh_ref, dsc_ref, dg_ref):
        @pl.when(pl.program_id(0) == 0)
        def _():
            dsh_ref[...] = jnp.zeros_like(dsh_ref)
            dsc_ref[...] = jnp.zeros_like(dsc_ref)
            dg_ref[...] = jnp.zeros_like(dg_ref)

        xv = x_ref[...]
        dhv = dh_ref[...].astype(F32)
        r = lax.rsqrt(_rowmean(xv * xv) + EPS)
        xh = xv * r
        dsh_ref[...] += _colsum(dhv)
        dsc_ref[...] += _colsum(dhv * (xh * g_ref[...]))
        dn = dhv * (1.0 + sc_ref[...])
        dg_ref[...] += _colsum(dn * xh)
        dxh = dn * g_ref[...]
        dx_ref[...] = dxo_ref[...] + r * (dxh - xh * _rowmean(dxh * xh))

    return pl.pallas_call(
        body, name=name, grid=(T // tm,),
        out_shape=(jax.ShapeDtypeStruct((T, d), F32),) + (jax.ShapeDtypeStruct((1, d), F32),) * 3,
        in_specs=[_row_spec(tm, d), _row_spec(tm, d), _row_spec(tm, d), _vec_spec(d), _vec_spec(d)],
        out_specs=(_row_spec(tm, d),) + (_vec_spec(d),) * 3,
        compiler_params=_params("arbitrary"),
    )(dxo, dh, x, pre_g, scale)


def _loss_head(y, target, name, *, tm=512):
    T, d = y.shape
    tm = min(tm, T)

    def body(y_ref, t_ref, dy_ref, l_ref):
        @pl.when(pl.program_id(0) == 0)
        def _():
            l_ref[...] = jnp.zeros_like(l_ref)

        e = y_ref[...] - t_ref[...]
        dy_ref[...] = e * (1.0 / d)
        l_ref[...] += 0.5 * jnp.sum(_rowmean(e * e), axis=0, keepdims=True)

    return pl.pallas_call(
        body, name=name, grid=(T // tm,),
        out_shape=(jax.ShapeDtypeStruct((T, d), F32), jax.ShapeDtypeStruct((1, 128), F32)),
        in_specs=[_row_spec(tm, d), _row_spec(tm, d)],
        out_specs=(_row_spec(tm, d), pl.BlockSpec((1, 128), lambda m: (0, 0))),
        compiler_params=_params("arbitrary"),
    )(y, target)


def _ada_fwd(c_all, w, b, name):
    L, K, n = w.shape

    def body(c_ref, w_ref, b_ref, o_ref):
        cv = c_ref[...]
        cond = cv * _sigmoid(cv)
        for l in range(L):
            o_ref[l] = _dot(cond, w_ref[l], precision=lax.Precision.HIGHEST) + b_ref[l]

    return pl.pallas_call(
        body, name=name,
        out_shape=jax.ShapeDtypeStruct((L, NDEV, n), F32),
        compiler_params=pltpu.CompilerParams(vmem_limit_bytes=VMEM_LIMIT),
    )(c_all, w, b)


def _ada_bwd(c_all_t, gmod, name):
    L, _, n = gmod.shape
    K = c_all_t.shape[0]

    def body(c_ref, g_ref, o_ref):
        cv = c_ref[...]
        cond = cv * _sigmoid(cv)
        for l in range(L):
            o_ref[l] = _dot(cond, g_ref[l], precision=lax.Precision.HIGHEST)

    return pl.pallas_call(
        body, name=name,
        out_shape=jax.ShapeDtypeStruct((L, K, n), F32),
        compiler_params=pltpu.CompilerParams(vmem_limit_bytes=VMEM_LIMIT),
    )(c_all_t, gmod)


def _tri(n, upper=False, block=None):
    r = lax.broadcasted_iota(jnp.int32, (n, n), 0)
    c = lax.broadcasted_iota(jnp.int32, (n, n), 1)
    m = (c >= r) if upper else (c <= r)
    if block is not None:
        m = m & ((r // block) == (c // block))
    return m.astype(F32)


def _hgrn_gates(proj_ref, lb_ref, jh):
    lb = lb_ref[:, 512 * jh:512 * (jh + 1)]
    qp = proj_ref[jh]
    fx = proj_ref[2 + jh]
    sq = _sigmoid(qp)
    sig = _sigmoid(fx)
    f = lb + (1.0 - lb) * sig
    k = (1.0 - lb) * (1.0 - sig)
    return lb, qp, sq, sig, f, k


def _hgrn_fwd(proj, lb, out_norm, name, *, tb=128):
    T = proj.shape[1]
    tb = min(tb, T)
    nc = tb // HG_CHUNK
    lmat = _tri(tb, block=HG_CHUNK)

    def body(proj_ref, lb_ref, on_ref, l_ref, o_ref, og_ref, st_ref, s_scr, b_scr):
        @pl.when(pl.program_id(0) == 0)
        def _():
            s_scr[...] = jnp.zeros_like(s_scr)

        r_i = lax.broadcasted_iota(jnp.int32, (HG_CHUNK, HG_CHUNK), 0)
        c_i = lax.broadcasted_iota(jnp.int32, (HG_CHUNK, HG_CHUNK), 1)
        causal = c_i <= r_i
        onv = on_ref[...]
        for jh in range(2):
            lbv, qp, sq, sig, f, k = _hgrn_gates(proj_ref, lb_ref, jh)
            q = qp * sq
            b_scr[...] = _dot(l_ref[...], jnp.log(f), precision=lax.Precision.HIGHEST)
            v = proj_ref[4 + jh]
            gp = proj_ref[6 + jh]
            gs = gp * _sigmoid(gp)
            for hh in range(4):
                hd = 4 * jh + hh
                cs = slice(HG_HEAD * hh, HG_HEAD * (hh + 1))
                for ci in range(nc):
                    r0 = HG_CHUNK * ci
                    rs = slice(r0, r0 + HG_CHUNK)
                    bc = b_scr[rs, cs]
                    bm = b_scr[r0 + HG_CHUNK // 2 - 1:r0 + HG_CHUNK // 2, cs]
                    bl = b_scr[r0 + HG_CHUNK - 1:r0 + HG_CHUNK, cs]
                    qc, kc, vc = q[rs, cs], k[rs, cs], v[rs, cs].astype(BF16)
                    qe = (qc * jnp.exp(bc)).astype(BF16)
                    qt = (qc * jnp.exp(bc - bm)).astype(BF16)
                    kt = (kc * jnp.exp(bm - bc)).astype(BF16)
                    kd = (kc * jnp.exp(bl - bc)).astype(BF16)
                    st = s_scr[hd]
                    stb = st.astype(BF16)
                    st_ref[ci, hd] = stb
                    a = jnp.where(causal, _dot(qt, kt, NT), 0.0).astype(BF16)
                    o = _dot(qe, stb, NT) + _dot(a, vc)
                    s_scr[hd] = st * jnp.exp(bl) + _dot(vc, kd, TN)
                    o_ref[rs, HG_HEAD * hd:HG_HEAD * (hd + 1)] = o
                    r = lax.rsqrt(_rowmean(o * o) + EPS)
                    og_ref[rs, HG_HEAD * hd:HG_HEAD * (hd + 1)] = (o * r * onv * gs[rs, cs]).astype(BF16)

    return pl.pallas_call(
        body, name=name, grid=(T // tb,),
        out_shape=(jax.ShapeDtypeStruct((T, D_MODEL), F32), jax.ShapeDtypeStruct((T, D_MODEL), BF16),
                   jax.ShapeDtypeStruct((T // HG_CHUNK, HG_HEADS, HG_HEAD, HG_HEAD), BF16)),
        in_specs=[pl.BlockSpec((8, tb, 512), lambda i: (0, i, 0)),
                  pl.BlockSpec((1, D_MODEL), lambda i: (0, 0)),
                  pl.BlockSpec((1, HG_HEAD), lambda i: (0, 0)),
                  pl.BlockSpec((tb, tb), lambda i: (0, 0))],
        out_specs=(pl.BlockSpec((tb, D_MODEL), lambda i: (i, 0)),
                   pl.BlockSpec((tb, D_MODEL), lambda i: (i, 0)),
                   pl.BlockSpec((nc, HG_HEADS, HG_HEAD, HG_HEAD), lambda i: (i, 0, 0, 0))),
        scratch_shapes=[pltpu.VMEM((HG_HEADS, HG_HEAD, HG_HEAD), F32), pltpu.VMEM((tb, 512), F32)],
        compiler_params=_params("arbitrary"),
    )(proj, lb, out_norm, lmat)


def _hgrn_bwd(proj, o, dog, states, lb, out_norm, name, *, tb=128):
    T = proj.shape[1]
    tb = min(tb, T)
    nc = tb // HG_CHUNK
    nb = T // tb
    lmat = _tri(tb, block=HG_CHUNK)
    umat = _tri(tb, upper=True, block=HG_CHUNK)

    def body(proj_ref, o_ref, dog_ref, st_ref, lb_ref, on_ref, l_ref, u_ref,
             dproj_ref, dlb_ref, don_ref, ds_scr, b_scr, dq_scr, dk_scr, dv_scr, dg_scr, db_scr):
        @pl.when(pl.program_id(0) == 0)
        def _():
            ds_scr[...] = jnp.zeros_like(ds_scr)
            dlb_ref[...] = jnp.zeros_like(dlb_ref)
            don_ref[...] = jnp.zeros_like(don_ref)

        r_i = lax.broadcasted_iota(jnp.int32, (HG_CHUNK, HG_CHUNK), 0)
        c_i = lax.broadcasted_iota(jnp.int32, (HG_CHUNK, HG_CHUNK), 1)
        causal = c_i <= r_i
        causal_t = r_i <= c_i
        last_row = lax.broadcasted_iota(jnp.int32, (HG_CHUNK, HG_HEAD), 0) == HG_CHUNK - 1
        onv = on_ref[...]
        don_acc = jnp.zeros((1, HG_HEAD), F32)
        for jh in range(2):
            lbv, qp, sq, sig, f, k = _hgrn_gates(proj_ref, lb_ref, jh)
            q = qp * sq
            b_scr[...] = _dot(l_ref[...], jnp.log(f), precision=lax.Precision.HIGHEST)
            v = proj_ref[4 + jh]
            gp = proj_ref[6 + jh]
            sg = _sigmoid(gp)
            for hh in range(4):
                hd = 4 * jh + hh
                cs = slice(HG_HEAD * hh, HG_HEAD * (hh + 1))
                hs = slice(HG_HEAD * hd, HG_HEAD * (hd + 1))
                for ci in reversed(range(nc)):
                    r0 = HG_CHUNK * ci
                    rs = slice(r0, r0 + HG_CHUNK)
                    oc = o_ref[rs, hs]
                    r = lax.rsqrt(_rowmean(oc * oc) + EPS)
                    oh = oc * r
                    gc, sgc = gp[rs, cs], sg[rs, cs]
                    dogc = dog_ref[rs, hs].astype(F32)
                    don = dogc * (gc * sgc)
                    dg_scr[rs, cs] = dogc * (oh * onv) * (sgc * (1.0 + gc * (1.0 - sgc)))
                    don_acc += _colsum(don * oh)
                    donh = don * onv
                    do = (r * (donh - oh * _rowmean(donh * oh))).astype(BF16)
                    bc = b_scr[rs, cs]
                    bm = b_scr[r0 + HG_CHUNK // 2 - 1:r0 + HG_CHUNK // 2, cs]
                    bl = b_scr[r0 + HG_CHUNK - 1:r0 + HG_CHUNK, cs]
                    qc, kc, vc = q[rs, cs], k[rs, cs], v[rs, cs].astype(BF16)
                    e_b, e_q, e_k, e_d = jnp.exp(bc), jnp.exp(bc - bm), jnp.exp(bm - bc), jnp.exp(bl - bc)
                    qe = (qc * e_b).astype(BF16)
                    qt = (qc * e_q).astype(BF16)
                    kt = (kc * e_k).astype(BF16)
                    kd = (kc * e_d).astype(BF16)
                    stb = st_ref[ci, hd]
                    dst = ds_scr[hd]
                    dstb = dst.astype(BF16)
                    a_t = jnp.where(causal_t, _dot(kt, qt, NT), 0.0).astype(BF16)
                    da = jnp.where(causal, _dot(do, vc, NT), 0.0).astype(BF16)
                    da_t = jnp.where(causal_t, _dot(vc, do, NT), 0.0).astype(BF16)
                    dv_scr[rs, cs] = _dot(a_t, do) + _dot(kd, dstb, NT)
                    dqe, dqt = _dot(do, stb), _dot(da, kt)
                    dkt, dkd = _dot(da_t, qt), _dot(vc, dstb)
                    dq_scr[rs, cs] = dqe * e_b + dqt * e_q
                    dk_scr[rs, cs] = dkt * e_k + dkd * e_d
                    e_l = jnp.exp(bl)
                    s_end = stb.astype(F32) * e_l + _dot(vc, kd, TN)
                    dbc = (qe.astype(F32) * dqe + qt.astype(F32) * dqt
                           - kt.astype(F32) * dkt - kd.astype(F32) * dkd)
                    db_scr[rs, cs] = dbc + jnp.where(last_row, _colsum(dstb.astype(F32) * s_end), 0.0)
                    ds_scr[hd] = dst * e_l + _dot(do, qe, TN)
            dq = dq_scr[...]
            dk = dk_scr[...]
            cols = slice(512 * jh, 512 * (jh + 1))
            dlogf = _dot(u_ref[...], db_scr[...], precision=lax.Precision.HIGHEST)
            one_m_sig = 1.0 - sig
            dsig = (1.0 - lbv) * sig * one_m_sig
            dproj_ref[jh] = (dq * (sq * (1.0 + qp * (1.0 - sq)))).astype(BF16)
            dproj_ref[2 + jh] = (dlogf * dsig / f - dk * dsig).astype(BF16)
            dproj_ref[4 + jh] = dv_scr[...].astype(BF16)
            dproj_ref[6 + jh] = dg_scr[...].astype(BF16)
            dlb_ref[:, cols] += _colsum(dlogf * one_m_sig / f - dk * one_m_sig)
        don_ref[...] += don_acc

    rev = lambda i: nb - 1 - i
    return pl.pallas_call(
        body, name=name, grid=(nb,),
        out_shape=(jax.ShapeDtypeStruct((8, T, 512), BF16), jax.ShapeDtypeStruct((1, D_MODEL), F32),
                   jax.ShapeDtypeStruct((1, HG_HEAD), F32)),
        in_specs=[pl.BlockSpec((8, tb, 512), lambda i: (0, rev(i), 0)),
                  pl.BlockSpec((tb, D_MODEL), lambda i: (rev(i), 0)),
                  pl.BlockSpec((tb, D_MODEL), lambda i: (rev(i), 0)),
                  pl.BlockSpec((nc, HG_HEADS, HG_HEAD, HG_HEAD), lambda i: (rev(i), 0, 0, 0)),
                  pl.BlockSpec((1, D_MODEL), lambda i: (0, 0)),
                  pl.BlockSpec((1, HG_HEAD), lambda i: (0, 0)),
                  pl.BlockSpec((tb, tb), lambda i: (0, 0)),
                  pl.BlockSpec((tb, tb), lambda i: (0, 0))],
        out_specs=(pl.BlockSpec((8, tb, 512), lambda i: (0, rev(i), 0)),
                   pl.BlockSpec((1, D_MODEL), lambda i: (0, 0)),
                   pl.BlockSpec((1, HG_HEAD), lambda i: (0, 0))),
        scratch_shapes=[pltpu.VMEM((HG_HEADS, HG_HEAD, HG_HEAD), F32)] + [pltpu.VMEM((tb, 512), F32)] * 6,
        compiler_params=_params("arbitrary"),
    )(proj, o, dog, states, lb, out_norm, lmat, umat)


def _gm_norm(pre_ref, lg_ref, lbias_ref):
    vs = [_gelu(pre_ref[4 + j].astype(F32)) for j in range(4)]
    width = 4 * vs[0].shape[1]
    mu = sum(jnp.sum(v, axis=1, keepdims=True) for v in vs) / width
    ds = [v - mu for v in vs]
    var = sum(jnp.sum(d * d, axis=1, keepdims=True) for d in ds) / width
    rstd = lax.rsqrt(var + EPS)
    vhat = [d * rstd for d in ds]
    vn = [vhat[j] * lg_ref[j:j + 1, :] + lbias_ref[j:j + 1, :] for j in range(4)]
    return vhat, vn, rstd


def _gm_spatial_fwd(pre, ln_g, ln_b, ws, bsb, name, *, tb=256):
    T = pre.shape[1]
    tb = min(tb, T)
    nc = tb // GM_CHUNK

    def body(pre_ref, lg_ref, lbias_ref, ws_ref, bs_ref, o_ref):
        _, vn, _ = _gm_norm(pre_ref, lg_ref, lbias_ref)
        for j in range(4):
            u = _gelu(pre_ref[j].astype(F32))
            for e in range(2):
                g = 2 * j + e
                cs = slice(GM_GDIM * e, GM_GDIM * (e + 1))
                wg = ws_ref[g].astype(BF16)
                for ci in range(nc):
                    rs = slice(GM_CHUNK * ci, GM_CHUNK * (ci + 1))
                    vm = _dot(wg, vn[j][rs, cs].astype(BF16)) + bs_ref[g]
                    o_ref[j, rs, cs] = (u[rs, cs] * vm).astype(BF16)

    return pl.pallas_call(
        body, name=name, grid=(T // tb,),
        out_shape=jax.ShapeDtypeStruct((4, T, 768), BF16),
        in_specs=[pl.BlockSpec((8, tb, 768), lambda i: (0, i, 0)),
                  pl.BlockSpec((4, 768), lambda i: (0, 0)),
                  pl.BlockSpec((4, 768), lambda i: (0, 0)),
                  pl.BlockSpec((GM_GROUPS, GM_CHUNK, GM_CHUNK), lambda i: (0, 0, 0)),
                  pl.BlockSpec((GM_GROUPS, GM_CHUNK, GM_GDIM), lambda i: (0, 0, 0))],
        out_specs=pl.BlockSpec((4, tb, 768), lambda i: (0, i, 0)),
        compiler_params=_params("parallel"),
    )(pre, ln_g, ln_b, ws, bsb)


def _gm_spatial_bwd(pre, dm, ln_g, ln_b, ws, ws_t, bsb, name, *, tb=256):
    T = pre.shape[1]
    tb = min(tb, T)
    nc = tb // GM_CHUNK
    nb = T // tb

    def body(pre_ref, dm_ref, lg_ref, lbias_ref, ws_ref, wst_ref, bs_ref,
             dpre_ref, dws_ref, dbs_ref, dlg_ref, dlb_ref, dbin_ref, dbs_scr, dvn_scr, du_scr):
        i = pl.program_id(0)

        @pl.when(i == 0)
        def _():
            dws_ref[...] = jnp.zeros_like(dws_ref)
            dbs_scr[...] = jnp.zeros_like(dbs_scr)
            dlg_ref[...] = jnp.zeros_like(dlg_ref)
            dlb_ref[...] = jnp.zeros_like(dlb_ref)
            dbin_ref[...] = jnp.zeros_like(dbin_ref)

        vhat, vn, rstd = _gm_norm(pre_ref, lg_ref, lbias_ref)
        for j in range(4):
            u = _gelu(pre_ref[j].astype(F32))
            for e in range(2):
                g = 2 * j + e
                cs = slice(GM_GDIM * e, GM_GDIM * (e + 1))
                wg = ws_ref[g].astype(BF16)
                wgt = wst_ref[g].astype(BF16)
                for ci in range(nc):
                    rs = slice(GM_CHUNK * ci, GM_CHUNK * (ci + 1))
                    vnb = vn[j][rs, cs].astype(BF16)
                    vm = _dot(wg, vnb) + bs_ref[g]
                    dmg = dm_ref[j, rs, cs].astype(F32)
                    du_scr[j, rs, cs] = dmg * vm
                    dvm = dmg * u[rs, cs]
                    dvmb = dvm.astype(BF16)
                    dws_ref[g] += _dot(dvmb, vnb, NT)
                    dbs_scr[g] += dvm
                    dvn_scr[j, rs, cs] = _dot(wgt, dvmb)
        width = 4 * 768
        dvh = []
        for j in range(4):
            dvn = dvn_scr[j]
            dlg_ref[j:j + 1, :] += _colsum(dvn * vhat[j])
            dlb_ref[j:j + 1, :] += _colsum(dvn)
            dvh.append(dvn * lg_ref[j:j + 1, :])
        m1 = sum(jnp.sum(d, axis=1, keepdims=True) for d in dvh) / width
        m2 = sum(jnp.sum(dvh[j] * vhat[j], axis=1, keepdims=True) for j in range(4)) / width
        for j in range(4):
            dv = rstd * (dvh[j] - m1 - vhat[j] * m2)
            dpv = dv * _gelu_grad(pre_ref[4 + j].astype(F32))
            dpu = du_scr[j] * _gelu_grad(pre_ref[j].astype(F32))
            dpre_ref[4 + j] = dpv.astype(BF16)
            dpre_ref[j] = dpu.astype(BF16)
            dbin_ref[4 + j:5 + j, :] += _colsum(dpv)
            dbin_ref[j:j + 1, :] += _colsum(dpu)

        @pl.when(i == nb - 1)
        def _():
            r_i = lax.broadcasted_iota(jnp.int32, (GM_CHUNK, GM_CHUNK), 0)
            c_i = lax.broadcasted_iota(jnp.int32, (GM_CHUNK, GM_CHUNK), 1)
            for g in range(GM_GROUPS):
                dws_ref[g] = jnp.where(c_i <= r_i, dws_ref[g], 0.0)
                dbs_ref[g] = jnp.broadcast_to(jnp.sum(dbs_scr[g], axis=1, keepdims=True), (GM_CHUNK, GM_CHUNK))

    sq = pl.BlockSpec((GM_GROUPS, GM_CHUNK, GM_CHUNK), lambda i: (0, 0, 0))
    v4 = pl.BlockSpec((4, 768), lambda i: (0, 0))
    return pl.pallas_call(
        body, name=name, grid=(nb,),
        out_shape=(jax.ShapeDtypeStruct((8, T, 768), BF16),
                   jax.ShapeDtypeStruct((GM_GROUPS, GM_CHUNK, GM_CHUNK), F32),
                   jax.ShapeDtypeStruct((GM_GROUPS, GM_CHUNK, GM_CHUNK), F32),
                   jax.ShapeDtypeStruct((4, 768), F32), jax.ShapeDtypeStruct((4, 768), F32),
                   jax.ShapeDtypeStruct((8, 768), F32)),
        in_specs=[pl.BlockSpec((8, tb, 768), lambda i: (0, i, 0)),
                  pl.BlockSpec((4, tb, 768), lambda i: (0, i, 0)),
                  v4, v4, sq, sq,
                  pl.BlockSpec((GM_GROUPS, GM_CHUNK, GM_GDIM), lambda i: (0, 0, 0))],
        out_specs=(pl.BlockSpec((8, tb, 768), lambda i: (0, i, 0)), sq, sq, v4, v4,
                   pl.BlockSpec((8, 768), lambda i: (0, 0))),
        scratch_shapes=[pltpu.VMEM((GM_GROUPS, GM_CHUNK, GM_GDIM), F32),
                        pltpu.VMEM((4, tb, 768), F32), pltpu.VMEM((4, tb, 768), F32)],
        compiler_params=_params("arbitrary"),
    )(pre, dm, ln_g, ln_b, ws, ws_t, bsb)


def _adamw(slots, w, m, v, name, *, tr=256):
    S, R, C = slots.shape
    tr = next((t for t in (tr, tr // 2, tr // 4, tr // 8, tr // 16) if R % t == 0), R) if R > tr else R
    bc1 = 1.0 - ADAM_B1 ** ADAM_STEP
    bc2 = 1.0 - ADAM_B2 ** ADAM_STEP

    def body(s_ref, w_ref, m_ref, v_ref, g_ref, d_ref, nm_ref, nv_ref):
        g = s_ref[0].astype(F32)
        for s in range(1, S):
            g = g + s_ref[s].astype(F32)
        mn = ADAM_B1 * m_ref[...] + (1.0 - ADAM_B1) * g
        vn = ADAM_B2 * v_ref[...] + (1.0 - ADAM_B2) * (g * g)
        g_ref[...] = g
        nm_ref[...] = mn
        nv_ref[...] = vn
        d_ref[...] = -ADAM_LR * ((mn / bc1) / (jnp.sqrt(vn / bc2) + ADAM_EPS) + ADAM_WD * w_ref[...])

    spec = pl.BlockSpec((tr, C), lambda i: (i, 0))
    return pl.pallas_call(
        body, name=name, grid=(R // tr,),
        out_shape=(jax.ShapeDtypeStruct((R, C), F32),) * 4,
        in_specs=[pl.BlockSpec((S, tr, C), lambda i: (0, i, 0)), spec, spec, spec],
        out_specs=(spec,) * 4,
        compiler_params=_params("parallel"),
    )(slots, w, m, v)


def _update(slots, w, m, v, name):
    shp = w.shape
    C = shp[-1]
    R = math.prod(shp[:-1])
    outs = _adamw(slots.reshape(slots.shape[0], R, C), w.reshape(R, C), m.reshape(R, C), v.reshape(R, C), name)
    return tuple(o.reshape(shp) for o in outs)


def kernel(x, c, ada_w, ada_b, norm_pre, norm_post, ffn_w_in, ffn_w_out, hg_w_in, hg_w_out, hg_out_norm, hg_lb, gm_w_in, gm_b_in, gm_ln_g, gm_ln_b, gm_w_s, gm_b_s, gm_w_out, loss_target, m_ada_w, m_ada_b, m_norm_pre, m_norm_post, m_ffn_w_in, m_ffn_w_out, m_hg_w_in, m_hg_w_out, m_hg_out_norm, m_hg_lb, m_gm_w_in, m_gm_b_in, m_gm_ln_g, m_gm_ln_b, m_gm_w_s, m_gm_b_s, m_gm_w_out, v_ada_w, v_ada_b, v_norm_pre, v_norm_post, v_ffn_w_in, v_ffn_w_out, v_hg_w_in, v_hg_w_out, v_hg_out_norm, v_hg_lb, v_gm_w_in, v_gm_b_in, v_gm_ln_g, v_gm_ln_b, v_gm_w_s, v_gm_b_s, v_gm_w_out):
    me = 4 * lax.axis_index("x") + 2 * lax.axis_index("y") + lax.axis_index("c")
    T = x.shape[1]
    x0 = x.reshape(T, D_MODEL)
    target = loss_target.reshape(T, D_MODEL)
    n_ada = ada_w.shape[-1]

    pack = jnp.concatenate([
        c.reshape(8, 128), norm_pre.reshape(6, 128), norm_post.reshape(6, 128),
        gm_b_in.reshape(6, 128), gm_ln_g.reshape(3, 128), gm_ln_b.reshape(3, 128)], axis=0)
    packs = _all_gather(pack, "gather_small")
    c_all = packs[:, 0:8].reshape(NDEV, D_MODEL)
    npre = packs[:, 8:14].reshape(NDEV, 2, 3, 128).transpose(1, 2, 0, 3).reshape(2, 3, D_MODEL)
    npost = packs[:, 14:20].reshape(NDEV, 2, 3, 128).transpose(1, 2, 0, 3).reshape(2, 3, D_MODEL)
    b_in = packs[:, 20:26].reshape(NDEV, 1, 768)
    ln_g = packs[:, 26:29].reshape(4, 768)
    ln_b = packs[:, 29:32].reshape(4, 768)

    ada_b_mine = lax.dynamic_slice_in_dim(ada_b, me * n_ada, n_ada, axis=1).reshape(2, 1, n_ada)
    mod_cols = _ada_fwd(c_all, ada_w, ada_b_mine, "ada_fwd")
    mod_all = _all_gather(mod_cols, "gather_mod")
    mod = lax.dynamic_index_in_dim(mod_all, me, axis=2, keepdims=False)
    mod = mod.transpose(1, 0, 2).reshape(2, 9, 1, D_MODEL)

    w_fi = _all_gather(ffn_w_in.astype(BF16), "gather_ffn_in")
    w_fo = _all_gather(ffn_w_out.astype(BF16), "gather_ffn_out")
    w_hi = _all_gather(hg_w_in[0].astype(BF16), "gather_hg_in")
    w_ho = _all_gather(hg_w_out[0].astype(BF16), "gather_hg_out")
    w_mi = _all_gather(gm_w_in[0].astype(BF16), "gather_gm_in")
    w_mo = _all_gather(gm_w_out[0].astype(BF16), "gather_gm_out")
    n_ff = w_fi.shape[-1]
    w_ho = w_ho.reshape(1, D_MODEL, D_MODEL)
    w_mo = w_mo.reshape(4, 768, D_MODEL)

    sm = jax.nn.softmax(hg_lb, axis=0)
    lb0 = sm[0:1]
    on = hg_out_norm.reshape(1, HG_HEAD)
    tril = jnp.tril(jnp.ones((GM_CHUNK, GM_CHUNK), F32))
    ws = gm_w_s[0] * tril[None]
    ws_t = ws.transpose(0, 2, 1)
    bsb = jnp.broadcast_to(gm_b_s[0][:, :, None], (GM_GROUPS, GM_CHUNK, GM_GDIM))

    res_ws = (0.5, 1.0, 0.5)

    def vecs(i, s):
        return (npre[i, s].reshape(1, D_MODEL), npost[i, s].reshape(1, D_MODEL),
                mod[i, 3 * s], mod[i, 3 * s + 1], mod[i, 3 * s + 2])

    def ffn_weights(i, f):
        wg = w_fi[:, i, f].reshape(2, 4, D_MODEL, n_ff)
        wo = w_fo[:, i, f].reshape(4, n_ff, D_MODEL)
        return wg, wo

    saved = {}
    xs = x0
    for i in range(2):
        for s in range(3):
            tag = f"l{i}s{s}"
            pre_g, post_g, shift, scale, gate = vecs(i, s)
            h = _prenorm_fwd(xs, pre_g, scale, shift, "prenorm_" + tag)
            if s != 1:
                wg, wo = ffn_weights(i, s // 2)
                ab, g = _ffn_in(h, wg, "ffn_in_" + tag)
                y = _mm_sum(g, wo, "ffn_out_" + tag)
                saved[tag] = (xs, h, y, ab, g)
            elif i == 0:
                proj = _mm_blocks(h, w_hi, "hg_in", out_dtype=F32)
                o, og, states = _hgrn_fwd(proj, lb0, on, "hg_mix")
                y = _mm_sum(og.reshape(1, T, D_MODEL), w_ho, "hg_out")
                saved[tag] = (xs, h, y, proj, o, og, states)
            else:
                pre = _mm_blocks(h, w_mi, "gm_in", bias=b_in)
                sp = _gm_spatial_fwd(pre, ln_g, ln_b, ws, bsb, "gm_mix")
                y = _mm_sum(sp, w_mo, "gm_out")
                saved[tag] = (xs, h, y, pre, sp)
            xs = _postnorm_fwd(xs, y, post_g, gate, res_ws[s], "postnorm_" + tag)

    dx, loss_part = _loss_head(xs, target, "loss_head")
    loss = lax.psum(loss_part[0, 0], ("x", "y", "c"))

    g_fi = [[None, None], [None, None]]
    g_fo = [[None, None], [None, None]]
    d_npre = [[None] * 3, [None] * 3]
    d_npost = [[None] * 3, [None] * 3]
    d_mod = [[None] * 9, [None] * 9]
    for i in reversed(range(2)):
        for s in reversed(range(3)):
            tag = f"l{i}s{s}"
            pre_g, post_g, shift, scale, gate = vecs(i, s)
            xin, h, y = saved[tag][:3]
            dy, dgate, dpost = _postnorm_bwd(dx, y, post_g, gate, res_ws[s], "postnorm_bwd_" + tag)
            if s != 1:
                wg, wo = ffn_weights(i, s // 2)
                ab, g = saved[tag][3:]
                dab = _ffn_dgate(dy, wo, ab, "ffn_dgate_" + tag)
                g_fo[i][s // 2] = _mm_wgrad(g, dy.reshape(1, T, D_MODEL), "ffn_out_wgrad_" + tag)
                dab8 = dab.reshape(8, T, n_ff)
                dh = _mm_sum(dab8, wg.reshape(8, D_MODEL, n_ff), "ffn_in_dgrad_" + tag, transpose_w=True, out_dtype=BF16)
                g_fi[i][s // 2] = _mm_wgrad(h.reshape(1, T, D_MODEL), dab8, "ffn_in_wgrad_" + tag)
            elif i == 0:
                proj, o, og, states = saved[tag][3:]
                dog = _mm_blocks(dy, w_ho, "hg_out_dgrad", transpose_w=True)[0]
                g_ho = _mm_wgrad(og.reshape(1, T, D_MODEL), dy.reshape(1, T, D_MODEL), "hg_out_wgrad")
                dproj, d_lb0, d_on = _hgrn_bwd(proj, o, dog, states, lb0, on, "hg_mix_bwd")
                dh = _mm_sum(dproj, w_hi, "hg_in_dgrad", transpose_w=True, out_dtype=BF16)
                g_hi = _mm_wgrad(h.reshape(1, T, D_MODEL), dproj, "hg_in_wgrad")
            else:
                pre, sp = saved[tag][3:]
                dm = _mm_blocks(dy, w_mo, "gm_out_dgrad", transpose_w=True)
                g_mo = _mm_wgrad(sp, dy.reshape(1, T, D_MODEL), "gm_out_wgrad")
                dpre, d_ws, d_bs, d_lg, d_lbias, d_bin = _gm_spatial_bwd(pre, dm, ln_g, ln_b, ws, ws_t, bsb, "gm_mix_bwd")
                dh = _mm_sum(dpre, w_mi, "gm_in_dgrad", transpose_w=True, out_dtype=BF16)
                g_mi = _mm_wgrad(h.reshape(1, T, D_MODEL), dpre, "gm_in_wgrad")
            dx, dshift, dscale, dpre_g = _prenorm_bwd(dx, dh, xin, pre_g, scale, "prenorm_bwd_" + tag)
            d_npre[i][s], d_npost[i][s] = dpre_g, dpost
            d_mod[i][3 * s], d_mod[i][3 * s + 1], d_mod[i][3 * s + 2] = dshift, dscale, dgate
    grad_x = dx.reshape(x.shape)

    s_fi = _scatter_blocks([g_fi[0][0], g_fi[0][1], g_fi[1][0], g_fi[1][1]], "scatter_ffn_in")
    s_fo = _scatter_blocks([g.reshape(NDEV, n_ff // 2, D_MODEL) for g in
                            (g_fo[0][0], g_fo[0][1], g_fo[1][0], g_fo[1][1])], "scatter_ffn_out")
    s_hi = _scatter_blocks([g_hi], "scatter_hg_in")
    s_ho = _scatter_blocks([g_ho.reshape(NDEV, 128, D_MODEL)], "scatter_hg_out")
    s_mi = _scatter_blocks([g_mi], "scatter_gm_in")
    s_mo = _scatter_blocks([g_mo.reshape(NDEV, 384, D_MODEL)], "scatter_gm_out")

    gmod = jnp.stack([jnp.concatenate(d_mod[i], axis=0) for i in range(2)])
    d_sm = lb0 * d_lb0
    d_hg_lb = jnp.concatenate([d_sm, jnp.zeros((2, D_MODEL), F32)], axis=0) - sm * d_sm
    small = [gmod, jnp.stack([jnp.concatenate(r, axis=0) for r in d_npre]),
             jnp.stack([jnp.concatenate(r, axis=0) for r in d_npost]),
             d_on, d_hg_lb, d_bin, d_lg, d_lbias, d_ws, d_bs[:, :, 0]]
    sizes = [a.size for a in small]
    flat = jnp.concatenate([a.reshape(-1) for a in small])
    rows = -(-flat.size // (8 * 128)) * 8
    flat = jnp.pad(flat, (0, rows * 128 - flat.size)).reshape(rows, 128)
    flats = _all_gather(flat, "gather_small_grads").reshape(NDEV, rows * 128)
    parts, off = [], 0
    for a, n in zip(small, sizes):
        parts.append(flats[:, off:off + n].reshape((NDEV,) + a.shape))
        off += n
    p_mod, p_npre, p_npost, p_on, p_lb, p_bin, p_lg, p_lbias, p_ws, p_bs = parts

    def mine(p, width):
        return lax.dynamic_slice_in_dim(p, me * width, width, axis=p.ndim - 1)

    gmod_cols = mine(p_mod.reshape(NDEV, 2, 9 * D_MODEL), n_ada).transpose(1, 0, 2)
    g_ada_w = _ada_bwd(jnp.pad(c_all.T, ((0, 0), (0, 120))), jnp.pad(gmod_cols, ((0, 0), (0, 120), (0, 0))), "ada_bwd")

    out = {}
    out["ada_w"] = _update(g_ada_w[None], ada_w, m_ada_w, v_ada_w, "adamw_ada_w")
    out["ada_b"] = _update(p_mod.reshape(NDEV, 2, 9 * D_MODEL), ada_b, m_ada_b, v_ada_b, "adamw_ada_b")
    out["norm_pre"] = _update(mine(p_npre, 128), norm_pre, m_norm_pre, v_norm_pre, "adamw_norm_pre")
    out["norm_post"] = _update(mine(p_npost, 128), norm_post, m_norm_post, v_norm_post, "adamw_norm_post")
    out["ffn_w_in"] = _update(s_fi.reshape((NDEV,) + ffn_w_in.shape), ffn_w_in, m_ffn_w_in, v_ffn_w_in, "adamw_ffn_in")
    out["ffn_w_out"] = _update(s_fo.reshape((NDEV,) + ffn_w_out.shape), ffn_w_out, m_ffn_w_out, v_ffn_w_out, "adamw_ffn_out")
    out["hg_w_in"] = _update(s_hi, hg_w_in, m_hg_w_in, v_hg_w_in, "adamw_hg_in")
    out["hg_w_out"] = _update(s_ho, hg_w_out, m_hg_w_out, v_hg_w_out, "adamw_hg_out")
    out["hg_out_norm"] = _update(p_on, hg_out_norm, m_hg_out_norm, v_hg_out_norm, "adamw_hg_norm")
    out["hg_lb"] = _update(p_lb, hg_lb, m_hg_lb, v_hg_lb, "adamw_hg_lb")
    out["gm_w_in"] = _update(s_mi, gm_w_in, m_gm_w_in, v_gm_w_in, "adamw_gm_in")
    out["gm_b_in"] = _update(mine(p_bin.reshape(NDEV, 1, 8 * 768), 768), gm_b_in, m_gm_b_in, v_gm_b_in, "adamw_gm_b_in")
    out["gm_ln_g"] = _update(mine(p_lg.reshape(NDEV, 1, 4 * 768), 384), gm_ln_g, m_gm_ln_g, v_gm_ln_g, "adamw_gm_ln_g")
    out["gm_ln_b"] = _update(mine(p_lbias.reshape(NDEV, 1, 4 * 768), 384), gm_ln_b, m_gm_ln_b, v_gm_ln_b, "adamw_gm_ln_b")
    out["gm_w_s"] = _update(p_ws[:, None], gm_w_s, m_gm_w_s, v_gm_w_s, "adamw_gm_w_s")
    out["gm_b_s"] = _update(p_bs[:, None], gm_b_s, m_gm_b_s, v_gm_b_s, "adamw_gm_b_s")
    out["gm_w_out"] = _update(s_mo, gm_w_out, m_gm_w_out, v_gm_w_out, "adamw_gm_out")

    names = ["ada_w", "ada_b", "norm_pre", "norm_post", "ffn_w_in", "ffn_w_out", "hg_w_in", "hg_w_out",
             "hg_out_norm", "hg_lb", "gm_w_in", "gm_b_in", "gm_ln_g", "gm_ln_b", "gm_w_s", "gm_b_s", "gm_w_out"]
    return (loss, grad_x, *[out[n][0] for n in names], *[out[n][1] for n in names],
            *[out[n][2] for n in names], *[out[n][3] for n in names])
```

```python
import functools
import math

import jax
import jax.numpy as jnp
from jax import lax
from jax.experimental import pallas as pl
from jax.experimental.pallas import tpu as pltpu

F32 = jnp.float32
BF16 = jnp.bfloat16
NDEV = 8
D_MODEL = 1024
EPS = 1e-6
HG_CHUNK = 64
HG_HEAD = 128
HG_HEADS = 8
GM_CHUNK = 128
GM_GDIM = 384
GM_GROUPS = 8
ADAM_LR = 0.001
ADAM_B1 = 0.9
ADAM_B2 = 0.999
ADAM_EPS = 1e-08
ADAM_WD = 0.01
ADAM_STEP = 10
VMEM_LIMIT = 56 * 2 ** 20

NN = (((1,), (0,)), ((), ()))
NT = (((1,), (1,)), ((), ()))
TN = (((0,), (0,)), ((), ()))
MESH = pl.DeviceIdType.MESH
ANY = pl.BlockSpec(memory_space=pl.ANY)


def _dot(a, b, dims=NN, precision=None):
    return lax.dot_general(a, b, dims, preferred_element_type=F32, precision=precision)


def _params(*sem):
    return pltpu.CompilerParams(dimension_semantics=sem, vmem_limit_bytes=VMEM_LIMIT)


def _sigmoid(x):
    return 1.0 / (1.0 + jnp.exp(-x))


def _sigmoid_t(x):
    return 0.5 * jnp.tanh(0.5 * x) + 0.5


def _gelu(x):
    c = math.sqrt(2.0 / math.pi)
    return 0.5 * x * (1.0 + jnp.tanh(c * (x + 0.044715 * x * x * x)))


def _gelu_grad(x):
    c = math.sqrt(2.0 / math.pi)
    t = jnp.tanh(c * (x + 0.044715 * x * x * x))
    return 0.5 * (1.0 + t) + 0.5 * x * (1.0 - t * t) * c * (1.0 + 3.0 * 0.044715 * x * x)


def _colsum(x):
    return jnp.sum(x, axis=0, keepdims=True)


def _rowmean(x):
    return jnp.mean(x, axis=-1, keepdims=True)


def _all_gather(shard, name):
    def body(x_ref, out_ref, send_sems, recv_sems, local_sem):
        x, y, c = lax.axis_index("x"), lax.axis_index("y"), lax.axis_index("c")
        me, sibling = (x, y, c), (x, y, 1 - c)
        chips = [(1 - x, y), (x, 1 - y), (1 - x, 1 - y)]

        def slot(p):
            return out_ref.at[4 * p[0] + 2 * p[1] + p[2]]

        def copy(k, block, to, src=None):
            return pltpu.make_async_remote_copy(
                src_ref=slot(block) if src is None else src, dst_ref=slot(block),
                send_sem=send_sems.at[k], recv_sem=recv_sems.at[k],
                device_id=to, device_id_type=MESH)

        mine = pltpu.make_async_copy(x_ref, slot(me), local_sem)
        mine.start()
        first = [copy(0, me, sibling, src=x_ref)]
        first += [copy(1 + j, me, (*chip, c), src=x_ref) for j, chip in enumerate(chips)]
        for cp in first:
            cp.start()
        passed = [copy(4 + j, (*chip, c), sibling) for j, chip in enumerate(chips)]
        for j, chip in enumerate(chips):
            copy(1 + j, (*chip, c), me).wait_recv()
            passed[j].start()
        copy(0, sibling, me).wait_recv()
        for j, chip in enumerate(chips):
            copy(4 + j, (*chip, 1 - c), me).wait_recv()
        for cp in first + passed:
            cp.wait_send()
        mine.wait()

    return pl.pallas_call(
        body, name=name,
        out_shape=jax.ShapeDtypeStruct((NDEV,) + shard.shape, shard.dtype),
        in_specs=[ANY], out_specs=ANY,
        scratch_shapes=[pltpu.SemaphoreType.DMA((7,)), pltpu.SemaphoreType.DMA((7,)),
                        pltpu.SemaphoreType.DMA(())],
    )(shard)


def _scatter_blocks(parts, name):
    n = len(parts)
    blk = parts[0].shape[1:]

    def body(*refs):
        g_refs = refs[:n]
        recv_ref = refs[n]
        send_sems, recv_sems, local_sems = refs[n + 1:]
        x, y, c = lax.axis_index("x"), lax.axis_index("y"), lax.axis_index("c")
        me = 4 * x + 2 * y + c
        peers = []
        for k in range(1, NDEV):
            px = 1 - x if k & 4 else x
            py = 1 - y if k & 2 else y
            pc = 1 - c if k & 1 else c
            peers.append((px, py, pc))

        def copy(a, k):
            p = peers[k]
            pid = 4 * p[0] + 2 * p[1] + p[2]
            return (pltpu.make_async_remote_copy(
                        src_ref=g_refs[a].at[pid], dst_ref=recv_ref.at[me, a],
                        send_sem=send_sems.at[a, k], recv_sem=recv_sems.at[a, k],
                        device_id=p, device_id_type=MESH),
                    pltpu.make_async_remote_copy(
                        src_ref=g_refs[a].at[pid], dst_ref=recv_ref.at[pid, a],
                        send_sem=send_sems.at[a, k], recv_sem=recv_sems.at[a, k],
                        device_id=p, device_id_type=MESH))

        local = [pltpu.make_async_copy(g_refs[a].at[me], recv_ref.at[me, a], local_sems.at[a])
                 for a in range(n)]
        for cp in local:
            cp.start()
        for a in range(n):
            for k in range(NDEV - 1):
                copy(a, k)[0].start()
        for a in range(n):
            for k in range(NDEV - 1):
                send, recv = copy(a, k)
                send.wait_send()
                recv.wait_recv()
        for cp in local:
            cp.wait()

    return pl.pallas_call(
        body, name=name,
        out_shape=jax.ShapeDtypeStruct((NDEV, n) + blk, parts[0].dtype),
        in_specs=[ANY] * n, out_specs=ANY,
        scratch_shapes=[pltpu.SemaphoreType.DMA((n, NDEV - 1)), pltpu.SemaphoreType.DMA((n, NDEV - 1)),
                        pltpu.SemaphoreType.DMA((n,))],
    )(*parts)


def _mm_blocks(a, w, name, *, transpose_w=False, bias=None, out_dtype=BF16, tm=1024):
    T, K = a.shape
    J = w.shape[0]
    n = w.shape[1] if transpose_w else w.shape[2]
    tm = min(tm, T)
    dims = NT if transpose_w else NN

    def body(*refs):
        if bias is None:
            a_ref, w_ref, o_ref = refs
            o_ref[...] = _dot(a_ref[...], w_ref[...], dims).astype(out_dtype)
        else:
            a_ref, w_ref, b_ref, o_ref = refs
            o_ref[...] = (_dot(a_ref[...], w_ref[...], dims) + b_ref[...]).astype(out_dtype)

    in_specs = [pl.BlockSpec((tm, K), lambda j, m: (m, 0)),
                pl.BlockSpec((None,) + w.shape[1:], lambda j, m: (j, 0, 0))]
    args = [a, w]
    if bias is not None:
        in_specs.append(pl.BlockSpec((None, 1, n), lambda j, m: (j, 0, 0)))
        args.append(bias)
    return pl.pallas_call(
        body, name=name, grid=(J, T // tm),
        out_shape=jax.ShapeDtypeStruct((J, T, n), out_dtype),
        in_specs=in_specs,
        out_specs=pl.BlockSpec((None, tm, n), lambda j, m: (j, m, 0)),
        compiler_params=_params("parallel", "parallel"),
    )(*args)


def _sum_dots(a_ref, w_ref, dims):
    acc = _dot(a_ref[0], w_ref[0], dims)
    for j in range(1, a_ref.shape[0]):
        acc += _dot(a_ref[j], w_ref[j], dims)
    return acc


def _rms(v):
    return lax.rsqrt(_rowmean(v * v) + EPS)


def _zero_at_start(*refs):
    @pl.when(pl.program_id(0) == 0)
    def _():
        for r in refs:
            r[...] = jnp.zeros_like(r)


def _postnorm_bwd_math(dxo, yv, g, gate, res_w, dgate_ref, dpost_ref):
    r = _rms(yv)
    yh = yv * r
    dr = dxo * res_w
    dgate_ref[...] += _colsum(dr * (yh * g))
    dp = dr * gate
    dpost_ref[...] += _colsum(dp * yh)
    dyh = dp * g
    return (r * (dyh - yh * _rowmean(dyh * yh))).astype(BF16)


def _out_proj(a, w, x, post_g, gate, res_w, nxt, name, *, tm=512):
    J, T, n = a.shape
    d = w.shape[2]
    tm = min(tm, T)

    def body(a_ref, w_ref, x_ref, pg_ref, gate_ref, ng_ref, nsc_ref, nsh_ref, y_ref, xn_ref, h_ref):
        y = _sum_dots(a_ref, w_ref, NN)
        y_ref[...] = y
        xn = x_ref[...] + res_w * gate_ref[...] * (y * _rms(y) * pg_ref[...])
        xn_ref[...] = xn
        h_ref[...] = (xn * _rms(xn) * ng_ref[...] * (1.0 + nsc_ref[...]) + nsh_ref[...]).astype(BF16)

    return pl.pallas_call(
        body, name=name, grid=(T // tm,),
        out_shape=(jax.ShapeDtypeStruct((T, d), F32), jax.ShapeDtypeStruct((T, d), F32),
                   jax.ShapeDtypeStruct((T, d), BF16)),
        in_specs=[pl.BlockSpec((J, tm, n), lambda m: (0, m, 0)), pl.BlockSpec(w.shape, lambda m: (0, 0, 0)),
                  _row_spec(tm, d)] + [_vec_spec(d)] * 5,
        out_specs=(_row_spec(tm, d),) * 3,
        compiler_params=_params("parallel"),
    )(a, w, x, post_g, gate, *nxt)


def _out_proj_last(a, w, x, post_g, gate, res_w, target, name, *, tm=512):
    J, T, n = a.shape
    d = w.shape[2]
    tm = min(tm, T)

    def body(a_ref, w_ref, x_ref, pg_ref, gate_ref, t_ref, dx_ref, dy_ref, dgate_ref, dpost_ref, l_ref):
        _zero_at_start(dgate_ref, dpost_ref, l_ref)
        y = _sum_dots(a_ref, w_ref, NN)
        e = x_ref[...] + res_w * gate_ref[...] * (y * _rms(y) * pg_ref[...]) - t_ref[...]
        l_ref[...] += 0.5 * jnp.sum(_rowmean(e * e), axis=0, keepdims=True)
        dx = e * (1.0 / d)
        dx_ref[...] = dx
        dy_ref[...] = _postnorm_bwd_math(dx, y, pg_ref[...], gate_ref[...], res_w, dgate_ref, dpost_ref)

    return pl.pallas_call(
        body, name=name, grid=(T // tm,),
        out_shape=(jax.ShapeDtypeStruct((T, d), F32), jax.ShapeDtypeStruct((T, d), BF16),
                   jax.ShapeDtypeStruct((1, d), F32), jax.ShapeDtypeStruct((1, d), F32),
                   jax.ShapeDtypeStruct((1, 128), F32)),
        in_specs=[pl.BlockSpec((J, tm, n), lambda m: (0, m, 0)), pl.BlockSpec(w.shape, lambda m: (0, 0, 0)),
                  _row_spec(tm, d), _vec_spec(d), _vec_spec(d), _row_spec(tm, d)],
        out_specs=(_row_spec(tm, d), _row_spec(tm, d), _vec_spec(d), _vec_spec(d),
                   pl.BlockSpec((1, 128), lambda m: (0, 0))),
        compiler_params=_params("arbitrary"),
    )(a, w, x, post_g, gate, target)


def _in_grad(dz, w, dxo, x, pre_g, scale, prev, name, *, tm=256):
    J, T, n = dz.shape
    d = w.shape[1]
    tm = min(tm, T)
    has_prev = prev is not None
    res_w = prev[3] if has_prev else None

    def body(*refs):
        dz_ref, w_ref, dxo_ref, x_ref, g_ref, sc_ref = refs[:6]
        if has_prev:
            yp_ref, ppg_ref, pgate_ref, dx_ref, dsh_ref, dsc_ref, dg_ref, dyp_ref, dgate_ref, dpost_ref = refs[6:]
            _zero_at_start(dsh_ref, dsc_ref, dg_ref, dgate_ref, dpost_ref)
        else:
            dx_ref, dsh_ref, dsc_ref, dg_ref = refs[6:]
            _zero_at_start(dsh_ref, dsc_ref, dg_ref)
        dh = _sum_dots(dz_ref, w_ref, NT)
        xv = x_ref[...]
        r = _rms(xv)
        xh = xv * r
        dsh_ref[...] += _colsum(dh)
        dsc_ref[...] += _colsum(dh * (xh * g_ref[...]))
        dn = dh * (1.0 + sc_ref[...])
        dg_ref[...] += _colsum(dn * xh)
        dxh = dn * g_ref[...]
        dx = dxo_ref[...] + r * (dxh - xh * _rowmean(dxh * xh))
        dx_ref[...] = dx
        if has_prev:
            dyp_ref[...] = _postnorm_bwd_math(dx, yp_ref[...], ppg_ref[...], pgate_ref[...], res_w, dgate_ref, dpost_ref)

    vec = jax.ShapeDtypeStruct((1, d), F32)
    in_specs = [pl.BlockSpec((J, tm, n), lambda m: (0, m, 0)), pl.BlockSpec(w.shape, lambda m: (0, 0, 0)),
                _row_spec(tm, d), _row_spec(tm, d), _vec_spec(d), _vec_spec(d)]
    out_shape = [jax.ShapeDtypeStruct((T, d), F32), vec, vec, vec]
    out_specs = [_row_spec(tm, d), _vec_spec(d), _vec_spec(d), _vec_spec(d)]
    args = [dz, w, dxo, x, pre_g, scale]
    if has_prev:
        in_specs += [_row_spec(tm, d), _vec_spec(d), _vec_spec(d)]
        out_shape += [jax.ShapeDtypeStruct((T, d), BF16), vec, vec]
        out_specs += [_row_spec(tm, d), _vec_spec(d), _vec_spec(d)]
        args += list(prev[:3])
    return pl.pallas_call(
        body, name=name, grid=(T // tm,),
        out_shape=tuple(out_shape), in_specs=in_specs, out_specs=tuple(out_specs),
        compiler_params=_params("arbitrary"),
    )(*args)


def _mm_wgrad(xs, ys, name, *, tt=2048):
    Jx, T, P = xs.shape
    Jy, _, Q = ys.shape
    J = max(Jx, Jy)
    tt = min(tt, T)
    nt = T // tt

    def body(x_ref, y_ref, o_ref, acc_ref):
        t = pl.program_id(1)

        @pl.when(t == 0)
        def _():
            acc_ref[...] = jnp.zeros_like(acc_ref)

        acc_ref[...] += _dot(x_ref[...], y_ref[...], TN)

        @pl.when(t == nt - 1)
        def _():
            o_ref[...] = acc_ref[...].astype(BF16)

    return pl.pallas_call(
        body, name=name, grid=(J, nt),
        out_shape=jax.ShapeDtypeStruct((J, P, Q), BF16),
        in_specs=[pl.BlockSpec((None, tt, P), (lambda j, t: (j, t, 0)) if Jx > 1 else (lambda j, t: (0, t, 0))),
                  pl.BlockSpec((None, tt, Q), (lambda j, t: (j, t, 0)) if Jy > 1 else (lambda j, t: (0, t, 0)))],
        out_specs=pl.BlockSpec((None, P, Q), lambda j, t: (j, 0, 0)),
        scratch_shapes=[pltpu.VMEM((P, Q), F32)],
        compiler_params=_params("parallel", "arbitrary"),
    )(xs, ys)


def _ffn_in(h, wg, name, *, tm=1024):
    T, K = h.shape
    n = wg.shape[-1]
    tm = min(tm, T)

    def body(h_ref, w_ref, ab_ref, g_ref):
        hh = h_ref[...]
        a = _dot(hh, w_ref[0])
        b = _dot(hh, w_ref[1])
        ab_ref[0] = a.astype(BF16)
        ab_ref[1] = b.astype(BF16)
        g_ref[...] = (a * _sigmoid_t(a) * b).astype(BF16)

    return pl.pallas_call(
        body, name=name, grid=(4, T // tm),
        out_shape=(jax.ShapeDtypeStruct((2, 4, T, n), BF16), jax.ShapeDtypeStruct((4, T, n), BF16)),
        in_specs=[pl.BlockSpec((tm, K), lambda j, m: (m, 0)),
                  pl.BlockSpec((2, None, K, n), lambda j, m: (0, j, 0, 0))],
        out_specs=(pl.BlockSpec((2, None, tm, n), lambda j, m: (0, j, m, 0)),
                   pl.BlockSpec((None, tm, n), lambda j, m: (j, m, 0))),
        compiler_params=_params("parallel", "parallel"),
    )(h, wg)


def _ffn_dgate(dy, wo4, ab, name, *, tm=1024):
    T, N = dy.shape
    n = wo4.shape[1]
    tm = min(tm, T)

    def body(dy_ref, w_ref, ab_ref, dab_ref):
        dg = _dot(dy_ref[...], w_ref[...], NT)
        a = ab_ref[0].astype(F32)
        b = ab_ref[1].astype(F32)
        s = _sigmoid_t(a)
        dab_ref[0] =(dg * b * (s * (1.0 + a * (1.0 - s)))).astype(BF16)
        dab_ref[1] = (dg * (a * s)).astype(BF16)

    return pl.pallas_call(
        body, name=name, grid=(4, T // tm),
        out_shape=jax.ShapeDtypeStruct((2, 4, T, n), BF16),
        in_specs=[pl.BlockSpec((tm, N), lambda j, m: (m, 0)),
                  pl.BlockSpec((None, n, N), lambda j, m: (j, 0, 0)),
                  pl.BlockSpec((2, None, tm, n), lambda j, m: (0, j, m, 0))],
        out_specs=pl.BlockSpec((2, None, tm, n), lambda j, m: (0, j, m, 0)),
        compiler_params=_params("parallel", "parallel"),
    )(dy, wo4, ab)


def _row_spec(tm, d):
    return pl.BlockSpec((tm, d), lambda m: (m, 0))


def _vec_spec(d):
    return pl.BlockSpec((1, d), lambda m: (0, 0))


def _prenorm_fwd(x, pre_g, scale, shift, name, *, tm=512):
    T, d = x.shape
    tm = min(tm, T)

    def body(x_ref, g_ref, sc_ref, sh_ref, h_ref):
        xv = x_ref[...]
        r = lax.rsqrt(_rowmean(xv * xv) + EPS)
        h_ref[...] = (xv * r * g_ref[...] * (1.0 + sc_ref[...]) + sh_ref[...]).astype(BF16)

    return pl.pallas_call(
        body, name=name, grid=(T // tm,),
        out_shape=jax.ShapeDtypeStruct((T, d), BF16),
        in_specs=[_row_spec(tm, d), _vec_spec(d), _vec_spec(d), _vec_spec(d)],
        out_specs=_row_spec(tm, d),
        compiler_params=_params("parallel"),
    )(x, pre_g, scale, shift)


def _postnorm_fwd(x, y, post_g, gate, res_w, name, *, tm=512):
    T, d = x.shape
    tm = min(tm, T)

    def body(x_ref, y_ref, g_ref, gate_ref, o_ref):
        yv = y_ref[...]
        r = lax.rsqrt(_rowmean(yv * yv) + EPS)
        o_ref[...] = x_ref[...] + res_w * gate_ref[...] * (yv * r * g_ref[...])

    return pl.pallas_call(
        body, name=name, grid=(T // tm,),
        out_shape=jax.ShapeDtypeStruct((T, d), F32),
        in_specs=[_row_spec(tm, d), _row_spec(tm, d), _vec_spec(d), _vec_spec(d)],
        out_specs=_row_spec(tm, d),
        compiler_params=_params("parallel"),
    )(x, y, post_g, gate)


def _postnorm_bwd(dxo, y, post_g, gate, res_w, name, *, tm=512):
    T, d = y.shape
    tm = min(tm, T)

    def body(dxo_ref, y_ref, g_ref, gate_ref, dy_ref, dgate_ref, dg_ref):
        @pl.when(pl.program_id(0) == 0)
        def _():
            dgate_ref[...] = jnp.zeros_like(dgate_ref)
            dg_ref[...] = jnp.zeros_like(dg_ref)

        yv = y_ref[...]
        r = lax.rsqrt(_rowmean(yv * yv) + EPS)
        yh = yv * r
        dr = dxo_ref[...] * res_w
        dgate_ref[...] += _colsum(dr * (yh * g_ref[...]))
        dp = dr * gate_ref[...]
        dg_ref[...] += _colsum(dp * yh)
        dyh = dp * g_ref[...]
        dy_ref[...] = (r * (dyh - yh * _rowmean(dyh * yh))).astype(BF16)

    return pl.pallas_call(
        body, name=name, grid=(T // tm,),
        out_shape=(jax.ShapeDtypeStruct((T, d), BF16), jax.ShapeDtypeStruct((1, d), F32),
                   jax.ShapeDtypeStruct((1, d), F32)),
        in_specs=[_row_spec(tm, d), _row_spec(tm, d), _vec_spec(d), _vec_spec(d)],
        out_specs=(_row_spec(tm, d), _vec_spec(d), _vec_spec(d)),
        compiler_params=_params("arbitrary"),
    )(dxo, y, post_g, gate)


def _prenorm_bwd(dxo, dh, x, pre_g, scale, name, *, tm=512):
    T, d = x.shape
    tm = min(tm, T)

    def body(dxo_ref, dh_ref, x_ref, g_ref, sc_ref, dx_ref, dsh_ref, dsc_ref, dg_ref):
        @pl.when(pl.program_id(0) == 0)
        def _():
            dsh_ref[...] = jnp.zeros_like(dsh_ref)
            dsc_ref[...] = jnp.zeros_like(dsc_ref)
            dg_ref[...] = jnp.zeros_like(dg_ref)

        xv = x_ref[...]
        dhv = dh_ref[...].astype(F32)
        r = lax.rsqrt(_rowmean(xv * xv) + EPS)
        xh = xv * r
        dsh_ref[...] += _colsum(dhv)
        dsc_ref[...] += _colsum(dhv * (xh * g_ref[...]))
        dn = dhv * (1.0 + sc_ref[...])
        dg_ref[...] += _colsum(dn * xh)
        dxh = dn * g_ref[...]
        dx_ref[...] = dxo_ref[...] + r * (dxh - xh * _rowmean(dxh * xh))

    return pl.pallas_call(
        body, name=name, grid=(T // tm,),
        out_shape=(jax.ShapeDtypeStruct((T, d), F32),) + (jax.ShapeDtypeStruct((1, d), F32),) * 3,
        in_specs=[_row_spec(tm, d), _row_spec(tm, d), _row_spec(tm, d), _vec_spec(d), _vec_spec(d)],
        out_specs=(_row_spec(tm, d),) + (_vec_spec(d),) * 3,
        compiler_params=_params("arbitrary"),
    )(dxo, dh, x, pre_g, scale)


def _loss_head(y, target, name, *, tm=512):
    T, d = y.shape
    tm = min(tm, T)

    def body(y_ref, t_ref, dy_ref, l_ref):
        @pl.when(pl.program_id(0) == 0)
        def _():
            l_ref[...] = jnp.zeros_like(l_ref)

        e = y_ref[...] - t_ref[...]
        dy_ref[...] = e * (1.0 / d)
        l_ref[...] += 0.5 * jnp.sum(_rowmean(e * e), axis=0, keepdims=True)

    return pl.pallas_call(
        body, name=name, grid=(T // tm,),
        out_shape=(jax.ShapeDtypeStruct((T, d), F32), jax.ShapeDtypeStruct((1, 128), F32)),
        in_specs=[_row_spec(tm, d), _row_spec(tm, d)],
        out_specs=(_row_spec(tm, d), pl.BlockSpec((1, 128), lambda m: (0, 0))),
        compiler_params=_params("arbitrary"),
    )(y, target)


def _ada_fwd(c_all, w, b, name):
    L, K, n = w.shape

    def body(c_ref, w_ref, b_ref, o_ref):
        cv = c_ref[...]
        cond = cv * _sigmoid(cv)
        for l in range(L):
            o_ref[l] = _dot(cond, w_ref[l], precision=lax.Precision.HIGHEST) + b_ref[l]

    return pl.pallas_call(
        body, name=name,
        out_shape=jax.ShapeDtypeStruct((L, NDEV, n), F32),
        compiler_params=pltpu.CompilerParams(vmem_limit_bytes=VMEM_LIMIT),
    )(c_all, w, b)


def _ada_bwd(c_all_t, gmod, name):
    L, _, n = gmod.shape
    K = c_all_t.shape[0]

    def body(c_ref, g_ref, o_ref):
        cv = c_ref[...]
        cond = cv * _sigmoid(cv)
        for l in range(L):
            o_ref[l] = _dot(cond, g_ref[l], precision=lax.Precision.HIGHEST)

    return pl.pallas_call(
        body, name=name,
        out_shape=jax.ShapeDtypeStruct((L, K, n), F32),
        compiler_params=pltpu.CompilerParams(vmem_limit_bytes=VMEM_LIMIT),
    )(c_all_t, gmod)


def _tri(n, upper=False, block=None):
    r = lax.broadcasted_iota(jnp.int32, (n, n), 0)
    c = lax.broadcasted_iota(jnp.int32, (n, n), 1)
    m = (c >= r) if upper else (c <= r)
    if block is not None:
        m = m & ((r // block) == (c // block))
    return m.astype(F32)


def _hgrn_gates(proj_ref, lb_ref, jh):
    lb = lb_ref[:, 512 * jh:512 * (jh + 1)]
    qp = proj_ref[jh]
    fx = proj_ref[2 + jh]
    sq = _sigmoid(qp)
    sig = _sigmoid(fx)
    f = lb + (1.0 - lb) * sig
    k = (1.0 - lb) * (1.0 - sig)
    return lb, qp, sq, sig, f, k


def _hgrn_fwd(proj, lb, out_norm, name, *, tb=128):
    T = proj.shape[1]
    tb = min(tb, T)
    nc = tb // HG_CHUNK
    lmat = _tri(tb, block=HG_CHUNK)

    def body(proj_ref, lb_ref, on_ref, l_ref, o_ref, og_ref, st_ref, s_scr, b_scr):
        @pl.when(pl.program_id(0) == 0)
        def _():
            s_scr[...] = jnp.zeros_like(s_scr)

        r_i = lax.broadcasted_iota(jnp.int32, (HG_CHUNK, HG_CHUNK), 0)
        c_i = lax.broadcasted_iota(jnp.int32, (HG_CHUNK, HG_CHUNK), 1)
        causal = c_i <= r_i
        onv = on_ref[...]
        for jh in range(2):
            lbv, qp, sq, sig, f, k = _hgrn_gates(proj_ref, lb_ref, jh)
            q = qp * sq
            b_scr[...] = _dot(l_ref[...], jnp.log(f), precision=lax.Precision.HIGHEST)
            v = proj_ref[4 + jh]
            gp = proj_ref[6 + jh]
            gs = gp * _sigmoid(gp)
            for hh in range(4):
                hd = 4 * jh + hh
                cs = slice(HG_HEAD * hh, HG_HEAD * (hh + 1))
                for ci in range(nc):
                    r0 = HG_CHUNK * ci
                    rs = slice(r0, r0 + HG_CHUNK)
                    bc = b_scr[rs, cs]
                    bm = b_scr[r0 + HG_CHUNK // 2 - 1:r0 + HG_CHUNK // 2, cs]
                    bl = b_scr[r0 + HG_CHUNK - 1:r0 + HG_CHUNK, cs]
                    qc, kc, vc = q[rs, cs], k[rs, cs], v[rs, cs].astype(BF16)
                    qe = (qc * jnp.exp(bc)).astype(BF16)
                    qt = (qc * jnp.exp(bc - bm)).astype(BF16)
                    kt = (kc * jnp.exp(bm - bc)).astype(BF16)
                    kd = (kc * jnp.exp(bl - bc)).astype(BF16)
                    st = s_scr[hd]
                    stb = st.astype(BF16)
                    st_ref[ci, hd] = stb
                    a = jnp.where(causal, _dot(qt, kt, NT), 0.0).astype(BF16)
                    o = _dot(qe, stb, NT) + _dot(a, vc)
                    s_scr[hd] = st * jnp.exp(bl) + _dot(vc, kd, TN)
                    o_ref[rs, HG_HEAD * hd:HG_HEAD * (hd + 1)] = o
                    r = lax.rsqrt(_rowmean(o * o) + EPS)
                    og_ref[rs, HG_HEAD * hd:HG_HEAD * (hd + 1)] = (o * r * onv * gs[rs, cs]).astype(BF16)

    return pl.pallas_call(
        body, name=name, grid=(T // tb,),
        out_shape=(jax.ShapeDtypeStruct((T, D_MODEL), F32), jax.ShapeDtypeStruct((T, D_MODEL), BF16),
                   jax.ShapeDtypeStruct((T // HG_CHUNK, HG_HEADS, HG_HEAD, HG_HEAD), BF16)),
        in_specs=[pl.BlockSpec((8, tb, 512), lambda i: (0, i, 0)),
                  pl.BlockSpec((1, D_MODEL), lambda i: (0, 0)),
                  pl.BlockSpec((1, HG_HEAD), lambda i: (0, 0)),
                  pl.BlockSpec((tb, tb), lambda i: (0, 0))],
        out_specs=(pl.BlockSpec((tb, D_MODEL), lambda i: (i, 0)),
                   pl.BlockSpec((tb, D_MODEL), lambda i: (i, 0)),
                   pl.BlockSpec((nc, HG_HEADS, HG_HEAD, HG_HEAD), lambda i: (i, 0, 0, 0))),
        scratch_shapes=[pltpu.VMEM((HG_HEADS, HG_HEAD, HG_HEAD), F32), pltpu.VMEM((tb, 512), F32)],
        compiler_params=_params("arbitrary"),
    )(proj, lb, out_norm, lmat)


def _hgrn_bwd(proj, o, dog, states, lb, out_norm, name, *, tb=128):
    T = proj.shape[1]
    tb = min(tb, T)
    nc = tb // HG_CHUNK
    nb = T // tb
    lmat = _tri(tb, block=HG_CHUNK)
    umat = _tri(tb, upper=True, block=HG_CHUNK)

    def body(proj_ref, o_ref, dog_ref, st_ref, lb_ref, on_ref, l_ref, u_ref,
             dproj_ref, dlb_ref, don_ref, ds_scr, b_scr, dq_scr, dk_scr, dv_scr, dg_scr, db_scr):
        @pl.when(pl.program_id(0) == 0)
        def _():
            ds_scr[...] = jnp.zeros_like(ds_scr)
            dlb_ref[...] = jnp.zeros_like(dlb_ref)
            don_ref[...] = jnp.zeros_like(don_ref)

        r_i = lax.broadcasted_iota(jnp.int32, (HG_CHUNK, HG_CHUNK), 0)
        c_i = lax.broadcasted_iota(jnp.int32, (HG_CHUNK, HG_CHUNK), 1)
        causal = c_i <= r_i
        causal_t = r_i <= c_i
        last_row = lax.broadcasted_iota(jnp.int32, (HG_CHUNK, HG_HEAD), 0) == HG_CHUNK - 1
        onv = on_ref[...]
        don_acc = jnp.zeros((1, HG_HEAD), F32)
        for jh in range(2):
            lbv, qp, sq, sig, f, k = _hgrn_gates(proj_ref, lb_ref, jh)
            q = qp * sq
            b_scr[...] = _dot(l_ref[...], jnp.log(f), precision=lax.Precision.HIGHEST)
            v = proj_ref[4 + jh]
            gp = proj_ref[6 + jh]
            sg = _sigmoid(gp)
            for hh in range(4):
                hd = 4 * jh + hh
                cs = slice(HG_HEAD * hh, HG_HEAD * (hh + 1))
                hs = slice(HG_HEAD * hd, HG_HEAD * (hd + 1))
                for ci in reversed(range(nc)):
                    r0 = HG_CHUNK * ci
                    rs = slice(r0, r0 + HG_CHUNK)
                    oc = o_ref[rs, hs]
                    r = lax.rsqrt(_rowmean(oc * oc) + EPS)
                    oh = oc * r
                    gc, sgc = gp[rs, cs], sg[rs, cs]
                    dogc = dog_ref[rs, hs].astype(F32)
                    don = dogc * (gc * sgc)
                    dg_scr[rs, cs] = dogc * (oh * onv) * (sgc * (1.0 + gc * (1.0 - sgc)))
                    don_acc += _colsum(don * oh)
                    donh = don * onv
                    do = (r * (donh - oh * _rowmean(donh * oh))).astype(BF16)
                    bc = b_scr[rs, cs]
                    bm = b_scr[r0 + HG_CHUNK // 2 - 1:r0 + HG_CHUNK // 2, cs]
                    bl = b_scr[r0 + HG_CHUNK - 1:r0 + HG_CHUNK, cs]
                    qc, kc, vc = q[rs, cs], k[rs, cs], v[rs, cs].astype(BF16)
                    e_b, e_q, e_k, e_d = jnp.exp(bc), jnp.exp(bc - bm), jnp.exp(bm - bc), jnp.exp(bl - bc)
                    qe = (qc * e_b).astype(BF16)
                    qt = (qc * e_q).astype(BF16)
                    kt = (kc * e_k).astype(BF16)
                    kd = (kc * e_d).astype(BF16)
                    stb = st_ref[ci, hd]
                    dst = ds_scr[hd]
                    dstb = dst.astype(BF16)
                    a_t = jnp.where(causal_t, _dot(kt, qt, NT), 0.0).astype(BF16)
                    da = jnp.where(causal, _dot(do, vc, NT), 0.0).astype(BF16)
                    da_t = jnp.where(causal_t, _dot(vc, do, NT), 0.0).astype(BF16)
                    dv_scr[rs, cs] = _dot(a_t, do) + _dot(kd, dstb, NT)
                    dqe, dqt = _dot(do, stb), _dot(da, kt)
                    dkt, dkd = _dot(da_t, qt), _dot(vc, dstb)
                    dq_scr[rs, cs] = dqe * e_b + dqt * e_q
                    dk_scr[rs, cs] = dkt * e_k + dkd * e_d
                    e_l = jnp.exp(bl)
                    s_end = stb.astype(F32) * e_l + _dot(vc, kd, TN)
                    dbc = (qe.astype(F32) * dqe + qt.astype(F32) * dqt
                           - kt.astype(F32) * dkt - kd.astype(F32) * dkd)
                    db_scr[rs, cs] = dbc + jnp.where(last_row, _colsum(dstb.astype(F32) * s_end), 0.0)
                    ds_scr[hd] = dst * e_l + _dot(do, qe, TN)
            dq = dq_scr[...]
            dk = dk_scr[...]
            cols = slice(512 * jh, 512 * (jh + 1))
            dlogf = _dot(u_ref[...], db_scr[...], precision=lax.Precision.HIGHEST)
            one_m_sig = 1.0 - sig
            dsig = (1.0 - lbv) * sig * one_m_sig
            dproj_ref[jh] = (dq * (sq * (1.0 + qp * (1.0 - sq)))).astype(BF16)
            dproj_ref[2 + jh] = (dlogf * dsig / f - dk * dsig).astype(BF16)
            dproj_ref[4 + jh] = dv_scr[...].astype(BF16)
            dproj_ref[6 + jh] = dg_scr[...].astype(BF16)
            dlb_ref[:, cols] += _colsum(dlogf * one_m_sig / f - dk * one_m_sig)
        don_ref[...] += don_acc

    rev = lambda i: nb - 1 - i
    return pl.pallas_call(
        body, name=name, grid=(nb,),
        out_shape=(jax.ShapeDtypeStruct((8, T, 512), BF16), jax.ShapeDtypeStruct((1, D_MODEL), F32),
                   jax.ShapeDtypeStruct((1, HG_HEAD), F32)),
        in_specs=[pl.BlockSpec((8, tb, 512), lambda i: (0, rev(i), 0)),
                  pl.BlockSpec((tb, D_MODEL), lambda i: (rev(i), 0)),
                  pl.BlockSpec((tb, D_MODEL), lambda i: (rev(i), 0)),
                  pl.BlockSpec((nc, HG_HEADS, HG_HEAD, HG_HEAD), lambda i: (rev(i), 0, 0, 0)),
                  pl.BlockSpec((1, D_MODEL), lambda i: (0, 0)),
                  pl.BlockSpec((1, HG_HEAD), lambda i: (0, 0)),
                  pl.BlockSpec((tb, tb), lambda i: (0, 0)),
                  pl.BlockSpec((tb, tb), lambda i: (0, 0))],
        out_specs=(pl.BlockSpec((8, tb, 512), lambda i: (0, rev(i), 0)),
                   pl.BlockSpec((1, D_MODEL), lambda i: (0, 0)),
                   pl.BlockSpec((1, HG_HEAD), lambda i: (0, 0))),
        scratch_shapes=[pltpu.VMEM((HG_HEADS, HG_HEAD, HG_HEAD), F32)] + [pltpu.VMEM((tb, 512), F32)] * 6,
        compiler_params=_params("arbitrary"),
    )(proj, o, dog, states, lb, out_norm, lmat, umat)


def _gm_norm(pre_ref, lg_ref, lbias_ref):
    vs = [_gelu(pre_ref[4 + j].astype(F32)) for j in range(4)]
    width = 4 * vs[0].shape[1]
    mu = sum(jnp.sum(v, axis=1, keepdims=True) for v in vs) / width
    ds = [v - mu for v in vs]
    var = sum(jnp.sum(d * d, axis=1, keepdims=True) for d in ds) / width
    rstd = lax.rsqrt(var + EPS)
    vhat = [d * rstd for d in ds]
    vn = [vhat[j] * lg_ref[j:j + 1, :] + lbias_ref[j:j + 1, :] for j in range(4)]
    return vhat, vn, rstd


def _gm_spatial_fwd(pre, ln_g, ln_b, ws, bsb, name, *, tb=256):
    T = pre.shape[1]
    tb = min(tb, T)
    nc = tb // GM_CHUNK

    def body(pre_ref, lg_ref, lbias_ref, ws_ref, bs_ref, o_ref):
        _, vn, _ = _gm_norm(pre_ref, lg_ref, lbias_ref)
        for j in range(4):
            u = _gelu(pre_ref[j].astype(F32))
            for e in range(2):
                g = 2 * j + e
                cs = slice(GM_GDIM * e, GM_GDIM * (e + 1))
                wg = ws_ref[g].astype(BF16)
                for ci in range(nc):
                    rs = slice(GM_CHUNK * ci, GM_CHUNK * (ci + 1))
                    vm = _dot(wg, vn[j][rs, cs].astype(BF16)) + bs_ref[g]
                    o_ref[j, rs, cs] = (u[rs, cs] * vm).astype(BF16)

    return pl.pallas_call(
        body, name=name, grid=(T // tb,),
        out_shape=jax.ShapeDtypeStruct((4, T, 768), BF16),
        in_specs=[pl.BlockSpec((8, tb, 768), lambda i: (0, i, 0)),
                  pl.BlockSpec((4, 768), lambda i: (0, 0)),
                  pl.BlockSpec((4, 768), lambda i: (0, 0)),
                  pl.BlockSpec((GM_GROUPS, GM_CHUNK, GM_CHUNK), lambda i: (0, 0, 0)),
                  pl.BlockSpec((GM_GROUPS, GM_CHUNK, GM_GDIM), lambda i: (0, 0, 0))],
        out_specs=pl.BlockSpec((4, tb, 768), lambda i: (0, i, 0)),
        compiler_params=_params("parallel"),
    )(pre, ln_g, ln_b, ws, bsb)


def _gm_spatial_bwd(pre, dm, ln_g, ln_b, ws, ws_t, bsb, name, *, tb=256):
    T = pre.shape[1]
    tb = min(tb, T)
    nc = tb // GM_CHUNK
    nb = T // tb

    def body(pre_ref, dm_ref, lg_ref, lbias_ref, ws_ref, wst_ref, bs_ref,
             dpre_ref, dws_ref, dbs_ref, dlg_ref, dlb_ref, dbin_ref, dbs_scr, dvn_scr, du_scr):
        i = pl.program_id(0)

        @pl.when(i == 0)
        def _():
            dws_ref[...] = jnp.zeros_like(dws_ref)
            dbs_scr[...] = jnp.zeros_like(dbs_scr)
            dlg_ref[...] = jnp.zeros_like(dlg_ref)
            dlb_ref[...] = jnp.zeros_like(dlb_ref)
            dbin_ref[...] = jnp.zeros_like(dbin_ref)

        vhat, vn, rstd = _gm_norm(pre_ref, lg_ref, lbias_ref)
        for j in range(4):
            u = _gelu(pre_ref[j].astype(F32))
            for e in range(2):
                g = 2 * j + e
                cs = slice(GM_GDIM * e, GM_GDIM * (e + 1))
                wg = ws_ref[g].astype(BF16)
                wgt = wst_ref[g].astype(BF16)
                for ci in range(nc):
                    rs = slice(GM_CHUNK * ci, GM_CHUNK * (ci + 1))
                    vnb = vn[j][rs, cs].astype(BF16)
                    vm = _dot(wg, vnb) + bs_ref[g]
                    dmg = dm_ref[j, rs, cs].astype(F32)
                    du_scr[j, rs, cs] = dmg * vm
                    dvm = dmg * u[rs, cs]
                    dvmb = dvm.astype(BF16)
                    dws_ref[g] += _dot(dvmb, vnb, NT)
                    dbs_scr[g] += dvm
                    dvn_scr[j, rs, cs] = _dot(wgt, dvmb)
        width = 4 * 768
        dvh = []
        for j in range(4):
            dvn = dvn_scr[j]
            dlg_ref[j:j + 1, :] += _colsum(dvn * vhat[j])
            dlb_ref[j:j + 1, :] += _colsum(dvn)
            dvh.append(dvn * lg_ref[j:j + 1, :])
        m1 = sum(jnp.sum(d, axis=1, keepdims=True) for d in dvh) / width
        m2 = sum(jnp.sum(dvh[j] * vhat[j], axis=1, keepdims=True) for j in range(4)) / width
        for j in range(4):
            dv = rstd * (dvh[j] - m1 - vhat[j] * m2)
            dpv = dv * _gelu_grad(pre_ref[4 + j].astype(F32))
            dpu = du_scr[j] * _gelu_grad(pre_ref[j].astype(F32))
            dpre_ref[4 + j] = dpv.astype(BF16)
            dpre_ref[j] = dpu.astype(BF16)
            dbin_ref[4 + j:5 + j, :] += _colsum(dpv)
            dbin_ref[j:j + 1, :] += _colsum(dpu)

        @pl.when(i == nb - 1)
        def _():
            r_i = lax.broadcasted_iota(jnp.int32, (GM_CHUNK, GM_CHUNK), 0)
            c_i = lax.broadcasted_iota(jnp.int32, (GM_CHUNK, GM_CHUNK), 1)
            for g in range(GM_GROUPS):
                dws_ref[g] = jnp.where(c_i <= r_i, dws_ref[g], 0.0)
                dbs_ref[g] = jnp.broadcast_to(jnp.sum(dbs_scr[g], axis=1, keepdims=True), (GM_CHUNK, GM_CHUNK))

    sq = pl.BlockSpec((GM_GROUPS, GM_CHUNK, GM_CHUNK), lambda i: (0, 0, 0))
    v4 = pl.BlockSpec((4, 768), lambda i: (0, 0))
    return pl.pallas_call(
        body, name=name, grid=(nb,),
        out_shape=(jax.ShapeDtypeStruct((8, T, 768), BF16),
                   jax.ShapeDtypeStruct((GM_GROUPS, GM_CHUNK, GM_CHUNK), F32),
                   jax.ShapeDtypeStruct((GM_GROUPS, GM_CHUNK, GM_CHUNK), F32),
                   jax.ShapeDtypeStruct((4, 768), F32), jax.ShapeDtypeStruct((4, 768), F32),
                   jax.ShapeDtypeStruct((8, 768), F32)),
        in_specs=[pl.BlockSpec((8, tb, 768), lambda i: (0, i, 0)),
                  pl.BlockSpec((4, tb, 768), lambda i: (0, i, 0)),
                  v4, v4, sq, sq,
                  pl.BlockSpec((GM_GROUPS, GM_CHUNK, GM_GDIM), lambda i: (0, 0, 0))],
        out_specs=(pl.BlockSpec((8, tb, 768), lambda i: (0, i, 0)), sq, sq, v4, v4,
                   pl.BlockSpec((8, 768), lambda i: (0, 0))),
        scratch_shapes=[pltpu.VMEM((GM_GROUPS, GM_CHUNK, GM_GDIM), F32),
                        pltpu.VMEM((4, tb, 768), F32), pltpu.VMEM((4, tb, 768), F32)],
        compiler_params=_params("arbitrary"),
    )(pre, dm, ln_g, ln_b, ws, ws_t, bsb)


def _adamw(slots, w, m, v, name, *, tr=256):
    S, R, C = slots.shape
    tr = next((t for t in (tr, tr // 2, tr // 4, tr // 8, tr // 16) if R % t == 0), R) if R > tr else R
    bc1 = 1.0 - ADAM_B1 ** ADAM_STEP
    bc2 = 1.0 - ADAM_B2 ** ADAM_STEP

    def body(s_ref, w_ref, m_ref, v_ref, g_ref, d_ref, nm_ref, nv_ref):
        g = s_ref[0].astype(F32)
        for s in range(1, S):
            g = g + s_ref[s].astype(F32)
        mn = ADAM_B1 * m_ref[...] + (1.0 - ADAM_B1) * g
        vn = ADAM_B2 * v_ref[...] + (1.0 - ADAM_B2) * (g * g)
        g_ref[...] = g
        nm_ref[...] = mn
        nv_ref[...] = vn
        d_ref[...] = -ADAM_LR * ((mn / bc1) / (jnp.sqrt(vn / bc2) + ADAM_EPS) + ADAM_WD * w_ref[...])

    spec = pl.BlockSpec((tr, C), lambda i: (i, 0))
    return pl.pallas_call(
        body, name=name, grid=(R // tr,),
        out_shape=(jax.ShapeDtypeStruct((R, C), F32),) * 4,
        in_specs=[pl.BlockSpec((S, tr, C), lambda i: (0, i, 0)), spec, spec, spec],
        out_specs=(spec,) * 4,
        compiler_params=_params("parallel"),
    )(slots, w, m, v)


def _update(slots, w, m, v, name):
    shp = w.shape
    C = shp[-1]
    R = math.prod(shp[:-1])
    outs = _adamw(slots.reshape(slots.shape[0], R, C), w.reshape(R, C), m.reshape(R, C), v.reshape(R, C), name)
    return tuple(o.reshape(shp) for o in outs)


def kernel(x, c, ada_w, ada_b, norm_pre, norm_post, ffn_w_in, ffn_w_out, hg_w_in, hg_w_out, hg_out_norm, hg_lb, gm_w_in, gm_b_in, gm_ln_g, gm_ln_b, gm_w_s, gm_b_s, gm_w_out, loss_target, m_ada_w, m_ada_b, m_norm_pre, m_norm_post, m_ffn_w_in, m_ffn_w_out, m_hg_w_in, m_hg_w_out, m_hg_out_norm, m_hg_lb, m_gm_w_in, m_gm_b_in, m_gm_ln_g, m_gm_ln_b, m_gm_w_s, m_gm_b_s, m_gm_w_out, v_ada_w, v_ada_b, v_norm_pre, v_norm_post, v_ffn_w_in, v_ffn_w_out, v_hg_w_in, v_hg_w_out, v_hg_out_norm, v_hg_lb, v_gm_w_in, v_gm_b_in, v_gm_ln_g, v_gm_ln_b, v_gm_w_s, v_gm_b_s, v_gm_w_out):
    me = 4 * lax.axis_index("x") + 2 * lax.axis_index("y") + lax.axis_index("c")
    T = x.shape[1]
    x0 = x.reshape(T, D_MODEL)
    target = loss_target.reshape(T, D_MODEL)
    n_ada = ada_w.shape[-1]

    pack = jnp.concatenate([
        c.reshape(8, 128), norm_pre.reshape(6, 128), norm_post.reshape(6, 128),
        gm_b_in.reshape(6, 128), gm_ln_g.reshape(3, 128), gm_ln_b.reshape(3, 128)], axis=0)
    packs = _all_gather(pack, "gather_small")
    c_all = packs[:, 0:8].reshape(NDEV, D_MODEL)
    npre = packs[:, 8:14].reshape(NDEV, 2, 3, 128).transpose(1, 2, 0, 3).reshape(2, 3, D_MODEL)
    npost = packs[:, 14:20].reshape(NDEV, 2, 3, 128).transpose(1, 2, 0, 3).reshape(2, 3, D_MODEL)
    b_in = packs[:, 20:26].reshape(NDEV, 1, 768)
    ln_g = packs[:, 26:29].reshape(4, 768)
    ln_b = packs[:, 29:32].reshape(4, 768)

    ada_b_mine = lax.dynamic_slice_in_dim(ada_b, me * n_ada, n_ada, axis=1).reshape(2, 1, n_ada)
    mod_cols = _ada_fwd(c_all, ada_w, ada_b_mine, "ada_fwd")
    mod_all = _all_gather(mod_cols, "gather_mod")
    mod = lax.dynamic_index_in_dim(mod_all, me, axis=2, keepdims=False)
    mod = mod.transpose(1, 0, 2).reshape(2, 9, 1, D_MODEL)

    w_fi = _all_gather(ffn_w_in.astype(BF16), "gather_ffn_in")
    w_fo = _all_gather(ffn_w_out.astype(BF16), "gather_ffn_out")
    w_hi = _all_gather(hg_w_in[0].astype(BF16), "gather_hg_in")
    w_ho = _all_gather(hg_w_out[0].astype(BF16), "gather_hg_out")
    w_mi = _all_gather(gm_w_in[0].astype(BF16), "gather_gm_in")
    w_mo = _all_gather(gm_w_out[0].astype(BF16), "gather_gm_out")
    n_ff = w_fi.shape[-1]
    w_ho = w_ho.reshape(1, D_MODEL, D_MODEL)
    w_mo = w_mo.reshape(4, 768, D_MODEL)

    sm = jax.nn.softmax(hg_lb, axis=0)
    lb0 = sm[0:1]
    on = hg_out_norm.reshape(1, HG_HEAD)
    tril = jnp.tril(jnp.ones((GM_CHUNK, GM_CHUNK), F32))
    ws = gm_w_s[0] * tril[None]
    ws_t = ws.transpose(0, 2, 1)
    bsb = jnp.broadcast_to(gm_b_s[0][:, :, None], (GM_GROUPS, GM_CHUNK, GM_GDIM))

    res_ws = (0.5, 1.0, 0.5)

    def vecs(i, s):
        return (npre[i, s].reshape(1, D_MODEL), npost[i, s].reshape(1, D_MODEL),
                mod[i, 3 * s], mod[i, 3 * s + 1], mod[i, 3 * s + 2])

    def ffn_weights(i, f):
        wg = w_fi[:, i, f].reshape(2, 4, D_MODEL, n_ff)
        wo = w_fo[:, i, f].reshape(4, n_ff, D_MODEL)
        return wg, wo

    order = [(i, s) for i in range(2) for s in range(3)]
    saved = {}
    xs = x0
    pre_g, _, shift, scale, _ = vecs(0, 0)
    h = _prenorm_fwd(xs, pre_g, scale, shift, "prenorm_l0s0")
    for pos, (i, s) in enumerate(order):
        tag = f"l{i}s{s}"
        _, post_g, _, _, gate = vecs(i, s)
        if s != 1:
            wg, wo = ffn_weights(i, s // 2)
            ab, a = _ffn_in(h, wg, "ffn_in_" + tag)
            extra = (ab, a)
        elif i == 0:
            proj = _mm_blocks(h, w_hi, "hg_in", out_dtype=F32)
            o, og, states = _hgrn_fwd(proj, lb0, on, "hg_mix")
            a, wo = og.reshape(1, T, D_MODEL), w_ho
            extra = (proj, o, og, states)
        else:
            pre = _mm_blocks(h, w_mi, "gm_in", bias=b_in)
            a = _gm_spatial_fwd(pre, ln_g, ln_b, ws, bsb, "gm_mix")
            wo = w_mo
            extra = (pre, a)
        if pos + 1 < len(order):
            npre_g, _, nshift, nscale, _ = vecs(*order[pos + 1])
            y, x_next, h_next = _out_proj(a, wo, xs, post_g, gate, res_ws[s], (npre_g, nscale, nshift), "out_" + tag)
            saved[tag] = (xs, h, y) + extra
            xs, h = x_next, h_next
        else:
            dx, dy, dgate, dpost, loss_part = _out_proj_last(a, wo, xs, post_g, gate, res_ws[s], target, "out_" + tag)
            saved[tag] = (xs, h, None) + extra
    loss = lax.psum(loss_part[0, 0], ("x", "y", "c"))

    g_fi = [[None, None], [None, None]]
    g_fo = [[None, None], [None, None]]
    d_npre = [[None] * 3, [None] * 3]
    d_npost = [[None] * 3, [None] * 3]
    d_mod = [[None] * 9, [None] * 9]
    for pos in reversed(range(len(order))):
        i, s = order[pos]
        tag = f"l{i}s{s}"
        pre_g, _, _, scale, _ = vecs(i, s)
        xin, h = saved[tag][:2]
        dy1 = dy.reshape(1, T, D_MODEL)
        if s != 1:
            wg, wo = ffn_weights(i, s // 2)
            ab, g = saved[tag][3:]
            dz = _ffn_dgate(dy, wo, ab, "ffn_dgate_" + tag).reshape(8, T, n_ff)
            g_fo[i][s // 2] = _mm_wgrad(g, dy1, "ffn_out_wgrad_" + tag)
            w_in = wg.reshape(8, D_MODEL, n_ff)
            g_fi[i][s // 2] = _mm_wgrad(h.reshape(1, T, D_MODEL), dz, "ffn_in_wgrad_" + tag)
        elif i == 0:
            proj, o, og, states = saved[tag][3:]
            dog = _mm_blocks(dy, w_ho, "hg_out_dgrad", transpose_w=True)[0]
            g_ho = _mm_wgrad(og.reshape(1, T, D_MODEL), dy1, "hg_out_wgrad")
            dz, d_lb0, d_on = _hgrn_bwd(proj, o, dog, states, lb0, on, "hg_mix_bwd")
            w_in = w_hi
            g_hi = _mm_wgrad(h.reshape(1, T, D_MODEL), dz, "hg_in_wgrad")
        else:
            pre, sp = saved[tag][3:]
            dm = _mm_blocks(dy, w_mo, "gm_out_dgrad", transpose_w=True)
            g_mo = _mm_wgrad(sp, dy1, "gm_out_wgrad")
            dz, d_ws, d_bs, d_lg, d_lbias, d_bin = _gm_spatial_bwd(pre, dm, ln_g, ln_b, ws, ws_t, bsb, "gm_mix_bwd")
            w_in = w_mi
            g_mi = _mm_wgrad(h.reshape(1, T, D_MODEL), dz, "gm_in_wgrad")
        d_npost[i][s] = dpost
        d_mod[i][3 * s + 2] = dgate
        if pos > 0:
            pi, ps = order[pos - 1]
            _, ppost_g, _, _, pgate = vecs(pi, ps)
            prev = (saved[f"l{pi}s{ps}"][2], ppost_g, pgate, res_ws[ps])
            dx, dshift, dscale, dpre_g, dy, dgate, dpost = _in_grad(dz, w_in, dx, xin, pre_g, scale, prev, "in_grad_" + tag)
        else:
            dx, dshift, dscale, dpre_g = _in_grad(dz, w_in, dx, xin, pre_g, scale, None, "in_grad_" + tag)
        d_npre[i][s] = dpre_g
        d_mod[i][3 * s], d_mod[i][3 * s + 1] = dshift, dscale
    grad_x = dx.reshape(x.shape)

    s_fi = _scatter_blocks([g_fi[0][0], g_fi[0][1], g_fi[1][0], g_fi[1][1]], "scatter_ffn_in")
    s_fo = _scatter_blocks([g.reshape(NDEV, n_ff // 2, D_MODEL) for g in
                            (g_fo[0][0], g_fo[0][1], g_fo[1][0], g_fo[1][1])], "scatter_ffn_out")
    s_hi = _scatter_blocks([g_hi], "scatter_hg_in")
    s_ho = _scatter_blocks([g_ho.reshape(NDEV, 128, D_MODEL)], "scatter_hg_out")
    s_mi = _scatter_blocks([g_mi], "scatter_gm_in")
    s_mo = _scatter_blocks([g_mo.reshape(NDEV, 384, D_MODEL)], "scatter_gm_out")

    gmod = jnp.stack([jnp.concatenate(d_mod[i], axis=0) for i in range(2)])
    d_sm = lb0 * d_lb0
    d_hg_lb = jnp.concatenate([d_sm, jnp.zeros((2, D_MODEL), F32)], axis=0) - sm * d_sm
    small = [gmod, jnp.stack([jnp.concatenate(r, axis=0) for r in d_npre]),
             jnp.stack([jnp.concatenate(r, axis=0) for r in d_npost]),
             d_on, d_hg_lb, d_bin, d_lg, d_lbias, d_ws, d_bs[:, :, 0]]
    sizes = [a.size for a in small]
    flat = jnp.concatenate([a.reshape(-1) for a in small])
    rows = -(-flat.size // (8 * 128)) * 8
    flat = jnp.pad(flat, (0, rows * 128 - flat.size)).reshape(rows, 128)
    flats = _all_gather(flat, "gather_small_grads").reshape(NDEV, rows * 128)
    parts, off = [], 0
    for a, n in zip(small, sizes):
        parts.append(flats[:, off:off + n].reshape((NDEV,) + a.shape))
        off += n
    p_mod, p_npre, p_npost, p_on, p_lb, p_bin, p_lg, p_lbias, p_ws, p_bs = parts

    def mine(p, width):
        return lax.dynamic_slice_in_dim(p, me * width, width, axis=p.ndim - 1)

    gmod_cols = mine(p_mod.reshape(NDEV, 2, 9 * D_MODEL), n_ada).transpose(1, 0, 2)
    g_ada_w = _ada_bwd(jnp.pad(c_all.T, ((0, 0), (0, 120))), jnp.pad(gmod_cols, ((0, 0), (0, 120), (0, 0))), "ada_bwd")

    out = {}
    out["ada_w"] = _update(g_ada_w[None], ada_w, m_ada_w, v_ada_w, "adamw_ada_w")
    out["ada_b"] = _update(p_mod.reshape(NDEV, 2, 9 * D_MODEL), ada_b, m_ada_b, v_ada_b, "adamw_ada_b")
    out["norm_pre"] = _update(mine(p_npre, 128), norm_pre, m_norm_pre, v_norm_pre, "adamw_norm_pre")
    out["norm_post"] = _update(mine(p_npost, 128), norm_post, m_norm_post, v_norm_post, "adamw_norm_post")
    out["ffn_w_in"] = _update(s_fi.reshape((NDEV,) + ffn_w_in.shape), ffn_w_in, m_ffn_w_in, v_ffn_w_in, "adamw_ffn_in")
    out["ffn_w_out"] = _update(s_fo.reshape((NDEV,) + ffn_w_out.shape), ffn_w_out, m_ffn_w_out, v_ffn_w_out, "adamw_ffn_out")
    out["hg_w_in"] = _update(s_hi, hg_w_in, m_hg_w_in, v_hg_w_in, "adamw_hg_in")
    out["hg_w_out"] = _update(s_ho, hg_w_out, m_hg_w_out, v_hg_w_out, "adamw_hg_out")
    out["hg_out_norm"] = _update(p_on, hg_out_norm, m_hg_out_norm, v_hg_out_norm, "adamw_hg_norm")
    out["hg_lb"] = _update(p_lb, hg_lb, m_hg_lb, v_hg_lb, "adamw_hg_lb")
    out["gm_w_in"] = _update(s_mi, gm_w_in, m_gm_w_in, v_gm_w_in, "adamw_gm_in")
    out["gm_b_in"] = _update(mine(p_bin.reshape(NDEV, 1, 8 * 768), 768), gm_b_in, m_gm_b_in, v_gm_b_in, "adamw_gm_b_in")
    out["gm_ln_g"] = _update(mine(p_lg.reshape(NDEV, 1, 4 * 768), 384), gm_ln_g, m_gm_ln_g, v_gm_ln_g, "adamw_gm_ln_g")
    out["gm_ln_b"] = _update(mine(p_lbias.reshape(NDEV, 1, 4 * 768), 384), gm_ln_b, m_gm_ln_b, v_gm_ln_b, "adamw_gm_ln_b")
    out["gm_w_s"] = _update(p_ws[:, None], gm_w_s, m_gm_w_s, v_gm_w_s, "adamw_gm_w_s")
    out["gm_b_s"] = _update(p_bs[:, None], gm_b_s, m_gm_b_s, v_gm_b_s, "adamw_gm_b_s")
    out["gm_w_out"] = _update(s_mo, gm_w_out, m_gm_w_out, v_gm_w_out, "adamw_gm_out")

    names = ["ada_w", "ada_b", "norm_pre", "norm_post", "ffn_w_in", "ffn_w_out", "hg_w_in", "hg_w_out",
             "hg_out_norm", "hg_lb", "gm_w_in", "gm_b_in", "gm_ln_g", "gm_ln_b", "gm_w_s", "gm_b_s", "gm_w_out"]
    return (loss, grad_x, *[out[n][0] for n in names], *[out[n][1] for n in names],
            *[out[n][2] for n in names], *[out[n][3] for n in names])
```

```python
import functools
import math

import jax
import jax.numpy as jnp
from jax import lax
from jax.experimental import pallas as pl
from jax.experimental.pallas import tpu as pltpu

F32 = jnp.float32
BF16 = jnp.bfloat16
NDEV = 8
D_MODEL = 1024
EPS = 1e-6
HG_CHUNK = 64
HG_HEAD = 128
HG_HEADS = 8
GM_CHUNK = 128
GM_GDIM = 384
GM_GROUPS = 8
ADAM_LR = 0.001
ADAM_B1 = 0.9
ADAM_B2 = 0.999
ADAM_EPS = 1e-08
ADAM_WD = 0.01
ADAM_STEP = 10
VMEM_LIMIT = 56 * 2 ** 20

NN = (((1,), (0,)), ((), ()))
NT = (((1,), (1,)), ((), ()))
TN = (((0,), (0,)), ((), ()))
MESH = pl.DeviceIdType.MESH
ANY = pl.BlockSpec(memory_space=pl.ANY)


def _dot(a, b, dims=NN, precision=None):
    return lax.dot_general(a, b, dims, preferred_element_type=F32, precision=precision)


def _params(*sem):
    return pltpu.CompilerParams(dimension_semantics=sem, vmem_limit_bytes=VMEM_LIMIT)


def _sigmoid(x):
    return 1.0 / (1.0 + jnp.exp(-x))


def _sigmoid_t(x):
    return 0.5 * jnp.tanh(0.5 * x) + 0.5


def _gelu(x):
    c = math.sqrt(2.0 / math.pi)
    return 0.5 * x * (1.0 + jnp.tanh(c * (x + 0.044715 * x * x * x)))


def _gelu_grad(x):
    c = math.sqrt(2.0 / math.pi)
    t = jnp.tanh(c * (x + 0.044715 * x * x * x))
    return 0.5 * (1.0 + t) + 0.5 * x * (1.0 - t * t) * c * (1.0 + 3.0 * 0.044715 * x * x)


def _colsum(x):
    return jnp.sum(x, axis=0, keepdims=True)


def _rowmean(x):
    return jnp.mean(x, axis=-1, keepdims=True)


def _all_gather(shard, name):
    def body(x_ref, out_ref, send_sems, recv_sems, local_sem):
        x, y, c = lax.axis_index("x"), lax.axis_index("y"), lax.axis_index("c")
        me, sibling = (x, y, c), (x, y, 1 - c)
        chips = [(1 - x, y), (x, 1 - y), (1 - x, 1 - y)]

        def slot(p):
            return out_ref.at[4 * p[0] + 2 * p[1] + p[2]]

        def copy(k, block, to, src=None):
            return pltpu.make_async_remote_copy(
                src_ref=slot(block) if src is None else src, dst_ref=slot(block),
                send_sem=send_sems.at[k], recv_sem=recv_sems.at[k],
                device_id=to, device_id_type=MESH)

        mine = pltpu.make_async_copy(x_ref, slot(me), local_sem)
        mine.start()
        first = [copy(0, me, sibling, src=x_ref)]
        first += [copy(1 + j, me, (*chip, c), src=x_ref) for j, chip in enumerate(chips)]
        for cp in first:
            cp.start()
        passed = [copy(4 + j, (*chip, c), sibling) for j, chip in enumerate(chips)]
        for j, chip in enumerate(chips):
            copy(1 + j, (*chip, c), me).wait_recv()
            passed[j].start()
        copy(0, sibling, me).wait_recv()
        for j, chip in enumerate(chips):
            copy(4 + j, (*chip, 1 - c), me).wait_recv()
        for cp in first + passed:
            cp.wait_send()
        mine.wait()

    return pl.pallas_call(
        body, name=name,
        out_shape=jax.ShapeDtypeStruct((NDEV,) + shard.shape, shard.dtype),
        in_specs=[ANY], out_specs=ANY,
        scratch_shapes=[pltpu.SemaphoreType.DMA((7,)), pltpu.SemaphoreType.DMA((7,)),
                        pltpu.SemaphoreType.DMA(())],
    )(shard)


def _scatter_blocks(parts, name):
    n = len(parts)
    blk = parts[0].shape[1:]

    def body(*refs):
        g_refs = refs[:n]
        recv_ref = refs[n]
        send_sems, recv_sems, local_sems = refs[n + 1:]
        x, y, c = lax.axis_index("x"), lax.axis_index("y"), lax.axis_index("c")
        me = 4 * x + 2 * y + c
        peers = []
        for k in range(1, NDEV):
            px = 1 - x if k & 4 else x
            py = 1 - y if k & 2 else y
            pc = 1 - c if k & 1 else c
            peers.append((px, py, pc))

        def copy(a, k):
            p = peers[k]
            pid = 4 * p[0] + 2 * p[1] + p[2]
            return (pltpu.make_async_remote_copy(
                        src_ref=g_refs[a].at[pid], dst_ref=recv_ref.at[me, a],
                        send_sem=send_sems.at[a, k], recv_sem=recv_sems.at[a, k],
                        device_id=p, device_id_type=MESH),
                    pltpu.make_async_remote_copy(
                        src_ref=g_refs[a].at[pid], dst_ref=recv_ref.at[pid, a],
                        send_sem=send_sems.at[a, k], recv_sem=recv_sems.at[a, k],
                        device_id=p, device_id_type=MESH))

        local = [pltpu.make_async_copy(g_refs[a].at[me], recv_ref.at[me, a], local_sems.at[a])
                 for a in range(n)]
        for cp in local:
            cp.start()
        for a in range(n):
            for k in range(NDEV - 1):
                copy(a, k)[0].start()
        for a in range(n):
            for k in range(NDEV - 1):
                send, recv = copy(a, k)
                send.wait_send()
                recv.wait_recv()
        for cp in local:
            cp.wait()

    return pl.pallas_call(
        body, name=name,
        out_shape=jax.ShapeDtypeStruct((NDEV, n) + blk, parts[0].dtype),
        in_specs=[ANY] * n, out_specs=ANY,
        scratch_shapes=[pltpu.SemaphoreType.DMA((n, NDEV - 1)), pltpu.SemaphoreType.DMA((n, NDEV - 1)),
                        pltpu.SemaphoreType.DMA((n,))],
    )(*parts)


def _mm_blocks(a, w, name, *, transpose_w=False, bias=None, out_dtype=BF16, tm=1024):
    T, K = a.shape
    J = w.shape[0]
    n = w.shape[1] if transpose_w else w.shape[2]
    tm = min(tm, T)
    dims = NT if transpose_w else NN

    def body(*refs):
        if bias is None:
            a_ref, w_ref, o_ref = refs
            o_ref[...] = _dot(a_ref[...], w_ref[...], dims).astype(out_dtype)
        else:
            a_ref, w_ref, b_ref, o_ref = refs
            o_ref[...] = (_dot(a_ref[...], w_ref[...], dims) + b_ref[...]).astype(out_dtype)

    in_specs = [pl.BlockSpec((tm, K), lambda j, m: (m, 0)),
                pl.BlockSpec((None,) + w.shape[1:], lambda j, m: (j, 0, 0))]
    args = [a, w]
    if bias is not None:
        in_specs.append(pl.BlockSpec((None, 1, n), lambda j, m: (j, 0, 0)))
        args.append(bias)
    return pl.pallas_call(
        body, name=name, grid=(J, T // tm),
        out_shape=jax.ShapeDtypeStruct((J, T, n), out_dtype),
        in_specs=in_specs,
        out_specs=pl.BlockSpec((None, tm, n), lambda j, m: (j, m, 0)),
        compiler_params=_params("parallel", "parallel"),
    )(*args)


SUB_ROWS = 256


def _whole_spec(w):
    return pl.BlockSpec(w.shape, lambda m: (0, 0, 0), pipeline_mode=pl.Buffered(1))


def _row_blocks(tm):
    sub = min(SUB_ROWS, tm)
    return [slice(r, r + sub) for r in range(0, tm, sub)]


def _sum_dots(a_ref, w_ref, dims, rows=slice(None)):
    acc = _dot(a_ref[0, rows], w_ref[0], dims)
    for j in range(1, a_ref.shape[0]):
        acc += _dot(a_ref[j, rows], w_ref[j], dims)
    return acc


def _rms(v):
    return lax.rsqrt(_rowmean(v * v) + EPS)


def _zero_at_start(*refs):
    @pl.when(pl.program_id(0) == 0)
    def _():
        for r in refs:
            r[...] = jnp.zeros_like(r)


def _postnorm_bwd_math(dxo, yv, g, gate, res_w, dgate_ref, dpost_ref):
    r = _rms(yv)
    yh = yv * r
    dr = dxo * res_w
    dgate_ref[...] += _colsum(dr * (yh * g))
    dp = dr * gate
    dpost_ref[...] += _colsum(dp * yh)
    dyh = dp * g
    return (r * (dyh - yh * _rowmean(dyh * yh))).astype(BF16)


def _out_proj(a, w, x, post_g, gate, res_w, nxt, name, *, tm=512):
    J, T, n = a.shape
    d = w.shape[2]
    tm = min(tm, T)

    def body(a_ref, w_ref, x_ref, pg_ref, gate_ref, ng_ref, nsc_ref, nsh_ref, y_ref, xn_ref, h_ref):
        for rows in _row_blocks(tm):
            y = _sum_dots(a_ref, w_ref, NN, rows)
            y_ref[rows] = y
            xn = x_ref[rows] + res_w * gate_ref[...] * (y * _rms(y) * pg_ref[...])
            xn_ref[rows] = xn
            h_ref[rows] = (xn * _rms(xn) * ng_ref[...] * (1.0 + nsc_ref[...]) + nsh_ref[...]).astype(BF16)

    return pl.pallas_call(
        body, name=name, grid=(T // tm,),
        out_shape=(jax.ShapeDtypeStruct((T, d), F32), jax.ShapeDtypeStruct((T, d), F32),
                   jax.ShapeDtypeStruct((T, d), BF16)),
        in_specs=[pl.BlockSpec((J, tm, n), lambda m: (0, m, 0)), _whole_spec(w),
                  _row_spec(tm, d)] + [_vec_spec(d)] * 5,
        out_specs=(_row_spec(tm, d),) * 3,
        compiler_params=_params("parallel"),
    )(a, w, x, post_g, gate, *nxt)


def _out_proj_last(a, w, x, post_g, gate, res_w, target, name, *, tm=512):
    J, T, n = a.shape
    d = w.shape[2]
    tm = min(tm, T)

    def body(a_ref, w_ref, x_ref, pg_ref, gate_ref, t_ref, dx_ref, dy_ref, dgate_ref, dpost_ref, l_ref):
        _zero_at_start(dgate_ref, dpost_ref, l_ref)
        for rows in _row_blocks(tm):
            y = _sum_dots(a_ref, w_ref, NN, rows)
            e = x_ref[rows] + res_w * gate_ref[...] * (y * _rms(y) * pg_ref[...]) - t_ref[rows]
            l_ref[...] += 0.5 * jnp.sum(_rowmean(e * e), axis=0, keepdims=True)
            dx = e * (1.0 / d)
            dx_ref[rows] = dx
            dy_ref[rows] = _postnorm_bwd_math(dx, y, pg_ref[...], gate_ref[...], res_w, dgate_ref, dpost_ref)

    return pl.pallas_call(
        body, name=name, grid=(T // tm,),
        out_shape=(jax.ShapeDtypeStruct((T, d), F32), jax.ShapeDtypeStruct((T, d), BF16),
                   jax.ShapeDtypeStruct((1, d), F32), jax.ShapeDtypeStruct((1, d), F32),
                   jax.ShapeDtypeStruct((1, 128), F32)),
        in_specs=[pl.BlockSpec((J, tm, n), lambda m: (0, m, 0)), _whole_spec(w),
                  _row_spec(tm, d), _vec_spec(d), _vec_spec(d), _row_spec(tm, d)],
        out_specs=(_row_spec(tm, d), _row_spec(tm, d), _vec_spec(d), _vec_spec(d),
                   pl.BlockSpec((1, 128), lambda m: (0, 0))),
        compiler_params=_params("arbitrary"),
    )(a, w, x, post_g, gate, target)


def _in_grad(dz, w, dxo, x, pre_g, scale, prev, name, *, tm=512):
    J, T, n = dz.shape
    d = w.shape[1]
    tm = min(tm, T)
    has_prev = prev is not None
    res_w = prev[3] if has_prev else None

    def body(*refs):
        dz_ref, w_ref, dxo_ref, x_ref, g_ref, sc_ref = refs[:6]
        if has_prev:
            yp_ref, ppg_ref, pgate_ref, dx_ref, dsh_ref, dsc_ref, dg_ref, dyp_ref, dgate_ref, dpost_ref = refs[6:]
            _zero_at_start(dsh_ref, dsc_ref, dg_ref, dgate_ref, dpost_ref)
        else:
            dx_ref, dsh_ref, dsc_ref, dg_ref = refs[6:]
            _zero_at_start(dsh_ref, dsc_ref, dg_ref)
        for rows in _row_blocks(tm):
            dh = _sum_dots(dz_ref, w_ref, NT, rows)
            xv = x_ref[rows]
            r = _rms(xv)
            xh = xv * r
            dsh_ref[...] += _colsum(dh)
            dsc_ref[...] += _colsum(dh * (xh * g_ref[...]))
            dn = dh * (1.0 + sc_ref[...])
            dg_ref[...] += _colsum(dn * xh)
            dxh = dn * g_ref[...]
            dx = dxo_ref[rows] + r * (dxh - xh * _rowmean(dxh * xh))
            dx_ref[rows] = dx
            if has_prev:
                dyp_ref[rows] = _postnorm_bwd_math(dx, yp_ref[rows], ppg_ref[...], pgate_ref[...], res_w,
                                                   dgate_ref, dpost_ref)

    vec = jax.ShapeDtypeStruct((1, d), F32)
    in_specs = [pl.BlockSpec((J, tm, n), lambda m: (0, m, 0)), _whole_spec(w),
                _row_spec(tm, d), _row_spec(tm, d), _vec_spec(d), _vec_spec(d)]
    out_shape = [jax.ShapeDtypeStruct((T, d), F32), vec, vec, vec]
    out_specs = [_row_spec(tm, d), _vec_spec(d), _vec_spec(d), _vec_spec(d)]
    args = [dz, w, dxo, x, pre_g, scale]
    if has_prev:
        in_specs += [_row_spec(tm, d), _vec_spec(d), _vec_spec(d)]
        out_shape += [jax.ShapeDtypeStruct((T, d), BF16), vec, vec]
        out_specs += [_row_spec(tm, d), _vec_spec(d), _vec_spec(d)]
        args += list(prev[:3])
    return pl.pallas_call(
        body, name=name, grid=(T // tm,),
        out_shape=tuple(out_shape), in_specs=in_specs, out_specs=tuple(out_specs),
        compiler_params=_params("arbitrary"),
    )(*args)


def _mm_wgrad(xs, ys, name, *, tt=2048):
    Jx, T, P = xs.shape
    Jy, _, Q = ys.shape
    J = max(Jx, Jy)
    tt = min(tt, T)
    nt = T // tt

    def body(x_ref, y_ref, o_ref, acc_ref):
        t = pl.program_id(1)

        @pl.when(t == 0)
        def _():
            acc_ref[...] = jnp.zeros_like(acc_ref)

        acc_ref[...] += _dot(x_ref[...], y_ref[...], TN)

        @pl.when(t == nt - 1)
        def _():
            o_ref[...] = acc_ref[...].astype(BF16)

    return pl.pallas_call(
        body, name=name, grid=(J, nt),
        out_shape=jax.ShapeDtypeStruct((J, P, Q), BF16),
        in_specs=[pl.BlockSpec((None, tt, P), (lambda j, t: (j, t, 0)) if Jx > 1 else (lambda j, t: (0, t, 0))),
                  pl.BlockSpec((None, tt, Q), (lambda j, t: (j, t, 0)) if Jy > 1 else (lambda j, t: (0, t, 0)))],
        out_specs=pl.BlockSpec((None, P, Q), lambda j, t: (j, 0, 0)),
        scratch_shapes=[pltpu.VMEM((P, Q), F32)],
        compiler_params=_params("parallel", "arbitrary"),
    )(xs, ys)


def _ffn_in(h, wg, name, *, tm=1024):
    T, K = h.shape
    n = wg.shape[-1]
    tm = min(tm, T)

    def body(h_ref, w_ref, ab_ref, g_ref):
        for rows in _row_blocks(tm):
            hh = h_ref[rows]
            a = _dot(hh, w_ref[0])
            b = _dot(hh, w_ref[1])
            s = _sigmoid_t(a)
            silu = a * s
            ab_ref[0, rows] = (b * (s * (1.0 + a * (1.0 - s)))).astype(BF16)
            ab_ref[1, rows] = silu.astype(BF16)
            g_ref[rows] = (silu * b).astype(BF16)

    return pl.pallas_call(
        body, name=name, grid=(4, T // tm),
        out_shape=(jax.ShapeDtypeStruct((2, 4, T, n), BF16), jax.ShapeDtypeStruct((4, T, n), BF16)),
        in_specs=[pl.BlockSpec((tm, K), lambda j, m: (m, 0)),
                  pl.BlockSpec((2, None, K, n), lambda j, m: (0, j, 0, 0))],
        out_specs=(pl.BlockSpec((2, None, tm, n), lambda j, m: (0, j, m, 0)),
                   pl.BlockSpec((None, tm, n), lambda j, m: (j, m, 0))),
        compiler_params=_params("parallel", "parallel"),
    )(h, wg)


def _ffn_dgate(dy, wo4, ab, name, *, tm=1024):
    T, N = dy.shape
    n = wo4.shape[1]
    tm = min(tm, T)

    def body(dy_ref, w_ref, ab_ref, dab_ref):
        for rows in _row_blocks(tm):
            dg = _dot(dy_ref[rows], w_ref[...], NT)
            dab_ref[0, rows] = (dg * ab_ref[0, rows].astype(F32)).astype(BF16)
            dab_ref[1, rows] = (dg * ab_ref[1, rows].astype(F32)).astype(BF16)

    return pl.pallas_call(
        body, name=name, grid=(4, T // tm),
        out_shape=jax.ShapeDtypeStruct((2, 4, T, n), BF16),
        in_specs=[pl.BlockSpec((tm, N), lambda j, m: (m, 0)),
                  pl.BlockSpec((None, n, N), lambda j, m: (j, 0, 0)),
                  pl.BlockSpec((2, None, tm, n), lambda j, m: (0, j, m, 0))],
        out_specs=pl.BlockSpec((2, None, tm, n), lambda j, m: (0, j, m, 0)),
        compiler_params=_params("parallel", "parallel"),
    )(dy, wo4, ab)


def _row_spec(tm, d):
    return pl.BlockSpec((tm, d), lambda m: (m, 0))


def _vec_spec(d):
    return pl.BlockSpec((1, d), lambda m: (0, 0))


def _prenorm_fwd(x, pre_g, scale, shift, name, *, tm=512):
    T, d = x.shape
    tm = min(tm, T)

    def body(x_ref, g_ref, sc_ref, sh_ref, h_ref):
        xv = x_ref[...]
        r = lax.rsqrt(_rowmean(xv * xv) + EPS)
        h_ref[...] = (xv * r * g_ref[...] * (1.0 + sc_ref[...]) + sh_ref[...]).astype(BF16)

    return pl.pallas_call(
        body, name=name, grid=(T // tm,),
        out_shape=jax.ShapeDtypeStruct((T, d), BF16),
        in_specs=[_row_spec(tm, d), _vec_spec(d), _vec_spec(d), _vec_spec(d)],
        out_specs=_row_spec(tm, d),
        compiler_params=_params("parallel"),
    )(x, pre_g, scale, shift)


def _postnorm_fwd(x, y, post_g, gate, res_w, name, *, tm=512):
    T, d = x.shape
    tm = min(tm, T)

    def body(x_ref, y_ref, g_ref, gate_ref, o_ref):
        yv = y_ref[...]
        r = lax.rsqrt(_rowmean(yv * yv) + EPS)
        o_ref[...] = x_ref[...] + res_w * gate_ref[...] * (yv * r * g_ref[...])

    return pl.pallas_call(
        body, name=name, grid=(T // tm,),
        out_shape=jax.ShapeDtypeStruct((T, d), F32),
        in_specs=[_row_spec(tm, d), _row_spec(tm, d), _vec_spec(d), _vec_spec(d)],
        out_specs=_row_spec(tm, d),
        compiler_params=_params("parallel"),
    )(x, y, post_g, gate)


def _postnorm_bwd(dxo, y, post_g, gate, res_w, name, *, tm=512):
    T, d = y.shape
    tm = min(tm, T)

    def body(dxo_ref, y_ref, g_ref, gate_ref, dy_ref, dgate_ref, dg_ref):
        @pl.when(pl.program_id(0) == 0)
        def _():
            dgate_ref[...] = jnp.zeros_like(dgate_ref)
            dg_ref[...] = jnp.zeros_like(dg_ref)

        yv = y_ref[...]
        r = lax.rsqrt(_rowmean(yv * yv) + EPS)
        yh = yv * r
        dr = dxo_ref[...] * res_w
        dgate_ref[...] += _colsum(dr * (yh * g_ref[...]))
        dp = dr * gate_ref[...]
        dg_ref[...] += _colsum(dp * yh)
        dyh = dp * g_ref[...]
        dy_ref[...] = (r * (dyh - yh * _rowmean(dyh * yh))).astype(BF16)

    return pl.pallas_call(
        body, name=name, grid=(T // tm,),
        out_shape=(jax.ShapeDtypeStruct((T, d), BF16), jax.ShapeDtypeStruct((1, d), F32),
                   jax.ShapeDtypeStruct((1, d), F32)),
        in_specs=[_row_spec(tm, d), _row_spec(tm, d), _vec_spec(d), _vec_spec(d)],
        out_specs=(_row_spec(tm, d), _vec_spec(d), _vec_spec(d)),
        compiler_params=_params("arbitrary"),
    )(dxo, y, post_g, gate)


def _prenorm_bwd(dxo, dh, x, pre_g, scale, name, *, tm=512):
    T, d = x.shape
    tm = min(tm, T)

    def body(dxo_ref, dh_ref, x_ref, g_ref, sc_ref, dx_ref, dsh_ref, dsc_ref, dg_ref):
        @pl.when(pl.program_id(0) == 0)
        def _():
            dsh_ref[...] = jnp.zeros_like(dsh_ref)
            dsc_ref[...] = jnp.zeros_like(dsc_ref)
            dg_ref[...] = jnp.zeros_like(dg_ref)

        xv = x_ref[...]
        dhv = dh_ref[...].astype(F32)
        r = lax.rsqrt(_rowmean(xv * xv) + EPS)
        xh = xv * r
        dsh_ref[...] += _colsum(dhv)
        dsc_ref[...] += _colsum(dhv * (xh * g_ref[...]))
        dn = dhv * (1.0 + sc_ref[...])
        dg_ref[...] += _colsum(dn * xh)
        dxh = dn * g_ref[...]
        dx_ref[...] = dxo_ref[...] + r * (dxh - xh * _rowmean(dxh * xh))

    return pl.pallas_call(
        body, name=name, grid=(T // tm,),
        out_shape=(jax.ShapeDtypeStruct((T, d), F32),) + (jax.ShapeDtypeStruct((1, d), F32),) * 3,
        in_specs=[_row_spec(tm, d), _row_spec(tm, d), _row_spec(tm, d), _vec_spec(d), _vec_spec(d)],
        out_specs=(_row_spec(tm, d),) + (_vec_spec(d),) * 3,
        compiler_params=_params("arbitrary"),
    )(dxo, dh, x, pre_g, scale)


def _loss_head(y, target, name, *, tm=512):
    T, d = y.shape
    tm = min(tm, T)

    def body(y_ref, t_ref, dy_ref, l_ref):
        @pl.when(pl.program_id(0) == 0)
        def _():
            l_ref[...] = jnp.zeros_like(l_ref)

        e = y_ref[...] - t_ref[...]
        dy_ref[...] = e * (1.0 / d)
        l_ref[...] += 0.5 * jnp.sum(_rowmean(e * e), axis=0, keepdims=True)

    return pl.pallas_call(
        body, name=name, grid=(T // tm,),
        out_shape=(jax.ShapeDtypeStruct((T, d), F32), jax.ShapeDtypeStruct((1, 128), F32)),
        in_specs=[_row_spec(tm, d), _row_spec(tm, d)],
        out_specs=(_row_spec(tm, d), pl.BlockSpec((1, 128), lambda m: (0, 0))),
        compiler_params=_params("arbitrary"),
    )(y, target)


def _ada_fwd(c_all, w, b, name):
    L, K, n = w.shape

    def body(c_ref, w_ref, b_ref, o_ref):
        cv = c_ref[...]
        cond = cv * _sigmoid(cv)
        for l in range(L):
            o_ref[l] = _dot(cond, w_ref[l], precision=lax.Precision.HIGHEST) + b_ref[l]

    return pl.pallas_call(
        body, name=name,
        out_shape=jax.ShapeDtypeStruct((L, NDEV, n), F32),
        compiler_params=pltpu.CompilerParams(vmem_limit_bytes=VMEM_LIMIT),
    )(c_all, w, b)


def _ada_bwd(c_all_t, gmod, name):
    L, _, n = gmod.shape
    K = c_all_t.shape[0]

    def body(c_ref, g_ref, o_ref):
        cv = c_ref[...]
        cond = cv * _sigmoid(cv)
        for l in range(L):
            o_ref[l] = _dot(cond, g_ref[l], precision=lax.Precision.HIGHEST)

    return pl.pallas_call(
        body, name=name,
        out_shape=jax.ShapeDtypeStruct((L, K, n), F32),
        compiler_params=pltpu.CompilerParams(vmem_limit_bytes=VMEM_LIMIT),
    )(c_all_t, gmod)


def _tri(n, upper=False, block=None):
    r = lax.broadcasted_iota(jnp.int32, (n, n), 0)
    c = lax.broadcasted_iota(jnp.int32, (n, n), 1)
    m = (c >= r) if upper else (c <= r)
    if block is not None:
        m = m & ((r // block) == (c // block))
    return m.astype(F32)


def _hgrn_gates(proj_ref, lb_ref, jh):
    lb = lb_ref[:, 512 * jh:512 * (jh + 1)]
    qp = proj_ref[jh]
    fx = proj_ref[2 + jh]
    sq = _sigmoid(qp)
    sig = _sigmoid(fx)
    f = lb + (1.0 - lb) * sig
    k = (1.0 - lb) * (1.0 - sig)
    return lb, qp, sq, sig, f, k


def _hgrn_fwd(proj, lb, out_norm, name, *, tb=128):
    T = proj.shape[1]
    tb = min(tb, T)
    nc = tb // HG_CHUNK
    lmat = _tri(tb, block=HG_CHUNK)

    def body(proj_ref, lb_ref, on_ref, l_ref, o_ref, og_ref, st_ref, s_scr, b_scr):
        @pl.when(pl.program_id(0) == 0)
        def _():
            s_scr[...] = jnp.zeros_like(s_scr)

        r_i = lax.broadcasted_iota(jnp.int32, (HG_CHUNK, HG_CHUNK), 0)
        c_i = lax.broadcasted_iota(jnp.int32, (HG_CHUNK, HG_CHUNK), 1)
        causal = c_i <= r_i
        onv = on_ref[...]
        for jh in range(2):
            lbv, qp, sq, sig, f, k = _hgrn_gates(proj_ref, lb_ref, jh)
            q = qp * sq
            b_scr[...] = _dot(l_ref[...], jnp.log(f), precision=lax.Precision.HIGHEST)
            v = proj_ref[4 + jh]
            gp = proj_ref[6 + jh]
            gs = gp * _sigmoid(gp)
            for hh in range(4):
                hd = 4 * jh + hh
                cs = slice(HG_HEAD * hh, HG_HEAD * (hh + 1))
                for ci in range(nc):
                    r0 = HG_CHUNK * ci
                    rs = slice(r0, r0 + HG_CHUNK)
                    bc = b_scr[rs, cs]
                    bm = b_scr[r0 + HG_CHUNK // 2 - 1:r0 + HG_CHUNK // 2, cs]
                    bl = b_scr[r0 + HG_CHUNK - 1:r0 + HG_CHUNK, cs]
                    qc, kc, vc = q[rs, cs], k[rs, cs], v[rs, cs].astype(BF16)
                    qe = (qc * jnp.exp(bc)).astype(BF16)
                    qt = (qc * jnp.exp(bc - bm)).astype(BF16)
                    kt = (kc * jnp.exp(bm - bc)).astype(BF16)
                    kd = (kc * jnp.exp(bl - bc)).astype(BF16)
                    st = s_scr[hd]
                    stb = st.astype(BF16)
                    st_ref[ci, hd] = stb
                    a = jnp.where(causal, _dot(qt, kt, NT), 0.0).astype(BF16)
                    o = _dot(qe, stb, NT) + _dot(a, vc)
                    s_scr[hd] = st * jnp.exp(bl) + _dot(vc, kd, TN)
                    o_ref[rs, HG_HEAD * hd:HG_HEAD * (hd + 1)] = o
                    r = lax.rsqrt(_rowmean(o * o) + EPS)
                    og_ref[rs, HG_HEAD * hd:HG_HEAD * (hd + 1)] = (o * r * onv * gs[rs, cs]).astype(BF16)

    return pl.pallas_call(
        body, name=name, grid=(T // tb,),
        out_shape=(jax.ShapeDtypeStruct((T, D_MODEL), F32), jax.ShapeDtypeStruct((T, D_MODEL), BF16),
                   jax.ShapeDtypeStruct((T // HG_CHUNK, HG_HEADS, HG_HEAD, HG_HEAD), BF16)),
        in_specs=[pl.BlockSpec((8, tb, 512), lambda i: (0, i, 0)),
                  pl.BlockSpec((1, D_MODEL), lambda i: (0, 0)),
                  pl.BlockSpec((1, HG_HEAD), lambda i: (0, 0)),
                  pl.BlockSpec((tb, tb), lambda i: (0, 0))],
        out_specs=(pl.BlockSpec((tb, D_MODEL), lambda i: (i, 0)),
                   pl.BlockSpec((tb, D_MODEL), lambda i: (i, 0)),
                   pl.BlockSpec((nc, HG_HEADS, HG_HEAD, HG_HEAD), lambda i: (i, 0, 0, 0))),
        scratch_shapes=[pltpu.VMEM((HG_HEADS, HG_HEAD, HG_HEAD), F32), pltpu.VMEM((tb, 512), F32)],
        compiler_params=_params("arbitrary"),
    )(proj, lb, out_norm, lmat)


def _hgrn_bwd(proj, o, dog, states, lb, out_norm, name, *, tb=128):
    T = proj.shape[1]
    tb = min(tb, T)
    nc = tb // HG_CHUNK
    nb = T // tb
    lmat = _tri(tb, block=HG_CHUNK)
    umat = _tri(tb, upper=True, block=HG_CHUNK)

    def body(proj_ref, o_ref, dog_ref, st_ref, lb_ref, on_ref, l_ref, u_ref,
             dproj_ref, dlb_ref, don_ref, ds_scr, b_scr, dq_scr, dk_scr, dv_scr, dg_scr, db_scr):
        @pl.when(pl.program_id(0) == 0)
        def _():
            ds_scr[...] = jnp.zeros_like(ds_scr)
            dlb_ref[...] = jnp.zeros_like(dlb_ref)
            don_ref[...] = jnp.zeros_like(don_ref)

        r_i = lax.broadcasted_iota(jnp.int32, (HG_CHUNK, HG_CHUNK), 0)
        c_i = lax.broadcasted_iota(jnp.int32, (HG_CHUNK, HG_CHUNK), 1)
        causal = c_i <= r_i
        causal_t = r_i <= c_i
        last_row = lax.broadcasted_iota(jnp.int32, (HG_CHUNK, HG_HEAD), 0) == HG_CHUNK - 1
        onv = on_ref[...]
        don_acc = jnp.zeros((1, HG_HEAD), F32)
        for jh in range(2):
            lbv, qp, sq, sig, f, k = _hgrn_gates(proj_ref, lb_ref, jh)
            q = qp * sq
            b_scr[...] = _dot(l_ref[...], jnp.log(f), precision=lax.Precision.HIGHEST)
            v = proj_ref[4 + jh]
            gp = proj_ref[6 + jh]
            sg = _sigmoid(gp)
            for hh in range(4):
                hd = 4 * jh + hh
                cs = slice(HG_HEAD * hh, HG_HEAD * (hh + 1))
                hs = slice(HG_HEAD * hd, HG_HEAD * (hd + 1))
                for ci in reversed(range(nc)):
                    r0 = HG_CHUNK * ci
                    rs = slice(r0, r0 + HG_CHUNK)
                    oc = o_ref[rs, hs]
                    r = lax.rsqrt(_rowmean(oc * oc) + EPS)
                    oh = oc * r
                    gc, sgc = gp[rs, cs], sg[rs, cs]
                    dogc = dog_ref[rs, hs].astype(F32)
                    don = dogc * (gc * sgc)
                    dg_scr[rs, cs] = dogc * (oh * onv) * (sgc * (1.0 + gc * (1.0 - sgc)))
                    don_acc += _colsum(don * oh)
                    donh = don * onv
                    do = (r * (donh - oh * _rowmean(donh * oh))).astype(BF16)
                    bc = b_scr[rs, cs]
                    bm = b_scr[r0 + HG_CHUNK // 2 - 1:r0 + HG_CHUNK // 2, cs]
                    bl = b_scr[r0 + HG_CHUNK - 1:r0 + HG_CHUNK, cs]
                    qc, kc, vc = q[rs, cs], k[rs, cs], v[rs, cs].astype(BF16)
                    e_b, e_q, e_k, e_d = jnp.exp(bc), jnp.exp(bc - bm), jnp.exp(bm - bc), jnp.exp(bl - bc)
                    qe = (qc * e_b).astype(BF16)
                    qt = (qc * e_q).astype(BF16)
                    kt = (kc * e_k).astype(BF16)
                    kd = (kc * e_d).astype(BF16)
                    stb = st_ref[ci, hd]
                    dst = ds_scr[hd]
                    dstb = dst.astype(BF16)
                    a_t = jnp.where(causal_t, _dot(kt, qt, NT), 0.0).astype(BF16)
                    da = jnp.where(causal, _dot(do, vc, NT), 0.0).astype(BF16)
                    da_t = jnp.where(causal_t, _dot(vc, do, NT), 0.0).astype(BF16)
                    dv_scr[rs, cs] = _dot(a_t, do) + _dot(kd, dstb, NT)
                    dqe, dqt = _dot(do, stb), _dot(da, kt)
                    dkt, dkd = _dot(da_t, qt), _dot(vc, dstb)
                    dq_scr[rs, cs] = dqe * e_b + dqt * e_q
                    dk_scr[rs, cs] = dkt * e_k + dkd * e_d
                    e_l = jnp.exp(bl)
                    s_end = stb.astype(F32) * e_l + _dot(vc, kd, TN)
                    dbc = (qe.astype(F32) * dqe + qt.astype(F32) * dqt
                           - kt.astype(F32) * dkt - kd.astype(F32) * dkd)
                    db_scr[rs, cs] = dbc + jnp.where(last_row, _colsum(dstb.astype(F32) * s_end), 0.0)
                    ds_scr[hd] = dst * e_l + _dot(do, qe, TN)
            dq = dq_scr[...]
            dk = dk_scr[...]
            cols = slice(512 * jh, 512 * (jh + 1))
            dlogf = _dot(u_ref[...], db_scr[...], precision=lax.Precision.HIGHEST)
            one_m_sig = 1.0 - sig
            dsig = (1.0 - lbv) * sig * one_m_sig
            dproj_ref[jh] = (dq * (sq * (1.0 + qp * (1.0 - sq)))).astype(BF16)
            dproj_ref[2 + jh] = (dlogf * dsig / f - dk * dsig).astype(BF16)
            dproj_ref[4 + jh] = dv_scr[...].astype(BF16)
            dproj_ref[6 + jh] = dg_scr[...].astype(BF16)
            dlb_ref[:, cols] += _colsum(dlogf * one_m_sig / f - dk * one_m_sig)
        don_ref[...] += don_acc

    rev = lambda i: nb - 1 - i
    return pl.pallas_call(
        body, name=name, grid=(nb,),
        out_shape=(jax.ShapeDtypeStruct((8, T, 512), BF16), jax.ShapeDtypeStruct((1, D_MODEL), F32),
                   jax.ShapeDtypeStruct((1, HG_HEAD), F32)),
        in_specs=[pl.BlockSpec((8, tb, 512), lambda i: (0, rev(i), 0)),
                  pl.BlockSpec((tb, D_MODEL), lambda i: (rev(i), 0)),
                  pl.BlockSpec((tb, D_MODEL), lambda i: (rev(i), 0)),
                  pl.BlockSpec((nc, HG_HEADS, HG_HEAD, HG_HEAD), lambda i: (rev(i), 0, 0, 0)),
                  pl.BlockSpec((1, D_MODEL), lambda i: (0, 0)),
                  pl.BlockSpec((1, HG_HEAD), lambda i: (0, 0)),
                  pl.BlockSpec((tb, tb), lambda i: (0, 0)),
                  pl.BlockSpec((tb, tb), lambda i: (0, 0))],
        out_specs=(pl.BlockSpec((8, tb, 512), lambda i: (0, rev(i), 0)),
                   pl.BlockSpec((1, D_MODEL), lambda i: (0, 0)),
                   pl.BlockSpec((1, HG_HEAD), lambda i: (0, 0))),
        scratch_shapes=[pltpu.VMEM((HG_HEADS, HG_HEAD, HG_HEAD), F32)] + [pltpu.VMEM((tb, 512), F32)] * 6,
        compiler_params=_params("arbitrary"),
    )(proj, o, dog, states, lb, out_norm, lmat, umat)


def _gm_norm(pre_ref, lg_ref, lbias_ref):
    vs = [_gelu(pre_ref[4 + j].astype(F32)) for j in range(4)]
    width = 4 * vs[0].shape[1]
    mu = sum(jnp.sum(v, axis=1, keepdims=True) for v in vs) / width
    ds = [v - mu for v in vs]
    var = sum(jnp.sum(d * d, axis=1, keepdims=True) for d in ds) / width
    rstd = lax.rsqrt(var + EPS)
    vhat = [d * rstd for d in ds]
    vn = [vhat[j] * lg_ref[j:j + 1, :] + lbias_ref[j:j + 1, :] for j in range(4)]
    return vhat, vn, rstd


def _gm_spatial_fwd(pre, ln_g, ln_b, ws, bsb, name, *, tb=256):
    T = pre.shape[1]
    tb = min(tb, T)
    nc = tb // GM_CHUNK

    def body(pre_ref, lg_ref, lbias_ref, ws_ref, bs_ref, o_ref):
        _, vn, _ = _gm_norm(pre_ref, lg_ref, lbias_ref)
        for j in range(4):
            u = _gelu(pre_ref[j].astype(F32))
            for e in range(2):
                g = 2 * j + e
                cs = slice(GM_GDIM * e, GM_GDIM * (e + 1))
                wg = ws_ref[g].astype(BF16)
                for ci in range(nc):
                    rs = slice(GM_CHUNK * ci, GM_CHUNK * (ci + 1))
                    vm = _dot(wg, vn[j][rs, cs].astype(BF16)) + bs_ref[g]
                    o_ref[j, rs, cs] = (u[rs, cs] * vm).astype(BF16)

    return pl.pallas_call(
        body, name=name, grid=(T // tb,),
        out_shape=jax.ShapeDtypeStruct((4, T, 768), BF16),
        in_specs=[pl.BlockSpec((8, tb, 768), lambda i: (0, i, 0)),
                  pl.BlockSpec((4, 768), lambda i: (0, 0)),
                  pl.BlockSpec((4, 768), lambda i: (0, 0)),
                  pl.BlockSpec((GM_GROUPS, GM_CHUNK, GM_CHUNK), lambda i: (0, 0, 0)),
                  pl.BlockSpec((GM_GROUPS, GM_CHUNK, GM_GDIM), lambda i: (0, 0, 0))],
        out_specs=pl.BlockSpec((4, tb, 768), lambda i: (0, i, 0)),
        compiler_params=_params("parallel"),
    )(pre, ln_g, ln_b, ws, bsb)


def _gm_spatial_bwd(pre, dm, ln_g, ln_b, ws, ws_t, bsb, name, *, tb=256):
    T = pre.shape[1]
    tb = min(tb, T)
    nc = tb // GM_CHUNK
    nb = T // tb

    def body(pre_ref, dm_ref, lg_ref, lbias_ref, ws_ref, wst_ref, bs_ref,
             dpre_ref, dws_ref, dbs_ref, dlg_ref, dlb_ref, dbin_ref, dbs_scr, dvn_scr, du_scr):
        i = pl.program_id(0)

        @pl.when(i == 0)
        def _():
            dws_ref[...] = jnp.zeros_like(dws_ref)
            dbs_scr[...] = jnp.zeros_like(dbs_scr)
            dlg_ref[...] = jnp.zeros_like(dlg_ref)
            dlb_ref[...] = jnp.zeros_like(dlb_ref)
            dbin_ref[...] = jnp.zeros_like(dbin_ref)

        vhat, vn, rstd = _gm_norm(pre_ref, lg_ref, lbias_ref)
        for j in range(4):
            u = _gelu(pre_ref[j].astype(F32))
            for e in range(2):
                g = 2 * j + e
                cs = slice(GM_GDIM * e, GM_GDIM * (e + 1))
                wg = ws_ref[g].astype(BF16)
                wgt = wst_ref[g].astype(BF16)
                for ci in range(nc):
                    rs = slice(GM_CHUNK * ci, GM_CHUNK * (ci + 1))
                    vnb = vn[j][rs, cs].astype(BF16)
                    vm = _dot(wg, vnb) + bs_ref[g]
                    dmg = dm_ref[j, rs, cs].astype(F32)
                    du_scr[j, rs, cs] = dmg * vm
                    dvm = dmg * u[rs, cs]
                    dvmb = dvm.astype(BF16)
                    dws_ref[g] += _dot(dvmb, vnb, NT)
                    dbs_scr[g] += dvm
                    dvn_scr[j, rs, cs] = _dot(wgt, dvmb)
        width = 4 * 768
        dvh = []
        for j in range(4):
            dvn = dvn_scr[j]
            dlg_ref[j:j + 1, :] += _colsum(dvn * vhat[j])
            dlb_ref[j:j + 1, :] += _colsum(dvn)
            dvh.append(dvn * lg_ref[j:j + 1, :])
        m1 = sum(jnp.sum(d, axis=1, keepdims=True) for d in dvh) / width
        m2 = sum(jnp.sum(dvh[j] * vhat[j], axis=1, keepdims=True) for j in range(4)) / width
        for j in range(4):
            dv = rstd * (dvh[j] - m1 - vhat[j] * m2)
            dpv = dv * _gelu_grad(pre_ref[4 + j].astype(F32))
            dpu = du_scr[j] * _gelu_grad(pre_ref[j].astype(F32))
            dpre_ref[4 + j] = dpv.astype(BF16)
            dpre_ref[j] = dpu.astype(BF16)
            dbin_ref[4 + j:5 + j, :] += _colsum(dpv)
            dbin_ref[j:j + 1, :] += _colsum(dpu)

        @pl.when(i == nb - 1)
        def _():
            r_i = lax.broadcasted_iota(jnp.int32, (GM_CHUNK, GM_CHUNK), 0)
            c_i = lax.broadcasted_iota(jnp.int32, (GM_CHUNK, GM_CHUNK), 1)
            for g in range(GM_GROUPS):
                dws_ref[g] = jnp.where(c_i <= r_i, dws_ref[g], 0.0)
                dbs_ref[g] = jnp.broadcast_to(jnp.sum(dbs_scr[g], axis=1, keepdims=True), (GM_CHUNK, GM_CHUNK))

    sq = pl.BlockSpec((GM_GROUPS, GM_CHUNK, GM_CHUNK), lambda i: (0, 0, 0))
    v4 = pl.BlockSpec((4, 768), lambda i: (0, 0))
    return pl.pallas_call(
        body, name=name, grid=(nb,),
        out_shape=(jax.ShapeDtypeStruct((8, T, 768), BF16),
                   jax.ShapeDtypeStruct((GM_GROUPS, GM_CHUNK, GM_CHUNK), F32),
                   jax.ShapeDtypeStruct((GM_GROUPS, GM_CHUNK, GM_CHUNK), F32),
                   jax.ShapeDtypeStruct((4, 768), F32), jax.ShapeDtypeStruct((4, 768), F32),
                   jax.ShapeDtypeStruct((8, 768), F32)),
        in_specs=[pl.BlockSpec((8, tb, 768), lambda i: (0, i, 0)),
                  pl.BlockSpec((4, tb, 768), lambda i: (0, i, 0)),
                  v4, v4, sq, sq,
                  pl.BlockSpec((GM_GROUPS, GM_CHUNK, GM_GDIM), lambda i: (0, 0, 0))],
        out_specs=(pl.BlockSpec((8, tb, 768), lambda i: (0, i, 0)), sq, sq, v4, v4,
                   pl.BlockSpec((8, 768), lambda i: (0, 0))),
        scratch_shapes=[pltpu.VMEM((GM_GROUPS, GM_CHUNK, GM_GDIM), F32),
                        pltpu.VMEM((4, tb, 768), F32), pltpu.VMEM((4, tb, 768), F32)],
        compiler_params=_params("arbitrary"),
    )(pre, dm, ln_g, ln_b, ws, ws_t, bsb)


def _adamw(slots, w, m, v, name, *, tr=256):
    S, R, C = slots.shape
    tr = next((t for t in (tr, tr // 2, tr // 4, tr // 8, tr // 16) if R % t == 0), R) if R > tr else R
    bc1 = 1.0 - ADAM_B1 ** ADAM_STEP
    bc2 = 1.0 - ADAM_B2 ** ADAM_STEP

    def body(s_ref, w_ref, m_ref, v_ref, g_ref, d_ref, nm_ref, nv_ref):
        g = s_ref[0].astype(F32)
        for s in range(1, S):
            g = g + s_ref[s].astype(F32)
        mn = ADAM_B1 * m_ref[...] + (1.0 - ADAM_B1) * g
        vn = ADAM_B2 * v_ref[...] + (1.0 - ADAM_B2) * (g * g)
        g_ref[...] = g
        nm_ref[...] = mn
        nv_ref[...] = vn
        d_ref[...] = -ADAM_LR * ((mn / bc1) / (jnp.sqrt(vn / bc2) + ADAM_EPS) + ADAM_WD * w_ref[...])

    spec = pl.BlockSpec((tr, C), lambda i: (i, 0))
    return pl.pallas_call(
        body, name=name, grid=(R // tr,),
        out_shape=(jax.ShapeDtypeStruct((R, C), F32),) * 4,
        in_specs=[pl.BlockSpec((S, tr, C), lambda i: (0, i, 0)), spec, spec, spec],
        out_specs=(spec,) * 4,
        compiler_params=_params("parallel"),
    )(slots, w, m, v)


def _update(slots, w, m, v, name):
    shp = w.shape
    C = shp[-1]
    R = math.prod(shp[:-1])
    outs = _adamw(slots.reshape(slots.shape[0], R, C), w.reshape(R, C), m.reshape(R, C), v.reshape(R, C), name)
    return tuple(o.reshape(shp) for o in outs)


def kernel(x, c, ada_w, ada_b, norm_pre, norm_post, ffn_w_in, ffn_w_out, hg_w_in, hg_w_out, hg_out_norm, hg_lb, gm_w_in, gm_b_in, gm_ln_g, gm_ln_b, gm_w_s, gm_b_s, gm_w_out, loss_target, m_ada_w, m_ada_b, m_norm_pre, m_norm_post, m_ffn_w_in, m_ffn_w_out, m_hg_w_in, m_hg_w_out, m_hg_out_norm, m_hg_lb, m_gm_w_in, m_gm_b_in, m_gm_ln_g, m_gm_ln_b, m_gm_w_s, m_gm_b_s, m_gm_w_out, v_ada_w, v_ada_b, v_norm_pre, v_norm_post, v_ffn_w_in, v_ffn_w_out, v_hg_w_in, v_hg_w_out, v_hg_out_norm, v_hg_lb, v_gm_w_in, v_gm_b_in, v_gm_ln_g, v_gm_ln_b, v_gm_w_s, v_gm_b_s, v_gm_w_out):
    me = 4 * lax.axis_index("x") + 2 * lax.axis_index("y") + lax.axis_index("c")
    T = x.shape[1]
    x0 = x.reshape(T, D_MODEL)
    target = loss_target.reshape(T, D_MODEL)
    n_ada = ada_w.shape[-1]

    pack = jnp.concatenate([
        c.reshape(8, 128), norm_pre.reshape(6, 128), norm_post.reshape(6, 128),
        gm_b_in.reshape(6, 128), gm_ln_g.reshape(3, 128), gm_ln_b.reshape(3, 128)], axis=0)
    packs = _all_gather(pack, "gather_small")
    c_all = packs[:, 0:8].reshape(NDEV, D_MODEL)
    npre = packs[:, 8:14].reshape(NDEV, 2, 3, 128).transpose(1, 2, 0, 3).reshape(2, 3, D_MODEL)
    npost = packs[:, 14:20].reshape(NDEV, 2, 3, 128).transpose(1, 2, 0, 3).reshape(2, 3, D_MODEL)
    b_in = packs[:, 20:26].reshape(NDEV, 1, 768)
    ln_g = packs[:, 26:29].reshape(4, 768)
    ln_b = packs[:, 29:32].reshape(4, 768)

    ada_b_mine = lax.dynamic_slice_in_dim(ada_b, me * n_ada, n_ada, axis=1).reshape(2, 1, n_ada)
    mod_cols = _ada_fwd(c_all, ada_w, ada_b_mine, "ada_fwd")
    mod_all = _all_gather(mod_cols, "gather_mod")
    mod = lax.dynamic_index_in_dim(mod_all, me, axis=2, keepdims=False)
    mod = mod.transpose(1, 0, 2).reshape(2, 9, 1, D_MODEL)

    w_fi = _all_gather(ffn_w_in.astype(BF16), "gather_ffn_in")
    w_fo = _all_gather(ffn_w_out.astype(BF16), "gather_ffn_out")
    w_hi = _all_gather(hg_w_in[0].astype(BF16), "gather_hg_in")
    w_ho = _all_gather(hg_w_out[0].astype(BF16), "gather_hg_out")
    w_mi = _all_gather(gm_w_in[0].astype(BF16), "gather_gm_in")
    w_mo = _all_gather(gm_w_out[0].astype(BF16), "gather_gm_out")
    n_ff = w_fi.shape[-1]
    w_ho = w_ho.reshape(1, D_MODEL, D_MODEL)
    w_mo = w_mo.reshape(4, 768, D_MODEL)

    sm = jax.nn.softmax(hg_lb, axis=0)
    lb0 = sm[0:1]
    on = hg_out_norm.reshape(1, HG_HEAD)
    tril = jnp.tril(jnp.ones((GM_CHUNK, GM_CHUNK), F32))
    ws = gm_w_s[0] * tril[None]
    ws_t = ws.transpose(0, 2, 1)
    bsb = jnp.broadcast_to(gm_b_s[0][:, :, None], (GM_GROUPS, GM_CHUNK, GM_GDIM))

    res_ws = (0.5, 1.0, 0.5)

    def vecs(i, s):
        return (npre[i, s].reshape(1, D_MODEL), npost[i, s].reshape(1, D_MODEL),
                mod[i, 3 * s], mod[i, 3 * s + 1], mod[i, 3 * s + 2])

    def ffn_weights(i, f):
        wg = w_fi[:, i, f].reshape(2, 4, D_MODEL, n_ff)
        wo = w_fo[:, i, f].reshape(4, n_ff, D_MODEL)
        return wg, wo

    order = [(i, s) for i in range(2) for s in range(3)]
    saved = {}
    xs = x0
    pre_g, _, shift, scale, _ = vecs(0, 0)
    h = _prenorm_fwd(xs, pre_g, scale, shift, "prenorm_l0s0")
    for pos, (i, s) in enumerate(order):
        tag = f"l{i}s{s}"
        _, post_g, _, _, gate = vecs(i, s)
        if s != 1:
            wg, wo = ffn_weights(i, s // 2)
            ab, a = _ffn_in(h, wg, "ffn_in_" + tag)
            extra = (ab, a)
        elif i == 0:
            proj = _mm_blocks(h, w_hi, "hg_in", out_dtype=F32)
            o, og, states = _hgrn_fwd(proj, lb0, on, "hg_mix")
            a, wo = og.reshape(1, T, D_MODEL), w_ho
            extra = (proj, o, og, states)
        else:
            pre = _mm_blocks(h, w_mi, "gm_in", bias=b_in)
            a = _gm_spatial_fwd(pre, ln_g, ln_b, ws, bsb, "gm_mix")
            wo = w_mo
            extra = (pre, a)
        if pos + 1 < len(order):
            npre_g, _, nshift, nscale, _ = vecs(*order[pos + 1])
            y, x_next, h_next = _out_proj(a, wo, xs, post_g, gate, res_ws[s], (npre_g, nscale, nshift), "out_" + tag)
            saved[tag] = (xs, h, y) + extra
            xs, h = x_next, h_next
        else:
            dx, dy, dgate, dpost, loss_part = _out_proj_last(a, wo, xs, post_g, gate, res_ws[s], target, "out_" + tag)
            saved[tag] = (xs, h, None) + extra
    loss = lax.psum(loss_part[0, 0], ("x", "y", "c"))

    g_fi = [[None, None], [None, None]]
    g_fo = [[None, None], [None, None]]
    d_npre = [[None] * 3, [None] * 3]
    d_npost = [[None] * 3, [None] * 3]
    d_mod = [[None] * 9, [None] * 9]
    for pos in reversed(range(len(order))):
        i, s = order[pos]
        tag = f"l{i}s{s}"
        pre_g, _, _, scale, _ = vecs(i, s)
        xin, h = saved[tag][:2]
        dy1 = dy.reshape(1, T, D_MODEL)
        if s != 1:
            wg, wo = ffn_weights(i, s // 2)
            ab, g = saved[tag][3:]
            dz = _ffn_dgate(dy, wo, ab, "ffn_dgate_" + tag).reshape(8, T, n_ff)
            g_fo[i][s // 2] = _mm_wgrad(g, dy1, "ffn_out_wgrad_" + tag)
            w_in = wg.reshape(8, D_MODEL, n_ff)
            g_fi[i][s // 2] = _mm_wgrad(h.reshape(1, T, D_MODEL), dz, "ffn_in_wgrad_" + tag)
        elif i == 0:
            proj, o, og, states = saved[tag][3:]
            dog = _mm_blocks(dy, w_ho, "hg_out_dgrad", transpose_w=True)[0]
            g_ho = _mm_wgrad(og.reshape(1, T, D_MODEL), dy1, "hg_out_wgrad")
            dz, d_lb0, d_on = _hgrn_bwd(proj, o, dog, states, lb0, on, "hg_mix_bwd")
            w_in = w_hi
            g_hi = _mm_wgrad(h.reshape(1, T, D_MODEL), dz, "hg_in_wgrad")
        else:
            pre, sp = saved[tag][3:]
            dm = _mm_blocks(dy, w_mo, "gm_out_dgrad", transpose_w=True)
            g_mo = _mm_wgrad(sp, dy1, "gm_out_wgrad")
            dz, d_ws, d_bs, d_lg, d_lbias, d_bin = _gm_spatial_bwd(pre, dm, ln_g, ln_b, ws, ws_t, bsb, "gm_mix_bwd")
            w_in = w_mi
            g_mi = _mm_wgrad(h.reshape(1, T, D_MODEL), dz, "gm_in_wgrad")
        d_npost[i][s] = dpost
        d_mod[i][3 * s + 2] = dgate
        if pos > 0:
            pi, ps = order[pos - 1]
            _, ppost_g, _, _, pgate = vecs(pi, ps)
            prev = (saved[f"l{pi}s{ps}"][2], ppost_g, pgate, res_ws[ps])
            dx, dshift, dscale, dpre_g, dy, dgate, dpost = _in_grad(dz, w_in, dx, xin, pre_g, scale, prev, "in_grad_" + tag)
        else:
            dx, dshift, dscale, dpre_g = _in_grad(dz, w_in, dx, xin, pre_g, scale, None, "in_grad_" + tag)
        d_npre[i][s] = dpre_g
        d_mod[i][3 * s], d_mod[i][3 * s + 1] = dshift, dscale
    grad_x = dx.reshape(x.shape)

    s_fi = _scatter_blocks([g_fi[0][0], g_fi[0][1], g_fi[1][0], g_fi[1][1]], "scatter_ffn_in")
    s_fo = _scatter_blocks([g.reshape(NDEV, n_ff // 2, D_MODEL) for g in
                            (g_fo[0][0], g_fo[0][1], g_fo[1][0], g_fo[1][1])], "scatter_ffn_out")
    s_hi = _scatter_blocks([g_hi], "scatter_hg_in")
    s_ho = _scatter_blocks([g_ho.reshape(NDEV, 128, D_MODEL)], "scatter_hg_out")
    s_mi = _scatter_blocks([g_mi], "scatter_gm_in")
    s_mo = _scatter_blocks([g_mo.reshape(NDEV, 384, D_MODEL)], "scatter_gm_out")

    gmod = jnp.stack([jnp.concatenate(d_mod[i], axis=0) for i in range(2)])
    d_sm = lb0 * d_lb0
    d_hg_lb = jnp.concatenate([d_sm, jnp.zeros((2, D_MODEL), F32)], axis=0) - sm * d_sm
    small = [gmod, jnp.stack([jnp.concatenate(r, axis=0) for r in d_npre]),
             jnp.stack([jnp.concatenate(r, axis=0) for r in d_npost]),
             d_on, d_hg_lb, d_bin, d_lg, d_lbias, d_ws, d_bs[:, :, 0]]
    sizes = [a.size for a in small]
    flat = jnp.concatenate([a.reshape(-1) for a in small])
    rows = -(-flat.size // (8 * 128)) * 8
    flat = jnp.pad(flat, (0, rows * 128 - flat.size)).reshape(rows, 128)
    flats = _all_gather(flat, "gather_small_grads").reshape(NDEV, rows * 128)
    parts, off = [], 0
    for a, n in zip(small, sizes):
        parts.append(flats[:, off:off + n].reshape((NDEV,) + a.shape))
        off += n
    p_mod, p_npre, p_npost, p_on, p_lb, p_bin, p_lg, p_lbias, p_ws, p_bs = parts

    def mine(p, width):
        return lax.dynamic_slice_in_dim(p, me * width, width, axis=p.ndim - 1)

    gmod_cols = mine(p_mod.reshape(NDEV, 2, 9 * D_MODEL), n_ada).transpose(1, 0, 2)
    g_ada_w = _ada_bwd(jnp.pad(c_all.T, ((0, 0), (0, 120))), jnp.pad(gmod_cols, ((0, 0), (0, 120), (0, 0))), "ada_bwd")

    out = {}
    out["ada_w"] = _update(g_ada_w[None], ada_w, m_ada_w, v_ada_w, "adamw_ada_w")
    out["ada_b"] = _update(p_mod.reshape(NDEV, 2, 9 * D_MODEL), ada_b, m_ada_b, v_ada_b, "adamw_ada_b")
    out["norm_pre"] = _update(mine(p_npre, 128), norm_pre, m_norm_pre, v_norm_pre, "adamw_norm_pre")
    out["norm_post"] = _update(mine(p_npost, 128), norm_post, m_norm_post, v_norm_post, "adamw_norm_post")
    out["ffn_w_in"] = _update(s_fi.reshape((NDEV,) + ffn_w_in.shape), ffn_w_in, m_ffn_w_in, v_ffn_w_in, "adamw_ffn_in")
    out["ffn_w_out"] = _update(s_fo.reshape((NDEV,) + ffn_w_out.shape), ffn_w_out, m_ffn_w_out, v_ffn_w_out, "adamw_ffn_out")
    out["hg_w_in"] = _update(s_hi, hg_w_in, m_hg_w_in, v_hg_w_in, "adamw_hg_in")
    out["hg_w_out"] = _update(s_ho, hg_w_out, m_hg_w_out, v_hg_w_out, "adamw_hg_out")
    out["hg_out_norm"] = _update(p_on, hg_out_norm, m_hg_out_norm, v_hg_out_norm, "adamw_hg_norm")
    out["hg_lb"] = _update(p_lb, hg_lb, m_hg_lb, v_hg_lb, "adamw_hg_lb")
    out["gm_w_in"] = _update(s_mi, gm_w_in, m_gm_w_in, v_gm_w_in, "adamw_gm_in")
    out["gm_b_in"] = _update(mine(p_bin.reshape(NDEV, 1, 8 * 768), 768), gm_b_in, m_gm_b_in, v_gm_b_in, "adamw_gm_b_in")
    out["gm_ln_g"] = _update(mine(p_lg.reshape(NDEV, 1, 4 * 768), 384), gm_ln_g, m_gm_ln_g, v_gm_ln_g, "adamw_gm_ln_g")
    out["gm_ln_b"] = _update(mine(p_lbias.reshape(NDEV, 1, 4 * 768), 384), gm_ln_b, m_gm_ln_b, v_gm_ln_b, "adamw_gm_ln_b")
    out["gm_w_s"] = _update(p_ws[:, None], gm_w_s, m_gm_w_s, v_gm_w_s, "adamw_gm_w_s")
    out["gm_b_s"] = _update(p_bs[:, None], gm_b_s, m_gm_b_s, v_gm_b_s, "adamw_gm_b_s")
    out["gm_w_out"] = _update(s_mo, gm_w_out, m_gm_w_out, v_gm_w_out, "adamw_gm_out")

    names = ["ada_w", "ada_b", "norm_pre", "norm_post", "ffn_w_in", "ffn_w_out", "hg_w_in", "hg_w_out",
             "hg_out_norm", "hg_lb", "gm_w_in", "gm_b_in", "gm_ln_g", "gm_ln_b", "gm_w_s", "gm_b_s", "gm_w_out"]
    return (loss, grad_x, *[out[n][0] for n in names], *[out[n][1] for n in names],
            *[out[n][2] for n in names], *[out[n][3] for n in names])
```

```python
import functools
import math

import jax
import jax.numpy as jnp
from jax import lax
from jax.experimental import pallas as pl
from jax.experimental.pallas import tpu as pltpu

F32 = jnp.float32
BF16 = jnp.bfloat16
NDEV = 8
D_MODEL = 1024
EPS = 1e-6
HG_CHUNK = 64
HG_HEAD = 128
HG_HEADS = 8
GM_CHUNK = 128
GM_GDIM = 384
GM_GROUPS = 8
ADAM_LR = 0.001
ADAM_B1 = 0.9
ADAM_B2 = 0.999
ADAM_EPS = 1e-08
ADAM_WD = 0.01
ADAM_STEP = 10
VMEM_LIMIT = 56 * 2 ** 20

NN = (((1,), (0,)), ((), ()))
NT = (((1,), (1,)), ((), ()))
TN = (((0,), (0,)), ((), ()))
MESH = pl.DeviceIdType.MESH
ANY = pl.BlockSpec(memory_space=pl.ANY)


def _dot(a, b, dims=NN, precision=None):
    return lax.dot_general(a, b, dims, preferred_element_type=F32, precision=precision)


def _params(*sem):
    return pltpu.CompilerParams(dimension_semantics=sem, vmem_limit_bytes=VMEM_LIMIT)


def _sigmoid(x):
    return 1.0 / (1.0 + jnp.exp(-x))


def _sigmoid_t(x):
    return 0.5 * jnp.tanh(0.5 * x) + 0.5


def _gelu(x):
    c = math.sqrt(2.0 / math.pi)
    return 0.5 * x * (1.0 + jnp.tanh(c * (x + 0.044715 * x * x * x)))


def _gelu_grad(x):
    c = math.sqrt(2.0 / math.pi)
    t = jnp.tanh(c * (x + 0.044715 * x * x * x))
    return 0.5 * (1.0 + t) + 0.5 * x * (1.0 - t * t) * c * (1.0 + 3.0 * 0.044715 * x * x)


def _colsum(x):
    return jnp.sum(x, axis=0, keepdims=True)


def _rowmean(x):
    return jnp.mean(x, axis=-1, keepdims=True)


def _all_gather(shard, name):
    def body(x_ref, out_ref, send_sems, recv_sems, local_sem):
        x, y, c = lax.axis_index("x"), lax.axis_index("y"), lax.axis_index("c")
        me, sibling = (x, y, c), (x, y, 1 - c)
        chips = [(1 - x, y), (x, 1 - y), (1 - x, 1 - y)]

        def slot(p):
            return out_ref.at[4 * p[0] + 2 * p[1] + p[2]]

        def copy(k, block, to, src=None):
            return pltpu.make_async_remote_copy(
                src_ref=slot(block) if src is None else src, dst_ref=slot(block),
                send_sem=send_sems.at[k], recv_sem=recv_sems.at[k],
                device_id=to, device_id_type=MESH)

        mine = pltpu.make_async_copy(x_ref, slot(me), local_sem)
        mine.start()
        first = [copy(0, me, sibling, src=x_ref)]
        first += [copy(1 + j, me, (*chip, c), src=x_ref) for j, chip in enumerate(chips)]
        for cp in first:
            cp.start()
        passed = [copy(4 + j, (*chip, c), sibling) for j, chip in enumerate(chips)]
        for j, chip in enumerate(chips):
            copy(1 + j, (*chip, c), me).wait_recv()
            passed[j].start()
        copy(0, sibling, me).wait_recv()
        for j, chip in enumerate(chips):
            copy(4 + j, (*chip, 1 - c), me).wait_recv()
        for cp in first + passed:
            cp.wait_send()
        mine.wait()

    return pl.pallas_call(
        body, name=name,
        out_shape=jax.ShapeDtypeStruct((NDEV,) + shard.shape, shard.dtype),
        in_specs=[ANY], out_specs=ANY,
        scratch_shapes=[pltpu.SemaphoreType.DMA((7,)), pltpu.SemaphoreType.DMA((7,)),
                        pltpu.SemaphoreType.DMA(())],
    )(shard)


class _Exchange:
    def __init__(self, kind, arrays):
        self.gather = kind == "gather"
        self.arrays = list(arrays)
        self.n = n = len(self.arrays)
        self.out_shape = [jax.ShapeDtypeStruct(((NDEV,) + a.shape) if self.gather else a.shape, a.dtype)
                          for a in self.arrays]
        self.scratch = [pltpu.SemaphoreType.DMA((n, NDEV - 1)), pltpu.SemaphoreType.DMA((n, NDEV - 1)),
                        pltpu.SemaphoreType.DMA((n,))]

    def _copies(self, in_refs, out_refs, sems):
        send_sems, recv_sems, local_sems = sems
        x, y, c = lax.axis_index("x"), lax.axis_index("y"), lax.axis_index("c")
        me = 4 * x + 2 * y + c
        peers = [(1 - x if k & 4 else x, 1 - y if k & 2 else y, 1 - c if k & 1 else c) for k in range(1, NDEV)]
        local, send, recv = [], [], []
        for a in range(self.n):
            src = (lambda pid, a=a: in_refs[a]) if self.gather else (lambda pid, a=a: in_refs[a].at[pid])
            local.append(pltpu.make_async_copy(src(me), out_refs[a].at[me], local_sems.at[a]))
            for k, p in enumerate(peers):
                pid = 4 * p[0] + 2 * p[1] + p[2]
                for lst, slot in ((send, me), (recv, pid)):
                    lst.append(pltpu.make_async_remote_copy(
                        src_ref=src(pid), dst_ref=out_refs[a].at[slot],
                        send_sem=send_sems.at[a, k], recv_sem=recv_sems.at[a, k],
                        device_id=p, device_id_type=MESH))
        return local, send, recv

    def start(self, first, in_refs, out_refs, sems):
        @pl.when(first)
        def _():
            local, send, _ = self._copies(in_refs, out_refs, sems)
            for cp in local + send:
                cp.start()

    def finish(self, last, in_refs, out_refs, sems):
        @pl.when(last)
        def _():
            local, send, recv = self._copies(in_refs, out_refs, sems)
            for cp in send:
                cp.wait_send()
            for cp in recv:
                cp.wait_recv()
            for cp in local:
                cp.wait()


def _host(exchange, n_in, n_out, body, first_last):
    if exchange is None:
        return body, [], [], [], []
    n = exchange.n

    def hosted(*refs):
        ins, refs = refs[:n_in], refs[n_in:]
        xin, refs = refs[:n], refs[n:]
        outs, refs = refs[:n_out], refs[n_out:]
        xout, refs = refs[:n], refs[n:]
        scratch, sems = refs[:len(refs) - 3], refs[len(refs) - 3:]
        first, last = first_last()
        exchange.start(first, xin, xout, sems)
        body(*ins, *outs, *scratch)
        exchange.finish(last, xin, xout, sems)

    return hosted, exchange.arrays, [ANY] * n, exchange.out_shape, exchange.scratch


def _first_last(steps):
    def at():
        i = pl.program_id(0)
        return i == 0, i == steps - 1
    return at


def _mm_blocks(a, w, name, *, transpose_w=False, bias=None, out_dtype=BF16, tm=512, exchange=None):
    T, K = a.shape
    J = w.shape[0]
    n = w.shape[1] if transpose_w else w.shape[2]
    tm = min(tm, T)
    dims = NT if transpose_w else NN

    def body(*refs):
        a_ref, w_ref = refs[:2]
        o_ref = refs[-1]
        av = a_ref[...]
        for j in range(J):
            r = _dot(av, w_ref[j], dims)
            if bias is not None:
                r = r + refs[2][j]
            o_ref[j] = r.astype(out_dtype)

    in_specs = [pl.BlockSpec((tm, K), lambda m: (m, 0)), _whole_spec(w)]
    args = [a, w]
    if bias is not None:
        in_specs.append(_whole_spec(bias))
        args.append(bias)
    body, x_args, x_in, x_out, x_scratch = _host(exchange, len(args), 1, body, _first_last(T // tm))
    res = pl.pallas_call(
        body, name=name, grid=(T // tm,),
        out_shape=[jax.ShapeDtypeStruct((J, T, n), out_dtype)] + x_out,
        in_specs=in_specs + x_in,
        out_specs=[pl.BlockSpec((J, tm, n), lambda m: (0, m, 0))] + x_in,
        scratch_shapes=x_scratch,
        compiler_params=_params("arbitrary" if exchange else "parallel"),
    )(*args, *x_args)
    return res if exchange else res[0]


SUB_ROWS = 256


def _whole_spec(w):
    return pl.BlockSpec(w.shape, lambda m: (0, 0, 0), pipeline_mode=pl.Buffered(1))


def _row_blocks(tm):
    sub = min(SUB_ROWS, tm)
    return [slice(r, r + sub) for r in range(0, tm, sub)]


def _sum_dots(a_ref, w_ref, dims, rows=slice(None)):
    acc = _dot(a_ref[0, rows], w_ref[0], dims)
    for j in range(1, a_ref.shape[0]):
        acc += _dot(a_ref[j, rows], w_ref[j], dims)
    return acc


def _rms(v):
    return lax.rsqrt(_rowmean(v * v) + EPS)


def _zero_at_start(*refs):
    @pl.when(pl.program_id(0) == 0)
    def _():
        for r in refs:
            r[...] = jnp.zeros_like(r)


def _postnorm_bwd_math(dxo, yv, g, gate, res_w, dgate_ref, dpost_ref):
    r = _rms(yv)
    yh = yv * r
    dr = dxo * res_w
    dgate_ref[...] += _colsum(dr * (yh * g))
    dp = dr * gate
    dpost_ref[...] += _colsum(dp * yh)
    dyh = dp * g
    return (r * (dyh - yh * _rowmean(dyh * yh))).astype(BF16)


def _out_proj(a, w, x, post_g, gate, res_w, nxt, name, *, tm=512):
    J, T, n = a.shape
    d = w.shape[2]
    tm = min(tm, T)

    def body(a_ref, w_ref, x_ref, pg_ref, gate_ref, ng_ref, nsc_ref, nsh_ref, y_ref, xn_ref, h_ref):
        for rows in _row_blocks(tm):
            y = _sum_dots(a_ref, w_ref, NN, rows)
            y_ref[rows] = y
            xn = x_ref[rows] + res_w * gate_ref[...] * (y * _rms(y) * pg_ref[...])
            xn_ref[rows] = xn
            h_ref[rows] = (xn * _rms(xn) * ng_ref[...] * (1.0 + nsc_ref[...]) + nsh_ref[...]).astype(BF16)

    return pl.pallas_call(
        body, name=name, grid=(T // tm,),
        out_shape=(jax.ShapeDtypeStruct((T, d), F32), jax.ShapeDtypeStruct((T, d), F32),
                   jax.ShapeDtypeStruct((T, d), BF16)),
        in_specs=[pl.BlockSpec((J, tm, n), lambda m: (0, m, 0)), _whole_spec(w),
                  _row_spec(tm, d)] + [_vec_spec(d)] * 5,
        out_specs=(_row_spec(tm, d),) * 3,
        compiler_params=_params("parallel"),
    )(a, w, x, post_g, gate, *nxt)


def _out_proj_last(a, w, x, post_g, gate, res_w, target, name, *, tm=512):
    J, T, n = a.shape
    d = w.shape[2]
    tm = min(tm, T)

    def body(a_ref, w_ref, x_ref, pg_ref, gate_ref, t_ref, dx_ref, dy_ref, dgate_ref, dpost_ref, l_ref):
        _zero_at_start(dgate_ref, dpost_ref, l_ref)
        for rows in _row_blocks(tm):
            y = _sum_dots(a_ref, w_ref, NN, rows)
            e = x_ref[rows] + res_w * gate_ref[...] * (y * _rms(y) * pg_ref[...]) - t_ref[rows]
            l_ref[...] += 0.5 * jnp.sum(_rowmean(e * e), axis=0, keepdims=True)
            dx = e * (1.0 / d)
            dx_ref[rows] = dx
            dy_ref[rows] = _postnorm_bwd_math(dx, y, pg_ref[...], gate_ref[...], res_w, dgate_ref, dpost_ref)

    return pl.pallas_call(
        body, name=name, grid=(T // tm,),
        out_shape=(jax.ShapeDtypeStruct((T, d), F32), jax.ShapeDtypeStruct((T, d), BF16),
                   jax.ShapeDtypeStruct((1, d), F32), jax.ShapeDtypeStruct((1, d), F32),
                   jax.ShapeDtypeStruct((1, 128), F32)),
        in_specs=[pl.BlockSpec((J, tm, n), lambda m: (0, m, 0)), _whole_spec(w),
                  _row_spec(tm, d), _vec_spec(d), _vec_spec(d), _row_spec(tm, d)],
        out_specs=(_row_spec(tm, d), _row_spec(tm, d), _vec_spec(d), _vec_spec(d),
                   pl.BlockSpec((1, 128), lambda m: (0, 0))),
        compiler_params=_params("arbitrary"),
    )(a, w, x, post_g, gate, target)


def _in_grad(dz, w, dxo, x, pre_g, scale, prev, name, *, tm=512, exchange=None):
    J, T, n = dz.shape
    d = w.shape[1]
    tm = min(tm, T)
    has_prev = prev is not None
    res_w = prev[3] if has_prev else None

    def body(*refs):
        dz_ref, w_ref, dxo_ref, x_ref, g_ref, sc_ref = refs[:6]
        if has_prev:
            yp_ref, ppg_ref, pgate_ref, dx_ref, dsh_ref, dsc_ref, dg_ref, dyp_ref, dgate_ref, dpost_ref = refs[6:]
            _zero_at_start(dsh_ref, dsc_ref, dg_ref, dgate_ref, dpost_ref)
        else:
            dx_ref, dsh_ref, dsc_ref, dg_ref = refs[6:]
            _zero_at_start(dsh_ref, dsc_ref, dg_ref)
        for rows in _row_blocks(tm):
            dh = _sum_dots(dz_ref, w_ref, NT, rows)
            xv = x_ref[rows]
            r = _rms(xv)
            xh = xv * r
            dsh_ref[...] += _colsum(dh)
            dsc_ref[...] += _colsum(dh * (xh * g_ref[...]))
            dn = dh * (1.0 + sc_ref[...])
            dg_ref[...] += _colsum(dn * xh)
            dxh = dn * g_ref[...]
            dx = dxo_ref[rows] + r * (dxh - xh * _rowmean(dxh * xh))
            dx_ref[rows] = dx
            if has_prev:
                dyp_ref[rows] = _postnorm_bwd_math(dx, yp_ref[rows], ppg_ref[...], pgate_ref[...], res_w,
                                                   dgate_ref, dpost_ref)

    vec = jax.ShapeDtypeStruct((1, d), F32)
    in_specs = [pl.BlockSpec((J, tm, n), lambda m: (0, m, 0)), _whole_spec(w),
                _row_spec(tm, d), _row_spec(tm, d), _vec_spec(d), _vec_spec(d)]
    out_shape = [jax.ShapeDtypeStruct((T, d), F32), vec, vec, vec]
    out_specs = [_row_spec(tm, d), _vec_spec(d), _vec_spec(d), _vec_spec(d)]
    args = [dz, w, dxo, x, pre_g, scale]
    if has_prev:
        in_specs += [_row_spec(tm, d), _vec_spec(d), _vec_spec(d)]
        out_shape += [jax.ShapeDtypeStruct((T, d), BF16), vec, vec]
        out_specs += [_row_spec(tm, d), _vec_spec(d), _vec_spec(d)]
        args += list(prev[:3])
    body, x_args, x_in, x_out, x_scratch = _host(exchange, len(args), len(out_shape), body, _first_last(T // tm))
    return pl.pallas_call(
        body, name=name, grid=(T // tm,),
        out_shape=out_shape + x_out, in_specs=in_specs + x_in, out_specs=out_specs + x_in,
        scratch_shapes=x_scratch,
        compiler_params=_params("arbitrary"),
    )(*args, *x_args)


def _mm_wgrad(xs, ys, name, *, tt=2048):
    Jx, T, P = xs.shape
    Jy, _, Q = ys.shape
    J = max(Jx, Jy)
    tt = min(tt, T)
    nt = T // tt

    def body(x_ref, y_ref, o_ref, acc_ref):
        t = pl.program_id(1)

        @pl.when(t == 0)
        def _():
            acc_ref[...] = jnp.zeros_like(acc_ref)

        acc_ref[...] += _dot(x_ref[...], y_ref[...], TN)

        @pl.when(t == nt - 1)
        def _():
            o_ref[...] = acc_ref[...].astype(BF16)

    return pl.pallas_call(
        body, name=name, grid=(J, nt),
        out_shape=jax.ShapeDtypeStruct((J, P, Q), BF16),
        in_specs=[pl.BlockSpec((None, tt, P), (lambda j, t: (j, t, 0)) if Jx > 1 else (lambda j, t: (0, t, 0))),
                  pl.BlockSpec((None, tt, Q), (lambda j, t: (j, t, 0)) if Jy > 1 else (lambda j, t: (0, t, 0)))],
        out_specs=pl.BlockSpec((None, P, Q), lambda j, t: (j, 0, 0)),
        scratch_shapes=[pltpu.VMEM((P, Q), F32)],
        compiler_params=_params("parallel", "arbitrary"),
    )(xs, ys)


def _ffn_in(h, wg, name, *, tm=512, exchange=None):
    T, K = h.shape
    n = wg.shape[-1]
    tm = min(tm, T)

    def body(h_ref, w_ref, ab_ref, g_ref):
        hh = h_ref[...]
        for j in range(4):
            a = _dot(hh, w_ref[j])
            b = _dot(hh, w_ref[4 + j])
            s = _sigmoid_t(a)
            silu = a * s
            ab_ref[j] = (b * (s * (1.0 + a * (1.0 - s)))).astype(BF16)
            ab_ref[4 + j] = silu.astype(BF16)
            g_ref[j] = (silu * b).astype(BF16)

    body, x_args, x_in, x_out, x_scratch = _host(exchange, 2, 2, body, _first_last(T // tm))
    res = pl.pallas_call(
        body, name=name, grid=(T // tm,),
        out_shape=[jax.ShapeDtypeStruct((8, T, n), BF16), jax.ShapeDtypeStruct((4, T, n), BF16)] + x_out,
        in_specs=[pl.BlockSpec((tm, K), lambda m: (m, 0)), _whole_spec(wg)] + x_in,
        out_specs=[pl.BlockSpec((8, tm, n), lambda m: (0, m, 0)), pl.BlockSpec((4, tm, n), lambda m: (0, m, 0))] + x_in,
        scratch_shapes=x_scratch,
        compiler_params=_params("arbitrary" if exchange else "parallel"),
    )(h, wg, *x_args)
    return res[0], res[1], res[2:]


def _ffn_dgate(dy, wo4, ab, name, *, tm=512):
    T, N = dy.shape
    n = wo4.shape[1]
    tm = min(tm, T)

    def body(dy_ref, w_ref, ab_ref, dab_ref):
        dyv = dy_ref[...]
        for j in range(4):
            dg = _dot(dyv, w_ref[j], NT)
            dab_ref[j] = (dg * ab_ref[j].astype(F32)).astype(BF16)
            dab_ref[4 + j] = (dg * ab_ref[4 + j].astype(F32)).astype(BF16)

    return pl.pallas_call(
        body, name=name, grid=(T // tm,),
        out_shape=jax.ShapeDtypeStruct((8, T, n), BF16),
        in_specs=[pl.BlockSpec((tm, N), lambda m: (m, 0)), _whole_spec(wo4),
                  pl.BlockSpec((8, tm, n), lambda m: (0, m, 0))],
        out_specs=pl.BlockSpec((8, tm, n), lambda m: (0, m, 0)),
        compiler_params=_params("parallel"),
    )(dy, wo4, ab)


def _row_spec(tm, d):
    return pl.BlockSpec((tm, d), lambda m: (m, 0))


def _vec_spec(d):
    return pl.BlockSpec((1, d), lambda m: (0, 0))


def _prenorm_fwd(x, pre_g, scale, shift, name, *, tm=512):
    T, d = x.shape
    tm = min(tm, T)

    def body(x_ref, g_ref, sc_ref, sh_ref, h_ref):
        xv = x_ref[...]
        r = lax.rsqrt(_rowmean(xv * xv) + EPS)
        h_ref[...] = (xv * r * g_ref[...] * (1.0 + sc_ref[...]) + sh_ref[...]).astype(BF16)

    return pl.pallas_call(
        body, name=name, grid=(T // tm,),
        out_shape=jax.ShapeDtypeStruct((T, d), BF16),
        in_specs=[_row_spec(tm, d), _vec_spec(d), _vec_spec(d), _vec_spec(d)],
        out_specs=_row_spec(tm, d),
        compiler_params=_params("parallel"),
    )(x, pre_g, scale, shift)


def _postnorm_fwd(x, y, post_g, gate, res_w, name, *, tm=512):
    T, d = x.shape
    tm = min(tm, T)

    def body(x_ref, y_ref, g_ref, gate_ref, o_ref):
        yv = y_ref[...]
        r = lax.rsqrt(_rowmean(yv * yv) + EPS)
        o_ref[...] = x_ref[...] + res_w * gate_ref[...] * (yv * r * g_ref[...])

    return pl.pallas_call(
        body, name=name, grid=(T // tm,),
        out_shape=jax.ShapeDtypeStruct((T, d), F32),
        in_specs=[_row_spec(tm, d), _row_spec(tm, d), _vec_spec(d), _vec_spec(d)],
        out_specs=_row_spec(tm, d),
        compiler_params=_params("parallel"),
    )(x, y, post_g, gate)


def _postnorm_bwd(dxo, y, post_g, gate, res_w, name, *, tm=512):
    T, d = y.shape
    tm = min(tm, T)

    def body(dxo_ref, y_ref, g_ref, gate_ref, dy_ref, dgate_ref, dg_ref):
        @pl.when(pl.program_id(0) == 0)
        def _():
            dgate_ref[...] = jnp.zeros_like(dgate_ref)
            dg_ref[...] = jnp.zeros_like(dg_ref)

        yv = y_ref[...]
        r = lax.rsqrt(_rowmean(yv * yv) + EPS)
        yh = yv * r
        dr = dxo_ref[...] * res_w
        dgate_ref[...] += _colsum(dr * (yh * g_ref[...]))
        dp = dr * gate_ref[...]
        dg_ref[...] += _colsum(dp * yh)
        dyh = dp * g_ref[...]
        dy_ref[...] = (r * (dyh - yh * _rowmean(dyh * yh))).astype(BF16)

    return pl.pallas_call(
        body, name=name, grid=(T // tm,),
        out_shape=(jax.ShapeDtypeStruct((T, d), BF16), jax.ShapeDtypeStruct((1, d), F32),
                   jax.ShapeDtypeStruct((1, d), F32)),
        in_specs=[_row_spec(tm, d), _row_spec(tm, d), _vec_spec(d), _vec_spec(d)],
        out_specs=(_row_spec(tm, d), _vec_spec(d), _vec_spec(d)),
        compiler_params=_params("arbitrary"),
    )(dxo, y, post_g, gate)


def _prenorm_bwd(dxo, dh, x, pre_g, scale, name, *, tm=512):
    T, d = x.shape
    tm = min(tm, T)

    def body(dxo_ref, dh_ref, x_ref, g_ref, sc_ref, dx_ref, dsh_ref, dsc_ref, dg_ref):
        @pl.when(pl.program_id(0) == 0)
        def _():
            dsh_ref[...] = jnp.zeros_like(dsh_ref)
            dsc_ref[...] = jnp.zeros_like(dsc_ref)
            dg_ref[...] = jnp.zeros_like(dg_ref)

        xv = x_ref[...]
        dhv = dh_ref[...].astype(F32)
        r = lax.rsqrt(_rowmean(xv * xv) + EPS)
        xh = xv * r
        dsh_ref[...] += _colsum(dhv)
        dsc_ref[...] += _colsum(dhv * (xh * g_ref[...]))
        dn = dhv * (1.0 + sc_ref[...])
        dg_ref[...] += _colsum(dn * xh)
        dxh = dn * g_ref[...]
        dx_ref[...] = dxo_ref[...] + r * (dxh - xh * _rowmean(dxh * xh))

    return pl.pallas_call(
        body, name=name, grid=(T // tm,),
        out_shape=(jax.ShapeDtypeStruct((T, d), F32),) + (jax.ShapeDtypeStruct((1, d), F32),) * 3,
        in_specs=[_row_spec(tm, d), _row_spec(tm, d), _row_spec(tm, d), _vec_spec(d), _vec_spec(d)],
        out_specs=(_row_spec(tm, d),) + (_vec_spec(d),) * 3,
        compiler_params=_params("arbitrary"),
    )(dxo, dh, x, pre_g, scale)


def _loss_head(y, target, name, *, tm=512):
    T, d = y.shape
    tm = min(tm, T)

    def body(y_ref, t_ref, dy_ref, l_ref):
        @pl.when(pl.program_id(0) == 0)
        def _():
            l_ref[...] = jnp.zeros_like(l_ref)

        e = y_ref[...] - t_ref[...]
        dy_ref[...] = e * (1.0 / d)
        l_ref[...] += 0.5 * jnp.sum(_rowmean(e * e), axis=0, keepdims=True)

    return pl.pallas_call(
        body, name=name, grid=(T // tm,),
        out_shape=(jax.ShapeDtypeStruct((T, d), F32), jax.ShapeDtypeStruct((1, 128), F32)),
        in_specs=[_row_spec(tm, d), _row_spec(tm, d)],
        out_specs=(_row_spec(tm, d), pl.BlockSpec((1, 128), lambda m: (0, 0))),
        compiler_params=_params("arbitrary"),
    )(y, target)


def _ada_fwd(c_all, w, b, name):
    L, K, n = w.shape

    def body(c_ref, w_ref, b_ref, o_ref):
        cv = c_ref[...]
        cond = cv * _sigmoid(cv)
        for l in range(L):
            o_ref[l] = _dot(cond, w_ref[l], precision=lax.Precision.HIGHEST) + b_ref[l]

    return pl.pallas_call(
        body, name=name,
        out_shape=jax.ShapeDtypeStruct((L, NDEV, n), F32),
        compiler_params=pltpu.CompilerParams(vmem_limit_bytes=VMEM_LIMIT),
    )(c_all, w, b)


def _ada_bwd(c_all_t, gmod, name):
    L, _, n = gmod.shape
    K = c_all_t.shape[0]

    def body(c_ref, g_ref, o_ref):
        cv = c_ref[...]
        cond = cv * _sigmoid(cv)
        for l in range(L):
            o_ref[l] = _dot(cond, g_ref[l], precision=lax.Precision.HIGHEST)

    return pl.pallas_call(
        body, name=name,
        out_shape=jax.ShapeDtypeStruct((L, K, n), F32),
        compiler_params=pltpu.CompilerParams(vmem_limit_bytes=VMEM_LIMIT),
    )(c_all_t, gmod)


def _tri(n, upper=False, block=None):
    r = lax.broadcasted_iota(jnp.int32, (n, n), 0)
    c = lax.broadcasted_iota(jnp.int32, (n, n), 1)
    m = (c >= r) if upper else (c <= r)
    if block is not None:
        m = m & ((r // block) == (c // block))
    return m.astype(F32)


def _hgrn_gates(proj_ref, lb_ref, jh):
    lb = lb_ref[:, 512 * jh:512 * (jh + 1)]
    qp = proj_ref[jh]
    fx = proj_ref[2 + jh]
    sq = _sigmoid(qp)
    sig = _sigmoid(fx)
    f = lb + (1.0 - lb) * sig
    k = (1.0 - lb) * (1.0 - sig)
    return lb, qp, sq, sig, f, k


def _hgrn_fwd(proj, lb, out_norm, name, *, tb=128):
    T = proj.shape[1]
    tb = min(tb, T)
    nc = tb // HG_CHUNK
    lmat = _tri(tb, block=HG_CHUNK)

    def body(proj_ref, lb_ref, on_ref, l_ref, o_ref, og_ref, st_ref, s_scr, b_scr):
        @pl.when(pl.program_id(0) == 0)
        def _():
            s_scr[...] = jnp.zeros_like(s_scr)

        r_i = lax.broadcasted_iota(jnp.int32, (HG_CHUNK, HG_CHUNK), 0)
        c_i = lax.broadcasted_iota(jnp.int32, (HG_CHUNK, HG_CHUNK), 1)
        causal = c_i <= r_i
        onv = on_ref[...]
        for jh in range(2):
            lbv, qp, sq, sig, f, k = _hgrn_gates(proj_ref, lb_ref, jh)
            q = qp * sq
            b_scr[...] = _dot(l_ref[...], jnp.log(f), precision=lax.Precision.HIGHEST)
            v = proj_ref[4 + jh]
            gp = proj_ref[6 + jh]
            gs = gp * _sigmoid(gp)
            for hh in range(4):
                hd = 4 * jh + hh
                cs = slice(HG_HEAD * hh, HG_HEAD * (hh + 1))
                for ci in range(nc):
                    r0 = HG_CHUNK * ci
                    rs = slice(r0, r0 + HG_CHUNK)
                    bc = b_scr[rs, cs]
                    bm = b_scr[r0 + HG_CHUNK // 2 - 1:r0 + HG_CHUNK // 2, cs]
                    bl = b_scr[r0 + HG_CHUNK - 1:r0 + HG_CHUNK, cs]
                    qc, kc, vc = q[rs, cs], k[rs, cs], v[rs, cs].astype(BF16)
                    qe = (qc * jnp.exp(bc)).astype(BF16)
                    qt = (qc * jnp.exp(bc - bm)).astype(BF16)
                    kt = (kc * jnp.exp(bm - bc)).astype(BF16)
                    kd = (kc * jnp.exp(bl - bc)).astype(BF16)
                    st = s_scr[hd]
                    stb = st.astype(BF16)
                    st_ref[ci, hd] = stb
                    a = jnp.where(causal, _dot(qt, kt, NT), 0.0).astype(BF16)
                    o = _dot(qe, stb, NT) + _dot(a, vc)
                    s_scr[hd] = st * jnp.exp(bl) + _dot(vc, kd, TN)
                    o_ref[rs, HG_HEAD * hd:HG_HEAD * (hd + 1)] = o
                    r = lax.rsqrt(_rowmean(o * o) + EPS)
                    og_ref[rs, HG_HEAD * hd:HG_HEAD * (hd + 1)] = (o * r * onv * gs[rs, cs]).astype(BF16)

    return pl.pallas_call(
        body, name=name, grid=(T // tb,),
        out_shape=(jax.ShapeDtypeStruct((T, D_MODEL), F32), jax.ShapeDtypeStruct((T, D_MODEL), BF16),
                   jax.ShapeDtypeStruct((T // HG_CHUNK, HG_HEADS, HG_HEAD, HG_HEAD), BF16)),
        in_specs=[pl.BlockSpec((8, tb, 512), lambda i: (0, i, 0)),
                  pl.BlockSpec((1, D_MODEL), lambda i: (0, 0)),
                  pl.BlockSpec((1, HG_HEAD), lambda i: (0, 0)),
                  pl.BlockSpec((tb, tb), lambda i: (0, 0))],
        out_specs=(pl.BlockSpec((tb, D_MODEL), lambda i: (i, 0)),
                   pl.BlockSpec((tb, D_MODEL), lambda i: (i, 0)),
                   pl.BlockSpec((nc, HG_HEADS, HG_HEAD, HG_HEAD), lambda i: (i, 0, 0, 0))),
        scratch_shapes=[pltpu.VMEM((HG_HEADS, HG_HEAD, HG_HEAD), F32), pltpu.VMEM((tb, 512), F32)],
        compiler_params=_params("arbitrary"),
    )(proj, lb, out_norm, lmat)


def _hgrn_bwd(proj, o, dog, states, lb, out_norm, name, *, tb=128):
    T = proj.shape[1]
    tb = min(tb, T)
    nc = tb // HG_CHUNK
    nb = T // tb
    lmat = _tri(tb, block=HG_CHUNK)
    umat = _tri(tb, upper=True, block=HG_CHUNK)

    def body(proj_ref, o_ref, dog_ref, st_ref, lb_ref, on_ref, l_ref, u_ref,
             dproj_ref, dlb_ref, don_ref, ds_scr, b_scr, dq_scr, dk_scr, dv_scr, dg_scr, db_scr):
        @pl.when(pl.program_id(0) == 0)
        def _():
            ds_scr[...] = jnp.zeros_like(ds_scr)
            dlb_ref[...] = jnp.zeros_like(dlb_ref)
            don_ref[...] = jnp.zeros_like(don_ref)

        r_i = lax.broadcasted_iota(jnp.int32, (HG_CHUNK, HG_CHUNK), 0)
        c_i = lax.broadcasted_iota(jnp.int32, (HG_CHUNK, HG_CHUNK), 1)
        causal = c_i <= r_i
        causal_t = r_i <= c_i
        last_row = lax.broadcasted_iota(jnp.int32, (HG_CHUNK, HG_HEAD), 0) == HG_CHUNK - 1
        onv = on_ref[...]
        don_acc = jnp.zeros((1, HG_HEAD), F32)
        for jh in range(2):
            lbv, qp, sq, sig, f, k = _hgrn_gates(proj_ref, lb_ref, jh)
            q = qp * sq
            b_scr[...] = _dot(l_ref[...], jnp.log(f), precision=lax.Precision.HIGHEST)
            v = proj_ref[4 + jh]
            gp = proj_ref[6 + jh]
            sg = _sigmoid(gp)
            for hh in range(4):
                hd = 4 * jh + hh
                cs = slice(HG_HEAD * hh, HG_HEAD * (hh + 1))
                hs = slice(HG_HEAD * hd, HG_HEAD * (hd + 1))
                for ci in reversed(range(nc)):
                    r0 = HG_CHUNK * ci
                    rs = slice(r0, r0 + HG_CHUNK)
                    oc = o_ref[rs, hs]
                    r = lax.rsqrt(_rowmean(oc * oc) + EPS)
                    oh = oc * r
                    gc, sgc = gp[rs, cs], sg[rs, cs]
                    dogc = dog_ref[rs, hs].astype(F32)
                    don = dogc * (gc * sgc)
                    dg_scr[rs, cs] = dogc * (oh * onv) * (sgc * (1.0 + gc * (1.0 - sgc)))
                    don_acc += _colsum(don * oh)
                    donh = don * onv
                    do = (r * (donh - oh * _rowmean(donh * oh))).astype(BF16)
                    bc = b_scr[rs, cs]
                    bm = b_scr[r0 + HG_CHUNK // 2 - 1:r0 + HG_CHUNK // 2, cs]
                    bl = b_scr[r0 + HG_CHUNK - 1:r0 + HG_CHUNK, cs]
                    qc, kc, vc = q[rs, cs], k[rs, cs], v[rs, cs].astype(BF16)
                    e_b, e_q, e_k, e_d = jnp.exp(bc), jnp.exp(bc - bm), jnp.exp(bm - bc), jnp.exp(bl - bc)
                    qe = (qc * e_b).astype(BF16)
                    qt = (qc * e_q).astype(BF16)
                    kt = (kc * e_k).astype(BF16)
                    kd = (kc * e_d).astype(BF16)
                    stb = st_ref[ci, hd]
                    dst = ds_scr[hd]
                    dstb = dst.astype(BF16)
                    a_t = jnp.where(causal_t, _dot(kt, qt, NT), 0.0).astype(BF16)
                    da = jnp.where(causal, _dot(do, vc, NT), 0.0).astype(BF16)
                    da_t = jnp.where(causal_t, _dot(vc, do, NT), 0.0).astype(BF16)
                    dv_scr[rs, cs] = _dot(a_t, do) + _dot(kd, dstb, NT)
                    dqe, dqt = _dot(do, stb), _dot(da, kt)
                    dkt, dkd = _dot(da_t, qt), _dot(vc, dstb)
                    dq_scr[rs, cs] = dqe * e_b + dqt * e_q
                    dk_scr[rs, cs] = dkt * e_k + dkd * e_d
                    e_l = jnp.exp(bl)
                    s_end = stb.astype(F32) * e_l + _dot(vc, kd, TN)
                    dbc = (qe.astype(F32) * dqe + qt.astype(F32) * dqt
                           - kt.astype(F32) * dkt - kd.astype(F32) * dkd)
                    db_scr[rs, cs] = dbc + jnp.where(last_row, _colsum(dstb.astype(F32) * s_end), 0.0)
                    ds_scr[hd] = dst * e_l + _dot(do, qe, TN)
            dq = dq_scr[...]
            dk = dk_scr[...]
            cols = slice(512 * jh, 512 * (jh + 1))
            dlogf = _dot(u_ref[...], db_scr[...], precision=lax.Precision.HIGHEST)
            one_m_sig = 1.0 - sig
            dsig = (1.0 - lbv) * sig * one_m_sig
            dproj_ref[jh] = (dq * (sq * (1.0 + qp * (1.0 - sq)))).astype(BF16)
            dproj_ref[2 + jh] = (dlogf * dsig / f - dk * dsig).astype(BF16)
            dproj_ref[4 + jh] = dv_scr[...].astype(BF16)
            dproj_ref[6 + jh] = dg_scr[...].astype(BF16)
            dlb_ref[:, cols] += _colsum(dlogf * one_m_sig / f - dk * one_m_sig)
        don_ref[...] += don_acc

    rev = lambda i: nb - 1 - i
    return pl.pallas_call(
        body, name=name, grid=(nb,),
        out_shape=(jax.ShapeDtypeStruct((8, T, 512), BF16), jax.ShapeDtypeStruct((1, D_MODEL), F32),
                   jax.ShapeDtypeStruct((1, HG_HEAD), F32)),
        in_specs=[pl.BlockSpec((8, tb, 512), lambda i: (0, rev(i), 0)),
                  pl.BlockSpec((tb, D_MODEL), lambda i: (rev(i), 0)),
                  pl.BlockSpec((tb, D_MODEL), lambda i: (rev(i), 0)),
                  pl.BlockSpec((nc, HG_HEADS, HG_HEAD, HG_HEAD), lambda i: (rev(i), 0, 0, 0)),
                  pl.BlockSpec((1, D_MODEL), lambda i: (0, 0)),
                  pl.BlockSpec((1, HG_HEAD), lambda i: (0, 0)),
                  pl.BlockSpec((tb, tb), lambda i: (0, 0)),
                  pl.BlockSpec((tb, tb), lambda i: (0, 0))],
        out_specs=(pl.BlockSpec((8, tb, 512), lambda i: (0, rev(i), 0)),
                   pl.BlockSpec((1, D_MODEL), lambda i: (0, 0)),
                   pl.BlockSpec((1, HG_HEAD), lambda i: (0, 0))),
        scratch_shapes=[pltpu.VMEM((HG_HEADS, HG_HEAD, HG_HEAD), F32)] + [pltpu.VMEM((tb, 512), F32)] * 6,
        compiler_params=_params("arbitrary"),
    )(proj, o, dog, states, lb, out_norm, lmat, umat)


def _gm_norm(pre_ref, lg_ref, lbias_ref):
    vs = [_gelu(pre_ref[4 + j].astype(F32)) for j in range(4)]
    width = 4 * vs[0].shape[1]
    mu = sum(jnp.sum(v, axis=1, keepdims=True) for v in vs) / width
    ds = [v - mu for v in vs]
    var = sum(jnp.sum(d * d, axis=1, keepdims=True) for d in ds) / width
    rstd = lax.rsqrt(var + EPS)
    vhat = [d * rstd for d in ds]
    vn = [vhat[j] * lg_ref[j:j + 1, :] + lbias_ref[j:j + 1, :] for j in range(4)]
    return vhat, vn, rstd


def _gm_spatial_fwd(pre, ln_g, ln_b, ws, bsb, name, *, tb=256):
    T = pre.shape[1]
    tb = min(tb, T)
    nc = tb // GM_CHUNK

    def body(pre_ref, lg_ref, lbias_ref, ws_ref, bs_ref, o_ref):
        _, vn, _ = _gm_norm(pre_ref, lg_ref, lbias_ref)
        for j in range(4):
            u = _gelu(pre_ref[j].astype(F32))
            for e in range(2):
                g = 2 * j + e
                cs = slice(GM_GDIM * e, GM_GDIM * (e + 1))
                wg = ws_ref[g].astype(BF16)
                for ci in range(nc):
                    rs = slice(GM_CHUNK * ci, GM_CHUNK * (ci + 1))
                    vm = _dot(wg, vn[j][rs, cs].astype(BF16)) + bs_ref[g]
                    o_ref[j, rs, cs] = (u[rs, cs] * vm).astype(BF16)

    return pl.pallas_call(
        body, name=name, grid=(T // tb,),
        out_shape=jax.ShapeDtypeStruct((4, T, 768), BF16),
        in_specs=[pl.BlockSpec((8, tb, 768), lambda i: (0, i, 0)),
                  pl.BlockSpec((4, 768), lambda i: (0, 0)),
                  pl.BlockSpec((4, 768), lambda i: (0, 0)),
                  pl.BlockSpec((GM_GROUPS, GM_CHUNK, GM_CHUNK), lambda i: (0, 0, 0)),
                  pl.BlockSpec((GM_GROUPS, GM_CHUNK, GM_GDIM), lambda i: (0, 0, 0))],
        out_specs=pl.BlockSpec((4, tb, 768), lambda i: (0, i, 0)),
        compiler_params=_params("parallel"),
    )(pre, ln_g, ln_b, ws, bsb)


def _gm_spatial_bwd(pre, dm, ln_g, ln_b, ws, ws_t, bsb, name, *, tb=256):
    T = pre.shape[1]
    tb = min(tb, T)
    nc = tb // GM_CHUNK
    nb = T // tb

    def body(pre_ref, dm_ref, lg_ref, lbias_ref, ws_ref, wst_ref, bs_ref,
             dpre_ref, dws_ref, dbs_ref, dlg_ref, dlb_ref, dbin_ref, dbs_scr, dvn_scr, du_scr):
        i = pl.program_id(0)

        @pl.when(i == 0)
        def _():
            dws_ref[...] = jnp.zeros_like(dws_ref)
            dbs_scr[...] = jnp.zeros_like(dbs_scr)
            dlg_ref[...] = jnp.zeros_like(dlg_ref)
            dlb_ref[...] = jnp.zeros_like(dlb_ref)
            dbin_ref[...] = jnp.zeros_like(dbin_ref)

        vhat, vn, rstd = _gm_norm(pre_ref, lg_ref, lbias_ref)
        for j in range(4):
            u = _gelu(pre_ref[j].astype(F32))
            for e in range(2):
                g = 2 * j + e
                cs = slice(GM_GDIM * e, GM_GDIM * (e + 1))
                wg = ws_ref[g].astype(BF16)
                wgt = wst_ref[g].astype(BF16)
                for ci in range(nc):
                    rs = slice(GM_CHUNK * ci, GM_CHUNK * (ci + 1))
                    vnb = vn[j][rs, cs].astype(BF16)
                    vm = _dot(wg, vnb) + bs_ref[g]
                    dmg = dm_ref[j, rs, cs].astype(F32)
                    du_scr[j, rs, cs] = dmg * vm
                    dvm = dmg * u[rs, cs]
                    dvmb = dvm.astype(BF16)
                    dws_ref[g] += _dot(dvmb, vnb, NT)
                    dbs_scr[g] += dvm
                    dvn_scr[j, rs, cs] = _dot(wgt, dvmb)
        width = 4 * 768
        dvh = []
        for j in range(4):
            dvn = dvn_scr[j]
            dlg_ref[j:j + 1, :] += _colsum(dvn * vhat[j])
            dlb_ref[j:j + 1, :] += _colsum(dvn)
            dvh.append(dvn * lg_ref[j:j + 1, :])
        m1 = sum(jnp.sum(d, axis=1, keepdims=True) for d in dvh) / width
        m2 = sum(jnp.sum(dvh[j] * vhat[j], axis=1, keepdims=True) for j in range(4)) / width
        for j in range(4):
            dv = rstd * (dvh[j] - m1 - vhat[j] * m2)
            dpv = dv * _gelu_grad(pre_ref[4 + j].astype(F32))
            dpu = du_scr[j] * _gelu_grad(pre_ref[j].astype(F32))
            dpre_ref[4 + j] = dpv.astype(BF16)
            dpre_ref[j] = dpu.astype(BF16)
            dbin_ref[4 + j:5 + j, :] += _colsum(dpv)
            dbin_ref[j:j + 1, :] += _colsum(dpu)

        @pl.when(i == nb - 1)
        def _():
            r_i = lax.broadcasted_iota(jnp.int32, (GM_CHUNK, GM_CHUNK), 0)
            c_i = lax.broadcasted_iota(jnp.int32, (GM_CHUNK, GM_CHUNK), 1)
            for g in range(GM_GROUPS):
                dws_ref[g] = jnp.where(c_i <= r_i, dws_ref[g], 0.0)
                dbs_ref[g] = jnp.broadcast_to(jnp.sum(dbs_scr[g], axis=1, keepdims=True), (GM_CHUNK, GM_CHUNK))

    sq = pl.BlockSpec((GM_GROUPS, GM_CHUNK, GM_CHUNK), lambda i: (0, 0, 0))
    v4 = pl.BlockSpec((4, 768), lambda i: (0, 0))
    return pl.pallas_call(
        body, name=name, grid=(nb,),
        out_shape=(jax.ShapeDtypeStruct((8, T, 768), BF16),
                   jax.ShapeDtypeStruct((GM_GROUPS, GM_CHUNK, GM_CHUNK), F32),
                   jax.ShapeDtypeStruct((GM_GROUPS, GM_CHUNK, GM_CHUNK), F32),
                   jax.ShapeDtypeStruct((4, 768), F32), jax.ShapeDtypeStruct((4, 768), F32),
                   jax.ShapeDtypeStruct((8, 768), F32)),
        in_specs=[pl.BlockSpec((8, tb, 768), lambda i: (0, i, 0)),
                  pl.BlockSpec((4, tb, 768), lambda i: (0, i, 0)),
                  v4, v4, sq, sq,
                  pl.BlockSpec((GM_GROUPS, GM_CHUNK, GM_GDIM), lambda i: (0, 0, 0))],
        out_specs=(pl.BlockSpec((8, tb, 768), lambda i: (0, i, 0)), sq, sq, v4, v4,
                   pl.BlockSpec((8, 768), lambda i: (0, 0))),
        scratch_shapes=[pltpu.VMEM((GM_GROUPS, GM_CHUNK, GM_GDIM), F32),
                        pltpu.VMEM((4, tb, 768), F32), pltpu.VMEM((4, tb, 768), F32)],
        compiler_params=_params("arbitrary"),
    )(pre, dm, ln_g, ln_b, ws, ws_t, bsb)


def _adamw(slots, w, m, v, name, *, tr=256):
    S, R, C = slots.shape
    tr = next((t for t in (tr, tr // 2, tr // 4, tr // 8, tr // 16) if R % t == 0), R) if R > tr else R
    bc1 = 1.0 - ADAM_B1 ** ADAM_STEP
    bc2 = 1.0 - ADAM_B2 ** ADAM_STEP

    def body(s_ref, w_ref, m_ref, v_ref, g_ref, d_ref, nm_ref, nv_ref):
        g = s_ref[0].astype(F32)
        for s in range(1, S):
            g = g + s_ref[s].astype(F32)
        mn = ADAM_B1 * m_ref[...] + (1.0 - ADAM_B1) * g
        vn = ADAM_B2 * v_ref[...] + (1.0 - ADAM_B2) * (g * g)
        g_ref[...] = g
        nm_ref[...] = mn
        nv_ref[...] = vn
        d_ref[...] = -ADAM_LR * ((mn / bc1) / (jnp.sqrt(vn / bc2) + ADAM_EPS) + ADAM_WD * w_ref[...])

    spec = pl.BlockSpec((tr, C), lambda i: (i, 0))
    return pl.pallas_call(
        body, name=name, grid=(R // tr,),
        out_shape=(jax.ShapeDtypeStruct((R, C), F32),) * 4,
        in_specs=[pl.BlockSpec((S, tr, C), lambda i: (0, i, 0)), spec, spec, spec],
        out_specs=(spec,) * 4,
        compiler_params=_params("parallel"),
    )(slots, w, m, v)


def _update(slots, w, m, v, name):
    shp = w.shape
    C = shp[-1]
    R = math.prod(shp[:-1])
    outs = _adamw(slots.reshape(slots.shape[0], R, C), w.reshape(R, C), m.reshape(R, C), v.reshape(R, C), name)
    return tuple(o.reshape(shp) for o in outs)


def kernel(x, c, ada_w, ada_b, norm_pre, norm_post, ffn_w_in, ffn_w_out, hg_w_in, hg_w_out, hg_out_norm, hg_lb, gm_w_in, gm_b_in, gm_ln_g, gm_ln_b, gm_w_s, gm_b_s, gm_w_out, loss_target, m_ada_w, m_ada_b, m_norm_pre, m_norm_post, m_ffn_w_in, m_ffn_w_out, m_hg_w_in, m_hg_w_out, m_hg_out_norm, m_hg_lb, m_gm_w_in, m_gm_b_in, m_gm_ln_g, m_gm_ln_b, m_gm_w_s, m_gm_b_s, m_gm_w_out, v_ada_w, v_ada_b, v_norm_pre, v_norm_post, v_ffn_w_in, v_ffn_w_out, v_hg_w_in, v_hg_w_out, v_hg_out_norm, v_hg_lb, v_gm_w_in, v_gm_b_in, v_gm_ln_g, v_gm_ln_b, v_gm_w_s, v_gm_b_s, v_gm_w_out):
    me = 4 * lax.axis_index("x") + 2 * lax.axis_index("y") + lax.axis_index("c")
    T = x.shape[1]
    x0 = x.reshape(T, D_MODEL)
    target = loss_target.reshape(T, D_MODEL)
    n_ada = ada_w.shape[-1]

    pack = jnp.concatenate([
        c.reshape(8, 128), norm_pre.reshape(6, 128), norm_post.reshape(6, 128),
        gm_b_in.reshape(6, 128), gm_ln_g.reshape(3, 128), gm_ln_b.reshape(3, 128)], axis=0)
    packs = _all_gather(pack, "gather_small")
    c_all = packs[:, 0:8].reshape(NDEV, D_MODEL)
    npre = packs[:, 8:14].reshape(NDEV, 2, 3, 128).transpose(1, 2, 0, 3).reshape(2, 3, D_MODEL)
    npost = packs[:, 14:20].reshape(NDEV, 2, 3, 128).transpose(1, 2, 0, 3).reshape(2, 3, D_MODEL)
    b_in = packs[:, 20:26].reshape(NDEV, 1, 768)
    ln_g = packs[:, 26:29].reshape(4, 768)
    ln_b = packs[:, 29:32].reshape(4, 768)

    ada_b_mine = lax.dynamic_slice_in_dim(ada_b, me * n_ada, n_ada, axis=1).reshape(2, 1, n_ada)
    mod_cols = _ada_fwd(c_all, ada_w, ada_b_mine, "ada_fwd")
    mod_all = _all_gather(mod_cols, "gather_mod")
    mod = lax.dynamic_index_in_dim(mod_all, me, axis=2, keepdims=False)
    mod = mod.transpose(1, 0, 2).reshape(2, 9, 1, D_MODEL)

    sh_fi, sh_fo = ffn_w_in.astype(BF16), ffn_w_out.astype(BF16)
    sh_hi, sh_ho = hg_w_in[0].astype(BF16), hg_w_out[0].astype(BF16)
    sh_mi, sh_mo = gm_w_in[0].astype(BF16), gm_w_out[0].astype(BF16)
    n_ff = sh_fi.shape[-1]
    w_fi = {(0, 0): _all_gather(sh_fi[0, 0], "gather_ffn_in_first")}
    w_fo = {(0, 0): _all_gather(sh_fo[0, 0], "gather_ffn_out_first")}
    riders = {"l0s0": [sh_hi, sh_ho, sh_fi[0, 1], sh_fo[0, 1]], "l0s1": [sh_mi, sh_mo],
              "l0s2": [sh_fi[1, 0], sh_fo[1, 0]], "l1s0": [sh_fi[1, 1], sh_fo[1, 1]]}

    sm = jax.nn.softmax(hg_lb, axis=0)
    lb0 = sm[0:1]
    on = hg_out_norm.reshape(1, HG_HEAD)
    tril = jnp.tril(jnp.ones((GM_CHUNK, GM_CHUNK), F32))
    ws = gm_w_s[0] * tril[None]
    ws_t = ws.transpose(0, 2, 1)
    bsb = jnp.broadcast_to(gm_b_s[0][:, :, None], (GM_GROUPS, GM_CHUNK, GM_GDIM))

    res_ws = (0.5, 1.0, 0.5)

    def vecs(i, s):
        return (npre[i, s].reshape(1, D_MODEL), npost[i, s].reshape(1, D_MODEL),
                mod[i, 3 * s], mod[i, 3 * s + 1], mod[i, 3 * s + 2])

    def ffn_weights(i, f):
        return w_fi[i, f], w_fo[i, f].reshape(4, n_ff, D_MODEL)

    order = [(i, s) for i in range(2) for s in range(3)]
    saved = {}
    xs = x0
    pre_g, _, shift, scale, _ = vecs(0, 0)
    h = _prenorm_fwd(xs, pre_g, scale, shift, "prenorm_l0s0")
    for pos, (i, s) in enumerate(order):
        tag = f"l{i}s{s}"
        _, post_g, _, _, gate = vecs(i, s)
        rider = _Exchange("gather", riders[tag]) if tag in riders else None
        if s != 1:
            wg, wo = ffn_weights(i, s // 2)
            ab, a, got = _ffn_in(h, wg, "ffn_in_" + tag, exchange=rider)
            extra = (ab, a)
            if tag == "l0s0":
                w_hi, w_ho, w_fi[0, 1], w_fo[0, 1] = got
                w_ho = w_ho.reshape(1, D_MODEL, D_MODEL)
            elif tag == "l0s2":
                w_fi[1, 0], w_fo[1, 0] = got
            elif tag == "l1s0":
                w_fi[1, 1], w_fo[1, 1] = got
        elif i == 0:
            proj, w_mi, w_mo = _mm_blocks(h, w_hi, "hg_in", out_dtype=F32, exchange=rider)
            w_mo = w_mo.reshape(4, 768, D_MODEL)
            o, og, states = _hgrn_fwd(proj, lb0, on, "hg_mix")
            a, wo = og.reshape(1, T, D_MODEL), w_ho
            extra = (proj, o, og, states)
        else:
            pre = _mm_blocks(h, w_mi, "gm_in", bias=b_in)
            a = _gm_spatial_fwd(pre, ln_g, ln_b, ws, bsb, "gm_mix")
            wo = w_mo
            extra = (pre, a)
        if pos + 1 < len(order):
            npre_g, _, nshift, nscale, _ = vecs(*order[pos + 1])
            y, x_next, h_next = _out_proj(a, wo, xs, post_g, gate, res_ws[s], (npre_g, nscale, nshift), "out_" + tag)
            saved[tag] = (xs, h, y) + extra
            xs, h = x_next, h_next
        else:
            dx, dy, dgate, dpost, loss_part = _out_proj_last(a, wo, xs, post_g, gate, res_ws[s], target, "out_" + tag)
            saved[tag] = (xs, h, None) + extra
    loss = lax.psum(loss_part[0, 0], ("x", "y", "c"))

    slots = {}
    d_npre = [[None] * 3, [None] * 3]
    d_npost = [[None] * 3, [None] * 3]
    d_mod = [[None] * 9, [None] * 9]
    for pos in reversed(range(len(order))):
        i, s = order[pos]
        tag = f"l{i}s{s}"
        pre_g, _, _, scale, _ = vecs(i, s)
        xin, h = saved[tag][:2]
        dy1 = dy.reshape(1, T, D_MODEL)
        if s != 1:
            w_in, wo = ffn_weights(i, s // 2)
            ab, g = saved[tag][3:]
            dz = _ffn_dgate(dy, wo, ab, "ffn_dgate_" + tag)
            g_out = _mm_wgrad(g, dy1, "ffn_out_wgrad_" + tag).reshape(NDEV, n_ff // 2, D_MODEL)
            g_in = _mm_wgrad(h.reshape(1, T, D_MODEL), dz, "ffn_in_wgrad_" + tag)
        elif i == 0:
            proj, o, og, states = saved[tag][3:]
            dog = _mm_blocks(dy, w_ho, "hg_out_dgrad", transpose_w=True)[0]
            g_out = _mm_wgrad(og.reshape(1, T, D_MODEL), dy1, "hg_out_wgrad").reshape(NDEV, 128, D_MODEL)
            dz, d_lb0, d_on = _hgrn_bwd(proj, o, dog, states, lb0, on, "hg_mix_bwd")
            w_in = w_hi
            g_in = _mm_wgrad(h.reshape(1, T, D_MODEL), dz, "hg_in_wgrad")
        else:
            pre, sp = saved[tag][3:]
            dm = _mm_blocks(dy, w_mo, "gm_out_dgrad", transpose_w=True)
            g_out = _mm_wgrad(sp, dy1, "gm_out_wgrad").reshape(NDEV, 384, D_MODEL)
            dz, d_ws, d_bs, d_lg, d_lbias, d_bin = _gm_spatial_bwd(pre, dm, ln_g, ln_b, ws, ws_t, bsb, "gm_mix_bwd")
            w_in = w_mi
            g_in = _mm_wgrad(h.reshape(1, T, D_MODEL), dz, "gm_in_wgrad")
        d_npost[i][s] = dpost
        d_mod[i][3 * s + 2] = dgate
        rider = _Exchange("scatter", [g_in, g_out])
        if pos > 0:
            pi, ps = order[pos - 1]
            _, ppost_g, _, _, pgate = vecs(pi, ps)
            prev = (saved[f"l{pi}s{ps}"][2], ppost_g, pgate, res_ws[ps])
            dx, dshift, dscale, dpre_g, dy, dgate, dpost, r_in, r_out = _in_grad(
                dz, w_in, dx, xin, pre_g, scale, prev, "in_grad_" + tag, exchange=rider)
        else:
            dx, dshift, dscale, dpre_g, r_in, r_out = _in_grad(
                dz, w_in, dx, xin, pre_g, scale, None, "in_grad_" + tag, exchange=rider)
        slots[tag] = (r_in, r_out)
        d_npre[i][s] = dpre_g
        d_mod[i][3 * s], d_mod[i][3 * s + 1] = dshift, dscale
    grad_x = dx.reshape(x.shape)

    ffn_tags = ["l0s0", "l0s2", "l1s0", "l1s2"]
    s_fi = jnp.stack([slots[t][0] for t in ffn_tags], axis=1)
    s_fo = jnp.stack([slots[t][1] for t in ffn_tags], axis=1)
    (s_hi, s_ho), (s_mi, s_mo) = slots["l0s1"], slots["l1s1"]
    s_hi, s_ho, s_mi, s_mo = s_hi[:, None], s_ho[:, None], s_mi[:, None], s_mo[:, None]

    gmod = jnp.stack([jnp.concatenate(d_mod[i], axis=0) for i in range(2)])
    d_sm = lb0 * d_lb0
    d_hg_lb = jnp.concatenate([d_sm, jnp.zeros((2, D_MODEL), F32)], axis=0) - sm * d_sm
    small = [gmod, jnp.stack([jnp.concatenate(r, axis=0) for r in d_npre]),
             jnp.stack([jnp.concatenate(r, axis=0) for r in d_npost]),
             d_on, d_hg_lb, d_bin, d_lg, d_lbias, d_ws, d_bs[:, :, 0]]
    sizes = [a.size for a in small]
    flat = jnp.concatenate([a.reshape(-1) for a in small])
    rows = -(-flat.size // (8 * 128)) * 8
    flat = jnp.pad(flat, (0, rows * 128 - flat.size)).reshape(rows, 128)
    flats = _all_gather(flat, "gather_small_grads").reshape(NDEV, rows * 128)
    parts, off = [], 0
    for a, n in zip(small, sizes):
        parts.append(flats[:, off:off + n].reshape((NDEV,) + a.shape))
        off += n
    p_mod, p_npre, p_npost, p_on, p_lb, p_bin, p_lg, p_lbias, p_ws, p_bs = parts

    def mine(p, width):
        return lax.dynamic_slice_in_dim(p, me * width, width, axis=p.ndim - 1)

    gmod_cols = mine(p_mod.reshape(NDEV, 2, 9 * D_MODEL), n_ada).transpose(1, 0, 2)
    g_ada_w = _ada_bwd(jnp.pad(c_all.T, ((0, 0), (0, 120))), jnp.pad(gmod_cols, ((0, 0), (0, 120), (0, 0))), "ada_bwd")

    out = {}
    out["ada_w"] = _update(g_ada_w[None], ada_w, m_ada_w, v_ada_w, "adamw_ada_w")
    out["ada_b"] = _update(p_mod.reshape(NDEV, 2, 9 * D_MODEL), ada_b, m_ada_b, v_ada_b, "adamw_ada_b")
    out["norm_pre"] = _update(mine(p_npre, 128), norm_pre, m_norm_pre, v_norm_pre, "adamw_norm_pre")
    out["norm_post"] = _update(mine(p_npost, 128), norm_post, m_norm_post, v_norm_post, "adamw_norm_post")
    out["ffn_w_in"] = _update(s_fi.reshape((NDEV,) + ffn_w_in.shape), ffn_w_in, m_ffn_w_in, v_ffn_w_in, "adamw_ffn_in")
    out["ffn_w_out"] = _update(s_fo.reshape((NDEV,) + ffn_w_out.shape), ffn_w_out, m_ffn_w_out, v_ffn_w_out, "adamw_ffn_out")
    out["hg_w_in"] = _update(s_hi, hg_w_in, m_hg_w_in, v_hg_w_in, "adamw_hg_in")
    out["hg_w_out"] = _update(s_ho, hg_w_out, m_hg_w_out, v_hg_w_out, "adamw_hg_out")
    out["hg_out_norm"] = _update(p_on, hg_out_norm, m_hg_out_norm, v_hg_out_norm, "adamw_hg_norm")
    out["hg_lb"] = _update(p_lb, hg_lb, m_hg_lb, v_hg_lb, "adamw_hg_lb")
    out["gm_w_in"] = _update(s_mi, gm_w_in, m_gm_w_in, v_gm_w_in, "adamw_gm_in")
    out["gm_b_in"] = _update(mine(p_bin.reshape(NDEV, 1, 8 * 768), 768), gm_b_in, m_gm_b_in, v_gm_b_in, "adamw_gm_b_in")
    out["gm_ln_g"] = _update(mine(p_lg.reshape(NDEV, 1, 4 * 768), 384), gm_ln_g, m_gm_ln_g, v_gm_ln_g, "adamw_gm_ln_g")
    out["gm_ln_b"] = _update(mine(p_lbias.reshape(NDEV, 1, 4 * 768), 384), gm_ln_b, m_gm_ln_b, v_gm_ln_b, "adamw_gm_ln_b")
    out["gm_w_s"] = _update(p_ws[:, None], gm_w_s, m_gm_w_s, v_gm_w_s, "adamw_gm_w_s")
    out["gm_b_s"] = _update(p_bs[:, None], gm_b_s, m_gm_b_s, v_gm_b_s, "adamw_gm_b_s")
    out["gm_w_out"] = _update(s_mo, gm_w_out, m_gm_w_out, v_gm_w_out, "adamw_gm_out")

    names = ["ada_w", "ada_b", "norm_pre", "norm_post", "ffn_w_in", "ffn_w_out", "hg_w_in", "hg_w_out",
             "hg_out_norm", "hg_lb", "gm_w_in", "gm_b_in", "gm_ln_g", "gm_ln_b", "gm_w_s", "gm_b_s", "gm_w_out"]
    return (loss, grad_x, *[out[n][0] for n in names], *[out[n][1] for n in names],
            *[out[n][2] for n in names], *[out[n][3] for n in names])
```

```python
import functools
import math

import jax
import jax.numpy as jnp
from jax import lax
from jax.experimental import pallas as pl
from jax.experimental.pallas import tpu as pltpu

F32 = jnp.float32
BF16 = jnp.bfloat16
NDEV = 8
D_MODEL = 1024
EPS = 1e-6
HG_CHUNK = 64
HG_HEAD = 128
HG_HEADS = 8
GM_CHUNK = 128
GM_GDIM = 384
GM_GROUPS = 8
ADAM_LR = 0.001
ADAM_B1 = 0.9
ADAM_B2 = 0.999
ADAM_EPS = 1e-08
ADAM_WD = 0.01
ADAM_STEP = 10
VMEM_LIMIT = 56 * 2 ** 20

NN = (((1,), (0,)), ((), ()))
NT = (((1,), (1,)), ((), ()))
TN = (((0,), (0,)), ((), ()))
MESH = pl.DeviceIdType.MESH
ANY = pl.BlockSpec(memory_space=pl.ANY)


def _dot(a, b, dims=NN, precision=None):
    return lax.dot_general(a, b, dims, preferred_element_type=F32, precision=precision)


def _params(*sem):
    return pltpu.CompilerParams(dimension_semantics=sem, vmem_limit_bytes=VMEM_LIMIT)


def _sigmoid(x):
    return 1.0 / (1.0 + jnp.exp(-x))


def _sigmoid_t(x):
    return 0.5 * jnp.tanh(0.5 * x) + 0.5


def _gelu_and_grad(x):
    c = math.sqrt(2.0 / math.pi)
    x2 = x * x
    t = jnp.tanh(x * (c + (c * 0.044715) * x2))
    hx = 0.5 * x
    p = 1.0 + t
    return hx * p, 0.5 * p + hx * (1.0 - t * t) * (c + (3.0 * c * 0.044715) * x2)


def _colsum(x):
    return jnp.sum(x, axis=0, keepdims=True)


def _rowmean(x):
    return jnp.mean(x, axis=-1, keepdims=True)


def _all_gather(shard, name):
    def body(x_ref, out_ref, send_sems, recv_sems, local_sem):
        x, y, c = lax.axis_index("x"), lax.axis_index("y"), lax.axis_index("c")
        me, sibling = (x, y, c), (x, y, 1 - c)
        chips = [(1 - x, y), (x, 1 - y), (1 - x, 1 - y)]

        def slot(p):
            return out_ref.at[4 * p[0] + 2 * p[1] + p[2]]

        def copy(k, block, to, src=None):
            return pltpu.make_async_remote_copy(
                src_ref=slot(block) if src is None else src, dst_ref=slot(block),
                send_sem=send_sems.at[k], recv_sem=recv_sems.at[k],
                device_id=to, device_id_type=MESH)

        mine = pltpu.make_async_copy(x_ref, slot(me), local_sem)
        mine.start()
        first = [copy(0, me, sibling, src=x_ref)]
        first += [copy(1 + j, me, (*chip, c), src=x_ref) for j, chip in enumerate(chips)]
        for cp in first:
            cp.start()
        passed = [copy(4 + j, (*chip, c), sibling) for j, chip in enumerate(chips)]
        for j, chip in enumerate(chips):
            copy(1 + j, (*chip, c), me).wait_recv()
            passed[j].start()
        copy(0, sibling, me).wait_recv()
        for j, chip in enumerate(chips):
            copy(4 + j, (*chip, 1 - c), me).wait_recv()
        for cp in first + passed:
            cp.wait_send()
        mine.wait()

    return pl.pallas_call(
        body, name=name,
        out_shape=jax.ShapeDtypeStruct((NDEV,) + shard.shape, shard.dtype),
        in_specs=[ANY], out_specs=ANY,
        scratch_shapes=[pltpu.SemaphoreType.DMA((7,)), pltpu.SemaphoreType.DMA((7,)),
                        pltpu.SemaphoreType.DMA(())],
    )(shard)


class _Exchange:
    def __init__(self, kind, arrays):
        self.gather = kind == "gather"
        self.arrays = list(arrays)
        self.n = n = len(self.arrays)
        self.out_shape = [jax.ShapeDtypeStruct(((NDEV,) + a.shape) if self.gather else a.shape, a.dtype)
                          for a in self.arrays]
        self.scratch = [pltpu.SemaphoreType.DMA((n, NDEV - 1)), pltpu.SemaphoreType.DMA((n, NDEV - 1)),
                        pltpu.SemaphoreType.DMA((n,))]

    def _copies(self, in_refs, out_refs, sems):
        send_sems, recv_sems, local_sems = sems
        x, y, c = lax.axis_index("x"), lax.axis_index("y"), lax.axis_index("c")
        me = 4 * x + 2 * y + c
        peers = [(1 - x if k & 4 else x, 1 - y if k & 2 else y, 1 - c if k & 1 else c) for k in range(1, NDEV)]
        local, send, recv = [], [], []
        for a in range(self.n):
            src = (lambda pid, a=a: in_refs[a]) if self.gather else (lambda pid, a=a: in_refs[a].at[pid])
            local.append(pltpu.make_async_copy(src(me), out_refs[a].at[me], local_sems.at[a]))
            for k, p in enumerate(peers):
                pid = 4 * p[0] + 2 * p[1] + p[2]
                for lst, slot in ((send, me), (recv, pid)):
                    lst.append(pltpu.make_async_remote_copy(
                        src_ref=src(pid), dst_ref=out_refs[a].at[slot],
                        send_sem=send_sems.at[a, k], recv_sem=recv_sems.at[a, k],
                        device_id=p, device_id_type=MESH))
        return local, send, recv

    def start(self, first, in_refs, out_refs, sems):
        @pl.when(first)
        def _():
            local, send, _ = self._copies(in_refs, out_refs, sems)
            for cp in local + send:
                cp.start()

    def finish(self, last, in_refs, out_refs, sems):
        @pl.when(last)
        def _():
            local, send, recv = self._copies(in_refs, out_refs, sems)
            for cp in send:
                cp.wait_send()
            for cp in recv:
                cp.wait_recv()
            for cp in local:
                cp.wait()


def _host(exchange, n_in, n_out, body, first_last):
    if exchange is None:
        return body, [], [], [], []
    n = exchange.n

    def hosted(*refs):
        ins, refs = refs[:n_in], refs[n_in:]
        xin, refs = refs[:n], refs[n:]
        outs, refs = refs[:n_out], refs[n_out:]
        xout, refs = refs[:n], refs[n:]
        scratch, sems = refs[:len(refs) - 3], refs[len(refs) - 3:]
        first, last = first_last()
        exchange.start(first, xin, xout, sems)
        body(*ins, *outs, *scratch)
        exchange.finish(last, xin, xout, sems)

    return hosted, exchange.arrays, [ANY] * n, exchange.out_shape, exchange.scratch


def _first_last(steps):
    def at():
        i = pl.program_id(0)
        return i == 0, i == steps - 1
    return at


def _mm_blocks(a, w, name, *, transpose_w=False, bias=None, out_dtype=BF16, tm=512, gelu=False, exchange=None):
    T, K = a.shape
    J = w.shape[0]
    n = w.shape[1] if transpose_w else w.shape[2]
    tm = min(tm, T)
    dims = NT if transpose_w else NN
    n_in = 2 + (bias is not None)

    def body(*refs):
        a_ref, w_ref = refs[:2]
        av = a_ref[...]
        for j in range(J):
            r = _dot(av, w_ref[j], dims)
            if bias is not None:
                r = r + refs[2][j]
            if gelu:
                z, dz = _gelu_and_grad(r)
                refs[n_in][j] = z.astype(BF16)
                refs[n_in + 1][j] = dz.astype(BF16)
            else:
                refs[n_in][j] = r.astype(out_dtype)

    in_specs = [pl.BlockSpec((tm, K), lambda m: (m, 0)), _whole_spec(w)]
    args = [a, w]
    if bias is not None:
        in_specs.append(_whole_spec(bias))
        args.append(bias)
    outs = [jax.ShapeDtypeStruct((J, T, n), BF16)] * 2 if gelu else [jax.ShapeDtypeStruct((J, T, n), out_dtype)]
    body, x_args, x_in, x_out, x_scratch = _host(exchange, n_in, len(outs), body, _first_last(T // tm))
    res = pl.pallas_call(
        body, name=name, grid=(T // tm,),
        out_shape=outs + x_out,
        in_specs=in_specs + x_in,
        out_specs=[pl.BlockSpec((J, tm, n), lambda m: (0, m, 0))] * len(outs) + x_in,
        scratch_shapes=x_scratch,
        compiler_params=_params("arbitrary" if exchange else "parallel"),
    )(*args, *x_args)
    return res if (exchange or gelu) else res[0]


SUB_ROWS = 256


def _whole_spec(w):
    return pl.BlockSpec(w.shape, lambda m: (0, 0, 0), pipeline_mode=pl.Buffered(1))


def _row_blocks(tm):
    sub = min(SUB_ROWS, tm)
    return [slice(r, r + sub) for r in range(0, tm, sub)]


def _sum_dots(a_ref, w_ref, dims, rows=slice(None)):
    acc = _dot(a_ref[0, rows], w_ref[0], dims)
    for j in range(1, a_ref.shape[0]):
        acc += _dot(a_ref[j, rows], w_ref[j], dims)
    return acc


def _rms(v):
    return lax.rsqrt(_rowmean(v * v) + EPS)


def _zero_at_start(*refs):
    @pl.when(pl.program_id(0) == 0)
    def _():
        for r in refs:
            r[...] = jnp.zeros_like(r)


def _lead_lag(nt):
    return (lambda m: jnp.minimum(m, nt - 1)), (lambda m: jnp.maximum(m - 1, 0))


def _stage(scr):
    m = pl.program_id(0)

    @pl.when(m == 0)
    def _():
        scr[...] = jnp.zeros_like(scr)

    return m % 2


def _postnorm_bwd_math(dxo, yv, g, gate, res_w, dgate_ref, dpost_ref, valid=None):
    r = _rms(yv)
    yh = yv * r
    dr = dxo * res_w
    keep = (lambda v: v) if valid is None else (lambda v: jnp.where(valid, v, 0.0))
    dgate_ref[...] += keep(_colsum(dr * (yh * g)))
    dp = dr * gate
    dpost_ref[...] += keep(_colsum(dp * yh))
    dyh = dp * g
    return (r * (dyh - yh * _rowmean(dyh * yh))).astype(BF16)


def _out_proj(a, w, x, post_g, gate, res_w, nxt, name, *, tm=512):
    J, T, n = a.shape
    d = w.shape[2]
    tm = min(tm, T)
    nt = T // tm
    lead, lag = _lead_lag(nt)

    def body(a_ref, w_ref, x_ref, pg_ref, gate_ref, ng_ref, nsc_ref, nsh_ref, y_ref, xn_ref, h_ref, y_scr):
        slot = _stage(y_scr)
        y_scr[slot] = _sum_dots(a_ref, w_ref, NN)
        y = y_scr[1 - slot]
        y_ref[...] = y
        xn = x_ref[...] + res_w * gate_ref[...] * (y * _rms(y) * pg_ref[...])
        xn_ref[...] = xn
        h_ref[...] = (xn * _rms(xn) * ng_ref[...] * (1.0 + nsc_ref[...]) + nsh_ref[...]).astype(BF16)

    return pl.pallas_call(
        body, name=name, grid=(nt + 1,),
        out_shape=(jax.ShapeDtypeStruct((T, d), F32), jax.ShapeDtypeStruct((T, d), F32),
                   jax.ShapeDtypeStruct((T, d), BF16)),
        in_specs=[pl.BlockSpec((J, tm, n), lambda m: (0, lead(m), 0)), _whole_spec(w),
                  pl.BlockSpec((tm, d), lambda m: (lag(m), 0))] + [_vec_spec(d)] * 5,
        out_specs=(pl.BlockSpec((tm, d), lambda m: (lag(m), 0)),) * 3,
        scratch_shapes=[pltpu.VMEM((2, tm, d), F32)],
        compiler_params=_params("arbitrary"),
    )(a, w, x, post_g, gate, *nxt)


def _out_proj_last(a, w, x, post_g, gate, res_w, target, name, *, tm=512):
    J, T, n = a.shape
    d = w.shape[2]
    tm = min(tm, T)

    def body(a_ref, w_ref, x_ref, pg_ref, gate_ref, t_ref, dx_ref, dy_ref, dgate_ref, dpost_ref, l_ref):
        _zero_at_start(dgate_ref, dpost_ref, l_ref)
        for rows in _row_blocks(tm):
            y = _sum_dots(a_ref, w_ref, NN, rows)
            e = x_ref[rows] + res_w * gate_ref[...] * (y * _rms(y) * pg_ref[...]) - t_ref[rows]
            l_ref[...] += 0.5 * jnp.sum(_rowmean(e * e), axis=0, keepdims=True)
            dx = e * (1.0 / d)
            dx_ref[rows] = dx
            dy_ref[rows] = _postnorm_bwd_math(dx, y, pg_ref[...], gate_ref[...], res_w, dgate_ref, dpost_ref)

    return pl.pallas_call(
        body, name=name, grid=(T // tm,),
        out_shape=(jax.ShapeDtypeStruct((T, d), F32), jax.ShapeDtypeStruct((T, d), BF16),
                   jax.ShapeDtypeStruct((1, d), F32), jax.ShapeDtypeStruct((1, d), F32),
                   jax.ShapeDtypeStruct((1, 128), F32)),
        in_specs=[pl.BlockSpec((J, tm, n), lambda m: (0, m, 0)), _whole_spec(w),
                  _row_spec(tm, d), _vec_spec(d), _vec_spec(d), _row_spec(tm, d)],
        out_specs=(_row_spec(tm, d), _row_spec(tm, d), _vec_spec(d), _vec_spec(d),
                   pl.BlockSpec((1, 128), lambda m: (0, 0))),
        compiler_params=_params("arbitrary"),
    )(a, w, x, post_g, gate, target)


def _in_grad(dz, w, dxo, x, pre_g, scale, prev, name, *, tm=512, exchange=None):
    J, T, n = dz.shape
    d = w.shape[1]
    tm = min(tm, T)
    has_prev = prev is not None
    res_w = prev[3] if has_prev else None

    nt = T // tm
    lead, lag = _lead_lag(nt)

    def body(*refs):
        dz_ref, w_ref, dxo_ref, x_ref, g_ref, sc_ref = refs[:6]
        dh_scr = refs[-1]
        if has_prev:
            yp_ref, ppg_ref, pgate_ref, dx_ref, dsh_ref, dsc_ref, dg_ref, dyp_ref, dgate_ref, dpost_ref = refs[6:-1]
            _zero_at_start(dsh_ref, dsc_ref, dg_ref, dgate_ref, dpost_ref)
        else:
            dx_ref, dsh_ref, dsc_ref, dg_ref = refs[6:-1]
            _zero_at_start(dsh_ref, dsc_ref, dg_ref)
        slot = _stage(dh_scr)
        dh_scr[slot] = _sum_dots(dz_ref, w_ref, NT)
        valid = pl.program_id(0) > 0
        dh = dh_scr[1 - slot]
        xv = x_ref[...]
        r = _rms(xv)
        xh = xv * r
        dsh_ref[...] += _colsum(dh)
        dsc_ref[...] += _colsum(dh * (xh * g_ref[...]))
        dn = dh * (1.0 + sc_ref[...])
        dg_ref[...] += _colsum(dn * xh)
        dxh = dn * g_ref[...]
        dx = dxo_ref[...] + r * (dxh - xh * _rowmean(dxh * xh))
        dx_ref[...] = dx
        if has_prev:
            dyp_ref[...] = _postnorm_bwd_math(dx, yp_ref[...], ppg_ref[...], pgate_ref[...], res_w,
                                               dgate_ref, dpost_ref, valid)

    vec = jax.ShapeDtypeStruct((1, d), F32)
    lagged = pl.BlockSpec((tm, d), lambda m: (lag(m), 0))
    in_specs = [pl.BlockSpec((J, tm, n), lambda m: (0, lead(m), 0)), _whole_spec(w),
                lagged, lagged, _vec_spec(d), _vec_spec(d)]
    out_shape = [jax.ShapeDtypeStruct((T, d), F32), vec, vec, vec]
    out_specs = [lagged, _vec_spec(d), _vec_spec(d), _vec_spec(d)]
    args = [dz, w, dxo, x, pre_g, scale]
    if has_prev:
        in_specs += [lagged, _vec_spec(d), _vec_spec(d)]
        out_shape += [jax.ShapeDtypeStruct((T, d), BF16), vec, vec]
        out_specs += [lagged, _vec_spec(d), _vec_spec(d)]
        args += list(prev[:3])
    scratch = [pltpu.VMEM((2, tm, d), F32)]
    body, x_args, x_in, x_out, x_scratch = _host(exchange, len(args), len(out_shape), body, _first_last(nt + 1))
    return pl.pallas_call(
        body, name=name, grid=(nt + 1,),
        out_shape=out_shape + x_out, in_specs=in_specs + x_in, out_specs=out_specs + x_in,
        scratch_shapes=scratch + x_scratch,
        compiler_params=_params("arbitrary"),
    )(*args, *x_args)


def _mm_wgrad(xs, ys, name, *, tt=2048):
    Jx, T, P = xs.shape
    Jy, _, Q = ys.shape
    J = max(Jx, Jy)
    tt = min(tt, T)
    nt = T // tt

    def body(x_ref, y_ref, o_ref, acc_ref):
        t = pl.program_id(1)

        @pl.when(t == 0)
        def _():
            acc_ref[...] = jnp.zeros_like(acc_ref)

        acc_ref[...] += _dot(x_ref[...], y_ref[...], TN)

        @pl.when(t == nt - 1)
        def _():
            o_ref[...] = acc_ref[...].astype(BF16)

    return pl.pallas_call(
        body, name=name, grid=(J, nt),
        out_shape=jax.ShapeDtypeStruct((J, P, Q), BF16),
        in_specs=[pl.BlockSpec((None, tt, P), (lambda j, t: (j, t, 0)) if Jx > 1 else (lambda j, t: (0, t, 0))),
                  pl.BlockSpec((None, tt, Q), (lambda j, t: (j, t, 0)) if Jy > 1 else (lambda j, t: (0, t, 0)))],
        out_specs=pl.BlockSpec((None, P, Q), lambda j, t: (j, 0, 0)),
        scratch_shapes=[pltpu.VMEM((P, Q), F32)],
        compiler_params=_params("parallel", "arbitrary"),
    )(xs, ys)


def _ffn_in(h, wg, name, *, tm=512, exchange=None):
    T, K = h.shape
    n = wg.shape[-1]
    tm = min(tm, T)

    def body(h_ref, w_ref, ab_ref, g_ref):
        hh = h_ref[...]
        for j in range(4):
            a = _dot(hh, w_ref[j])
            b = _dot(hh, w_ref[4 + j])
            s = _sigmoid_t(a)
            silu = a * s
            ab_ref[j] = (b * (s * (1.0 + a * (1.0 - s)))).astype(BF16)
            ab_ref[4 + j] = silu.astype(BF16)
            g_ref[j] = (silu * b).astype(BF16)

    body, x_args, x_in, x_out, x_scratch = _host(exchange, 2, 2, body, _first_last(T // tm))
    res = pl.pallas_call(
        body, name=name, grid=(T // tm,),
        out_shape=[jax.ShapeDtypeStruct((8, T, n), BF16), jax.ShapeDtypeStruct((4, T, n), BF16)] + x_out,
        in_specs=[pl.BlockSpec((tm, K), lambda m: (m, 0)), _whole_spec(wg)] + x_in,
        out_specs=[pl.BlockSpec((8, tm, n), lambda m: (0, m, 0)), pl.BlockSpec((4, tm, n), lambda m: (0, m, 0))] + x_in,
        scratch_shapes=x_scratch,
        compiler_params=_params("arbitrary" if exchange else "parallel"),
    )(h, wg, *x_args)
    return res[0], res[1], res[2:]


def _ffn_dgate(dy, wo4, ab, name, *, tm=512):
    T, N = dy.shape
    n = wo4.shape[1]
    tm = min(tm, T)

    def body(dy_ref, w_ref, ab_ref, dab_ref):
        dyv = dy_ref[...]
        for j in range(4):
            dg = _dot(dyv, w_ref[j], NT)
            dab_ref[j] = (dg * ab_ref[j].astype(F32)).astype(BF16)
            dab_ref[4 + j] = (dg * ab_ref[4 + j].astype(F32)).astype(BF16)

    return pl.pallas_call(
        body, name=name, grid=(T // tm,),
        out_shape=jax.ShapeDtypeStruct((8, T, n), BF16),
        in_specs=[pl.BlockSpec((tm, N), lambda m: (m, 0)), _whole_spec(wo4),
                  pl.BlockSpec((8, tm, n), lambda m: (0, m, 0))],
        out_specs=pl.BlockSpec((8, tm, n), lambda m: (0, m, 0)),
        compiler_params=_params("parallel"),
    )(dy, wo4, ab)


def _row_spec(tm, d):
    return pl.BlockSpec((tm, d), lambda m: (m, 0))


def _vec_spec(d):
    return pl.BlockSpec((1, d), lambda m: (0, 0))


def _prenorm_fwd(x, pre_g, scale, shift, name, *, tm=512):
    T, d = x.shape
    tm = min(tm, T)

    def body(x_ref, g_ref, sc_ref, sh_ref, h_ref):
        xv = x_ref[...]
        r = lax.rsqrt(_rowmean(xv * xv) + EPS)
        h_ref[...] = (xv * r * g_ref[...] * (1.0 + sc_ref[...]) + sh_ref[...]).astype(BF16)

    return pl.pallas_call(
        body, name=name, grid=(T // tm,),
        out_shape=jax.ShapeDtypeStruct((T, d), BF16),
        in_specs=[_row_spec(tm, d), _vec_spec(d), _vec_spec(d), _vec_spec(d)],
        out_specs=_row_spec(tm, d),
        compiler_params=_params("parallel"),
    )(x, pre_g, scale, shift)


def _postnorm_fwd(x, y, post_g, gate, res_w, name, *, tm=512):
    T, d = x.shape
    tm = min(tm, T)

    def body(x_ref, y_ref, g_ref, gate_ref, o_ref):
        yv = y_ref[...]
        r = lax.rsqrt(_rowmean(yv * yv) + EPS)
        o_ref[...] = x_ref[...] + res_w * gate_ref[...] * (yv * r * g_ref[...])

    return pl.pallas_call(
        body, name=name, grid=(T // tm,),
        out_shape=jax.ShapeDtypeStruct((T, d), F32),
        in_specs=[_row_spec(tm, d), _row_spec(tm, d), _vec_spec(d), _vec_spec(d)],
        out_specs=_row_spec(tm, d),
        compiler_params=_params("parallel"),
    )(x, y, post_g, gate)


def _postnorm_bwd(dxo, y, post_g, gate, res_w, name, *, tm=512):
    T, d = y.shape
    tm = min(tm, T)

    def body(dxo_ref, y_ref, g_ref, gate_ref, dy_ref, dgate_ref, dg_ref):
        @pl.when(pl.program_id(0) == 0)
        def _():
            dgate_ref[...] = jnp.zeros_like(dgate_ref)
            dg_ref[...] = jnp.zeros_like(dg_ref)

        yv = y_ref[...]
        r = lax.rsqrt(_rowmean(yv * yv) + EPS)
        yh = yv * r
        dr = dxo_ref[...] * res_w
        dgate_ref[...] += _colsum(dr * (yh * g_ref[...]))
        dp = dr * gate_ref[...]
        dg_ref[...] += _colsum(dp * yh)
        dyh = dp * g_ref[...]
        dy_ref[...] = (r * (dyh - yh * _rowmean(dyh * yh))).astype(BF16)

    return pl.pallas_call(
        body, name=name, grid=(T // tm,),
        out_shape=(jax.ShapeDtypeStruct((T, d), BF16), jax.ShapeDtypeStruct((1, d), F32),
                   jax.ShapeDtypeStruct((1, d), F32)),
        in_specs=[_row_spec(tm, d), _row_spec(tm, d), _vec_spec(d), _vec_spec(d)],
        out_specs=(_row_spec(tm, d), _vec_spec(d), _vec_spec(d)),
        compiler_params=_params("arbitrary"),
    )(dxo, y, post_g, gate)


def _prenorm_bwd(dxo, dh, x, pre_g, scale, name, *, tm=512):
    T, d = x.shape
    tm = min(tm, T)

    def body(dxo_ref, dh_ref, x_ref, g_ref, sc_ref, dx_ref, dsh_ref, dsc_ref, dg_ref):
        @pl.when(pl.program_id(0) == 0)
        def _():
            dsh_ref[...] = jnp.zeros_like(dsh_ref)
            dsc_ref[...] = jnp.zeros_like(dsc_ref)
            dg_ref[...] = jnp.zeros_like(dg_ref)

        xv = x_ref[...]
        dhv = dh_ref[...].astype(F32)
        r = lax.rsqrt(_rowmean(xv * xv) + EPS)
        xh = xv * r
        dsh_ref[...] += _colsum(dhv)
        dsc_ref[...] += _colsum(dhv * (xh * g_ref[...]))
        dn = dhv * (1.0 + sc_ref[...])
        dg_ref[...] += _colsum(dn * xh)
        dxh = dn * g_ref[...]
        dx_ref[...] = dxo_ref[...] + r * (dxh - xh * _rowmean(dxh * xh))

    return pl.pallas_call(
        body, name=name, grid=(T // tm,),
        out_shape=(jax.ShapeDtypeStruct((T, d), F32),) + (jax.ShapeDtypeStruct((1, d), F32),) * 3,
        in_specs=[_row_spec(tm, d), _row_spec(tm, d), _row_spec(tm, d), _vec_spec(d), _vec_spec(d)],
        out_specs=(_row_spec(tm, d),) + (_vec_spec(d),) * 3,
        compiler_params=_params("arbitrary"),
    )(dxo, dh, x, pre_g, scale)


def _loss_head(y, target, name, *, tm=512):
    T, d = y.shape
    tm = min(tm, T)

    def body(y_ref, t_ref, dy_ref, l_ref):
        @pl.when(pl.program_id(0) == 0)
        def _():
            l_ref[...] = jnp.zeros_like(l_ref)

        e = y_ref[...] - t_ref[...]
        dy_ref[...] = e * (1.0 / d)
        l_ref[...] += 0.5 * jnp.sum(_rowmean(e * e), axis=0, keepdims=True)

    return pl.pallas_call(
        body, name=name, grid=(T // tm,),
        out_shape=(jax.ShapeDtypeStruct((T, d), F32), jax.ShapeDtypeStruct((1, 128), F32)),
        in_specs=[_row_spec(tm, d), _row_spec(tm, d)],
        out_specs=(_row_spec(tm, d), pl.BlockSpec((1, 128), lambda m: (0, 0))),
        compiler_params=_params("arbitrary"),
    )(y, target)


def _ada_fwd(c_all, w, b, name):
    L, K, n = w.shape

    def body(c_ref, w_ref, b_ref, o_ref):
        cv = c_ref[...]
        cond = cv * _sigmoid(cv)
        for l in range(L):
            o_ref[l] = _dot(cond, w_ref[l], precision=lax.Precision.HIGHEST) + b_ref[l]

    return pl.pallas_call(
        body, name=name,
        out_shape=jax.ShapeDtypeStruct((L, NDEV, n), F32),
        compiler_params=pltpu.CompilerParams(vmem_limit_bytes=VMEM_LIMIT),
    )(c_all, w, b)


def _ada_bwd(c_all_t, gmod, name):
    L, _, n = gmod.shape
    K = c_all_t.shape[0]

    def body(c_ref, g_ref, o_ref):
        cv = c_ref[...]
        cond = cv * _sigmoid(cv)
        for l in range(L):
            o_ref[l] = _dot(cond, g_ref[l], precision=lax.Precision.HIGHEST)

    return pl.pallas_call(
        body, name=name,
        out_shape=jax.ShapeDtypeStruct((L, K, n), F32),
        compiler_params=pltpu.CompilerParams(vmem_limit_bytes=VMEM_LIMIT),
    )(c_all_t, gmod)


def _tri(n, upper=False, block=None):
    r = lax.broadcasted_iota(jnp.int32, (n, n), 0)
    c = lax.broadcasted_iota(jnp.int32, (n, n), 1)
    m = (c >= r) if upper else (c <= r)
    if block is not None:
        m = m & ((r // block) == (c // block))
    return m.astype(F32)


def _hgrn_gates(proj_ref, lb_ref, jh):
    lb = lb_ref[:, 512 * jh:512 * (jh + 1)]
    qp = proj_ref[jh]
    fx = proj_ref[2 + jh]
    sq = _sigmoid(qp)
    sig = _sigmoid(fx)
    f = lb + (1.0 - lb) * sig
    k = (1.0 - lb) * (1.0 - sig)
    return lb, qp, sq, sig, f, k


def _hgrn_fwd(proj, lb, out_norm, name, *, tb=128):
    T = proj.shape[1]
    tb = min(tb, T)
    nc = tb // HG_CHUNK
    lmat = _tri(tb, block=HG_CHUNK)

    def body(proj_ref, lb_ref, on_ref, l_ref, o_ref, og_ref, st_ref, s_scr, b_scr):
        @pl.when(pl.program_id(0) == 0)
        def _():
            s_scr[...] = jnp.zeros_like(s_scr)

        r_i = lax.broadcasted_iota(jnp.int32, (HG_CHUNK, HG_CHUNK), 0)
        c_i = lax.broadcasted_iota(jnp.int32, (HG_CHUNK, HG_CHUNK), 1)
        causal = c_i <= r_i
        onv = on_ref[...]
        for jh in range(2):
            lbv, qp, sq, sig, f, k = _hgrn_gates(proj_ref, lb_ref, jh)
            q = qp * sq
            b_scr[...] = _dot(l_ref[...], jnp.log(f), precision=lax.Precision.HIGHEST)
            v = proj_ref[4 + jh]
            gp = proj_ref[6 + jh]
            gs = gp * _sigmoid(gp)
            for hh in range(4):
                hd = 4 * jh + hh
                cs = slice(HG_HEAD * hh, HG_HEAD * (hh + 1))
                for ci in range(nc):
                    r0 = HG_CHUNK * ci
                    rs = slice(r0, r0 + HG_CHUNK)
                    bc = b_scr[rs, cs]
                    bm = b_scr[r0 + HG_CHUNK // 2 - 1:r0 + HG_CHUNK // 2, cs]
                    bl = b_scr[r0 + HG_CHUNK - 1:r0 + HG_CHUNK, cs]
                    qc, kc, vc = q[rs, cs], k[rs, cs], v[rs, cs].astype(BF16)
                    qe = (qc * jnp.exp(bc)).astype(BF16)
                    qt = (qc * jnp.exp(bc - bm)).astype(BF16)
                    kt = (kc * jnp.exp(bm - bc)).astype(BF16)
                    kd = (kc * jnp.exp(bl - bc)).astype(BF16)
                    st = s_scr[hd]
                    stb = st.astype(BF16)
                    st_ref[ci, hd] = stb
                    a = jnp.where(causal, _dot(qt, kt, NT), 0.0).astype(BF16)
                    o = _dot(qe, stb, NT) + _dot(a, vc)
                    s_scr[hd] = st * jnp.exp(bl) + _dot(vc, kd, TN)
                    o_ref[rs, HG_HEAD * hd:HG_HEAD * (hd + 1)] = o
                    r = lax.rsqrt(_rowmean(o * o) + EPS)
                    og_ref[rs, HG_HEAD * hd:HG_HEAD * (hd + 1)] = (o * r * onv * gs[rs, cs]).astype(BF16)

    return pl.pallas_call(
        body, name=name, grid=(T // tb,),
        out_shape=(jax.ShapeDtypeStruct((T, D_MODEL), F32), jax.ShapeDtypeStruct((T, D_MODEL), BF16),
                   jax.ShapeDtypeStruct((T // HG_CHUNK, HG_HEADS, HG_HEAD, HG_HEAD), BF16)),
        in_specs=[pl.BlockSpec((8, tb, 512), lambda i: (0, i, 0)),
                  pl.BlockSpec((1, D_MODEL), lambda i: (0, 0)),
                  pl.BlockSpec((1, HG_HEAD), lambda i: (0, 0)),
                  pl.BlockSpec((tb, tb), lambda i: (0, 0))],
        out_specs=(pl.BlockSpec((tb, D_MODEL), lambda i: (i, 0)),
                   pl.BlockSpec((tb, D_MODEL), lambda i: (i, 0)),
                   pl.BlockSpec((nc, HG_HEADS, HG_HEAD, HG_HEAD), lambda i: (i, 0, 0, 0))),
        scratch_shapes=[pltpu.VMEM((HG_HEADS, HG_HEAD, HG_HEAD), F32), pltpu.VMEM((tb, 512), F32)],
        compiler_params=_params("arbitrary"),
    )(proj, lb, out_norm, lmat)


def _hgrn_bwd(proj, o, dog, states, lb, out_norm, name, *, tb=128):
    T = proj.shape[1]
    tb = min(tb, T)
    nc = tb // HG_CHUNK
    nb = T // tb
    lmat = _tri(tb, block=HG_CHUNK)
    umat = _tri(tb, upper=True, block=HG_CHUNK)

    def body(proj_ref, o_ref, dog_ref, st_ref, lb_ref, on_ref, l_ref, u_ref,
             dproj_ref, dlb_ref, don_ref, ds_scr, b_scr, dq_scr, dk_scr, dv_scr, dg_scr, db_scr):
        @pl.when(pl.program_id(0) == 0)
        def _():
            ds_scr[...] = jnp.zeros_like(ds_scr)
            dlb_ref[...] = jnp.zeros_like(dlb_ref)
            don_ref[...] = jnp.zeros_like(don_ref)

        r_i = lax.broadcasted_iota(jnp.int32, (HG_CHUNK, HG_CHUNK), 0)
        c_i = lax.broadcasted_iota(jnp.int32, (HG_CHUNK, HG_CHUNK), 1)
        causal = c_i <= r_i
        causal_t = r_i <= c_i
        last_row = lax.broadcasted_iota(jnp.int32, (HG_CHUNK, HG_HEAD), 0) == HG_CHUNK - 1
        onv = on_ref[...]
        don_acc = jnp.zeros((1, HG_HEAD), F32)
        for jh in range(2):
            lbv, qp, sq, sig, f, k = _hgrn_gates(proj_ref, lb_ref, jh)
            q = qp * sq
            b_scr[...] = _dot(l_ref[...], jnp.log(f), precision=lax.Precision.HIGHEST)
            v = proj_ref[4 + jh]
            gp = proj_ref[6 + jh]
            sg = _sigmoid(gp)
            for hh in range(4):
                hd = 4 * jh + hh
                cs = slice(HG_HEAD * hh, HG_HEAD * (hh + 1))
                hs = slice(HG_HEAD * hd, HG_HEAD * (hd + 1))
                for ci in reversed(range(nc)):
                    r0 = HG_CHUNK * ci
                    rs = slice(r0, r0 + HG_CHUNK)
                    oc = o_ref[rs, hs]
                    r = lax.rsqrt(_rowmean(oc * oc) + EPS)
                    oh = oc * r
                    gc, sgc = gp[rs, cs], sg[rs, cs]
                    dogc = dog_ref[rs, hs].astype(F32)
                    don = dogc * (gc * sgc)
                    dg_scr[rs, cs] = dogc * (oh * onv) * (sgc * (1.0 + gc * (1.0 - sgc)))
                    don_acc += _colsum(don * oh)
                    donh = don * onv
                    do = (r * (donh - oh * _rowmean(donh * oh))).astype(BF16)
                    bc = b_scr[rs, cs]
                    bm = b_scr[r0 + HG_CHUNK // 2 - 1:r0 + HG_CHUNK // 2, cs]
                    bl = b_scr[r0 + HG_CHUNK - 1:r0 + HG_CHUNK, cs]
                    qc, kc, vc = q[rs, cs], k[rs, cs], v[rs, cs].astype(BF16)
                    e_b, e_q, e_k, e_d = jnp.exp(bc), jnp.exp(bc - bm), jnp.exp(bm - bc), jnp.exp(bl - bc)
                    qe = (qc * e_b).astype(BF16)
                    qt = (qc * e_q).astype(BF16)
                    kt = (kc * e_k).astype(BF16)
                    kd = (kc * e_d).astype(BF16)
                    stb = st_ref[ci, hd]
                    dst = ds_scr[hd]
                    dstb = dst.astype(BF16)
                    a_t = jnp.where(causal_t, _dot(kt, qt, NT), 0.0).astype(BF16)
                    da = jnp.where(causal, _dot(do, vc, NT), 0.0).astype(BF16)
                    da_t = jnp.where(causal_t, _dot(vc, do, NT), 0.0).astype(BF16)
                    dv_scr[rs, cs] = _dot(a_t, do) + _dot(kd, dstb, NT)
                    dqe, dqt = _dot(do, stb), _dot(da, kt)
                    dkt, dkd = _dot(da_t, qt), _dot(vc, dstb)
                    dq_scr[rs, cs] = dqe * e_b + dqt * e_q
                    dk_scr[rs, cs] = dkt * e_k + dkd * e_d
                    e_l = jnp.exp(bl)
                    s_end = stb.astype(F32) * e_l + _dot(vc, kd, TN)
                    dbc = (qe.astype(F32) * dqe + qt.astype(F32) * dqt
                           - kt.astype(F32) * dkt - kd.astype(F32) * dkd)
                    db_scr[rs, cs] = dbc + jnp.where(last_row, _colsum(dstb.astype(F32) * s_end), 0.0)
                    ds_scr[hd] = dst * e_l + _dot(do, qe, TN)
            dq = dq_scr[...]
            dk = dk_scr[...]
            cols = slice(512 * jh, 512 * (jh + 1))
            dlogf = _dot(u_ref[...], db_scr[...], precision=lax.Precision.HIGHEST)
            one_m_sig = 1.0 - sig
            dsig = (1.0 - lbv) * sig * one_m_sig
            dproj_ref[jh] = (dq * (sq * (1.0 + qp * (1.0 - sq)))).astype(BF16)
            dproj_ref[2 + jh] = (dlogf * dsig / f - dk * dsig).astype(BF16)
            dproj_ref[4 + jh] = dv_scr[...].astype(BF16)
            dproj_ref[6 + jh] = dg_scr[...].astype(BF16)
            dlb_ref[:, cols] += _colsum(dlogf * one_m_sig / f - dk * one_m_sig)
        don_ref[...] += don_acc

    rev = lambda i: nb - 1 - i
    return pl.pallas_call(
        body, name=name, grid=(nb,),
        out_shape=(jax.ShapeDtypeStruct((8, T, 512), BF16), jax.ShapeDtypeStruct((1, D_MODEL), F32),
                   jax.ShapeDtypeStruct((1, HG_HEAD), F32)),
        in_specs=[pl.BlockSpec((8, tb, 512), lambda i: (0, rev(i), 0)),
                  pl.BlockSpec((tb, D_MODEL), lambda i: (rev(i), 0)),
                  pl.BlockSpec((tb, D_MODEL), lambda i: (rev(i), 0)),
                  pl.BlockSpec((nc, HG_HEADS, HG_HEAD, HG_HEAD), lambda i: (rev(i), 0, 0, 0)),
                  pl.BlockSpec((1, D_MODEL), lambda i: (0, 0)),
                  pl.BlockSpec((1, HG_HEAD), lambda i: (0, 0)),
                  pl.BlockSpec((tb, tb), lambda i: (0, 0)),
                  pl.BlockSpec((tb, tb), lambda i: (0, 0))],
        out_specs=(pl.BlockSpec((8, tb, 512), lambda i: (0, rev(i), 0)),
                   pl.BlockSpec((1, D_MODEL), lambda i: (0, 0)),
                   pl.BlockSpec((1, HG_HEAD), lambda i: (0, 0))),
        scratch_shapes=[pltpu.VMEM((HG_HEADS, HG_HEAD, HG_HEAD), F32)] + [pltpu.VMEM((tb, 512), F32)] * 6,
        compiler_params=_params("arbitrary"),
    )(proj, o, dog, states, lb, out_norm, lmat, umat)


def _gm_norm(pre_ref, lg_ref, lbias_ref):
    vs = [pre_ref[4 + j].astype(F32) for j in range(4)]
    width = 4 * vs[0].shape[1]
    mu = sum(jnp.sum(v, axis=1, keepdims=True) for v in vs) / width
    ds = [v - mu for v in vs]
    var = sum(jnp.sum(d * d, axis=1, keepdims=True) for d in ds) / width
    rstd = lax.rsqrt(var + EPS)
    vhat = [d * rstd for d in ds]
    vn = [vhat[j] * lg_ref[j:j + 1, :] + lbias_ref[j:j + 1, :] for j in range(4)]
    return vhat, vn, rstd


def _gm_spatial_fwd(pre, ln_g, ln_b, ws, bsb, name, *, tb=256):
    T = pre.shape[1]
    tb = min(tb, T)
    nc = tb // GM_CHUNK

    def body(pre_ref, lg_ref, lbias_ref, ws_ref, bs_ref, o_ref):
        _, vn, _ = _gm_norm(pre_ref, lg_ref, lbias_ref)
        for j in range(4):
            u = pre_ref[j].astype(F32)
            for e in range(2):
                g = 2 * j + e
                cs = slice(GM_GDIM * e, GM_GDIM * (e + 1))
                wg = ws_ref[g].astype(BF16)
                for ci in range(nc):
                    rs = slice(GM_CHUNK * ci, GM_CHUNK * (ci + 1))
                    vm = _dot(wg, vn[j][rs, cs].astype(BF16)) + bs_ref[g]
                    o_ref[j, rs, cs] = (u[rs, cs] * vm).astype(BF16)

    return pl.pallas_call(
        body, name=name, grid=(T // tb,),
        out_shape=jax.ShapeDtypeStruct((4, T, 768), BF16),
        in_specs=[pl.BlockSpec((8, tb, 768), lambda i: (0, i, 0)),
                  pl.BlockSpec((4, 768), lambda i: (0, 0)),
                  pl.BlockSpec((4, 768), lambda i: (0, 0)),
                  pl.BlockSpec((GM_GROUPS, GM_CHUNK, GM_CHUNK), lambda i: (0, 0, 0)),
                  pl.BlockSpec((GM_GROUPS, GM_CHUNK, GM_GDIM), lambda i: (0, 0, 0))],
        out_specs=pl.BlockSpec((4, tb, 768), lambda i: (0, i, 0)),
        compiler_params=_params("parallel"),
    )(pre, ln_g, ln_b, ws, bsb)


def _gm_spatial_bwd(pre, gp, dm, ln_g, ln_b, ws, ws_t, bsb, name, *, tb=256):
    T = pre.shape[1]
    tb = min(tb, T)
    nc = tb // GM_CHUNK
    nb = T // tb

    def body(pre_ref, gp_ref, dm_ref, lg_ref, lbias_ref, ws_ref, wst_ref, bs_ref,
             dpre_ref, dws_ref, dbs_ref, dlg_ref, dlb_ref, dbin_ref, dbs_scr, dvn_scr, du_scr):
        i = pl.program_id(0)

        @pl.when(i == 0)
        def _():
            dws_ref[...] = jnp.zeros_like(dws_ref)
            dbs_scr[...] = jnp.zeros_like(dbs_scr)
            dlg_ref[...] = jnp.zeros_like(dlg_ref)
            dlb_ref[...] = jnp.zeros_like(dlb_ref)
            dbin_ref[...] = jnp.zeros_like(dbin_ref)

        vhat, vn, rstd = _gm_norm(pre_ref, lg_ref, lbias_ref)
        for j in range(4):
            u = pre_ref[j].astype(F32)
            for e in range(2):
                g = 2 * j + e
                cs = slice(GM_GDIM * e, GM_GDIM * (e + 1))
                wg = ws_ref[g].astype(BF16)
                wgt = wst_ref[g].astype(BF16)
                for ci in range(nc):
                    rs = slice(GM_CHUNK * ci, GM_CHUNK * (ci + 1))
                    vnb = vn[j][rs, cs].astype(BF16)
                    vm = _dot(wg, vnb) + bs_ref[g]
                    dmg = dm_ref[j, rs, cs].astype(F32)
                    du_scr[j, rs, cs] = dmg * vm
                    dvm = dmg * u[rs, cs]
                    dvmb = dvm.astype(BF16)
                    dws_ref[g] += _dot(dvmb, vnb, NT)
                    dbs_scr[g] += dvm
                    dvn_scr[j, rs, cs] = _dot(wgt, dvmb)
        width = 4 * 768
        dvh = []
        for j in range(4):
            dvn = dvn_scr[j]
            dlg_ref[j:j + 1, :] += _colsum(dvn * vhat[j])
            dlb_ref[j:j + 1, :] += _colsum(dvn)
            dvh.append(dvn * lg_ref[j:j + 1, :])
        m1 = sum(jnp.sum(d, axis=1, keepdims=True) for d in dvh) / width
        m2 = sum(jnp.sum(dvh[j] * vhat[j], axis=1, keepdims=True) for j in range(4)) / width
        for j in range(4):
            dv = rstd * (dvh[j] - m1 - vhat[j] * m2)
            dpv = dv * gp_ref[4 + j].astype(F32)
            dpu = du_scr[j] * gp_ref[j].astype(F32)
            dpre_ref[4 + j] = dpv.astype(BF16)
            dpre_ref[j] = dpu.astype(BF16)
            dbin_ref[4 + j:5 + j, :] += _colsum(dpv)
            dbin_ref[j:j + 1, :] += _colsum(dpu)

        @pl.when(i == nb - 1)
        def _():
            r_i = lax.broadcasted_iota(jnp.int32, (GM_CHUNK, GM_CHUNK), 0)
            c_i = lax.broadcasted_iota(jnp.int32, (GM_CHUNK, GM_CHUNK), 1)
            for g in range(GM_GROUPS):
                dws_ref[g] = jnp.where(c_i <= r_i, dws_ref[g], 0.0)
                dbs_ref[g] = jnp.broadcast_to(jnp.sum(dbs_scr[g], axis=1, keepdims=True), (GM_CHUNK, GM_CHUNK))

    sq = pl.BlockSpec((GM_GROUPS, GM_CHUNK, GM_CHUNK), lambda i: (0, 0, 0))
    v4 = pl.BlockSpec((4, 768), lambda i: (0, 0))
    return pl.pallas_call(
        body, name=name, grid=(nb,),
        out_shape=(jax.ShapeDtypeStruct((8, T, 768), BF16),
                   jax.ShapeDtypeStruct((GM_GROUPS, GM_CHUNK, GM_CHUNK), F32),
                   jax.ShapeDtypeStruct((GM_GROUPS, GM_CHUNK, GM_CHUNK), F32),
                   jax.ShapeDtypeStruct((4, 768), F32), jax.ShapeDtypeStruct((4, 768), F32),
                   jax.ShapeDtypeStruct((8, 768), F32)),
        in_specs=[pl.BlockSpec((8, tb, 768), lambda i: (0, i, 0)),
                  pl.BlockSpec((8, tb, 768), lambda i: (0, i, 0)),
                  pl.BlockSpec((4, tb, 768), lambda i: (0, i, 0)),
                  v4, v4, sq, sq,
                  pl.BlockSpec((GM_GROUPS, GM_CHUNK, GM_GDIM), lambda i: (0, 0, 0))],
        out_specs=(pl.BlockSpec((8, tb, 768), lambda i: (0, i, 0)), sq, sq, v4, v4,
                   pl.BlockSpec((8, 768), lambda i: (0, 0))),
        scratch_shapes=[pltpu.VMEM((GM_GROUPS, GM_CHUNK, GM_GDIM), F32),
                        pltpu.VMEM((4, tb, 768), F32), pltpu.VMEM((4, tb, 768), F32)],
        compiler_params=_params("arbitrary"),
    )(pre, gp, dm, ln_g, ln_b, ws, ws_t, bsb)


def _adamw(slots, w, m, v, name, *, tr=256):
    S, R, C = slots.shape
    tr = next((t for t in (tr, tr // 2, tr // 4, tr // 8, tr // 16) if R % t == 0), R) if R > tr else R
    bc1 = 1.0 - ADAM_B1 ** ADAM_STEP
    bc2 = 1.0 - ADAM_B2 ** ADAM_STEP

    def body(s_ref, w_ref, m_ref, v_ref, g_ref, d_ref, nm_ref, nv_ref):
        g = s_ref[0].astype(F32)
        for s in range(1, S):
            g = g + s_ref[s].astype(F32)
        mn = ADAM_B1 * m_ref[...] + (1.0 - ADAM_B1) * g
        vn = ADAM_B2 * v_ref[...] + (1.0 - ADAM_B2) * (g * g)
        g_ref[...] = g
        nm_ref[...] = mn
        nv_ref[...] = vn
        d_ref[...] = -ADAM_LR * ((mn / bc1) / (jnp.sqrt(vn / bc2) + ADAM_EPS) + ADAM_WD * w_ref[...])

    spec = pl.BlockSpec((tr, C), lambda i: (i, 0))
    return pl.pallas_call(
        body, name=name, grid=(R // tr,),
        out_shape=(jax.ShapeDtypeStruct((R, C), F32),) * 4,
        in_specs=[pl.BlockSpec((S, tr, C), lambda i: (0, i, 0)), spec, spec, spec],
        out_specs=(spec,) * 4,
        compiler_params=_params("parallel"),
    )(slots, w, m, v)


def _update(slots, w, m, v, name):
    shp = w.shape
    C = shp[-1]
    R = math.prod(shp[:-1])
    outs = _adamw(slots.reshape(slots.shape[0], R, C), w.reshape(R, C), m.reshape(R, C), v.reshape(R, C), name)
    return tuple(o.reshape(shp) for o in outs)


def kernel(x, c, ada_w, ada_b, norm_pre, norm_post, ffn_w_in, ffn_w_out, hg_w_in, hg_w_out, hg_out_norm, hg_lb, gm_w_in, gm_b_in, gm_ln_g, gm_ln_b, gm_w_s, gm_b_s, gm_w_out, loss_target, m_ada_w, m_ada_b, m_norm_pre, m_norm_post, m_ffn_w_in, m_ffn_w_out, m_hg_w_in, m_hg_w_out, m_hg_out_norm, m_hg_lb, m_gm_w_in, m_gm_b_in, m_gm_ln_g, m_gm_ln_b, m_gm_w_s, m_gm_b_s, m_gm_w_out, v_ada_w, v_ada_b, v_norm_pre, v_norm_post, v_ffn_w_in, v_ffn_w_out, v_hg_w_in, v_hg_w_out, v_hg_out_norm, v_hg_lb, v_gm_w_in, v_gm_b_in, v_gm_ln_g, v_gm_ln_b, v_gm_w_s, v_gm_b_s, v_gm_w_out):
    me = 4 * lax.axis_index("x") + 2 * lax.axis_index("y") + lax.axis_index("c")
    T = x.shape[1]
    x0 = x.reshape(T, D_MODEL)
    target = loss_target.reshape(T, D_MODEL)
    n_ada = ada_w.shape[-1]

    pack = jnp.concatenate([
        c.reshape(8, 128), norm_pre.reshape(6, 128), norm_post.reshape(6, 128),
        gm_b_in.reshape(6, 128), gm_ln_g.reshape(3, 128), gm_ln_b.reshape(3, 128)], axis=0)
    packs = _all_gather(pack, "gather_small")
    c_all = packs[:, 0:8].reshape(NDEV, D_MODEL)
    npre = packs[:, 8:14].reshape(NDEV, 2, 3, 128).transpose(1, 2, 0, 3).reshape(2, 3, D_MODEL)
    npost = packs[:, 14:20].reshape(NDEV, 2, 3, 128).transpose(1, 2, 0, 3).reshape(2, 3, D_MODEL)
    b_in = packs[:, 20:26].reshape(NDEV, 1, 768)
    ln_g = packs[:, 26:29].reshape(4, 768)
    ln_b = packs[:, 29:32].reshape(4, 768)

    ada_b_mine = lax.dynamic_slice_in_dim(ada_b, me * n_ada, n_ada, axis=1).reshape(2, 1, n_ada)
    mod_cols = _ada_fwd(c_all, ada_w, ada_b_mine, "ada_fwd")
    mod_all = _all_gather(mod_cols, "gather_mod")
    mod = lax.dynamic_index_in_dim(mod_all, me, axis=2, keepdims=False)
    mod = mod.transpose(1, 0, 2).reshape(2, 9, 1, D_MODEL)

    sh_fi, sh_fo = ffn_w_in.astype(BF16), ffn_w_out.astype(BF16)
    sh_hi, sh_ho = hg_w_in[0].astype(BF16), hg_w_out[0].astype(BF16)
    sh_mi, sh_mo = gm_w_in[0].astype(BF16), gm_w_out[0].astype(BF16)
    n_ff = sh_fi.shape[-1]
    w_fi = {(0, 0): _all_gather(sh_fi[0, 0], "gather_ffn_in_first")}
    w_fo = {}
    riders = {"l0s0": [sh_fo[0, 0], sh_hi, sh_ho, sh_mi, sh_mo], "l0s1": [sh_fi[0, 1], sh_fo[0, 1]],
              "l0s2": [sh_fi[1, 0], sh_fo[1, 0]], "l1s0": [sh_fi[1, 1], sh_fo[1, 1]]}

    sm = jax.nn.softmax(hg_lb, axis=0)
    lb0 = sm[0:1]
    on = hg_out_norm.reshape(1, HG_HEAD)
    tril = jnp.tril(jnp.ones((GM_CHUNK, GM_CHUNK), F32))
    ws = gm_w_s[0] * tril[None]
    ws_t = ws.transpose(0, 2, 1)
    bsb = jnp.broadcast_to(gm_b_s[0][:, :, None], (GM_GROUPS, GM_CHUNK, GM_GDIM))

    res_ws = (0.5, 1.0, 0.5)

    def vecs(i, s):
        return (npre[i, s].reshape(1, D_MODEL), npost[i, s].reshape(1, D_MODEL),
                mod[i, 3 * s], mod[i, 3 * s + 1], mod[i, 3 * s + 2])

    def ffn_weights(i, f):
        return w_fi[i, f], w_fo[i, f].reshape(4, n_ff, D_MODEL)

    order = [(i, s) for i in range(2) for s in range(3)]
    saved = {}
    xs = x0
    pre_g, _, shift, scale, _ = vecs(0, 0)
    h = _prenorm_fwd(xs, pre_g, scale, shift, "prenorm_l0s0")
    for pos, (i, s) in enumerate(order):
        tag = f"l{i}s{s}"
        _, post_g, _, _, gate = vecs(i, s)
        rider = _Exchange("gather", riders[tag]) if tag in riders else None
        if s != 1:
            ab, a, got = _ffn_in(h, w_fi[i, s // 2], "ffn_in_" + tag, exchange=rider)
            extra = (ab, a)
            if tag == "l0s0":
                w_fo[0, 0], w_hi, w_ho, w_mi, w_mo = got
                w_ho = w_ho.reshape(1, D_MODEL, D_MODEL)
                w_mo = w_mo.reshape(4, 768, D_MODEL)
            elif tag == "l0s2":
                w_fi[1, 0], w_fo[1, 0] = got
            elif tag == "l1s0":
                w_fi[1, 1], w_fo[1, 1] = got
            wo = ffn_weights(i, s // 2)[1]
        elif i == 0:
            proj, w_fi[0, 1], w_fo[0, 1] = _mm_blocks(h, w_hi, "hg_in", out_dtype=F32, exchange=rider)
            o, og, states = _hgrn_fwd(proj, lb0, on, "hg_mix")
            a, wo = og.reshape(1, T, D_MODEL), w_ho
            extra = (proj, o, og, states)
        else:
            pre, gp = _mm_blocks(h, w_mi, "gm_in", bias=b_in, gelu=True)
            a = _gm_spatial_fwd(pre, ln_g, ln_b, ws, bsb, "gm_mix")
            wo = w_mo
            extra = (pre, gp, a)
        if pos + 1 < len(order):
            npre_g, _, nshift, nscale, _ = vecs(*order[pos + 1])
            y, x_next, h_next = _out_proj(a, wo, xs, post_g, gate, res_ws[s], (npre_g, nscale, nshift), "out_" + tag)
            saved[tag] = (xs, h, y) + extra
            xs, h = x_next, h_next
        else:
            dx, dy, dgate, dpost, loss_part = _out_proj_last(a, wo, xs, post_g, gate, res_ws[s], target, "out_" + tag)
            saved[tag] = (xs, h, None) + extra
    loss = lax.psum(loss_part[0, 0], ("x", "y", "c"))

    slots = {}
    d_npre = [[None] * 3, [None] * 3]
    d_npost = [[None] * 3, [None] * 3]
    d_mod = [[None] * 9, [None] * 9]
    for pos in reversed(range(len(order))):
        i, s = order[pos]
        tag = f"l{i}s{s}"
        pre_g, _, _, scale, _ = vecs(i, s)
        xin, h = saved[tag][:2]
        dy1 = dy.reshape(1, T, D_MODEL)
        if s != 1:
            w_in, wo = ffn_weights(i, s // 2)
            ab, g = saved[tag][3:]
            dz = _ffn_dgate(dy, wo, ab, "ffn_dgate_" + tag)
            g_out = _mm_wgrad(g, dy1, "ffn_out_wgrad_" + tag).reshape(NDEV, n_ff // 2, D_MODEL)
            g_in = _mm_wgrad(h.reshape(1, T, D_MODEL), dz, "ffn_in_wgrad_" + tag)
        elif i == 0:
            proj, o, og, states = saved[tag][3:]
            dog = _mm_blocks(dy, w_ho, "hg_out_dgrad", transpose_w=True)[0]
            g_out = _mm_wgrad(og.reshape(1, T, D_MODEL), dy1, "hg_out_wgrad").reshape(NDEV, 128, D_MODEL)
            dz, d_lb0, d_on = _hgrn_bwd(proj, o, dog, states, lb0, on, "hg_mix_bwd")
            w_in = w_hi
            g_in = _mm_wgrad(h.reshape(1, T, D_MODEL), dz, "hg_in_wgrad")
        else:
            pre, gp, sp = saved[tag][3:]
            dm = _mm_blocks(dy, w_mo, "gm_out_dgrad", transpose_w=True)
            g_out = _mm_wgrad(sp, dy1, "gm_out_wgrad").reshape(NDEV, 384, D_MODEL)
            dz, d_ws, d_bs, d_lg, d_lbias, d_bin = _gm_spatial_bwd(pre, gp, dm, ln_g, ln_b, ws, ws_t, bsb, "gm_mix_bwd")
            w_in = w_mi
            g_in = _mm_wgrad(h.reshape(1, T, D_MODEL), dz, "gm_in_wgrad")
        d_npost[i][s] = dpost
        d_mod[i][3 * s + 2] = dgate
        rider = _Exchange("scatter", [g_in, g_out])
        if pos > 0:
            pi, ps = order[pos - 1]
            _, ppost_g, _, _, pgate = vecs(pi, ps)
            prev = (saved[f"l{pi}s{ps}"][2], ppost_g, pgate, res_ws[ps])
            dx, dshift, dscale, dpre_g, dy, dgate, dpost, r_in, r_out = _in_grad(
                dz, w_in, dx, xin, pre_g, scale, prev, "in_grad_" + tag, exchange=rider)
        else:
            dx, dshift, dscale, dpre_g, r_in, r_out = _in_grad(
                dz, w_in, dx, xin, pre_g, scale, None, "in_grad_" + tag, exchange=rider)
        slots[tag] = (r_in, r_out)
        d_npre[i][s] = dpre_g
        d_mod[i][3 * s], d_mod[i][3 * s + 1] = dshift, dscale
    grad_x = dx.reshape(x.shape)

    ffn_tags = ["l0s0", "l0s2", "l1s0", "l1s2"]
    s_fi = jnp.stack([slots[t][0] for t in ffn_tags], axis=1)
    s_fo = jnp.stack([slots[t][1] for t in ffn_tags], axis=1)
    (s_hi, s_ho), (s_mi, s_mo) = slots["l0s1"], slots["l1s1"]
    s_hi, s_ho, s_mi, s_mo = s_hi[:, None], s_ho[:, None], s_mi[:, None], s_mo[:, None]

    gmod = jnp.stack([jnp.concatenate(d_mod[i], axis=0) for i in range(2)])
    d_sm = lb0 * d_lb0
    d_hg_lb = jnp.concatenate([d_sm, jnp.zeros((2, D_MODEL), F32)], axis=0) - sm * d_sm
    small = [gmod, jnp.stack([jnp.concatenate(r, axis=0) for r in d_npre]),
             jnp.stack([jnp.concatenate(r, axis=0) for r in d_npost]),
             d_on, d_hg_lb, d_bin, d_lg, d_lbias, d_ws, d_bs[:, :, 0]]
    sizes = [a.size for a in small]
    flat = jnp.concatenate([a.reshape(-1) for a in small])
    rows = -(-flat.size // (8 * 128)) * 8
    flat = jnp.pad(flat, (0, rows * 128 - flat.size)).reshape(rows, 128)
    flats = _all_gather(flat, "gather_small_grads").reshape(NDEV, rows * 128)
    parts, off = [], 0
    for a, n in zip(small, sizes):
        parts.append(flats[:, off:off + n].reshape((NDEV,) + a.shape))
        off += n
    p_mod, p_npre, p_npost, p_on, p_lb, p_bin, p_lg, p_lbias, p_ws, p_bs = parts

    def mine(p, width):
        return lax.dynamic_slice_in_dim(p, me * width, width, axis=p.ndim - 1)

    gmod_cols = mine(p_mod.reshape(NDEV, 2, 9 * D_MODEL), n_ada).transpose(1, 0, 2)
    g_ada_w = _ada_bwd(jnp.pad(c_all.T, ((0, 0), (0, 120))), jnp.pad(gmod_cols, ((0, 0), (0, 120), (0, 0))), "ada_bwd")

    out = {}
    out["ada_w"] = _update(g_ada_w[None], ada_w, m_ada_w, v_ada_w, "adamw_ada_w")
    out["ada_b"] = _update(p_mod.reshape(NDEV, 2, 9 * D_MODEL), ada_b, m_ada_b, v_ada_b, "adamw_ada_b")
    out["norm_pre"] = _update(mine(p_npre, 128), norm_pre, m_norm_pre, v_norm_pre, "adamw_norm_pre")
    out["norm_post"] = _update(mine(p_npost, 128), norm_post, m_norm_post, v_norm_post, "adamw_norm_post")
    out["ffn_w_in"] = _update(s_fi.reshape((NDEV,) + ffn_w_in.shape), ffn_w_in, m_ffn_w_in, v_ffn_w_in, "adamw_ffn_in")
    out["ffn_w_out"] = _update(s_fo.reshape((NDEV,) + ffn_w_out.shape), ffn_w_out, m_ffn_w_out, v_ffn_w_out, "adamw_ffn_out")
    out["hg_w_in"] = _update(s_hi, hg_w_in, m_hg_w_in, v_hg_w_in, "adamw_hg_in")
    out["hg_w_out"] = _update(s_ho, hg_w_out, m_hg_w_out, v_hg_w_out, "adamw_hg_out")
    out["hg_out_norm"] = _update(p_on, hg_out_norm, m_hg_out_norm, v_hg_out_norm, "adamw_hg_norm")
    out["hg_lb"] = _update(p_lb, hg_lb, m_hg_lb, v_hg_lb, "adamw_hg_lb")
    out["gm_w_in"] = _update(s_mi, gm_w_in, m_gm_w_in, v_gm_w_in, "adamw_gm_in")
    out["gm_b_in"] = _update(mine(p_bin.reshape(NDEV, 1, 8 * 768), 768), gm_b_in, m_gm_b_in, v_gm_b_in, "adamw_gm_b_in")
    out["gm_ln_g"] = _update(mine(p_lg.reshape(NDEV, 1, 4 * 768), 384), gm_ln_g, m_gm_ln_g, v_gm_ln_g, "adamw_gm_ln_g")
    out["gm_ln_b"] = _update(mine(p_lbias.reshape(NDEV, 1, 4 * 768), 384), gm_ln_b, m_gm_ln_b, v_gm_ln_b, "adamw_gm_ln_b")
    out["gm_w_s"] = _update(p_ws[:, None], gm_w_s, m_gm_w_s, v_gm_w_s, "adamw_gm_w_s")
    out["gm_b_s"] = _update(p_bs[:, None], gm_b_s, m_gm_b_s, v_gm_b_s, "adamw_gm_b_s")
    out["gm_w_out"] = _update(s_mo, gm_w_out, m_gm_w_out, v_gm_w_out, "adamw_gm_out")

    names = ["ada_w", "ada_b", "norm_pre", "norm_post", "ffn_w_in", "ffn_w_out", "hg_w_in", "hg_w_out",
             "hg_out_norm", "hg_lb", "gm_w_in", "gm_b_in", "gm_ln_g", "gm_ln_b", "gm_w_s", "gm_b_s", "gm_w_out"]
    return (loss, grad_x, *[out[n][0] for n in names], *[out[n][1] for n in names],
            *[out[n][2] for n in names], *[out[n][3] for n in names])
```

```python
import math

import jax
import jax.numpy as jnp
from jax import lax
from jax.experimental import pallas as pl
from jax.experimental.pallas import tpu as pltpu

F32 = jnp.float32
BF16 = jnp.bfloat16
NDEV = 8
D_MODEL = 1024
EPS = 1e-6
HG_CHUNK = 64
HG_HEAD = 128
HG_HEADS = 8
GM_CHUNK = 128
GM_GDIM = 384
GM_GROUPS = 8
ADAM_LR = 0.001
ADAM_B1 = 0.9
ADAM_B2 = 0.999
ADAM_EPS = 1e-08
ADAM_WD = 0.01
ADAM_STEP = 10
VMEM_LIMIT = 56 * 2 ** 20

NN = (((1,), (0,)), ((), ()))
NT = (((1,), (1,)), ((), ()))
TN = (((0,), (0,)), ((), ()))
MESH = pl.DeviceIdType.MESH
ANY = pl.BlockSpec(memory_space=pl.ANY)


def _dot(a, b, dims=NN, precision=None):
    return lax.dot_general(a, b, dims, preferred_element_type=F32, precision=precision)


def _params(*sem):
    return pltpu.CompilerParams(dimension_semantics=sem, vmem_limit_bytes=VMEM_LIMIT)


def _sigmoid(x):
    return 1.0 / (1.0 + jnp.exp(-x))


def _sigmoid_t(x):
    return 0.5 * jnp.tanh(0.5 * x) + 0.5


def _gelu_and_grad(x):
    c = math.sqrt(2.0 / math.pi)
    x2 = x * x
    t = jnp.tanh(x * (c + (c * 0.044715) * x2))
    hx = 0.5 * x
    p = 1.0 + t
    return hx * p, 0.5 * p + hx * (1.0 - t * t) * (c + (3.0 * c * 0.044715) * x2)


def _colsum(x):
    return jnp.sum(x, axis=0, keepdims=True)


def _rowmean(x):
    return jnp.mean(x, axis=-1, keepdims=True)


def _all_gather(shard, name):
    def body(x_ref, out_ref, send_sems, recv_sems, local_sem):
        x, y, c = lax.axis_index("x"), lax.axis_index("y"), lax.axis_index("c")
        me, sibling = (x, y, c), (x, y, 1 - c)
        chips = [(1 - x, y), (x, 1 - y), (1 - x, 1 - y)]

        def slot(p):
            return out_ref.at[4 * p[0] + 2 * p[1] + p[2]]

        def copy(k, block, to, src=None):
            return pltpu.make_async_remote_copy(
                src_ref=slot(block) if src is None else src, dst_ref=slot(block),
                send_sem=send_sems.at[k], recv_sem=recv_sems.at[k],
                device_id=to, device_id_type=MESH)

        mine = pltpu.make_async_copy(x_ref, slot(me), local_sem)
        mine.start()
        first = [copy(0, me, sibling, src=x_ref)]
        first += [copy(1 + j, me, (*chip, c), src=x_ref) for j, chip in enumerate(chips)]
        for cp in first:
            cp.start()
        passed = [copy(4 + j, (*chip, c), sibling) for j, chip in enumerate(chips)]
        for j, chip in enumerate(chips):
            copy(1 + j, (*chip, c), me).wait_recv()
            passed[j].start()
        copy(0, sibling, me).wait_recv()
        for j, chip in enumerate(chips):
            copy(4 + j, (*chip, 1 - c), me).wait_recv()
        for cp in first + passed:
            cp.wait_send()
        mine.wait()

    return pl.pallas_call(
        body, name=name,
        out_shape=jax.ShapeDtypeStruct((NDEV,) + shard.shape, shard.dtype),
        in_specs=[ANY], out_specs=ANY,
        scratch_shapes=[pltpu.SemaphoreType.DMA((7,)), pltpu.SemaphoreType.DMA((7,)),
                        pltpu.SemaphoreType.DMA(())],
    )(shard)


class _Exchange:
    def __init__(self, kind, arrays):
        self.gather = kind == "gather"
        self.arrays = list(arrays)
        self.n = n = len(self.arrays)
        self.out_shape = [jax.ShapeDtypeStruct(((NDEV,) + a.shape) if self.gather else a.shape, a.dtype)
                          for a in self.arrays]
        self.scratch = [pltpu.SemaphoreType.DMA((n, NDEV - 1)), pltpu.SemaphoreType.DMA((n, NDEV - 1)),
                        pltpu.SemaphoreType.DMA((n,))]

    def _copies(self, in_refs, out_refs, sems):
        send_sems, recv_sems, local_sems = sems
        x, y, c = lax.axis_index("x"), lax.axis_index("y"), lax.axis_index("c")
        me = 4 * x + 2 * y + c
        peers = [(1 - x if k & 4 else x, 1 - y if k & 2 else y, 1 - c if k & 1 else c) for k in range(1, NDEV)]
        local, send, recv = [], [], []
        for a in range(self.n):
            src = (lambda pid, a=a: in_refs[a]) if self.gather else (lambda pid, a=a: in_refs[a].at[pid])
            local.append(pltpu.make_async_copy(src(me), out_refs[a].at[me], local_sems.at[a]))
            for k, p in enumerate(peers):
                pid = 4 * p[0] + 2 * p[1] + p[2]
                for lst, slot in ((send, me), (recv, pid)):
                    lst.append(pltpu.make_async_remote_copy(
                        src_ref=src(pid), dst_ref=out_refs[a].at[slot],
                        send_sem=send_sems.at[a, k], recv_sem=recv_sems.at[a, k],
                        device_id=p, device_id_type=MESH))
        return local, send, recv

    def start(self, first, in_refs, out_refs, sems):
        @pl.when(first)
        def _():
            local, send, _ = self._copies(in_refs, out_refs, sems)
            for cp in local + send:
                cp.start()

    def finish(self, last, in_refs, out_refs, sems):
        @pl.when(last)
        def _():
            local, send, recv = self._copies(in_refs, out_refs, sems)
            for cp in send:
                cp.wait_send()
            for cp in recv:
                cp.wait_recv()
            for cp in local:
                cp.wait()


def _host(exchange, n_in, n_out, body, first_last):
    if exchange is None:
        return body, [], [], [], []
    n = exchange.n

    def hosted(*refs):
        ins, refs = refs[:n_in], refs[n_in:]
        xin, refs = refs[:n], refs[n:]
        outs, refs = refs[:n_out], refs[n_out:]
        xout, refs = refs[:n], refs[n:]
        scratch, sems = refs[:len(refs) - 3], refs[len(refs) - 3:]
        first, last = first_last()
        exchange.start(first, xin, xout, sems)
        body(*ins, *outs, *scratch)
        exchange.finish(last, xin, xout, sems)

    return hosted, exchange.arrays, [ANY] * n, exchange.out_shape, exchange.scratch


def _first_last(steps):
    def at():
        i = pl.program_id(0)
        return i == 0, i == steps - 1
    return at


class _ColBlocks:
    def __init__(self, ref, width):
        self.ref, self.width = ref, width

    def _index(self, key):
        key = key if isinstance(key, tuple) else (key,)
        rows = key[1] if len(key) > 1 else slice(None)
        cols = key[2] if len(key) > 2 else slice(0, self.width)
        c0 = key[0] * self.width
        return rows, slice(c0 + cols.start, c0 + cols.stop)

    def __getitem__(self, key):
        return self.ref[self._index(key)]

    def __setitem__(self, key, value):
        self.ref[self._index(key)] = value


def _col_chunks(width, chunk=768):
    return [slice(c, min(c + chunk, width)) for c in range(0, width, chunk)]


def _row_spec(tm, d):
    return pl.BlockSpec((tm, d), lambda m: (m, 0))


def _vec_spec(d):
    return pl.BlockSpec((1, d), lambda m: (0, 0))


def _whole_spec(w):
    nd = w.ndim
    return pl.BlockSpec(w.shape, lambda m: (0,) * nd, pipeline_mode=pl.Buffered(1))


def _mm_blocks(a, w, name, *, bias=None, out_dtype=BF16, tm=512, chunk=768, gelu=False, exchange=None):
    T, K = a.shape
    C = w.shape[0]
    tm = min(tm, T)
    n_in = 2 + (bias is not None)

    def body(*refs):
        a_ref, w_ref = refs[:2]
        av = a_ref[...]
        for cols in _col_chunks(C, chunk):
            r = _dot(av, w_ref[cols], NT)
            if bias is not None:
                r = r + refs[2][:, cols]
            if gelu:
                z, dz = _gelu_and_grad(r)
                refs[n_in][:, cols] = z.astype(BF16)
                refs[n_in + 1][:, cols] = dz.astype(BF16)
            else:
                refs[n_in][:, cols] = r.astype(out_dtype)

    in_specs = [_row_spec(tm, K), _whole_spec(w)]
    args = [a, w]
    if bias is not None:
        in_specs.append(_whole_spec(bias))
        args.append(bias)
    outs = [jax.ShapeDtypeStruct((T, C), BF16)] * 2 if gelu else [jax.ShapeDtypeStruct((T, C), out_dtype)]
    body, x_args, x_in, x_out, x_scratch = _host(exchange, n_in, len(outs), body, _first_last(T // tm))
    res = pl.pallas_call(
        body, name=name, grid=(T // tm,),
        out_shape=outs + x_out,
        in_specs=in_specs + x_in,
        out_specs=[_row_spec(tm, C)] * len(outs) + x_in,
        scratch_shapes=x_scratch,
        compiler_params=_params("arbitrary" if exchange else "parallel"),
    )(*args, *x_args)
    return res if (exchange or gelu) else res[0]


def _rms(v):
    return lax.rsqrt(_rowmean(v * v) + EPS)


def _zero_at_start(*refs):
    @pl.when(pl.program_id(0) == 0)
    def _():
        for r in refs:
            r[...] = jnp.zeros_like(r)


def _postnorm_bwd_math(dxo, yv, g, gate, res_w, dgate_ref, dpost_ref):
    r = _rms(yv)
    yh = yv * r
    dr = dxo * res_w
    dgate_ref[...] += _colsum(dr * (yh * g))
    dp = dr * gate
    dpost_ref[...] += _colsum(dp * yh)
    dyh = dp * g
    return (r * (dyh - yh * _rowmean(dyh * yh))).astype(BF16)


def _out_proj(a, w, x, post_g, gate, res_w, nxt, name, *, tm=512):
    T, ka = a.shape
    d = w.shape[1]
    tm = min(tm, T)

    def body(a_ref, w_ref, x_ref, pg_ref, gate_ref, ng_ref, nsc_ref, nsh_ref, y_ref, xn_ref, h_ref):
        y = _dot(a_ref[...], w_ref[...])
        y_ref[...] = y
        xn = x_ref[...] + res_w * gate_ref[...] * (y * _rms(y) * pg_ref[...])
        xn_ref[...] = xn
        h_ref[...] = (xn * _rms(xn) * ng_ref[...] * (1.0 + nsc_ref[...]) + nsh_ref[...]).astype(BF16)

    return pl.pallas_call(
        body, name=name, grid=(T // tm,),
        out_shape=(jax.ShapeDtypeStruct((T, d), F32), jax.ShapeDtypeStruct((T, d), F32),
                   jax.ShapeDtypeStruct((T, d), BF16)),
        in_specs=[_row_spec(tm, ka), _whole_spec(w), _row_spec(tm, d)] + [_vec_spec(d)] * 5,
        out_specs=(_row_spec(tm, d),) * 3,
        compiler_params=_params("parallel"),
    )(a, w, x, post_g, gate, *nxt)


def _out_proj_last(a, w, x, post_g, gate, res_w, target, name, *, tm=512):
    T, ka = a.shape
    d = w.shape[1]
    tm = min(tm, T)

    def body(a_ref, w_ref, x_ref, pg_ref, gate_ref, t_ref, dx_ref, dy_ref, dgate_ref, dpost_ref, l_ref):
        _zero_at_start(dgate_ref, dpost_ref, l_ref)
        y = _dot(a_ref[...], w_ref[...])
        e = x_ref[...] + res_w * gate_ref[...] * (y * _rms(y) * pg_ref[...]) - t_ref[...]
        l_ref[...] += 0.5 * jnp.sum(_rowmean(e * e), axis=0, keepdims=True)
        dx = e * (1.0 / d)
        dx_ref[...] = dx
        dy_ref[...] = _postnorm_bwd_math(dx, y, pg_ref[...], gate_ref[...], res_w, dgate_ref, dpost_ref)

    return pl.pallas_call(
        body, name=name, grid=(T // tm,),
        out_shape=(jax.ShapeDtypeStruct((T, d), F32), jax.ShapeDtypeStruct((T, d), BF16),
                   jax.ShapeDtypeStruct((1, d), F32), jax.ShapeDtypeStruct((1, d), F32),
                   jax.ShapeDtypeStruct((1, 128), F32)),
        in_specs=[_row_spec(tm, ka), _whole_spec(w), _row_spec(tm, d), _vec_spec(d), _vec_spec(d), _row_spec(tm, d)],
        out_specs=(_row_spec(tm, d), _row_spec(tm, d), _vec_spec(d), _vec_spec(d),
                   pl.BlockSpec((1, 128), lambda m: (0, 0))),
        compiler_params=_params("arbitrary"),
    )(a, w, x, post_g, gate, target)


def _in_grad(dz, w, dxo, x, pre_g, scale, prev, name, *, tm=512, exchange=None):
    T, C = dz.shape
    d = w.shape[1]
    tm = min(tm, T)
    has_prev = prev is not None
    res_w = prev[3] if has_prev else None

    def body(*refs):
        dz_ref, w_ref, dxo_ref, x_ref, g_ref, sc_ref = refs[:6]
        if has_prev:
            yp_ref, ppg_ref, pgate_ref, dx_ref, dsh_ref, dsc_ref, dg_ref, dyp_ref, dgate_ref, dpost_ref = refs[6:]
            _zero_at_start(dsh_ref, dsc_ref, dg_ref, dgate_ref, dpost_ref)
        else:
            dx_ref, dsh_ref, dsc_ref, dg_ref = refs[6:]
            _zero_at_start(dsh_ref, dsc_ref, dg_ref)
        dh = _dot(dz_ref[...], w_ref[...])
        xv = x_ref[...]
        r = _rms(xv)
        xh = xv * r
        dsh_ref[...] += _colsum(dh)
        dsc_ref[...] += _colsum(dh * (xh * g_ref[...]))
        dn = dh * (1.0 + sc_ref[...])
        dg_ref[...] += _colsum(dn * xh)
        dxh = dn * g_ref[...]
        dx = dxo_ref[...] + r * (dxh - xh * _rowmean(dxh * xh))
        dx_ref[...] = dx
        if has_prev:
            dyp_ref[...] = _postnorm_bwd_math(dx, yp_ref[...], ppg_ref[...], pgate_ref[...], res_w,
                                               dgate_ref, dpost_ref)

    vec = jax.ShapeDtypeStruct((1, d), F32)
    in_specs = [_row_spec(tm, C), _whole_spec(w), _row_spec(tm, d), _row_spec(tm, d), _vec_spec(d), _vec_spec(d)]
    out_shape = [jax.ShapeDtypeStruct((T, d), F32), vec, vec, vec]
    out_specs = [_row_spec(tm, d), _vec_spec(d), _vec_spec(d), _vec_spec(d)]
    args = [dz, w, dxo, x, pre_g, scale]
    if has_prev:
        in_specs += [_row_spec(tm, d), _vec_spec(d), _vec_spec(d)]
        out_shape += [jax.ShapeDtypeStruct((T, d), BF16), vec, vec]
        out_specs += [_row_spec(tm, d), _vec_spec(d), _vec_spec(d)]
        args += list(prev[:3])
    body, x_args, x_in, x_out, x_scratch = _host(exchange, len(args), len(out_shape), body, _first_last(T // tm))
    return pl.pallas_call(
        body, name=name, grid=(T // tm,),
        out_shape=out_shape + x_out, in_specs=in_specs + x_in, out_specs=out_specs + x_in,
        scratch_shapes=x_scratch,
        compiler_params=_params("arbitrary"),
    )(*args, *x_args)


def _mm_wgrad(x, y, name, *, xw=None, yw=None, tt=2048):
    T, P = x.shape
    Q = y.shape[1]
    xw, yw = xw or P, yw or Q
    jx, jy = P // xw, Q // yw
    assert jx == 1 or jy == 1
    tt = min(tt, T)
    nt = T // tt

    def body(x_ref, y_ref, o_ref, acc_ref):
        t = pl.program_id(1)

        @pl.when(t == 0)
        def _():
            acc_ref[...] = jnp.zeros_like(acc_ref)

        acc_ref[...] += _dot(x_ref[...], y_ref[...], TN)

        @pl.when(t == nt - 1)
        def _():
            o_ref[...] = acc_ref[...].astype(BF16)

    if jy > 1:
        out_shape = jax.ShapeDtypeStruct((jy, P, yw), BF16)
        out_spec = pl.BlockSpec((None, P, yw), lambda j, t: (j, 0, 0))
    else:
        out_shape = jax.ShapeDtypeStruct((P, Q), BF16)
        out_spec = pl.BlockSpec((xw, Q), lambda j, t: (j, 0))
    return pl.pallas_call(
        body, name=name, grid=(max(jx, jy), nt),
        out_shape=out_shape,
        in_specs=[pl.BlockSpec((tt, xw), (lambda j, t: (t, j)) if jx > 1 else (lambda j, t: (t, 0))),
                  pl.BlockSpec((tt, yw), (lambda j, t: (t, j)) if jy > 1 else (lambda j, t: (t, 0)))],
        out_specs=out_spec,
        scratch_shapes=[pltpu.VMEM((xw, yw), F32)],
        compiler_params=_params("parallel", "arbitrary"),
    )(x, y)


def _ffn_in(h, wt, name, *, tm=512, exchange=None):
    T, K = h.shape
    F = wt.shape[0] // 2
    tm = min(tm, T)

    def body(h_ref, w_ref, pq_ref, g_ref):
        hh = h_ref[...]
        for cols in _col_chunks(F):
            hi = slice(F + cols.start, F + cols.stop)
            a = _dot(hh, w_ref[cols], NT)
            b = _dot(hh, w_ref[hi], NT)
            s = _sigmoid_t(a)
            silu = a * s
            pq_ref[:, cols] = (b * (s * (1.0 + a * (1.0 - s)))).astype(BF16)
            pq_ref[:, hi] = silu.astype(BF16)
            g_ref[:, cols] = (silu * b).astype(BF16)

    body, x_args, x_in, x_out, x_scratch = _host(exchange, 2, 2, body, _first_last(T // tm))
    res = pl.pallas_call(
        body, name=name, grid=(T // tm,),
        out_shape=[jax.ShapeDtypeStruct((T, 2 * F), BF16), jax.ShapeDtypeStruct((T, F), BF16)] + x_out,
        in_specs=[_row_spec(tm, K), _whole_spec(wt)] + x_in,
        out_specs=[_row_spec(tm, 2 * F), _row_spec(tm, F)] + x_in,
        scratch_shapes=x_scratch,
        compiler_params=_params("arbitrary" if exchange else "parallel"),
    )(h, wt, *x_args)
    return res[0], res[1], res[2:]


def _ffn_dgate(dy, w_out, pq, name, *, tm=512):
    T, N = dy.shape
    F = w_out.shape[0]
    tm = min(tm, T)

    def body(dy_ref, w_ref, pq_ref, dz_ref):
        dyv = dy_ref[...]
        for cols in _col_chunks(F):
            hi = slice(F + cols.start, F + cols.stop)
            dg = _dot(dyv, w_ref[cols], NT)
            dz_ref[:, cols] = (dg * pq_ref[:, cols].astype(F32)).astype(BF16)
            dz_ref[:, hi] = (dg * pq_ref[:, hi].astype(F32)).astype(BF16)

    return pl.pallas_call(
        body, name=name, grid=(T // tm,),
        out_shape=jax.ShapeDtypeStruct((T, 2 * F), BF16),
        in_specs=[_row_spec(tm, N), _whole_spec(w_out), _row_spec(tm, 2 * F)],
        out_specs=_row_spec(tm, 2 * F),
        compiler_params=_params("parallel"),
    )(dy, w_out, pq)


def _prenorm_fwd(x, pre_g, scale, shift, name, *, tm=512):
    T, d = x.shape
    tm = min(tm, T)

    def body(x_ref, g_ref, sc_ref, sh_ref, h_ref):
        xv = x_ref[...]
        h_ref[...] = (xv * _rms(xv) * g_ref[...] * (1.0 + sc_ref[...]) + sh_ref[...]).astype(BF16)

    return pl.pallas_call(
        body, name=name, grid=(T // tm,),
        out_shape=jax.ShapeDtypeStruct((T, d), BF16),
        in_specs=[_row_spec(tm, d), _vec_spec(d), _vec_spec(d), _vec_spec(d)],
        out_specs=_row_spec(tm, d),
        compiler_params=_params("parallel"),
    )(x, pre_g, scale, shift)


def _ada_fwd(c_all, w, b, name):
    L, K, n = w.shape

    def body(c_ref, w_ref, b_ref, o_ref):
        cv = c_ref[...]
        cond = cv * _sigmoid(cv)
        for l in range(L):
            o_ref[l] = _dot(cond, w_ref[l], precision=lax.Precision.HIGHEST) + b_ref[l]

    return pl.pallas_call(
        body, name=name,
        out_shape=jax.ShapeDtypeStruct((L, NDEV, n), F32),
        compiler_params=pltpu.CompilerParams(vmem_limit_bytes=VMEM_LIMIT),
    )(c_all, w, b)


def _ada_bwd(c_all_t, gmod, name):
    L, _, n = gmod.shape
    K = c_all_t.shape[0]

    def body(c_ref, g_ref, o_ref):
        cv = c_ref[...]
        cond = cv * _sigmoid(cv)
        for l in range(L):
            o_ref[l] = _dot(cond, g_ref[l], precision=lax.Precision.HIGHEST)

    return pl.pallas_call(
        body, name=name,
        out_shape=jax.ShapeDtypeStruct((L, K, n), F32),
        compiler_params=pltpu.CompilerParams(vmem_limit_bytes=VMEM_LIMIT),
    )(c_all_t, gmod)


def _tri(n, upper=False, block=None):
    r = lax.broadcasted_iota(jnp.int32, (n, n), 0)
    c = lax.broadcasted_iota(jnp.int32, (n, n), 1)
    m = (c >= r) if upper else (c <= r)
    if block is not None:
        m = m & ((r // block) == (c // block))
    return m.astype(F32)


def _hgrn_gates(proj_ref, lb_ref, jh):
    proj_ref = _ColBlocks(proj_ref, 512)
    lb = lb_ref[:, 512 * jh:512 * (jh + 1)]
    qp = proj_ref[jh]
    fx = proj_ref[2 + jh]
    sq = _sigmoid(qp)
    sig = _sigmoid(fx)
    f = lb + (1.0 - lb) * sig
    k = (1.0 - lb) * (1.0 - sig)
    return lb, qp, sq, sig, f, k


def _hgrn_fwd(proj, lb, out_norm, name, *, tb=128, exchange=None):
    T = proj.shape[0]
    tb = min(tb, T)
    nc = tb // HG_CHUNK
    lmat = _tri(tb, block=HG_CHUNK)

    def body(proj_ref, lb_ref, on_ref, l_ref, o_ref, og_ref, st_ref, s_scr, b_scr):
        @pl.when(pl.program_id(0) == 0)
        def _():
            s_scr[...] = jnp.zeros_like(s_scr)

        r_i = lax.broadcasted_iota(jnp.int32, (HG_CHUNK, HG_CHUNK), 0)
        c_i = lax.broadcasted_iota(jnp.int32, (HG_CHUNK, HG_CHUNK), 1)
        causal = c_i <= r_i
        onv = on_ref[...]
        blocks = _ColBlocks(proj_ref, 512)
        for jh in range(2):
            lbv, qp, sq, sig, f, k = _hgrn_gates(proj_ref, lb_ref, jh)
            q = qp * sq
            b_scr[...] = _dot(l_ref[...], jnp.log(f), precision=lax.Precision.HIGHEST)
            v = blocks[4 + jh]
            gp = blocks[6 + jh]
            gs = gp * _sigmoid(gp)
            for hh in range(4):
                hd = 4 * jh + hh
                cs = slice(HG_HEAD * hh, HG_HEAD * (hh + 1))
                for ci in range(nc):
                    r0 = HG_CHUNK * ci
                    rs = slice(r0, r0 + HG_CHUNK)
                    bc = b_scr[rs, cs]
                    bm = b_scr[r0 + HG_CHUNK // 2 - 1:r0 + HG_CHUNK // 2, cs]
                    bl = b_scr[r0 + HG_CHUNK - 1:r0 + HG_CHUNK, cs]
                    qc, kc, vc = q[rs, cs], k[rs, cs], v[rs, cs].astype(BF16)
                    qe = (qc * jnp.exp(bc)).astype(BF16)
                    qt = (qc * jnp.exp(bc - bm)).astype(BF16)
                    kt = (kc * jnp.exp(bm - bc)).astype(BF16)
                    kd = (kc * jnp.exp(bl - bc)).astype(BF16)
                    st = s_scr[hd]
                    stb = st.astype(BF16)
                    st_ref[ci, hd] = stb
                    a = jnp.where(causal, _dot(qt, kt, NT), 0.0).astype(BF16)
                    o = _dot(qe, stb, NT) + _dot(a, vc)
                    s_scr[hd] = st * jnp.exp(bl) + _dot(vc, kd, TN)
                    o_ref[rs, HG_HEAD * hd:HG_HEAD * (hd + 1)] = o
                    r = lax.rsqrt(_rowmean(o * o) + EPS)
                    og_ref[rs, HG_HEAD * hd:HG_HEAD * (hd + 1)] = (o * r * onv * gs[rs, cs]).astype(BF16)

    body, x_args, x_in, x_out, x_scratch = _host(exchange, 4, 3, body, _first_last(T // tb))
    res = pl.pallas_call(
        body, name=name, grid=(T // tb,),
        out_shape=[jax.ShapeDtypeStruct((T, D_MODEL), F32), jax.ShapeDtypeStruct((T, D_MODEL), BF16),
                   jax.ShapeDtypeStruct((T // HG_CHUNK, HG_HEADS, HG_HEAD, HG_HEAD), BF16)] + x_out,
        in_specs=[pl.BlockSpec((tb, 4 * D_MODEL), lambda i: (i, 0)),
                  pl.BlockSpec((1, D_MODEL), lambda i: (0, 0)),
                  pl.BlockSpec((1, HG_HEAD), lambda i: (0, 0)),
                  pl.BlockSpec((tb, tb), lambda i: (0, 0))] + x_in,
        out_specs=[pl.BlockSpec((tb, D_MODEL), lambda i: (i, 0)),
                   pl.BlockSpec((tb, D_MODEL), lambda i: (i, 0)),
                   pl.BlockSpec((nc, HG_HEADS, HG_HEAD, HG_HEAD), lambda i: (i, 0, 0, 0))] + x_in,
        scratch_shapes=[pltpu.VMEM((HG_HEADS, HG_HEAD, HG_HEAD), F32), pltpu.VMEM((tb, 512), F32)] + x_scratch,
        compiler_params=_params("arbitrary"),
    )(proj, lb, out_norm, lmat, *x_args)
    return res[0], res[1], res[2], res[3:]


def _hgrn_bwd(proj, o, dog, states, lb, out_norm, name, *, tb=128):
    T = proj.shape[0]
    tb = min(tb, T)
    nc = tb // HG_CHUNK
    nb = T // tb
    lmat = _tri(tb, block=HG_CHUNK)
    umat = _tri(tb, upper=True, block=HG_CHUNK)

    def body(proj_ref, o_ref, dog_ref, st_ref, lb_ref, on_ref, l_ref, u_ref,
             dproj_ref, dlb_ref, don_ref, ds_scr, b_scr, dq_scr, dk_scr, dv_scr, dg_scr, db_scr):
        @pl.when(pl.program_id(0) == 0)
        def _():
            ds_scr[...] = jnp.zeros_like(ds_scr)
            dlb_ref[...] = jnp.zeros_like(dlb_ref)
            don_ref[...] = jnp.zeros_like(don_ref)

        r_i = lax.broadcasted_iota(jnp.int32, (HG_CHUNK, HG_CHUNK), 0)
        c_i = lax.broadcasted_iota(jnp.int32, (HG_CHUNK, HG_CHUNK), 1)
        causal = c_i <= r_i
        causal_t = r_i <= c_i
        last_row = lax.broadcasted_iota(jnp.int32, (HG_CHUNK, HG_HEAD), 0) == HG_CHUNK - 1
        onv = on_ref[...]
        don_acc = jnp.zeros((1, HG_HEAD), F32)
        blocks = _ColBlocks(proj_ref, 512)
        dproj_ref = _ColBlocks(dproj_ref, 512)
        for jh in range(2):
            lbv, qp, sq, sig, f, k = _hgrn_gates(proj_ref, lb_ref, jh)
            q = qp * sq
            b_scr[...] = _dot(l_ref[...], jnp.log(f), precision=lax.Precision.HIGHEST)
            v = blocks[4 + jh]
            gp = blocks[6 + jh]
            sg = _sigmoid(gp)
            for hh in range(4):
                hd = 4 * jh + hh
                cs = slice(HG_HEAD * hh, HG_HEAD * (hh + 1))
                hs = slice(HG_HEAD * hd, HG_HEAD * (hd + 1))
                for ci in reversed(range(nc)):
                    r0 = HG_CHUNK * ci
                    rs = slice(r0, r0 + HG_CHUNK)
                    oc = o_ref[rs, hs]
                    r = lax.rsqrt(_rowmean(oc * oc) + EPS)
                    oh = oc * r
                    gc, sgc = gp[rs, cs], sg[rs, cs]
                    dogc = dog_ref[rs, hs].astype(F32)
                    don = dogc * (gc * sgc)
                    dg_scr[rs, cs] = dogc * (oh * onv) * (sgc * (1.0 + gc * (1.0 - sgc)))
                    don_acc += _colsum(don * oh)
                    donh = don * onv
                    do = (r * (donh - oh * _rowmean(donh * oh))).astype(BF16)
                    bc = b_scr[rs, cs]
                    bm = b_scr[r0 + HG_CHUNK // 2 - 1:r0 + HG_CHUNK // 2, cs]
                    bl = b_scr[r0 + HG_CHUNK - 1:r0 + HG_CHUNK, cs]
                    qc, kc, vc = q[rs, cs], k[rs, cs], v[rs, cs].astype(BF16)
                    e_b, e_q, e_k, e_d = jnp.exp(bc), jnp.exp(bc - bm), jnp.exp(bm - bc), jnp.exp(bl - bc)
                    qe = (qc * e_b).astype(BF16)
                    qt = (qc * e_q).astype(BF16)
                    kt = (kc * e_k).astype(BF16)
                    kd = (kc * e_d).astype(BF16)
                    stb = st_ref[ci, hd]
                    dst = ds_scr[hd]
                    dstb = dst.astype(BF16)
                    a_t = jnp.where(causal_t, _dot(kt, qt, NT), 0.0).astype(BF16)
                    da = jnp.where(causal, _dot(do, vc, NT), 0.0).astype(BF16)
                    da_t = jnp.where(causal_t, _dot(vc, do, NT), 0.0).astype(BF16)
                    dv_scr[rs, cs] = _dot(a_t, do) + _dot(kd, dstb, NT)
                    dqe, dqt = _dot(do, stb), _dot(da, kt)
                    dkt, dkd = _dot(da_t, qt), _dot(vc, dstb)
                    dq_scr[rs, cs] = dqe * e_b + dqt * e_q
                    dk_scr[rs, cs] = dkt * e_k + dkd * e_d
                    e_l = jnp.exp(bl)
                    s_end = stb.astype(F32) * e_l + _dot(vc, kd, TN)
                    dbc = (qe.astype(F32) * dqe + qt.astype(F32) * dqt
                           - kt.astype(F32) * dkt - kd.astype(F32) * dkd)
                    db_scr[rs, cs] = dbc + jnp.where(last_row, _colsum(dstb.astype(F32) * s_end), 0.0)
                    ds_scr[hd] = dst * e_l + _dot(do, qe, TN)
            dq = dq_scr[...]
            dk = dk_scr[...]
            cols = slice(512 * jh, 512 * (jh + 1))
            dlogf = _dot(u_ref[...], db_scr[...], precision=lax.Precision.HIGHEST)
            one_m_sig = 1.0 - sig
            dsig = (1.0 - lbv) * sig * one_m_sig
            dproj_ref[jh] = (dq * (sq * (1.0 + qp * (1.0 - sq)))).astype(BF16)
            dproj_ref[2 + jh] = (dlogf * dsig / f - dk * dsig).astype(BF16)
            dproj_ref[4 + jh] = dv_scr[...].astype(BF16)
            dproj_ref[6 + jh] = dg_scr[...].astype(BF16)
            dlb_ref[:, cols] += _colsum(dlogf * one_m_sig / f - dk * one_m_sig)
        don_ref[...] += don_acc

    rev = lambda i: nb - 1 - i
    return pl.pallas_call(
        body, name=name, grid=(nb,),
        out_shape=(jax.ShapeDtypeStruct((T, 4 * D_MODEL), BF16), jax.ShapeDtypeStruct((1, D_MODEL), F32),
                   jax.ShapeDtypeStruct((1, HG_HEAD), F32)),
        in_specs=[pl.BlockSpec((tb, 4 * D_MODEL), lambda i: (rev(i), 0)),
                  pl.BlockSpec((tb, D_MODEL), lambda i: (rev(i), 0)),
                  pl.BlockSpec((tb, D_MODEL), lambda i: (rev(i), 0)),
                  pl.BlockSpec((nc, HG_HEADS, HG_HEAD, HG_HEAD), lambda i: (rev(i), 0, 0, 0)),
                  pl.BlockSpec((1, D_MODEL), lambda i: (0, 0)),
                  pl.BlockSpec((1, HG_HEAD), lambda i: (0, 0)),
                  pl.BlockSpec((tb, tb), lambda i: (0, 0)),
                  pl.BlockSpec((tb, tb), lambda i: (0, 0))],
        out_specs=(pl.BlockSpec((tb, 4 * D_MODEL), lambda i: (rev(i), 0)),
                   pl.BlockSpec((1, D_MODEL), lambda i: (0, 0)),
                   pl.BlockSpec((1, HG_HEAD), lambda i: (0, 0))),
        scratch_shapes=[pltpu.VMEM((HG_HEADS, HG_HEAD, HG_HEAD), F32)] + [pltpu.VMEM((tb, 512), F32)] * 6,
        compiler_params=_params("arbitrary"),
    )(proj, o, dog, states, lb, out_norm, lmat, umat)


def _gm_norm(pre_ref, lg_ref, lbias_ref):
    pre_ref = _ColBlocks(pre_ref, 768)
    vs = [pre_ref[4 + j].astype(F32) for j in range(4)]
    width = 4 * vs[0].shape[1]
    mu = sum(jnp.sum(v, axis=1, keepdims=True) for v in vs) / width
    ds = [v - mu for v in vs]
    var = sum(jnp.sum(d * d, axis=1, keepdims=True) for d in ds) / width
    rstd = lax.rsqrt(var + EPS)
    vhat = [d * rstd for d in ds]
    vn = [vhat[j] * lg_ref[j:j + 1, :] + lbias_ref[j:j + 1, :] for j in range(4)]
    return vhat, vn, rstd


def _gm_spatial_fwd(pre, ln_g, ln_b, ws, bsb, name, *, tb=256):
    T = pre.shape[0]
    tb = min(tb, T)
    nc = tb // GM_CHUNK

    def body(pre_ref, lg_ref, lbias_ref, ws_ref, bs_ref, o_ref):
        _, vn, _ = _gm_norm(pre_ref, lg_ref, lbias_ref)
        pre_ref, o_ref = _ColBlocks(pre_ref, 768), _ColBlocks(o_ref, 768)
        for j in range(4):
            u = pre_ref[j].astype(F32)
            for e in range(2):
                g = 2 * j + e
                cs = slice(GM_GDIM * e, GM_GDIM * (e + 1))
                wg = ws_ref[g].astype(BF16)
                for ci in range(nc):
                    rs = slice(GM_CHUNK * ci, GM_CHUNK * (ci + 1))
                    vm = _dot(wg, vn[j][rs, cs].astype(BF16)) + bs_ref[g]
                    o_ref[j, rs, cs] = (u[rs, cs] * vm).astype(BF16)

    return pl.pallas_call(
        body, name=name, grid=(T // tb,),
        out_shape=jax.ShapeDtypeStruct((T, 4 * 768), BF16),
        in_specs=[pl.BlockSpec((tb, 8 * 768), lambda i: (i, 0)),
                  pl.BlockSpec((4, 768), lambda i: (0, 0)),
                  pl.BlockSpec((4, 768), lambda i: (0, 0)),
                  pl.BlockSpec((GM_GROUPS, GM_CHUNK, GM_CHUNK), lambda i: (0, 0, 0)),
                  pl.BlockSpec((GM_GROUPS, GM_CHUNK, GM_GDIM), lambda i: (0, 0, 0))],
        out_specs=pl.BlockSpec((tb, 4 * 768), lambda i: (i, 0)),
        compiler_params=_params("parallel"),
    )(pre, ln_g, ln_b, ws, bsb)


def _gm_spatial_bwd(pre, gp, dm, ln_g, ln_b, ws, ws_t, bsb, name, *, tb=256):
    T = pre.shape[0]
    tb = min(tb, T)
    nc = tb // GM_CHUNK
    nb = T // tb

    def body(pre_ref, gp_ref, dm_ref, lg_ref, lbias_ref, ws_ref, wst_ref, bs_ref,
             dpre_ref, dws_ref, dbs_ref, dlg_ref, dlb_ref, dbin_ref, dbs_scr, dvn_scr, du_scr):
        i = pl.program_id(0)

        @pl.when(i == 0)
        def _():
            dws_ref[...] = jnp.zeros_like(dws_ref)
            dbs_scr[...] = jnp.zeros_like(dbs_scr)
            dlg_ref[...] = jnp.zeros_like(dlg_ref)
            dlb_ref[...] = jnp.zeros_like(dlb_ref)
            dbin_ref[...] = jnp.zeros_like(dbin_ref)

        vhat, vn, rstd = _gm_norm(pre_ref, lg_ref, lbias_ref)
        pre_ref, gp_ref, dm_ref = _ColBlocks(pre_ref, 768), _ColBlocks(gp_ref, 768), _ColBlocks(dm_ref, 768)
        dpre_ref = _ColBlocks(dpre_ref, 768)
        for j in range(4):
            u = pre_ref[j].astype(F32)
            for e in range(2):
                g = 2 * j + e
                cs = slice(GM_GDIM * e, GM_GDIM * (e + 1))
                wg = ws_ref[g].astype(BF16)
                wgt = wst_ref[g].astype(BF16)
                for ci in range(nc):
                    rs = slice(GM_CHUNK * ci, GM_CHUNK * (ci + 1))
                    vnb = vn[j][rs, cs].astype(BF16)
                    vm = _dot(wg, vnb) + bs_ref[g]
                    dmg = dm_ref[j, rs, cs].astype(F32)
                    du_scr[j, rs, cs] = dmg * vm
                    dvm = dmg * u[rs, cs]
                    dvmb = dvm.astype(BF16)
                    dws_ref[g] += _dot(dvmb, vnb, NT)
                    dbs_scr[g] += dvm
                    dvn_scr[j, rs, cs] = _dot(wgt, dvmb)
        width = 4 * 768
        dvh = []
        for j in range(4):
            dvn = dvn_scr[j]
            dlg_ref[j:j + 1, :] += _colsum(dvn * vhat[j])
            dlb_ref[j:j + 1, :] += _colsum(dvn)
            dvh.append(dvn * lg_ref[j:j + 1, :])
        m1 = sum(jnp.sum(d, axis=1, keepdims=True) for d in dvh) / width
        m2 = sum(jnp.sum(dvh[j] * vhat[j], axis=1, keepdims=True) for j in range(4)) / width
        for j in range(4):
            dv = rstd * (dvh[j] - m1 - vhat[j] * m2)
            dpv = dv * gp_ref[4 + j].astype(F32)
            dpu = du_scr[j] * gp_ref[j].astype(F32)
            dpre_ref[4 + j] = dpv.astype(BF16)
            dpre_ref[j] = dpu.astype(BF16)
            dbin_ref[4 + j:5 + j, :] += _colsum(dpv)
            dbin_ref[j:j + 1, :] += _colsum(dpu)

        @pl.when(i == nb - 1)
        def _():
            r_i = lax.broadcasted_iota(jnp.int32, (GM_CHUNK, GM_CHUNK), 0)
            c_i = lax.broadcasted_iota(jnp.int32, (GM_CHUNK, GM_CHUNK), 1)
            for g in range(GM_GROUPS):
                dws_ref[g] = jnp.where(c_i <= r_i, dws_ref[g], 0.0)
                dbs_ref[g] = jnp.broadcast_to(jnp.sum(dbs_scr[g], axis=1, keepdims=True), (GM_CHUNK, GM_CHUNK))

    sq = pl.BlockSpec((GM_GROUPS, GM_CHUNK, GM_CHUNK), lambda i: (0, 0, 0))
    v4 = pl.BlockSpec((4, 768), lambda i: (0, 0))
    return pl.pallas_call(
        body, name=name, grid=(nb,),
        out_shape=(jax.ShapeDtypeStruct((T, 8 * 768), BF16),
                   jax.ShapeDtypeStruct((GM_GROUPS, GM_CHUNK, GM_CHUNK), F32),
                   jax.ShapeDtypeStruct((GM_GROUPS, GM_CHUNK, GM_CHUNK), F32),
                   jax.ShapeDtypeStruct((4, 768), F32), jax.ShapeDtypeStruct((4, 768), F32),
                   jax.ShapeDtypeStruct((8, 768), F32)),
        in_specs=[pl.BlockSpec((tb, 8 * 768), lambda i: (i, 0)),
                  pl.BlockSpec((tb, 8 * 768), lambda i: (i, 0)),
                  pl.BlockSpec((tb, 4 * 768), lambda i: (i, 0)),
                  v4, v4, sq, sq,
                  pl.BlockSpec((GM_GROUPS, GM_CHUNK, GM_GDIM), lambda i: (0, 0, 0))],
        out_specs=(pl.BlockSpec((tb, 8 * 768), lambda i: (i, 0)), sq, sq, v4, v4,
                   pl.BlockSpec((8, 768), lambda i: (0, 0))),
        scratch_shapes=[pltpu.VMEM((GM_GROUPS, GM_CHUNK, GM_GDIM), F32),
                        pltpu.VMEM((4, tb, 768), F32), pltpu.VMEM((4, tb, 768), F32)],
        compiler_params=_params("arbitrary"),
    )(pre, gp, dm, ln_g, ln_b, ws, ws_t, bsb)


def _adamw(slots, w, m, v, name, *, tr=256):
    S, R, C = slots.shape
    tr = next((t for t in (tr, tr // 2, tr // 4, tr // 8, tr // 16) if R % t == 0), R) if R > tr else R
    bc1 = 1.0 - ADAM_B1 ** ADAM_STEP
    bc2 = 1.0 - ADAM_B2 ** ADAM_STEP

    def body(s_ref, w_ref, m_ref, v_ref, g_ref, d_ref, nm_ref, nv_ref):
        g = s_ref[0].astype(F32)
        for s in range(1, S):
            g = g + s_ref[s].astype(F32)
        mn = ADAM_B1 * m_ref[...] + (1.0 - ADAM_B1) * g
        vn = ADAM_B2 * v_ref[...] + (1.0 - ADAM_B2) * (g * g)
        g_ref[...] = g
        nm_ref[...] = mn
        nv_ref[...] = vn
        d_ref[...] = -ADAM_LR * ((mn / bc1) / (jnp.sqrt(vn / bc2) + ADAM_EPS) + ADAM_WD * w_ref[...])

    spec = pl.BlockSpec((tr, C), lambda i: (i, 0))
    return pl.pallas_call(
        body, name=name, grid=(R // tr,),
        out_shape=(jax.ShapeDtypeStruct((R, C), F32),) * 4,
        in_specs=[pl.BlockSpec((S, tr, C), lambda i: (0, i, 0)), spec, spec, spec],
        out_specs=(spec,) * 4,
        compiler_params=_params("parallel"),
    )(slots, w, m, v)


def _update(slots, w, m, v, name):
    shp = w.shape
    C = shp[-1]
    R = math.prod(shp[:-1])
    outs = _adamw(slots.reshape(slots.shape[0], R, C), w.reshape(R, C), m.reshape(R, C), v.reshape(R, C), name)
    return tuple(o.reshape(shp) for o in outs)


def kernel(x, c, ada_w, ada_b, norm_pre, norm_post, ffn_w_in, ffn_w_out, hg_w_in, hg_w_out, hg_out_norm, hg_lb, gm_w_in, gm_b_in, gm_ln_g, gm_ln_b, gm_w_s, gm_b_s, gm_w_out, loss_target, m_ada_w, m_ada_b, m_norm_pre, m_norm_post, m_ffn_w_in, m_ffn_w_out, m_hg_w_in, m_hg_w_out, m_hg_out_norm, m_hg_lb, m_gm_w_in, m_gm_b_in, m_gm_ln_g, m_gm_ln_b, m_gm_w_s, m_gm_b_s, m_gm_w_out, v_ada_w, v_ada_b, v_norm_pre, v_norm_post, v_ffn_w_in, v_ffn_w_out, v_hg_w_in, v_hg_w_out, v_hg_out_norm, v_hg_lb, v_gm_w_in, v_gm_b_in, v_gm_ln_g, v_gm_ln_b, v_gm_w_s, v_gm_b_s, v_gm_w_out):
    me = 4 * lax.axis_index("x") + 2 * lax.axis_index("y") + lax.axis_index("c")
    T = x.shape[1]
    x0 = x.reshape(T, D_MODEL)
    target = loss_target.reshape(T, D_MODEL)
    n_ada = ada_w.shape[-1]

    pack = jnp.concatenate([
        c.reshape(8, 128), norm_pre.reshape(6, 128), norm_post.reshape(6, 128),
        gm_b_in.reshape(6, 128), gm_ln_g.reshape(3, 128), gm_ln_b.reshape(3, 128)], axis=0)
    packs = _all_gather(pack, "gather_small")
    c_all = packs[:, 0:8].reshape(NDEV, D_MODEL)
    npre = packs[:, 8:14].reshape(NDEV, 2, 3, 128).transpose(1, 2, 0, 3).reshape(2, 3, D_MODEL)
    npost = packs[:, 14:20].reshape(NDEV, 2, 3, 128).transpose(1, 2, 0, 3).reshape(2, 3, D_MODEL)
    b_in = packs[:, 20:26].reshape(1, NDEV * 768)
    ln_g = packs[:, 26:29].reshape(4, 768)
    ln_b = packs[:, 29:32].reshape(4, 768)

    ada_b_mine = lax.dynamic_slice_in_dim(ada_b, me * n_ada, n_ada, axis=1).reshape(2, 1, n_ada)
    mod_cols = _ada_fwd(c_all, ada_w, ada_b_mine, "ada_fwd")
    mod_all = _all_gather(mod_cols, "gather_mod")
    mod = lax.dynamic_index_in_dim(mod_all, me, axis=2, keepdims=False)
    mod = mod.transpose(1, 0, 2).reshape(2, 9, 1, D_MODEL)

    sh_fi, sh_fo = ffn_w_in.astype(BF16).swapaxes(-1, -2), ffn_w_out.astype(BF16)
    sh_hi, sh_ho = hg_w_in[0].astype(BF16).T, hg_w_out[0].astype(BF16)
    sh_mi, sh_mo = gm_w_in[0].astype(BF16).T, gm_w_out[0].astype(BF16)

    def whole(gathered):
        return gathered.reshape(-1, D_MODEL)

    w_fi = {(0, 0): whole(_all_gather(sh_fi[0, 0], "gather_ffn_in_first"))}
    w_fo = {}
    riders = {"l0s0": [sh_fo[0, 0], sh_hi, sh_ho], "l0s1": [sh_fi[0, 1], sh_fo[0, 1]], "hg_mix": [sh_mi, sh_mo],
              "l0s2": [sh_fi[1, 0], sh_fo[1, 0]], "l1s0": [sh_fi[1, 1], sh_fo[1, 1]]}

    sm = jax.nn.softmax(hg_lb, axis=0)
    lb0 = sm[0:1]
    on = hg_out_norm.reshape(1, HG_HEAD)
    tril = jnp.tril(jnp.ones((GM_CHUNK, GM_CHUNK), F32))
    ws = gm_w_s[0] * tril[None]
    ws_t = ws.transpose(0, 2, 1)
    bsb = jnp.broadcast_to(gm_b_s[0][:, :, None], (GM_GROUPS, GM_CHUNK, GM_GDIM))

    res_ws = (0.5, 1.0, 0.5)

    def vecs(i, s):
        return (npre[i, s].reshape(1, D_MODEL), npost[i, s].reshape(1, D_MODEL),
                mod[i, 3 * s], mod[i, 3 * s + 1], mod[i, 3 * s + 2])

    order = [(i, s) for i in range(2) for s in range(3)]
    saved = {}
    xs = x0
    pre_g, _, shift, scale, _ = vecs(0, 0)
    h = _prenorm_fwd(xs, pre_g, scale, shift, "prenorm_l0s0")
    for pos, (i, s) in enumerate(order):
        tag = f"l{i}s{s}"
        _, post_g, _, _, gate = vecs(i, s)
        rider = _Exchange("gather", riders[tag]) if tag in riders else None
        if s != 1:
            pq, a, got = _ffn_in(h, w_fi[i, s // 2], "ffn_in_" + tag, exchange=rider)
            extra = (pq, a)
            if tag == "l0s0":
                w_fo[0, 0], w_hi, w_ho = map(whole, got)
            elif tag == "l0s2":
                w_fi[1, 0], w_fo[1, 0] = map(whole, got)
            elif tag == "l1s0":
                w_fi[1, 1], w_fo[1, 1] = map(whole, got)
            wo = w_fo[i, s // 2]
        elif i == 0:
            proj, *got = _mm_blocks(h, w_hi, "hg_in", out_dtype=F32, exchange=rider)
            w_fi[0, 1], w_fo[0, 1] = map(whole, got)
            o, og, states, got = _hgrn_fwd(proj, lb0, on, "hg_mix", exchange=_Exchange("gather", riders["hg_mix"]))
            w_mi, w_mo = map(whole, got)
            a, wo = og, w_ho
            extra = (proj, o, og, states)
        else:
            pre, gp = _mm_blocks(h, w_mi, "gm_in", bias=b_in, gelu=True)
            a = _gm_spatial_fwd(pre, ln_g, ln_b, ws, bsb, "gm_mix")
            wo = w_mo
            extra = (pre, gp, a)
        if pos + 1 < len(order):
            npre_g, _, nshift, nscale, _ = vecs(*order[pos + 1])
            y, x_next, h_next = _out_proj(a, wo, xs, post_g, gate, res_ws[s], (npre_g, nscale, nshift), "out_" + tag)
            saved[tag] = (xs, h, y) + extra
            xs, h = x_next, h_next
        else:
            dx, dy, dgate, dpost, loss_part = _out_proj_last(a, wo, xs, post_g, gate, res_ws[s], target, "out_" + tag)
            saved[tag] = (xs, h, None) + extra
    loss = lax.psum(loss_part[0, 0], ("x", "y", "c"))

    slots = {}
    d_npre = [[None] * 3, [None] * 3]
    d_npost = [[None] * 3, [None] * 3]
    d_mod = [[None] * 9, [None] * 9]
    for pos in reversed(range(len(order))):
        i, s = order[pos]
        tag = f"l{i}s{s}"
        pre_g, _, _, scale, _ = vecs(i, s)
        xin, h = saved[tag][:2]
        if s != 1:
            w_in, wo = w_fi[i, s // 2], w_fo[i, s // 2]
            pq, g = saved[tag][3:]
            dz = _ffn_dgate(dy, wo, pq, "ffn_dgate_" + tag)
            g_out = _mm_wgrad(g, dy, "ffn_out_wgrad_" + tag, xw=1408)
            g_in = _mm_wgrad(dz, h, "ffn_in_wgrad_" + tag, xw=1408)
        elif i == 0:
            proj, o, og, states = saved[tag][3:]
            dog = _mm_blocks(dy, w_ho, "hg_out_dgrad")
            g_out = _mm_wgrad(og, dy, "hg_out_wgrad")
            dz, d_lb0, d_on = _hgrn_bwd(proj, o, dog, states, lb0, on, "hg_mix_bwd")
            w_in = w_hi
            g_in = _mm_wgrad(h, dz, "hg_in_wgrad", yw=512)
        else:
            pre, gp, sp = saved[tag][3:]
            dm = _mm_blocks(dy, w_mo, "gm_out_dgrad")
            g_out = _mm_wgrad(sp, dy, "gm_out_wgrad", xw=768)
            dz, d_ws, d_bs, d_lg, d_lbias, d_bin = _gm_spatial_bwd(pre, gp, dm, ln_g, ln_b, ws, ws_t, bsb, "gm_mix_bwd")
            w_in = w_mi
            g_in = _mm_wgrad(h, dz, "gm_in_wgrad", yw=768)
        g_out = g_out.reshape(NDEV, -1, D_MODEL)
        g_in = g_in.reshape(NDEV, -1, D_MODEL) if s != 1 else g_in
        d_npost[i][s] = dpost
        d_mod[i][3 * s + 2] = dgate
        rider = _Exchange("scatter", [g_in, g_out])
        if pos > 0:
            pi, ps = order[pos - 1]
            _, ppost_g, _, _, pgate = vecs(pi, ps)
            prev = (saved[f"l{pi}s{ps}"][2], ppost_g, pgate, res_ws[ps])
            dx, dshift, dscale, dpre_g, dy, dgate, dpost, r_in, r_out = _in_grad(
                dz, w_in, dx, xin, pre_g, scale, prev, "in_grad_" + tag, exchange=rider)
        else:
            dx, dshift, dscale, dpre_g, r_in, r_out = _in_grad(
                dz, w_in, dx, xin, pre_g, scale, None, "in_grad_" + tag, exchange=rider)
        slots[tag] = (r_in, r_out)
        d_npre[i][s] = dpre_g
        d_mod[i][3 * s], d_mod[i][3 * s + 1] = dshift, dscale
    grad_x = dx.reshape(x.shape)

    ffn_tags = ["l0s0", "l0s2", "l1s0", "l1s2"]
    s_fi = jnp.stack([slots[t][0] for t in ffn_tags], axis=1).swapaxes(-1, -2)
    s_fo = jnp.stack([slots[t][1] for t in ffn_tags], axis=1)
    (s_hi, s_ho), (s_mi, s_mo) = slots["l0s1"], slots["l1s1"]
    s_hi, s_ho, s_mi, s_mo = s_hi[:, None], s_ho[:, None], s_mi[:, None], s_mo[:, None]

    gmod = jnp.stack([jnp.concatenate(d_mod[i], axis=0) for i in range(2)])
    d_sm = lb0 * d_lb0
    d_hg_lb = jnp.concatenate([d_sm, jnp.zeros((2, D_MODEL), F32)], axis=0) - sm * d_sm
    small = [gmod, jnp.stack([jnp.concatenate(r, axis=0) for r in d_npre]),
             jnp.stack([jnp.concatenate(r, axis=0) for r in d_npost]),
             d_on, d_hg_lb, d_bin, d_lg, d_lbias, d_ws, d_bs[:, :, 0]]
    sizes = [a.size for a in small]
    flat = jnp.concatenate([a.reshape(-1) for a in small])
    rows = -(-flat.size // (8 * 128)) * 8
    flat = jnp.pad(flat, (0, rows * 128 - flat.size)).reshape(rows, 128)
    flats = _all_gather(flat, "gather_small_grads").reshape(NDEV, rows * 128)
    parts, off = [], 0
    for a, n in zip(small, sizes):
        parts.append(flats[:, off:off + n].reshape((NDEV,) + a.shape))
        off += n
    p_mod, p_npre, p_npost, p_on, p_lb, p_bin, p_lg, p_lbias, p_ws, p_bs = parts

    def mine(p, width):
        return lax.dynamic_slice_in_dim(p, me * width, width, axis=p.ndim - 1)

    gmod_cols = mine(p_mod.reshape(NDEV, 2, 9 * D_MODEL), n_ada).transpose(1, 0, 2)
    g_ada_w = _ada_bwd(jnp.pad(c_all.T, ((0, 0), (0, 120))), jnp.pad(gmod_cols, ((0, 0), (0, 120), (0, 0))), "ada_bwd")

    out = {}
    out["ada_w"] = _update(g_ada_w[None], ada_w, m_ada_w, v_ada_w, "adamw_ada_w")
    out["ada_b"] = _update(p_mod.reshape(NDEV, 2, 9 * D_MODEL), ada_b, m_ada_b, v_ada_b, "adamw_ada_b")
    out["norm_pre"] = _update(mine(p_npre, 128), norm_pre, m_norm_pre, v_norm_pre, "adamw_norm_pre")
    out["norm_post"] = _update(mine(p_npost, 128), norm_post, m_norm_post, v_norm_post, "adamw_norm_post")
    out["ffn_w_in"] = _update(s_fi.reshape((NDEV,) + ffn_w_in.shape), ffn_w_in, m_ffn_w_in, v_ffn_w_in, "adamw_ffn_in")
    out["ffn_w_out"] = _update(s_fo.reshape((NDEV,) + ffn_w_out.shape), ffn_w_out, m_ffn_w_out, v_ffn_w_out, "adamw_ffn_out")
    out["hg_w_in"] = _update(s_hi, hg_w_in, m_hg_w_in, v_hg_w_in, "adamw_hg_in")
    out["hg_w_out"] = _update(s_ho, hg_w_out, m_hg_w_out, v_hg_w_out, "adamw_hg_out")
    out["hg_out_norm"] = _update(p_on, hg_out_norm, m_hg_out_norm, v_hg_out_norm, "adamw_hg_norm")
    out["hg_lb"] = _update(p_lb, hg_lb, m_hg_lb, v_hg_lb, "adamw_hg_lb")
    out["gm_w_in"] = _update(s_mi, gm_w_in, m_gm_w_in, v_gm_w_in, "adamw_gm_in")
    out["gm_b_in"] = _update(mine(p_bin.reshape(NDEV, 1, 8 * 768), 768), gm_b_in, m_gm_b_in, v_gm_b_in, "adamw_gm_b_in")
    out["gm_ln_g"] = _update(mine(p_lg.reshape(NDEV, 1, 4 * 768), 384), gm_ln_g, m_gm_ln_g, v_gm_ln_g, "adamw_gm_ln_g")
    out["gm_ln_b"] = _update(mine(p_lbias.reshape(NDEV, 1, 4 * 768), 384), gm_ln_b, m_gm_ln_b, v_gm_ln_b, "adamw_gm_ln_b")
    out["gm_w_s"] = _update(p_ws[:, None], gm_w_s, m_gm_w_s, v_gm_w_s, "adamw_gm_w_s")
    out["gm_b_s"] = _update(p_bs[:, None], gm_b_s, m_gm_b_s, v_gm_b_s, "adamw_gm_b_s")
    out["gm_w_out"] = _update(s_mo, gm_w_out, m_gm_w_out, v_gm_w_out, "adamw_gm_out")

    names = ["ada_w", "ada_b", "norm_pre", "norm_post", "ffn_w_in", "ffn_w_out", "hg_w_in", "hg_w_out",
             "hg_out_norm", "hg_lb", "gm_w_in", "gm_b_in", "gm_ln_g", "gm_ln_b", "gm_w_s", "gm_b_s", "gm_w_out"]
    return (loss, grad_x, *[out[n][0] for n in names], *[out[n][1] for n in names],
            *[out[n][2] for n in names], *[out[n][3] for n in names])
```

```python
import math

import jax
import jax.numpy as jnp
from jax import lax
from jax.experimental import pallas as pl
from jax.experimental.pallas import tpu as pltpu

F32 = jnp.float32
BF16 = jnp.bfloat16
NDEV = 8
D_MODEL = 1024
EPS = 1e-6
HG_CHUNK = 64
HG_HEAD = 128
HG_HEADS = 8
GM_CHUNK = 128
GM_GDIM = 384
GM_GROUPS = 8
ADAM_LR = 0.001
ADAM_B1 = 0.9
ADAM_B2 = 0.999
ADAM_EPS = 1e-08
ADAM_WD = 0.01
ADAM_STEP = 10
VMEM_LIMIT = 56 * 2 ** 20

NN = (((1,), (0,)), ((), ()))
NT = (((1,), (1,)), ((), ()))
TN = (((0,), (0,)), ((), ()))
MESH = pl.DeviceIdType.MESH
ANY = pl.BlockSpec(memory_space=pl.ANY)


def _dot(a, b, dims=NN, precision=None):
    return lax.dot_general(a, b, dims, preferred_element_type=F32, precision=precision)


def _params(*sem):
    return pltpu.CompilerParams(dimension_semantics=sem, vmem_limit_bytes=VMEM_LIMIT)


def _sigmoid(x):
    return 1.0 / (1.0 + jnp.exp(-x))


def _sigmoid_t(x):
    return 0.5 * jnp.tanh(0.5 * x) + 0.5


def _gelu_and_grad(x):
    c = math.sqrt(2.0 / math.pi)
    x2 = x * x
    t = jnp.tanh(x * (c + (c * 0.044715) * x2))
    hx = 0.5 * x
    p = 1.0 + t
    return hx * p, 0.5 * p + hx * (1.0 - t * t) * (c + (3.0 * c * 0.044715) * x2)


def _colsum(x):
    return jnp.sum(x, axis=0, keepdims=True)


def _rowmean(x):
    return jnp.mean(x, axis=-1, keepdims=True)


def _all_gather(shard, name):
    def body(x_ref, out_ref, send_sems, recv_sems, local_sem):
        x, y, c = lax.axis_index("x"), lax.axis_index("y"), lax.axis_index("c")
        me, sibling = (x, y, c), (x, y, 1 - c)
        chips = [(1 - x, y), (x, 1 - y), (1 - x, 1 - y)]

        def slot(p):
            return out_ref.at[4 * p[0] + 2 * p[1] + p[2]]

        def copy(k, block, to, src=None):
            return pltpu.make_async_remote_copy(
                src_ref=slot(block) if src is None else src, dst_ref=slot(block),
                send_sem=send_sems.at[k], recv_sem=recv_sems.at[k],
                device_id=to, device_id_type=MESH)

        mine = pltpu.make_async_copy(x_ref, slot(me), local_sem)
        mine.start()
        first = [copy(0, me, sibling, src=x_ref)]
        first += [copy(1 + j, me, (*chip, c), src=x_ref) for j, chip in enumerate(chips)]
        for cp in first:
            cp.start()
        passed = [copy(4 + j, (*chip, c), sibling) for j, chip in enumerate(chips)]
        for j, chip in enumerate(chips):
            copy(1 + j, (*chip, c), me).wait_recv()
            passed[j].start()
        copy(0, sibling, me).wait_recv()
        for j, chip in enumerate(chips):
            copy(4 + j, (*chip, 1 - c), me).wait_recv()
        for cp in first + passed:
            cp.wait_send()
        mine.wait()

    return pl.pallas_call(
        body, name=name,
        out_shape=jax.ShapeDtypeStruct((NDEV,) + shard.shape, shard.dtype),
        in_specs=[ANY], out_specs=ANY,
        scratch_shapes=[pltpu.SemaphoreType.DMA((7,)), pltpu.SemaphoreType.DMA((7,)),
                        pltpu.SemaphoreType.DMA(())],
    )(shard)


class _Exchange:
    def __init__(self, kind, arrays):
        self.gather = kind == "gather"
        self.arrays = list(arrays)
        self.n = n = len(self.arrays)
        self.out_shape = [jax.ShapeDtypeStruct(((NDEV,) + a.shape) if self.gather else a.shape, a.dtype)
                          for a in self.arrays]
        self.scratch = [pltpu.SemaphoreType.DMA((n, NDEV - 1)), pltpu.SemaphoreType.DMA((n, NDEV - 1)),
                        pltpu.SemaphoreType.DMA((n,))]

    def _copies(self, in_refs, out_refs, sems):
        send_sems, recv_sems, local_sems = sems
        x, y, c = lax.axis_index("x"), lax.axis_index("y"), lax.axis_index("c")
        me = 4 * x + 2 * y + c
        peers = [(1 - x if k & 4 else x, 1 - y if k & 2 else y, 1 - c if k & 1 else c) for k in range(1, NDEV)]
        local, send, recv = [], [], []
        for a in range(self.n):
            src = (lambda pid, a=a: in_refs[a]) if self.gather else (lambda pid, a=a: in_refs[a].at[pid])
            local.append(pltpu.make_async_copy(src(me), out_refs[a].at[me], local_sems.at[a]))
            for k, p in enumerate(peers):
                pid = 4 * p[0] + 2 * p[1] + p[2]
                for lst, slot in ((send, me), (recv, pid)):
                    lst.append(pltpu.make_async_remote_copy(
                        src_ref=src(pid), dst_ref=out_refs[a].at[slot],
                        send_sem=send_sems.at[a, k], recv_sem=recv_sems.at[a, k],
                        device_id=p, device_id_type=MESH))
        return local, send, recv

    def start(self, first, in_refs, out_refs, sems):
        @pl.when(first)
        def _():
            local, send, _ = self._copies(in_refs, out_refs, sems)
            for cp in local + send:
                cp.start()

    def finish(self, last, in_refs, out_refs, sems):
        @pl.when(last)
        def _():
            local, send, recv = self._copies(in_refs, out_refs, sems)
            for cp in send:
                cp.wait_send()
            for cp in recv:
                cp.wait_recv()
            for cp in local:
                cp.wait()


def _host(exchange, n_in, n_out, body, first_last):
    if exchange is None:
        return body, [], [], [], []
    n = exchange.n

    def hosted(*refs):
        ins, refs = refs[:n_in], refs[n_in:]
        xin, refs = refs[:n], refs[n:]
        outs, refs = refs[:n_out], refs[n_out:]
        xout, refs = refs[:n], refs[n:]
        scratch, sems = refs[:len(refs) - 3], refs[len(refs) - 3:]
        first, last = first_last()
        exchange.start(first, xin, xout, sems)
        body(*ins, *outs, *scratch)
        exchange.finish(last, xin, xout, sems)

    return hosted, exchange.arrays, [ANY] * n, exchange.out_shape, exchange.scratch


def _first_last(steps):
    def at():
        i = pl.program_id(0)
        return i == 0, i == steps - 1
    return at


class _ColBlocks:
    def __init__(self, ref, width):
        self.ref, self.width = ref, width

    def _index(self, key):
        key = key if isinstance(key, tuple) else (key,)
        rows = key[1] if len(key) > 1 else slice(None)
        cols = key[2] if len(key) > 2 else slice(0, self.width)
        c0 = key[0] * self.width
        return rows, slice(c0 + cols.start, c0 + cols.stop)

    def __getitem__(self, key):
        return self.ref[self._index(key)]

    def __setitem__(self, key, value):
        self.ref[self._index(key)] = value


def _col_chunks(width, chunk=768):
    return [slice(c, min(c + chunk, width)) for c in range(0, width, chunk)]


def _row_spec(tm, d):
    return pl.BlockSpec((tm, d), lambda m: (m, 0))


def _vec_spec(d):
    return pl.BlockSpec((1, d), lambda m: (0, 0))


def _whole_spec(w):
    nd = w.ndim
    return pl.BlockSpec(w.shape, lambda m: (0,) * nd, pipeline_mode=pl.Buffered(1))


def _mm_blocks(a, w, name, *, bias=None, out_dtype=BF16, tm=512, chunk=768, gelu=False, exchange=None):
    T, K = a.shape
    C = w.shape[0]
    tm = min(tm, T)
    n_in = 2 + (bias is not None)

    def body(*refs):
        a_ref, w_ref = refs[:2]
        av = a_ref[...]
        for cols in _col_chunks(C, chunk):
            r = _dot(av, w_ref[cols], NT)
            if bias is not None:
                r = r + refs[2][:, cols]
            if gelu:
                z, dz = _gelu_and_grad(r)
                refs[n_in][:, cols] = z.astype(BF16)
                refs[n_in + 1][:, cols] = dz.astype(BF16)
            else:
                refs[n_in][:, cols] = r.astype(out_dtype)

    in_specs = [_row_spec(tm, K), _whole_spec(w)]
    args = [a, w]
    if bias is not None:
        in_specs.append(_whole_spec(bias))
        args.append(bias)
    outs = [jax.ShapeDtypeStruct((T, C), BF16)] * 2 if gelu else [jax.ShapeDtypeStruct((T, C), out_dtype)]
    body, x_args, x_in, x_out, x_scratch = _host(exchange, n_in, len(outs), body, _first_last(T // tm))
    res = pl.pallas_call(
        body, name=name, grid=(T // tm,),
        out_shape=outs + x_out,
        in_specs=in_specs + x_in,
        out_specs=[_row_spec(tm, C)] * len(outs) + x_in,
        scratch_shapes=x_scratch,
        compiler_params=_params("arbitrary" if exchange else "parallel"),
    )(*args, *x_args)
    return res if (exchange or gelu) else res[0]


def _rms(v):
    return lax.rsqrt(_rowmean(v * v) + EPS)


def _zero_at_start(*refs):
    @pl.when(pl.program_id(0) == 0)
    def _():
        for r in refs:
            r[...] = jnp.zeros_like(r)


def _postnorm_bwd_math(dxo, yv, g, gate, res_w, dgate_ref, dpost_ref):
    r = _rms(yv)
    yh = yv * r
    both = res_w * _colsum(dxo * yh)
    dgate_ref[...] += g * both
    dpost_ref[...] += gate * both
    dyh = dxo * (res_w * gate * g)
    return (r * (dyh - yh * _rowmean(dyh * yh))).astype(BF16)


def _out_proj(a, w, x, post_g, gate, res_w, nxt, name, *, tm=512):
    T, ka = a.shape
    d = w.shape[1]
    tm = min(tm, T)

    def body(a_ref, w_ref, x_ref, pg_ref, gate_ref, ng_ref, nsc_ref, nsh_ref, y_ref, xn_ref, h_ref):
        y = _dot(a_ref[...], w_ref[...])
        y_ref[...] = y
        xn = x_ref[...] + (y * _rms(y)) * (res_w * gate_ref[...] * pg_ref[...])
        xn_ref[...] = xn
        h_ref[...] = ((xn * _rms(xn)) * (ng_ref[...] * (1.0 + nsc_ref[...])) + nsh_ref[...]).astype(BF16)

    return pl.pallas_call(
        body, name=name, grid=(T // tm,),
        out_shape=(jax.ShapeDtypeStruct((T, d), F32), jax.ShapeDtypeStruct((T, d), F32),
                   jax.ShapeDtypeStruct((T, d), BF16)),
        in_specs=[_row_spec(tm, ka), _whole_spec(w), _row_spec(tm, d)] + [_vec_spec(d)] * 5,
        out_specs=(_row_spec(tm, d),) * 3,
        compiler_params=_params("parallel"),
    )(a, w, x, post_g, gate, *nxt)


def _out_proj_last(a, w, x, post_g, gate, res_w, target, name, *, tm=512):
    T, ka = a.shape
    d = w.shape[1]
    tm = min(tm, T)

    def body(a_ref, w_ref, x_ref, pg_ref, gate_ref, t_ref, dx_ref, dy_ref, dgate_ref, dpost_ref, l_ref):
        _zero_at_start(dgate_ref, dpost_ref, l_ref)
        y = _dot(a_ref[...], w_ref[...])
        e = x_ref[...] + res_w * gate_ref[...] * (y * _rms(y) * pg_ref[...]) - t_ref[...]
        l_ref[...] += 0.5 * jnp.sum(_rowmean(e * e), axis=0, keepdims=True)
        dx = e * (1.0 / d)
        dx_ref[...] = dx
        dy_ref[...] = _postnorm_bwd_math(dx, y, pg_ref[...], gate_ref[...], res_w, dgate_ref, dpost_ref)

    return pl.pallas_call(
        body, name=name, grid=(T // tm,),
        out_shape=(jax.ShapeDtypeStruct((T, d), F32), jax.ShapeDtypeStruct((T, d), BF16),
                   jax.ShapeDtypeStruct((1, d), F32), jax.ShapeDtypeStruct((1, d), F32),
                   jax.ShapeDtypeStruct((1, 128), F32)),
        in_specs=[_row_spec(tm, ka), _whole_spec(w), _row_spec(tm, d), _vec_spec(d), _vec_spec(d), _row_spec(tm, d)],
        out_specs=(_row_spec(tm, d), _row_spec(tm, d), _vec_spec(d), _vec_spec(d),
                   pl.BlockSpec((1, 128), lambda m: (0, 0))),
        compiler_params=_params("arbitrary"),
    )(a, w, x, post_g, gate, target)


def _in_grad(dz, w, dxo, x, pre_g, scale, prev, name, *, tm=512, exchange=None):
    T, C = dz.shape
    d = w.shape[1]
    tm = min(tm, T)
    has_prev = prev is not None
    res_w = prev[3] if has_prev else None

    def body(*refs):
        dz_ref, w_ref, dxo_ref, x_ref, g_ref, sc_ref = refs[:6]
        if has_prev:
            yp_ref, ppg_ref, pgate_ref, dx_ref, dsh_ref, dsc_ref, dg_ref, dyp_ref, dgate_ref, dpost_ref = refs[6:]
            _zero_at_start(dsh_ref, dsc_ref, dg_ref, dgate_ref, dpost_ref)
        else:
            dx_ref, dsh_ref, dsc_ref, dg_ref = refs[6:]
            _zero_at_start(dsh_ref, dsc_ref, dg_ref)
        dh = _dot(dz_ref[...], w_ref[...])
        xv = x_ref[...]
        r = _rms(xv)
        xh = xv * r
        gain = 1.0 + sc_ref[...]
        both = _colsum(dh * xh)
        dsh_ref[...] += _colsum(dh)
        dsc_ref[...] += g_ref[...] * both
        dg_ref[...] += gain * both
        dxh = dh * (gain * g_ref[...])
        dx = dxo_ref[...] + r * (dxh - xh * _rowmean(dxh * xh))
        dx_ref[...] = dx
        if has_prev:
            dyp_ref[...] = _postnorm_bwd_math(dx, yp_ref[...], ppg_ref[...], pgate_ref[...], res_w,
                                               dgate_ref, dpost_ref)

    vec = jax.ShapeDtypeStruct((1, d), F32)
    in_specs = [_row_spec(tm, C), _whole_spec(w), _row_spec(tm, d), _row_spec(tm, d), _vec_spec(d), _vec_spec(d)]
    out_shape = [jax.ShapeDtypeStruct((T, d), F32), vec, vec, vec]
    out_specs = [_row_spec(tm, d), _vec_spec(d), _vec_spec(d), _vec_spec(d)]
    args = [dz, w, dxo, x, pre_g, scale]
    if has_prev:
        in_specs += [_row_spec(tm, d), _vec_spec(d), _vec_spec(d)]
        out_shape += [jax.ShapeDtypeStruct((T, d), BF16), vec, vec]
        out_specs += [_row_spec(tm, d), _vec_spec(d), _vec_spec(d)]
        args += list(prev[:3])
    body, x_args, x_in, x_out, x_scratch = _host(exchange, len(args), len(out_shape), body, _first_last(T // tm))
    return pl.pallas_call(
        body, name=name, grid=(T // tm,),
        out_shape=out_shape + x_out, in_specs=in_specs + x_in, out_specs=out_specs + x_in,
        scratch_shapes=x_scratch,
        compiler_params=_params("arbitrary"),
    )(*args, *x_args)


def _mm_wgrad(x, y, name, *, xw=None, yw=None, tt=2048):
    T, P = x.shape
    Q = y.shape[1]
    xw, yw = xw or P, yw or Q
    jx, jy = P // xw, Q // yw
    assert jx == 1 or jy == 1
    tt = min(tt, T)
    nt = T // tt

    def body(x_ref, y_ref, o_ref, acc_ref):
        t = pl.program_id(1)

        @pl.when(t == 0)
        def _():
            acc_ref[...] = jnp.zeros_like(acc_ref)

        acc_ref[...] += _dot(x_ref[...], y_ref[...], TN)

        @pl.when(t == nt - 1)
        def _():
            o_ref[...] = acc_ref[...].astype(BF16)

    if jy > 1:
        out_shape = jax.ShapeDtypeStruct((jy, P, yw), BF16)
        out_spec = pl.BlockSpec((None, P, yw), lambda j, t: (j, 0, 0))
    else:
        out_shape = jax.ShapeDtypeStruct((P, Q), BF16)
        out_spec = pl.BlockSpec((xw, Q), lambda j, t: (j, 0))
    return pl.pallas_call(
        body, name=name, grid=(max(jx, jy), nt),
        out_shape=out_shape,
        in_specs=[pl.BlockSpec((tt, xw), (lambda j, t: (t, j)) if jx > 1 else (lambda j, t: (t, 0))),
                  pl.BlockSpec((tt, yw), (lambda j, t: (t, j)) if jy > 1 else (lambda j, t: (t, 0)))],
        out_specs=out_spec,
        scratch_shapes=[pltpu.VMEM((xw, yw), F32)],
        compiler_params=_params("parallel", "arbitrary"),
    )(x, y)


def _ffn_in(h, wt, name, *, tm=512, exchange=None):
    T, K = h.shape
    F = wt.shape[0] // 2
    tm = min(tm, T)

    def body(h_ref, w_ref, pq_ref, g_ref):
        hh = h_ref[...]
        for cols in _col_chunks(F):
            hi = slice(F + cols.start, F + cols.stop)
            a = _dot(hh, w_ref[cols], NT)
            b = _dot(hh, w_ref[hi], NT)
            s = _sigmoid_t(a)
            silu = a * s
            pq_ref[:, cols] = (b * (s * (1.0 + a * (1.0 - s)))).astype(BF16)
            pq_ref[:, hi] = silu.astype(BF16)
            g_ref[:, cols] = (silu * b).astype(BF16)

    body, x_args, x_in, x_out, x_scratch = _host(exchange, 2, 2, body, _first_last(T // tm))
    res = pl.pallas_call(
        body, name=name, grid=(T // tm,),
        out_shape=[jax.ShapeDtypeStruct((T, 2 * F), BF16), jax.ShapeDtypeStruct((T, F), BF16)] + x_out,
        in_specs=[_row_spec(tm, K), _whole_spec(wt)] + x_in,
        out_specs=[_row_spec(tm, 2 * F), _row_spec(tm, F)] + x_in,
        scratch_shapes=x_scratch,
        compiler_params=_params("arbitrary" if exchange else "parallel"),
    )(h, wt, *x_args)
    return res[0], res[1], res[2:]


def _ffn_dgate(dy, w_out, pq, name, *, tm=512):
    T, N = dy.shape
    F = w_out.shape[0]
    tm = min(tm, T)

    def body(dy_ref, w_ref, pq_ref, dz_ref):
        dyv = dy_ref[...]
        for cols in _col_chunks(F):
            hi = slice(F + cols.start, F + cols.stop)
            dg = _dot(dyv, w_ref[cols], NT)
            dz_ref[:, cols] = (dg * pq_ref[:, cols].astype(F32)).astype(BF16)
            dz_ref[:, hi] = (dg * pq_ref[:, hi].astype(F32)).astype(BF16)

    return pl.pallas_call(
        body, name=name, grid=(T // tm,),
        out_shape=jax.ShapeDtypeStruct((T, 2 * F), BF16),
        in_specs=[_row_spec(tm, N), _whole_spec(w_out), _row_spec(tm, 2 * F)],
        out_specs=_row_spec(tm, 2 * F),
        compiler_params=_params("parallel"),
    )(dy, w_out, pq)


def _prenorm_fwd(x, pre_g, scale, shift, name, *, tm=512):
    T, d = x.shape
    tm = min(tm, T)

    def body(x_ref, g_ref, sc_ref, sh_ref, h_ref):
        xv = x_ref[...]
        h_ref[...] = (xv * _rms(xv) * g_ref[...] * (1.0 + sc_ref[...]) + sh_ref[...]).astype(BF16)

    return pl.pallas_call(
        body, name=name, grid=(T // tm,),
        out_shape=jax.ShapeDtypeStruct((T, d), BF16),
        in_specs=[_row_spec(tm, d), _vec_spec(d), _vec_spec(d), _vec_spec(d)],
        out_specs=_row_spec(tm, d),
        compiler_params=_params("parallel"),
    )(x, pre_g, scale, shift)


def _ada_fwd(c_all, w, b, name):
    L, K, n = w.shape

    def body(c_ref, w_ref, b_ref, o_ref):
        cv = c_ref[...]
        cond = cv * _sigmoid(cv)
        for l in range(L):
            o_ref[l] = _dot(cond, w_ref[l], precision=lax.Precision.HIGHEST) + b_ref[l]

    return pl.pallas_call(
        body, name=name,
        out_shape=jax.ShapeDtypeStruct((L, NDEV, n), F32),
        compiler_params=pltpu.CompilerParams(vmem_limit_bytes=VMEM_LIMIT),
    )(c_all, w, b)


def _ada_bwd(c_all_t, gmod, name):
    L, _, n = gmod.shape
    K = c_all_t.shape[0]

    def body(c_ref, g_ref, o_ref):
        cv = c_ref[...]
        cond = cv * _sigmoid(cv)
        for l in range(L):
            o_ref[l] = _dot(cond, g_ref[l], precision=lax.Precision.HIGHEST)

    return pl.pallas_call(
        body, name=name,
        out_shape=jax.ShapeDtypeStruct((L, K, n), F32),
        compiler_params=pltpu.CompilerParams(vmem_limit_bytes=VMEM_LIMIT),
    )(c_all_t, gmod)


def _tri(n, upper=False, block=None):
    r = lax.broadcasted_iota(jnp.int32, (n, n), 0)
    c = lax.broadcasted_iota(jnp.int32, (n, n), 1)
    m = (c >= r) if upper else (c <= r)
    if block is not None:
        m = m & ((r // block) == (c // block))
    return m.astype(BF16)


def _tri_dot(tri, x):
    hi = x.astype(BF16)
    lo = (x - hi.astype(F32)).astype(BF16)
    return _dot(tri, hi) + _dot(tri, lo)


def _hgrn_gates(proj_ref, lb_ref, jh):
    proj_ref = _ColBlocks(proj_ref, 512)
    lb = lb_ref[:, 512 * jh:512 * (jh + 1)]
    qp = proj_ref[jh]
    fx = proj_ref[2 + jh]
    sq = _sigmoid_t(qp)
    sig = _sigmoid_t(fx)
    f = lb + (1.0 - lb) * sig
    k = (1.0 - lb) * (1.0 - sig)
    return lb, qp, sq, sig, f, k


def _hgrn_fwd(proj, lb, out_norm, name, *, tb=128, exchange=None):
    T = proj.shape[0]
    tb = min(tb, T)
    nc = tb // HG_CHUNK
    lmat = _tri(tb, block=HG_CHUNK)

    def body(proj_ref, lb_ref, on_ref, l_ref, o_ref, og_ref, st_ref, s_scr, b_scr):
        @pl.when(pl.program_id(0) == 0)
        def _():
            s_scr[...] = jnp.zeros_like(s_scr)

        r_i = lax.broadcasted_iota(jnp.int32, (HG_CHUNK, HG_CHUNK), 0)
        c_i = lax.broadcasted_iota(jnp.int32, (HG_CHUNK, HG_CHUNK), 1)
        causal = c_i <= r_i
        onv = on_ref[...]
        blocks = _ColBlocks(proj_ref, 512)
        for jh in range(2):
            lbv, qp, sq, sig, f, k = _hgrn_gates(proj_ref, lb_ref, jh)
            q = qp * sq
            b_half = b_scr.at[jh]
            b_half[...] = _tri_dot(l_ref[...], jnp.log(f))
            v = blocks[4 + jh]
            gp = blocks[6 + jh]
            gs = gp * _sigmoid_t(gp)
            for hh in range(4):
                hd = 4 * jh + hh
                cs = slice(HG_HEAD * hh, HG_HEAD * (hh + 1))
                for ci in range(nc):
                    r0 = HG_CHUNK * ci
                    rs = slice(r0, r0 + HG_CHUNK)
                    bc = b_half[rs, cs]
                    bm = b_half[r0 + HG_CHUNK // 2 - 1:r0 + HG_CHUNK // 2, cs]
                    bl = b_half[r0 + HG_CHUNK - 1:r0 + HG_CHUNK, cs]
                    qc, kc, vc = q[rs, cs], k[rs, cs], v[rs, cs].astype(BF16)
                    qe = (qc * jnp.exp(bc)).astype(BF16)
                    qt = (qc * jnp.exp(bc - bm)).astype(BF16)
                    kt = (kc * jnp.exp(bm - bc)).astype(BF16)
                    kd = (kc * jnp.exp(bl - bc)).astype(BF16)
                    st = s_scr[hd]
                    stb = st.astype(BF16)
                    st_ref[ci, hd] = stb
                    a = jnp.where(causal, _dot(qt, kt, NT), 0.0).astype(BF16)
                    o = _dot(qe, stb, NT) + _dot(a, vc)
                    s_scr[hd] = st * jnp.exp(bl) + _dot(vc, kd, TN)
                    o_ref[rs, HG_HEAD * hd:HG_HEAD * (hd + 1)] = o
                    r = lax.rsqrt(_rowmean(o * o) + EPS)
                    og_ref[rs, HG_HEAD * hd:HG_HEAD * (hd + 1)] = (o * r * onv * gs[rs, cs]).astype(BF16)

    body, x_args, x_in, x_out, x_scratch = _host(exchange, 4, 3, body, _first_last(T // tb))
    res = pl.pallas_call(
        body, name=name, grid=(T // tb,),
        out_shape=[jax.ShapeDtypeStruct((T, D_MODEL), F32), jax.ShapeDtypeStruct((T, D_MODEL), BF16),
                   jax.ShapeDtypeStruct((T // HG_CHUNK, HG_HEADS, HG_HEAD, HG_HEAD), BF16)] + x_out,
        in_specs=[pl.BlockSpec((tb, 4 * D_MODEL), lambda i: (i, 0)),
                  pl.BlockSpec((1, D_MODEL), lambda i: (0, 0)),
                  pl.BlockSpec((1, HG_HEAD), lambda i: (0, 0)),
                  pl.BlockSpec((tb, tb), lambda i: (0, 0))] + x_in,
        out_specs=[pl.BlockSpec((tb, D_MODEL), lambda i: (i, 0)),
                   pl.BlockSpec((tb, D_MODEL), lambda i: (i, 0)),
                   pl.BlockSpec((nc, HG_HEADS, HG_HEAD, HG_HEAD), lambda i: (i, 0, 0, 0))] + x_in,
        scratch_shapes=[pltpu.VMEM((HG_HEADS, HG_HEAD, HG_HEAD), F32), pltpu.VMEM((2, tb, 512), F32)] + x_scratch,
        compiler_params=_params("arbitrary"),
    )(proj, lb, out_norm, lmat, *x_args)
    return res[0], res[1], res[2], res[3:]


def _hgrn_bwd(proj, o, dog, states, lb, out_norm, name, *, tb=128):
    T = proj.shape[0]
    tb = min(tb, T)
    nc = tb // HG_CHUNK
    nb = T // tb
    lmat = _tri(tb, block=HG_CHUNK)
    umat = _tri(tb, upper=True, block=HG_CHUNK)

    def body(proj_ref, o_ref, dog_ref, st_ref, lb_ref, on_ref, l_ref, u_ref,
             dproj_ref, dlb_ref, don_ref, ds_scr, *half_scr):
        @pl.when(pl.program_id(0) == 0)
        def _():
            ds_scr[...] = jnp.zeros_like(ds_scr)
            dlb_ref[...] = jnp.zeros_like(dlb_ref)
            don_ref[...] = jnp.zeros_like(don_ref)

        r_i = lax.broadcasted_iota(jnp.int32, (HG_CHUNK, HG_CHUNK), 0)
        c_i = lax.broadcasted_iota(jnp.int32, (HG_CHUNK, HG_CHUNK), 1)
        causal = c_i <= r_i
        causal_t = r_i <= c_i
        last_row = lax.broadcasted_iota(jnp.int32, (HG_CHUNK, HG_HEAD), 0) == HG_CHUNK - 1
        onv = on_ref[...]
        don_acc = jnp.zeros((1, HG_HEAD), F32)
        blocks = _ColBlocks(proj_ref, 512)
        dproj_ref = _ColBlocks(dproj_ref, 512)
        for jh in range(2):
            b_scr, dq_scr, dk_scr, dv_scr, dg_scr, db_scr = [s.at[jh] for s in half_scr]
            lbv, qp, sq, sig, f, k = _hgrn_gates(proj_ref, lb_ref, jh)
            q = qp * sq
            b_scr[...] = _tri_dot(l_ref[...], jnp.log(f))
            v = blocks[4 + jh]
            gp = blocks[6 + jh]
            sg = _sigmoid_t(gp)
            for ci in reversed(range(nc)):
                r0 = HG_CHUNK * ci
                rs = slice(r0, r0 + HG_CHUNK)
                for hh in range(4):
                    hd = 4 * jh + hh
                    cs = slice(HG_HEAD * hh, HG_HEAD * (hh + 1))
                    hs = slice(HG_HEAD * hd, HG_HEAD * (hd + 1))
                    oc = o_ref[rs, hs]
                    r = lax.rsqrt(_rowmean(oc * oc) + EPS)
                    oh = oc * r
                    gc, sgc = gp[rs, cs], sg[rs, cs]
                    dogc = dog_ref[rs, hs].astype(F32)
                    don = dogc * (gc * sgc)
                    dg_scr[rs, cs] = dogc * (oh * onv) * (sgc * (1.0 + gc * (1.0 - sgc)))
                    don_acc += _colsum(don * oh)
                    donh = don * onv
                    do = (r * (donh - oh * _rowmean(donh * oh))).astype(BF16)
                    bc = b_scr[rs, cs]
                    bm = b_scr[r0 + HG_CHUNK // 2 - 1:r0 + HG_CHUNK // 2, cs]
                    bl = b_scr[r0 + HG_CHUNK - 1:r0 + HG_CHUNK, cs]
                    qc, kc, vc = q[rs, cs], k[rs, cs], v[rs, cs].astype(BF16)
                    e_b, e_q, e_k, e_d = jnp.exp(bc), jnp.exp(bc - bm), jnp.exp(bm - bc), jnp.exp(bl - bc)
                    qe = (qc * e_b).astype(BF16)
                    qt = (qc * e_q).astype(BF16)
                    kt = (kc * e_k).astype(BF16)
                    kd = (kc * e_d).astype(BF16)
                    stb = st_ref[ci, hd]
                    dst = ds_scr[hd]
                    dstb = dst.astype(BF16)
                    a_t = jnp.where(causal_t, _dot(kt, qt, NT), 0.0).astype(BF16)
                    da = jnp.where(causal, _dot(do, vc, NT), 0.0).astype(BF16)
                    da_t = jnp.where(causal_t, _dot(vc, do, NT), 0.0).astype(BF16)
                    dv_scr[rs, cs] = _dot(a_t, do) + _dot(kd, dstb, NT)
                    dqe, dqt = _dot(do, stb), _dot(da, kt)
                    dkt, dkd = _dot(da_t, qt), _dot(vc, dstb)
                    dq_scr[rs, cs] = dqe * e_b + dqt * e_q
                    dk_scr[rs, cs] = dkt * e_k + dkd * e_d
                    e_l = jnp.exp(bl)
                    s_end = stb.astype(F32) * e_l + _dot(vc, kd, TN)
                    dbc = (qe.astype(F32) * dqe + qt.astype(F32) * dqt
                           - kt.astype(F32) * dkt - kd.astype(F32) * dkd)
                    db_scr[rs, cs] = dbc + jnp.where(last_row, _colsum(dstb.astype(F32) * s_end), 0.0)
                    ds_scr[hd] = dst * e_l + _dot(do, qe, TN)
            dq = dq_scr[...]
            dk = dk_scr[...]
            cols = slice(512 * jh, 512 * (jh + 1))
            dlogf = _tri_dot(u_ref[...], db_scr[...])
            one_m_sig = 1.0 - sig
            dsig = (1.0 - lbv) * sig * one_m_sig
            dboth = dlogf / f - dk
            dproj_ref[jh] = (dq * (sq * (1.0 + qp * (1.0 - sq)))).astype(BF16)
            dproj_ref[2 + jh] = (dboth * dsig).astype(BF16)
            dproj_ref[4 + jh] = dv_scr[...].astype(BF16)
            dproj_ref[6 + jh] = dg_scr[...].astype(BF16)
            dlb_ref[:, cols] += _colsum(dboth * one_m_sig)
        don_ref[...] += don_acc

    rev = lambda i: nb - 1 - i
    return pl.pallas_call(
        body, name=name, grid=(nb,),
        out_shape=(jax.ShapeDtypeStruct((T, 4 * D_MODEL), BF16), jax.ShapeDtypeStruct((1, D_MODEL), F32),
                   jax.ShapeDtypeStruct((1, HG_HEAD), F32)),
        in_specs=[pl.BlockSpec((tb, 4 * D_MODEL), lambda i: (rev(i), 0)),
                  pl.BlockSpec((tb, D_MODEL), lambda i: (rev(i), 0)),
                  pl.BlockSpec((tb, D_MODEL), lambda i: (rev(i), 0)),
                  pl.BlockSpec((nc, HG_HEADS, HG_HEAD, HG_HEAD), lambda i: (rev(i), 0, 0, 0)),
                  pl.BlockSpec((1, D_MODEL), lambda i: (0, 0)),
                  pl.BlockSpec((1, HG_HEAD), lambda i: (0, 0)),
                  pl.BlockSpec((tb, tb), lambda i: (0, 0)),
                  pl.BlockSpec((tb, tb), lambda i: (0, 0))],
        out_specs=(pl.BlockSpec((tb, 4 * D_MODEL), lambda i: (rev(i), 0)),
                   pl.BlockSpec((1, D_MODEL), lambda i: (0, 0)),
                   pl.BlockSpec((1, HG_HEAD), lambda i: (0, 0))),
        scratch_shapes=[pltpu.VMEM((HG_HEADS, HG_HEAD, HG_HEAD), F32)] + [pltpu.VMEM((2, tb, 512), F32)] * 6,
        compiler_params=_params("arbitrary"),
    )(proj, o, dog, states, lb, out_norm, lmat, umat)


def _gm_norm(pre_ref, lg_ref, lbias_ref):
    pre_ref = _ColBlocks(pre_ref, 768)
    vs = [pre_ref[4 + j].astype(F32) for j in range(4)]
    width = 4 * vs[0].shape[1]
    mu = sum(jnp.sum(v, axis=1, keepdims=True) for v in vs) / width
    ds = [v - mu for v in vs]
    var = sum(jnp.sum(d * d, axis=1, keepdims=True) for d in ds) / width
    rstd = lax.rsqrt(var + EPS)
    vhat = [d * rstd for d in ds]
    vn = [vhat[j] * lg_ref[j:j + 1, :] + lbias_ref[j:j + 1, :] for j in range(4)]
    return vhat, vn, rstd


def _gm_spatial_fwd(pre, ln_g, ln_b, ws, bsb, name, *, tb=256):
    T = pre.shape[0]
    tb = min(tb, T)
    nc = tb // GM_CHUNK

    def body(pre_ref, lg_ref, lbias_ref, ws_ref, bs_ref, o_ref):
        _, vn, _ = _gm_norm(pre_ref, lg_ref, lbias_ref)
        pre_ref, o_ref = _ColBlocks(pre_ref, 768), _ColBlocks(o_ref, 768)
        for j in range(4):
            u = pre_ref[j].astype(F32)
            for e in range(2):
                g = 2 * j + e
                cs = slice(GM_GDIM * e, GM_GDIM * (e + 1))
                wg = ws_ref[g].astype(BF16)
                for ci in range(nc):
                    rs = slice(GM_CHUNK * ci, GM_CHUNK * (ci + 1))
                    vm = _dot(wg, vn[j][rs, cs].astype(BF16)) + bs_ref[g]
                    o_ref[j, rs, cs] = (u[rs, cs] * vm).astype(BF16)

    return pl.pallas_call(
        body, name=name, grid=(T // tb,),
        out_shape=jax.ShapeDtypeStruct((T, 4 * 768), BF16),
        in_specs=[pl.BlockSpec((tb, 8 * 768), lambda i: (i, 0)),
                  pl.BlockSpec((4, 768), lambda i: (0, 0)),
                  pl.BlockSpec((4, 768), lambda i: (0, 0)),
                  pl.BlockSpec((GM_GROUPS, GM_CHUNK, GM_CHUNK), lambda i: (0, 0, 0)),
                  pl.BlockSpec((GM_GROUPS, GM_CHUNK, GM_GDIM), lambda i: (0, 0, 0))],
        out_specs=pl.BlockSpec((tb, 4 * 768), lambda i: (i, 0)),
        compiler_params=_params("parallel"),
    )(pre, ln_g, ln_b, ws, bsb)


def _gm_spatial_bwd(pre, gp, dm, ln_g, ln_b, ws, ws_t, bsb, name, *, tb=256):
    T = pre.shape[0]
    tb = min(tb, T)
    nc = tb // GM_CHUNK
    nb = T // tb

    def body(pre_ref, gp_ref, dm_ref, lg_ref, lbias_ref, ws_ref, wst_ref, bs_ref,
             dpre_ref, dws_ref, dbs_ref, dlg_ref, dlb_ref, dbin_ref, dbs_scr, dvn_scr, du_scr):
        i = pl.program_id(0)

        @pl.when(i == 0)
        def _():
            dws_ref[...] = jnp.zeros_like(dws_ref)
            dbs_scr[...] = jnp.zeros_like(dbs_scr)
            dlg_ref[...] = jnp.zeros_like(dlg_ref)
            dlb_ref[...] = jnp.zeros_like(dlb_ref)
            dbin_ref[...] = jnp.zeros_like(dbin_ref)

        vhat, vn, rstd = _gm_norm(pre_ref, lg_ref, lbias_ref)
        pre_ref, gp_ref, dm_ref = _ColBlocks(pre_ref, 768), _ColBlocks(gp_ref, 768), _ColBlocks(dm_ref, 768)
        dpre_ref = _ColBlocks(dpre_ref, 768)
        for j in range(4):
            u = pre_ref[j].astype(F32)
            for e in range(2):
                g = 2 * j + e
                cs = slice(GM_GDIM * e, GM_GDIM * (e + 1))
                wg = ws_ref[g].astype(BF16)
                wgt = wst_ref[g].astype(BF16)
                for ci in range(nc):
                    rs = slice(GM_CHUNK * ci, GM_CHUNK * (ci + 1))
                    vnb = vn[j][rs, cs].astype(BF16)
                    vm = _dot(wg, vnb) + bs_ref[g]
                    dmg = dm_ref[j, rs, cs].astype(F32)
                    du_scr[j, rs, cs] = dmg * vm
                    dvm = dmg * u[rs, cs]
                    dvmb = dvm.astype(BF16)
                    dws_ref[g] += _dot(dvmb, vnb, NT)
                    dbs_scr[g] += dvm
                    dvn_scr[j, rs, cs] = _dot(wgt, dvmb)
        width = 4 * 768
        dvh = []
        for j in range(4):
            dvn = dvn_scr[j]
            dlg_ref[j:j + 1, :] += _colsum(dvn * vhat[j])
            dlb_ref[j:j + 1, :] += _colsum(dvn)
            dvh.append(dvn * lg_ref[j:j + 1, :])
        m1 = sum(jnp.sum(d, axis=1, keepdims=True) for d in dvh) / width
        m2 = sum(jnp.sum(dvh[j] * vhat[j], axis=1, keepdims=True) for j in range(4)) / width
        for j in range(4):
            dv = rstd * (dvh[j] - m1 - vhat[j] * m2)
            dpv = dv * gp_ref[4 + j].astype(F32)
            dpu = du_scr[j] * gp_ref[j].astype(F32)
            dpre_ref[4 + j] = dpv.astype(BF16)
            dpre_ref[j] = dpu.astype(BF16)
            dbin_ref[4 + j:5 + j, :] += _colsum(dpv)
            dbin_ref[j:j + 1, :] += _colsum(dpu)

        @pl.when(i == nb - 1)
        def _():
            r_i = lax.broadcasted_iota(jnp.int32, (GM_CHUNK, GM_CHUNK), 0)
            c_i = lax.broadcasted_iota(jnp.int32, (GM_CHUNK, GM_CHUNK), 1)
            for g in range(GM_GROUPS):
                dws_ref[g] = jnp.where(c_i <= r_i, dws_ref[g], 0.0)
                dbs_ref[g] = jnp.broadcast_to(jnp.sum(dbs_scr[g], axis=1, keepdims=True), (GM_CHUNK, GM_CHUNK))

    sq = pl.BlockSpec((GM_GROUPS, GM_CHUNK, GM_CHUNK), lambda i: (0, 0, 0))
    v4 = pl.BlockSpec((4, 768), lambda i: (0, 0))
    return pl.pallas_call(
        body, name=name, grid=(nb,),
        out_shape=(jax.ShapeDtypeStruct((T, 8 * 768), BF16),
                   jax.ShapeDtypeStruct((GM_GROUPS, GM_CHUNK, GM_CHUNK), F32),
                   jax.ShapeDtypeStruct((GM_GROUPS, GM_CHUNK, GM_CHUNK), F32),
                   jax.ShapeDtypeStruct((4, 768), F32), jax.ShapeDtypeStruct((4, 768), F32),
                   jax.ShapeDtypeStruct((8, 768), F32)),
        in_specs=[pl.BlockSpec((tb, 8 * 768), lambda i: (i, 0)),
                  pl.BlockSpec((tb, 8 * 768), lambda i: (i, 0)),
                  pl.BlockSpec((tb, 4 * 768), lambda i: (i, 0)),
                  v4, v4, sq, sq,
                  pl.BlockSpec((GM_GROUPS, GM_CHUNK, GM_GDIM), lambda i: (0, 0, 0))],
        out_specs=(pl.BlockSpec((tb, 8 * 768), lambda i: (i, 0)), sq, sq, v4, v4,
                   pl.BlockSpec((8, 768), lambda i: (0, 0))),
        scratch_shapes=[pltpu.VMEM((GM_GROUPS, GM_CHUNK, GM_GDIM), F32),
                        pltpu.VMEM((4, tb, 768), F32), pltpu.VMEM((4, tb, 768), F32)],
        compiler_params=_params("arbitrary"),
    )(pre, gp, dm, ln_g, ln_b, ws, ws_t, bsb)


def _adamw(slots, w, m, v, name, *, tr=256):
    S, R, C = slots.shape
    tr = next((t for t in (tr, tr // 2, tr // 4, tr // 8, tr // 16) if R % t == 0), R) if R > tr else R
    bc1 = 1.0 - ADAM_B1 ** ADAM_STEP
    bc2 = 1.0 - ADAM_B2 ** ADAM_STEP

    def body(s_ref, w_ref, m_ref, v_ref, g_ref, d_ref, nm_ref, nv_ref):
        g = s_ref[0].astype(F32)
        for s in range(1, S):
            g = g + s_ref[s].astype(F32)
        mn = ADAM_B1 * m_ref[...] + (1.0 - ADAM_B1) * g
        vn = ADAM_B2 * v_ref[...] + (1.0 - ADAM_B2) * (g * g)
        g_ref[...] = g
        nm_ref[...] = mn
        nv_ref[...] = vn
        d_ref[...] = -ADAM_LR * ((mn / bc1) / (jnp.sqrt(vn / bc2) + ADAM_EPS) + ADAM_WD * w_ref[...])

    spec = pl.BlockSpec((tr, C), lambda i: (i, 0))
    return pl.pallas_call(
        body, name=name, grid=(R // tr,),
        out_shape=(jax.ShapeDtypeStruct((R, C), F32),) * 4,
        in_specs=[pl.BlockSpec((S, tr, C), lambda i: (0, i, 0)), spec, spec, spec],
        out_specs=(spec,) * 4,
        compiler_params=_params("parallel"),
    )(slots, w, m, v)


def _update(slots, w, m, v, name):
    shp = w.shape
    C = shp[-1]
    R = math.prod(shp[:-1])
    outs = _adamw(slots.reshape(slots.shape[0], R, C), w.reshape(R, C), m.reshape(R, C), v.reshape(R, C), name)
    return tuple(o.reshape(shp) for o in outs)


def kernel(x, c, ada_w, ada_b, norm_pre, norm_post, ffn_w_in, ffn_w_out, hg_w_in, hg_w_out, hg_out_norm, hg_lb, gm_w_in, gm_b_in, gm_ln_g, gm_ln_b, gm_w_s, gm_b_s, gm_w_out, loss_target, m_ada_w, m_ada_b, m_norm_pre, m_norm_post, m_ffn_w_in, m_ffn_w_out, m_hg_w_in, m_hg_w_out, m_hg_out_norm, m_hg_lb, m_gm_w_in, m_gm_b_in, m_gm_ln_g, m_gm_ln_b, m_gm_w_s, m_gm_b_s, m_gm_w_out, v_ada_w, v_ada_b, v_norm_pre, v_norm_post, v_ffn_w_in, v_ffn_w_out, v_hg_w_in, v_hg_w_out, v_hg_out_norm, v_hg_lb, v_gm_w_in, v_gm_b_in, v_gm_ln_g, v_gm_ln_b, v_gm_w_s, v_gm_b_s, v_gm_w_out):
    me = 4 * lax.axis_index("x") + 2 * lax.axis_index("y") + lax.axis_index("c")
    T = x.shape[1]
    x0 = x.reshape(T, D_MODEL)
    target = loss_target.reshape(T, D_MODEL)
    n_ada = ada_w.shape[-1]

    pack = jnp.concatenate([
        c.reshape(8, 128), norm_pre.reshape(6, 128), norm_post.reshape(6, 128),
        gm_b_in.reshape(6, 128), gm_ln_g.reshape(3, 128), gm_ln_b.reshape(3, 128)], axis=0)
    packs = _all_gather(pack, "gather_small")
    c_all = packs[:, 0:8].reshape(NDEV, D_MODEL)
    npre = packs[:, 8:14].reshape(NDEV, 2, 3, 128).transpose(1, 2, 0, 3).reshape(2, 3, D_MODEL)
    npost = packs[:, 14:20].reshape(NDEV, 2, 3, 128).transpose(1, 2, 0, 3).reshape(2, 3, D_MODEL)
    b_in = packs[:, 20:26].reshape(1, NDEV * 768)
    ln_g = packs[:, 26:29].reshape(4, 768)
    ln_b = packs[:, 29:32].reshape(4, 768)

    ada_b_mine = lax.dynamic_slice_in_dim(ada_b, me * n_ada, n_ada, axis=1).reshape(2, 1, n_ada)
    mod_cols = _ada_fwd(c_all, ada_w, ada_b_mine, "ada_fwd")
    mod_all = _all_gather(mod_cols, "gather_mod")
    mod = lax.dynamic_index_in_dim(mod_all, me, axis=2, keepdims=False)
    mod = mod.transpose(1, 0, 2).reshape(2, 9, 1, D_MODEL)

    sh_fi, sh_fo = ffn_w_in.astype(BF16).swapaxes(-1, -2), ffn_w_out.astype(BF16)
    sh_hi, sh_ho = hg_w_in[0].astype(BF16).T, hg_w_out[0].astype(BF16)
    sh_mi, sh_mo = gm_w_in[0].astype(BF16).T, gm_w_out[0].astype(BF16)

    def whole(gathered):
        return gathered.reshape(-1, D_MODEL)

    w_fi = {(0, 0): whole(_all_gather(sh_fi[0, 0], "gather_ffn_in_first"))}
    w_fo = {}
    riders = {"l0s0": [sh_fo[0, 0], sh_hi, sh_ho], "l0s1": [sh_fi[0, 1], sh_fo[0, 1]], "hg_mix": [sh_mi, sh_mo],
              "l0s2": [sh_fi[1, 0], sh_fo[1, 0]], "l1s0": [sh_fi[1, 1], sh_fo[1, 1]]}

    sm = jax.nn.softmax(hg_lb, axis=0)
    lb0 = sm[0:1]
    on = hg_out_norm.reshape(1, HG_HEAD)
    tril = jnp.tril(jnp.ones((GM_CHUNK, GM_CHUNK), F32))
    ws = gm_w_s[0] * tril[None]
    ws_t = ws.transpose(0, 2, 1)
    bsb = jnp.broadcast_to(gm_b_s[0][:, :, None], (GM_GROUPS, GM_CHUNK, GM_GDIM))

    res_ws = (0.5, 1.0, 0.5)

    def vecs(i, s):
        return (npre[i, s].reshape(1, D_MODEL), npost[i, s].reshape(1, D_MODEL),
                mod[i, 3 * s], mod[i, 3 * s + 1], mod[i, 3 * s + 2])

    order = [(i, s) for i in range(2) for s in range(3)]
    saved = {}
    xs = x0
    pre_g, _, shift, scale, _ = vecs(0, 0)
    h = _prenorm_fwd(xs, pre_g, scale, shift, "prenorm_l0s0")
    for pos, (i, s) in enumerate(order):
        tag = f"l{i}s{s}"
        _, post_g, _, _, gate = vecs(i, s)
        rider = _Exchange("gather", riders[tag]) if tag in riders else None
        if s != 1:
            pq, a, got = _ffn_in(h, w_fi[i, s // 2], "ffn_in_" + tag, exchange=rider)
            extra = (pq, a)
            if tag == "l0s0":
                w_fo[0, 0], w_hi, w_ho = map(whole, got)
            elif tag == "l0s2":
                w_fi[1, 0], w_fo[1, 0] = map(whole, got)
            elif tag == "l1s0":
                w_fi[1, 1], w_fo[1, 1] = map(whole, got)
            wo = w_fo[i, s // 2]
        elif i == 0:
            proj, *got = _mm_blocks(h, w_hi, "hg_in", out_dtype=F32, exchange=rider)
            w_fi[0, 1], w_fo[0, 1] = map(whole, got)
            o, og, states, got = _hgrn_fwd(proj, lb0, on, "hg_mix", exchange=_Exchange("gather", riders["hg_mix"]))
            w_mi, w_mo = map(whole, got)
            a, wo = og, w_ho
            extra = (proj, o, og, states)
        else:
            pre, gp = _mm_blocks(h, w_mi, "gm_in", bias=b_in, gelu=True)
            a = _gm_spatial_fwd(pre, ln_g, ln_b, ws, bsb, "gm_mix")
            wo = w_mo
            extra = (pre, gp, a)
        if pos + 1 < len(order):
            npre_g, _, nshift, nscale, _ = vecs(*order[pos + 1])
            y, x_next, h_next = _out_proj(a, wo, xs, post_g, gate, res_ws[s], (npre_g, nscale, nshift), "out_" + tag)
            saved[tag] = (xs, h, y) + extra
            xs, h = x_next, h_next
        else:
            dx, dy, dgate, dpost, loss_part = _out_proj_last(a, wo, xs, post_g, gate, res_ws[s], target, "out_" + tag)
            saved[tag] = (xs, h, None) + extra
    loss = lax.psum(loss_part[0, 0], ("x", "y", "c"))

    slots = {}
    d_npre = [[None] * 3, [None] * 3]
    d_npost = [[None] * 3, [None] * 3]
    d_mod = [[None] * 9, [None] * 9]
    for pos in reversed(range(len(order))):
        i, s = order[pos]
        tag = f"l{i}s{s}"
        pre_g, _, _, scale, _ = vecs(i, s)
        xin, h = saved[tag][:2]
        if s != 1:
            w_in, wo = w_fi[i, s // 2], w_fo[i, s // 2]
            pq, g = saved[tag][3:]
            dz = _ffn_dgate(dy, wo, pq, "ffn_dgate_" + tag)
            g_out = _mm_wgrad(g, dy, "ffn_out_wgrad_" + tag, xw=1408)
            g_in = _mm_wgrad(dz, h, "ffn_in_wgrad_" + tag, xw=1408)
        elif i == 0:
            proj, o, og, states = saved[tag][3:]
            dog = _mm_blocks(dy, w_ho, "hg_out_dgrad")
            g_out = _mm_wgrad(og, dy, "hg_out_wgrad")
            dz, d_lb0, d_on = _hgrn_bwd(proj, o, dog, states, lb0, on, "hg_mix_bwd")
            w_in = w_hi
            g_in = _mm_wgrad(h, dz, "hg_in_wgrad", yw=512)
        else:
            pre, gp, sp = saved[tag][3:]
            dm = _mm_blocks(dy, w_mo, "gm_out_dgrad")
            g_out = _mm_wgrad(sp, dy, "gm_out_wgrad", xw=768)
            dz, d_ws, d_bs, d_lg, d_lbias, d_bin = _gm_spatial_bwd(pre, gp, dm, ln_g, ln_b, ws, ws_t, bsb, "gm_mix_bwd")
            w_in = w_mi
            g_in = _mm_wgrad(h, dz, "gm_in_wgrad", yw=768)
        g_out = g_out.reshape(NDEV, -1, D_MODEL)
        g_in = g_in.reshape(NDEV, -1, D_MODEL) if s != 1 else g_in
        d_npost[i][s] = dpost
        d_mod[i][3 * s + 2] = dgate
        rider = _Exchange("scatter", [g_in, g_out])
        if pos > 0:
            pi, ps = order[pos - 1]
            _, ppost_g, _, _, pgate = vecs(pi, ps)
            prev = (saved[f"l{pi}s{ps}"][2], ppost_g, pgate, res_ws[ps])
            dx, dshift, dscale, dpre_g, dy, dgate, dpost, r_in, r_out = _in_grad(
                dz, w_in, dx, xin, pre_g, scale, prev, "in_grad_" + tag, exchange=rider)
        else:
            dx, dshift, dscale, dpre_g, r_in, r_out = _in_grad(
                dz, w_in, dx, xin, pre_g, scale, None, "in_grad_" + tag, exchange=rider)
        slots[tag] = (r_in, r_out)
        d_npre[i][s] = dpre_g
        d_mod[i][3 * s], d_mod[i][3 * s + 1] = dshift, dscale
    grad_x = dx.reshape(x.shape)

    ffn_tags = ["l0s0", "l0s2", "l1s0", "l1s2"]
    s_fi = jnp.stack([slots[t][0] for t in ffn_tags], axis=1).swapaxes(-1, -2)
    s_fo = jnp.stack([slots[t][1] for t in ffn_tags], axis=1)
    (s_hi, s_ho), (s_mi, s_mo) = slots["l0s1"], slots["l1s1"]
    s_hi, s_ho, s_mi, s_mo = s_hi[:, None], s_ho[:, None], s_mi[:, None], s_mo[:, None]

    gmod = jnp.stack([jnp.concatenate(d_mod[i], axis=0) for i in range(2)])
    d_sm = lb0 * d_lb0
    d_hg_lb = jnp.concatenate([d_sm, jnp.zeros((2, D_MODEL), F32)], axis=0) - sm * d_sm
    small = [gmod, jnp.stack([jnp.concatenate(r, axis=0) for r in d_npre]),
             jnp.stack([jnp.concatenate(r, axis=0) for r in d_npost]),
             d_on, d_hg_lb, d_bin, d_lg, d_lbias, d_ws, d_bs[:, :, 0]]
    sizes = [a.size for a in small]
    flat = jnp.concatenate([a.reshape(-1) for a in small])
    rows = -(-flat.size // (8 * 128)) * 8
    flat = jnp.pad(flat, (0, rows * 128 - flat.size)).reshape(rows, 128)
    flats = _all_gather(flat, "gather_small_grads").reshape(NDEV, rows * 128)
    parts, off = [], 0
    for a, n in zip(small, sizes):
        parts.append(flats[:, off:off + n].reshape((NDEV,) + a.shape))
        off += n
    p_mod, p_npre, p_npost, p_on, p_lb, p_bin, p_lg, p_lbias, p_ws, p_bs = parts

    def mine(p, width):
        return lax.dynamic_slice_in_dim(p, me * width, width, axis=p.ndim - 1)

    gmod_cols = mine(p_mod.reshape(NDEV, 2, 9 * D_MODEL), n_ada).transpose(1, 0, 2)
    g_ada_w = _ada_bwd(jnp.pad(c_all.T, ((0, 0), (0, 120))), jnp.pad(gmod_cols, ((0, 0), (0, 120), (0, 0))), "ada_bwd")

    out = {}
    out["ada_w"] = _update(g_ada_w[None], ada_w, m_ada_w, v_ada_w, "adamw_ada_w")
    out["ada_b"] = _update(p_mod.reshape(NDEV, 2, 9 * D_MODEL), ada_b, m_ada_b, v_ada_b, "adamw_ada_b")
    out["norm_pre"] = _update(mine(p_npre, 128), norm_pre, m_norm_pre, v_norm_pre, "adamw_norm_pre")
    out["norm_post"] = _update(mine(p_npost, 128), norm_post, m_norm_post, v_norm_post, "adamw_norm_post")
    out["ffn_w_in"] = _update(s_fi.reshape((NDEV,) + ffn_w_in.shape), ffn_w_in, m_ffn_w_in, v_ffn_w_in, "adamw_ffn_in")
    out["ffn_w_out"] = _update(s_fo.reshape((NDEV,) + ffn_w_out.shape), ffn_w_out, m_ffn_w_out, v_ffn_w_out, "adamw_ffn_out")
    out["hg_w_in"] = _update(s_hi, hg_w_in, m_hg_w_in, v_hg_w_in, "adamw_hg_in")
    out["hg_w_out"] = _update(s_ho, hg_w_out, m_hg_w_out, v_hg_w_out, "adamw_hg_out")
    out["hg_out_norm"] = _update(p_on, hg_out_norm, m_hg_out_norm, v_hg_out_norm, "adamw_hg_norm")
    out["hg_lb"] = _update(p_lb, hg_lb, m_hg_lb, v_hg_lb, "adamw_hg_lb")
    out["gm_w_in"] = _update(s_mi, gm_w_in, m_gm_w_in, v_gm_w_in, "adamw_gm_in")
    out["gm_b_in"] = _update(mine(p_bin.reshape(NDEV, 1, 8 * 768), 768), gm_b_in, m_gm_b_in, v_gm_b_in, "adamw_gm_b_in")
    out["gm_ln_g"] = _update(mine(p_lg.reshape(NDEV, 1, 4 * 768), 384), gm_ln_g, m_gm_ln_g, v_gm_ln_g, "adamw_gm_ln_g")
    out["gm_ln_b"] = _update(mine(p_lbias.reshape(NDEV, 1, 4 * 768), 384), gm_ln_b, m_gm_ln_b, v_gm_ln_b, "adamw_gm_ln_b")
    out["gm_w_s"] = _update(p_ws[:, None], gm_w_s, m_gm_w_s, v_gm_w_s, "adamw_gm_w_s")
    out["gm_b_s"] = _update(p_bs[:, None], gm_b_s, m_gm_b_s, v_gm_b_s, "adamw_gm_b_s")
    out["gm_w_out"] = _update(s_mo, gm_w_out, m_gm_w_out, v_gm_w_out, "adamw_gm_out")

    names = ["ada_w", "ada_b", "norm_pre", "norm_post", "ffn_w_in", "ffn_w_out", "hg_w_in", "hg_w_out",
             "hg_out_norm", "hg_lb", "gm_w_in", "gm_b_in", "gm_ln_g", "gm_ln_b", "gm_w_s", "gm_b_s", "gm_w_out"]
    return (loss, grad_x, *[out[n][0] for n in names], *[out[n][1] for n in names],
            *[out[n][2] for n in names], *[out[n][3] for n in names])
```

```python
import math

import jax
import jax.numpy as jnp
from jax import lax
from jax.experimental import pallas as pl
from jax.experimental.pallas import tpu as pltpu

F32 = jnp.float32
BF16 = jnp.bfloat16
NDEV = 8
D_MODEL = 1024
EPS = 1e-6
HG_CHUNK = 64
HG_HEAD = 128
HG_HEADS = 8
GM_CHUNK = 128
GM_GDIM = 384
GM_GROUPS = 8
ADAM_LR = 0.001
ADAM_B1 = 0.9
ADAM_B2 = 0.999
ADAM_EPS = 1e-08
ADAM_WD = 0.01
ADAM_STEP = 10
VMEM_LIMIT = 56 * 2 ** 20

NN = (((1,), (0,)), ((), ()))
NT = (((1,), (1,)), ((), ()))
TN = (((0,), (0,)), ((), ()))
MESH = pl.DeviceIdType.MESH
ANY = pl.BlockSpec(memory_space=pl.ANY)


def _dot(a, b, dims=NN, precision=None):
    return lax.dot_general(a, b, dims, preferred_element_type=F32, precision=precision)


def _params(*sem):
    return pltpu.CompilerParams(dimension_semantics=sem, vmem_limit_bytes=VMEM_LIMIT)


def _sigmoid(x):
    return 1.0 / (1.0 + jnp.exp(-x))


def _sigmoid_t(x):
    return 0.5 * jnp.tanh(0.5 * x) + 0.5


def _gelu_and_grad(x):
    c = math.sqrt(2.0 / math.pi)
    x2 = x * x
    t = jnp.tanh(x * (c + (c * 0.044715) * x2))
    hx = 0.5 * x
    p = 1.0 + t
    return hx * p, 0.5 * p + hx * (1.0 - t * t) * (c + (3.0 * c * 0.044715) * x2)


def _colsum(x):
    return jnp.sum(x, axis=0, keepdims=True)


def _rowmean(x):
    return jnp.mean(x, axis=-1, keepdims=True)


def _all_gather(shard, name):
    def body(x_ref, out_ref, send_sems, recv_sems, local_sem):
        x, y, c = lax.axis_index("x"), lax.axis_index("y"), lax.axis_index("c")
        me, sibling = (x, y, c), (x, y, 1 - c)
        chips = [(1 - x, y), (x, 1 - y), (1 - x, 1 - y)]

        def slot(p):
            return out_ref.at[4 * p[0] + 2 * p[1] + p[2]]

        def copy(k, block, to, src=None):
            return pltpu.make_async_remote_copy(
                src_ref=slot(block) if src is None else src, dst_ref=slot(block),
                send_sem=send_sems.at[k], recv_sem=recv_sems.at[k],
                device_id=to, device_id_type=MESH)

        mine = pltpu.make_async_copy(x_ref, slot(me), local_sem)
        mine.start()
        first = [copy(0, me, sibling, src=x_ref)]
        first += [copy(1 + j, me, (*chip, c), src=x_ref) for j, chip in enumerate(chips)]
        for cp in first:
            cp.start()
        passed = [copy(4 + j, (*chip, c), sibling) for j, chip in enumerate(chips)]
        for j, chip in enumerate(chips):
            copy(1 + j, (*chip, c), me).wait_recv()
            passed[j].start()
        copy(0, sibling, me).wait_recv()
        for j, chip in enumerate(chips):
            copy(4 + j, (*chip, 1 - c), me).wait_recv()
        for cp in first + passed:
            cp.wait_send()
        mine.wait()

    return pl.pallas_call(
        body, name=name,
        out_shape=jax.ShapeDtypeStruct((NDEV,) + shard.shape, shard.dtype),
        in_specs=[ANY], out_specs=ANY,
        scratch_shapes=[pltpu.SemaphoreType.DMA((7,)), pltpu.SemaphoreType.DMA((7,)),
                        pltpu.SemaphoreType.DMA(())],
    )(shard)


class _Exchange:
    def __init__(self, kind, arrays):
        self.gather = kind == "gather"
        self.arrays = list(arrays)
        self.n = n = len(self.arrays)
        self.out_shape = [jax.ShapeDtypeStruct(((NDEV,) + a.shape) if self.gather else a.shape, a.dtype)
                          for a in self.arrays]
        self.scratch = [pltpu.SemaphoreType.DMA((n, NDEV - 1)), pltpu.SemaphoreType.DMA((n, NDEV - 1)),
                        pltpu.SemaphoreType.DMA((n,))]

    def _copies(self, in_refs, out_refs, sems):
        send_sems, recv_sems, local_sems = sems
        x, y, c = lax.axis_index("x"), lax.axis_index("y"), lax.axis_index("c")
        me = 4 * x + 2 * y + c
        peers = [(1 - x if k & 4 else x, 1 - y if k & 2 else y, 1 - c if k & 1 else c) for k in range(1, NDEV)]
        local, send, recv = [], [], []
        for a in range(self.n):
            src = (lambda pid, a=a: in_refs[a]) if self.gather else (lambda pid, a=a: in_refs[a].at[pid])
            local.append(pltpu.make_async_copy(src(me), out_refs[a].at[me], local_sems.at[a]))
            for k, p in enumerate(peers):
                pid = 4 * p[0] + 2 * p[1] + p[2]
                for lst, slot in ((send, me), (recv, pid)):
                    lst.append(pltpu.make_async_remote_copy(
                        src_ref=src(pid), dst_ref=out_refs[a].at[slot],
                        send_sem=send_sems.at[a, k], recv_sem=recv_sems.at[a, k],
                        device_id=p, device_id_type=MESH))
        return local, send, recv

    def start(self, first, in_refs, out_refs, sems):
        @pl.when(first)
        def _():
            local, send, _ = self._copies(in_refs, out_refs, sems)
            for cp in local + send:
                cp.start()

    def finish(self, last, in_refs, out_refs, sems):
        @pl.when(last)
        def _():
            local, send, recv = self._copies(in_refs, out_refs, sems)
            for cp in send:
                cp.wait_send()
            for cp in recv:
                cp.wait_recv()
            for cp in local:
                cp.wait()


def _host(exchange, n_in, n_out, body, first_last):
    if exchange is None:
        return body, [], [], [], []
    n = exchange.n

    def hosted(*refs):
        ins, refs = refs[:n_in], refs[n_in:]
        xin, refs = refs[:n], refs[n:]
        outs, refs = refs[:n_out], refs[n_out:]
        xout, refs = refs[:n], refs[n:]
        scratch, sems = refs[:len(refs) - 3], refs[len(refs) - 3:]
        first, last = first_last()
        exchange.start(first, xin, xout, sems)
        body(*ins, *outs, *scratch)
        exchange.finish(last, xin, xout, sems)

    return hosted, exchange.arrays, [ANY] * n, exchange.out_shape, exchange.scratch


def _first_last(steps):
    def at():
        i = pl.program_id(0)
        return i == 0, i == steps - 1
    return at


class _ColBlocks:
    def __init__(self, ref, width):
        self.ref, self.width = ref, width

    def _index(self, key):
        key = key if isinstance(key, tuple) else (key,)
        rows = key[1] if len(key) > 1 else slice(None)
        cols = key[2] if len(key) > 2 else slice(0, self.width)
        c0 = key[0] * self.width
        return rows, slice(c0 + cols.start, c0 + cols.stop)

    def __getitem__(self, key):
        return self.ref[self._index(key)]

    def __setitem__(self, key, value):
        self.ref[self._index(key)] = value


def _col_chunks(width, chunk=768):
    return [slice(c, min(c + chunk, width)) for c in range(0, width, chunk)]


def _row_spec(tm, d):
    return pl.BlockSpec((tm, d), lambda m: (m, 0))


def _vec_spec(d):
    return pl.BlockSpec((1, d), lambda m: (0, 0))


def _whole_spec(w):
    nd = w.ndim
    return pl.BlockSpec(w.shape, lambda m: (0,) * nd, pipeline_mode=pl.Buffered(1))


def _mm_blocks(a, w, name, *, bias=None, out_dtype=BF16, tm=512, chunk=768, gelu=False, halves=False, exchange=None):
    T, K = a.shape
    C = w.shape[0]
    tm = min(tm, T)
    n_in = 2 + (bias is not None)
    width = C // 2 if halves else C
    assert not halves or (not gelu and width % chunk == 0)

    def body(*refs):
        a_ref, w_ref = refs[:2]
        av = a_ref[...]
        for cols in _col_chunks(C, chunk):
            r = _dot(av, w_ref[cols], NT)
            if bias is not None:
                r = r + refs[2][:, cols]
            if gelu:
                z, dz = _gelu_and_grad(r)
                refs[n_in][:, cols] = z.astype(BF16)
                refs[n_in + 1][:, cols] = dz.astype(BF16)
            else:
                k, c0 = divmod(cols.start, width)
                refs[n_in + k][:, c0:c0 + cols.stop - cols.start] = r.astype(out_dtype)

    in_specs = [_row_spec(tm, K), _whole_spec(w)]
    args = [a, w]
    if bias is not None:
        in_specs.append(_whole_spec(bias))
        args.append(bias)
    if gelu:
        outs = [jax.ShapeDtypeStruct((T, C), BF16)] * 2
    else:
        outs = [jax.ShapeDtypeStruct((T, width), out_dtype)] * (C // width)
    body, x_args, x_in, x_out, x_scratch = _host(exchange, n_in, len(outs), body, _first_last(T // tm))
    res = pl.pallas_call(
        body, name=name, grid=(T // tm,),
        out_shape=outs + x_out,
        in_specs=in_specs + x_in,
        out_specs=[_row_spec(tm, o.shape[1]) for o in outs] + x_in,
        scratch_shapes=x_scratch,
        compiler_params=_params("arbitrary" if exchange else "parallel"),
    )(*args, *x_args)
    return res if (exchange or gelu or halves) else res[0]


def _rms(v):
    return lax.rsqrt(_rowmean(v * v) + EPS)


def _zero_at_start(*refs):
    @pl.when(pl.program_id(0) == 0)
    def _():
        for r in refs:
            r[...] = jnp.zeros_like(r)


def _postnorm_bwd_math(dxo, yv, g, gate, res_w, dgate_ref, dpost_ref):
    r = _rms(yv)
    yh = yv * r
    both = res_w * _colsum(dxo * yh)
    dgate_ref[...] += g * both
    dpost_ref[...] += gate * both
    dyh = dxo * (res_w * gate * g)
    return (r * (dyh - yh * _rowmean(dyh * yh))).astype(BF16)


def _out_proj(a, w, x, post_g, gate, res_w, nxt, name, *, tm=512):
    T, ka = a.shape
    d = w.shape[1]
    tm = min(tm, T)

    def body(a_ref, w_ref, x_ref, pg_ref, gate_ref, ng_ref, nsc_ref, nsh_ref, y_ref, xn_ref, h_ref):
        y = _dot(a_ref[...], w_ref[...])
        y_ref[...] = y
        xn = x_ref[...] + (y * _rms(y)) * (res_w * gate_ref[...] * pg_ref[...])
        xn_ref[...] = xn
        h_ref[...] = ((xn * _rms(xn)) * (ng_ref[...] * (1.0 + nsc_ref[...])) + nsh_ref[...]).astype(BF16)

    return pl.pallas_call(
        body, name=name, grid=(T // tm,),
        out_shape=(jax.ShapeDtypeStruct((T, d), F32), jax.ShapeDtypeStruct((T, d), F32),
                   jax.ShapeDtypeStruct((T, d), BF16)),
        in_specs=[_row_spec(tm, ka), _whole_spec(w), _row_spec(tm, d)] + [_vec_spec(d)] * 5,
        out_specs=(_row_spec(tm, d),) * 3,
        compiler_params=_params("parallel"),
    )(a, w, x, post_g, gate, *nxt)


def _out_proj_last(a, w, x, post_g, gate, res_w, target, name, *, tm=512):
    T, ka = a.shape
    d = w.shape[1]
    tm = min(tm, T)

    def body(a_ref, w_ref, x_ref, pg_ref, gate_ref, t_ref, dx_ref, dy_ref, dgate_ref, dpost_ref, l_ref):
        _zero_at_start(dgate_ref, dpost_ref, l_ref)
        y = _dot(a_ref[...], w_ref[...])
        e = x_ref[...] + res_w * gate_ref[...] * (y * _rms(y) * pg_ref[...]) - t_ref[...]
        l_ref[...] += 0.5 * jnp.sum(_rowmean(e * e), axis=0, keepdims=True)
        dx = e * (1.0 / d)
        dx_ref[...] = dx
        dy_ref[...] = _postnorm_bwd_math(dx, y, pg_ref[...], gate_ref[...], res_w, dgate_ref, dpost_ref)

    return pl.pallas_call(
        body, name=name, grid=(T // tm,),
        out_shape=(jax.ShapeDtypeStruct((T, d), F32), jax.ShapeDtypeStruct((T, d), BF16),
                   jax.ShapeDtypeStruct((1, d), F32), jax.ShapeDtypeStruct((1, d), F32),
                   jax.ShapeDtypeStruct((1, 128), F32)),
        in_specs=[_row_spec(tm, ka), _whole_spec(w), _row_spec(tm, d), _vec_spec(d), _vec_spec(d), _row_spec(tm, d)],
        out_specs=(_row_spec(tm, d), _row_spec(tm, d), _vec_spec(d), _vec_spec(d),
                   pl.BlockSpec((1, 128), lambda m: (0, 0))),
        compiler_params=_params("arbitrary"),
    )(a, w, x, post_g, gate, target)


def _in_grad(dz, w, dxo, x, pre_g, scale, prev, name, *, tm=512, exchange=None):
    parts = list(dz) if isinstance(dz, (list, tuple)) else [dz]
    n_dz = len(parts)
    T = parts[0].shape[0]
    d = w.shape[1]
    tm = min(tm, T)
    has_prev = prev is not None
    res_w = prev[3] if has_prev else None

    def body(*refs):
        dz_refs, refs = refs[:n_dz], refs[n_dz:]
        w_ref, dxo_ref, x_ref, g_ref, sc_ref = refs[:5]
        if has_prev:
            yp_ref, ppg_ref, pgate_ref, dx_ref, dsh_ref, dsc_ref, dg_ref, dyp_ref, dgate_ref, dpost_ref = refs[5:]
            _zero_at_start(dsh_ref, dsc_ref, dg_ref, dgate_ref, dpost_ref)
        else:
            dx_ref, dsh_ref, dsc_ref, dg_ref = refs[5:]
            _zero_at_start(dsh_ref, dsc_ref, dg_ref)
        dh, row = None, 0
        for dz_ref in dz_refs:
            rows = dz_ref.shape[1]
            part = _dot(dz_ref[...], w_ref[row:row + rows])
            dh = part if dh is None else dh + part
            row += rows
        xv = x_ref[...]
        r = _rms(xv)
        xh = xv * r
        gain = 1.0 + sc_ref[...]
        both = _colsum(dh * xh)
        dsh_ref[...] += _colsum(dh)
        dsc_ref[...] += g_ref[...] * both
        dg_ref[...] += gain * both
        dxh = dh * (gain * g_ref[...])
        dx = dxo_ref[...] + r * (dxh - xh * _rowmean(dxh * xh))
        dx_ref[...] = dx
        if has_prev:
            dyp_ref[...] = _postnorm_bwd_math(dx, yp_ref[...], ppg_ref[...], pgate_ref[...], res_w,
                                               dgate_ref, dpost_ref)

    vec = jax.ShapeDtypeStruct((1, d), F32)
    in_specs = ([_row_spec(tm, p.shape[1]) for p in parts]
                + [_whole_spec(w), _row_spec(tm, d), _row_spec(tm, d), _vec_spec(d), _vec_spec(d)])
    out_shape = [jax.ShapeDtypeStruct((T, d), F32), vec, vec, vec]
    out_specs = [_row_spec(tm, d), _vec_spec(d), _vec_spec(d), _vec_spec(d)]
    args = parts + [w, dxo, x, pre_g, scale]
    if has_prev:
        in_specs += [_row_spec(tm, d), _vec_spec(d), _vec_spec(d)]
        out_shape += [jax.ShapeDtypeStruct((T, d), BF16), vec, vec]
        out_specs += [_row_spec(tm, d), _vec_spec(d), _vec_spec(d)]
        args += list(prev[:3])
    body, x_args, x_in, x_out, x_scratch = _host(exchange, len(args), len(out_shape), body, _first_last(T // tm))
    return pl.pallas_call(
        body, name=name, grid=(T // tm,),
        out_shape=out_shape + x_out, in_specs=in_specs + x_in, out_specs=out_specs + x_in,
        scratch_shapes=x_scratch,
        compiler_params=_params("arbitrary"),
    )(*args, *x_args)


def _mm_wgrad(x, y, name, *, xw=None, yw=None, tt=2048):
    T, P = x.shape
    Q = y.shape[1]
    xw, yw = xw or P, yw or Q
    jx, jy = P // xw, Q // yw
    assert jx == 1 or jy == 1
    tt = min(tt, T)
    nt = T // tt

    def body(x_ref, y_ref, o_ref, acc_ref):
        t = pl.program_id(1)

        @pl.when(t == 0)
        def _():
            acc_ref[...] = jnp.zeros_like(acc_ref)

        acc_ref[...] += _dot(x_ref[...], y_ref[...], TN)

        @pl.when(t == nt - 1)
        def _():
            o_ref[...] = acc_ref[...].astype(BF16)

    if jy > 1:
        out_shape = jax.ShapeDtypeStruct((jy, P, yw), BF16)
        out_spec = pl.BlockSpec((None, P, yw), lambda j, t: (j, 0, 0))
    else:
        out_shape = jax.ShapeDtypeStruct((P, Q), BF16)
        out_spec = pl.BlockSpec((xw, Q), lambda j, t: (j, 0))
    return pl.pallas_call(
        body, name=name, grid=(max(jx, jy), nt),
        out_shape=out_shape,
        in_specs=[pl.BlockSpec((tt, xw), (lambda j, t: (t, j)) if jx > 1 else (lambda j, t: (t, 0))),
                  pl.BlockSpec((tt, yw), (lambda j, t: (t, j)) if jy > 1 else (lambda j, t: (t, 0)))],
        out_specs=out_spec,
        scratch_shapes=[pltpu.VMEM((xw, yw), F32)],
        compiler_params=_params("parallel", "arbitrary"),
    )(x, y)


def _ffn_in(h, wt, name, *, tm=512, exchange=None):
    T, K = h.shape
    F = wt.shape[0] // 2
    tm = min(tm, T)

    def body(h_ref, w_ref, p_ref, q_ref, g_ref):
        hh = h_ref[...]
        for cols in _col_chunks(F):
            a = _dot(hh, w_ref[cols], NT)
            b = _dot(hh, w_ref[F + cols.start:F + cols.stop], NT)
            s = _sigmoid_t(a)
            silu = a * s
            p_ref[:, cols] = (b * (s * (1.0 + a * (1.0 - s)))).astype(BF16)
            q_ref[:, cols] = silu.astype(BF16)
            g_ref[:, cols] = (silu * b).astype(BF16)

    body, x_args, x_in, x_out, x_scratch = _host(exchange, 2, 3, body, _first_last(T // tm))
    res = pl.pallas_call(
        body, name=name, grid=(T // tm,),
        out_shape=[jax.ShapeDtypeStruct((T, F), BF16)] * 3 + x_out,
        in_specs=[_row_spec(tm, K), _whole_spec(wt)] + x_in,
        out_specs=[_row_spec(tm, F)] * 3 + x_in,
        scratch_shapes=x_scratch,
        compiler_params=_params("arbitrary" if exchange else "parallel"),
    )(h, wt, *x_args)
    return res[0], res[1], res[2], res[3:]


def _ffn_dgate(dy, w_out, p, q, name, *, tm=512):
    T, N = dy.shape
    F = w_out.shape[0]
    tm = min(tm, T)

    def body(dy_ref, w_ref, p_ref, q_ref, da_ref, db_ref):
        dyv = dy_ref[...]
        for cols in _col_chunks(F):
            dg = _dot(dyv, w_ref[cols], NT)
            da_ref[:, cols] = (dg * p_ref[:, cols].astype(F32)).astype(BF16)
            db_ref[:, cols] = (dg * q_ref[:, cols].astype(F32)).astype(BF16)

    return pl.pallas_call(
        body, name=name, grid=(T // tm,),
        out_shape=[jax.ShapeDtypeStruct((T, F), BF16)] * 2,
        in_specs=[_row_spec(tm, N), _whole_spec(w_out), _row_spec(tm, F), _row_spec(tm, F)],
        out_specs=[_row_spec(tm, F)] * 2,
        compiler_params=_params("parallel"),
    )(dy, w_out, p, q)


def _prenorm_fwd(x, pre_g, scale, shift, name, *, tm=512):
    T, d = x.shape
    tm = min(tm, T)

    def body(x_ref, g_ref, sc_ref, sh_ref, h_ref):
        xv = x_ref[...]
        h_ref[...] = (xv * _rms(xv) * g_ref[...] * (1.0 + sc_ref[...]) + sh_ref[...]).astype(BF16)

    return pl.pallas_call(
        body, name=name, grid=(T // tm,),
        out_shape=jax.ShapeDtypeStruct((T, d), BF16),
        in_specs=[_row_spec(tm, d), _vec_spec(d), _vec_spec(d), _vec_spec(d)],
        out_specs=_row_spec(tm, d),
        compiler_params=_params("parallel"),
    )(x, pre_g, scale, shift)


def _ada_fwd(c_all, w, b, name):
    L, K, n = w.shape

    def body(c_ref, w_ref, b_ref, o_ref):
        cv = c_ref[...]
        cond = cv * _sigmoid(cv)
        for l in range(L):
            o_ref[l] = _dot(cond, w_ref[l], precision=lax.Precision.HIGHEST) + b_ref[l]

    return pl.pallas_call(
        body, name=name,
        out_shape=jax.ShapeDtypeStruct((L, NDEV, n), F32),
        compiler_params=pltpu.CompilerParams(vmem_limit_bytes=VMEM_LIMIT),
    )(c_all, w, b)


def _ada_bwd(c_all_t, gmod, name):
    L, _, n = gmod.shape
    K = c_all_t.shape[0]

    def body(c_ref, g_ref, o_ref):
        cv = c_ref[...]
        cond = cv * _sigmoid(cv)
        for l in range(L):
            o_ref[l] = _dot(cond, g_ref[l], precision=lax.Precision.HIGHEST)

    return pl.pallas_call(
        body, name=name,
        out_shape=jax.ShapeDtypeStruct((L, K, n), F32),
        compiler_params=pltpu.CompilerParams(vmem_limit_bytes=VMEM_LIMIT),
    )(c_all_t, gmod)


def _tri(n, upper=False, block=None):
    r = lax.broadcasted_iota(jnp.int32, (n, n), 0)
    c = lax.broadcasted_iota(jnp.int32, (n, n), 1)
    m = (c >= r) if upper else (c <= r)
    if block is not None:
        m = m & ((r // block) == (c // block))
    return m.astype(BF16)


def _tri_dot(tri, x):
    hi = x.astype(BF16)
    lo = (x - hi.astype(F32)).astype(BF16)
    return _dot(tri, hi) + _dot(tri, lo)


def _hgrn_gates(proj_ref, lb_ref, jh):
    proj_ref = _ColBlocks(proj_ref, 512)
    lb = lb_ref[:, 512 * jh:512 * (jh + 1)]
    qp = proj_ref[jh]
    fx = proj_ref[2 + jh]
    sq = _sigmoid_t(qp)
    sig = _sigmoid_t(fx)
    f = lb + (1.0 - lb) * sig
    k = (1.0 - lb) * (1.0 - sig)
    return lb, qp, sq, sig, f, k


def _hgrn_fwd(proj_qf, proj_ig, lb, out_norm, name, *, tb=128, exchange=None):
    T = proj_qf.shape[0]
    tb = min(tb, T)
    nc = tb // HG_CHUNK
    lmat = _tri(tb, block=HG_CHUNK)

    def body(proj_ref, ig_ref, lb_ref, on_ref, l_ref, o_ref, og_ref, st_ref, s_scr, b_scr):
        @pl.when(pl.program_id(0) == 0)
        def _():
            s_scr[...] = jnp.zeros_like(s_scr)

        r_i = lax.broadcasted_iota(jnp.int32, (HG_CHUNK, HG_CHUNK), 0)
        c_i = lax.broadcasted_iota(jnp.int32, (HG_CHUNK, HG_CHUNK), 1)
        causal = c_i <= r_i
        onv = on_ref[...]
        blocks = _ColBlocks(ig_ref, 512)
        for jh in range(2):
            lbv, qp, sq, sig, f, k = _hgrn_gates(proj_ref, lb_ref, jh)
            q = qp * sq
            b_half = b_scr.at[jh]
            b_half[...] = _tri_dot(l_ref[...], jnp.log(f))
            v = blocks[jh]
            gp = blocks[2 + jh]
            gs = gp * _sigmoid_t(gp)
            for hh in range(4):
                hd = 4 * jh + hh
                cs = slice(HG_HEAD * hh, HG_HEAD * (hh + 1))
                for ci in range(nc):
                    r0 = HG_CHUNK * ci
                    rs = slice(r0, r0 + HG_CHUNK)
                    bc = b_half[rs, cs]
                    bm = b_half[r0 + HG_CHUNK // 2 - 1:r0 + HG_CHUNK // 2, cs]
                    bl = b_half[r0 + HG_CHUNK - 1:r0 + HG_CHUNK, cs]
                    qc, kc, vc = q[rs, cs], k[rs, cs], v[rs, cs].astype(BF16)
                    qe = (qc * jnp.exp(bc)).astype(BF16)
                    qt = (qc * jnp.exp(bc - bm)).astype(BF16)
                    kt = (kc * jnp.exp(bm - bc)).astype(BF16)
                    kd = (kc * jnp.exp(bl - bc)).astype(BF16)
                    st = s_scr[hd]
                    stb = st.astype(BF16)
                    st_ref[ci, hd] = stb
                    a = jnp.where(causal, _dot(qt, kt, NT), 0.0).astype(BF16)
                    o = _dot(qe, stb, NT) + _dot(a, vc)
                    s_scr[hd] = st * jnp.exp(bl) + _dot(vc, kd, TN)
                    o_ref[rs, HG_HEAD * hd:HG_HEAD * (hd + 1)] = o
                    r = lax.rsqrt(_rowmean(o * o) + EPS)
                    og_ref[rs, HG_HEAD * hd:HG_HEAD * (hd + 1)] = (o * r * onv * gs[rs, cs]).astype(BF16)

    body, x_args, x_in, x_out, x_scratch = _host(exchange, 5, 3, body, _first_last(T // tb))
    res = pl.pallas_call(
        body, name=name, grid=(T // tb,),
        out_shape=[jax.ShapeDtypeStruct((T, D_MODEL), F32), jax.ShapeDtypeStruct((T, D_MODEL), BF16),
                   jax.ShapeDtypeStruct((T // HG_CHUNK, HG_HEADS, HG_HEAD, HG_HEAD), BF16)] + x_out,
        in_specs=[pl.BlockSpec((tb, 2 * D_MODEL), lambda i: (i, 0)),
                  pl.BlockSpec((tb, 2 * D_MODEL), lambda i: (i, 0)),
                  pl.BlockSpec((1, D_MODEL), lambda i: (0, 0)),
                  pl.BlockSpec((1, HG_HEAD), lambda i: (0, 0)),
                  pl.BlockSpec((tb, tb), lambda i: (0, 0))] + x_in,
        out_specs=[pl.BlockSpec((tb, D_MODEL), lambda i: (i, 0)),
                   pl.BlockSpec((tb, D_MODEL), lambda i: (i, 0)),
                   pl.BlockSpec((nc, HG_HEADS, HG_HEAD, HG_HEAD), lambda i: (i, 0, 0, 0))] + x_in,
        scratch_shapes=[pltpu.VMEM((HG_HEADS, HG_HEAD, HG_HEAD), F32), pltpu.VMEM((2, tb, 512), F32)] + x_scratch,
        compiler_params=_params("arbitrary"),
    )(proj_qf, proj_ig, lb, out_norm, lmat, *x_args)
    return res[0], res[1], res[2], res[3:]


def _hgrn_bwd(proj_qf, proj_ig, o, dog, states, lb, out_norm, name, *, tb=128):
    T = proj_qf.shape[0]
    tb = min(tb, T)
    nc = tb // HG_CHUNK
    nb = T // tb
    lmat = _tri(tb, block=HG_CHUNK)
    umat = _tri(tb, upper=True, block=HG_CHUNK)

    def body(proj_ref, ig_ref, o_ref, dog_ref, st_ref, lb_ref, on_ref, l_ref, u_ref,
             dproj_ref, dlb_ref, don_ref, ds_scr, *half_scr):
        @pl.when(pl.program_id(0) == 0)
        def _():
            ds_scr[...] = jnp.zeros_like(ds_scr)
            dlb_ref[...] = jnp.zeros_like(dlb_ref)
            don_ref[...] = jnp.zeros_like(don_ref)

        r_i = lax.broadcasted_iota(jnp.int32, (HG_CHUNK, HG_CHUNK), 0)
        c_i = lax.broadcasted_iota(jnp.int32, (HG_CHUNK, HG_CHUNK), 1)
        causal = c_i <= r_i
        causal_t = r_i <= c_i
        last_row = lax.broadcasted_iota(jnp.int32, (HG_CHUNK, HG_HEAD), 0) == HG_CHUNK - 1
        onv = on_ref[...]
        don_acc = jnp.zeros((1, HG_HEAD), F32)
        blocks = _ColBlocks(ig_ref, 512)
        dproj_ref = _ColBlocks(dproj_ref, 512)
        for jh in range(2):
            b_scr, dq_scr, dk_scr, dv_scr, dg_scr, db_scr = [s.at[jh] for s in half_scr]
            lbv, qp, sq, sig, f, k = _hgrn_gates(proj_ref, lb_ref, jh)
            q = qp * sq
            b_scr[...] = _tri_dot(l_ref[...], jnp.log(f))
            v = blocks[jh]
            gp = blocks[2 + jh]
            sg = _sigmoid_t(gp)
            for ci in reversed(range(nc)):
                r0 = HG_CHUNK * ci
                rs = slice(r0, r0 + HG_CHUNK)
                for hh in range(4):
                    hd = 4 * jh + hh
                    cs = slice(HG_HEAD * hh, HG_HEAD * (hh + 1))
                    hs = slice(HG_HEAD * hd, HG_HEAD * (hd + 1))
                    oc = o_ref[rs, hs]
                    r = lax.rsqrt(_rowmean(oc * oc) + EPS)
                    oh = oc * r
                    gc, sgc = gp[rs, cs], sg[rs, cs]
                    dogc = dog_ref[rs, hs].astype(F32)
                    don = dogc * (gc * sgc)
                    dg_scr[rs, cs] = dogc * (oh * onv) * (sgc * (1.0 + gc * (1.0 - sgc)))
                    don_acc += _colsum(don * oh)
                    donh = don * onv
                    do = (r * (donh - oh * _rowmean(donh * oh))).astype(BF16)
                    bc = b_scr[rs, cs]
                    bm = b_scr[r0 + HG_CHUNK // 2 - 1:r0 + HG_CHUNK // 2, cs]
                    bl = b_scr[r0 + HG_CHUNK - 1:r0 + HG_CHUNK, cs]
                    qc, kc, vc = q[rs, cs], k[rs, cs], v[rs, cs].astype(BF16)
                    e_b, e_q, e_k, e_d = jnp.exp(bc), jnp.exp(bc - bm), jnp.exp(bm - bc), jnp.exp(bl - bc)
                    qe = (qc * e_b).astype(BF16)
                    qt = (qc * e_q).astype(BF16)
                    kt = (kc * e_k).astype(BF16)
                    kd = (kc * e_d).astype(BF16)
                    stb = st_ref[ci, hd]
                    dst = ds_scr[hd]
                    dstb = dst.astype(BF16)
                    a_t = jnp.where(causal_t, _dot(kt, qt, NT), 0.0).astype(BF16)
                    da = jnp.where(causal, _dot(do, vc, NT), 0.0).astype(BF16)
                    da_t = jnp.where(causal_t, _dot(vc, do, NT), 0.0).astype(BF16)
                    dv_scr[rs, cs] = _dot(a_t, do) + _dot(kd, dstb, NT)
                    dqe, dqt = _dot(do, stb), _dot(da, kt)
                    dkt, dkd = _dot(da_t, qt), _dot(vc, dstb)
                    dq_scr[rs, cs] = dqe * e_b + dqt * e_q
                    dk_scr[rs, cs] = dkt * e_k + dkd * e_d
                    e_l = jnp.exp(bl)
                    s_end = stb.astype(F32) * e_l + _dot(vc, kd, TN)
                    dbc = (qe.astype(F32) * dqe + qt.astype(F32) * dqt
                           - kt.astype(F32) * dkt - kd.astype(F32) * dkd)
                    db_scr[rs, cs] = dbc + jnp.where(last_row, _colsum(dstb.astype(F32) * s_end), 0.0)
                    ds_scr[hd] = dst * e_l + _dot(do, qe, TN)
            dq = dq_scr[...]
            dk = dk_scr[...]
            cols = slice(512 * jh, 512 * (jh + 1))
            dlogf = _tri_dot(u_ref[...], db_scr[...])
            one_m_sig = 1.0 - sig
            dsig = (1.0 - lbv) * sig * one_m_sig
            dboth = dlogf / f - dk
            dproj_ref[jh] = (dq * (sq * (1.0 + qp * (1.0 - sq)))).astype(BF16)
            dproj_ref[2 + jh] = (dboth * dsig).astype(BF16)
            dproj_ref[4 + jh] = dv_scr[...].astype(BF16)
            dproj_ref[6 + jh] = dg_scr[...].astype(BF16)
            dlb_ref[:, cols] += _colsum(dboth * one_m_sig)
        don_ref[...] += don_acc

    rev = lambda i: nb - 1 - i
    return pl.pallas_call(
        body, name=name, grid=(nb,),
        out_shape=(jax.ShapeDtypeStruct((T, 4 * D_MODEL), BF16), jax.ShapeDtypeStruct((1, D_MODEL), F32),
                   jax.ShapeDtypeStruct((1, HG_HEAD), F32)),
        in_specs=[pl.BlockSpec((tb, 2 * D_MODEL), lambda i: (rev(i), 0)),
                  pl.BlockSpec((tb, 2 * D_MODEL), lambda i: (rev(i), 0)),
                  pl.BlockSpec((tb, D_MODEL), lambda i: (rev(i), 0)),
                  pl.BlockSpec((tb, D_MODEL), lambda i: (rev(i), 0)),
                  pl.BlockSpec((nc, HG_HEADS, HG_HEAD, HG_HEAD), lambda i: (rev(i), 0, 0, 0)),
                  pl.BlockSpec((1, D_MODEL), lambda i: (0, 0)),
                  pl.BlockSpec((1, HG_HEAD), lambda i: (0, 0)),
                  pl.BlockSpec((tb, tb), lambda i: (0, 0)),
                  pl.BlockSpec((tb, tb), lambda i: (0, 0))],
        out_specs=(pl.BlockSpec((tb, 4 * D_MODEL), lambda i: (rev(i), 0)),
                   pl.BlockSpec((1, D_MODEL), lambda i: (0, 0)),
                   pl.BlockSpec((1, HG_HEAD), lambda i: (0, 0))),
        scratch_shapes=[pltpu.VMEM((HG_HEADS, HG_HEAD, HG_HEAD), F32)] + [pltpu.VMEM((2, tb, 512), F32)] * 6,
        compiler_params=_params("arbitrary"),
    )(proj_qf, proj_ig, o, dog, states, lb, out_norm, lmat, umat)


def _gm_norm(pre_ref, lg_ref, lbias_ref):
    pre_ref = _ColBlocks(pre_ref, 768)
    vs = [pre_ref[4 + j].astype(F32) for j in range(4)]
    width = 4 * vs[0].shape[1]
    mu = sum(jnp.sum(v, axis=1, keepdims=True) for v in vs) / width
    ds = [v - mu for v in vs]
    var = sum(jnp.sum(d * d, axis=1, keepdims=True) for d in ds) / width
    rstd = lax.rsqrt(var + EPS)
    vhat = [d * rstd for d in ds]
    vn = [vhat[j] * lg_ref[j:j + 1, :] + lbias_ref[j:j + 1, :] for j in range(4)]
    return vhat, vn, rstd


def _gm_spatial_fwd(pre, ln_g, ln_b, ws, bsb, name, *, tb=256):
    T = pre.shape[0]
    tb = min(tb, T)
    nc = tb // GM_CHUNK

    def body(pre_ref, lg_ref, lbias_ref, ws_ref, bs_ref, o_ref):
        _, vn, _ = _gm_norm(pre_ref, lg_ref, lbias_ref)
        pre_ref, o_ref = _ColBlocks(pre_ref, 768), _ColBlocks(o_ref, 768)
        for j in range(4):
            u = pre_ref[j].astype(F32)
            for e in range(2):
                g = 2 * j + e
                cs = slice(GM_GDIM * e, GM_GDIM * (e + 1))
                wg = ws_ref[g].astype(BF16)
                for ci in range(nc):
                    rs = slice(GM_CHUNK * ci, GM_CHUNK * (ci + 1))
                    vm = _dot(wg, vn[j][rs, cs].astype(BF16)) + bs_ref[g]
                    o_ref[j, rs, cs] = (u[rs, cs] * vm).astype(BF16)

    return pl.pallas_call(
        body, name=name, grid=(T // tb,),
        out_shape=jax.ShapeDtypeStruct((T, 4 * 768), BF16),
        in_specs=[pl.BlockSpec((tb, 8 * 768), lambda i: (i, 0)),
                  pl.BlockSpec((4, 768), lambda i: (0, 0)),
                  pl.BlockSpec((4, 768), lambda i: (0, 0)),
                  pl.BlockSpec((GM_GROUPS, GM_CHUNK, GM_CHUNK), lambda i: (0, 0, 0)),
                  pl.BlockSpec((GM_GROUPS, GM_CHUNK, GM_GDIM), lambda i: (0, 0, 0))],
        out_specs=pl.BlockSpec((tb, 4 * 768), lambda i: (i, 0)),
        compiler_params=_params("parallel"),
    )(pre, ln_g, ln_b, ws, bsb)


def _gm_spatial_bwd(pre, gp, dm, ln_g, ln_b, ws, ws_t, bsb, name, *, tb=256):
    T = pre.shape[0]
    tb = min(tb, T)
    nc = tb // GM_CHUNK
    nb = T // tb

    def body(pre_ref, gp_ref, dm_ref, lg_ref, lbias_ref, ws_ref, wst_ref, bs_ref,
             dpre_ref, dws_ref, dbs_ref, dlg_ref, dlb_ref, dbin_ref, dbs_scr, dvn_scr, du_scr):
        i = pl.program_id(0)

        @pl.when(i == 0)
        def _():
            dws_ref[...] = jnp.zeros_like(dws_ref)
            dbs_scr[...] = jnp.zeros_like(dbs_scr)
            dlg_ref[...] = jnp.zeros_like(dlg_ref)
            dlb_ref[...] = jnp.zeros_like(dlb_ref)
            dbin_ref[...] = jnp.zeros_like(dbin_ref)

        vhat, vn, rstd = _gm_norm(pre_ref, lg_ref, lbias_ref)
        pre_ref, gp_ref, dm_ref = _ColBlocks(pre_ref, 768), _ColBlocks(gp_ref, 768), _ColBlocks(dm_ref, 768)
        dpre_ref = _ColBlocks(dpre_ref, 768)
        for j in range(4):
            u = pre_ref[j].astype(F32)
            for e in range(2):
                g = 2 * j + e
                cs = slice(GM_GDIM * e, GM_GDIM * (e + 1))
                wg = ws_ref[g].astype(BF16)
                wgt = wst_ref[g].astype(BF16)
                for ci in range(nc):
                    rs = slice(GM_CHUNK * ci, GM_CHUNK * (ci + 1))
                    vnb = vn[j][rs, cs].astype(BF16)
                    vm = _dot(wg, vnb) + bs_ref[g]
                    dmg = dm_ref[j, rs, cs].astype(F32)
                    du_scr[j, rs, cs] = dmg * vm
                    dvm = dmg * u[rs, cs]
                    dvmb = dvm.astype(BF16)
                    dws_ref[g] += _dot(dvmb, vnb, NT)
                    dbs_scr[g] += dvm
                    dvn_scr[j, rs, cs] = _dot(wgt, dvmb)
        width = 4 * 768
        dvh = []
        for j in range(4):
            dvn = dvn_scr[j]
            dlg_ref[j:j + 1, :] += _colsum(dvn * vhat[j])
            dlb_ref[j:j + 1, :] += _colsum(dvn)
            dvh.append(dvn * lg_ref[j:j + 1, :])
        m1 = sum(jnp.sum(d, axis=1, keepdims=True) for d in dvh) / width
        m2 = sum(jnp.sum(dvh[j] * vhat[j], axis=1, keepdims=True) for j in range(4)) / width
        for j in range(4):
            dv = rstd * (dvh[j] - m1 - vhat[j] * m2)
            dpv = dv * gp_ref[4 + j].astype(F32)
            dpu = du_scr[j] * gp_ref[j].astype(F32)
            dpre_ref[4 + j] = dpv.astype(BF16)
            dpre_ref[j] = dpu.astype(BF16)
            dbin_ref[4 + j:5 + j, :] += _colsum(dpv)
            dbin_ref[j:j + 1, :] += _colsum(dpu)

        @pl.when(i == nb - 1)
        def _():
            r_i = lax.broadcasted_iota(jnp.int32, (GM_CHUNK, GM_CHUNK), 0)
            c_i = lax.broadcasted_iota(jnp.int32, (GM_CHUNK, GM_CHUNK), 1)
            for g in range(GM_GROUPS):
                dws_ref[g] = jnp.where(c_i <= r_i, dws_ref[g], 0.0)
                dbs_ref[g] = jnp.broadcast_to(jnp.sum(dbs_scr[g], axis=1, keepdims=True), (GM_CHUNK, GM_CHUNK))

    sq = pl.BlockSpec((GM_GROUPS, GM_CHUNK, GM_CHUNK), lambda i: (0, 0, 0))
    v4 = pl.BlockSpec((4, 768), lambda i: (0, 0))
    return pl.pallas_call(
        body, name=name, grid=(nb,),
        out_shape=(jax.ShapeDtypeStruct((T, 8 * 768), BF16),
                   jax.ShapeDtypeStruct((GM_GROUPS, GM_CHUNK, GM_CHUNK), F32),
                   jax.ShapeDtypeStruct((GM_GROUPS, GM_CHUNK, GM_CHUNK), F32),
                   jax.ShapeDtypeStruct((4, 768), F32), jax.ShapeDtypeStruct((4, 768), F32),
                   jax.ShapeDtypeStruct((8, 768), F32)),
        in_specs=[pl.BlockSpec((tb, 8 * 768), lambda i: (i, 0)),
                  pl.BlockSpec((tb, 8 * 768), lambda i: (i, 0)),
                  pl.BlockSpec((tb, 4 * 768), lambda i: (i, 0)),
                  v4, v4, sq, sq,
                  pl.BlockSpec((GM_GROUPS, GM_CHUNK, GM_GDIM), lambda i: (0, 0, 0))],
        out_specs=(pl.BlockSpec((tb, 8 * 768), lambda i: (i, 0)), sq, sq, v4, v4,
                   pl.BlockSpec((8, 768), lambda i: (0, 0))),
        scratch_shapes=[pltpu.VMEM((GM_GROUPS, GM_CHUNK, GM_GDIM), F32),
                        pltpu.VMEM((4, tb, 768), F32), pltpu.VMEM((4, tb, 768), F32)],
        compiler_params=_params("arbitrary"),
    )(pre, gp, dm, ln_g, ln_b, ws, ws_t, bsb)


def _adamw(slots, w, m, v, name, *, tr=256):
    S, R, C = slots.shape
    tr = next((t for t in (tr, tr // 2, tr // 4, tr // 8, tr // 16) if R % t == 0), R) if R > tr else R
    bc1 = 1.0 - ADAM_B1 ** ADAM_STEP
    bc2 = 1.0 - ADAM_B2 ** ADAM_STEP

    def body(s_ref, w_ref, m_ref, v_ref, g_ref, d_ref, nm_ref, nv_ref):
        g = s_ref[0].astype(F32)
        for s in range(1, S):
            g = g + s_ref[s].astype(F32)
        mn = ADAM_B1 * m_ref[...] + (1.0 - ADAM_B1) * g
        vn = ADAM_B2 * v_ref[...] + (1.0 - ADAM_B2) * (g * g)
        g_ref[...] = g
        nm_ref[...] = mn
        nv_ref[...] = vn
        d_ref[...] = -ADAM_LR * ((mn / bc1) / (jnp.sqrt(vn / bc2) + ADAM_EPS) + ADAM_WD * w_ref[...])

    spec = pl.BlockSpec((tr, C), lambda i: (i, 0))
    return pl.pallas_call(
        body, name=name, grid=(R // tr,),
        out_shape=(jax.ShapeDtypeStruct((R, C), F32),) * 4,
        in_specs=[pl.BlockSpec((S, tr, C), lambda i: (0, i, 0)), spec, spec, spec],
        out_specs=(spec,) * 4,
        compiler_params=_params("parallel"),
    )(slots, w, m, v)


def _update(slots, w, m, v, name):
    shp = w.shape
    C = shp[-1]
    R = math.prod(shp[:-1])
    outs = _adamw(slots.reshape(slots.shape[0], R, C), w.reshape(R, C), m.reshape(R, C), v.reshape(R, C), name)
    return tuple(o.reshape(shp) for o in outs)


def kernel(x, c, ada_w, ada_b, norm_pre, norm_post, ffn_w_in, ffn_w_out, hg_w_in, hg_w_out, hg_out_norm, hg_lb, gm_w_in, gm_b_in, gm_ln_g, gm_ln_b, gm_w_s, gm_b_s, gm_w_out, loss_target, m_ada_w, m_ada_b, m_norm_pre, m_norm_post, m_ffn_w_in, m_ffn_w_out, m_hg_w_in, m_hg_w_out, m_hg_out_norm, m_hg_lb, m_gm_w_in, m_gm_b_in, m_gm_ln_g, m_gm_ln_b, m_gm_w_s, m_gm_b_s, m_gm_w_out, v_ada_w, v_ada_b, v_norm_pre, v_norm_post, v_ffn_w_in, v_ffn_w_out, v_hg_w_in, v_hg_w_out, v_hg_out_norm, v_hg_lb, v_gm_w_in, v_gm_b_in, v_gm_ln_g, v_gm_ln_b, v_gm_w_s, v_gm_b_s, v_gm_w_out):
    me = 4 * lax.axis_index("x") + 2 * lax.axis_index("y") + lax.axis_index("c")
    T = x.shape[1]
    x0 = x.reshape(T, D_MODEL)
    target = loss_target.reshape(T, D_MODEL)
    n_ada = ada_w.shape[-1]

    pack = jnp.concatenate([
        c.reshape(8, 128), norm_pre.reshape(6, 128), norm_post.reshape(6, 128),
        gm_b_in.reshape(6, 128), gm_ln_g.reshape(3, 128), gm_ln_b.reshape(3, 128)], axis=0)
    packs = _all_gather(pack, "gather_small")
    c_all = packs[:, 0:8].reshape(NDEV, D_MODEL)
    npre = packs[:, 8:14].reshape(NDEV, 2, 3, 128).transpose(1, 2, 0, 3).reshape(2, 3, D_MODEL)
    npost = packs[:, 14:20].reshape(NDEV, 2, 3, 128).transpose(1, 2, 0, 3).reshape(2, 3, D_MODEL)
    b_in = packs[:, 20:26].reshape(1, NDEV * 768)
    ln_g = packs[:, 26:29].reshape(4, 768)
    ln_b = packs[:, 29:32].reshape(4, 768)

    ada_b_mine = lax.dynamic_slice_in_dim(ada_b, me * n_ada, n_ada, axis=1).reshape(2, 1, n_ada)
    mod_cols = _ada_fwd(c_all, ada_w, ada_b_mine, "ada_fwd")
    mod_all = _all_gather(mod_cols, "gather_mod")
    mod = lax.dynamic_index_in_dim(mod_all, me, axis=2, keepdims=False)
    mod = mod.transpose(1, 0, 2).reshape(2, 9, 1, D_MODEL)

    sh_fi, sh_fo = ffn_w_in.astype(BF16).swapaxes(-1, -2), ffn_w_out.astype(BF16)
    sh_hi, sh_ho = hg_w_in[0].astype(BF16).T, hg_w_out[0].astype(BF16)
    sh_mi, sh_mo = gm_w_in[0].astype(BF16).T, gm_w_out[0].astype(BF16)

    def whole(gathered):
        return gathered.reshape(-1, D_MODEL)

    w_fi = {(0, 0): whole(_all_gather(sh_fi[0, 0], "gather_ffn_in_first"))}
    w_fo = {}
    riders = {"l0s0": [sh_fo[0, 0], sh_hi, sh_ho], "l0s1": [sh_fi[0, 1], sh_fo[0, 1]], "hg_mix": [sh_mi, sh_mo],
              "l0s2": [sh_fi[1, 0], sh_fo[1, 0]], "l1s0": [sh_fi[1, 1], sh_fo[1, 1]]}

    sm = jax.nn.softmax(hg_lb, axis=0)
    lb0 = sm[0:1]
    on = hg_out_norm.reshape(1, HG_HEAD)
    tril = jnp.tril(jnp.ones((GM_CHUNK, GM_CHUNK), F32))
    ws = gm_w_s[0] * tril[None]
    ws_t = ws.transpose(0, 2, 1)
    bsb = jnp.broadcast_to(gm_b_s[0][:, :, None], (GM_GROUPS, GM_CHUNK, GM_GDIM))

    res_ws = (0.5, 1.0, 0.5)

    def vecs(i, s):
        return (npre[i, s].reshape(1, D_MODEL), npost[i, s].reshape(1, D_MODEL),
                mod[i, 3 * s], mod[i, 3 * s + 1], mod[i, 3 * s + 2])

    order = [(i, s) for i in range(2) for s in range(3)]
    saved = {}
    xs = x0
    pre_g, _, shift, scale, _ = vecs(0, 0)
    h = _prenorm_fwd(xs, pre_g, scale, shift, "prenorm_l0s0")
    for pos, (i, s) in enumerate(order):
        tag = f"l{i}s{s}"
        _, post_g, _, _, gate = vecs(i, s)
        rider = _Exchange("gather", riders[tag]) if tag in riders else None
        if s != 1:
            p, q, a, got = _ffn_in(h, w_fi[i, s // 2], "ffn_in_" + tag, exchange=rider)
            extra = (p, q, a)
            if tag == "l0s0":
                w_fo[0, 0], w_hi, w_ho = map(whole, got)
            elif tag == "l0s2":
                w_fi[1, 0], w_fo[1, 0] = map(whole, got)
            elif tag == "l1s0":
                w_fi[1, 1], w_fo[1, 1] = map(whole, got)
            wo = w_fo[i, s // 2]
        elif i == 0:
            p_qf, p_ig, *got = _mm_blocks(h, w_hi, "hg_in", out_dtype=F32, chunk=1024, halves=True, exchange=rider)
            w_fi[0, 1], w_fo[0, 1] = map(whole, got)
            o, og, states, got = _hgrn_fwd(p_qf, p_ig, lb0, on, "hg_mix",
                                           exchange=_Exchange("gather", riders["hg_mix"]))
            w_mi, w_mo = map(whole, got)
            a, wo = og, w_ho
            extra = (p_qf, p_ig, o, og, states)
        else:
            pre, gp = _mm_blocks(h, w_mi, "gm_in", bias=b_in, gelu=True)
            a = _gm_spatial_fwd(pre, ln_g, ln_b, ws, bsb, "gm_mix")
            wo = w_mo
            extra = (pre, gp, a)
        if pos + 1 < len(order):
            npre_g, _, nshift, nscale, _ = vecs(*order[pos + 1])
            y, x_next, h_next = _out_proj(a, wo, xs, post_g, gate, res_ws[s], (npre_g, nscale, nshift), "out_" + tag)
            saved[tag] = (xs, h, y) + extra
            xs, h = x_next, h_next
        else:
            dx, dy, dgate, dpost, loss_part = _out_proj_last(a, wo, xs, post_g, gate, res_ws[s], target, "out_" + tag)
            saved[tag] = (xs, h, None) + extra
    loss = lax.psum(loss_part[0, 0], ("x", "y", "c"))

    slots = {}
    d_npre = [[None] * 3, [None] * 3]
    d_npost = [[None] * 3, [None] * 3]
    d_mod = [[None] * 9, [None] * 9]
    for pos in reversed(range(len(order))):
        i, s = order[pos]
        tag = f"l{i}s{s}"
        pre_g, _, _, scale, _ = vecs(i, s)
        xin, h = saved[tag][:2]
        if s != 1:
            w_in, wo = w_fi[i, s // 2], w_fo[i, s // 2]
            p, q, g = saved[tag][3:]
            dz = _ffn_dgate(dy, wo, p, q, "ffn_dgate_" + tag)
            g_out = _mm_wgrad(g, dy, "ffn_out_wgrad_" + tag, xw=1408)
            g_in = jnp.concatenate([_mm_wgrad(dz[0], h, "ffn_in_wgrad_a_" + tag, xw=1408),
                                    _mm_wgrad(dz[1], h, "ffn_in_wgrad_b_" + tag, xw=1408)])
        elif i == 0:
            p_qf, p_ig, o, og, states = saved[tag][3:]
            dog = _mm_blocks(dy, w_ho, "hg_out_dgrad")
            g_out = _mm_wgrad(og, dy, "hg_out_wgrad")
            dz, d_lb0, d_on = _hgrn_bwd(p_qf, p_ig, o, dog, states, lb0, on, "hg_mix_bwd")
            w_in = w_hi
            g_in = _mm_wgrad(h, dz, "hg_in_wgrad", yw=512)
        else:
            pre, gp, sp = saved[tag][3:]
            dm = _mm_blocks(dy, w_mo, "gm_out_dgrad")
            g_out = _mm_wgrad(sp, dy, "gm_out_wgrad", xw=768)
            dz, d_ws, d_bs, d_lg, d_lbias, d_bin = _gm_spatial_bwd(pre, gp, dm, ln_g, ln_b, ws, ws_t, bsb, "gm_mix_bwd")
            w_in = w_mi
            g_in = _mm_wgrad(h, dz, "gm_in_wgrad", yw=768)
        g_out = g_out.reshape(NDEV, -1, D_MODEL)
        g_in = g_in.reshape(NDEV, -1, D_MODEL) if s != 1 else g_in
        d_npost[i][s] = dpost
        d_mod[i][3 * s + 2] = dgate
        rider = _Exchange("scatter", [g_in, g_out])
        if pos > 0:
            pi, ps = order[pos - 1]
            _, ppost_g, _, _, pgate = vecs(pi, ps)
            prev = (saved[f"l{pi}s{ps}"][2], ppost_g, pgate, res_ws[ps])
            dx, dshift, dscale, dpre_g, dy, dgate, dpost, r_in, r_out = _in_grad(
                dz, w_in, dx, xin, pre_g, scale, prev, "in_grad_" + tag, exchange=rider)
        else:
            dx, dshift, dscale, dpre_g, r_in, r_out = _in_grad(
                dz, w_in, dx, xin, pre_g, scale, None, "in_grad_" + tag, exchange=rider)
        slots[tag] = (r_in, r_out)
        d_npre[i][s] = dpre_g
        d_mod[i][3 * s], d_mod[i][3 * s + 1] = dshift, dscale
    grad_x = dx.reshape(x.shape)

    ffn_tags = ["l0s0", "l0s2", "l1s0", "l1s2"]
    s_fi = jnp.stack([slots[t][0] for t in ffn_tags], axis=1).swapaxes(-1, -2)
    s_fo = jnp.stack([slots[t][1] for t in ffn_tags], axis=1)
    (s_hi, s_ho), (s_mi, s_mo) = slots["l0s1"], slots["l1s1"]
    s_hi, s_ho, s_mi, s_mo = s_hi[:, None], s_ho[:, None], s_mi[:, None], s_mo[:, None]

    gmod = jnp.stack([jnp.concatenate(d_mod[i], axis=0) for i in range(2)])
    d_sm = lb0 * d_lb0
    d_hg_lb = jnp.concatenate([d_sm, jnp.zeros((2, D_MODEL), F32)], axis=0) - sm * d_sm
    small = [gmod, jnp.stack([jnp.concatenate(r, axis=0) for r in d_npre]),
             jnp.stack([jnp.concatenate(r, axis=0) for r in d_npost]),
             d_on, d_hg_lb, d_bin, d_lg, d_lbias, d_ws, d_bs[:, :, 0]]
    sizes = [a.size for a in small]
    flat = jnp.concatenate([a.reshape(-1) for a in small])
    rows = -(-flat.size // (8 * 128)) * 8
    flat = jnp.pad(flat, (0, rows * 128 - flat.size)).reshape(rows, 128)
    flats = _all_gather(flat, "gather_small_grads").reshape(NDEV, rows * 128)
    parts, off = [], 0
    for a, n in zip(small, sizes):
        parts.append(flats[:, off:off + n].reshape((NDEV,) + a.shape))
        off += n
    p_mod, p_npre, p_npost, p_on, p_lb, p_bin, p_lg, p_lbias, p_ws, p_bs = parts

    def mine(p, width):
        return lax.dynamic_slice_in_dim(p, me * width, width, axis=p.ndim - 1)

    gmod_cols = mine(p_mod.reshape(NDEV, 2, 9 * D_MODEL), n_ada).transpose(1, 0, 2)
    g_ada_w = _ada_bwd(jnp.pad(c_all.T, ((0, 0), (0, 120))), jnp.pad(gmod_cols, ((0, 0), (0, 120), (0, 0))), "ada_bwd")

    out = {}
    out["ada_w"] = _update(g_ada_w[None], ada_w, m_ada_w, v_ada_w, "adamw_ada_w")
    out["ada_b"] = _update(p_mod.reshape(NDEV, 2, 9 * D_MODEL), ada_b, m_ada_b, v_ada_b, "adamw_ada_b")
    out["norm_pre"] = _update(mine(p_npre, 128), norm_pre, m_norm_pre, v_norm_pre, "adamw_norm_pre")
    out["norm_post"] = _update(mine(p_npost, 128), norm_post, m_norm_post, v_norm_post, "adamw_norm_post")
    out["ffn_w_in"] = _update(s_fi.reshape((NDEV,) + ffn_w_in.shape), ffn_w_in, m_ffn_w_in, v_ffn_w_in, "adamw_ffn_in")
    out["ffn_w_out"] = _update(s_fo.reshape((NDEV,) + ffn_w_out.shape), ffn_w_out, m_ffn_w_out, v_ffn_w_out, "adamw_ffn_out")
    out["hg_w_in"] = _update(s_hi, hg_w_in, m_hg_w_in, v_hg_w_in, "adamw_hg_in")
    out["hg_w_out"] = _update(s_ho, hg_w_out, m_hg_w_out, v_hg_w_out, "adamw_hg_out")
    out["hg_out_norm"] = _update(p_on, hg_out_norm, m_hg_out_norm, v_hg_out_norm, "adamw_hg_norm")
    out["hg_lb"] = _update(p_lb, hg_lb, m_hg_lb, v_hg_lb, "adamw_hg_lb")
    out["gm_w_in"] = _update(s_mi, gm_w_in, m_gm_w_in, v_gm_w_in, "adamw_gm_in")
    out["gm_b_in"] = _update(mine(p_bin.reshape(NDEV, 1, 8 * 768), 768), gm_b_in, m_gm_b_in, v_gm_b_in, "adamw_gm_b_in")
    out["gm_ln_g"] = _update(mine(p_lg.reshape(NDEV, 1, 4 * 768), 384), gm_ln_g, m_gm_ln_g, v_gm_ln_g, "adamw_gm_ln_g")
    out["gm_ln_b"] = _update(mine(p_lbias.reshape(NDEV, 1, 4 * 768), 384), gm_ln_b, m_gm_ln_b, v_gm_ln_b, "adamw_gm_ln_b")
    out["gm_w_s"] = _update(p_ws[:, None], gm_w_s, m_gm_w_s, v_gm_w_s, "adamw_gm_w_s")
    out["gm_b_s"] = _update(p_bs[:, None], gm_b_s, m_gm_b_s, v_gm_b_s, "adamw_gm_b_s")
    out["gm_w_out"] = _update(s_mo, gm_w_out, m_gm_w_out, v_gm_w_out, "adamw_gm_out")

    names = ["ada_w", "ada_b", "norm_pre", "norm_post", "ffn_w_in", "ffn_w_out", "hg_w_in", "hg_w_out",
             "hg_out_norm", "hg_lb", "gm_w_in", "gm_b_in", "gm_ln_g", "gm_ln_b", "gm_w_s", "gm_b_s", "gm_w_out"]
    return (loss, grad_x, *[out[n][0] for n in names], *[out[n][1] for n in names],
            *[out[n][2] for n in names], *[out[n][3] for n in names])
```

```python
import math

import jax
import jax.numpy as jnp
from jax import lax
from jax.experimental import pallas as pl
from jax.experimental.pallas import tpu as pltpu

F32 = jnp.float32
BF16 = jnp.bfloat16
NDEV = 8
D_MODEL = 1024
EPS = 1e-6
HG_CHUNK = 64
HG_HEAD = 128
HG_HEADS = 8
GM_CHUNK = 128
GM_GDIM = 384
GM_GROUPS = 8
ADAM_LR = 0.001
ADAM_B1 = 0.9
ADAM_B2 = 0.999
ADAM_EPS = 1e-08
ADAM_WD = 0.01
ADAM_STEP = 10
VMEM_LIMIT = 56 * 2 ** 20

NN = (((1,), (0,)), ((), ()))
NT = (((1,), (1,)), ((), ()))
TN = (((0,), (0,)), ((), ()))
MESH = pl.DeviceIdType.MESH
ANY = pl.BlockSpec(memory_space=pl.ANY)


def _dot(a, b, dims=NN, precision=None):
    return lax.dot_general(a, b, dims, preferred_element_type=F32, precision=precision)


def _params(*sem):
    return pltpu.CompilerParams(dimension_semantics=sem, vmem_limit_bytes=VMEM_LIMIT)


def _sigmoid(x):
    return 1.0 / (1.0 + jnp.exp(-x))


def _sigmoid_t(x):
    return 0.5 * jnp.tanh(0.5 * x) + 0.5


def _gelu_and_grad(x):
    c = math.sqrt(2.0 / math.pi)
    m = (c * 0.044715) * (x * x)
    t = jnp.tanh(x * (c + m))
    hp = 0.5 + 0.5 * t
    return x * hp, hp * (1.0 + (x * (1.0 - t)) * (c + 3.0 * m))


def _colsum(x):
    return jnp.sum(x, axis=0, keepdims=True)


def _rowmean(x):
    return jnp.mean(x, axis=-1, keepdims=True)


def _all_gather(shard, name):
    def body(x_ref, out_ref, send_sems, recv_sems, local_sem):
        x, y, c = lax.axis_index("x"), lax.axis_index("y"), lax.axis_index("c")
        me, sibling = (x, y, c), (x, y, 1 - c)
        chips = [(1 - x, y), (x, 1 - y), (1 - x, 1 - y)]

        def slot(p):
            return out_ref.at[4 * p[0] + 2 * p[1] + p[2]]

        def copy(k, block, to, src=None):
            return pltpu.make_async_remote_copy(
                src_ref=slot(block) if src is None else src, dst_ref=slot(block),
                send_sem=send_sems.at[k], recv_sem=recv_sems.at[k],
                device_id=to, device_id_type=MESH)

        mine = pltpu.make_async_copy(x_ref, slot(me), local_sem)
        mine.start()
        first = [copy(0, me, sibling, src=x_ref)]
        first += [copy(1 + j, me, (*chip, c), src=x_ref) for j, chip in enumerate(chips)]
        for cp in first:
            cp.start()
        passed = [copy(4 + j, (*chip, c), sibling) for j, chip in enumerate(chips)]
        for j, chip in enumerate(chips):
            copy(1 + j, (*chip, c), me).wait_recv()
            passed[j].start()
        copy(0, sibling, me).wait_recv()
        for j, chip in enumerate(chips):
            copy(4 + j, (*chip, 1 - c), me).wait_recv()
        for cp in first + passed:
            cp.wait_send()
        mine.wait()

    return pl.pallas_call(
        body, name=name,
        out_shape=jax.ShapeDtypeStruct((NDEV,) + shard.shape, shard.dtype),
        in_specs=[ANY], out_specs=ANY,
        scratch_shapes=[pltpu.SemaphoreType.DMA((7,)), pltpu.SemaphoreType.DMA((7,)),
                        pltpu.SemaphoreType.DMA(())],
    )(shard)


class _Exchange:
    def __init__(self, kind, arrays):
        self.gather = kind == "gather"
        self.arrays = list(arrays)
        self.n = n = len(self.arrays)
        self.out_shape = [jax.ShapeDtypeStruct(((NDEV,) + a.shape) if self.gather else a.shape, a.dtype)
                          for a in self.arrays]
        self.scratch = [pltpu.SemaphoreType.DMA((n, NDEV - 1)), pltpu.SemaphoreType.DMA((n, NDEV - 1)),
                        pltpu.SemaphoreType.DMA((n,))]

    def _copies(self, in_refs, out_refs, sems):
        send_sems, recv_sems, local_sems = sems
        x, y, c = lax.axis_index("x"), lax.axis_index("y"), lax.axis_index("c")
        me = 4 * x + 2 * y + c
        peers = [(1 - x if k & 4 else x, 1 - y if k & 2 else y, 1 - c if k & 1 else c) for k in range(1, NDEV)]
        local, send, recv = [], [], []
        for a in range(self.n):
            src = (lambda pid, a=a: in_refs[a]) if self.gather else (lambda pid, a=a: in_refs[a].at[pid])
            local.append(pltpu.make_async_copy(src(me), out_refs[a].at[me], local_sems.at[a]))
            for k, p in enumerate(peers):
                pid = 4 * p[0] + 2 * p[1] + p[2]
                for lst, slot in ((send, me), (recv, pid)):
                    lst.append(pltpu.make_async_remote_copy(
                        src_ref=src(pid), dst_ref=out_refs[a].at[slot],
                        send_sem=send_sems.at[a, k], recv_sem=recv_sems.at[a, k],
                        device_id=p, device_id_type=MESH))
        return local, send, recv

    def start(self, first, in_refs, out_refs, sems):
        @pl.when(first)
        def _():
            local, send, _ = self._copies(in_refs, out_refs, sems)
            for cp in local + send:
                cp.start()

    def finish(self, last, in_refs, out_refs, sems):
        @pl.when(last)
        def _():
            local, send, recv = self._copies(in_refs, out_refs, sems)
            for cp in send:
                cp.wait_send()
            for cp in recv:
                cp.wait_recv()
            for cp in local:
                cp.wait()


def _host(exchange, n_in, n_out, body, first_last):
    if exchange is None:
        return body, [], [], [], []
    n = exchange.n

    def hosted(*refs):
        ins, refs = refs[:n_in], refs[n_in:]
        xin, refs = refs[:n], refs[n:]
        outs, refs = refs[:n_out], refs[n_out:]
        xout, refs = refs[:n], refs[n:]
        scratch, sems = refs[:len(refs) - 3], refs[len(refs) - 3:]
        first, last = first_last()
        exchange.start(first, xin, xout, sems)
        body(*ins, *outs, *scratch)
        exchange.finish(last, xin, xout, sems)

    return hosted, exchange.arrays, [ANY] * n, exchange.out_shape, exchange.scratch


def _first_last(steps):
    def at():
        i = pl.program_id(0)
        return i == 0, i == steps - 1
    return at


class _ColBlocks:
    def __init__(self, ref, width):
        self.ref, self.width = ref, width

    def _index(self, key):
        key = key if isinstance(key, tuple) else (key,)
        rows = key[1] if len(key) > 1 else slice(None)
        cols = key[2] if len(key) > 2 else slice(0, self.width)
        c0 = key[0] * self.width
        return rows, slice(c0 + cols.start, c0 + cols.stop)

    def __getitem__(self, key):
        return self.ref[self._index(key)]

    def __setitem__(self, key, value):
        self.ref[self._index(key)] = value


def _col_chunks(width, chunk=768):
    return [slice(c, min(c + chunk, width)) for c in range(0, width, chunk)]


def _row_spec(tm, d):
    return pl.BlockSpec((tm, d), lambda m: (m, 0))


def _vec_spec(d):
    return pl.BlockSpec((1, d), lambda m: (0, 0))


def _whole_spec(w):
    nd = w.ndim
    return pl.BlockSpec(w.shape, lambda m: (0,) * nd, pipeline_mode=pl.Buffered(1))


def _mm_blocks(a, w, name, *, bias=None, out_dtype=BF16, tm=512, chunk=768, gelu=False, exchange=None):
    T, K = a.shape
    C = w.shape[0]
    tm = min(tm, T)
    n_in = 2 + (bias is not None)

    def body(*refs):
        a_ref, w_ref = refs[:2]
        av = a_ref[...]
        for cols in _col_chunks(C, chunk):
            r = _dot(av, w_ref[cols], NT)
            if bias is not None:
                r = r + refs[2][:, cols]
            if gelu:
                z, dz = _gelu_and_grad(r)
                refs[n_in][:, cols] = z.astype(BF16)
                refs[n_in + 1][:, cols] = dz.astype(BF16)
            else:
                refs[n_in][:, cols] = r.astype(out_dtype)

    in_specs = [_row_spec(tm, K), _whole_spec(w)]
    args = [a, w]
    if bias is not None:
        in_specs.append(_whole_spec(bias))
        args.append(bias)
    outs = [jax.ShapeDtypeStruct((T, C), BF16)] * 2 if gelu else [jax.ShapeDtypeStruct((T, C), out_dtype)]
    body, x_args, x_in, x_out, x_scratch = _host(exchange, n_in, len(outs), body, _first_last(T // tm))
    res = pl.pallas_call(
        body, name=name, grid=(T // tm,),
        out_shape=outs + x_out,
        in_specs=in_specs + x_in,
        out_specs=[_row_spec(tm, C)] * len(outs) + x_in,
        scratch_shapes=x_scratch,
        compiler_params=_params("arbitrary" if exchange else "parallel"),
    )(*args, *x_args)
    return res if (exchange or gelu) else res[0]


def _rms(v):
    return lax.rsqrt(_rowmean(v * v) + EPS)


def _zero_at_start(*refs):
    @pl.when(pl.program_id(0) == 0)
    def _():
        for r in refs:
            r[...] = jnp.zeros_like(r)


def _postnorm_bwd_math(dxo, yv, g, gate, res_w, dgate_ref, dpost_ref):
    r = _rms(yv)
    yh = yv * r
    both = res_w * _colsum(dxo * yh)
    dgate_ref[...] += g * both
    dpost_ref[...] += gate * both
    dyh = dxo * (res_w * gate * g)
    return (r * (dyh - yh * _rowmean(dyh * yh))).astype(BF16)


def _out_proj(a, w, x, post_g, gate, res_w, nxt, name, *, tm=512):
    T, ka = a.shape
    d = w.shape[1]
    tm = min(tm, T)

    def body(a_ref, w_ref, x_ref, pg_ref, gate_ref, ng_ref, nsc_ref, nsh_ref, y_ref, xn_ref, h_ref):
        y = _dot(a_ref[...], w_ref[...])
        y_ref[...] = y
        xn = x_ref[...] + (y * _rms(y)) * (res_w * gate_ref[...] * pg_ref[...])
        xn_ref[...] = xn
        h_ref[...] = ((xn * _rms(xn)) * (ng_ref[...] * (1.0 + nsc_ref[...])) + nsh_ref[...]).astype(BF16)

    return pl.pallas_call(
        body, name=name, grid=(T // tm,),
        out_shape=(jax.ShapeDtypeStruct((T, d), F32), jax.ShapeDtypeStruct((T, d), F32),
                   jax.ShapeDtypeStruct((T, d), BF16)),
        in_specs=[_row_spec(tm, ka), _whole_spec(w), _row_spec(tm, d)] + [_vec_spec(d)] * 5,
        out_specs=(_row_spec(tm, d),) * 3,
        compiler_params=_params("parallel"),
    )(a, w, x, post_g, gate, *nxt)


def _out_proj_last(a, w, x, post_g, gate, res_w, target, name, *, tm=512):
    T, ka = a.shape
    d = w.shape[1]
    tm = min(tm, T)

    def body(a_ref, w_ref, x_ref, pg_ref, gate_ref, t_ref, dx_ref, dy_ref, dgate_ref, dpost_ref, l_ref):
        _zero_at_start(dgate_ref, dpost_ref, l_ref)
        y = _dot(a_ref[...], w_ref[...])
        e = x_ref[...] + res_w * gate_ref[...] * (y * _rms(y) * pg_ref[...]) - t_ref[...]
        l_ref[...] += 0.5 * jnp.sum(_rowmean(e * e), axis=0, keepdims=True)
        dx = e * (1.0 / d)
        dx_ref[...] = dx
        dy_ref[...] = _postnorm_bwd_math(dx, y, pg_ref[...], gate_ref[...], res_w, dgate_ref, dpost_ref)

    return pl.pallas_call(
        body, name=name, grid=(T // tm,),
        out_shape=(jax.ShapeDtypeStruct((T, d), F32), jax.ShapeDtypeStruct((T, d), BF16),
                   jax.ShapeDtypeStruct((1, d), F32), jax.ShapeDtypeStruct((1, d), F32),
                   jax.ShapeDtypeStruct((1, 128), F32)),
        in_specs=[_row_spec(tm, ka), _whole_spec(w), _row_spec(tm, d), _vec_spec(d), _vec_spec(d), _row_spec(tm, d)],
        out_specs=(_row_spec(tm, d), _row_spec(tm, d), _vec_spec(d), _vec_spec(d),
                   pl.BlockSpec((1, 128), lambda m: (0, 0))),
        compiler_params=_params("arbitrary"),
    )(a, w, x, post_g, gate, target)


def _in_grad(dz, w, dxo, x, pre_g, scale, prev, name, *, tm=512, exchange=None):
    T, C = dz.shape
    d = w.shape[1]
    tm = min(tm, T)
    has_prev = prev is not None
    res_w = prev[3] if has_prev else None

    def body(*refs):
        dz_ref, w_ref, dxo_ref, x_ref, g_ref, sc_ref = refs[:6]
        if has_prev:
            yp_ref, ppg_ref, pgate_ref, dx_ref, dsh_ref, dsc_ref, dg_ref, dyp_ref, dgate_ref, dpost_ref = refs[6:]
            _zero_at_start(dsh_ref, dsc_ref, dg_ref, dgate_ref, dpost_ref)
        else:
            dx_ref, dsh_ref, dsc_ref, dg_ref = refs[6:]
            _zero_at_start(dsh_ref, dsc_ref, dg_ref)
        dh = _dot(dz_ref[...], w_ref[...])
        xv = x_ref[...]
        r = _rms(xv)
        xh = xv * r
        gain = 1.0 + sc_ref[...]
        both = _colsum(dh * xh)
        dsh_ref[...] += _colsum(dh)
        dsc_ref[...] += g_ref[...] * both
        dg_ref[...] += gain * both
        dxh = dh * (gain * g_ref[...])
        dx = dxo_ref[...] + r * (dxh - xh * _rowmean(dxh * xh))
        dx_ref[...] = dx
        if has_prev:
            dyp_ref[...] = _postnorm_bwd_math(dx, yp_ref[...], ppg_ref[...], pgate_ref[...], res_w,
                                               dgate_ref, dpost_ref)

    vec = jax.ShapeDtypeStruct((1, d), F32)
    in_specs = [_row_spec(tm, C), _whole_spec(w), _row_spec(tm, d), _row_spec(tm, d), _vec_spec(d), _vec_spec(d)]
    out_shape = [jax.ShapeDtypeStruct((T, d), F32), vec, vec, vec]
    out_specs = [_row_spec(tm, d), _vec_spec(d), _vec_spec(d), _vec_spec(d)]
    args = [dz, w, dxo, x, pre_g, scale]
    if has_prev:
        in_specs += [_row_spec(tm, d), _vec_spec(d), _vec_spec(d)]
        out_shape += [jax.ShapeDtypeStruct((T, d), BF16), vec, vec]
        out_specs += [_row_spec(tm, d), _vec_spec(d), _vec_spec(d)]
        args += list(prev[:3])
    body, x_args, x_in, x_out, x_scratch = _host(exchange, len(args), len(out_shape), body, _first_last(T // tm))
    return pl.pallas_call(
        body, name=name, grid=(T // tm,),
        out_shape=out_shape + x_out, in_specs=in_specs + x_in, out_specs=out_specs + x_in,
        scratch_shapes=x_scratch,
        compiler_params=_params("arbitrary"),
    )(*args, *x_args)


def _mm_wgrad(x, y, name, *, xw=None, yw=None, tt=2048):
    T, P = x.shape
    Q = y.shape[1]
    xw, yw = xw or P, yw or Q
    jx, jy = P // xw, Q // yw
    assert jx == 1 or jy == 1
    tt = min(tt, T)
    nt = T // tt

    def body(x_ref, y_ref, o_ref, acc_ref):
        t = pl.program_id(1)

        @pl.when(t == 0)
        def _():
            acc_ref[...] = jnp.zeros_like(acc_ref)

        acc_ref[...] += _dot(x_ref[...], y_ref[...], TN)

        @pl.when(t == nt - 1)
        def _():
            o_ref[...] = acc_ref[...].astype(BF16)

    if jy > 1:
        out_shape = jax.ShapeDtypeStruct((jy, P, yw), BF16)
        out_spec = pl.BlockSpec((None, P, yw), lambda j, t: (j, 0, 0))
    else:
        out_shape = jax.ShapeDtypeStruct((P, Q), BF16)
        out_spec = pl.BlockSpec((xw, Q), lambda j, t: (j, 0))
    return pl.pallas_call(
        body, name=name, grid=(max(jx, jy), nt),
        out_shape=out_shape,
        in_specs=[pl.BlockSpec((tt, xw), (lambda j, t: (t, j)) if jx > 1 else (lambda j, t: (t, 0))),
                  pl.BlockSpec((tt, yw), (lambda j, t: (t, j)) if jy > 1 else (lambda j, t: (t, 0)))],
        out_specs=out_spec,
        scratch_shapes=[pltpu.VMEM((xw, yw), F32)],
        compiler_params=_params("parallel", "arbitrary"),
    )(x, y)


def _ffn_in(h, wt, name, *, tm=512, opening=None, exchange=None):
    T, K = h.shape
    F = wt.shape[0] // 2
    tm = min(tm, T)
    n_vec = 3 if opening else 0

    def body(*refs):
        h_ref, w_ref = refs[0], refs[1 + n_vec]
        pq_ref, g_ref = refs[2 + n_vec], refs[3 + n_vec]
        hh = h_ref[...]
        if opening:
            g_vec, sc_vec, sh_vec = refs[1:4]
            hh = (hh * _rms(hh) * (g_vec[...] * (1.0 + sc_vec[...])) + sh_vec[...]).astype(BF16)
            refs[4 + n_vec][...] = hh
        for cols in _col_chunks(F):
            hi = slice(F + cols.start, F + cols.stop)
            a = _dot(hh, w_ref[cols], NT)
            b = _dot(hh, w_ref[hi], NT)
            s = _sigmoid_t(a)
            silu = a * s
            pq_ref[:, cols] = (b * (s * (1.0 + a * (1.0 - s)))).astype(BF16)
            pq_ref[:, hi] = silu.astype(BF16)
            g_ref[:, cols] = (silu * b).astype(BF16)

    outs = [jax.ShapeDtypeStruct((T, 2 * F), BF16), jax.ShapeDtypeStruct((T, F), BF16)]
    out_specs = [_row_spec(tm, 2 * F), _row_spec(tm, F)]
    if opening:
        outs.append(jax.ShapeDtypeStruct((T, K), BF16))
        out_specs.append(_row_spec(tm, K))
    vecs = list(opening) if opening else []
    body, x_args, x_in, x_out, x_scratch = _host(exchange, 2 + n_vec, len(outs), body, _first_last(T // tm))
    res = pl.pallas_call(
        body, name=name, grid=(T // tm,),
        out_shape=outs + x_out,
        in_specs=[_row_spec(tm, K)] + [_vec_spec(K)] * n_vec + [_whole_spec(wt)] + x_in,
        out_specs=out_specs + x_in,
        scratch_shapes=x_scratch,
        compiler_params=_params("arbitrary" if exchange else "parallel"),
    )(h, *vecs, wt, *x_args)
    return res[:len(outs)], res[len(outs):]


def _ffn_dgate(dy, w_out, pq, name, *, tm=512):
    T, N = dy.shape
    F = w_out.shape[0]
    tm = min(tm, T)

    def body(dy_ref, w_ref, pq_ref, dz_ref):
        dyv = dy_ref[...]
        for cols in _col_chunks(F):
            hi = slice(F + cols.start, F + cols.stop)
            dg = _dot(dyv, w_ref[cols], NT)
            dz_ref[:, cols] = (dg * pq_ref[:, cols].astype(F32)).astype(BF16)
            dz_ref[:, hi] = (dg * pq_ref[:, hi].astype(F32)).astype(BF16)

    return pl.pallas_call(
        body, name=name, grid=(T // tm,),
        out_shape=jax.ShapeDtypeStruct((T, 2 * F), BF16),
        in_specs=[_row_spec(tm, N), _whole_spec(w_out), _row_spec(tm, 2 * F)],
        out_specs=_row_spec(tm, 2 * F),
        compiler_params=_params("parallel"),
    )(dy, w_out, pq)


def _ada_fwd(c_all, w, b, name):
    L, K, n = w.shape

    def body(c_ref, w_ref, b_ref, o_ref):
        cv = c_ref[...]
        cond = cv * _sigmoid(cv)
        for l in range(L):
            o_ref[l] = _dot(cond, w_ref[l], precision=lax.Precision.HIGHEST) + b_ref[l]

    return pl.pallas_call(
        body, name=name,
        out_shape=jax.ShapeDtypeStruct((L, NDEV, n), F32),
        compiler_params=pltpu.CompilerParams(vmem_limit_bytes=VMEM_LIMIT),
    )(c_all, w, b)


def _ada_bwd(c_all_t, gmod, name):
    L, _, n = gmod.shape
    K = c_all_t.shape[0]

    def body(c_ref, g_ref, o_ref):
        cv = c_ref[...]
        cond = cv * _sigmoid(cv)
        for l in range(L):
            o_ref[l] = _dot(cond, g_ref[l], precision=lax.Precision.HIGHEST)

    return pl.pallas_call(
        body, name=name,
        out_shape=jax.ShapeDtypeStruct((L, K, n), F32),
        compiler_params=pltpu.CompilerParams(vmem_limit_bytes=VMEM_LIMIT),
    )(c_all_t, gmod)


def _tri(n, upper=False, block=None):
    r = lax.broadcasted_iota(jnp.int32, (n, n), 0)
    c = lax.broadcasted_iota(jnp.int32, (n, n), 1)
    m = (c >= r) if upper else (c <= r)
    if block is not None:
        m = m & ((r // block) == (c // block))
    return m.astype(BF16)


def _tri_dot(tri, x):
    hi = x.astype(BF16)
    lo = (x - hi.astype(F32)).astype(BF16)
    return _dot(tri, hi) + _dot(tri, lo)


def _hgrn_gates(proj_ref, lb_ref, jh):
    proj_ref = _ColBlocks(proj_ref, 512)
    lb = lb_ref[:, 512 * jh:512 * (jh + 1)]
    qp = proj_ref[jh]
    fx = proj_ref[2 + jh]
    sq = _sigmoid_t(qp)
    sig = _sigmoid_t(fx)
    f = lb + (1.0 - lb) * sig
    k = (1.0 - lb) * (1.0 - sig)
    return lb, qp, sq, sig, f, k


def _hgrn_fwd(proj, lb, out_norm, name, *, tb=128, exchange=None):
    T = proj.shape[0]
    tb = min(tb, T)
    nc = tb // HG_CHUNK
    lmat = _tri(tb, block=HG_CHUNK)

    def body(proj_ref, lb_ref, on_ref, l_ref, o_ref, og_ref, st_ref, s_scr, b_scr):
        @pl.when(pl.program_id(0) == 0)
        def _():
            s_scr[...] = jnp.zeros_like(s_scr)

        r_i = lax.broadcasted_iota(jnp.int32, (HG_CHUNK, HG_CHUNK), 0)
        c_i = lax.broadcasted_iota(jnp.int32, (HG_CHUNK, HG_CHUNK), 1)
        causal = c_i <= r_i
        onv = on_ref[...]
        blocks = _ColBlocks(proj_ref, 512)
        for jh in range(2):
            lbv, qp, sq, sig, f, k = _hgrn_gates(proj_ref, lb_ref, jh)
            q = qp * sq
            b_half = b_scr.at[jh]
            b_half[...] = _tri_dot(l_ref[...], jnp.log(f))
            v = blocks[4 + jh]
            gp = blocks[6 + jh]
            gs = gp * _sigmoid_t(gp)
            for hh in range(4):
                hd = 4 * jh + hh
                cs = slice(HG_HEAD * hh, HG_HEAD * (hh + 1))
                for ci in range(nc):
                    r0 = HG_CHUNK * ci
                    rs = slice(r0, r0 + HG_CHUNK)
                    bc = b_half[rs, cs]
                    bm = b_half[r0 + HG_CHUNK // 2 - 1:r0 + HG_CHUNK // 2, cs]
                    bl = b_half[r0 + HG_CHUNK - 1:r0 + HG_CHUNK, cs]
                    qc, kc, vc = q[rs, cs], k[rs, cs], v[rs, cs].astype(BF16)
                    e_q, e_k = jnp.exp(bc - bm), jnp.exp(bm - bc)
                    qe = (qc * (e_q * jnp.exp(bm))).astype(BF16)
                    qt = (qc * e_q).astype(BF16)
                    kt = (kc * e_k).astype(BF16)
                    kd = (kc * (e_k * jnp.exp(bl - bm))).astype(BF16)
                    st = s_scr[hd]
                    stb = st.astype(BF16)
                    st_ref[ci, hd] = stb
                    a = jnp.where(causal, _dot(qt, kt, NT), 0.0).astype(BF16)
                    o = _dot(qe, stb, NT) + _dot(a, vc)
                    s_scr[hd] = st * jnp.exp(bl) + _dot(vc, kd, TN)
                    o_ref[rs, HG_HEAD * hd:HG_HEAD * (hd + 1)] = o
                    r = lax.rsqrt(_rowmean(o * o) + EPS)
                    og_ref[rs, HG_HEAD * hd:HG_HEAD * (hd + 1)] = (o * r * onv * gs[rs, cs]).astype(BF16)

    body, x_args, x_in, x_out, x_scratch = _host(exchange, 4, 3, body, _first_last(T // tb))
    res = pl.pallas_call(
        body, name=name, grid=(T // tb,),
        out_shape=[jax.ShapeDtypeStruct((T, D_MODEL), F32), jax.ShapeDtypeStruct((T, D_MODEL), BF16),
                   jax.ShapeDtypeStruct((T // HG_CHUNK, HG_HEADS, HG_HEAD, HG_HEAD), BF16)] + x_out,
        in_specs=[pl.BlockSpec((tb, 4 * D_MODEL), lambda i: (i, 0)),
                  pl.BlockSpec((1, D_MODEL), lambda i: (0, 0)),
                  pl.BlockSpec((1, HG_HEAD), lambda i: (0, 0)),
                  pl.BlockSpec((tb, tb), lambda i: (0, 0))] + x_in,
        out_specs=[pl.BlockSpec((tb, D_MODEL), lambda i: (i, 0)),
                   pl.BlockSpec((tb, D_MODEL), lambda i: (i, 0)),
                   pl.BlockSpec((nc, HG_HEADS, HG_HEAD, HG_HEAD), lambda i: (i, 0, 0, 0))] + x_in,
        scratch_shapes=[pltpu.VMEM((HG_HEADS, HG_HEAD, HG_HEAD), F32), pltpu.VMEM((2, tb, 512), F32)] + x_scratch,
        compiler_params=_params("arbitrary"),
    )(proj, lb, out_norm, lmat, *x_args)
    return res[0], res[1], res[2], res[3:]


def _hgrn_bwd(proj, o, dog, states, lb, out_norm, name, *, tb=128):
    T = proj.shape[0]
    tb = min(tb, T)
    nc = tb // HG_CHUNK
    nb = T // tb
    lmat = _tri(tb, block=HG_CHUNK)
    umat = _tri(tb, upper=True, block=HG_CHUNK)

    def body(proj_ref, o_ref, dog_ref, st_ref, lb_ref, on_ref, l_ref, u_ref,
             dproj_ref, dlb_ref, don_ref, ds_scr, *half_scr):
        @pl.when(pl.program_id(0) == 0)
        def _():
            ds_scr[...] = jnp.zeros_like(ds_scr)
            dlb_ref[...] = jnp.zeros_like(dlb_ref)
            don_ref[...] = jnp.zeros_like(don_ref)

        r_i = lax.broadcasted_iota(jnp.int32, (HG_CHUNK, HG_CHUNK), 0)
        c_i = lax.broadcasted_iota(jnp.int32, (HG_CHUNK, HG_CHUNK), 1)
        causal = c_i <= r_i
        causal_t = r_i <= c_i
        last_row = lax.broadcasted_iota(jnp.int32, (HG_CHUNK, HG_HEAD), 0) == HG_CHUNK - 1
        onv = on_ref[...]
        don_acc = jnp.zeros((1, HG_HEAD), F32)
        blocks = _ColBlocks(proj_ref, 512)
        dproj_ref = _ColBlocks(dproj_ref, 512)
        for jh in range(2):
            b_scr, dq_scr, dk_scr, dv_scr, dg_scr, db_scr = [s.at[jh] for s in half_scr]
            lbv, qp, sq, sig, f, k = _hgrn_gates(proj_ref, lb_ref, jh)
            q = qp * sq
            b_scr[...] = _tri_dot(l_ref[...], jnp.log(f))
            v = blocks[4 + jh]
            gp = blocks[6 + jh]
            sg = _sigmoid_t(gp)
            for ci in reversed(range(nc)):
                r0 = HG_CHUNK * ci
                rs = slice(r0, r0 + HG_CHUNK)
                for hh in range(4):
                    hd = 4 * jh + hh
                    cs = slice(HG_HEAD * hh, HG_HEAD * (hh + 1))
                    hs = slice(HG_HEAD * hd, HG_HEAD * (hd + 1))
                    oc = o_ref[rs, hs]
                    r = lax.rsqrt(_rowmean(oc * oc) + EPS)
                    oh = oc * r
                    gc, sgc = gp[rs, cs], sg[rs, cs]
                    dogc = dog_ref[rs, hs].astype(F32)
                    don = dogc * (gc * sgc)
                    dg_scr[rs, cs] = dogc * (oh * onv) * (sgc * (1.0 + gc * (1.0 - sgc)))
                    don_acc += _colsum(don * oh)
                    donh = don * onv
                    do = (r * (donh - oh * _rowmean(donh * oh))).astype(BF16)
                    bc = b_scr[rs, cs]
                    bm = b_scr[r0 + HG_CHUNK // 2 - 1:r0 + HG_CHUNK // 2, cs]
                    bl = b_scr[r0 + HG_CHUNK - 1:r0 + HG_CHUNK, cs]
                    qc, kc, vc = q[rs, cs], k[rs, cs], v[rs, cs].astype(BF16)
                    e_q, e_k = jnp.exp(bc - bm), jnp.exp(bm - bc)
                    e_b, e_d = e_q * jnp.exp(bm), e_k * jnp.exp(bl - bm)
                    qe = (qc * e_b).astype(BF16)
                    qt = (qc * e_q).astype(BF16)
                    kt = (kc * e_k).astype(BF16)
                    kd = (kc * e_d).astype(BF16)
                    stb = st_ref[ci, hd]
                    dst = ds_scr[hd]
                    dstb = dst.astype(BF16)
                    a_t = jnp.where(causal_t, _dot(kt, qt, NT), 0.0).astype(BF16)
                    da = jnp.where(causal, _dot(do, vc, NT), 0.0).astype(BF16)
                    da_t = jnp.where(causal_t, _dot(vc, do, NT), 0.0).astype(BF16)
                    dv_scr[rs, cs] = _dot(a_t, do) + _dot(kd, dstb, NT)
                    dqe, dqt = _dot(do, stb), _dot(da, kt)
                    dkt, dkd = _dot(da_t, qt), _dot(vc, dstb)
                    dq_scr[rs, cs] = dqe * e_b + dqt * e_q
                    dk_scr[rs, cs] = dkt * e_k + dkd * e_d
                    e_l = jnp.exp(bl)
                    s_end = stb.astype(F32) * e_l + _dot(vc, kd, TN)
                    dbc = (qe.astype(F32) * dqe + qt.astype(F32) * dqt
                           - kt.astype(F32) * dkt - kd.astype(F32) * dkd)
                    db_scr[rs, cs] = dbc + jnp.where(last_row, _colsum(dstb.astype(F32) * s_end), 0.0)
                    ds_scr[hd] = dst * e_l + _dot(do, qe, TN)
            dq = dq_scr[...]
            dk = dk_scr[...]
            cols = slice(512 * jh, 512 * (jh + 1))
            dlogf = _tri_dot(u_ref[...], db_scr[...])
            one_m_sig = 1.0 - sig
            dsig = (1.0 - lbv) * sig * one_m_sig
            dboth = dlogf / f - dk
            dproj_ref[jh] = (dq * (sq * (1.0 + qp * (1.0 - sq)))).astype(BF16)
            dproj_ref[2 + jh] = (dboth * dsig).astype(BF16)
            dproj_ref[4 + jh] = dv_scr[...].astype(BF16)
            dproj_ref[6 + jh] = dg_scr[...].astype(BF16)
            dlb_ref[:, cols] += _colsum(dboth * one_m_sig)
        don_ref[...] += don_acc

    rev = lambda i: nb - 1 - i
    return pl.pallas_call(
        body, name=name, grid=(nb,),
        out_shape=(jax.ShapeDtypeStruct((T, 4 * D_MODEL), BF16), jax.ShapeDtypeStruct((1, D_MODEL), F32),
                   jax.ShapeDtypeStruct((1, HG_HEAD), F32)),
        in_specs=[pl.BlockSpec((tb, 4 * D_MODEL), lambda i: (rev(i), 0)),
                  pl.BlockSpec((tb, D_MODEL), lambda i: (rev(i), 0)),
                  pl.BlockSpec((tb, D_MODEL), lambda i: (rev(i), 0)),
                  pl.BlockSpec((nc, HG_HEADS, HG_HEAD, HG_HEAD), lambda i: (rev(i), 0, 0, 0)),
                  pl.BlockSpec((1, D_MODEL), lambda i: (0, 0)),
                  pl.BlockSpec((1, HG_HEAD), lambda i: (0, 0)),
                  pl.BlockSpec((tb, tb), lambda i: (0, 0)),
                  pl.BlockSpec((tb, tb), lambda i: (0, 0))],
        out_specs=(pl.BlockSpec((tb, 4 * D_MODEL), lambda i: (rev(i), 0)),
                   pl.BlockSpec((1, D_MODEL), lambda i: (0, 0)),
                   pl.BlockSpec((1, HG_HEAD), lambda i: (0, 0))),
        scratch_shapes=[pltpu.VMEM((HG_HEADS, HG_HEAD, HG_HEAD), F32)] + [pltpu.VMEM((2, tb, 512), F32)] * 6,
        compiler_params=_params("arbitrary"),
    )(proj, o, dog, states, lb, out_norm, lmat, umat)


def _gm_norm(pre_ref, lg_ref, lbias_ref):
    pre_ref = _ColBlocks(pre_ref, 768)
    vs = [pre_ref[4 + j].astype(F32) for j in range(4)]
    width = 4 * vs[0].shape[1]
    mu = sum(jnp.sum(v, axis=1, keepdims=True) for v in vs) / width
    ds = [v - mu for v in vs]
    var = sum(jnp.sum(d * d, axis=1, keepdims=True) for d in ds) / width
    rstd = lax.rsqrt(var + EPS)
    vhat = [d * rstd for d in ds]
    vn = [vhat[j] * lg_ref[j:j + 1, :] + lbias_ref[j:j + 1, :] for j in range(4)]
    return vhat, vn, rstd


def _gm_spatial_fwd(pre, ln_g, ln_b, ws, bsb, name, *, tb=256):
    T = pre.shape[0]
    tb = min(tb, T)
    nc = tb // GM_CHUNK

    def body(pre_ref, lg_ref, lbias_ref, ws_ref, bs_ref, o_ref):
        _, vn, _ = _gm_norm(pre_ref, lg_ref, lbias_ref)
        pre_ref, o_ref = _ColBlocks(pre_ref, 768), _ColBlocks(o_ref, 768)
        for j in range(4):
            u = pre_ref[j].astype(F32)
            for e in range(2):
                g = 2 * j + e
                cs = slice(GM_GDIM * e, GM_GDIM * (e + 1))
                wg = ws_ref[g].astype(BF16)
                for ci in range(nc):
                    rs = slice(GM_CHUNK * ci, GM_CHUNK * (ci + 1))
                    vm = _dot(wg, vn[j][rs, cs].astype(BF16)) + bs_ref[g]
                    o_ref[j, rs, cs] = (u[rs, cs] * vm).astype(BF16)

    return pl.pallas_call(
        body, name=name, grid=(T // tb,),
        out_shape=jax.ShapeDtypeStruct((T, 4 * 768), BF16),
        in_specs=[pl.BlockSpec((tb, 8 * 768), lambda i: (i, 0)),
                  pl.BlockSpec((4, 768), lambda i: (0, 0)),
                  pl.BlockSpec((4, 768), lambda i: (0, 0)),
                  pl.BlockSpec((GM_GROUPS, GM_CHUNK, GM_CHUNK), lambda i: (0, 0, 0)),
                  pl.BlockSpec((GM_GROUPS, GM_CHUNK, GM_GDIM), lambda i: (0, 0, 0))],
        out_specs=pl.BlockSpec((tb, 4 * 768), lambda i: (i, 0)),
        compiler_params=_params("parallel"),
    )(pre, ln_g, ln_b, ws, bsb)


def _gm_spatial_bwd(pre, gp, dm, ln_g, ln_b, ws, ws_t, bsb, name, *, tb=256):
    T = pre.shape[0]
    tb = min(tb, T)
    nc = tb // GM_CHUNK
    nb = T // tb

    def body(pre_ref, gp_ref, dm_ref, lg_ref, lbias_ref, ws_ref, wst_ref, bs_ref,
             dpre_ref, dws_ref, dbs_ref, dlg_ref, dlb_ref, dbin_ref, dbs_scr, dvn_scr, du_scr):
        i = pl.program_id(0)

        @pl.when(i == 0)
        def _():
            dws_ref[...] = jnp.zeros_like(dws_ref)
            dbs_scr[...] = jnp.zeros_like(dbs_scr)
            dlg_ref[...] = jnp.zeros_like(dlg_ref)
            dlb_ref[...] = jnp.zeros_like(dlb_ref)
            dbin_ref[...] = jnp.zeros_like(dbin_ref)

        vhat, vn, rstd = _gm_norm(pre_ref, lg_ref, lbias_ref)
        pre_ref, gp_ref, dm_ref = _ColBlocks(pre_ref, 768), _ColBlocks(gp_ref, 768), _ColBlocks(dm_ref, 768)
        dpre_ref = _ColBlocks(dpre_ref, 768)
        for j in range(4):
            u = pre_ref[j].astype(F32)
            for e in range(2):
                g = 2 * j + e
                cs = slice(GM_GDIM * e, GM_GDIM * (e + 1))
                wg = ws_ref[g].astype(BF16)
                wgt = wst_ref[g].astype(BF16)
                for ci in range(nc):
                    rs = slice(GM_CHUNK * ci, GM_CHUNK * (ci + 1))
                    vnb = vn[j][rs, cs].astype(BF16)
                    vm = _dot(wg, vnb) + bs_ref[g]
                    dmg = dm_ref[j, rs, cs].astype(F32)
                    du_scr[j, rs, cs] = dmg * vm
                    dvm = dmg * u[rs, cs]
                    dvmb = dvm.astype(BF16)
                    dws_ref[g] += _dot(dvmb, vnb, NT)
                    dbs_scr[g] += dvm
                    dvn_scr[j, rs, cs] = _dot(wgt, dvmb)
        width = 4 * 768
        dvh = []
        for j in range(4):
            dvn = dvn_scr[j]
            dlg_ref[j:j + 1, :] += _colsum(dvn * vhat[j])
            dlb_ref[j:j + 1, :] += _colsum(dvn)
            dvh.append(dvn * lg_ref[j:j + 1, :])
        m1 = sum(jnp.sum(d, axis=1, keepdims=True) for d in dvh) / width
        m2 = sum(jnp.sum(dvh[j] * vhat[j], axis=1, keepdims=True) for j in range(4)) / width
        for j in range(4):
            dv = rstd * (dvh[j] - m1 - vhat[j] * m2)
            dpv = dv * gp_ref[4 + j].astype(F32)
            dpu = du_scr[j] * gp_ref[j].astype(F32)
            dpre_ref[4 + j] = dpv.astype(BF16)
            dpre_ref[j] = dpu.astype(BF16)
            dbin_ref[4 + j:5 + j, :] += _colsum(dpv)
            dbin_ref[j:j + 1, :] += _colsum(dpu)

        @pl.when(i == nb - 1)
        def _():
            r_i = lax.broadcasted_iota(jnp.int32, (GM_CHUNK, GM_CHUNK), 0)
            c_i = lax.broadcasted_iota(jnp.int32, (GM_CHUNK, GM_CHUNK), 1)
            for g in range(GM_GROUPS):
                dws_ref[g] = jnp.where(c_i <= r_i, dws_ref[g], 0.0)
                dbs_ref[g] = jnp.broadcast_to(jnp.sum(dbs_scr[g], axis=1, keepdims=True), (GM_CHUNK, GM_CHUNK))

    sq = pl.BlockSpec((GM_GROUPS, GM_CHUNK, GM_CHUNK), lambda i: (0, 0, 0))
    v4 = pl.BlockSpec((4, 768), lambda i: (0, 0))
    return pl.pallas_call(
        body, name=name, grid=(nb,),
        out_shape=(jax.ShapeDtypeStruct((T, 8 * 768), BF16),
                   jax.ShapeDtypeStruct((GM_GROUPS, GM_CHUNK, GM_CHUNK), F32),
                   jax.ShapeDtypeStruct((GM_GROUPS, GM_CHUNK, GM_CHUNK), F32),
                   jax.ShapeDtypeStruct((4, 768), F32), jax.ShapeDtypeStruct((4, 768), F32),
                   jax.ShapeDtypeStruct((8, 768), F32)),
        in_specs=[pl.BlockSpec((tb, 8 * 768), lambda i: (i, 0)),
                  pl.BlockSpec((tb, 8 * 768), lambda i: (i, 0)),
                  pl.BlockSpec((tb, 4 * 768), lambda i: (i, 0)),
                  v4, v4, sq, sq,
                  pl.BlockSpec((GM_GROUPS, GM_CHUNK, GM_GDIM), lambda i: (0, 0, 0))],
        out_specs=(pl.BlockSpec((tb, 8 * 768), lambda i: (i, 0)), sq, sq, v4, v4,
                   pl.BlockSpec((8, 768), lambda i: (0, 0))),
        scratch_shapes=[pltpu.VMEM((GM_GROUPS, GM_CHUNK, GM_GDIM), F32),
                        pltpu.VMEM((4, tb, 768), F32), pltpu.VMEM((4, tb, 768), F32)],
        compiler_params=_params("arbitrary"),
    )(pre, gp, dm, ln_g, ln_b, ws, ws_t, bsb)


def _adamw(slots, w, m, v, name, *, tr=256):
    S, R, C = slots.shape
    tr = next((t for t in (tr, tr // 2, tr // 4, tr // 8, tr // 16) if R % t == 0), R) if R > tr else R
    bc1 = 1.0 - ADAM_B1 ** ADAM_STEP
    bc2 = 1.0 - ADAM_B2 ** ADAM_STEP

    def body(s_ref, w_ref, m_ref, v_ref, g_ref, d_ref, nm_ref, nv_ref):
        g = s_ref[0].astype(F32)
        for s in range(1, S):
            g = g + s_ref[s].astype(F32)
        mn = ADAM_B1 * m_ref[...] + (1.0 - ADAM_B1) * g
        vn = ADAM_B2 * v_ref[...] + (1.0 - ADAM_B2) * (g * g)
        g_ref[...] = g
        nm_ref[...] = mn
        nv_ref[...] = vn
        d_ref[...] = -ADAM_LR * ((mn / bc1) / (jnp.sqrt(vn / bc2) + ADAM_EPS) + ADAM_WD * w_ref[...])

    spec = pl.BlockSpec((tr, C), lambda i: (i, 0))
    return pl.pallas_call(
        body, name=name, grid=(R // tr,),
        out_shape=(jax.ShapeDtypeStruct((R, C), F32),) * 4,
        in_specs=[pl.BlockSpec((S, tr, C), lambda i: (0, i, 0)), spec, spec, spec],
        out_specs=(spec,) * 4,
        compiler_params=_params("parallel"),
    )(slots, w, m, v)


def _update(slots, w, m, v, name):
    shp = w.shape
    C = shp[-1]
    R = math.prod(shp[:-1])
    outs = _adamw(slots.reshape(slots.shape[0], R, C), w.reshape(R, C), m.reshape(R, C), v.reshape(R, C), name)
    return tuple(o.reshape(shp) for o in outs)


def kernel(x, c, ada_w, ada_b, norm_pre, norm_post, ffn_w_in, ffn_w_out, hg_w_in, hg_w_out, hg_out_norm, hg_lb, gm_w_in, gm_b_in, gm_ln_g, gm_ln_b, gm_w_s, gm_b_s, gm_w_out, loss_target, m_ada_w, m_ada_b, m_norm_pre, m_norm_post, m_ffn_w_in, m_ffn_w_out, m_hg_w_in, m_hg_w_out, m_hg_out_norm, m_hg_lb, m_gm_w_in, m_gm_b_in, m_gm_ln_g, m_gm_ln_b, m_gm_w_s, m_gm_b_s, m_gm_w_out, v_ada_w, v_ada_b, v_norm_pre, v_norm_post, v_ffn_w_in, v_ffn_w_out, v_hg_w_in, v_hg_w_out, v_hg_out_norm, v_hg_lb, v_gm_w_in, v_gm_b_in, v_gm_ln_g, v_gm_ln_b, v_gm_w_s, v_gm_b_s, v_gm_w_out):
    me = 4 * lax.axis_index("x") + 2 * lax.axis_index("y") + lax.axis_index("c")
    T = x.shape[1]
    x0 = x.reshape(T, D_MODEL)
    target = loss_target.reshape(T, D_MODEL)
    n_ada = ada_w.shape[-1]

    pack = jnp.concatenate([
        c.reshape(8, 128), norm_pre.reshape(6, 128), norm_post.reshape(6, 128),
        gm_b_in.reshape(6, 128), gm_ln_g.reshape(3, 128), gm_ln_b.reshape(3, 128)], axis=0)
    packs = _all_gather(pack, "gather_small")
    c_all = packs[:, 0:8].reshape(NDEV, D_MODEL)
    npre = packs[:, 8:14].reshape(NDEV, 2, 3, 128).transpose(1, 2, 0, 3).reshape(2, 3, D_MODEL)
    npost = packs[:, 14:20].reshape(NDEV, 2, 3, 128).transpose(1, 2, 0, 3).reshape(2, 3, D_MODEL)
    b_in = packs[:, 20:26].reshape(1, NDEV * 768)
    ln_g = packs[:, 26:29].reshape(4, 768)
    ln_b = packs[:, 29:32].reshape(4, 768)

    ada_b_mine = lax.dynamic_slice_in_dim(ada_b, me * n_ada, n_ada, axis=1).reshape(2, 1, n_ada)
    mod_cols = _ada_fwd(c_all, ada_w, ada_b_mine, "ada_fwd")
    mod_all = _all_gather(mod_cols, "gather_mod")
    mod = lax.dynamic_index_in_dim(mod_all, me, axis=2, keepdims=False)
    mod = mod.transpose(1, 0, 2).reshape(2, 9, 1, D_MODEL)

    sh_fi, sh_fo = ffn_w_in.astype(BF16).swapaxes(-1, -2), ffn_w_out.astype(BF16)
    sh_hi, sh_ho = hg_w_in[0].astype(BF16).T, hg_w_out[0].astype(BF16)
    sh_mi, sh_mo = gm_w_in[0].astype(BF16).T, gm_w_out[0].astype(BF16)

    def whole(gathered):
        return gathered.reshape(-1, D_MODEL)

    w_fi = {(0, 0): whole(_all_gather(sh_fi[0, 0], "gather_ffn_in_first"))}
    w_fo = {}
    riders = {"l0s0": [sh_fo[0, 0], sh_hi, sh_ho], "l0s1": [sh_fi[0, 1], sh_fo[0, 1]], "hg_mix": [sh_mi, sh_mo],
              "l0s2": [sh_fi[1, 0], sh_fo[1, 0]], "l1s0": [sh_fi[1, 1], sh_fo[1, 1]]}

    sm = jax.nn.softmax(hg_lb, axis=0)
    lb0 = sm[0:1]
    on = hg_out_norm.reshape(1, HG_HEAD)
    tril = jnp.tril(jnp.ones((GM_CHUNK, GM_CHUNK), F32))
    ws = gm_w_s[0] * tril[None]
    ws_t = ws.transpose(0, 2, 1)
    bsb = jnp.broadcast_to(gm_b_s[0][:, :, None], (GM_GROUPS, GM_CHUNK, GM_GDIM))

    res_ws = (0.5, 1.0, 0.5)

    def vecs(i, s):
        return (npre[i, s].reshape(1, D_MODEL), npost[i, s].reshape(1, D_MODEL),
                mod[i, 3 * s], mod[i, 3 * s + 1], mod[i, 3 * s + 2])

    order = [(i, s) for i in range(2) for s in range(3)]
    saved = {}
    xs = x0
    for pos, (i, s) in enumerate(order):
        tag = f"l{i}s{s}"
        pre_g, post_g, shift, scale, gate = vecs(i, s)
        rider = _Exchange("gather", riders[tag]) if tag in riders else None
        if s != 1:
            if pos == 0:
                (pq, a, h), got = _ffn_in(xs, w_fi[0, 0], "ffn_in_" + tag, opening=(pre_g, scale, shift), exchange=rider)
            else:
                (pq, a), got = _ffn_in(h, w_fi[i, s // 2], "ffn_in_" + tag, exchange=rider)
            extra = (pq, a)
            if tag == "l0s0":
                w_fo[0, 0], w_hi, w_ho = map(whole, got)
            elif tag == "l0s2":
                w_fi[1, 0], w_fo[1, 0] = map(whole, got)
            elif tag == "l1s0":
                w_fi[1, 1], w_fo[1, 1] = map(whole, got)
            wo = w_fo[i, s // 2]
        elif i == 0:
            proj, *got = _mm_blocks(h, w_hi, "hg_in", out_dtype=F32, exchange=rider)
            w_fi[0, 1], w_fo[0, 1] = map(whole, got)
            o, og, states, got = _hgrn_fwd(proj, lb0, on, "hg_mix", exchange=_Exchange("gather", riders["hg_mix"]))
            w_mi, w_mo = map(whole, got)
            a, wo = og, w_ho
            extra = (proj, o, og, states)
        else:
            pre, gp = _mm_blocks(h, w_mi, "gm_in", bias=b_in, gelu=True)
            a = _gm_spatial_fwd(pre, ln_g, ln_b, ws, bsb, "gm_mix")
            wo = w_mo
            extra = (pre, gp, a)
        if pos + 1 < len(order):
            npre_g, _, nshift, nscale, _ = vecs(*order[pos + 1])
            y, x_next, h_next = _out_proj(a, wo, xs, post_g, gate, res_ws[s], (npre_g, nscale, nshift), "out_" + tag)
            saved[tag] = (xs, h, y) + extra
            xs, h = x_next, h_next
        else:
            dx, dy, dgate, dpost, loss_part = _out_proj_last(a, wo, xs, post_g, gate, res_ws[s], target, "out_" + tag)
            saved[tag] = (xs, h, None) + extra
    loss = lax.psum(loss_part[0, 0], ("x", "y", "c"))

    slots = {}
    d_npre = [[None] * 3, [None] * 3]
    d_npost = [[None] * 3, [None] * 3]
    d_mod = [[None] * 9, [None] * 9]
    for pos in reversed(range(len(order))):
        i, s = order[pos]
        tag = f"l{i}s{s}"
        pre_g, _, _, scale, _ = vecs(i, s)
        xin, h = saved[tag][:2]
        if s != 1:
            w_in, wo = w_fi[i, s // 2], w_fo[i, s // 2]
            pq, g = saved[tag][3:]
            dz = _ffn_dgate(dy, wo, pq, "ffn_dgate_" + tag)
            g_out = _mm_wgrad(g, dy, "ffn_out_wgrad_" + tag, xw=1408)
            g_in = _mm_wgrad(dz, h, "ffn_in_wgrad_" + tag, xw=1408)
        elif i == 0:
            proj, o, og, states = saved[tag][3:]
            dog = _mm_blocks(dy, w_ho, "hg_out_dgrad")
            g_out = _mm_wgrad(og, dy, "hg_out_wgrad")
            dz, d_lb0, d_on = _hgrn_bwd(proj, o, dog, states, lb0, on, "hg_mix_bwd")
            w_in = w_hi
            g_in = _mm_wgrad(h, dz, "hg_in_wgrad", yw=512)
        else:
            pre, gp, sp = saved[tag][3:]
            dm = _mm_blocks(dy, w_mo, "gm_out_dgrad")
            g_out = _mm_wgrad(sp, dy, "gm_out_wgrad", xw=768)
            dz, d_ws, d_bs, d_lg, d_lbias, d_bin = _gm_spatial_bwd(pre, gp, dm, ln_g, ln_b, ws, ws_t, bsb, "gm_mix_bwd")
            w_in = w_mi
            g_in = _mm_wgrad(h, dz, "gm_in_wgrad", yw=768)
        g_out = g_out.reshape(NDEV, -1, D_MODEL)
        g_in = g_in.reshape(NDEV, -1, D_MODEL) if s != 1 else g_in
        d_npost[i][s] = dpost
        d_mod[i][3 * s + 2] = dgate
        rider = _Exchange("scatter", [g_in, g_out])
        if pos > 0:
            pi, ps = order[pos - 1]
            _, ppost_g, _, _, pgate = vecs(pi, ps)
            prev = (saved[f"l{pi}s{ps}"][2], ppost_g, pgate, res_ws[ps])
            dx, dshift, dscale, dpre_g, dy, dgate, dpost, r_in, r_out = _in_grad(
                dz, w_in, dx, xin, pre_g, scale, prev, "in_grad_" + tag, exchange=rider)
        else:
            dx, dshift, dscale, dpre_g, r_in, r_out = _in_grad(
                dz, w_in, dx, xin, pre_g, scale, None, "in_grad_" + tag, exchange=rider)
        slots[tag] = (r_in, r_out)
        d_npre[i][s] = dpre_g
        d_mod[i][3 * s], d_mod[i][3 * s + 1] = dshift, dscale
    grad_x = dx.reshape(x.shape)

    ffn_tags = ["l0s0", "l0s2", "l1s0", "l1s2"]
    s_fi = jnp.stack([slots[t][0] for t in ffn_tags], axis=1).swapaxes(-1, -2)
    s_fo = jnp.stack([slots[t][1] for t in ffn_tags], axis=1)
    (s_hi, s_ho), (s_mi, s_mo) = slots["l0s1"], slots["l1s1"]
    s_hi, s_ho, s_mi, s_mo = s_hi[:, None], s_ho[:, None], s_mi[:, None], s_mo[:, None]

    gmod = jnp.stack([jnp.concatenate(d_mod[i], axis=0) for i in range(2)])
    d_sm = lb0 * d_lb0
    d_hg_lb = jnp.concatenate([d_sm, jnp.zeros((2, D_MODEL), F32)], axis=0) - sm * d_sm
    small = [gmod, jnp.stack([jnp.concatenate(r, axis=0) for r in d_npre]),
             jnp.stack([jnp.concatenate(r, axis=0) for r in d_npost]),
             d_on, d_hg_lb, d_bin, d_lg, d_lbias, d_ws, d_bs[:, :, 0]]
    sizes = [a.size for a in small]
    flat = jnp.concatenate([a.reshape(-1) for a in small])
    rows = -(-flat.size // (8 * 128)) * 8
    flat = jnp.pad(flat, (0, rows * 128 - flat.size)).reshape(rows, 128)
    flats = _all_gather(flat, "gather_small_grads").reshape(NDEV, rows * 128)
    parts, off = [], 0
    for a, n in zip(small, sizes):
        parts.append(flats[:, off:off + n].reshape((NDEV,) + a.shape))
        off += n
    p_mod, p_npre, p_npost, p_on, p_lb, p_bin, p_lg, p_lbias, p_ws, p_bs = parts

    def mine(p, width):
        return lax.dynamic_slice_in_dim(p, me * width, width, axis=p.ndim - 1)

    gmod_cols = mine(p_mod.reshape(NDEV, 2, 9 * D_MODEL), n_ada).transpose(1, 0, 2)
    g_ada_w = _ada_bwd(jnp.pad(c_all.T, ((0, 0), (0, 120))), jnp.pad(gmod_cols, ((0, 0), (0, 120), (0, 0))), "ada_bwd")

    out = {}
    out["ada_w"] = _update(g_ada_w[None], ada_w, m_ada_w, v_ada_w, "adamw_ada_w")
    out["ada_b"] = _update(p_mod.reshape(NDEV, 2, 9 * D_MODEL), ada_b, m_ada_b, v_ada_b, "adamw_ada_b")
    out["norm_pre"] = _update(mine(p_npre, 128), norm_pre, m_norm_pre, v_norm_pre, "adamw_norm_pre")
    out["norm_post"] = _update(mine(p_npost, 128), norm_post, m_norm_post, v_norm_post, "adamw_norm_post")
    out["ffn_w_in"] = _update(s_fi.reshape((NDEV,) + ffn_w_in.shape), ffn_w_in, m_ffn_w_in, v_ffn_w_in, "adamw_ffn_in")
    out["ffn_w_out"] = _update(s_fo.reshape((NDEV,) + ffn_w_out.shape), ffn_w_out, m_ffn_w_out, v_ffn_w_out, "adamw_ffn_out")
    out["hg_w_in"] = _update(s_hi, hg_w_in, m_hg_w_in, v_hg_w_in, "adamw_hg_in")
    out["hg_w_out"] = _update(s_ho, hg_w_out, m_hg_w_out, v_hg_w_out, "adamw_hg_out")
    out["hg_out_norm"] = _update(p_on, hg_out_norm, m_hg_out_norm, v_hg_out_norm, "adamw_hg_norm")
    out["hg_lb"] = _update(p_lb, hg_lb, m_hg_lb, v_hg_lb, "adamw_hg_lb")
    out["gm_w_in"] = _update(s_mi, gm_w_in, m_gm_w_in, v_gm_w_in, "adamw_gm_in")
    out["gm_b_in"] = _update(mine(p_bin.reshape(NDEV, 1, 8 * 768), 768), gm_b_in, m_gm_b_in, v_gm_b_in, "adamw_gm_b_in")
    out["gm_ln_g"] = _update(mine(p_lg.reshape(NDEV, 1, 4 * 768), 384), gm_ln_g, m_gm_ln_g, v_gm_ln_g, "adamw_gm_ln_g")
    out["gm_ln_b"] = _update(mine(p_lbias.reshape(NDEV, 1, 4 * 768), 384), gm_ln_b, m_gm_ln_b, v_gm_ln_b, "adamw_gm_ln_b")
    out["gm_w_s"] = _update(p_ws[:, None], gm_w_s, m_gm_w_s, v_gm_w_s, "adamw_gm_w_s")
    out["gm_b_s"] = _update(p_bs[:, None], gm_b_s, m_gm_b_s, v_gm_b_s, "adamw_gm_b_s")
    out["gm_w_out"] = _update(s_mo, gm_w_out, m_gm_w_out, v_gm_w_out, "adamw_gm_out")

    names = ["ada_w", "ada_b", "norm_pre", "norm_post", "ffn_w_in", "ffn_w_out", "hg_w_in", "hg_w_out",
             "hg_out_norm", "hg_lb", "gm_w_in", "gm_b_in", "gm_ln_g", "gm_ln_b", "gm_w_s", "gm_b_s", "gm_w_out"]
    return (loss, grad_x, *[out[n][0] for n in names], *[out[n][1] for n in names],
            *[out[n][2] for n in names], *[out[n][3] for n in names])
```

```python
import math

import jax
import jax.numpy as jnp
from jax import lax
from jax.experimental import pallas as pl
from jax.experimental.pallas import tpu as pltpu

F32 = jnp.float32
BF16 = jnp.bfloat16
NDEV = 8
D_MODEL = 1024
EPS = 1e-6
HG_CHUNK = 64
HG_HEAD = 128
HG_HEADS = 8
GM_CHUNK = 128
GM_GDIM = 384
GM_GROUPS = 8
ADAM_LR = 0.001
ADAM_B1 = 0.9
ADAM_B2 = 0.999
ADAM_EPS = 1e-08
ADAM_WD = 0.01
ADAM_STEP = 10
VMEM_LIMIT = 56 * 2 ** 20

NN = (((1,), (0,)), ((), ()))
NT = (((1,), (1,)), ((), ()))
TN = (((0,), (0,)), ((), ()))
MESH = pl.DeviceIdType.MESH
ANY = pl.BlockSpec(memory_space=pl.ANY)


def _dot(a, b, dims=NN, precision=None):
    return lax.dot_general(a, b, dims, preferred_element_type=F32, precision=precision)


def _params(*sem):
    return pltpu.CompilerParams(dimension_semantics=sem, vmem_limit_bytes=VMEM_LIMIT)


def _sigmoid(x):
    return 1.0 / (1.0 + jnp.exp(-x))


def _sigmoid_t(x):
    return 0.5 * jnp.tanh(0.5 * x) + 0.5


def _gelu_and_grad(x):
    c = math.sqrt(2.0 / math.pi)
    m = (c * 0.044715) * (x * x)
    t = jnp.tanh(x * (c + m))
    hp = 0.5 + 0.5 * t
    return x * hp, hp * (1.0 + (x * (1.0 - t)) * (c + 3.0 * m))


def _colsum(x):
    return jnp.sum(x, axis=0, keepdims=True)


def _rowmean(x):
    return jnp.mean(x, axis=-1, keepdims=True)


def _all_gather(shard, name):
    def body(x_ref, out_ref, send_sems, recv_sems, local_sem):
        x, y, c = lax.axis_index("x"), lax.axis_index("y"), lax.axis_index("c")
        me, sibling = (x, y, c), (x, y, 1 - c)
        chips = [(1 - x, y), (x, 1 - y), (1 - x, 1 - y)]

        def slot(p):
            return out_ref.at[4 * p[0] + 2 * p[1] + p[2]]

        def copy(k, block, to, src=None):
            return pltpu.make_async_remote_copy(
                src_ref=slot(block) if src is None else src, dst_ref=slot(block),
                send_sem=send_sems.at[k], recv_sem=recv_sems.at[k],
                device_id=to, device_id_type=MESH)

        mine = pltpu.make_async_copy(x_ref, slot(me), local_sem)
        mine.start()
        first = [copy(0, me, sibling, src=x_ref)]
        first += [copy(1 + j, me, (*chip, c), src=x_ref) for j, chip in enumerate(chips)]
        for cp in first:
            cp.start()
        passed = [copy(4 + j, (*chip, c), sibling) for j, chip in enumerate(chips)]
        for j, chip in enumerate(chips):
            copy(1 + j, (*chip, c), me).wait_recv()
            passed[j].start()
        copy(0, sibling, me).wait_recv()
        for j, chip in enumerate(chips):
            copy(4 + j, (*chip, 1 - c), me).wait_recv()
        for cp in first + passed:
            cp.wait_send()
        mine.wait()

    return pl.pallas_call(
        body, name=name,
        out_shape=jax.ShapeDtypeStruct((NDEV,) + shard.shape, shard.dtype),
        in_specs=[ANY], out_specs=ANY,
        scratch_shapes=[pltpu.SemaphoreType.DMA((7,)), pltpu.SemaphoreType.DMA((7,)),
                        pltpu.SemaphoreType.DMA(())],
    )(shard)


class _Exchange:
    def __init__(self, kind, arrays):
        self.gather = kind == "gather"
        self.arrays = list(arrays)
        self.n = n = len(self.arrays)
        self.out_shape = [jax.ShapeDtypeStruct(((NDEV,) + a.shape) if self.gather else a.shape, a.dtype)
                          for a in self.arrays]
        self.scratch = [pltpu.SemaphoreType.DMA((n, NDEV - 1)), pltpu.SemaphoreType.DMA((n, NDEV - 1)),
                        pltpu.SemaphoreType.DMA((n,))]

    def _copies(self, in_refs, out_refs, sems):
        send_sems, recv_sems, local_sems = sems
        x, y, c = lax.axis_index("x"), lax.axis_index("y"), lax.axis_index("c")
        me = 4 * x + 2 * y + c
        peers = [(1 - x if k & 4 else x, 1 - y if k & 2 else y, 1 - c if k & 1 else c) for k in range(1, NDEV)]
        local, send, recv = [], [], []
        for a in range(self.n):
            src = (lambda pid, a=a: in_refs[a]) if self.gather else (lambda pid, a=a: in_refs[a].at[pid])
            local.append(pltpu.make_async_copy(src(me), out_refs[a].at[me], local_sems.at[a]))
            for k, p in enumerate(peers):
                pid = 4 * p[0] + 2 * p[1] + p[2]
                for lst, slot in ((send, me), (recv, pid)):
                    lst.append(pltpu.make_async_remote_copy(
                        src_ref=src(pid), dst_ref=out_refs[a].at[slot],
                        send_sem=send_sems.at[a, k], recv_sem=recv_sems.at[a, k],
                        device_id=p, device_id_type=MESH))
        return local, send, recv

    def start(self, first, in_refs, out_refs, sems):
        @pl.when(first)
        def _():
            local, send, _ = self._copies(in_refs, out_refs, sems)
            for cp in local + send:
                cp.start()

    def finish(self, last, in_refs, out_refs, sems):
        @pl.when(last)
        def _():
            local, send, recv = self._copies(in_refs, out_refs, sems)
            for cp in send:
                cp.wait_send()
            for cp in recv:
                cp.wait_recv()
            for cp in local:
                cp.wait()


def _host(exchange, n_in, n_out, body, first_last):
    if exchange is None:
        return body, [], [], [], []
    n = exchange.n

    def hosted(*refs):
        ins, refs = refs[:n_in], refs[n_in:]
        xin, refs = refs[:n], refs[n:]
        outs, refs = refs[:n_out], refs[n_out:]
        xout, refs = refs[:n], refs[n:]
        scratch, sems = refs[:len(refs) - 3], refs[len(refs) - 3:]
        first, last = first_last()
        exchange.start(first, xin, xout, sems)
        body(*ins, *outs, *scratch)
        exchange.finish(last, xin, xout, sems)

    return hosted, exchange.arrays, [ANY] * n, exchange.out_shape, exchange.scratch


def _first_last(steps):
    def at():
        i = pl.program_id(0)
        return i == 0, i == steps - 1
    return at


class _ColBlocks:
    def __init__(self, ref, width):
        self.ref, self.width = ref, width

    def _index(self, key):
        key = key if isinstance(key, tuple) else (key,)
        rows = key[1] if len(key) > 1 else slice(None)
        cols = key[2] if len(key) > 2 else slice(0, self.width)
        c0 = key[0] * self.width
        return rows, slice(c0 + cols.start, c0 + cols.stop)

    def __getitem__(self, key):
        return self.ref[self._index(key)]

    def __setitem__(self, key, value):
        self.ref[self._index(key)] = value


def _col_chunks(width, chunk=768):
    return [slice(c, min(c + chunk, width)) for c in range(0, width, chunk)]


def _row_spec(tm, d):
    return pl.BlockSpec((tm, d), lambda m: (m, 0))


def _vec_spec(d):
    return pl.BlockSpec((1, d), lambda m: (0, 0))


def _whole_spec(w):
    nd = w.ndim
    return pl.BlockSpec(w.shape, lambda m: (0,) * nd, pipeline_mode=pl.Buffered(1))


def _mm_blocks(a, w, name, *, bias=None, out_dtype=BF16, tm=512, chunk=768, gelu=False, exchange=None):
    T, K = a.shape
    C = w.shape[0]
    tm = min(tm, T)
    n_in = 2 + (bias is not None)

    def body(*refs):
        a_ref, w_ref = refs[:2]
        av = a_ref[...]
        for cols in _col_chunks(C, chunk):
            r = _dot(av, w_ref[cols], NT)
            if bias is not None:
                r = r + refs[2][:, cols]
            if gelu:
                z, dz = _gelu_and_grad(r)
                refs[n_in][:, cols] = z.astype(BF16)
                refs[n_in + 1][:, cols] = dz.astype(BF16)
            else:
                refs[n_in][:, cols] = r.astype(out_dtype)

    in_specs = [_row_spec(tm, K), _whole_spec(w)]
    args = [a, w]
    if bias is not None:
        in_specs.append(_whole_spec(bias))
        args.append(bias)
    outs = [jax.ShapeDtypeStruct((T, C), BF16)] * 2 if gelu else [jax.ShapeDtypeStruct((T, C), out_dtype)]
    body, x_args, x_in, x_out, x_scratch = _host(exchange, n_in, len(outs), body, _first_last(T // tm))
    res = pl.pallas_call(
        body, name=name, grid=(T // tm,),
        out_shape=outs + x_out,
        in_specs=in_specs + x_in,
        out_specs=[_row_spec(tm, C)] * len(outs) + x_in,
        scratch_shapes=x_scratch,
        compiler_params=_params("arbitrary" if exchange else "parallel"),
    )(*args, *x_args)
    return res if (exchange or gelu) else res[0]


def _rms(v):
    return lax.rsqrt(_rowmean(v * v) + EPS)


def _zero_at_start(*refs):
    @pl.when(pl.program_id(0) == 0)
    def _():
        for r in refs:
            r[...] = jnp.zeros_like(r)


def _postnorm_bwd_math(dxo, yv, g, gate, res_w, dgate_ref, dpost_ref):
    r = _rms(yv)
    yh = yv * r
    both = res_w * _colsum(dxo * yh)
    dgate_ref[...] += g * both
    dpost_ref[...] += gate * both
    dyh = dxo * (res_w * gate * g)
    return (r * (dyh - yh * _rowmean(dyh * yh))).astype(BF16)


def _out_proj(a, w, x, post_g, gate, res_w, nxt, name, *, tm=512):
    T, ka = a.shape
    d = w.shape[1]
    tm = min(tm, T)

    def body(a_ref, w_ref, x_ref, pg_ref, gate_ref, ng_ref, nsc_ref, nsh_ref, y_ref, xn_ref, h_ref):
        y = _dot(a_ref[...], w_ref[...])
        y_ref[...] = y
        xn = x_ref[...] + (y * _rms(y)) * (res_w * gate_ref[...] * pg_ref[...])
        xn_ref[...] = xn
        h_ref[...] = ((xn * _rms(xn)) * (ng_ref[...] * (1.0 + nsc_ref[...])) + nsh_ref[...]).astype(BF16)

    return pl.pallas_call(
        body, name=name, grid=(T // tm,),
        out_shape=(jax.ShapeDtypeStruct((T, d), F32), jax.ShapeDtypeStruct((T, d), F32),
                   jax.ShapeDtypeStruct((T, d), BF16)),
        in_specs=[_row_spec(tm, ka), _whole_spec(w), _row_spec(tm, d)] + [_vec_spec(d)] * 5,
        out_specs=(_row_spec(tm, d),) * 3,
        compiler_params=_params("parallel"),
    )(a, w, x, post_g, gate, *nxt)


def _out_proj_last(a, w, x, post_g, gate, res_w, target, name, *, tm=512):
    T, ka = a.shape
    d = w.shape[1]
    tm = min(tm, T)

    def body(a_ref, w_ref, x_ref, pg_ref, gate_ref, t_ref, dx_ref, dy_ref, dgate_ref, dpost_ref, l_ref):
        _zero_at_start(dgate_ref, dpost_ref, l_ref)
        y = _dot(a_ref[...], w_ref[...])
        e = x_ref[...] + res_w * gate_ref[...] * (y * _rms(y) * pg_ref[...]) - t_ref[...]
        l_ref[...] += 0.5 * jnp.sum(_rowmean(e * e), axis=0, keepdims=True)
        dx = e * (1.0 / d)
        dx_ref[...] = dx
        dy_ref[...] = _postnorm_bwd_math(dx, y, pg_ref[...], gate_ref[...], res_w, dgate_ref, dpost_ref)

    return pl.pallas_call(
        body, name=name, grid=(T // tm,),
        out_shape=(jax.ShapeDtypeStruct((T, d), F32), jax.ShapeDtypeStruct((T, d), BF16),
                   jax.ShapeDtypeStruct((1, d), F32), jax.ShapeDtypeStruct((1, d), F32),
                   jax.ShapeDtypeStruct((1, 128), F32)),
        in_specs=[_row_spec(tm, ka), _whole_spec(w), _row_spec(tm, d), _vec_spec(d), _vec_spec(d), _row_spec(tm, d)],
        out_specs=(_row_spec(tm, d), _row_spec(tm, d), _vec_spec(d), _vec_spec(d),
                   pl.BlockSpec((1, 128), lambda m: (0, 0))),
        compiler_params=_params("arbitrary"),
    )(a, w, x, post_g, gate, target)


def _in_grad(dz, w, dxo, x, pre_g, scale, prev, name, *, tm=512, exchange=None):
    T, C = dz.shape
    d = w.shape[1]
    tm = min(tm, T)
    has_prev = prev is not None
    res_w = prev[3] if has_prev else None

    def body(*refs):
        dz_ref, w_ref, dxo_ref, x_ref, g_ref, sc_ref = refs[:6]
        if has_prev:
            yp_ref, ppg_ref, pgate_ref, dx_ref, dsh_ref, dsc_ref, dg_ref, dyp_ref, dgate_ref, dpost_ref = refs[6:]
            _zero_at_start(dsh_ref, dsc_ref, dg_ref, dgate_ref, dpost_ref)
        else:
            dx_ref, dsh_ref, dsc_ref, dg_ref = refs[6:]
            _zero_at_start(dsh_ref, dsc_ref, dg_ref)
        dh = _dot(dz_ref[...], w_ref[...])
        xv = x_ref[...]
        r = _rms(xv)
        xh = xv * r
        gain = 1.0 + sc_ref[...]
        both = _colsum(dh * xh)
        dsh_ref[...] += _colsum(dh)
        dsc_ref[...] += g_ref[...] * both
        dg_ref[...] += gain * both
        dxh = dh * (gain * g_ref[...])
        dx = dxo_ref[...] + r * (dxh - xh * _rowmean(dxh * xh))
        dx_ref[...] = dx
        if has_prev:
            dyp_ref[...] = _postnorm_bwd_math(dx, yp_ref[...], ppg_ref[...], pgate_ref[...], res_w,
                                               dgate_ref, dpost_ref)

    vec = jax.ShapeDtypeStruct((1, d), F32)
    in_specs = [_row_spec(tm, C), _whole_spec(w), _row_spec(tm, d), _row_spec(tm, d), _vec_spec(d), _vec_spec(d)]
    out_shape = [jax.ShapeDtypeStruct((T, d), F32), vec, vec, vec]
    out_specs = [_row_spec(tm, d), _vec_spec(d), _vec_spec(d), _vec_spec(d)]
    args = [dz, w, dxo, x, pre_g, scale]
    if has_prev:
        in_specs += [_row_spec(tm, d), _vec_spec(d), _vec_spec(d)]
        out_shape += [jax.ShapeDtypeStruct((T, d), BF16), vec, vec]
        out_specs += [_row_spec(tm, d), _vec_spec(d), _vec_spec(d)]
        args += list(prev[:3])
    body, x_args, x_in, x_out, x_scratch = _host(exchange, len(args), len(out_shape), body, _first_last(T // tm))
    return pl.pallas_call(
        body, name=name, grid=(T // tm,),
        out_shape=out_shape + x_out, in_specs=in_specs + x_in, out_specs=out_specs + x_in,
        scratch_shapes=x_scratch,
        compiler_params=_params("arbitrary"),
    )(*args, *x_args)


def _mm_wgrad(x, y, name, *, xw=None, yw=None, tt=2048):
    T, P = x.shape
    Q = y.shape[1]
    xw, yw = xw or P, yw or Q
    jx, jy = P // xw, Q // yw
    assert jx == 1 or jy == 1
    tt = min(tt, T)
    nt = T // tt

    def body(x_ref, y_ref, o_ref, acc_ref):
        t = pl.program_id(1)

        @pl.when(t == 0)
        def _():
            acc_ref[...] = jnp.zeros_like(acc_ref)

        acc_ref[...] += _dot(x_ref[...], y_ref[...], TN)

        @pl.when(t == nt - 1)
        def _():
            o_ref[...] = acc_ref[...].astype(BF16)

    if jy > 1:
        out_shape = jax.ShapeDtypeStruct((jy, P, yw), BF16)
        out_spec = pl.BlockSpec((None, P, yw), lambda j, t: (j, 0, 0))
    else:
        out_shape = jax.ShapeDtypeStruct((P, Q), BF16)
        out_spec = pl.BlockSpec((xw, Q), lambda j, t: (j, 0))
    return pl.pallas_call(
        body, name=name, grid=(max(jx, jy), nt),
        out_shape=out_shape,
        in_specs=[pl.BlockSpec((tt, xw), (lambda j, t: (t, j)) if jx > 1 else (lambda j, t: (t, 0))),
                  pl.BlockSpec((tt, yw), (lambda j, t: (t, j)) if jy > 1 else (lambda j, t: (t, 0)))],
        out_specs=out_spec,
        scratch_shapes=[pltpu.VMEM((xw, yw), F32)],
        compiler_params=_params("parallel", "arbitrary"),
    )(x, y)


def _ffn_in(h, wt, name, *, tm=512, opening=None, exchange=None):
    T, K = h.shape
    F = wt.shape[0] // 2
    tm = min(tm, T)
    n_vec = 3 if opening else 0

    def body(*refs):
        h_ref, w_ref = refs[0], refs[1 + n_vec]
        pq_ref, g_ref = refs[2 + n_vec], refs[3 + n_vec]
        hh = h_ref[...]
        if opening:
            g_vec, sc_vec, sh_vec = refs[1:4]
            hh = (hh * _rms(hh) * (g_vec[...] * (1.0 + sc_vec[...])) + sh_vec[...]).astype(BF16)
            refs[4 + n_vec][...] = hh
        for cols in _col_chunks(F):
            hi = slice(F + cols.start, F + cols.stop)
            a = _dot(hh, w_ref[cols], NT)
            b = _dot(hh, w_ref[hi], NT)
            s = _sigmoid_t(a)
            silu = a * s
            pq_ref[:, cols] = (b * (s * (1.0 + a * (1.0 - s)))).astype(BF16)
            pq_ref[:, hi] = silu.astype(BF16)
            g_ref[:, cols] = (silu * b).astype(BF16)

    outs = [jax.ShapeDtypeStruct((T, 2 * F), BF16), jax.ShapeDtypeStruct((T, F), BF16)]
    out_specs = [_row_spec(tm, 2 * F), _row_spec(tm, F)]
    if opening:
        outs.append(jax.ShapeDtypeStruct((T, K), BF16))
        out_specs.append(_row_spec(tm, K))
    vecs = list(opening) if opening else []
    body, x_args, x_in, x_out, x_scratch = _host(exchange, 2 + n_vec, len(outs), body, _first_last(T // tm))
    res = pl.pallas_call(
        body, name=name, grid=(T // tm,),
        out_shape=outs + x_out,
        in_specs=[_row_spec(tm, K)] + [_vec_spec(K)] * n_vec + [_whole_spec(wt)] + x_in,
        out_specs=out_specs + x_in,
        scratch_shapes=x_scratch,
        compiler_params=_params("arbitrary" if exchange else "parallel"),
    )(h, *vecs, wt, *x_args)
    return res[:len(outs)], res[len(outs):]


def _ffn_dgate(dy, w_out, pq, name, *, tm=512):
    T, N = dy.shape
    F = w_out.shape[0]
    tm = min(tm, T)

    def body(dy_ref, w_ref, pq_ref, dz_ref):
        dyv = dy_ref[...]
        for cols in _col_chunks(F):
            hi = slice(F + cols.start, F + cols.stop)
            dg = _dot(dyv, w_ref[cols], NT)
            dz_ref[:, cols] = (dg * pq_ref[:, cols].astype(F32)).astype(BF16)
            dz_ref[:, hi] = (dg * pq_ref[:, hi].astype(F32)).astype(BF16)

    return pl.pallas_call(
        body, name=name, grid=(T // tm,),
        out_shape=jax.ShapeDtypeStruct((T, 2 * F), BF16),
        in_specs=[_row_spec(tm, N), _whole_spec(w_out), _row_spec(tm, 2 * F)],
        out_specs=_row_spec(tm, 2 * F),
        compiler_params=_params("parallel"),
    )(dy, w_out, pq)


def _ada_fwd(c_all, w, b, name):
    L, K, n = w.shape

    def body(c_ref, w_ref, b_ref, o_ref):
        cv = c_ref[...]
        cond = cv * _sigmoid(cv)
        for l in range(L):
            o_ref[l] = _dot(cond, w_ref[l], precision=lax.Precision.HIGHEST) + b_ref[l]

    return pl.pallas_call(
        body, name=name,
        out_shape=jax.ShapeDtypeStruct((L, NDEV, n), F32),
        compiler_params=pltpu.CompilerParams(vmem_limit_bytes=VMEM_LIMIT),
    )(c_all, w, b)


def _ada_bwd(c_all_t, gmod, name):
    L, _, n = gmod.shape
    K = c_all_t.shape[0]

    def body(c_ref, g_ref, o_ref):
        cv = c_ref[...]
        cond = cv * _sigmoid(cv)
        for l in range(L):
            o_ref[l] = _dot(cond, g_ref[l], precision=lax.Precision.HIGHEST)

    return pl.pallas_call(
        body, name=name,
        out_shape=jax.ShapeDtypeStruct((L, K, n), F32),
        compiler_params=pltpu.CompilerParams(vmem_limit_bytes=VMEM_LIMIT),
    )(c_all_t, gmod)


def _tri(n, upper=False, block=None):
    r = lax.broadcasted_iota(jnp.int32, (n, n), 0)
    c = lax.broadcasted_iota(jnp.int32, (n, n), 1)
    m = (c >= r) if upper else (c <= r)
    if block is not None:
        m = m & ((r // block) == (c // block))
    return m.astype(BF16)


TRI_ROWS = 128


def _tri_dot(tri, x):
    hi = x.astype(BF16)
    lo = (x - hi.astype(F32)).astype(BF16)
    rows = x.shape[0]
    step = min(TRI_ROWS, rows)
    parts = [_dot(tri, hi[r:r + step]) + _dot(tri, lo[r:r + step]) for r in range(0, rows, step)]
    return parts[0] if len(parts) == 1 else jnp.concatenate(parts, axis=0)


def _hgrn_gates(proj_ref, lb_ref, jh):
    proj_ref = _ColBlocks(proj_ref, 512)
    lb = lb_ref[:, 512 * jh:512 * (jh + 1)]
    qp = proj_ref[jh]
    fx = proj_ref[2 + jh]
    sq = _sigmoid_t(qp)
    sig = _sigmoid_t(fx)
    f = lb + (1.0 - lb) * sig
    k = (1.0 - lb) * (1.0 - sig)
    return lb, qp, sq, sig, f, k


def _hgrn_fwd(proj, lb, out_norm, name, *, tb=512, exchange=None):
    T = proj.shape[0]
    tb = min(tb, T)
    nc = tb // HG_CHUNK
    lmat = _tri(min(TRI_ROWS, tb), block=HG_CHUNK)

    def body(proj_ref, lb_ref, on_ref, l_ref, o_ref, og_ref, st_ref, s_scr, b_scr):
        @pl.when(pl.program_id(0) == 0)
        def _():
            s_scr[...] = jnp.zeros_like(s_scr)

        r_i = lax.broadcasted_iota(jnp.int32, (HG_CHUNK, HG_CHUNK), 0)
        c_i = lax.broadcasted_iota(jnp.int32, (HG_CHUNK, HG_CHUNK), 1)
        causal = c_i <= r_i
        onv = on_ref[...]
        blocks = _ColBlocks(proj_ref, 512)
        for jh in range(2):
            lbv, qp, sq, sig, f, k = _hgrn_gates(proj_ref, lb_ref, jh)
            q = qp * sq
            b_half = b_scr.at[jh]
            b_half[...] = _tri_dot(l_ref[...], jnp.log(f))
            v = blocks[4 + jh]
            gp = blocks[6 + jh]
            gs = gp * _sigmoid_t(gp)
            for hh in range(4):
                hd = 4 * jh + hh
                cs = slice(HG_HEAD * hh, HG_HEAD * (hh + 1))
                for ci in range(nc):
                    r0 = HG_CHUNK * ci
                    rs = slice(r0, r0 + HG_CHUNK)
                    bc = b_half[rs, cs]
                    bm = b_half[r0 + HG_CHUNK // 2 - 1:r0 + HG_CHUNK // 2, cs]
                    bl = b_half[r0 + HG_CHUNK - 1:r0 + HG_CHUNK, cs]
                    qc, kc, vc = q[rs, cs], k[rs, cs], v[rs, cs].astype(BF16)
                    e_q, e_k = jnp.exp(bc - bm), jnp.exp(bm - bc)
                    qe = (qc * (e_q * jnp.exp(bm))).astype(BF16)
                    qt = (qc * e_q).astype(BF16)
                    kt = (kc * e_k).astype(BF16)
                    kd = (kc * (e_k * jnp.exp(bl - bm))).astype(BF16)
                    st = s_scr[hd]
                    stb = st.astype(BF16)
                    st_ref[ci, hd] = stb
                    a = jnp.where(causal, _dot(qt, kt, NT), 0.0).astype(BF16)
                    o = _dot(qe, stb, NT) + _dot(a, vc)
                    s_scr[hd] = st * jnp.exp(bl) + _dot(vc, kd, TN)
                    o_ref[rs, HG_HEAD * hd:HG_HEAD * (hd + 1)] = o
                    r = lax.rsqrt(_rowmean(o * o) + EPS)
                    og_ref[rs, HG_HEAD * hd:HG_HEAD * (hd + 1)] = (o * r * onv * gs[rs, cs]).astype(BF16)

    body, x_args, x_in, x_out, x_scratch = _host(exchange, 4, 3, body, _first_last(T // tb))
    res = pl.pallas_call(
        body, name=name, grid=(T // tb,),
        out_shape=[jax.ShapeDtypeStruct((T, D_MODEL), F32), jax.ShapeDtypeStruct((T, D_MODEL), BF16),
                   jax.ShapeDtypeStruct((T // HG_CHUNK, HG_HEADS, HG_HEAD, HG_HEAD), BF16)] + x_out,
        in_specs=[pl.BlockSpec((tb, 4 * D_MODEL), lambda i: (i, 0)),
                  pl.BlockSpec((1, D_MODEL), lambda i: (0, 0)),
                  pl.BlockSpec((1, HG_HEAD), lambda i: (0, 0)),
                  pl.BlockSpec(lmat.shape, lambda i: (0, 0))] + x_in,
        out_specs=[pl.BlockSpec((tb, D_MODEL), lambda i: (i, 0)),
                   pl.BlockSpec((tb, D_MODEL), lambda i: (i, 0)),
                   pl.BlockSpec((nc, HG_HEADS, HG_HEAD, HG_HEAD), lambda i: (i, 0, 0, 0))] + x_in,
        scratch_shapes=[pltpu.VMEM((HG_HEADS, HG_HEAD, HG_HEAD), F32), pltpu.VMEM((2, tb, 512), F32)] + x_scratch,
        compiler_params=_params("arbitrary"),
    )(proj, lb, out_norm, lmat, *x_args)
    return res[0], res[1], res[2], res[3:]


def _hgrn_bwd(proj, o, dog, states, lb, out_norm, name, *, tb=512):
    T = proj.shape[0]
    tb = min(tb, T)
    nc = tb // HG_CHUNK
    nb = T // tb
    lmat = _tri(min(TRI_ROWS, tb), block=HG_CHUNK)
    umat = _tri(min(TRI_ROWS, tb), upper=True, block=HG_CHUNK)

    def body(proj_ref, o_ref, dog_ref, st_ref, lb_ref, on_ref, l_ref, u_ref,
             dproj_ref, dlb_ref, don_ref, ds_scr, *half_scr):
        @pl.when(pl.program_id(0) == 0)
        def _():
            ds_scr[...] = jnp.zeros_like(ds_scr)
            dlb_ref[...] = jnp.zeros_like(dlb_ref)
            don_ref[...] = jnp.zeros_like(don_ref)

        r_i = lax.broadcasted_iota(jnp.int32, (HG_CHUNK, HG_CHUNK), 0)
        c_i = lax.broadcasted_iota(jnp.int32, (HG_CHUNK, HG_CHUNK), 1)
        causal = c_i <= r_i
        causal_t = r_i <= c_i
        last_row = lax.broadcasted_iota(jnp.int32, (HG_CHUNK, HG_HEAD), 0) == HG_CHUNK - 1
        onv = on_ref[...]
        don_acc = jnp.zeros((1, HG_HEAD), F32)
        blocks = _ColBlocks(proj_ref, 512)
        dproj_ref = _ColBlocks(dproj_ref, 512)
        for jh in range(2):
            b_scr, dq_scr, dk_scr, dv_scr, dg_scr, db_scr = [s.at[jh] for s in half_scr]
            lbv, qp, sq, sig, f, k = _hgrn_gates(proj_ref, lb_ref, jh)
            q = qp * sq
            b_scr[...] = _tri_dot(l_ref[...], jnp.log(f))
            v = blocks[4 + jh]
            gp = blocks[6 + jh]
            sg = _sigmoid_t(gp)
            for ci in reversed(range(nc)):
                r0 = HG_CHUNK * ci
                rs = slice(r0, r0 + HG_CHUNK)
                for hh in range(4):
                    hd = 4 * jh + hh
                    cs = slice(HG_HEAD * hh, HG_HEAD * (hh + 1))
                    hs = slice(HG_HEAD * hd, HG_HEAD * (hd + 1))
                    oc = o_ref[rs, hs]
                    r = lax.rsqrt(_rowmean(oc * oc) + EPS)
                    oh = oc * r
                    gc, sgc = gp[rs, cs], sg[rs, cs]
                    dogc = dog_ref[rs, hs].astype(F32)
                    don = dogc * (gc * sgc)
                    dg_scr[rs, cs] = dogc * (oh * onv) * (sgc * (1.0 + gc * (1.0 - sgc)))
                    don_acc += _colsum(don * oh)
                    donh = don * onv
                    do = (r * (donh - oh * _rowmean(donh * oh))).astype(BF16)
                    bc = b_scr[rs, cs]
                    bm = b_scr[r0 + HG_CHUNK // 2 - 1:r0 + HG_CHUNK // 2, cs]
                    bl = b_scr[r0 + HG_CHUNK - 1:r0 + HG_CHUNK, cs]
                    qc, kc, vc = q[rs, cs], k[rs, cs], v[rs, cs].astype(BF16)
                    e_q, e_k = jnp.exp(bc - bm), jnp.exp(bm - bc)
                    e_b, e_d = e_q * jnp.exp(bm), e_k * jnp.exp(bl - bm)
                    qe = (qc * e_b).astype(BF16)
                    qt = (qc * e_q).astype(BF16)
                    kt = (kc * e_k).astype(BF16)
                    kd = (kc * e_d).astype(BF16)
                    stb = st_ref[ci, hd]
                    dst = ds_scr[hd]
                    dstb = dst.astype(BF16)
                    a_t = jnp.where(causal_t, _dot(kt, qt, NT), 0.0).astype(BF16)
                    da = jnp.where(causal, _dot(do, vc, NT), 0.0).astype(BF16)
                    da_t = jnp.where(causal_t, _dot(vc, do, NT), 0.0).astype(BF16)
                    dv_scr[rs, cs] = _dot(a_t, do) + _dot(kd, dstb, NT)
                    dqe, dqt = _dot(do, stb), _dot(da, kt)
                    dkt, dkd = _dot(da_t, qt), _dot(vc, dstb)
                    dq_scr[rs, cs] = dqe * e_b + dqt * e_q
                    dk_scr[rs, cs] = dkt * e_k + dkd * e_d
                    e_l = jnp.exp(bl)
                    s_end = stb.astype(F32) * e_l + _dot(vc, kd, TN)
                    dbc = (qe.astype(F32) * dqe + qt.astype(F32) * dqt
                           - kt.astype(F32) * dkt - kd.astype(F32) * dkd)
                    db_scr[rs, cs] = dbc + jnp.where(last_row, _colsum(dstb.astype(F32) * s_end), 0.0)
                    ds_scr[hd] = dst * e_l + _dot(do, qe, TN)
            dq = dq_scr[...]
            dk = dk_scr[...]
            cols = slice(512 * jh, 512 * (jh + 1))
            dlogf = _tri_dot(u_ref[...], db_scr[...])
            one_m_sig = 1.0 - sig
            dsig = (1.0 - lbv) * sig * one_m_sig
            dboth = dlogf / f - dk
            dproj_ref[jh] = (dq * (sq * (1.0 + qp * (1.0 - sq)))).astype(BF16)
            dproj_ref[2 + jh] = (dboth * dsig).astype(BF16)
            dproj_ref[4 + jh] = dv_scr[...].astype(BF16)
            dproj_ref[6 + jh] = dg_scr[...].astype(BF16)
            dlb_ref[:, cols] += _colsum(dboth * one_m_sig)
        don_ref[...] += don_acc

    rev = lambda i: nb - 1 - i
    return pl.pallas_call(
        body, name=name, grid=(nb,),
        out_shape=(jax.ShapeDtypeStruct((T, 4 * D_MODEL), BF16), jax.ShapeDtypeStruct((1, D_MODEL), F32),
                   jax.ShapeDtypeStruct((1, HG_HEAD), F32)),
        in_specs=[pl.BlockSpec((tb, 4 * D_MODEL), lambda i: (rev(i), 0)),
                  pl.BlockSpec((tb, D_MODEL), lambda i: (rev(i), 0)),
                  pl.BlockSpec((tb, D_MODEL), lambda i: (rev(i), 0)),
                  pl.BlockSpec((nc, HG_HEADS, HG_HEAD, HG_HEAD), lambda i: (rev(i), 0, 0, 0)),
                  pl.BlockSpec((1, D_MODEL), lambda i: (0, 0)),
                  pl.BlockSpec((1, HG_HEAD), lambda i: (0, 0)),
                  pl.BlockSpec(lmat.shape, lambda i: (0, 0)),
                  pl.BlockSpec(lmat.shape, lambda i: (0, 0))],
        out_specs=(pl.BlockSpec((tb, 4 * D_MODEL), lambda i: (rev(i), 0)),
                   pl.BlockSpec((1, D_MODEL), lambda i: (0, 0)),
                   pl.BlockSpec((1, HG_HEAD), lambda i: (0, 0))),
        scratch_shapes=[pltpu.VMEM((HG_HEADS, HG_HEAD, HG_HEAD), F32)] + [pltpu.VMEM((2, tb, 512), F32)] * 6,
        compiler_params=_params("arbitrary"),
    )(proj, o, dog, states, lb, out_norm, lmat, umat)


def _gm_norm(pre_ref, lg_ref, lbias_ref):
    pre_ref = _ColBlocks(pre_ref, 768)
    vs = [pre_ref[4 + j].astype(F32) for j in range(4)]
    width = 4 * vs[0].shape[1]
    mu = sum(jnp.sum(v, axis=1, keepdims=True) for v in vs) / width
    ds = [v - mu for v in vs]
    var = sum(jnp.sum(d * d, axis=1, keepdims=True) for d in ds) / width
    rstd = lax.rsqrt(var + EPS)
    vhat = [d * rstd for d in ds]
    vn = [vhat[j] * lg_ref[j:j + 1, :] + lbias_ref[j:j + 1, :] for j in range(4)]
    return vhat, vn, rstd


def _gm_spatial_fwd(pre, ln_g, ln_b, ws, bsb, name, *, tb=256):
    T = pre.shape[0]
    tb = min(tb, T)
    nc = tb // GM_CHUNK

    def body(pre_ref, lg_ref, lbias_ref, ws_ref, bs_ref, o_ref):
        _, vn, _ = _gm_norm(pre_ref, lg_ref, lbias_ref)
        pre_ref, o_ref = _ColBlocks(pre_ref, 768), _ColBlocks(o_ref, 768)
        for j in range(4):
            u = pre_ref[j].astype(F32)
            for e in range(2):
                g = 2 * j + e
                cs = slice(GM_GDIM * e, GM_GDIM * (e + 1))
                wg = ws_ref[g].astype(BF16)
                for ci in range(nc):
                    rs = slice(GM_CHUNK * ci, GM_CHUNK * (ci + 1))
                    vm = _dot(wg, vn[j][rs, cs].astype(BF16)) + bs_ref[g]
                    o_ref[j, rs, cs] = (u[rs, cs] * vm).astype(BF16)

    return pl.pallas_call(
        body, name=name, grid=(T // tb,),
        out_shape=jax.ShapeDtypeStruct((T, 4 * 768), BF16),
        in_specs=[pl.BlockSpec((tb, 8 * 768), lambda i: (i, 0)),
                  pl.BlockSpec((4, 768), lambda i: (0, 0)),
                  pl.BlockSpec((4, 768), lambda i: (0, 0)),
                  pl.BlockSpec((GM_GROUPS, GM_CHUNK, GM_CHUNK), lambda i: (0, 0, 0)),
                  pl.BlockSpec((GM_GROUPS, GM_CHUNK, GM_GDIM), lambda i: (0, 0, 0))],
        out_specs=pl.BlockSpec((tb, 4 * 768), lambda i: (i, 0)),
        compiler_params=_params("parallel"),
    )(pre, ln_g, ln_b, ws, bsb)


def _gm_spatial_bwd(pre, gp, dm, ln_g, ln_b, ws, ws_t, bsb, name, *, tb=256):
    T = pre.shape[0]
    tb = min(tb, T)
    nc = tb // GM_CHUNK
    nb = T // tb

    def body(pre_ref, gp_ref, dm_ref, lg_ref, lbias_ref, ws_ref, wst_ref, bs_ref,
             dpre_ref, dws_ref, dbs_ref, dlg_ref, dlb_ref, dbin_ref, dbs_scr, dvn_scr, du_scr):
        i = pl.program_id(0)

        @pl.when(i == 0)
        def _():
            dws_ref[...] = jnp.zeros_like(dws_ref)
            dbs_scr[...] = jnp.zeros_like(dbs_scr)
            dlg_ref[...] = jnp.zeros_like(dlg_ref)
            dlb_ref[...] = jnp.zeros_like(dlb_ref)
            dbin_ref[...] = jnp.zeros_like(dbin_ref)

        vhat, vn, rstd = _gm_norm(pre_ref, lg_ref, lbias_ref)
        pre_ref, gp_ref, dm_ref = _ColBlocks(pre_ref, 768), _ColBlocks(gp_ref, 768), _ColBlocks(dm_ref, 768)
        dpre_ref = _ColBlocks(dpre_ref, 768)
        for j in range(4):
            u = pre_ref[j].astype(F32)
            for e in range(2):
                g = 2 * j + e
                cs = slice(GM_GDIM * e, GM_GDIM * (e + 1))
                wg = ws_ref[g].astype(BF16)
                wgt = wst_ref[g].astype(BF16)
                for ci in range(nc):
                    rs = slice(GM_CHUNK * ci, GM_CHUNK * (ci + 1))
                    vnb = vn[j][rs, cs].astype(BF16)
                    vm = _dot(wg, vnb) + bs_ref[g]
                    dmg = dm_ref[j, rs, cs].astype(F32)
                    du_scr[j, rs, cs] = dmg * vm
                    dvm = dmg * u[rs, cs]
                    dvmb = dvm.astype(BF16)
                    dws_ref[g] += _dot(dvmb, vnb, NT)
                    dbs_scr[g] += dvm
                    dvn_scr[j, rs, cs] = _dot(wgt, dvmb)
        width = 4 * 768
        dvh = []
        for j in range(4):
            dvn = dvn_scr[j]
            dlg_ref[j:j + 1, :] += _colsum(dvn * vhat[j])
            dlb_ref[j:j + 1, :] += _colsum(dvn)
            dvh.append(dvn * lg_ref[j:j + 1, :])
        m1 = sum(jnp.sum(d, axis=1, keepdims=True) for d in dvh) / width
        m2 = sum(jnp.sum(dvh[j] * vhat[j], axis=1, keepdims=True) for j in range(4)) / width
        for j in range(4):
            dv = rstd * (dvh[j] - m1 - vhat[j] * m2)
            dpv = dv * gp_ref[4 + j].astype(F32)
            dpu = du_scr[j] * gp_ref[j].astype(F32)
            dpre_ref[4 + j] = dpv.astype(BF16)
            dpre_ref[j] = dpu.astype(BF16)
            dbin_ref[4 + j:5 + j, :] += _colsum(dpv)
            dbin_ref[j:j + 1, :] += _colsum(dpu)

        @pl.when(i == nb - 1)
        def _():
            r_i = lax.broadcasted_iota(jnp.int32, (GM_CHUNK, GM_CHUNK), 0)
            c_i = lax.broadcasted_iota(jnp.int32, (GM_CHUNK, GM_CHUNK), 1)
            for g in range(GM_GROUPS):
                dws_ref[g] = jnp.where(c_i <= r_i, dws_ref[g], 0.0)
                dbs_ref[g] = jnp.broadcast_to(jnp.sum(dbs_scr[g], axis=1, keepdims=True), (GM_CHUNK, GM_CHUNK))

    sq = pl.BlockSpec((GM_GROUPS, GM_CHUNK, GM_CHUNK), lambda i: (0, 0, 0))
    v4 = pl.BlockSpec((4, 768), lambda i: (0, 0))
    return pl.pallas_call(
        body, name=name, grid=(nb,),
        out_shape=(jax.ShapeDtypeStruct((T, 8 * 768), BF16),
                   jax.ShapeDtypeStruct((GM_GROUPS, GM_CHUNK, GM_CHUNK), F32),
                   jax.ShapeDtypeStruct((GM_GROUPS, GM_CHUNK, GM_CHUNK), F32),
                   jax.ShapeDtypeStruct((4, 768), F32), jax.ShapeDtypeStruct((4, 768), F32),
                   jax.ShapeDtypeStruct((8, 768), F32)),
        in_specs=[pl.BlockSpec((tb, 8 * 768), lambda i: (i, 0)),
                  pl.BlockSpec((tb, 8 * 768), lambda i: (i, 0)),
                  pl.BlockSpec((tb, 4 * 768), lambda i: (i, 0)),
                  v4, v4, sq, sq,
                  pl.BlockSpec((GM_GROUPS, GM_CHUNK, GM_GDIM), lambda i: (0, 0, 0))],
        out_specs=(pl.BlockSpec((tb, 8 * 768), lambda i: (i, 0)), sq, sq, v4, v4,
                   pl.BlockSpec((8, 768), lambda i: (0, 0))),
        scratch_shapes=[pltpu.VMEM((GM_GROUPS, GM_CHUNK, GM_GDIM), F32),
                        pltpu.VMEM((4, tb, 768), F32), pltpu.VMEM((4, tb, 768), F32)],
        compiler_params=_params("arbitrary"),
    )(pre, gp, dm, ln_g, ln_b, ws, ws_t, bsb)


def _adamw(slots, w, m, v, name, *, tr=256):
    S, R, C = slots.shape
    tr = next((t for t in (tr, tr // 2, tr // 4, tr // 8, tr // 16) if R % t == 0), R) if R > tr else R
    bc1 = 1.0 - ADAM_B1 ** ADAM_STEP
    bc2 = 1.0 - ADAM_B2 ** ADAM_STEP

    def body(s_ref, w_ref, m_ref, v_ref, g_ref, d_ref, nm_ref, nv_ref):
        g = s_ref[0].astype(F32)
        for s in range(1, S):
            g = g + s_ref[s].astype(F32)
        mn = ADAM_B1 * m_ref[...] + (1.0 - ADAM_B1) * g
        vn = ADAM_B2 * v_ref[...] + (1.0 - ADAM_B2) * (g * g)
        g_ref[...] = g
        nm_ref[...] = mn
        nv_ref[...] = vn
        d_ref[...] = -ADAM_LR * ((mn / bc1) / (jnp.sqrt(vn / bc2) + ADAM_EPS) + ADAM_WD * w_ref[...])

    spec = pl.BlockSpec((tr, C), lambda i: (i, 0))
    return pl.pallas_call(
        body, name=name, grid=(R // tr,),
        out_shape=(jax.ShapeDtypeStruct((R, C), F32),) * 4,
        in_specs=[pl.BlockSpec((S, tr, C), lambda i: (0, i, 0)), spec, spec, spec],
        out_specs=(spec,) * 4,
        compiler_params=_params("parallel"),
    )(slots, w, m, v)


def _update(slots, w, m, v, name):
    shp = w.shape
    C = shp[-1]
    R = math.prod(shp[:-1])
    outs = _adamw(slots.reshape(slots.shape[0], R, C), w.reshape(R, C), m.reshape(R, C), v.reshape(R, C), name)
    return tuple(o.reshape(shp) for o in outs)


def kernel(x, c, ada_w, ada_b, norm_pre, norm_post, ffn_w_in, ffn_w_out, hg_w_in, hg_w_out, hg_out_norm, hg_lb, gm_w_in, gm_b_in, gm_ln_g, gm_ln_b, gm_w_s, gm_b_s, gm_w_out, loss_target, m_ada_w, m_ada_b, m_norm_pre, m_norm_post, m_ffn_w_in, m_ffn_w_out, m_hg_w_in, m_hg_w_out, m_hg_out_norm, m_hg_lb, m_gm_w_in, m_gm_b_in, m_gm_ln_g, m_gm_ln_b, m_gm_w_s, m_gm_b_s, m_gm_w_out, v_ada_w, v_ada_b, v_norm_pre, v_norm_post, v_ffn_w_in, v_ffn_w_out, v_hg_w_in, v_hg_w_out, v_hg_out_norm, v_hg_lb, v_gm_w_in, v_gm_b_in, v_gm_ln_g, v_gm_ln_b, v_gm_w_s, v_gm_b_s, v_gm_w_out):
    me = 4 * lax.axis_index("x") + 2 * lax.axis_index("y") + lax.axis_index("c")
    T = x.shape[1]
    x0 = x.reshape(T, D_MODEL)
    target = loss_target.reshape(T, D_MODEL)
    n_ada = ada_w.shape[-1]

    pack = jnp.concatenate([
        c.reshape(8, 128), norm_pre.reshape(6, 128), norm_post.reshape(6, 128),
        gm_b_in.reshape(6, 128), gm_ln_g.reshape(3, 128), gm_ln_b.reshape(3, 128)], axis=0)
    packs = _all_gather(pack, "gather_small")
    c_all = packs[:, 0:8].reshape(NDEV, D_MODEL)
    npre = packs[:, 8:14].reshape(NDEV, 2, 3, 128).transpose(1, 2, 0, 3).reshape(2, 3, D_MODEL)
    npost = packs[:, 14:20].reshape(NDEV, 2, 3, 128).transpose(1, 2, 0, 3).reshape(2, 3, D_MODEL)
    b_in = packs[:, 20:26].reshape(1, NDEV * 768)
    ln_g = packs[:, 26:29].reshape(4, 768)
    ln_b = packs[:, 29:32].reshape(4, 768)

    ada_b_mine = lax.dynamic_slice_in_dim(ada_b, me * n_ada, n_ada, axis=1).reshape(2, 1, n_ada)
    mod_cols = _ada_fwd(c_all, ada_w, ada_b_mine, "ada_fwd")
    mod_all = _all_gather(mod_cols, "gather_mod")
    mod = lax.dynamic_index_in_dim(mod_all, me, axis=2, keepdims=False)
    mod = mod.transpose(1, 0, 2).reshape(2, 9, 1, D_MODEL)

    sh_fi, sh_fo = ffn_w_in.astype(BF16).swapaxes(-1, -2), ffn_w_out.astype(BF16)
    sh_hi, sh_ho = hg_w_in[0].astype(BF16).T, hg_w_out[0].astype(BF16)
    sh_mi, sh_mo = gm_w_in[0].astype(BF16).T, gm_w_out[0].astype(BF16)

    def whole(gathered):
        return gathered.reshape(-1, D_MODEL)

    w_fi = {(0, 0): whole(_all_gather(sh_fi[0, 0], "gather_ffn_in_first"))}
    w_fo = {}
    riders = {"l0s0": [sh_fo[0, 0], sh_hi, sh_ho], "l0s1": [sh_fi[0, 1], sh_fo[0, 1]], "hg_mix": [sh_mi, sh_mo],
              "l0s2": [sh_fi[1, 0], sh_fo[1, 0]], "l1s0": [sh_fi[1, 1], sh_fo[1, 1]]}

    sm = jax.nn.softmax(hg_lb, axis=0)
    lb0 = sm[0:1]
    on = hg_out_norm.reshape(1, HG_HEAD)
    tril = jnp.tril(jnp.ones((GM_CHUNK, GM_CHUNK), F32))
    ws = gm_w_s[0] * tril[None]
    ws_t = ws.transpose(0, 2, 1)
    bsb = jnp.broadcast_to(gm_b_s[0][:, :, None], (GM_GROUPS, GM_CHUNK, GM_GDIM))

    res_ws = (0.5, 1.0, 0.5)

    def vecs(i, s):
        return (npre[i, s].reshape(1, D_MODEL), npost[i, s].reshape(1, D_MODEL),
                mod[i, 3 * s], mod[i, 3 * s + 1], mod[i, 3 * s + 2])

    order = [(i, s) for i in range(2) for s in range(3)]
    saved = {}
    xs = x0
    for pos, (i, s) in enumerate(order):
        tag = f"l{i}s{s}"
        pre_g, post_g, shift, scale, gate = vecs(i, s)
        rider = _Exchange("gather", riders[tag]) if tag in riders else None
        if s != 1:
            if pos == 0:
                (pq, a, h), got = _ffn_in(xs, w_fi[0, 0], "ffn_in_" + tag, opening=(pre_g, scale, shift), exchange=rider)
            else:
                (pq, a), got = _ffn_in(h, w_fi[i, s // 2], "ffn_in_" + tag, exchange=rider)
            extra = (pq, a)
            if tag == "l0s0":
                w_fo[0, 0], w_hi, w_ho = map(whole, got)
            elif tag == "l0s2":
                w_fi[1, 0], w_fo[1, 0] = map(whole, got)
            elif tag == "l1s0":
                w_fi[1, 1], w_fo[1, 1] = map(whole, got)
            wo = w_fo[i, s // 2]
        elif i == 0:
            proj, *got = _mm_blocks(h, w_hi, "hg_in", out_dtype=F32, exchange=rider)
            w_fi[0, 1], w_fo[0, 1] = map(whole, got)
            o, og, states, got = _hgrn_fwd(proj, lb0, on, "hg_mix", exchange=_Exchange("gather", riders["hg_mix"]))
            w_mi, w_mo = map(whole, got)
            a, wo = og, w_ho
            extra = (proj, o, og, states)
        else:
            pre, gp = _mm_blocks(h, w_mi, "gm_in", bias=b_in, gelu=True)
            a = _gm_spatial_fwd(pre, ln_g, ln_b, ws, bsb, "gm_mix")
            wo = w_mo
            extra = (pre, gp, a)
        if pos + 1 < len(order):
            npre_g, _, nshift, nscale, _ = vecs(*order[pos + 1])
            y, x_next, h_next = _out_proj(a, wo, xs, post_g, gate, res_ws[s], (npre_g, nscale, nshift), "out_" + tag)
            saved[tag] = (xs, h, y) + extra
            xs, h = x_next, h_next
        else:
            dx, dy, dgate, dpost, loss_part = _out_proj_last(a, wo, xs, post_g, gate, res_ws[s], target, "out_" + tag)
            saved[tag] = (xs, h, None) + extra
    loss = lax.psum(loss_part[0, 0], ("x", "y", "c"))

    slots = {}
    d_npre = [[None] * 3, [None] * 3]
    d_npost = [[None] * 3, [None] * 3]
    d_mod = [[None] * 9, [None] * 9]
    for pos in reversed(range(len(order))):
        i, s = order[pos]
        tag = f"l{i}s{s}"
        pre_g, _, _, scale, _ = vecs(i, s)
        xin, h = saved[tag][:2]
        if s != 1:
            w_in, wo = w_fi[i, s // 2], w_fo[i, s // 2]
            pq, g = saved[tag][3:]
            dz = _ffn_dgate(dy, wo, pq, "ffn_dgate_" + tag)
            g_out = _mm_wgrad(g, dy, "ffn_out_wgrad_" + tag, xw=1408)
            g_in = _mm_wgrad(dz, h, "ffn_in_wgrad_" + tag, xw=1408)
        elif i == 0:
            proj, o, og, states = saved[tag][3:]
            dog = _mm_blocks(dy, w_ho, "hg_out_dgrad")
            g_out = _mm_wgrad(og, dy, "hg_out_wgrad")
            dz, d_lb0, d_on = _hgrn_bwd(proj, o, dog, states, lb0, on, "hg_mix_bwd")
            w_in = w_hi
            g_in = _mm_wgrad(h, dz, "hg_in_wgrad", yw=512)
        else:
            pre, gp, sp = saved[tag][3:]
            dm = _mm_blocks(dy, w_mo, "gm_out_dgrad")
            g_out = _mm_wgrad(sp, dy, "gm_out_wgrad", xw=768)
            dz, d_ws, d_bs, d_lg, d_lbias, d_bin = _gm_spatial_bwd(pre, gp, dm, ln_g, ln_b, ws, ws_t, bsb, "gm_mix_bwd")
            w_in = w_mi
            g_in = _mm_wgrad(h, dz, "gm_in_wgrad", yw=768)
        g_out = g_out.reshape(NDEV, -1, D_MODEL)
        g_in = g_in.reshape(NDEV, -1, D_MODEL) if s != 1 else g_in
        d_npost[i][s] = dpost
        d_mod[i][3 * s + 2] = dgate
        rider = _Exchange("scatter", [g_in, g_out])
        if pos > 0:
            pi, ps = order[pos - 1]
            _, ppost_g, _, _, pgate = vecs(pi, ps)
            prev = (saved[f"l{pi}s{ps}"][2], ppost_g, pgate, res_ws[ps])
            dx, dshift, dscale, dpre_g, dy, dgate, dpost, r_in, r_out = _in_grad(
                dz, w_in, dx, xin, pre_g, scale, prev, "in_grad_" + tag, exchange=rider)
        else:
            dx, dshift, dscale, dpre_g, r_in, r_out = _in_grad(
                dz, w_in, dx, xin, pre_g, scale, None, "in_grad_" + tag, exchange=rider)
        slots[tag] = (r_in, r_out)
        d_npre[i][s] = dpre_g
        d_mod[i][3 * s], d_mod[i][3 * s + 1] = dshift, dscale
    grad_x = dx.reshape(x.shape)

    ffn_tags = ["l0s0", "l0s2", "l1s0", "l1s2"]
    s_fi = jnp.stack([slots[t][0] for t in ffn_tags], axis=1).swapaxes(-1, -2)
    s_fo = jnp.stack([slots[t][1] for t in ffn_tags], axis=1)
    (s_hi, s_ho), (s_mi, s_mo) = slots["l0s1"], slots["l1s1"]
    s_hi, s_ho, s_mi, s_mo = s_hi[:, None], s_ho[:, None], s_mi[:, None], s_mo[:, None]

    gmod = jnp.stack([jnp.concatenate(d_mod[i], axis=0) for i in range(2)])
    d_sm = lb0 * d_lb0
    d_hg_lb = jnp.concatenate([d_sm, jnp.zeros((2, D_MODEL), F32)], axis=0) - sm * d_sm
    small = [gmod, jnp.stack([jnp.concatenate(r, axis=0) for r in d_npre]),
             jnp.stack([jnp.concatenate(r, axis=0) for r in d_npost]),
             d_on, d_hg_lb, d_bin, d_lg, d_lbias, d_ws, d_bs[:, :, 0]]
    sizes = [a.size for a in small]
    flat = jnp.concatenate([a.reshape(-1) for a in small])
    rows = -(-flat.size // (8 * 128)) * 8
    flat = jnp.pad(flat, (0, rows * 128 - flat.size)).reshape(rows, 128)
    flats = _all_gather(flat, "gather_small_grads").reshape(NDEV, rows * 128)
    parts, off = [], 0
    for a, n in zip(small, sizes):
        parts.append(flats[:, off:off + n].reshape((NDEV,) + a.shape))
        off += n
    p_mod, p_npre, p_npost, p_on, p_lb, p_bin, p_lg, p_lbias, p_ws, p_bs = parts

    def mine(p, width):
        return lax.dynamic_slice_in_dim(p, me * width, width, axis=p.ndim - 1)

    gmod_cols = mine(p_mod.reshape(NDEV, 2, 9 * D_MODEL), n_ada).transpose(1, 0, 2)
    g_ada_w = _ada_bwd(jnp.pad(c_all.T, ((0, 0), (0, 120))), jnp.pad(gmod_cols, ((0, 0), (0, 120), (0, 0))), "ada_bwd")

    out = {}
    out["ada_w"] = _update(g_ada_w[None], ada_w, m_ada_w, v_ada_w, "adamw_ada_w")
    out["ada_b"] = _update(p_mod.reshape(NDEV, 2, 9 * D_MODEL), ada_b, m_ada_b, v_ada_b, "adamw_ada_b")
    out["norm_pre"] = _update(mine(p_npre, 128), norm_pre, m_norm_pre, v_norm_pre, "adamw_norm_pre")
    out["norm_post"] = _update(mine(p_npost, 128), norm_post, m_norm_post, v_norm_post, "adamw_norm_post")
    out["ffn_w_in"] = _update(s_fi.reshape((NDEV,) + ffn_w_in.shape), ffn_w_in, m_ffn_w_in, v_ffn_w_in, "adamw_ffn_in")
    out["ffn_w_out"] = _update(s_fo.reshape((NDEV,) + ffn_w_out.shape), ffn_w_out, m_ffn_w_out, v_ffn_w_out, "adamw_ffn_out")
    out["hg_w_in"] = _update(s_hi, hg_w_in, m_hg_w_in, v_hg_w_in, "adamw_hg_in")
    out["hg_w_out"] = _update(s_ho, hg_w_out, m_hg_w_out, v_hg_w_out, "adamw_hg_out")
    out["hg_out_norm"] = _update(p_on, hg_out_norm, m_hg_out_norm, v_hg_out_norm, "adamw_hg_norm")
    out["hg_lb"] = _update(p_lb, hg_lb, m_hg_lb, v_hg_lb, "adamw_hg_lb")
    out["gm_w_in"] = _update(s_mi, gm_w_in, m_gm_w_in, v_gm_w_in, "adamw_gm_in")
    out["gm_b_in"] = _update(mine(p_bin.reshape(NDEV, 1, 8 * 768), 768), gm_b_in, m_gm_b_in, v_gm_b_in, "adamw_gm_b_in")
    out["gm_ln_g"] = _update(mine(p_lg.reshape(NDEV, 1, 4 * 768), 384), gm_ln_g, m_gm_ln_g, v_gm_ln_g, "adamw_gm_ln_g")
    out["gm_ln_b"] = _update(mine(p_lbias.reshape(NDEV, 1, 4 * 768), 384), gm_ln_b, m_gm_ln_b, v_gm_ln_b, "adamw_gm_ln_b")
    out["gm_w_s"] = _update(p_ws[:, None], gm_w_s, m_gm_w_s, v_gm_w_s, "adamw_gm_w_s")
    out["gm_b_s"] = _update(p_bs[:, None], gm_b_s, m_gm_b_s, v_gm_b_s, "adamw_gm_b_s")
    out["gm_w_out"] = _update(s_mo, gm_w_out, m_gm_w_out, v_gm_w_out, "adamw_gm_out")

    names = ["ada_w", "ada_b", "norm_pre", "norm_post", "ffn_w_in", "ffn_w_out", "hg_w_in", "hg_w_out",
             "hg_out_norm", "hg_lb", "gm_w_in", "gm_b_in", "gm_ln_g", "gm_ln_b", "gm_w_s", "gm_b_s", "gm_w_out"]
    return (loss, grad_x, *[out[n][0] for n in names], *[out[n][1] for n in names],
            *[out[n][2] for n in names], *[out[n][3] for n in names])
```

```python
import math

import jax
import jax.numpy as jnp
from jax import lax
from jax.experimental import pallas as pl
from jax.experimental.pallas import tpu as pltpu

F32 = jnp.float32
BF16 = jnp.bfloat16
NDEV = 8
D_MODEL = 1024
EPS = 1e-6
HG_CHUNK = 64
HG_HEAD = 128
HG_HEADS = 8
GM_CHUNK = 128
GM_GDIM = 384
GM_GROUPS = 8
ADAM_LR = 0.001
ADAM_B1 = 0.9
ADAM_B2 = 0.999
ADAM_EPS = 1e-08
ADAM_WD = 0.01
ADAM_STEP = 10
VMEM_LIMIT = 56 * 2 ** 20

NN = (((1,), (0,)), ((), ()))
NT = (((1,), (1,)), ((), ()))
TN = (((0,), (0,)), ((), ()))
MESH = pl.DeviceIdType.MESH
ANY = pl.BlockSpec(memory_space=pl.ANY)


def _dot(a, b, dims=NN, precision=None):
    return lax.dot_general(a, b, dims, preferred_element_type=F32, precision=precision)


def _params(*sem):
    return pltpu.CompilerParams(dimension_semantics=sem, vmem_limit_bytes=VMEM_LIMIT)


def _sigmoid(x):
    return 1.0 / (1.0 + jnp.exp(-x))


def _sigmoid_t(x):
    return 0.5 * jnp.tanh(0.5 * x) + 0.5


def _gelu_parts(x):
    c = math.sqrt(2.0 / math.pi)
    m = (c * 0.044715) * (x * x)
    t = jnp.tanh(x * (c + m))
    return c, m, t, 0.5 + 0.5 * t


def _gelu(x):
    return x * _gelu_parts(x)[3]


def _gelu_grad(x):
    c, m, t, hp = _gelu_parts(x)
    return hp * (1.0 + (x * (1.0 - t)) * (c + 3.0 * m))


def _colsum(x):
    return jnp.sum(x, axis=0, keepdims=True)


def _rowmean(x):
    return jnp.mean(x, axis=-1, keepdims=True)


def _all_gather(shard, name):
    def body(x_ref, out_ref, send_sems, recv_sems, local_sem):
        x, y, c = lax.axis_index("x"), lax.axis_index("y"), lax.axis_index("c")
        me, sibling = (x, y, c), (x, y, 1 - c)
        chips = [(1 - x, y), (x, 1 - y), (1 - x, 1 - y)]

        def slot(p):
            return out_ref.at[4 * p[0] + 2 * p[1] + p[2]]

        def copy(k, block, to, src=None):
            return pltpu.make_async_remote_copy(
                src_ref=slot(block) if src is None else src, dst_ref=slot(block),
                send_sem=send_sems.at[k], recv_sem=recv_sems.at[k],
                device_id=to, device_id_type=MESH)

        mine = pltpu.make_async_copy(x_ref, slot(me), local_sem)
        mine.start()
        first = [copy(0, me, sibling, src=x_ref)]
        first += [copy(1 + j, me, (*chip, c), src=x_ref) for j, chip in enumerate(chips)]
        for cp in first:
            cp.start()
        passed = [copy(4 + j, (*chip, c), sibling) for j, chip in enumerate(chips)]
        for j, chip in enumerate(chips):
            copy(1 + j, (*chip, c), me).wait_recv()
            passed[j].start()
        copy(0, sibling, me).wait_recv()
        for j, chip in enumerate(chips):
            copy(4 + j, (*chip, 1 - c), me).wait_recv()
        for cp in first + passed:
            cp.wait_send()
        mine.wait()

    return pl.pallas_call(
        body, name=name,
        out_shape=jax.ShapeDtypeStruct((NDEV,) + shard.shape, shard.dtype),
        in_specs=[ANY], out_specs=ANY,
        scratch_shapes=[pltpu.SemaphoreType.DMA((7,)), pltpu.SemaphoreType.DMA((7,)),
                        pltpu.SemaphoreType.DMA(())],
    )(shard)


class _Exchange:
    def __init__(self, kind, arrays):
        self.gather = kind == "gather"
        self.arrays = list(arrays)
        self.n = n = len(self.arrays)
        self.out_shape = [jax.ShapeDtypeStruct(((NDEV,) + a.shape) if self.gather else a.shape, a.dtype)
                          for a in self.arrays]
        self.scratch = [pltpu.SemaphoreType.DMA((n, NDEV - 1)), pltpu.SemaphoreType.DMA((n, NDEV - 1)),
                        pltpu.SemaphoreType.DMA((n,))]

    def _copies(self, in_refs, out_refs, sems):
        send_sems, recv_sems, local_sems = sems
        x, y, c = lax.axis_index("x"), lax.axis_index("y"), lax.axis_index("c")
        me = 4 * x + 2 * y + c
        peers = [(1 - x if k & 4 else x, 1 - y if k & 2 else y, 1 - c if k & 1 else c) for k in range(1, NDEV)]
        local, send, recv = [], [], []
        for a in range(self.n):
            src = (lambda pid, a=a: in_refs[a]) if self.gather else (lambda pid, a=a: in_refs[a].at[pid])
            local.append(pltpu.make_async_copy(src(me), out_refs[a].at[me], local_sems.at[a]))
            for k, p in enumerate(peers):
                pid = 4 * p[0] + 2 * p[1] + p[2]
                for lst, slot in ((send, me), (recv, pid)):
                    lst.append(pltpu.make_async_remote_copy(
                        src_ref=src(pid), dst_ref=out_refs[a].at[slot],
                        send_sem=send_sems.at[a, k], recv_sem=recv_sems.at[a, k],
                        device_id=p, device_id_type=MESH))
        return local, send, recv

    def start(self, first, in_refs, out_refs, sems):
        @pl.when(first)
        def _():
            local, send, _ = self._copies(in_refs, out_refs, sems)
            for cp in local + send:
                cp.start()

    def finish(self, last, in_refs, out_refs, sems):
        @pl.when(last)
        def _():
            local, send, recv = self._copies(in_refs, out_refs, sems)
            for cp in send:
                cp.wait_send()
            for cp in recv:
                cp.wait_recv()
            for cp in local:
                cp.wait()


def _host(exchange, n_in, n_out, body, first_last):
    if exchange is None:
        return body, [], [], [], []
    n = exchange.n

    def hosted(*refs):
        ins, refs = refs[:n_in], refs[n_in:]
        xin, refs = refs[:n], refs[n:]
        outs, refs = refs[:n_out], refs[n_out:]
        xout, refs = refs[:n], refs[n:]
        scratch, sems = refs[:len(refs) - 3], refs[len(refs) - 3:]
        first, last = first_last()
        exchange.start(first, xin, xout, sems)
        body(*ins, *outs, *scratch)
        exchange.finish(last, xin, xout, sems)

    return hosted, exchange.arrays, [ANY] * n, exchange.out_shape, exchange.scratch


def _first_last(steps):
    def at():
        i = pl.program_id(0)
        return i == 0, i == steps - 1
    return at


class _ColBlocks:
    def __init__(self, ref, width):
        self.ref, self.width = ref, width

    def _index(self, key):
        key = key if isinstance(key, tuple) else (key,)
        rows = key[1] if len(key) > 1 else slice(None)
        cols = key[2] if len(key) > 2 else slice(0, self.width)
        c0 = key[0] * self.width
        return rows, slice(c0 + cols.start, c0 + cols.stop)

    def __getitem__(self, key):
        return self.ref[self._index(key)]

    def __setitem__(self, key, value):
        self.ref[self._index(key)] = value


def _col_chunks(width, chunk=768):
    return [slice(c, min(c + chunk, width)) for c in range(0, width, chunk)]


def _row_spec(tm, d):
    return pl.BlockSpec((tm, d), lambda m: (m, 0))


def _vec_spec(d):
    return pl.BlockSpec((1, d), lambda m: (0, 0))


def _whole_spec(w):
    nd = w.ndim
    return pl.BlockSpec(w.shape, lambda m: (0,) * nd, pipeline_mode=pl.Buffered(1))


def _mm_blocks(a, w, name, *, bias=None, out_dtype=BF16, tm=512, chunk=768, gelu=False, gelu_grad_of=None,
               exchange=None):
    T, K = a.shape
    C = w.shape[0]
    tm = min(tm, T)
    n_in = 2 + (bias is not None) + (gelu_grad_of is not None)

    def body(*refs):
        a_ref, w_ref = refs[:2]
        av = a_ref[...]
        for cols in _col_chunks(C, chunk):
            r = _dot(av, w_ref[cols], NT)
            if bias is not None:
                r = r + refs[2][:, cols]
            if gelu:
                refs[n_in][:, cols] = _gelu(r).astype(BF16)
                refs[n_in + 1][:, cols] = r.astype(BF16)
            else:
                refs[n_in][:, cols] = r.astype(out_dtype)
        if gelu_grad_of is not None:
            pre_ref, gp_ref = refs[n_in - 1], refs[n_in + 1]
            for cols in _col_chunks(pre_ref.shape[1], chunk):
                gp_ref[:, cols] = _gelu_grad(pre_ref[:, cols].astype(F32)).astype(BF16)

    in_specs = [_row_spec(tm, K), _whole_spec(w)]
    args = [a, w]
    if bias is not None:
        in_specs.append(_whole_spec(bias))
        args.append(bias)
    outs = [jax.ShapeDtypeStruct((T, C), BF16)] * 2 if gelu else [jax.ShapeDtypeStruct((T, C), out_dtype)]
    if gelu_grad_of is not None:
        in_specs.append(_row_spec(tm, gelu_grad_of.shape[1]))
        args.append(gelu_grad_of)
        outs.append(jax.ShapeDtypeStruct(gelu_grad_of.shape, BF16))
    body, x_args, x_in, x_out, x_scratch = _host(exchange, n_in, len(outs), body, _first_last(T // tm))
    res = pl.pallas_call(
        body, name=name, grid=(T // tm,),
        out_shape=outs + x_out,
        in_specs=in_specs + x_in,
        out_specs=[_row_spec(tm, o.shape[1]) for o in outs] + x_in,
        scratch_shapes=x_scratch,
        compiler_params=_params("arbitrary" if exchange else "parallel"),
    )(*args, *x_args)
    return res if len(res) > 1 else res[0]


def _rms(v):
    return lax.rsqrt(_rowmean(v * v) + EPS)


def _zero_at_start(*refs):
    @pl.when(pl.program_id(0) == 0)
    def _():
        for r in refs:
            r[...] = jnp.zeros_like(r)


def _postnorm_bwd_math(dxo, yv, g, gate, res_w, dgate_ref, dpost_ref):
    r = _rms(yv)
    yh = yv * r
    both = res_w * _colsum(dxo * yh)
    dgate_ref[...] += g * both
    dpost_ref[...] += gate * both
    dyh = dxo * (res_w * gate * g)
    return (r * (dyh - yh * _rowmean(dyh * yh))).astype(BF16)


def _out_proj(a, w, x, post_g, gate, res_w, nxt, name, *, tm=512):
    T, ka = a.shape
    d = w.shape[1]
    tm = min(tm, T)

    def body(a_ref, w_ref, x_ref, pg_ref, gate_ref, ng_ref, nsc_ref, nsh_ref, y_ref, xn_ref, h_ref):
        y = _dot(a_ref[...], w_ref[...])
        y_ref[...] = y
        xn = x_ref[...] + (y * _rms(y)) * (res_w * gate_ref[...] * pg_ref[...])
        xn_ref[...] = xn
        h_ref[...] = ((xn * _rms(xn)) * (ng_ref[...] * (1.0 + nsc_ref[...])) + nsh_ref[...]).astype(BF16)

    return pl.pallas_call(
        body, name=name, grid=(T // tm,),
        out_shape=(jax.ShapeDtypeStruct((T, d), F32), jax.ShapeDtypeStruct((T, d), F32),
                   jax.ShapeDtypeStruct((T, d), BF16)),
        in_specs=[_row_spec(tm, ka), _whole_spec(w), _row_spec(tm, d)] + [_vec_spec(d)] * 5,
        out_specs=(_row_spec(tm, d),) * 3,
        compiler_params=_params("parallel"),
    )(a, w, x, post_g, gate, *nxt)


def _out_proj_last(a, w, x, post_g, gate, res_w, target, name, *, tm=512):
    T, ka = a.shape
    d = w.shape[1]
    tm = min(tm, T)

    def body(a_ref, w_ref, x_ref, pg_ref, gate_ref, t_ref, dx_ref, dy_ref, dgate_ref, dpost_ref, l_ref):
        _zero_at_start(dgate_ref, dpost_ref, l_ref)
        y = _dot(a_ref[...], w_ref[...])
        e = x_ref[...] + res_w * gate_ref[...] * (y * _rms(y) * pg_ref[...]) - t_ref[...]
        l_ref[...] += 0.5 * jnp.sum(_rowmean(e * e), axis=0, keepdims=True)
        dx = e * (1.0 / d)
        dx_ref[...] = dx
        dy_ref[...] = _postnorm_bwd_math(dx, y, pg_ref[...], gate_ref[...], res_w, dgate_ref, dpost_ref)

    return pl.pallas_call(
        body, name=name, grid=(T // tm,),
        out_shape=(jax.ShapeDtypeStruct((T, d), F32), jax.ShapeDtypeStruct((T, d), BF16),
                   jax.ShapeDtypeStruct((1, d), F32), jax.ShapeDtypeStruct((1, d), F32),
                   jax.ShapeDtypeStruct((1, 128), F32)),
        in_specs=[_row_spec(tm, ka), _whole_spec(w), _row_spec(tm, d), _vec_spec(d), _vec_spec(d), _row_spec(tm, d)],
        out_specs=(_row_spec(tm, d), _row_spec(tm, d), _vec_spec(d), _vec_spec(d),
                   pl.BlockSpec((1, 128), lambda m: (0, 0))),
        compiler_params=_params("arbitrary"),
    )(a, w, x, post_g, gate, target)


def _in_grad(dz, w, dxo, x, pre_g, scale, prev, name, *, tm=512, exchange=None):
    T, C = dz.shape
    d = w.shape[1]
    tm = min(tm, T)
    has_prev = prev is not None
    res_w = prev[3] if has_prev else None

    def body(*refs):
        dz_ref, w_ref, dxo_ref, x_ref, g_ref, sc_ref = refs[:6]
        if has_prev:
            yp_ref, ppg_ref, pgate_ref, dx_ref, dsh_ref, dsc_ref, dg_ref, dyp_ref, dgate_ref, dpost_ref = refs[6:]
            _zero_at_start(dsh_ref, dsc_ref, dg_ref, dgate_ref, dpost_ref)
        else:
            dx_ref, dsh_ref, dsc_ref, dg_ref = refs[6:]
            _zero_at_start(dsh_ref, dsc_ref, dg_ref)
        dh = _dot(dz_ref[...], w_ref[...])
        xv = x_ref[...]
        r = _rms(xv)
        xh = xv * r
        gain = 1.0 + sc_ref[...]
        both = _colsum(dh * xh)
        dsh_ref[...] += _colsum(dh)
        dsc_ref[...] += g_ref[...] * both
        dg_ref[...] += gain * both
        dxh = dh * (gain * g_ref[...])
        dx = dxo_ref[...] + r * (dxh - xh * _rowmean(dxh * xh))
        dx_ref[...] = dx
        if has_prev:
            dyp_ref[...] = _postnorm_bwd_math(dx, yp_ref[...], ppg_ref[...], pgate_ref[...], res_w,
                                               dgate_ref, dpost_ref)

    vec = jax.ShapeDtypeStruct((1, d), F32)
    in_specs = [_row_spec(tm, C), _whole_spec(w), _row_spec(tm, d), _row_spec(tm, d), _vec_spec(d), _vec_spec(d)]
    out_shape = [jax.ShapeDtypeStruct((T, d), F32), vec, vec, vec]
    out_specs = [_row_spec(tm, d), _vec_spec(d), _vec_spec(d), _vec_spec(d)]
    args = [dz, w, dxo, x, pre_g, scale]
    if has_prev:
        in_specs += [_row_spec(tm, d), _vec_spec(d), _vec_spec(d)]
        out_shape += [jax.ShapeDtypeStruct((T, d), BF16), vec, vec]
        out_specs += [_row_spec(tm, d), _vec_spec(d), _vec_spec(d)]
        args += list(prev[:3])
    body, x_args, x_in, x_out, x_scratch = _host(exchange, len(args), len(out_shape), body, _first_last(T // tm))
    return pl.pallas_call(
        body, name=name, grid=(T // tm,),
        out_shape=out_shape + x_out, in_specs=in_specs + x_in, out_specs=out_specs + x_in,
        scratch_shapes=x_scratch,
        compiler_params=_params("arbitrary"),
    )(*args, *x_args)


def _mm_wgrad(x, y, name, *, xw=None, yw=None, tt=2048):
    T, P = x.shape
    Q = y.shape[1]
    xw, yw = xw or P, yw or Q
    jx, jy = P // xw, Q // yw
    assert jx == 1 or jy == 1
    tt = min(tt, T)
    nt = T // tt

    def body(x_ref, y_ref, o_ref, acc_ref):
        t = pl.program_id(1)

        @pl.when(t == 0)
        def _():
            acc_ref[...] = jnp.zeros_like(acc_ref)

        acc_ref[...] += _dot(x_ref[...], y_ref[...], TN)

        @pl.when(t == nt - 1)
        def _():
            o_ref[...] = acc_ref[...].astype(BF16)

    if jy > 1:
        out_shape = jax.ShapeDtypeStruct((jy, P, yw), BF16)
        out_spec = pl.BlockSpec((None, P, yw), lambda j, t: (j, 0, 0))
    else:
        out_shape = jax.ShapeDtypeStruct((P, Q), BF16)
        out_spec = pl.BlockSpec((xw, Q), lambda j, t: (j, 0))
    return pl.pallas_call(
        body, name=name, grid=(max(jx, jy), nt),
        out_shape=out_shape,
        in_specs=[pl.BlockSpec((tt, xw), (lambda j, t: (t, j)) if jx > 1 else (lambda j, t: (t, 0))),
                  pl.BlockSpec((tt, yw), (lambda j, t: (t, j)) if jy > 1 else (lambda j, t: (t, 0)))],
        out_specs=out_spec,
        scratch_shapes=[pltpu.VMEM((xw, yw), F32)],
        compiler_params=_params("parallel", "arbitrary"),
    )(x, y)


def _ffn_in(h, wt, name, *, tm=512, opening=None, exchange=None):
    T, K = h.shape
    F = wt.shape[0] // 2
    tm = min(tm, T)
    n_vec = 3 if opening else 0

    def body(*refs):
        h_ref, w_ref = refs[0], refs[1 + n_vec]
        pq_ref, g_ref = refs[2 + n_vec], refs[3 + n_vec]
        hh = h_ref[...]
        if opening:
            g_vec, sc_vec, sh_vec = refs[1:4]
            hh = (hh * _rms(hh) * (g_vec[...] * (1.0 + sc_vec[...])) + sh_vec[...]).astype(BF16)
            refs[4 + n_vec][...] = hh
        for cols in _col_chunks(F):
            hi = slice(F + cols.start, F + cols.stop)
            a = _dot(hh, w_ref[cols], NT)
            b = _dot(hh, w_ref[hi], NT)
            s = _sigmoid_t(a)
            silu = a * s
            pq_ref[:, cols] = (b * (s * (1.0 + a * (1.0 - s)))).astype(BF16)
            pq_ref[:, hi] = silu.astype(BF16)
            g_ref[:, cols] = (silu * b).astype(BF16)

    outs = [jax.ShapeDtypeStruct((T, 2 * F), BF16), jax.ShapeDtypeStruct((T, F), BF16)]
    out_specs = [_row_spec(tm, 2 * F), _row_spec(tm, F)]
    if opening:
        outs.append(jax.ShapeDtypeStruct((T, K), BF16))
        out_specs.append(_row_spec(tm, K))
    vecs = list(opening) if opening else []
    body, x_args, x_in, x_out, x_scratch = _host(exchange, 2 + n_vec, len(outs), body, _first_last(T // tm))
    res = pl.pallas_call(
        body, name=name, grid=(T // tm,),
        out_shape=outs + x_out,
        in_specs=[_row_spec(tm, K)] + [_vec_spec(K)] * n_vec + [_whole_spec(wt)] + x_in,
        out_specs=out_specs + x_in,
        scratch_shapes=x_scratch,
        compiler_params=_params("arbitrary" if exchange else "parallel"),
    )(h, *vecs, wt, *x_args)
    return res[:len(outs)], res[len(outs):]


def _ffn_dgate(dy, w_out, pq, name, *, tm=512):
    T, N = dy.shape
    F = w_out.shape[0]
    tm = min(tm, T)

    def body(dy_ref, w_ref, pq_ref, dz_ref):
        dyv = dy_ref[...]
        for cols in _col_chunks(F):
            hi = slice(F + cols.start, F + cols.stop)
            dg = _dot(dyv, w_ref[cols], NT)
            dz_ref[:, cols] = (dg * pq_ref[:, cols].astype(F32)).astype(BF16)
            dz_ref[:, hi] = (dg * pq_ref[:, hi].astype(F32)).astype(BF16)

    return pl.pallas_call(
        body, name=name, grid=(T // tm,),
        out_shape=jax.ShapeDtypeStruct((T, 2 * F), BF16),
        in_specs=[_row_spec(tm, N), _whole_spec(w_out), _row_spec(tm, 2 * F)],
        out_specs=_row_spec(tm, 2 * F),
        compiler_params=_params("parallel"),
    )(dy, w_out, pq)


def _ada_fwd(c_all, w, b, name):
    L, K, n = w.shape

    def body(c_ref, w_ref, b_ref, o_ref):
        cv = c_ref[...]
        cond = cv * _sigmoid(cv)
        for l in range(L):
            o_ref[l] = _dot(cond, w_ref[l], precision=lax.Precision.HIGHEST) + b_ref[l]

    return pl.pallas_call(
        body, name=name,
        out_shape=jax.ShapeDtypeStruct((L, NDEV, n), F32),
        compiler_params=pltpu.CompilerParams(vmem_limit_bytes=VMEM_LIMIT),
    )(c_all, w, b)


def _ada_bwd(c_all_t, gmod, name):
    L, _, n = gmod.shape
    K = c_all_t.shape[0]

    def body(c_ref, g_ref, o_ref):
        cv = c_ref[...]
        cond = cv * _sigmoid(cv)
        for l in range(L):
            o_ref[l] = _dot(cond, g_ref[l], precision=lax.Precision.HIGHEST)

    return pl.pallas_call(
        body, name=name,
        out_shape=jax.ShapeDtypeStruct((L, K, n), F32),
        compiler_params=pltpu.CompilerParams(vmem_limit_bytes=VMEM_LIMIT),
    )(c_all_t, gmod)


def _tri(n, upper=False, block=None):
    r = lax.broadcasted_iota(jnp.int32, (n, n), 0)
    c = lax.broadcasted_iota(jnp.int32, (n, n), 1)
    m = (c >= r) if upper else (c <= r)
    if block is not None:
        m = m & ((r // block) == (c // block))
    return m.astype(BF16)


TRI_ROWS = 128


def _tri_dot(tri, x):
    hi = x.astype(BF16)
    lo = (x - hi.astype(F32)).astype(BF16)
    rows = x.shape[0]
    step = min(TRI_ROWS, rows)
    parts = [_dot(tri, hi[r:r + step]) + _dot(tri, lo[r:r + step]) for r in range(0, rows, step)]
    return parts[0] if len(parts) == 1 else jnp.concatenate(parts, axis=0)


def _hgrn_gates(proj_ref, lb_ref, jh):
    proj_ref = _ColBlocks(proj_ref, 512)
    lb = lb_ref[:, 512 * jh:512 * (jh + 1)]
    qp = proj_ref[jh]
    fx = proj_ref[2 + jh]
    sq = _sigmoid_t(qp)
    sig = _sigmoid_t(fx)
    f = lb + (1.0 - lb) * sig
    k = (1.0 - lb) * (1.0 - sig)
    return lb, qp, sq, sig, f, k


def _hgrn_fwd(proj, lb, out_norm, name, *, tb=512, exchange=None):
    T = proj.shape[0]
    tb = min(tb, T)
    nc = tb // HG_CHUNK
    lmat = _tri(min(TRI_ROWS, tb), block=HG_CHUNK)

    def body(proj_ref, lb_ref, on_ref, l_ref, o_ref, og_ref, st_ref, s_scr, b_scr):
        @pl.when(pl.program_id(0) == 0)
        def _():
            s_scr[...] = jnp.zeros_like(s_scr)

        r_i = lax.broadcasted_iota(jnp.int32, (HG_CHUNK, HG_CHUNK), 0)
        c_i = lax.broadcasted_iota(jnp.int32, (HG_CHUNK, HG_CHUNK), 1)
        causal = c_i <= r_i
        onv = on_ref[...]
        blocks = _ColBlocks(proj_ref, 512)
        for jh in range(2):
            lbv, qp, sq, sig, f, k = _hgrn_gates(proj_ref, lb_ref, jh)
            q = qp * sq
            b_half = b_scr.at[jh]
            b_half[...] = _tri_dot(l_ref[...], jnp.log(f))
            v = blocks[4 + jh]
            gp = blocks[6 + jh]
            gs = gp * _sigmoid_t(gp)
            for hh in range(4):
                hd = 4 * jh + hh
                cs = slice(HG_HEAD * hh, HG_HEAD * (hh + 1))
                for ci in range(nc):
                    r0 = HG_CHUNK * ci
                    rs = slice(r0, r0 + HG_CHUNK)
                    bc = b_half[rs, cs]
                    bm = b_half[r0 + HG_CHUNK // 2 - 1:r0 + HG_CHUNK // 2, cs]
                    bl = b_half[r0 + HG_CHUNK - 1:r0 + HG_CHUNK, cs]
                    qc, kc, vc = q[rs, cs], k[rs, cs], v[rs, cs].astype(BF16)
                    e_q, e_k = jnp.exp(bc - bm), jnp.exp(bm - bc)
                    qe = (qc * (e_q * jnp.exp(bm))).astype(BF16)
                    qt = (qc * e_q).astype(BF16)
                    kt = (kc * e_k).astype(BF16)
                    kd = (kc * (e_k * jnp.exp(bl - bm))).astype(BF16)
                    st = s_scr[hd]
                    stb = st.astype(BF16)
                    st_ref[ci, hd] = stb
                    a = jnp.where(causal, _dot(qt, kt, NT), 0.0).astype(BF16)
                    o = _dot(qe, stb, NT) + _dot(a, vc)
                    s_scr[hd] = st * jnp.exp(bl) + _dot(vc, kd, TN)
                    o_ref[rs, HG_HEAD * hd:HG_HEAD * (hd + 1)] = o
                    r = lax.rsqrt(_rowmean(o * o) + EPS)
                    og_ref[rs, HG_HEAD * hd:HG_HEAD * (hd + 1)] = (o * r * onv * gs[rs, cs]).astype(BF16)

    body, x_args, x_in, x_out, x_scratch = _host(exchange, 4, 3, body, _first_last(T // tb))
    res = pl.pallas_call(
        body, name=name, grid=(T // tb,),
        out_shape=[jax.ShapeDtypeStruct((T, D_MODEL), F32), jax.ShapeDtypeStruct((T, D_MODEL), BF16),
                   jax.ShapeDtypeStruct((T // HG_CHUNK, HG_HEADS, HG_HEAD, HG_HEAD), BF16)] + x_out,
        in_specs=[pl.BlockSpec((tb, 4 * D_MODEL), lambda i: (i, 0)),
                  pl.BlockSpec((1, D_MODEL), lambda i: (0, 0)),
                  pl.BlockSpec((1, HG_HEAD), lambda i: (0, 0)),
                  pl.BlockSpec(lmat.shape, lambda i: (0, 0))] + x_in,
        out_specs=[pl.BlockSpec((tb, D_MODEL), lambda i: (i, 0)),
                   pl.BlockSpec((tb, D_MODEL), lambda i: (i, 0)),
                   pl.BlockSpec((nc, HG_HEADS, HG_HEAD, HG_HEAD), lambda i: (i, 0, 0, 0))] + x_in,
        scratch_shapes=[pltpu.VMEM((HG_HEADS, HG_HEAD, HG_HEAD), F32), pltpu.VMEM((2, tb, 512), F32)] + x_scratch,
        compiler_params=_params("arbitrary"),
    )(proj, lb, out_norm, lmat, *x_args)
    return res[0], res[1], res[2], res[3:]


def _hgrn_bwd(proj, o, dog, states, lb, out_norm, name, *, tb=512):
    T = proj.shape[0]
    tb = min(tb, T)
    nc = tb // HG_CHUNK
    nb = T // tb
    lmat = _tri(min(TRI_ROWS, tb), block=HG_CHUNK)
    umat = _tri(min(TRI_ROWS, tb), upper=True, block=HG_CHUNK)

    def body(proj_ref, o_ref, dog_ref, st_ref, lb_ref, on_ref, l_ref, u_ref,
             dproj_ref, dlb_ref, don_ref, ds_scr, *half_scr):
        @pl.when(pl.program_id(0) == 0)
        def _():
            ds_scr[...] = jnp.zeros_like(ds_scr)
            dlb_ref[...] = jnp.zeros_like(dlb_ref)
            don_ref[...] = jnp.zeros_like(don_ref)

        r_i = lax.broadcasted_iota(jnp.int32, (HG_CHUNK, HG_CHUNK), 0)
        c_i = lax.broadcasted_iota(jnp.int32, (HG_CHUNK, HG_CHUNK), 1)
        causal = c_i <= r_i
        causal_t = r_i <= c_i
        last_row = lax.broadcasted_iota(jnp.int32, (HG_CHUNK, HG_HEAD), 0) == HG_CHUNK - 1
        onv = on_ref[...]
        don_acc = jnp.zeros((1, HG_HEAD), F32)
        blocks = _ColBlocks(proj_ref, 512)
        dproj_ref = _ColBlocks(dproj_ref, 512)
        for jh in range(2):
            b_scr, dq_scr, dk_scr, dv_scr, dg_scr, db_scr = [s.at[jh] for s in half_scr]
            lbv, qp, sq, sig, f, k = _hgrn_gates(proj_ref, lb_ref, jh)
            q = qp * sq
            b_scr[...] = _tri_dot(l_ref[...], jnp.log(f))
            v = blocks[4 + jh]
            gp = blocks[6 + jh]
            sg = _sigmoid_t(gp)
            for ci in reversed(range(nc)):
                r0 = HG_CHUNK * ci
                rs = slice(r0, r0 + HG_CHUNK)
                for hh in range(4):
                    hd = 4 * jh + hh
                    cs = slice(HG_HEAD * hh, HG_HEAD * (hh + 1))
                    hs = slice(HG_HEAD * hd, HG_HEAD * (hd + 1))
                    oc = o_ref[rs, hs]
                    r = lax.rsqrt(_rowmean(oc * oc) + EPS)
                    oh = oc * r
                    gc, sgc = gp[rs, cs], sg[rs, cs]
                    dogc = dog_ref[rs, hs].astype(F32)
                    don = dogc * (gc * sgc)
                    dg_scr[rs, cs] = dogc * (oh * onv) * (sgc * (1.0 + gc * (1.0 - sgc)))
                    don_acc += _colsum(don * oh)
                    donh = don * onv
                    do = (r * (donh - oh * _rowmean(donh * oh))).astype(BF16)
                    bc = b_scr[rs, cs]
                    bm = b_scr[r0 + HG_CHUNK // 2 - 1:r0 + HG_CHUNK // 2, cs]
                    bl = b_scr[r0 + HG_CHUNK - 1:r0 + HG_CHUNK, cs]
                    qc, kc, vc = q[rs, cs], k[rs, cs], v[rs, cs].astype(BF16)
                    e_q, e_k = jnp.exp(bc - bm), jnp.exp(bm - bc)
                    e_b, e_d = e_q * jnp.exp(bm), e_k * jnp.exp(bl - bm)
                    qe = (qc * e_b).astype(BF16)
                    qt = (qc * e_q).astype(BF16)
                    kt = (kc * e_k).astype(BF16)
                    kd = (kc * e_d).astype(BF16)
                    stb = st_ref[ci, hd]
                    dst = ds_scr[hd]
                    dstb = dst.astype(BF16)
                    a_t = jnp.where(causal_t, _dot(kt, qt, NT), 0.0).astype(BF16)
                    da = jnp.where(causal, _dot(do, vc, NT), 0.0).astype(BF16)
                    da_t = jnp.where(causal_t, _dot(vc, do, NT), 0.0).astype(BF16)
                    dv_scr[rs, cs] = _dot(a_t, do) + _dot(kd, dstb, NT)
                    dqe, dqt = _dot(do, stb), _dot(da, kt)
                    dkt, dkd = _dot(da_t, qt), _dot(vc, dstb)
                    dq_scr[rs, cs] = dqe * e_b + dqt * e_q
                    dk_scr[rs, cs] = dkt * e_k + dkd * e_d
                    e_l = jnp.exp(bl)
                    s_end = stb.astype(F32) * e_l + _dot(vc, kd, TN)
                    dbc = (qe.astype(F32) * dqe + qt.astype(F32) * dqt
                           - kt.astype(F32) * dkt - kd.astype(F32) * dkd)
                    db_scr[rs, cs] = dbc + jnp.where(last_row, _colsum(dstb.astype(F32) * s_end), 0.0)
                    ds_scr[hd] = dst * e_l + _dot(do, qe, TN)
            dq = dq_scr[...]
            dk = dk_scr[...]
            cols = slice(512 * jh, 512 * (jh + 1))
            dlogf = _tri_dot(u_ref[...], db_scr[...])
            one_m_sig = 1.0 - sig
            dsig = (1.0 - lbv) * sig * one_m_sig
            dboth = dlogf / f - dk
            dproj_ref[jh] = (dq * (sq * (1.0 + qp * (1.0 - sq)))).astype(BF16)
            dproj_ref[2 + jh] = (dboth * dsig).astype(BF16)
            dproj_ref[4 + jh] = dv_scr[...].astype(BF16)
            dproj_ref[6 + jh] = dg_scr[...].astype(BF16)
            dlb_ref[:, cols] += _colsum(dboth * one_m_sig)
        don_ref[...] += don_acc

    rev = lambda i: nb - 1 - i
    return pl.pallas_call(
        body, name=name, grid=(nb,),
        out_shape=(jax.ShapeDtypeStruct((T, 4 * D_MODEL), BF16), jax.ShapeDtypeStruct((1, D_MODEL), F32),
                   jax.ShapeDtypeStruct((1, HG_HEAD), F32)),
        in_specs=[pl.BlockSpec((tb, 4 * D_MODEL), lambda i: (rev(i), 0)),
                  pl.BlockSpec((tb, D_MODEL), lambda i: (rev(i), 0)),
                  pl.BlockSpec((tb, D_MODEL), lambda i: (rev(i), 0)),
                  pl.BlockSpec((nc, HG_HEADS, HG_HEAD, HG_HEAD), lambda i: (rev(i), 0, 0, 0)),
                  pl.BlockSpec((1, D_MODEL), lambda i: (0, 0)),
                  pl.BlockSpec((1, HG_HEAD), lambda i: (0, 0)),
                  pl.BlockSpec(lmat.shape, lambda i: (0, 0)),
                  pl.BlockSpec(lmat.shape, lambda i: (0, 0))],
        out_specs=(pl.BlockSpec((tb, 4 * D_MODEL), lambda i: (rev(i), 0)),
                   pl.BlockSpec((1, D_MODEL), lambda i: (0, 0)),
                   pl.BlockSpec((1, HG_HEAD), lambda i: (0, 0))),
        scratch_shapes=[pltpu.VMEM((HG_HEADS, HG_HEAD, HG_HEAD), F32)] + [pltpu.VMEM((2, tb, 512), F32)] * 6,
        compiler_params=_params("arbitrary"),
    )(proj, o, dog, states, lb, out_norm, lmat, umat)


def _gm_norm(pre_ref, lg_ref, lbias_ref):
    pre_ref = _ColBlocks(pre_ref, 768)
    vs = [pre_ref[4 + j].astype(F32) for j in range(4)]
    width = 4 * vs[0].shape[1]
    mu = sum(jnp.sum(v, axis=1, keepdims=True) for v in vs) / width
    ds = [v - mu for v in vs]
    var = sum(jnp.sum(d * d, axis=1, keepdims=True) for d in ds) / width
    rstd = lax.rsqrt(var + EPS)
    vhat = [d * rstd for d in ds]
    vn = [vhat[j] * lg_ref[j:j + 1, :] + lbias_ref[j:j + 1, :] for j in range(4)]
    return vhat, vn, rstd


def _gm_spatial_fwd(pre, ln_g, ln_b, ws, bsb, name, *, tb=256):
    T = pre.shape[0]
    tb = min(tb, T)
    nc = tb // GM_CHUNK

    def body(pre_ref, lg_ref, lbias_ref, ws_ref, bs_ref, o_ref):
        _, vn, _ = _gm_norm(pre_ref, lg_ref, lbias_ref)
        pre_ref, o_ref = _ColBlocks(pre_ref, 768), _ColBlocks(o_ref, 768)
        for j in range(4):
            u = pre_ref[j].astype(F32)
            for e in range(2):
                g = 2 * j + e
                cs = slice(GM_GDIM * e, GM_GDIM * (e + 1))
                wg = ws_ref[g].astype(BF16)
                for ci in range(nc):
                    rs = slice(GM_CHUNK * ci, GM_CHUNK * (ci + 1))
                    vm = _dot(wg, vn[j][rs, cs].astype(BF16)) + bs_ref[g]
                    o_ref[j, rs, cs] = (u[rs, cs] * vm).astype(BF16)

    return pl.pallas_call(
        body, name=name, grid=(T // tb,),
        out_shape=jax.ShapeDtypeStruct((T, 4 * 768), BF16),
        in_specs=[pl.BlockSpec((tb, 8 * 768), lambda i: (i, 0)),
                  pl.BlockSpec((4, 768), lambda i: (0, 0)),
                  pl.BlockSpec((4, 768), lambda i: (0, 0)),
                  pl.BlockSpec((GM_GROUPS, GM_CHUNK, GM_CHUNK), lambda i: (0, 0, 0)),
                  pl.BlockSpec((GM_GROUPS, GM_CHUNK, GM_GDIM), lambda i: (0, 0, 0))],
        out_specs=pl.BlockSpec((tb, 4 * 768), lambda i: (i, 0)),
        compiler_params=_params("parallel"),
    )(pre, ln_g, ln_b, ws, bsb)


def _gm_spatial_bwd(pre, gp, dm, ln_g, ln_b, ws, ws_t, bsb, name, *, tb=256):
    T = pre.shape[0]
    tb = min(tb, T)
    nc = tb // GM_CHUNK
    nb = T // tb

    def body(pre_ref, gp_ref, dm_ref, lg_ref, lbias_ref, ws_ref, wst_ref, bs_ref,
             dpre_ref, dws_ref, dbs_ref, dlg_ref, dlb_ref, dbin_ref, dbs_scr, dvn_scr, du_scr):
        i = pl.program_id(0)

        @pl.when(i == 0)
        def _():
            dws_ref[...] = jnp.zeros_like(dws_ref)
            dbs_scr[...] = jnp.zeros_like(dbs_scr)
            dlg_ref[...] = jnp.zeros_like(dlg_ref)
            dlb_ref[...] = jnp.zeros_like(dlb_ref)
            dbin_ref[...] = jnp.zeros_like(dbin_ref)

        vhat, vn, rstd = _gm_norm(pre_ref, lg_ref, lbias_ref)
        pre_ref, gp_ref, dm_ref = _ColBlocks(pre_ref, 768), _ColBlocks(gp_ref, 768), _ColBlocks(dm_ref, 768)
        dpre_ref = _ColBlocks(dpre_ref, 768)
        for j in range(4):
            u = pre_ref[j].astype(F32)
            for e in range(2):
                g = 2 * j + e
                cs = slice(GM_GDIM * e, GM_GDIM * (e + 1))
                wg = ws_ref[g].astype(BF16)
                wgt = wst_ref[g].astype(BF16)
                for ci in range(nc):
                    rs = slice(GM_CHUNK * ci, GM_CHUNK * (ci + 1))
                    vnb = vn[j][rs, cs].astype(BF16)
                    vm = _dot(wg, vnb) + bs_ref[g]
                    dmg = dm_ref[j, rs, cs].astype(F32)
                    du_scr[j, rs, cs] = dmg * vm
                    dvm = dmg * u[rs, cs]
                    dvmb = dvm.astype(BF16)
                    dws_ref[g] += _dot(dvmb, vnb, NT)
                    dbs_scr[g] += dvm
                    dvn_scr[j, rs, cs] = _dot(wgt, dvmb)
        width = 4 * 768
        dvh = []
        for j in range(4):
            dvn = dvn_scr[j]
            dlg_ref[j:j + 1, :] += _colsum(dvn * vhat[j])
            dlb_ref[j:j + 1, :] += _colsum(dvn)
            dvh.append(dvn * lg_ref[j:j + 1, :])
        m1 = sum(jnp.sum(d, axis=1, keepdims=True) for d in dvh) / width
        m2 = sum(jnp.sum(dvh[j] * vhat[j], axis=1, keepdims=True) for j in range(4)) / width
        for j in range(4):
            dv = rstd * (dvh[j] - m1 - vhat[j] * m2)
            dpv = dv * gp_ref[4 + j].astype(F32)
            dpu = du_scr[j] * gp_ref[j].astype(F32)
            dpre_ref[4 + j] = dpv.astype(BF16)
            dpre_ref[j] = dpu.astype(BF16)
            dbin_ref[4 + j:5 + j, :] += _colsum(dpv)
            dbin_ref[j:j + 1, :] += _colsum(dpu)

        @pl.when(i == nb - 1)
        def _():
            r_i = lax.broadcasted_iota(jnp.int32, (GM_CHUNK, GM_CHUNK), 0)
            c_i = lax.broadcasted_iota(jnp.int32, (GM_CHUNK, GM_CHUNK), 1)
            for g in range(GM_GROUPS):
                dws_ref[g] = jnp.where(c_i <= r_i, dws_ref[g], 0.0)
                dbs_ref[g] = jnp.broadcast_to(jnp.sum(dbs_scr[g], axis=1, keepdims=True), (GM_CHUNK, GM_CHUNK))

    sq = pl.BlockSpec((GM_GROUPS, GM_CHUNK, GM_CHUNK), lambda i: (0, 0, 0))
    v4 = pl.BlockSpec((4, 768), lambda i: (0, 0))
    return pl.pallas_call(
        body, name=name, grid=(nb,),
        out_shape=(jax.ShapeDtypeStruct((T, 8 * 768), BF16),
                   jax.ShapeDtypeStruct((GM_GROUPS, GM_CHUNK, GM_CHUNK), F32),
                   jax.ShapeDtypeStruct((GM_GROUPS, GM_CHUNK, GM_CHUNK), F32),
                   jax.ShapeDtypeStruct((4, 768), F32), jax.ShapeDtypeStruct((4, 768), F32),
                   jax.ShapeDtypeStruct((8, 768), F32)),
        in_specs=[pl.BlockSpec((tb, 8 * 768), lambda i: (i, 0)),
                  pl.BlockSpec((tb, 8 * 768), lambda i: (i, 0)),
                  pl.BlockSpec((tb, 4 * 768), lambda i: (i, 0)),
                  v4, v4, sq, sq,
                  pl.BlockSpec((GM_GROUPS, GM_CHUNK, GM_GDIM), lambda i: (0, 0, 0))],
        out_specs=(pl.BlockSpec((tb, 8 * 768), lambda i: (i, 0)), sq, sq, v4, v4,
                   pl.BlockSpec((8, 768), lambda i: (0, 0))),
        scratch_shapes=[pltpu.VMEM((GM_GROUPS, GM_CHUNK, GM_GDIM), F32),
                        pltpu.VMEM((4, tb, 768), F32), pltpu.VMEM((4, tb, 768), F32)],
        compiler_params=_params("arbitrary"),
    )(pre, gp, dm, ln_g, ln_b, ws, ws_t, bsb)


def _adamw(slots, w, m, v, name, *, tr=256):
    S, R, C = slots.shape
    tr = next((t for t in (tr, tr // 2, tr // 4, tr // 8, tr // 16) if R % t == 0), R) if R > tr else R
    bc1 = 1.0 - ADAM_B1 ** ADAM_STEP
    bc2 = 1.0 - ADAM_B2 ** ADAM_STEP

    def body(s_ref, w_ref, m_ref, v_ref, g_ref, d_ref, nm_ref, nv_ref):
        g = s_ref[0].astype(F32)
        for s in range(1, S):
            g = g + s_ref[s].astype(F32)
        mn = ADAM_B1 * m_ref[...] + (1.0 - ADAM_B1) * g
        vn = ADAM_B2 * v_ref[...] + (1.0 - ADAM_B2) * (g * g)
        g_ref[...] = g
        nm_ref[...] = mn
        nv_ref[...] = vn
        d_ref[...] = -ADAM_LR * ((mn / bc1) / (jnp.sqrt(vn / bc2) + ADAM_EPS) + ADAM_WD * w_ref[...])

    spec = pl.BlockSpec((tr, C), lambda i: (i, 0))
    return pl.pallas_call(
        body, name=name, grid=(R // tr,),
        out_shape=(jax.ShapeDtypeStruct((R, C), F32),) * 4,
        in_specs=[pl.BlockSpec((S, tr, C), lambda i: (0, i, 0)), spec, spec, spec],
        out_specs=(spec,) * 4,
        compiler_params=_params("parallel"),
    )(slots, w, m, v)


def _update(slots, w, m, v, name):
    shp = w.shape
    C = shp[-1]
    R = math.prod(shp[:-1])
    outs = _adamw(slots.reshape(slots.shape[0], R, C), w.reshape(R, C), m.reshape(R, C), v.reshape(R, C), name)
    return tuple(o.reshape(shp) for o in outs)


def kernel(x, c, ada_w, ada_b, norm_pre, norm_post, ffn_w_in, ffn_w_out, hg_w_in, hg_w_out, hg_out_norm, hg_lb, gm_w_in, gm_b_in, gm_ln_g, gm_ln_b, gm_w_s, gm_b_s, gm_w_out, loss_target, m_ada_w, m_ada_b, m_norm_pre, m_norm_post, m_ffn_w_in, m_ffn_w_out, m_hg_w_in, m_hg_w_out, m_hg_out_norm, m_hg_lb, m_gm_w_in, m_gm_b_in, m_gm_ln_g, m_gm_ln_b, m_gm_w_s, m_gm_b_s, m_gm_w_out, v_ada_w, v_ada_b, v_norm_pre, v_norm_post, v_ffn_w_in, v_ffn_w_out, v_hg_w_in, v_hg_w_out, v_hg_out_norm, v_hg_lb, v_gm_w_in, v_gm_b_in, v_gm_ln_g, v_gm_ln_b, v_gm_w_s, v_gm_b_s, v_gm_w_out):
    me = 4 * lax.axis_index("x") + 2 * lax.axis_index("y") + lax.axis_index("c")
    T = x.shape[1]
    x0 = x.reshape(T, D_MODEL)
    target = loss_target.reshape(T, D_MODEL)
    n_ada = ada_w.shape[-1]

    pack = jnp.concatenate([
        c.reshape(8, 128), norm_pre.reshape(6, 128), norm_post.reshape(6, 128),
        gm_b_in.reshape(6, 128), gm_ln_g.reshape(3, 128), gm_ln_b.reshape(3, 128)], axis=0)
    packs = _all_gather(pack, "gather_small")
    c_all = packs[:, 0:8].reshape(NDEV, D_MODEL)
    npre = packs[:, 8:14].reshape(NDEV, 2, 3, 128).transpose(1, 2, 0, 3).reshape(2, 3, D_MODEL)
    npost = packs[:, 14:20].reshape(NDEV, 2, 3, 128).transpose(1, 2, 0, 3).reshape(2, 3, D_MODEL)
    b_in = packs[:, 20:26].reshape(1, NDEV * 768)
    ln_g = packs[:, 26:29].reshape(4, 768)
    ln_b = packs[:, 29:32].reshape(4, 768)

    ada_b_mine = lax.dynamic_slice_in_dim(ada_b, me * n_ada, n_ada, axis=1).reshape(2, 1, n_ada)
    mod_cols = _ada_fwd(c_all, ada_w, ada_b_mine, "ada_fwd")
    mod_all = _all_gather(mod_cols, "gather_mod")
    mod = lax.dynamic_index_in_dim(mod_all, me, axis=2, keepdims=False)
    mod = mod.transpose(1, 0, 2).reshape(2, 9, 1, D_MODEL)

    sh_fi, sh_fo = ffn_w_in.astype(BF16).swapaxes(-1, -2), ffn_w_out.astype(BF16)
    sh_hi, sh_ho = hg_w_in[0].astype(BF16).T, hg_w_out[0].astype(BF16)
    sh_mi, sh_mo = gm_w_in[0].astype(BF16).T, gm_w_out[0].astype(BF16)

    def whole(gathered):
        return gathered.reshape(-1, D_MODEL)

    w_fi = {(0, 0): whole(_all_gather(sh_fi[0, 0], "gather_ffn_in_first"))}
    w_fo = {}
    riders = {"l0s0": [sh_fo[0, 0], sh_hi, sh_ho], "l0s1": [sh_fi[0, 1], sh_fo[0, 1]], "hg_mix": [sh_mi, sh_mo],
              "l0s2": [sh_fi[1, 0], sh_fo[1, 0]], "l1s0": [sh_fi[1, 1], sh_fo[1, 1]]}

    sm = jax.nn.softmax(hg_lb, axis=0)
    lb0 = sm[0:1]
    on = hg_out_norm.reshape(1, HG_HEAD)
    tril = jnp.tril(jnp.ones((GM_CHUNK, GM_CHUNK), F32))
    ws = gm_w_s[0] * tril[None]
    ws_t = ws.transpose(0, 2, 1)
    bsb = jnp.broadcast_to(gm_b_s[0][:, :, None], (GM_GROUPS, GM_CHUNK, GM_GDIM))

    res_ws = (0.5, 1.0, 0.5)

    def vecs(i, s):
        return (npre[i, s].reshape(1, D_MODEL), npost[i, s].reshape(1, D_MODEL),
                mod[i, 3 * s], mod[i, 3 * s + 1], mod[i, 3 * s + 2])

    order = [(i, s) for i in range(2) for s in range(3)]
    saved = {}
    xs = x0
    for pos, (i, s) in enumerate(order):
        tag = f"l{i}s{s}"
        pre_g, post_g, shift, scale, gate = vecs(i, s)
        rider = _Exchange("gather", riders[tag]) if tag in riders else None
        if s != 1:
            if pos == 0:
                (pq, a, h), got = _ffn_in(xs, w_fi[0, 0], "ffn_in_" + tag, opening=(pre_g, scale, shift), exchange=rider)
            else:
                (pq, a), got = _ffn_in(h, w_fi[i, s // 2], "ffn_in_" + tag, exchange=rider)
            extra = (pq, a)
            if tag == "l0s0":
                w_fo[0, 0], w_hi, w_ho = map(whole, got)
            elif tag == "l0s2":
                w_fi[1, 0], w_fo[1, 0] = map(whole, got)
            elif tag == "l1s0":
                w_fi[1, 1], w_fo[1, 1] = map(whole, got)
            wo = w_fo[i, s // 2]
        elif i == 0:
            proj, *got = _mm_blocks(h, w_hi, "hg_in", out_dtype=F32, exchange=rider)
            w_fi[0, 1], w_fo[0, 1] = map(whole, got)
            o, og, states, got = _hgrn_fwd(proj, lb0, on, "hg_mix", exchange=_Exchange("gather", riders["hg_mix"]))
            w_mi, w_mo = map(whole, got)
            a, wo = og, w_ho
            extra = (proj, o, og, states)
        else:
            z, pre = _mm_blocks(h, w_mi, "gm_in", bias=b_in, gelu=True)
            a = _gm_spatial_fwd(z, ln_g, ln_b, ws, bsb, "gm_mix")
            wo = w_mo
            extra = (z, pre, a)
        if pos + 1 < len(order):
            npre_g, _, nshift, nscale, _ = vecs(*order[pos + 1])
            y, x_next, h_next = _out_proj(a, wo, xs, post_g, gate, res_ws[s], (npre_g, nscale, nshift), "out_" + tag)
            saved[tag] = (xs, h, y) + extra
            xs, h = x_next, h_next
        else:
            dx, dy, dgate, dpost, loss_part = _out_proj_last(a, wo, xs, post_g, gate, res_ws[s], target, "out_" + tag)
            saved[tag] = (xs, h, None) + extra
    loss = lax.psum(loss_part[0, 0], ("x", "y", "c"))

    slots = {}
    d_npre = [[None] * 3, [None] * 3]
    d_npost = [[None] * 3, [None] * 3]
    d_mod = [[None] * 9, [None] * 9]
    for pos in reversed(range(len(order))):
        i, s = order[pos]
        tag = f"l{i}s{s}"
        pre_g, _, _, scale, _ = vecs(i, s)
        xin, h = saved[tag][:2]
        if s != 1:
            w_in, wo = w_fi[i, s // 2], w_fo[i, s // 2]
            pq, g = saved[tag][3:]
            dz = _ffn_dgate(dy, wo, pq, "ffn_dgate_" + tag)
            g_out = _mm_wgrad(g, dy, "ffn_out_wgrad_" + tag, xw=1408)
            g_in = _mm_wgrad(dz, h, "ffn_in_wgrad_" + tag, xw=1408)
        elif i == 0:
            proj, o, og, states = saved[tag][3:]
            dog = _mm_blocks(dy, w_ho, "hg_out_dgrad")
            g_out = _mm_wgrad(og, dy, "hg_out_wgrad")
            dz, d_lb0, d_on = _hgrn_bwd(proj, o, dog, states, lb0, on, "hg_mix_bwd")
            w_in = w_hi
            g_in = _mm_wgrad(h, dz, "hg_in_wgrad", yw=512)
        else:
            z, pre, sp = saved[tag][3:]
            dm, gp = _mm_blocks(dy, w_mo, "gm_out_dgrad", gelu_grad_of=pre)
            g_out = _mm_wgrad(sp, dy, "gm_out_wgrad", xw=768)
            dz, d_ws, d_bs, d_lg, d_lbias, d_bin = _gm_spatial_bwd(z, gp, dm, ln_g, ln_b, ws, ws_t, bsb, "gm_mix_bwd")
            w_in = w_mi
            g_in = _mm_wgrad(h, dz, "gm_in_wgrad", yw=768)
        g_out = g_out.reshape(NDEV, -1, D_MODEL)
        g_in = g_in.reshape(NDEV, -1, D_MODEL) if s != 1 else g_in
        d_npost[i][s] = dpost
        d_mod[i][3 * s + 2] = dgate
        rider = _Exchange("scatter", [g_in, g_out])
        if pos > 0:
            pi, ps = order[pos - 1]
            _, ppost_g, _, _, pgate = vecs(pi, ps)
            prev = (saved[f"l{pi}s{ps}"][2], ppost_g, pgate, res_ws[ps])
            dx, dshift, dscale, dpre_g, dy, dgate, dpost, r_in, r_out = _in_grad(
                dz, w_in, dx, xin, pre_g, scale, prev, "in_grad_" + tag, exchange=rider)
        else:
            dx, dshift, dscale, dpre_g, r_in, r_out = _in_grad(
                dz, w_in, dx, xin, pre_g, scale, None, "in_grad_" + tag, exchange=rider)
        slots[tag] = (r_in, r_out)
        d_npre[i][s] = dpre_g
        d_mod[i][3 * s], d_mod[i][3 * s + 1] = dshift, dscale
    grad_x = dx.reshape(x.shape)

    ffn_tags = ["l0s0", "l0s2", "l1s0", "l1s2"]
    s_fi = jnp.stack([slots[t][0] for t in ffn_tags], axis=1).swapaxes(-1, -2)
    s_fo = jnp.stack([slots[t][1] for t in ffn_tags], axis=1)
    (s_hi, s_ho), (s_mi, s_mo) = slots["l0s1"], slots["l1s1"]
    s_hi, s_ho, s_mi, s_mo = s_hi[:, None], s_ho[:, None], s_mi[:, None], s_mo[:, None]

    gmod = jnp.stack([jnp.concatenate(d_mod[i], axis=0) for i in range(2)])
    d_sm = lb0 * d_lb0
    d_hg_lb = jnp.concatenate([d_sm, jnp.zeros((2, D_MODEL), F32)], axis=0) - sm * d_sm
    small = [gmod, jnp.stack([jnp.concatenate(r, axis=0) for r in d_npre]),
             jnp.stack([jnp.concatenate(r, axis=0) for r in d_npost]),
             d_on, d_hg_lb, d_bin, d_lg, d_lbias, d_ws, d_bs[:, :, 0]]
    sizes = [a.size for a in small]
    flat = jnp.concatenate([a.reshape(-1) for a in small])
    rows = -(-flat.size // (8 * 128)) * 8
    flat = jnp.pad(flat, (0, rows * 128 - flat.size)).reshape(rows, 128)
    flats = _all_gather(flat, "gather_small_grads").reshape(NDEV, rows * 128)
    parts, off = [], 0
    for a, n in zip(small, sizes):
        parts.append(flats[:, off:off + n].reshape((NDEV,) + a.shape))
        off += n
    p_mod, p_npre, p_npost, p_on, p_lb, p_bin, p_lg, p_lbias, p_ws, p_bs = parts

    def mine(p, width):
        return lax.dynamic_slice_in_dim(p, me * width, width, axis=p.ndim - 1)

    gmod_cols = mine(p_mod.reshape(NDEV, 2, 9 * D_MODEL), n_ada).transpose(1, 0, 2)
    g_ada_w = _ada_bwd(jnp.pad(c_all.T, ((0, 0), (0, 120))), jnp.pad(gmod_cols, ((0, 0), (0, 120), (0, 0))), "ada_bwd")

    out = {}
    out["ada_w"] = _update(g_ada_w[None], ada_w, m_ada_w, v_ada_w, "adamw_ada_w")
    out["ada_b"] = _update(p_mod.reshape(NDEV, 2, 9 * D_MODEL), ada_b, m_ada_b, v_ada_b, "adamw_ada_b")
    out["norm_pre"] = _update(mine(p_npre, 128), norm_pre, m_norm_pre, v_norm_pre, "adamw_norm_pre")
    out["norm_post"] = _update(mine(p_npost, 128), norm_post, m_norm_post, v_norm_post, "adamw_norm_post")
    out["ffn_w_in"] = _update(s_fi.reshape((NDEV,) + ffn_w_in.shape), ffn_w_in, m_ffn_w_in, v_ffn_w_in, "adamw_ffn_in")
    out["ffn_w_out"] = _update(s_fo.reshape((NDEV,) + ffn_w_out.shape), ffn_w_out, m_ffn_w_out, v_ffn_w_out, "adamw_ffn_out")
    out["hg_w_in"] = _update(s_hi, hg_w_in, m_hg_w_in, v_hg_w_in, "adamw_hg_in")
    out["hg_w_out"] = _update(s_ho, hg_w_out, m_hg_w_out, v_hg_w_out, "adamw_hg_out")
    out["hg_out_norm"] = _update(p_on, hg_out_norm, m_hg_out_norm, v_hg_out_norm, "adamw_hg_norm")
    out["hg_lb"] = _update(p_lb, hg_lb, m_hg_lb, v_hg_lb, "adamw_hg_lb")
    out["gm_w_in"] = _update(s_mi, gm_w_in, m_gm_w_in, v_gm_w_in, "adamw_gm_in")
    out["gm_b_in"] = _update(mine(p_bin.reshape(NDEV, 1, 8 * 768), 768), gm_b_in, m_gm_b_in, v_gm_b_in, "adamw_gm_b_in")
    out["gm_ln_g"] = _update(mine(p_lg.reshape(NDEV, 1, 4 * 768), 384), gm_ln_g, m_gm_ln_g, v_gm_ln_g, "adamw_gm_ln_g")
    out["gm_ln_b"] = _update(mine(p_lbias.reshape(NDEV, 1, 4 * 768), 384), gm_ln_b, m_gm_ln_b, v_gm_ln_b, "adamw_gm_ln_b")
    out["gm_w_s"] = _update(p_ws[:, None], gm_w_s, m_gm_w_s, v_gm_w_s, "adamw_gm_w_s")
    out["gm_b_s"] = _update(p_bs[:, None], gm_b_s, m_gm_b_s, v_gm_b_s, "adamw_gm_b_s")
    out["gm_w_out"] = _update(s_mo, gm_w_out, m_gm_w_out, v_gm_w_out, "adamw_gm_out")

    names = ["ada_w", "ada_b", "norm_pre", "norm_post", "ffn_w_in", "ffn_w_out", "hg_w_in", "hg_w_out",
             "hg_out_norm", "hg_lb", "gm_w_in", "gm_b_in", "gm_ln_g", "gm_ln_b", "gm_w_s", "gm_b_s", "gm_w_out"]
    return (loss, grad_x, *[out[n][0] for n in names], *[out[n][1] for n in names],
            *[out[n][2] for n in names], *[out[n][3] for n in names])
```

```python
import functools
import math

import jax
import jax.numpy as jnp
from jax import lax
from jax.experimental import pallas as pl
from jax.experimental.pallas import tpu as pltpu

F32 = jnp.float32
BF16 = jnp.bfloat16
NDEV = 8
D_MODEL = 1024
EPS = 1e-6
HG_CHUNK = 64
HG_HEAD = 128
HG_HEADS = 8
GM_CHUNK = 128
GM_GDIM = 384
GM_GROUPS = 8
ADAM_LR = 0.001
ADAM_B1 = 0.9
ADAM_B2 = 0.999
ADAM_EPS = 1e-08
ADAM_WD = 0.01
ADAM_STEP = 10
VMEM_LIMIT = 56 * 2 ** 20

NN = (((1,), (0,)), ((), ()))
NT = (((1,), (1,)), ((), ()))
TN = (((0,), (0,)), ((), ()))
MESH = pl.DeviceIdType.MESH
ANY = pl.BlockSpec(memory_space=pl.ANY)


def _dot(a, b, dims=NN, precision=None):
    return lax.dot_general(a, b, dims, preferred_element_type=F32, precision=precision)


def _params(*sem):
    return pltpu.CompilerParams(dimension_semantics=sem, vmem_limit_bytes=VMEM_LIMIT)


def _sigmoid(x):
    return 1.0 / (1.0 + jnp.exp(-x))


def _sigmoid_t(x):
    return 0.5 * jnp.tanh(0.5 * x) + 0.5


def _gelu_and_grad(x):
    c = math.sqrt(2.0 / math.pi)
    m = (c * 0.044715) * (x * x)
    t = jnp.tanh(x * (c + m))
    hp = 0.5 + 0.5 * t
    return x * hp, hp * (1.0 + (x * (1.0 - t)) * (c + 3.0 * m))


def _colsum(x):
    return jnp.sum(x, axis=0, keepdims=True)


def _rowmean(x):
    return jnp.mean(x, axis=-1, keepdims=True)


def _all_gather(shard, name):
    def body(x_ref, out_ref, send_sems, recv_sems, local_sem):
        x, y, c = lax.axis_index("x"), lax.axis_index("y"), lax.axis_index("c")
        me, sibling = (x, y, c), (x, y, 1 - c)
        chips = [(1 - x, y), (x, 1 - y), (1 - x, 1 - y)]

        def slot(p):
            return out_ref.at[4 * p[0] + 2 * p[1] + p[2]]

        def copy(k, block, to, src=None):
            return pltpu.make_async_remote_copy(
                src_ref=slot(block) if src is None else src, dst_ref=slot(block),
                send_sem=send_sems.at[k], recv_sem=recv_sems.at[k],
                device_id=to, device_id_type=MESH)

        mine = pltpu.make_async_copy(x_ref, slot(me), local_sem)
        mine.start()
        first = [copy(0, me, sibling, src=x_ref)]
        first += [copy(1 + j, me, (*chip, c), src=x_ref) for j, chip in enumerate(chips)]
        for cp in first:
            cp.start()
        passed = [copy(4 + j, (*chip, c), sibling) for j, chip in enumerate(chips)]
        for j, chip in enumerate(chips):
            copy(1 + j, (*chip, c), me).wait_recv()
            passed[j].start()
        copy(0, sibling, me).wait_recv()
        for j, chip in enumerate(chips):
            copy(4 + j, (*chip, 1 - c), me).wait_recv()
        for cp in first + passed:
            cp.wait_send()
        mine.wait()

    return pl.pallas_call(
        body, name=name,
        out_shape=jax.ShapeDtypeStruct((NDEV,) + shard.shape, shard.dtype),
        in_specs=[ANY], out_specs=ANY,
        scratch_shapes=[pltpu.SemaphoreType.DMA((7,)), pltpu.SemaphoreType.DMA((7,)),
                        pltpu.SemaphoreType.DMA(())],
    )(shard)


class _Exchange:
    def __init__(self, kind, arrays):
        self.gather = kind == "gather"
        self.arrays = list(arrays)
        self.n = n = len(self.arrays)
        self.out_shape = [jax.ShapeDtypeStruct(((NDEV,) + a.shape) if self.gather else a.shape, a.dtype)
                          for a in self.arrays]
        self.scratch = [pltpu.SemaphoreType.DMA((n, NDEV - 1)), pltpu.SemaphoreType.DMA((n, NDEV - 1)),
                        pltpu.SemaphoreType.DMA((n,))]

    def _copies(self, in_refs, out_refs, sems):
        send_sems, recv_sems, local_sems = sems
        x, y, c = lax.axis_index("x"), lax.axis_index("y"), lax.axis_index("c")
        me = 4 * x + 2 * y + c
        peers = [(1 - x if k & 4 else x, 1 - y if k & 2 else y, 1 - c if k & 1 else c) for k in range(1, NDEV)]
        local, send, recv = [], [], []
        for a in range(self.n):
            src = (lambda pid, a=a: in_refs[a]) if self.gather else (lambda pid, a=a: in_refs[a].at[pid])
            local.append(pltpu.make_async_copy(src(me), out_refs[a].at[me], local_sems.at[a]))
            for k, p in enumerate(peers):
                pid = 4 * p[0] + 2 * p[1] + p[2]
                for lst, slot in ((send, me), (recv, pid)):
                    lst.append(pltpu.make_async_remote_copy(
                        src_ref=src(pid), dst_ref=out_refs[a].at[slot],
                        send_sem=send_sems.at[a, k], recv_sem=recv_sems.at[a, k],
                        device_id=p, device_id_type=MESH))
        return local, send, recv

    def start(self, first, in_refs, out_refs, sems):
        @pl.when(first)
        def _():
            local, send, _ = self._copies(in_refs, out_refs, sems)
            for cp in local + send:
                cp.start()

    def finish(self, last, in_refs, out_refs, sems):
        @pl.when(last)
        def _():
            local, send, recv = self._copies(in_refs, out_refs, sems)
            for cp in send:
                cp.wait_send()
            for cp in recv:
                cp.wait_recv()
            for cp in local:
                cp.wait()


def _host(exchange, n_in, n_out, body, first_last):
    if exchange is None:
        return body, [], [], [], []
    n = exchange.n

    def hosted(*refs):
        ins, refs = refs[:n_in], refs[n_in:]
        xin, refs = refs[:n], refs[n:]
        outs, refs = refs[:n_out], refs[n_out:]
        xout, refs = refs[:n], refs[n:]
        scratch, sems = refs[:len(refs) - 3], refs[len(refs) - 3:]
        first, last = first_last()
        exchange.start(first, xin, xout, sems)
        body(*ins, *outs, *scratch)
        exchange.finish(last, xin, xout, sems)

    return hosted, exchange.arrays, [ANY] * n, exchange.out_shape, exchange.scratch


def _first_last(steps):
    def at():
        i = pl.program_id(0)
        return i == 0, i == steps - 1
    return at


class _ColBlocks:
    def __init__(self, ref, width):
        self.ref, self.width = ref, width

    def _index(self, key):
        key = key if isinstance(key, tuple) else (key,)
        rows = key[1] if len(key) > 1 else slice(None)
        cols = key[2] if len(key) > 2 else slice(0, self.width)
        c0 = key[0] * self.width
        return rows, slice(c0 + cols.start, c0 + cols.stop)

    def __getitem__(self, key):
        return self.ref[self._index(key)]

    def __setitem__(self, key, value):
        self.ref[self._index(key)] = value


def _col_chunks(width, chunk=768):
    return [slice(c, min(c + chunk, width)) for c in range(0, width, chunk)]


def _row_spec(tm, d):
    return pl.BlockSpec((tm, d), lambda m: (m, 0))


def _vec_spec(d):
    return pl.BlockSpec((1, d), lambda m: (0, 0))


def _whole_spec(w):
    nd = w.ndim
    return pl.BlockSpec(w.shape, lambda m: (0,) * nd, pipeline_mode=pl.Buffered(1))


def _mm_blocks(a, w, name, *, bias=None, out_dtype=BF16, tm=512, chunk=768, gelu=False, exchange=None):
    T, K = a.shape
    C = w.shape[0]
    tm = min(tm, T)
    n_in = 2 + (bias is not None)

    def body(*refs):
        a_ref, w_ref = refs[:2]
        av = a_ref[...]
        for cols in _col_chunks(C, chunk):
            r = _dot(av, w_ref[cols], NT)
            if bias is not None:
                r = r + refs[2][:, cols]
            if gelu:
                z, dz = _gelu_and_grad(r)
                refs[n_in][:, cols] = z.astype(BF16)
                refs[n_in + 1][:, cols] = dz.astype(BF16)
            else:
                refs[n_in][:, cols] = r.astype(out_dtype)

    in_specs = [_row_spec(tm, K), _whole_spec(w)]
    args = [a, w]
    if bias is not None:
        in_specs.append(_whole_spec(bias))
        args.append(bias)
    outs = [jax.ShapeDtypeStruct((T, C), BF16)] * 2 if gelu else [jax.ShapeDtypeStruct((T, C), out_dtype)]
    body, x_args, x_in, x_out, x_scratch = _host(exchange, n_in, len(outs), body, _first_last(T // tm))
    res = pl.pallas_call(
        body, name=name, grid=(T // tm,),
        out_shape=outs + x_out,
        in_specs=in_specs + x_in,
        out_specs=[_row_spec(tm, C)] * len(outs) + x_in,
        scratch_shapes=x_scratch,
        compiler_params=_params("arbitrary" if exchange else "parallel"),
    )(*args, *x_args)
    return res if (exchange or gelu) else res[0]


def _rms(v):
    return lax.rsqrt(_rowmean(v * v) + EPS)


def _zero_at_start(*refs):
    @pl.when(pl.program_id(0) == 0)
    def _():
        for r in refs:
            r[...] = jnp.zeros_like(r)


def _postnorm_bwd_math(dxo, yv, g, gate, res_w, dgate_ref, dpost_ref):
    r = _rms(yv)
    yh = yv * r
    both = res_w * _colsum(dxo * yh)
    dgate_ref[...] += g * both
    dpost_ref[...] += gate * both
    dyh = dxo * (res_w * gate * g)
    return (r * (dyh - yh * _rowmean(dyh * yh))).astype(BF16)


def _out_proj(a, w, x, post_g, gate, res_w, nxt, name, *, tm=512):
    T, ka = a.shape
    d = w.shape[1]
    tm = min(tm, T)

    def body(a_ref, w_ref, x_ref, pg_ref, gate_ref, ng_ref, nsc_ref, nsh_ref, y_ref, xn_ref, h_ref):
        y = _dot(a_ref[...], w_ref[...])
        y_ref[...] = y
        xn = x_ref[...] + (y * _rms(y)) * (res_w * gate_ref[...] * pg_ref[...])
        xn_ref[...] = xn
        h_ref[...] = ((xn * _rms(xn)) * (ng_ref[...] * (1.0 + nsc_ref[...])) + nsh_ref[...]).astype(BF16)

    return pl.pallas_call(
        body, name=name, grid=(T // tm,),
        out_shape=(jax.ShapeDtypeStruct((T, d), F32), jax.ShapeDtypeStruct((T, d), F32),
                   jax.ShapeDtypeStruct((T, d), BF16)),
        in_specs=[_row_spec(tm, ka), _whole_spec(w), _row_spec(tm, d)] + [_vec_spec(d)] * 5,
        out_specs=(_row_spec(tm, d),) * 3,
        compiler_params=_params("parallel"),
    )(a, w, x, post_g, gate, *nxt)


def _out_proj_last(a, w, x, post_g, gate, res_w, target, name, *, tm=512):
    T, ka = a.shape
    d = w.shape[1]
    tm = min(tm, T)

    def body(a_ref, w_ref, x_ref, pg_ref, gate_ref, t_ref, dx_ref, dy_ref, dgate_ref, dpost_ref, l_ref):
        _zero_at_start(dgate_ref, dpost_ref, l_ref)
        y = _dot(a_ref[...], w_ref[...])
        e = x_ref[...] + res_w * gate_ref[...] * (y * _rms(y) * pg_ref[...]) - t_ref[...]
        l_ref[...] += 0.5 * jnp.sum(_rowmean(e * e), axis=0, keepdims=True)
        dx = e * (1.0 / d)
        dx_ref[...] = dx
        dy_ref[...] = _postnorm_bwd_math(dx, y, pg_ref[...], gate_ref[...], res_w, dgate_ref, dpost_ref)

    return pl.pallas_call(
        body, name=name, grid=(T // tm,),
        out_shape=(jax.ShapeDtypeStruct((T, d), F32), jax.ShapeDtypeStruct((T, d), BF16),
                   jax.ShapeDtypeStruct((1, d), F32), jax.ShapeDtypeStruct((1, d), F32),
                   jax.ShapeDtypeStruct((1, 128), F32)),
        in_specs=[_row_spec(tm, ka), _whole_spec(w), _row_spec(tm, d), _vec_spec(d), _vec_spec(d), _row_spec(tm, d)],
        out_specs=(_row_spec(tm, d), _row_spec(tm, d), _vec_spec(d), _vec_spec(d),
                   pl.BlockSpec((1, 128), lambda m: (0, 0))),
        compiler_params=_params("arbitrary"),
    )(a, w, x, post_g, gate, target)


def _in_grad(dz, w, dxo, x, pre_g, scale, prev, name, *, tm=512, exchange=None):
    T, C = dz.shape
    d = w.shape[1]
    tm = min(tm, T)
    has_prev = prev is not None
    res_w = prev[3] if has_prev else None

    def body(*refs):
        dz_ref, w_ref, dxo_ref, x_ref, g_ref, sc_ref = refs[:6]
        if has_prev:
            yp_ref, ppg_ref, pgate_ref, dx_ref, dsh_ref, dsc_ref, dg_ref, dyp_ref, dgate_ref, dpost_ref = refs[6:]
            _zero_at_start(dsh_ref, dsc_ref, dg_ref, dgate_ref, dpost_ref)
        else:
            dx_ref, dsh_ref, dsc_ref, dg_ref = refs[6:]
            _zero_at_start(dsh_ref, dsc_ref, dg_ref)
        dh = _dot(dz_ref[...], w_ref[...])
        xv = x_ref[...]
        r = _rms(xv)
        xh = xv * r
        gain = 1.0 + sc_ref[...]
        both = _colsum(dh * xh)
        dsh_ref[...] += _colsum(dh)
        dsc_ref[...] += g_ref[...] * both
        dg_ref[...] += gain * both
        dxh = dh * (gain * g_ref[...])
        dx = dxo_ref[...] + r * (dxh - xh * _rowmean(dxh * xh))
        dx_ref[...] = dx
        if has_prev:
            dyp_ref[...] = _postnorm_bwd_math(dx, yp_ref[...], ppg_ref[...], pgate_ref[...], res_w,
                                               dgate_ref, dpost_ref)

    vec = jax.ShapeDtypeStruct((1, d), F32)
    in_specs = [_row_spec(tm, C), _whole_spec(w), _row_spec(tm, d), _row_spec(tm, d), _vec_spec(d), _vec_spec(d)]
    out_shape = [jax.ShapeDtypeStruct((T, d), F32), vec, vec, vec]
    out_specs = [_row_spec(tm, d), _vec_spec(d), _vec_spec(d), _vec_spec(d)]
    args = [dz, w, dxo, x, pre_g, scale]
    if has_prev:
        in_specs += [_row_spec(tm, d), _vec_spec(d), _vec_spec(d)]
        out_shape += [jax.ShapeDtypeStruct((T, d), BF16), vec, vec]
        out_specs += [_row_spec(tm, d), _vec_spec(d), _vec_spec(d)]
        args += list(prev[:3])
    body, x_args, x_in, x_out, x_scratch = _host(exchange, len(args), len(out_shape), body, _first_last(T // tm))
    return pl.pallas_call(
        body, name=name, grid=(T // tm,),
        out_shape=out_shape + x_out, in_specs=in_specs + x_in, out_specs=out_specs + x_in,
        scratch_shapes=x_scratch,
        compiler_params=_params("arbitrary"),
    )(*args, *x_args)


def _mm_wgrad(x, y, name, *, xw=None, yw=None, tt=2048):
    T, P = x.shape
    Q = y.shape[1]
    xw, yw = xw or P, yw or Q
    jx, jy = P // xw, Q // yw
    assert jx == 1 or jy == 1
    tt = min(tt, T)
    nt = T // tt

    def body(x_ref, y_ref, o_ref, acc_ref):
        t = pl.program_id(1)

        @pl.when(t == 0)
        def _():
            acc_ref[...] = jnp.zeros_like(acc_ref)

        acc_ref[...] += _dot(x_ref[...], y_ref[...], TN)

        @pl.when(t == nt - 1)
        def _():
            o_ref[...] = acc_ref[...].astype(BF16)

    if jy > 1:
        out_shape = jax.ShapeDtypeStruct((jy, P, yw), BF16)
        out_spec = pl.BlockSpec((None, P, yw), lambda j, t: (j, 0, 0))
    else:
        out_shape = jax.ShapeDtypeStruct((P, Q), BF16)
        out_spec = pl.BlockSpec((xw, Q), lambda j, t: (j, 0))
    return pl.pallas_call(
        body, name=name, grid=(max(jx, jy), nt),
        out_shape=out_shape,
        in_specs=[pl.BlockSpec((tt, xw), (lambda j, t: (t, j)) if jx > 1 else (lambda j, t: (t, 0))),
                  pl.BlockSpec((tt, yw), (lambda j, t: (t, j)) if jy > 1 else (lambda j, t: (t, 0)))],
        out_specs=out_spec,
        scratch_shapes=[pltpu.VMEM((xw, yw), F32)],
        compiler_params=_params("parallel", "arbitrary"),
    )(x, y)


def _ffn_in(h, wt, name, *, tm=512, opening=None, exchange=None):
    T, K = h.shape
    F = wt.shape[0] // 2
    tm = min(tm, T)
    n_vec = 3 if opening else 0

    def body(*refs):
        h_ref, w_ref = refs[0], refs[1 + n_vec]
        pq_ref, g_ref = refs[2 + n_vec], refs[3 + n_vec]
        hh = h_ref[...]
        if opening:
            g_vec, sc_vec, sh_vec = refs[1:4]
            hh = (hh * _rms(hh) * (g_vec[...] * (1.0 + sc_vec[...])) + sh_vec[...]).astype(BF16)
            refs[4 + n_vec][...] = hh
        for cols in _col_chunks(F):
            hi = slice(F + cols.start, F + cols.stop)
            a = _dot(hh, w_ref[cols], NT)
            b = _dot(hh, w_ref[hi], NT)
            s = _sigmoid_t(a)
            silu = a * s
            pq_ref[:, cols] = (b * (s * (1.0 + a * (1.0 - s)))).astype(BF16)
            pq_ref[:, hi] = silu.astype(BF16)
            g_ref[:, cols] = (silu * b).astype(BF16)

    outs = [jax.ShapeDtypeStruct((T, 2 * F), BF16), jax.ShapeDtypeStruct((T, F), BF16)]
    out_specs = [_row_spec(tm, 2 * F), _row_spec(tm, F)]
    if opening:
        outs.append(jax.ShapeDtypeStruct((T, K), BF16))
        out_specs.append(_row_spec(tm, K))
    vecs = list(opening) if opening else []
    body, x_args, x_in, x_out, x_scratch = _host(exchange, 2 + n_vec, len(outs), body, _first_last(T // tm))
    res = pl.pallas_call(
        body, name=name, grid=(T // tm,),
        out_shape=outs + x_out,
        in_specs=[_row_spec(tm, K)] + [_vec_spec(K)] * n_vec + [_whole_spec(wt)] + x_in,
        out_specs=out_specs + x_in,
        scratch_shapes=x_scratch,
        compiler_params=_params("arbitrary" if exchange else "parallel"),
    )(h, *vecs, wt, *x_args)
    return res[:len(outs)], res[len(outs):]


def _ffn_dgate(dy, w_out, pq, name, *, tm=512):
    T, N = dy.shape
    F = w_out.shape[0]
    tm = min(tm, T)

    def body(dy_ref, w_ref, pq_ref, dz_ref):
        dyv = dy_ref[...]
        for cols in _col_chunks(F):
            hi = slice(F + cols.start, F + cols.stop)
            dg = _dot(dyv, w_ref[cols], NT)
            dz_ref[:, cols] = (dg * pq_ref[:, cols].astype(F32)).astype(BF16)
            dz_ref[:, hi] = (dg * pq_ref[:, hi].astype(F32)).astype(BF16)

    return pl.pallas_call(
        body, name=name, grid=(T // tm,),
        out_shape=jax.ShapeDtypeStruct((T, 2 * F), BF16),
        in_specs=[_row_spec(tm, N), _whole_spec(w_out), _row_spec(tm, 2 * F)],
        out_specs=_row_spec(tm, 2 * F),
        compiler_params=_params("parallel"),
    )(dy, w_out, pq)


def _ada_fwd(c_all, w, b, name):
    L, K, n = w.shape

    def body(c_ref, w_ref, b_ref, o_ref):
        cv = c_ref[...]
        cond = cv * _sigmoid(cv)
        for l in range(L):
            o_ref[l] = _dot(cond, w_ref[l], precision=lax.Precision.HIGHEST) + b_ref[l]

    return pl.pallas_call(
        body, name=name,
        out_shape=jax.ShapeDtypeStruct((L, NDEV, n), F32),
        compiler_params=pltpu.CompilerParams(vmem_limit_bytes=VMEM_LIMIT),
    )(c_all, w, b)


def _ada_bwd(c_all_t, gmod, name):
    L, _, n = gmod.shape
    K = c_all_t.shape[0]

    def body(c_ref, g_ref, o_ref):
        cv = c_ref[...]
        cond = cv * _sigmoid(cv)
        for l in range(L):
            o_ref[l] = _dot(cond, g_ref[l], precision=lax.Precision.HIGHEST)

    return pl.pallas_call(
        body, name=name,
        out_shape=jax.ShapeDtypeStruct((L, K, n), F32),
        compiler_params=pltpu.CompilerParams(vmem_limit_bytes=VMEM_LIMIT),
    )(c_all_t, gmod)


def _tri(n, upper=False, block=None):
    r = lax.broadcasted_iota(jnp.int32, (n, n), 0)
    c = lax.broadcasted_iota(jnp.int32, (n, n), 1)
    m = (c >= r) if upper else (c <= r)
    if block is not None:
        m = m & ((r // block) == (c // block))
    return m.astype(BF16)


TRI_ROWS = 128


def _tri_dot(tri, x):
    hi = x.astype(BF16)
    lo = (x - hi.astype(F32)).astype(BF16)
    rows = x.shape[0]
    step = min(TRI_ROWS, rows)
    parts = [_dot(tri, hi[r:r + step]) + _dot(tri, lo[r:r + step]) for r in range(0, rows, step)]
    return parts[0] if len(parts) == 1 else jnp.concatenate(parts, axis=0)


def _hgrn_gates(proj_ref, lb_ref, jh):
    proj_ref = _ColBlocks(proj_ref, 512)
    lb = lb_ref[:, 512 * jh:512 * (jh + 1)]
    qp = proj_ref[jh]
    fx = proj_ref[2 + jh]
    sq = _sigmoid_t(qp)
    sig = _sigmoid_t(fx)
    f = lb + (1.0 - lb) * sig
    k = (1.0 - lb) * (1.0 - sig)
    return lb, qp, sq, sig, f, k


def _hgrn_fwd(proj, lb, out_norm, name, *, tb=512, exchange=None):
    T = proj.shape[0]
    tb = min(tb, T)
    nc = tb // HG_CHUNK
    lmat = _tri(min(TRI_ROWS, tb), block=HG_CHUNK)

    def body(proj_ref, lb_ref, on_ref, l_ref, o_ref, og_ref, st_ref, s_scr, b_scr):
        @pl.when(pl.program_id(0) == 0)
        def _():
            s_scr[...] = jnp.zeros_like(s_scr)

        r_i = lax.broadcasted_iota(jnp.int32, (HG_CHUNK, HG_CHUNK), 0)
        c_i = lax.broadcasted_iota(jnp.int32, (HG_CHUNK, HG_CHUNK), 1)
        causal = c_i <= r_i
        onv = on_ref[...]
        blocks = _ColBlocks(proj_ref, 512)
        for jh in range(2):
            lbv, qp, sq, sig, f, k = _hgrn_gates(proj_ref, lb_ref, jh)
            q = qp * sq
            b_half = b_scr.at[jh]
            b_half[...] = _tri_dot(l_ref[...], jnp.log(f))
            v = blocks[4 + jh]
            gp = blocks[6 + jh]
            gs = gp * _sigmoid_t(gp)
            for hh in range(4):
                hd = 4 * jh + hh
                cs = slice(HG_HEAD * hh, HG_HEAD * (hh + 1))
                for ci in range(nc):
                    r0 = HG_CHUNK * ci
                    rs = slice(r0, r0 + HG_CHUNK)
                    bc = b_half[rs, cs]
                    bm = b_half[r0 + HG_CHUNK // 2 - 1:r0 + HG_CHUNK // 2, cs]
                    bl = b_half[r0 + HG_CHUNK - 1:r0 + HG_CHUNK, cs]
                    qc, kc, vc = q[rs, cs], k[rs, cs], v[rs, cs].astype(BF16)
                    e_q, e_k = jnp.exp(bc - bm), jnp.exp(bm - bc)
                    qe = (qc * (e_q * jnp.exp(bm))).astype(BF16)
                    qt = (qc * e_q).astype(BF16)
                    kt = (kc * e_k).astype(BF16)
                    kd = (kc * (e_k * jnp.exp(bl - bm))).astype(BF16)
                    st = s_scr[hd]
                    stb = st.astype(BF16)
                    st_ref[ci, hd] = stb
                    a = jnp.where(causal, _dot(qt, kt, NT), 0.0).astype(BF16)
                    o = _dot(qe, stb, NT) + _dot(a, vc)
                    s_scr[hd] = st * jnp.exp(bl) + _dot(vc, kd, TN)
                    o_ref[rs, HG_HEAD * hd:HG_HEAD * (hd + 1)] = o
                    r = lax.rsqrt(_rowmean(o * o) + EPS)
                    og_ref[rs, HG_HEAD * hd:HG_HEAD * (hd + 1)] = (o * r * onv * gs[rs, cs]).astype(BF16)

    body, x_args, x_in, x_out, x_scratch = _host(exchange, 4, 3, body, _first_last(T // tb))
    res = pl.pallas_call(
        body, name=name, grid=(T // tb,),
        out_shape=[jax.ShapeDtypeStruct((T, D_MODEL), F32), jax.ShapeDtypeStruct((T, D_MODEL), BF16),
                   jax.ShapeDtypeStruct((T // HG_CHUNK, HG_HEADS, HG_HEAD, HG_HEAD), BF16)] + x_out,
        in_specs=[pl.BlockSpec((tb, 4 * D_MODEL), lambda i: (i, 0)),
                  pl.BlockSpec((1, D_MODEL), lambda i: (0, 0)),
                  pl.BlockSpec((1, HG_HEAD), lambda i: (0, 0)),
                  pl.BlockSpec(lmat.shape, lambda i: (0, 0))] + x_in,
        out_specs=[pl.BlockSpec((tb, D_MODEL), lambda i: (i, 0)),
                   pl.BlockSpec((tb, D_MODEL), lambda i: (i, 0)),
                   pl.BlockSpec((nc, HG_HEADS, HG_HEAD, HG_HEAD), lambda i: (i, 0, 0, 0))] + x_in,
        scratch_shapes=[pltpu.VMEM((HG_HEADS, HG_HEAD, HG_HEAD), F32), pltpu.VMEM((2, tb, 512), F32)] + x_scratch,
        compiler_params=_params("arbitrary"),
    )(proj, lb, out_norm, lmat, *x_args)
    return res[0], res[1], res[2], res[3:]


def _hgrn_bwd(proj, o, dog, states, lb, out_norm, name, *, tb=512):
    T = proj.shape[0]
    tb = min(tb, T)
    nc = tb // HG_CHUNK
    nb = T // tb
    lmat = _tri(min(TRI_ROWS, tb), block=HG_CHUNK)
    umat = _tri(min(TRI_ROWS, tb), upper=True, block=HG_CHUNK)

    def body(proj_ref, o_ref, dog_ref, st_ref, lb_ref, on_ref, l_ref, u_ref,
             dproj_ref, dlb_ref, don_ref, ds_scr, *half_scr):
        @pl.when(pl.program_id(0) == 0)
        def _():
            ds_scr[...] = jnp.zeros_like(ds_scr)
            dlb_ref[...] = jnp.zeros_like(dlb_ref)
            don_ref[...] = jnp.zeros_like(don_ref)

        r_i = lax.broadcasted_iota(jnp.int32, (HG_CHUNK, HG_CHUNK), 0)
        c_i = lax.broadcasted_iota(jnp.int32, (HG_CHUNK, HG_CHUNK), 1)
        causal = c_i <= r_i
        causal_t = r_i <= c_i
        last_row = lax.broadcasted_iota(jnp.int32, (HG_CHUNK, HG_HEAD), 0) == HG_CHUNK - 1
        onv = on_ref[...]
        don_acc = jnp.zeros((1, HG_HEAD), F32)
        blocks = _ColBlocks(proj_ref, 512)
        dproj_ref = _ColBlocks(dproj_ref, 512)
        for jh in range(2):
            b_scr, dq_scr, dk_scr, dv_scr, dg_scr, db_scr = [s.at[jh] for s in half_scr]
            lbv, qp, sq, sig, f, k = _hgrn_gates(proj_ref, lb_ref, jh)
            q = qp * sq
            b_scr[...] = _tri_dot(l_ref[...], jnp.log(f))
            v = blocks[4 + jh]
            gp = blocks[6 + jh]
            sg = _sigmoid_t(gp)
            for ci in reversed(range(nc)):
                r0 = HG_CHUNK * ci
                rs = slice(r0, r0 + HG_CHUNK)
                for hh in range(4):
                    hd = 4 * jh + hh
                    cs = slice(HG_HEAD * hh, HG_HEAD * (hh + 1))
                    hs = slice(HG_HEAD * hd, HG_HEAD * (hd + 1))
                    oc = o_ref[rs, hs]
                    r = lax.rsqrt(_rowmean(oc * oc) + EPS)
                    oh = oc * r
                    gc, sgc = gp[rs, cs], sg[rs, cs]
                    dogc = dog_ref[rs, hs].astype(F32)
                    don = dogc * (gc * sgc)
                    dg_scr[rs, cs] = dogc * (oh * onv) * (sgc * (1.0 + gc * (1.0 - sgc)))
                    don_acc += _colsum(don * oh)
                    donh = don * onv
                    do = (r * (donh - oh * _rowmean(donh * oh))).astype(BF16)
                    bc = b_scr[rs, cs]
                    bm = b_scr[r0 + HG_CHUNK // 2 - 1:r0 + HG_CHUNK // 2, cs]
                    bl = b_scr[r0 + HG_CHUNK - 1:r0 + HG_CHUNK, cs]
                    qc, kc, vc = q[rs, cs], k[rs, cs], v[rs, cs].astype(BF16)
                    e_q, e_k = jnp.exp(bc - bm), jnp.exp(bm - bc)
                    e_b, e_d = e_q * jnp.exp(bm), e_k * jnp.exp(bl - bm)
                    qe = (qc * e_b).astype(BF16)
                    qt = (qc * e_q).astype(BF16)
                    kt = (kc * e_k).astype(BF16)
                    kd = (kc * e_d).astype(BF16)
                    stb = st_ref[ci, hd]
                    dst = ds_scr[hd]
                    dstb = dst.astype(BF16)
                    a_t = jnp.where(causal_t, _dot(kt, qt, NT), 0.0).astype(BF16)
                    da = jnp.where(causal, _dot(do, vc, NT), 0.0).astype(BF16)
                    da_t = jnp.where(causal_t, _dot(vc, do, NT), 0.0).astype(BF16)
                    dv_scr[rs, cs] = _dot(a_t, do) + _dot(kd, dstb, NT)
                    dqe, dqt = _dot(do, stb), _dot(da, kt)
                    dkt, dkd = _dot(da_t, qt), _dot(vc, dstb)
                    dq_scr[rs, cs] = dqe * e_b + dqt * e_q
                    dk_scr[rs, cs] = dkt * e_k + dkd * e_d
                    e_l = jnp.exp(bl)
                    s_end = stb.astype(F32) * e_l + _dot(vc, kd, TN)
                    dbc = (qe.astype(F32) * dqe + qt.astype(F32) * dqt
                           - kt.astype(F32) * dkt - kd.astype(F32) * dkd)
                    db_scr[rs, cs] = dbc + jnp.where(last_row, _colsum(dstb.astype(F32) * s_end), 0.0)
                    ds_scr[hd] = dst * e_l + _dot(do, qe, TN)
            dq = dq_scr[...]
            dk = dk_scr[...]
            cols = slice(512 * jh, 512 * (jh + 1))
            dlogf = _tri_dot(u_ref[...], db_scr[...])
            one_m_sig = 1.0 - sig
            dsig = (1.0 - lbv) * sig * one_m_sig
            dboth = dlogf / f - dk
            dproj_ref[jh] = (dq * (sq * (1.0 + qp * (1.0 - sq)))).astype(BF16)
            dproj_ref[2 + jh] = (dboth * dsig).astype(BF16)
            dproj_ref[4 + jh] = dv_scr[...].astype(BF16)
            dproj_ref[6 + jh] = dg_scr[...].astype(BF16)
            dlb_ref[:, cols] += _colsum(dboth * one_m_sig)
        don_ref[...] += don_acc

    rev = lambda i: nb - 1 - i
    return pl.pallas_call(
        body, name=name, grid=(nb,),
        out_shape=(jax.ShapeDtypeStruct((T, 4 * D_MODEL), BF16), jax.ShapeDtypeStruct((1, D_MODEL), F32),
                   jax.ShapeDtypeStruct((1, HG_HEAD), F32)),
        in_specs=[pl.BlockSpec((tb, 4 * D_MODEL), lambda i: (rev(i), 0)),
                  pl.BlockSpec((tb, D_MODEL), lambda i: (rev(i), 0)),
                  pl.BlockSpec((tb, D_MODEL), lambda i: (rev(i), 0)),
                  pl.BlockSpec((nc, HG_HEADS, HG_HEAD, HG_HEAD), lambda i: (rev(i), 0, 0, 0)),
                  pl.BlockSpec((1, D_MODEL), lambda i: (0, 0)),
                  pl.BlockSpec((1, HG_HEAD), lambda i: (0, 0)),
                  pl.BlockSpec(lmat.shape, lambda i: (0, 0)),
                  pl.BlockSpec(lmat.shape, lambda i: (0, 0))],
        out_specs=(pl.BlockSpec((tb, 4 * D_MODEL), lambda i: (rev(i), 0)),
                   pl.BlockSpec((1, D_MODEL), lambda i: (0, 0)),
                   pl.BlockSpec((1, HG_HEAD), lambda i: (0, 0))),
        scratch_shapes=[pltpu.VMEM((HG_HEADS, HG_HEAD, HG_HEAD), F32)] + [pltpu.VMEM((2, tb, 512), F32)] * 6,
        compiler_params=_params("arbitrary"),
    )(proj, o, dog, states, lb, out_norm, lmat, umat)


def _gm_norm(pre_ref, lg_ref, lbias_ref):
    pre_ref = _ColBlocks(pre_ref, 768)
    vs = [pre_ref[4 + j].astype(F32) for j in range(4)]
    width = 4 * vs[0].shape[1]
    mu = sum(jnp.sum(v, axis=1, keepdims=True) for v in vs) / width
    ds = [v - mu for v in vs]
    var = sum(jnp.sum(d * d, axis=1, keepdims=True) for d in ds) / width
    rstd = lax.rsqrt(var + EPS)
    vhat = [d * rstd for d in ds]
    vn = [vhat[j] * lg_ref[j:j + 1, :] + lbias_ref[j:j + 1, :] for j in range(4)]
    return vhat, vn, rstd


def _gm_spatial_fwd(pre, ln_g, ln_b, ws, bsb, name, *, tb=256):
    T = pre.shape[0]
    tb = min(tb, T)
    nc = tb // GM_CHUNK

    def body(pre_ref, lg_ref, lbias_ref, ws_ref, bs_ref, o_ref):
        _, vn, _ = _gm_norm(pre_ref, lg_ref, lbias_ref)
        pre_ref, o_ref = _ColBlocks(pre_ref, 768), _ColBlocks(o_ref, 768)
        for j in range(4):
            u = pre_ref[j].astype(F32)
            for e in range(2):
                g = 2 * j + e
                cs = slice(GM_GDIM * e, GM_GDIM * (e + 1))
                wg = ws_ref[g].astype(BF16)
                for ci in range(nc):
                    rs = slice(GM_CHUNK * ci, GM_CHUNK * (ci + 1))
                    vm = _dot(wg, vn[j][rs, cs].astype(BF16)) + bs_ref[g]
                    o_ref[j, rs, cs] = (u[rs, cs] * vm).astype(BF16)

    return pl.pallas_call(
        body, name=name, grid=(T // tb,),
        out_shape=jax.ShapeDtypeStruct((T, 4 * 768), BF16),
        in_specs=[pl.BlockSpec((tb, 8 * 768), lambda i: (i, 0)),
                  pl.BlockSpec((4, 768), lambda i: (0, 0)),
                  pl.BlockSpec((4, 768), lambda i: (0, 0)),
                  pl.BlockSpec((GM_GROUPS, GM_CHUNK, GM_CHUNK), lambda i: (0, 0, 0)),
                  pl.BlockSpec((GM_GROUPS, GM_CHUNK, GM_GDIM), lambda i: (0, 0, 0))],
        out_specs=pl.BlockSpec((tb, 4 * 768), lambda i: (i, 0)),
        compiler_params=_params("parallel"),
    )(pre, ln_g, ln_b, ws, bsb)


def _gm_spatial_bwd(pre, gp, dm, ln_g, ln_b, ws, ws_t, bsb, name, *, tb=256):
    T = pre.shape[0]
    tb = min(tb, T)
    nc = tb // GM_CHUNK
    nb = T // tb

    def body(pre_ref, gp_ref, dm_ref, lg_ref, lbias_ref, ws_ref, wst_ref, bs_ref,
             dpre_ref, dws_ref, dbs_ref, dlg_ref, dlb_ref, dbin_ref, dbs_scr, dvn_scr, du_scr):
        i = pl.program_id(0)

        @pl.when(i == 0)
        def _():
            dws_ref[...] = jnp.zeros_like(dws_ref)
            dbs_scr[...] = jnp.zeros_like(dbs_scr)
            dlg_ref[...] = jnp.zeros_like(dlg_ref)
            dlb_ref[...] = jnp.zeros_like(dlb_ref)
            dbin_ref[...] = jnp.zeros_like(dbin_ref)

        vhat, vn, rstd = _gm_norm(pre_ref, lg_ref, lbias_ref)
        pre_ref, gp_ref, dm_ref = _ColBlocks(pre_ref, 768), _ColBlocks(gp_ref, 768), _ColBlocks(dm_ref, 768)
        dpre_ref = _ColBlocks(dpre_ref, 768)
        for j in range(4):
            u = pre_ref[j].astype(F32)
            for e in range(2):
                g = 2 * j + e
                cs = slice(GM_GDIM * e, GM_GDIM * (e + 1))
                wg = ws_ref[g].astype(BF16)
                wgt = wst_ref[g].astype(BF16)
                for ci in range(nc):
                    rs = slice(GM_CHUNK * ci, GM_CHUNK * (ci + 1))
                    vnb = vn[j][rs, cs].astype(BF16)
                    vm = _dot(wg, vnb) + bs_ref[g]
                    dmg = dm_ref[j, rs, cs].astype(F32)
                    du_scr[j, rs, cs] = dmg * vm
                    dvm = dmg * u[rs, cs]
                    dvmb = dvm.astype(BF16)
                    dws_ref[g] += _dot(dvmb, vnb, NT)
                    dbs_scr[g] += dvm
                    dvn_scr[j, rs, cs] = _dot(wgt, dvmb)
        width = 4 * 768
        dvh = []
        for j in range(4):
            dvn = dvn_scr[j]
            dlg_ref[j:j + 1, :] += _colsum(dvn * vhat[j])
            dlb_ref[j:j + 1, :] += _colsum(dvn)
            dvh.append(dvn * lg_ref[j:j + 1, :])
        m1 = sum(jnp.sum(d, axis=1, keepdims=True) for d in dvh) / width
        m2 = sum(jnp.sum(dvh[j] * vhat[j], axis=1, keepdims=True) for j in range(4)) / width
        for j in range(4):
            dv = rstd * (dvh[j] - m1 - vhat[j] * m2)
            dpv = dv * gp_ref[4 + j].astype(F32)
            dpu = du_scr[j] * gp_ref[j].astype(F32)
            dpre_ref[4 + j] = dpv.astype(BF16)
            dpre_ref[j] = dpu.astype(BF16)
            dbin_ref[4 + j:5 + j, :] += _colsum(dpv)
            dbin_ref[j:j + 1, :] += _colsum(dpu)

        @pl.when(i == nb - 1)
        def _():
            r_i = lax.broadcasted_iota(jnp.int32, (GM_CHUNK, GM_CHUNK), 0)
            c_i = lax.broadcasted_iota(jnp.int32, (GM_CHUNK, GM_CHUNK), 1)
            for g in range(GM_GROUPS):
                dws_ref[g] = jnp.where(c_i <= r_i, dws_ref[g], 0.0)
                dbs_ref[g] = jnp.broadcast_to(jnp.sum(dbs_scr[g], axis=1, keepdims=True), (GM_CHUNK, GM_CHUNK))

    sq = pl.BlockSpec((GM_GROUPS, GM_CHUNK, GM_CHUNK), lambda i: (0, 0, 0))
    v4 = pl.BlockSpec((4, 768), lambda i: (0, 0))
    return pl.pallas_call(
        body, name=name, grid=(nb,),
        out_shape=(jax.ShapeDtypeStruct((T, 8 * 768), BF16),
                   jax.ShapeDtypeStruct((GM_GROUPS, GM_CHUNK, GM_CHUNK), F32),
                   jax.ShapeDtypeStruct((GM_GROUPS, GM_CHUNK, GM_CHUNK), F32),
                   jax.ShapeDtypeStruct((4, 768), F32), jax.ShapeDtypeStruct((4, 768), F32),
                   jax.ShapeDtypeStruct((8, 768), F32)),
        in_specs=[pl.BlockSpec((tb, 8 * 768), lambda i: (i, 0)),
                  pl.BlockSpec((tb, 8 * 768), lambda i: (i, 0)),
                  pl.BlockSpec((tb, 4 * 768), lambda i: (i, 0)),
                  v4, v4, sq, sq,
                  pl.BlockSpec((GM_GROUPS, GM_CHUNK, GM_GDIM), lambda i: (0, 0, 0))],
        out_specs=(pl.BlockSpec((tb, 8 * 768), lambda i: (i, 0)), sq, sq, v4, v4,
                   pl.BlockSpec((8, 768), lambda i: (0, 0))),
        scratch_shapes=[pltpu.VMEM((GM_GROUPS, GM_CHUNK, GM_GDIM), F32),
                        pltpu.VMEM((4, tb, 768), F32), pltpu.VMEM((4, tb, 768), F32)],
        compiler_params=_params("arbitrary"),
    )(pre, gp, dm, ln_g, ln_b, ws, ws_t, bsb)


def _adamw(parts, w, m, v, name, *, tr=256):
    n = len(parts)
    S, R, C = parts[0].shape
    if R > tr:
        tr = next((t for t in range(tr, 15, -16) if R % t == 0), R)
    else:
        tr = R
    nr = R // tr
    bc1 = 1.0 - ADAM_B1 ** ADAM_STEP
    bc2 = 1.0 - ADAM_B2 ** ADAM_STEP

    def body(*refs):
        s_refs = refs[:n]
        w_ref, m_ref, v_ref, g_ref, d_ref, nm_ref, nv_ref = refs[n:]

        def update(s_ref):
            g = s_ref[0].astype(F32)
            for s in range(1, S):
                g = g + s_ref[s].astype(F32)
            mn = ADAM_B1 * m_ref[...] + (1.0 - ADAM_B1) * g
            vn = ADAM_B2 * v_ref[...] + (1.0 - ADAM_B2) * (g * g)
            g_ref[...] = g
            nm_ref[...] = mn
            nv_ref[...] = vn
            d_ref[...] = -ADAM_LR * ((mn / bc1) / (jnp.sqrt(vn / bc2) + ADAM_EPS) + ADAM_WD * w_ref[...])

        if n == 1:
            update(s_refs[0])
        else:
            for p in range(n):
                pl.when(pl.program_id(0) == p)(functools.partial(update, s_refs[p]))

    spec = pl.BlockSpec((tr, C), lambda k, i: (k * nr + i, 0))
    slot_specs = [pl.BlockSpec((S, tr, C), lambda k, i, p=p: (0, jnp.where(k == p, i, 0), 0)) for p in range(n)]
    return pl.pallas_call(
        body, name=name, grid=(n, nr),
        out_shape=(jax.ShapeDtypeStruct((n * R, C), F32),) * 4,
        in_specs=slot_specs + [spec, spec, spec],
        out_specs=(spec,) * 4,
        compiler_params=_params("arbitrary", "arbitrary"),
    )(*parts, w, m, v)


def _update(slots, w, m, v, name):
    shp = w.shape
    C = shp[-1]
    parts = list(slots) if isinstance(slots, (list, tuple)) else [slots]
    S = parts[0].shape[0]
    rows = math.prod(shp[:-1])
    parts = [p.reshape(S, rows // len(parts), C) for p in parts]
    outs = _adamw(parts, w.reshape(rows, C), m.reshape(rows, C), v.reshape(rows, C), name)
    return tuple(o.reshape(shp) for o in outs)


def kernel(x, c, ada_w, ada_b, norm_pre, norm_post, ffn_w_in, ffn_w_out, hg_w_in, hg_w_out, hg_out_norm, hg_lb, gm_w_in, gm_b_in, gm_ln_g, gm_ln_b, gm_w_s, gm_b_s, gm_w_out, loss_target, m_ada_w, m_ada_b, m_norm_pre, m_norm_post, m_ffn_w_in, m_ffn_w_out, m_hg_w_in, m_hg_w_out, m_hg_out_norm, m_hg_lb, m_gm_w_in, m_gm_b_in, m_gm_ln_g, m_gm_ln_b, m_gm_w_s, m_gm_b_s, m_gm_w_out, v_ada_w, v_ada_b, v_norm_pre, v_norm_post, v_ffn_w_in, v_ffn_w_out, v_hg_w_in, v_hg_w_out, v_hg_out_norm, v_hg_lb, v_gm_w_in, v_gm_b_in, v_gm_ln_g, v_gm_ln_b, v_gm_w_s, v_gm_b_s, v_gm_w_out):
    me = 4 * lax.axis_index("x") + 2 * lax.axis_index("y") + lax.axis_index("c")
    T = x.shape[1]
    x0 = x.reshape(T, D_MODEL)
    target = loss_target.reshape(T, D_MODEL)
    n_ada = ada_w.shape[-1]

    pack = jnp.concatenate([
        c.reshape(8, 128), norm_pre.reshape(6, 128), norm_post.reshape(6, 128),
        gm_b_in.reshape(6, 128), gm_ln_g.reshape(3, 128), gm_ln_b.reshape(3, 128)], axis=0)
    packs = _all_gather(pack, "gather_small")
    c_all = packs[:, 0:8].reshape(NDEV, D_MODEL)
    npre = packs[:, 8:14].reshape(NDEV, 2, 3, 128).transpose(1, 2, 0, 3).reshape(2, 3, D_MODEL)
    npost = packs[:, 14:20].reshape(NDEV, 2, 3, 128).transpose(1, 2, 0, 3).reshape(2, 3, D_MODEL)
    b_in = packs[:, 20:26].reshape(1, NDEV * 768)
    ln_g = packs[:, 26:29].reshape(4, 768)
    ln_b = packs[:, 29:32].reshape(4, 768)

    ada_b_mine = lax.dynamic_slice_in_dim(ada_b, me * n_ada, n_ada, axis=1).reshape(2, 1, n_ada)
    mod_cols = _ada_fwd(c_all, ada_w, ada_b_mine, "ada_fwd")
    mod_all = _all_gather(mod_cols, "gather_mod")
    mod = lax.dynamic_index_in_dim(mod_all, me, axis=2, keepdims=False)
    mod = mod.transpose(1, 0, 2).reshape(2, 9, 1, D_MODEL)

    sh_fi, sh_fo = ffn_w_in.astype(BF16).swapaxes(-1, -2), ffn_w_out.astype(BF16)
    sh_hi, sh_ho = hg_w_in[0].astype(BF16).T, hg_w_out[0].astype(BF16)
    sh_mi, sh_mo = gm_w_in[0].astype(BF16).T, gm_w_out[0].astype(BF16)

    def whole(gathered):
        return gathered.reshape(-1, D_MODEL)

    w_fi = {(0, 0): whole(_all_gather(sh_fi[0, 0], "gather_ffn_in_first"))}
    w_fo = {}
    riders = {"l0s0": [sh_fo[0, 0], sh_hi, sh_ho], "l0s1": [sh_fi[0, 1], sh_fo[0, 1]], "hg_mix": [sh_mi, sh_mo],
              "l0s2": [sh_fi[1, 0], sh_fo[1, 0]], "l1s0": [sh_fi[1, 1], sh_fo[1, 1]]}

    sm = jax.nn.softmax(hg_lb, axis=0)
    lb0 = sm[0:1]
    on = hg_out_norm.reshape(1, HG_HEAD)
    tril = jnp.tril(jnp.ones((GM_CHUNK, GM_CHUNK), F32))
    ws = gm_w_s[0] * tril[None]
    ws_t = ws.transpose(0, 2, 1)
    bsb = jnp.broadcast_to(gm_b_s[0][:, :, None], (GM_GROUPS, GM_CHUNK, GM_GDIM))

    res_ws = (0.5, 1.0, 0.5)

    def vecs(i, s):
        return (npre[i, s].reshape(1, D_MODEL), npost[i, s].reshape(1, D_MODEL),
                mod[i, 3 * s], mod[i, 3 * s + 1], mod[i, 3 * s + 2])

    order = [(i, s) for i in range(2) for s in range(3)]
    saved = {}
    xs = x0
    for pos, (i, s) in enumerate(order):
        tag = f"l{i}s{s}"
        pre_g, post_g, shift, scale, gate = vecs(i, s)
        rider = _Exchange("gather", riders[tag]) if tag in riders else None
        if s != 1:
            if pos == 0:
                (pq, a, h), got = _ffn_in(xs, w_fi[0, 0], "ffn_in_" + tag, opening=(pre_g, scale, shift), exchange=rider)
            else:
                (pq, a), got = _ffn_in(h, w_fi[i, s // 2], "ffn_in_" + tag, exchange=rider)
            extra = (pq, a)
            if tag == "l0s0":
                w_fo[0, 0], w_hi, w_ho = map(whole, got)
            elif tag == "l0s2":
                w_fi[1, 0], w_fo[1, 0] = map(whole, got)
            elif tag == "l1s0":
                w_fi[1, 1], w_fo[1, 1] = map(whole, got)
            wo = w_fo[i, s // 2]
        elif i == 0:
            proj, *got = _mm_blocks(h, w_hi, "hg_in", out_dtype=F32, exchange=rider)
            w_fi[0, 1], w_fo[0, 1] = map(whole, got)
            o, og, states, got = _hgrn_fwd(proj, lb0, on, "hg_mix", exchange=_Exchange("gather", riders["hg_mix"]))
            w_mi, w_mo = map(whole, got)
            a, wo = og, w_ho
            extra = (proj, o, og, states)
        else:
            pre, gp = _mm_blocks(h, w_mi, "gm_in", bias=b_in, gelu=True)
            a = _gm_spatial_fwd(pre, ln_g, ln_b, ws, bsb, "gm_mix")
            wo = w_mo
            extra = (pre, gp, a)
        if pos + 1 < len(order):
            npre_g, _, nshift, nscale, _ = vecs(*order[pos + 1])
            y, x_next, h_next = _out_proj(a, wo, xs, post_g, gate, res_ws[s], (npre_g, nscale, nshift), "out_" + tag)
            saved[tag] = (xs, h, y) + extra
            xs, h = x_next, h_next
        else:
            dx, dy, dgate, dpost, loss_part = _out_proj_last(a, wo, xs, post_g, gate, res_ws[s], target, "out_" + tag)
            saved[tag] = (xs, h, None) + extra
    loss = lax.psum(loss_part[0, 0], ("x", "y", "c"))

    slots = {}
    d_npre = [[None] * 3, [None] * 3]
    d_npost = [[None] * 3, [None] * 3]
    d_mod = [[None] * 9, [None] * 9]
    for pos in reversed(range(len(order))):
        i, s = order[pos]
        tag = f"l{i}s{s}"
        pre_g, _, _, scale, _ = vecs(i, s)
        xin, h = saved[tag][:2]
        if s != 1:
            w_in, wo = w_fi[i, s // 2], w_fo[i, s // 2]
            pq, g = saved[tag][3:]
            dz = _ffn_dgate(dy, wo, pq, "ffn_dgate_" + tag)
            g_out = _mm_wgrad(g, dy, "ffn_out_wgrad_" + tag, xw=1408)
            g_in = _mm_wgrad(dz, h, "ffn_in_wgrad_" + tag, xw=1408)
        elif i == 0:
            proj, o, og, states = saved[tag][3:]
            dog = _mm_blocks(dy, w_ho, "hg_out_dgrad")
            g_out = _mm_wgrad(og, dy, "hg_out_wgrad")
            dz, d_lb0, d_on = _hgrn_bwd(proj, o, dog, states, lb0, on, "hg_mix_bwd")
            w_in = w_hi
            g_in = _mm_wgrad(h, dz, "hg_in_wgrad", yw=512)
        else:
            pre, gp, sp = saved[tag][3:]
            dm = _mm_blocks(dy, w_mo, "gm_out_dgrad")
            g_out = _mm_wgrad(sp, dy, "gm_out_wgrad", xw=768)
            dz, d_ws, d_bs, d_lg, d_lbias, d_bin = _gm_spatial_bwd(pre, gp, dm, ln_g, ln_b, ws, ws_t, bsb, "gm_mix_bwd")
            w_in = w_mi
            g_in = _mm_wgrad(h, dz, "gm_in_wgrad", yw=768)
        g_out = g_out.reshape(NDEV, -1, D_MODEL)
        g_in = g_in.reshape(NDEV, -1, D_MODEL) if s != 1 else g_in
        d_npost[i][s] = dpost
        d_mod[i][3 * s + 2] = dgate
        rider = _Exchange("scatter", [g_in, g_out])
        if pos > 0:
            pi, ps = order[pos - 1]
            _, ppost_g, _, _, pgate = vecs(pi, ps)
            prev = (saved[f"l{pi}s{ps}"][2], ppost_g, pgate, res_ws[ps])
            dx, dshift, dscale, dpre_g, dy, dgate, dpost, r_in, r_out = _in_grad(
                dz, w_in, dx, xin, pre_g, scale, prev, "in_grad_" + tag, exchange=rider)
        else:
            dx, dshift, dscale, dpre_g, r_in, r_out = _in_grad(
                dz, w_in, dx, xin, pre_g, scale, None, "in_grad_" + tag, exchange=rider)
        slots[tag] = (r_in, r_out)
        d_npre[i][s] = dpre_g
        d_mod[i][3 * s], d_mod[i][3 * s + 1] = dshift, dscale
    grad_x = dx.reshape(x.shape)

    ffn_tags = ["l0s0", "l0s2", "l1s0", "l1s2"]
    s_fi = [slots[t][0].swapaxes(-1, -2) for t in ffn_tags]
    s_fo = [slots[t][1] for t in ffn_tags]
    (s_hi, s_ho), (s_mi, s_mo) = slots["l0s1"], slots["l1s1"]
    s_hi, s_ho, s_mi, s_mo = s_hi[:, None], s_ho[:, None], s_mi[:, None], s_mo[:, None]

    gmod = jnp.stack([jnp.concatenate(d_mod[i], axis=0) for i in range(2)])
    d_sm = lb0 * d_lb0
    d_hg_lb = jnp.concatenate([d_sm, jnp.zeros((2, D_MODEL), F32)], axis=0) - sm * d_sm
    small = [gmod, jnp.stack([jnp.concatenate(r, axis=0) for r in d_npre]),
             jnp.stack([jnp.concatenate(r, axis=0) for r in d_npost]),
             d_on, d_hg_lb, d_bin, d_lg, d_lbias, d_ws, d_bs[:, :, 0]]
    sizes = [a.size for a in small]
    flat = jnp.concatenate([a.reshape(-1) for a in small])
    rows = -(-flat.size // (8 * 128)) * 8
    flat = jnp.pad(flat, (0, rows * 128 - flat.size)).reshape(rows, 128)
    flats = _all_gather(flat, "gather_small_grads").reshape(NDEV, rows * 128)
    parts, off = [], 0
    for a, n in zip(small, sizes):
        parts.append(flats[:, off:off + n].reshape((NDEV,) + a.shape))
        off += n
    p_mod, p_npre, p_npost, p_on, p_lb, p_bin, p_lg, p_lbias, p_ws, p_bs = parts

    def mine(p, width):
        return lax.dynamic_slice_in_dim(p, me * width, width, axis=p.ndim - 1)

    gmod_cols = mine(p_mod.reshape(NDEV, 2, 9 * D_MODEL), n_ada).transpose(1, 0, 2)
    g_ada_w = _ada_bwd(jnp.pad(c_all.T, ((0, 0), (0, 120))), jnp.pad(gmod_cols, ((0, 0), (0, 120), (0, 0))), "ada_bwd")

    out = {}
    out["ada_w"] = _update(g_ada_w[None], ada_w, m_ada_w, v_ada_w, "adamw_ada_w")
    out["ada_b"] = _update(p_mod.reshape(NDEV, 2, 9 * D_MODEL), ada_b, m_ada_b, v_ada_b, "adamw_ada_b")
    out["norm_pre"] = _update(mine(p_npre, 128), norm_pre, m_norm_pre, v_norm_pre, "adamw_norm_pre")
    out["norm_post"] = _update(mine(p_npost, 128), norm_post, m_norm_post, v_norm_post, "adamw_norm_post")
    out["ffn_w_in"] = _update(s_fi, ffn_w_in, m_ffn_w_in, v_ffn_w_in, "adamw_ffn_in")
    out["ffn_w_out"] = _update(s_fo, ffn_w_out, m_ffn_w_out, v_ffn_w_out, "adamw_ffn_out")
    out["hg_w_in"] = _update(s_hi, hg_w_in, m_hg_w_in, v_hg_w_in, "adamw_hg_in")
    out["hg_w_out"] = _update(s_ho, hg_w_out, m_hg_w_out, v_hg_w_out, "adamw_hg_out")
    out["hg_out_norm"] = _update(p_on, hg_out_norm, m_hg_out_norm, v_hg_out_norm, "adamw_hg_norm")
    out["hg_lb"] = _update(p_lb, hg_lb, m_hg_lb, v_hg_lb, "adamw_hg_lb")
    out["gm_w_in"] = _update(s_mi, gm_w_in, m_gm_w_in, v_gm_w_in, "adamw_gm_in")
    out["gm_b_in"] = _update(mine(p_bin.reshape(NDEV, 1, 8 * 768), 768), gm_b_in, m_gm_b_in, v_gm_b_in, "adamw_gm_b_in")
    out["gm_ln_g"] = _update(mine(p_lg.reshape(NDEV, 1, 4 * 768), 384), gm_ln_g, m_gm_ln_g, v_gm_ln_g, "adamw_gm_ln_g")
    out["gm_ln_b"] = _update(mine(p_lbias.reshape(NDEV, 1, 4 * 768), 384), gm_ln_b, m_gm_ln_b, v_gm_ln_b, "adamw_gm_ln_b")
    out["gm_w_s"] = _update(p_ws[:, None], gm_w_s, m_gm_w_s, v_gm_w_s, "adamw_gm_w_s")
    out["gm_b_s"] = _update(p_bs[:, None], gm_b_s, m_gm_b_s, v_gm_b_s, "adamw_gm_b_s")
    out["gm_w_out"] = _update(s_mo, gm_w_out, m_gm_w_out, v_gm_w_out, "adamw_gm_out")

    names = ["ada_w", "ada_b", "norm_pre", "norm_post", "ffn_w_in", "ffn_w_out", "hg_w_in", "hg_w_out",
             "hg_out_norm", "hg_lb", "gm_w_in", "gm_b_in", "gm_ln_g", "gm_ln_b", "gm_w_s", "gm_b_s", "gm_w_out"]
    return (loss, grad_x, *[out[n][0] for n in names], *[out[n][1] for n in names],
            *[out[n][2] for n in names], *[out[n][3] for n in names])
```

```python
import math

import jax
import jax.numpy as jnp
from jax import lax
from jax.experimental import pallas as pl
from jax.experimental.pallas import tpu as pltpu

F32 = jnp.float32
BF16 = jnp.bfloat16
NDEV = 8
D_MODEL = 1024
EPS = 1e-6
HG_CHUNK = 64
HG_HEAD = 128
HG_HEADS = 8
GM_CHUNK = 128
GM_GDIM = 384
GM_GROUPS = 8
ADAM_LR = 0.001
ADAM_B1 = 0.9
ADAM_B2 = 0.999
ADAM_EPS = 1e-08
ADAM_WD = 0.01
ADAM_STEP = 10
VMEM_LIMIT = 56 * 2 ** 20

NN = (((1,), (0,)), ((), ()))
NT = (((1,), (1,)), ((), ()))
TN = (((0,), (0,)), ((), ()))
MESH = pl.DeviceIdType.MESH
ANY = pl.BlockSpec(memory_space=pl.ANY)


def _dot(a, b, dims=NN, precision=None):
    return lax.dot_general(a, b, dims, preferred_element_type=F32, precision=precision)


def _params(*sem):
    return pltpu.CompilerParams(dimension_semantics=sem, vmem_limit_bytes=VMEM_LIMIT)


def _sigmoid(x):
    return 1.0 / (1.0 + jnp.exp(-x))


def _sigmoid_t(x):
    return 0.5 * jnp.tanh(0.5 * x) + 0.5


def _gelu_and_grad(x):
    c = math.sqrt(2.0 / math.pi)
    m = (c * 0.044715) * (x * x)
    t = jnp.tanh(x * (c + m))
    hp = 0.5 + 0.5 * t
    return x * hp, hp * (1.0 + (x * (1.0 - t)) * (c + 3.0 * m))


def _colsum(x):
    return jnp.sum(x, axis=0, keepdims=True)


def _rowmean(x):
    return jnp.mean(x, axis=-1, keepdims=True)


def _all_gather(shard, name):
    def body(x_ref, out_ref, send_sems, recv_sems, local_sem):
        x, y, c = lax.axis_index("x"), lax.axis_index("y"), lax.axis_index("c")
        me, sibling = (x, y, c), (x, y, 1 - c)
        chips = [(1 - x, y), (x, 1 - y), (1 - x, 1 - y)]

        def slot(p):
            return out_ref.at[4 * p[0] + 2 * p[1] + p[2]]

        def copy(k, block, to, src=None):
            return pltpu.make_async_remote_copy(
                src_ref=slot(block) if src is None else src, dst_ref=slot(block),
                send_sem=send_sems.at[k], recv_sem=recv_sems.at[k],
                device_id=to, device_id_type=MESH)

        mine = pltpu.make_async_copy(x_ref, slot(me), local_sem)
        mine.start()
        first = [copy(0, me, sibling, src=x_ref)]
        first += [copy(1 + j, me, (*chip, c), src=x_ref) for j, chip in enumerate(chips)]
        for cp in first:
            cp.start()
        passed = [copy(4 + j, (*chip, c), sibling) for j, chip in enumerate(chips)]
        for j, chip in enumerate(chips):
            copy(1 + j, (*chip, c), me).wait_recv()
            passed[j].start()
        copy(0, sibling, me).wait_recv()
        for j, chip in enumerate(chips):
            copy(4 + j, (*chip, 1 - c), me).wait_recv()
        for cp in first + passed:
            cp.wait_send()
        mine.wait()

    return pl.pallas_call(
        body, name=name,
        out_shape=jax.ShapeDtypeStruct((NDEV,) + shard.shape, shard.dtype),
        in_specs=[ANY], out_specs=ANY,
        scratch_shapes=[pltpu.SemaphoreType.DMA((7,)), pltpu.SemaphoreType.DMA((7,)),
                        pltpu.SemaphoreType.DMA(())],
    )(shard)


class _Exchange:
    def __init__(self, kind, arrays):
        self.gather = kind == "gather"
        self.arrays = list(arrays)
        self.n = n = len(self.arrays)
        self.out_shape = [jax.ShapeDtypeStruct(((NDEV,) + a.shape) if self.gather else a.shape, a.dtype)
                          for a in self.arrays]
        self.scratch = [pltpu.SemaphoreType.DMA((n, NDEV - 1)), pltpu.SemaphoreType.DMA((n, NDEV - 1)),
                        pltpu.SemaphoreType.DMA((n,))]

    def _copies(self, in_refs, out_refs, sems):
        send_sems, recv_sems, local_sems = sems
        x, y, c = lax.axis_index("x"), lax.axis_index("y"), lax.axis_index("c")
        me = 4 * x + 2 * y + c
        peers = [(1 - x if k & 4 else x, 1 - y if k & 2 else y, 1 - c if k & 1 else c) for k in range(1, NDEV)]
        local, send, recv = [], [], []
        for a in range(self.n):
            src = (lambda pid, a=a: in_refs[a]) if self.gather else (lambda pid, a=a: in_refs[a].at[pid])
            local.append(pltpu.make_async_copy(src(me), out_refs[a].at[me], local_sems.at[a]))
            for k, p in enumerate(peers):
                pid = 4 * p[0] + 2 * p[1] + p[2]
                for lst, slot in ((send, me), (recv, pid)):
                    lst.append(pltpu.make_async_remote_copy(
                        src_ref=src(pid), dst_ref=out_refs[a].at[slot],
                        send_sem=send_sems.at[a, k], recv_sem=recv_sems.at[a, k],
                        device_id=p, device_id_type=MESH))
        return local, send, recv

    def start(self, first, in_refs, out_refs, sems):
        @pl.when(first)
        def _():
            local, send, _ = self._copies(in_refs, out_refs, sems)
            for cp in local + send:
                cp.start()

    def finish(self, last, in_refs, out_refs, sems):
        @pl.when(last)
        def _():
            local, send, recv = self._copies(in_refs, out_refs, sems)
            for cp in send:
                cp.wait_send()
            for cp in recv:
                cp.wait_recv()
            for cp in local:
                cp.wait()


def _host(exchange, n_in, n_out, body, first_last):
    if exchange is None:
        return body, [], [], [], []
    n = exchange.n

    def hosted(*refs):
        ins, refs = refs[:n_in], refs[n_in:]
        xin, refs = refs[:n], refs[n:]
        outs, refs = refs[:n_out], refs[n_out:]
        xout, refs = refs[:n], refs[n:]
        scratch, sems = refs[:len(refs) - 3], refs[len(refs) - 3:]
        first, last = first_last()
        exchange.start(first, xin, xout, sems)
        body(*ins, *outs, *scratch)
        exchange.finish(last, xin, xout, sems)

    return hosted, exchange.arrays, [ANY] * n, exchange.out_shape, exchange.scratch


def _first_last(steps):
    def at():
        i = pl.program_id(0)
        return i == 0, i == steps - 1
    return at


class _ColBlocks:
    def __init__(self, ref, width):
        self.ref, self.width = ref, width

    def _index(self, key):
        key = key if isinstance(key, tuple) else (key,)
        rows = key[1] if len(key) > 1 else slice(None)
        cols = key[2] if len(key) > 2 else slice(0, self.width)
        c0 = key[0] * self.width
        return rows, slice(c0 + cols.start, c0 + cols.stop)

    def __getitem__(self, key):
        return self.ref[self._index(key)]

    def __setitem__(self, key, value):
        self.ref[self._index(key)] = value


def _col_chunks(width, chunk=768):
    return [slice(c, min(c + chunk, width)) for c in range(0, width, chunk)]


def _row_spec(tm, d):
    return pl.BlockSpec((tm, d), lambda m: (m, 0))


def _vec_spec(d):
    return pl.BlockSpec((1, d), lambda m: (0, 0))


def _whole_spec(w):
    nd = w.ndim
    return pl.BlockSpec(w.shape, lambda m: (0,) * nd, pipeline_mode=pl.Buffered(1))


def _mm_blocks(a, w, name, *, bias=None, out_dtype=BF16, tm=512, chunk=768, gelu=False, exchange=None):
    T, K = a.shape
    C = w.shape[0]
    tm = min(tm, T)
    n_in = 2 + (bias is not None)

    def body(*refs):
        a_ref, w_ref = refs[:2]
        av = a_ref[...]
        for cols in _col_chunks(C, chunk):
            r = _dot(av, w_ref[cols], NT)
            if bias is not None:
                r = r + refs[2][:, cols]
            if gelu:
                z, dz = _gelu_and_grad(r)
                refs[n_in][:, cols] = z.astype(BF16)
                refs[n_in + 1][:, cols] = dz.astype(BF16)
            else:
                refs[n_in][:, cols] = r.astype(out_dtype)

    in_specs = [_row_spec(tm, K), _whole_spec(w)]
    args = [a, w]
    if bias is not None:
        in_specs.append(_whole_spec(bias))
        args.append(bias)
    outs = [jax.ShapeDtypeStruct((T, C), BF16)] * 2 if gelu else [jax.ShapeDtypeStruct((T, C), out_dtype)]
    body, x_args, x_in, x_out, x_scratch = _host(exchange, n_in, len(outs), body, _first_last(T // tm))
    res = pl.pallas_call(
        body, name=name, grid=(T // tm,),
        out_shape=outs + x_out,
        in_specs=in_specs + x_in,
        out_specs=[_row_spec(tm, C)] * len(outs) + x_in,
        scratch_shapes=x_scratch,
        compiler_params=_params("arbitrary" if exchange else "parallel"),
    )(*args, *x_args)
    return res if (exchange or gelu) else res[0]


def _rms(v):
    return lax.rsqrt(_rowmean(v * v) + EPS)


def _zero_at_start(*refs):
    @pl.when(pl.program_id(0) == 0)
    def _():
        for r in refs:
            r[...] = jnp.zeros_like(r)


def _postnorm_bwd_math(dxo, yv, g, gate, res_w, dgate_ref, dpost_ref):
    r = _rms(yv)
    yh = yv * r
    both = res_w * _colsum(dxo * yh)
    dgate_ref[...] += g * both
    dpost_ref[...] += gate * both
    dyh = dxo * (res_w * gate * g)
    return (r * (dyh - yh * _rowmean(dyh * yh))).astype(BF16)


def _out_proj(a, w, x, post_g, gate, res_w, nxt, name, *, tm=512):
    T, ka = a.shape
    d = w.shape[1]
    tm = min(tm, T)

    def body(a_ref, w_ref, x_ref, pg_ref, gate_ref, ng_ref, nsc_ref, nsh_ref, y_ref, xn_ref, h_ref):
        y = _dot(a_ref[...], w_ref[...])
        y_ref[...] = y
        xn = x_ref[...] + (y * _rms(y)) * (res_w * gate_ref[...] * pg_ref[...])
        xn_ref[...] = xn
        h_ref[...] = ((xn * _rms(xn)) * (ng_ref[...] * (1.0 + nsc_ref[...])) + nsh_ref[...]).astype(BF16)

    return pl.pallas_call(
        body, name=name, grid=(T // tm,),
        out_shape=(jax.ShapeDtypeStruct((T, d), F32), jax.ShapeDtypeStruct((T, d), F32),
                   jax.ShapeDtypeStruct((T, d), BF16)),
        in_specs=[_row_spec(tm, ka), _whole_spec(w), _row_spec(tm, d)] + [_vec_spec(d)] * 5,
        out_specs=(_row_spec(tm, d),) * 3,
        compiler_params=_params("parallel"),
    )(a, w, x, post_g, gate, *nxt)


def _out_proj_last(a, w, x, post_g, gate, res_w, target, name, *, tm=512):
    T, ka = a.shape
    d = w.shape[1]
    tm = min(tm, T)

    def body(a_ref, w_ref, x_ref, pg_ref, gate_ref, t_ref, dx_ref, dy_ref, dgate_ref, dpost_ref, l_ref):
        _zero_at_start(dgate_ref, dpost_ref, l_ref)
        y = _dot(a_ref[...], w_ref[...])
        e = x_ref[...] + res_w * gate_ref[...] * (y * _rms(y) * pg_ref[...]) - t_ref[...]
        l_ref[...] += 0.5 * jnp.sum(_rowmean(e * e), axis=0, keepdims=True)
        dx = e * (1.0 / d)
        dx_ref[...] = dx
        dy_ref[...] = _postnorm_bwd_math(dx, y, pg_ref[...], gate_ref[...], res_w, dgate_ref, dpost_ref)

    return pl.pallas_call(
        body, name=name, grid=(T // tm,),
        out_shape=(jax.ShapeDtypeStruct((T, d), F32), jax.ShapeDtypeStruct((T, d), BF16),
                   jax.ShapeDtypeStruct((1, d), F32), jax.ShapeDtypeStruct((1, d), F32),
                   jax.ShapeDtypeStruct((1, 128), F32)),
        in_specs=[_row_spec(tm, ka), _whole_spec(w), _row_spec(tm, d), _vec_spec(d), _vec_spec(d), _row_spec(tm, d)],
        out_specs=(_row_spec(tm, d), _row_spec(tm, d), _vec_spec(d), _vec_spec(d),
                   pl.BlockSpec((1, 128), lambda m: (0, 0))),
        compiler_params=_params("arbitrary"),
    )(a, w, x, post_g, gate, target)


def _in_grad(dz, w, dxo, x, pre_g, scale, prev, name, *, tm=512, exchange=None):
    T, C = dz.shape
    d = w.shape[1]
    tm = min(tm, T)
    has_prev = prev is not None
    res_w = prev[3] if has_prev else None

    def body(*refs):
        dz_ref, w_ref, dxo_ref, x_ref, g_ref, sc_ref = refs[:6]
        if has_prev:
            yp_ref, ppg_ref, pgate_ref, dx_ref, dsh_ref, dsc_ref, dg_ref, dyp_ref, dgate_ref, dpost_ref = refs[6:]
            _zero_at_start(dsh_ref, dsc_ref, dg_ref, dgate_ref, dpost_ref)
        else:
            dx_ref, dsh_ref, dsc_ref, dg_ref = refs[6:]
            _zero_at_start(dsh_ref, dsc_ref, dg_ref)
        dh = _dot(dz_ref[...], w_ref[...])
        xv = x_ref[...]
        r = _rms(xv)
        xh = xv * r
        gain = 1.0 + sc_ref[...]
        both = _colsum(dh * xh)
        dsh_ref[...] += _colsum(dh)
        dsc_ref[...] += g_ref[...] * both
        dg_ref[...] += gain * both
        dxh = dh * (gain * g_ref[...])
        dx = dxo_ref[...] + r * (dxh - xh * _rowmean(dxh * xh))
        dx_ref[...] = dx
        if has_prev:
            dyp_ref[...] = _postnorm_bwd_math(dx, yp_ref[...], ppg_ref[...], pgate_ref[...], res_w,
                                               dgate_ref, dpost_ref)

    vec = jax.ShapeDtypeStruct((1, d), F32)
    in_specs = [_row_spec(tm, C), _whole_spec(w), _row_spec(tm, d), _row_spec(tm, d), _vec_spec(d), _vec_spec(d)]
    out_shape = [jax.ShapeDtypeStruct((T, d), F32), vec, vec, vec]
    out_specs = [_row_spec(tm, d), _vec_spec(d), _vec_spec(d), _vec_spec(d)]
    args = [dz, w, dxo, x, pre_g, scale]
    if has_prev:
        in_specs += [_row_spec(tm, d), _vec_spec(d), _vec_spec(d)]
        out_shape += [jax.ShapeDtypeStruct((T, d), BF16), vec, vec]
        out_specs += [_row_spec(tm, d), _vec_spec(d), _vec_spec(d)]
        args += list(prev[:3])
    body, x_args, x_in, x_out, x_scratch = _host(exchange, len(args), len(out_shape), body, _first_last(T // tm))
    return pl.pallas_call(
        body, name=name, grid=(T // tm,),
        out_shape=out_shape + x_out, in_specs=in_specs + x_in, out_specs=out_specs + x_in,
        scratch_shapes=x_scratch,
        compiler_params=_params("arbitrary"),
    )(*args, *x_args)


def _mm_wgrad(x, y, name, *, xw=None, yw=None, split=1, tt=2048):
    T, P = x.shape
    Q = y.shape[1]
    xw, yw = xw or P, yw or Q
    jx, jy = P // xw, Q // yw
    assert jx == 1 or jy == 1
    tt = min(tt, T)
    nt = T // tt
    part = yw // split

    def body(x_ref, y_ref, o_ref, acc_ref):
        t = pl.program_id(1)

        @pl.when(t == 0)
        def _():
            acc_ref[...] = jnp.zeros_like(acc_ref)

        acc_ref[...] += _dot(x_ref[...], y_ref[...], TN)

        @pl.when(t == nt - 1)
        def _():
            if jy > 1:
                for k in range(split):
                    o_ref[k] = acc_ref[:, part * k:part * (k + 1)].astype(BF16)
            else:
                o_ref[...] = acc_ref[...].astype(BF16)

    if jy > 1:
        out_shape = jax.ShapeDtypeStruct((jy * split, P, part), BF16)
        out_spec = pl.BlockSpec((split, P, part), lambda j, t: (j, 0, 0))
    else:
        out_shape = jax.ShapeDtypeStruct((P, Q), BF16)
        out_spec = pl.BlockSpec((xw, Q), lambda j, t: (j, 0))
    return pl.pallas_call(
        body, name=name, grid=(max(jx, jy), nt),
        out_shape=out_shape,
        in_specs=[pl.BlockSpec((tt, xw), (lambda j, t: (t, j)) if jx > 1 else (lambda j, t: (t, 0))),
                  pl.BlockSpec((tt, yw), (lambda j, t: (t, j)) if jy > 1 else (lambda j, t: (t, 0)))],
        out_specs=out_spec,
        scratch_shapes=[pltpu.VMEM((xw, yw), F32)],
        compiler_params=_params("parallel", "arbitrary"),
    )(x, y)


def _ffn_in(h, wt, name, *, tm=512, opening=None, exchange=None):
    T, K = h.shape
    F = wt.shape[0] // 2
    tm = min(tm, T)
    n_vec = 3 if opening else 0

    def body(*refs):
        h_ref, w_ref = refs[0], refs[1 + n_vec]
        pq_ref, g_ref = refs[2 + n_vec], refs[3 + n_vec]
        hh = h_ref[...]
        if opening:
            g_vec, sc_vec, sh_vec = refs[1:4]
            hh = (hh * _rms(hh) * (g_vec[...] * (1.0 + sc_vec[...])) + sh_vec[...]).astype(BF16)
            refs[4 + n_vec][...] = hh
        for cols in _col_chunks(F):
            hi = slice(F + cols.start, F + cols.stop)
            a = _dot(hh, w_ref[cols], NT)
            b = _dot(hh, w_ref[hi], NT)
            s = _sigmoid_t(a)
            silu = a * s
            pq_ref[:, cols] = (b * (s * (1.0 + a * (1.0 - s)))).astype(BF16)
            pq_ref[:, hi] = silu.astype(BF16)
            g_ref[:, cols] = (silu * b).astype(BF16)

    outs = [jax.ShapeDtypeStruct((T, 2 * F), BF16), jax.ShapeDtypeStruct((T, F), BF16)]
    out_specs = [_row_spec(tm, 2 * F), _row_spec(tm, F)]
    if opening:
        outs.append(jax.ShapeDtypeStruct((T, K), BF16))
        out_specs.append(_row_spec(tm, K))
    vecs = list(opening) if opening else []
    body, x_args, x_in, x_out, x_scratch = _host(exchange, 2 + n_vec, len(outs), body, _first_last(T // tm))
    res = pl.pallas_call(
        body, name=name, grid=(T // tm,),
        out_shape=outs + x_out,
        in_specs=[_row_spec(tm, K)] + [_vec_spec(K)] * n_vec + [_whole_spec(wt)] + x_in,
        out_specs=out_specs + x_in,
        scratch_shapes=x_scratch,
        compiler_params=_params("arbitrary" if exchange else "parallel"),
    )(h, *vecs, wt, *x_args)
    return res[:len(outs)], res[len(outs):]


def _ffn_dgate(dy, w_out, pq, name, *, tm=512):
    T, N = dy.shape
    F = w_out.shape[0]
    tm = min(tm, T)

    def body(dy_ref, w_ref, pq_ref, dz_ref):
        dyv = dy_ref[...]
        for cols in _col_chunks(F):
            hi = slice(F + cols.start, F + cols.stop)
            dg = _dot(dyv, w_ref[cols], NT)
            dz_ref[:, cols] = (dg * pq_ref[:, cols].astype(F32)).astype(BF16)
            dz_ref[:, hi] = (dg * pq_ref[:, hi].astype(F32)).astype(BF16)

    return pl.pallas_call(
        body, name=name, grid=(T // tm,),
        out_shape=jax.ShapeDtypeStruct((T, 2 * F), BF16),
        in_specs=[_row_spec(tm, N), _whole_spec(w_out), _row_spec(tm, 2 * F)],
        out_specs=_row_spec(tm, 2 * F),
        compiler_params=_params("parallel"),
    )(dy, w_out, pq)


def _ada_fwd(c_all, w, b, name):
    L, K, n = w.shape

    def body(c_ref, w_ref, b_ref, o_ref):
        cv = c_ref[...]
        cond = cv * _sigmoid(cv)
        for l in range(L):
            o_ref[l] = _dot(cond, w_ref[l], precision=lax.Precision.HIGHEST) + b_ref[l]

    return pl.pallas_call(
        body, name=name,
        out_shape=jax.ShapeDtypeStruct((L, NDEV, n), F32),
        compiler_params=pltpu.CompilerParams(vmem_limit_bytes=VMEM_LIMIT),
    )(c_all, w, b)


def _ada_bwd(c_all_t, gmod, name):
    L, _, n = gmod.shape
    K = c_all_t.shape[0]

    def body(c_ref, g_ref, o_ref):
        cv = c_ref[...]
        cond = cv * _sigmoid(cv)
        for l in range(L):
            o_ref[l] = _dot(cond, g_ref[l], precision=lax.Precision.HIGHEST)

    return pl.pallas_call(
        body, name=name,
        out_shape=jax.ShapeDtypeStruct((L, K, n), F32),
        compiler_params=pltpu.CompilerParams(vmem_limit_bytes=VMEM_LIMIT),
    )(c_all_t, gmod)


def _tri(n, upper=False, block=None):
    r = lax.broadcasted_iota(jnp.int32, (n, n), 0)
    c = lax.broadcasted_iota(jnp.int32, (n, n), 1)
    m = (c >= r) if upper else (c <= r)
    if block is not None:
        m = m & ((r // block) == (c // block))
    return m.astype(BF16)


TRI_ROWS = 128


def _tri_dot(tri, x):
    hi = x.astype(BF16)
    lo = (x - hi.astype(F32)).astype(BF16)
    rows = x.shape[0]
    step = min(TRI_ROWS, rows)
    parts = [_dot(tri, hi[r:r + step]) + _dot(tri, lo[r:r + step]) for r in range(0, rows, step)]
    return parts[0] if len(parts) == 1 else jnp.concatenate(parts, axis=0)


def _hgrn_gates(proj_ref, lb_ref, jh):
    proj_ref = _ColBlocks(proj_ref, 512)
    lb = lb_ref[:, 512 * jh:512 * (jh + 1)]
    qp = proj_ref[jh]
    fx = proj_ref[2 + jh]
    sq = _sigmoid_t(qp)
    sig = _sigmoid_t(fx)
    f = lb + (1.0 - lb) * sig
    k = (1.0 - lb) * (1.0 - sig)
    return lb, qp, sq, sig, f, k


def _hgrn_fwd(proj, lb, out_norm, name, *, tb=512, exchange=None):
    T = proj.shape[0]
    tb = min(tb, T)
    nc = tb // HG_CHUNK
    lmat = _tri(min(TRI_ROWS, tb), block=HG_CHUNK)

    def body(proj_ref, lb_ref, on_ref, l_ref, o_ref, og_ref, st_ref, s_scr, b_scr):
        @pl.when(pl.program_id(0) == 0)
        def _():
            s_scr[...] = jnp.zeros_like(s_scr)

        r_i = lax.broadcasted_iota(jnp.int32, (HG_CHUNK, HG_CHUNK), 0)
        c_i = lax.broadcasted_iota(jnp.int32, (HG_CHUNK, HG_CHUNK), 1)
        causal = c_i <= r_i
        onv = on_ref[...]
        blocks = _ColBlocks(proj_ref, 512)
        for jh in range(2):
            lbv, qp, sq, sig, f, k = _hgrn_gates(proj_ref, lb_ref, jh)
            q = qp * sq
            b_half = b_scr.at[jh]
            b_half[...] = _tri_dot(l_ref[...], jnp.log(f))
            v = blocks[4 + jh]
            gp = blocks[6 + jh]
            gs = gp * _sigmoid_t(gp)
            for hh in range(4):
                hd = 4 * jh + hh
                cs = slice(HG_HEAD * hh, HG_HEAD * (hh + 1))
                for ci in range(nc):
                    r0 = HG_CHUNK * ci
                    rs = slice(r0, r0 + HG_CHUNK)
                    bc = b_half[rs, cs]
                    bm = b_half[r0 + HG_CHUNK // 2 - 1:r0 + HG_CHUNK // 2, cs]
                    bl = b_half[r0 + HG_CHUNK - 1:r0 + HG_CHUNK, cs]
                    qc, kc, vc = q[rs, cs], k[rs, cs], v[rs, cs].astype(BF16)
                    e_q, e_k = jnp.exp(bc - bm), jnp.exp(bm - bc)
                    qe = (qc * (e_q * jnp.exp(bm))).astype(BF16)
                    qt = (qc * e_q).astype(BF16)
                    kt = (kc * e_k).astype(BF16)
                    kd = (kc * (e_k * jnp.exp(bl - bm))).astype(BF16)
                    st = s_scr[hd]
                    stb = st.astype(BF16)
                    st_ref[ci, hd] = stb
                    a = jnp.where(causal, _dot(qt, kt, NT), 0.0).astype(BF16)
                    o = _dot(qe, stb, NT) + _dot(a, vc)
                    s_scr[hd] = st * jnp.exp(bl) + _dot(vc, kd, TN)
                    o_ref[rs, HG_HEAD * hd:HG_HEAD * (hd + 1)] = o
                    r = lax.rsqrt(_rowmean(o * o) + EPS)
                    og_ref[rs, HG_HEAD * hd:HG_HEAD * (hd + 1)] = (o * r * onv * gs[rs, cs]).astype(BF16)

    body, x_args, x_in, x_out, x_scratch = _host(exchange, 4, 3, body, _first_last(T // tb))
    res = pl.pallas_call(
        body, name=name, grid=(T // tb,),
        out_shape=[jax.ShapeDtypeStruct((T, D_MODEL), F32), jax.ShapeDtypeStruct((T, D_MODEL), BF16),
                   jax.ShapeDtypeStruct((T // HG_CHUNK, HG_HEADS, HG_HEAD, HG_HEAD), BF16)] + x_out,
        in_specs=[pl.BlockSpec((tb, 4 * D_MODEL), lambda i: (i, 0)),
                  pl.BlockSpec((1, D_MODEL), lambda i: (0, 0)),
                  pl.BlockSpec((1, HG_HEAD), lambda i: (0, 0)),
                  pl.BlockSpec(lmat.shape, lambda i: (0, 0))] + x_in,
        out_specs=[pl.BlockSpec((tb, D_MODEL), lambda i: (i, 0)),
                   pl.BlockSpec((tb, D_MODEL), lambda i: (i, 0)),
                   pl.BlockSpec((nc, HG_HEADS, HG_HEAD, HG_HEAD), lambda i: (i, 0, 0, 0))] + x_in,
        scratch_shapes=[pltpu.VMEM((HG_HEADS, HG_HEAD, HG_HEAD), F32), pltpu.VMEM((2, tb, 512), F32)] + x_scratch,
        compiler_params=_params("arbitrary"),
    )(proj, lb, out_norm, lmat, *x_args)
    return res[0], res[1], res[2], res[3:]


def _hgrn_bwd(proj, o, dog, states, lb, out_norm, name, *, tb=512):
    T = proj.shape[0]
    tb = min(tb, T)
    nc = tb // HG_CHUNK
    nb = T // tb
    lmat = _tri(min(TRI_ROWS, tb), block=HG_CHUNK)
    umat = _tri(min(TRI_ROWS, tb), upper=True, block=HG_CHUNK)

    def body(proj_ref, o_ref, dog_ref, st_ref, lb_ref, on_ref, l_ref, u_ref,
             dproj_ref, dlb_ref, don_ref, ds_scr, *half_scr):
        @pl.when(pl.program_id(0) == 0)
        def _():
            ds_scr[...] = jnp.zeros_like(ds_scr)
            dlb_ref[...] = jnp.zeros_like(dlb_ref)
            don_ref[...] = jnp.zeros_like(don_ref)

        r_i = lax.broadcasted_iota(jnp.int32, (HG_CHUNK, HG_CHUNK), 0)
        c_i = lax.broadcasted_iota(jnp.int32, (HG_CHUNK, HG_CHUNK), 1)
        causal = c_i <= r_i
        causal_t = r_i <= c_i
        last_row = lax.broadcasted_iota(jnp.int32, (HG_CHUNK, HG_HEAD), 0) == HG_CHUNK - 1
        onv = on_ref[...]
        don_acc = jnp.zeros((1, HG_HEAD), F32)
        blocks = _ColBlocks(proj_ref, 512)
        dproj_ref = _ColBlocks(dproj_ref, 512)
        for jh in range(2):
            b_scr, dq_scr, dk_scr, dv_scr, dg_scr, db_scr = [s.at[jh] for s in half_scr]
            lbv, qp, sq, sig, f, k = _hgrn_gates(proj_ref, lb_ref, jh)
            q = qp * sq
            b_scr[...] = _tri_dot(l_ref[...], jnp.log(f))
            v = blocks[4 + jh]
            gp = blocks[6 + jh]
            sg = _sigmoid_t(gp)
            for ci in reversed(range(nc)):
                r0 = HG_CHUNK * ci
                rs = slice(r0, r0 + HG_CHUNK)
                for hh in range(4):
                    hd = 4 * jh + hh
                    cs = slice(HG_HEAD * hh, HG_HEAD * (hh + 1))
                    hs = slice(HG_HEAD * hd, HG_HEAD * (hd + 1))
                    oc = o_ref[rs, hs]
                    r = lax.rsqrt(_rowmean(oc * oc) + EPS)
                    oh = oc * r
                    gc, sgc = gp[rs, cs], sg[rs, cs]
                    dogc = dog_ref[rs, hs].astype(F32)
                    don = dogc * (gc * sgc)
                    dg_scr[rs, cs] = dogc * (oh * onv) * (sgc * (1.0 + gc * (1.0 - sgc)))
                    don_acc += _colsum(don * oh)
                    donh = don * onv
                    do = (r * (donh - oh * _rowmean(donh * oh))).astype(BF16)
                    bc = b_scr[rs, cs]
                    bm = b_scr[r0 + HG_CHUNK // 2 - 1:r0 + HG_CHUNK // 2, cs]
                    bl = b_scr[r0 + HG_CHUNK - 1:r0 + HG_CHUNK, cs]
                    qc, kc, vc = q[rs, cs], k[rs, cs], v[rs, cs].astype(BF16)
                    e_q, e_k = jnp.exp(bc - bm), jnp.exp(bm - bc)
                    e_b, e_d = e_q * jnp.exp(bm), e_k * jnp.exp(bl - bm)
                    qe = (qc * e_b).astype(BF16)
                    qt = (qc * e_q).astype(BF16)
                    kt = (kc * e_k).astype(BF16)
                    kd = (kc * e_d).astype(BF16)
                    stb = st_ref[ci, hd]
                    dst = ds_scr[hd]
                    dstb = dst.astype(BF16)
                    a_t = jnp.where(causal_t, _dot(kt, qt, NT), 0.0).astype(BF16)
                    da = jnp.where(causal, _dot(do, vc, NT), 0.0).astype(BF16)
                    da_t = jnp.where(causal_t, _dot(vc, do, NT), 0.0).astype(BF16)
                    dv_scr[rs, cs] = _dot(a_t, do) + _dot(kd, dstb, NT)
                    dqe, dqt = _dot(do, stb), _dot(da, kt)
                    dkt, dkd = _dot(da_t, qt), _dot(vc, dstb)
                    dq_scr[rs, cs] = dqe * e_b + dqt * e_q
                    dk_scr[rs, cs] = dkt * e_k + dkd * e_d
                    e_l = jnp.exp(bl)
                    kd_dkd = kd.astype(F32) * dkd
                    dbc = qe.astype(F32) * dqe + qt.astype(F32) * dqt - kt.astype(F32) * dkt - kd_dkd
                    handed = e_l * _colsum(dstb.astype(F32) * stb.astype(F32)) + _colsum(kd_dkd)
                    db_scr[rs, cs] = dbc + jnp.where(last_row, handed, 0.0)
                    ds_scr[hd] = dst * e_l + _dot(do, qe, TN)
            dq = dq_scr[...]
            dk = dk_scr[...]
            cols = slice(512 * jh, 512 * (jh + 1))
            dlogf = _tri_dot(u_ref[...], db_scr[...])
            one_m_sig = 1.0 - sig
            dsig = (1.0 - lbv) * sig * one_m_sig
            dboth = dlogf / f - dk
            dproj_ref[jh] = (dq * (sq * (1.0 + qp * (1.0 - sq)))).astype(BF16)
            dproj_ref[2 + jh] = (dboth * dsig).astype(BF16)
            dproj_ref[4 + jh] = dv_scr[...].astype(BF16)
            dproj_ref[6 + jh] = dg_scr[...].astype(BF16)
            dlb_ref[:, cols] += _colsum(dboth * one_m_sig)
        don_ref[...] += don_acc

    rev = lambda i: nb - 1 - i
    return pl.pallas_call(
        body, name=name, grid=(nb,),
        out_shape=(jax.ShapeDtypeStruct((T, 4 * D_MODEL), BF16), jax.ShapeDtypeStruct((1, D_MODEL), F32),
                   jax.ShapeDtypeStruct((1, HG_HEAD), F32)),
        in_specs=[pl.BlockSpec((tb, 4 * D_MODEL), lambda i: (rev(i), 0)),
                  pl.BlockSpec((tb, D_MODEL), lambda i: (rev(i), 0)),
                  pl.BlockSpec((tb, D_MODEL), lambda i: (rev(i), 0)),
                  pl.BlockSpec((nc, HG_HEADS, HG_HEAD, HG_HEAD), lambda i: (rev(i), 0, 0, 0)),
                  pl.BlockSpec((1, D_MODEL), lambda i: (0, 0)),
                  pl.BlockSpec((1, HG_HEAD), lambda i: (0, 0)),
                  pl.BlockSpec(lmat.shape, lambda i: (0, 0)),
                  pl.BlockSpec(lmat.shape, lambda i: (0, 0))],
        out_specs=(pl.BlockSpec((tb, 4 * D_MODEL), lambda i: (rev(i), 0)),
                   pl.BlockSpec((1, D_MODEL), lambda i: (0, 0)),
                   pl.BlockSpec((1, HG_HEAD), lambda i: (0, 0))),
        scratch_shapes=[pltpu.VMEM((HG_HEADS, HG_HEAD, HG_HEAD), F32)] + [pltpu.VMEM((2, tb, 512), F32)] * 6,
        compiler_params=_params("arbitrary"),
    )(proj, o, dog, states, lb, out_norm, lmat, umat)


def _gm_norm(pre_ref, lg_ref, lbias_ref):
    pre_ref = _ColBlocks(pre_ref, 768)
    vs = [pre_ref[4 + j].astype(F32) for j in range(4)]
    width = 4 * vs[0].shape[1]
    mu = sum(jnp.sum(v, axis=1, keepdims=True) for v in vs) / width
    ds = [v - mu for v in vs]
    var = sum(jnp.sum(d * d, axis=1, keepdims=True) for d in ds) / width
    rstd = lax.rsqrt(var + EPS)
    vhat = [d * rstd for d in ds]
    vn = [vhat[j] * lg_ref[j:j + 1, :] + lbias_ref[j:j + 1, :] for j in range(4)]
    return vhat, vn, rstd


def _gm_spatial_fwd(pre, ln_g, ln_b, ws, bsb, name, *, tb=256):
    T = pre.shape[0]
    tb = min(tb, T)
    nc = tb // GM_CHUNK

    def body(pre_ref, lg_ref, lbias_ref, ws_ref, bs_ref, o_ref):
        _, vn, _ = _gm_norm(pre_ref, lg_ref, lbias_ref)
        pre_ref, o_ref = _ColBlocks(pre_ref, 768), _ColBlocks(o_ref, 768)
        for j in range(4):
            u = pre_ref[j].astype(F32)
            for e in range(2):
                g = 2 * j + e
                cs = slice(GM_GDIM * e, GM_GDIM * (e + 1))
                wg = ws_ref[g].astype(BF16)
                for ci in range(nc):
                    rs = slice(GM_CHUNK * ci, GM_CHUNK * (ci + 1))
                    vm = _dot(wg, vn[j][rs, cs].astype(BF16)) + bs_ref[g]
                    o_ref[j, rs, cs] = (u[rs, cs] * vm).astype(BF16)

    return pl.pallas_call(
        body, name=name, grid=(T // tb,),
        out_shape=jax.ShapeDtypeStruct((T, 4 * 768), BF16),
        in_specs=[pl.BlockSpec((tb, 8 * 768), lambda i: (i, 0)),
                  pl.BlockSpec((4, 768), lambda i: (0, 0)),
                  pl.BlockSpec((4, 768), lambda i: (0, 0)),
                  pl.BlockSpec((GM_GROUPS, GM_CHUNK, GM_CHUNK), lambda i: (0, 0, 0)),
                  pl.BlockSpec((GM_GROUPS, GM_CHUNK, GM_GDIM), lambda i: (0, 0, 0))],
        out_specs=pl.BlockSpec((tb, 4 * 768), lambda i: (i, 0)),
        compiler_params=_params("parallel"),
    )(pre, ln_g, ln_b, ws, bsb)


def _gm_spatial_bwd(pre, gp, dm, ln_g, ln_b, ws, ws_t, bsb, name, *, tb=256):
    T = pre.shape[0]
    tb = min(tb, T)
    nc = tb // GM_CHUNK
    nb = T // tb

    def body(pre_ref, gp_ref, dm_ref, lg_ref, lbias_ref, ws_ref, wst_ref, bs_ref,
             dpre_ref, dws_ref, dbs_ref, dlg_ref, dlb_ref, dbin_ref, dbs_scr, dvn_scr, du_scr):
        i = pl.program_id(0)

        @pl.when(i == 0)
        def _():
            dws_ref[...] = jnp.zeros_like(dws_ref)
            dbs_scr[...] = jnp.zeros_like(dbs_scr)
            dlg_ref[...] = jnp.zeros_like(dlg_ref)
            dlb_ref[...] = jnp.zeros_like(dlb_ref)
            dbin_ref[...] = jnp.zeros_like(dbin_ref)

        vhat, vn, rstd = _gm_norm(pre_ref, lg_ref, lbias_ref)
        pre_ref, gp_ref, dm_ref = _ColBlocks(pre_ref, 768), _ColBlocks(gp_ref, 768), _ColBlocks(dm_ref, 768)
        dpre_ref = _ColBlocks(dpre_ref, 768)
        for j in range(4):
            u = pre_ref[j].astype(F32)
            for e in range(2):
                g = 2 * j + e
                cs = slice(GM_GDIM * e, GM_GDIM * (e + 1))
                wg = ws_ref[g].astype(BF16)
                wgt = wst_ref[g].astype(BF16)
                for ci in range(nc):
                    rs = slice(GM_CHUNK * ci, GM_CHUNK * (ci + 1))
                    vnb = vn[j][rs, cs].astype(BF16)
                    vm = _dot(wg, vnb) + bs_ref[g]
                    dmg = dm_ref[j, rs, cs].astype(F32)
                    du_scr[j, rs, cs] = dmg * vm
                    dvm = dmg * u[rs, cs]
                    dvmb = dvm.astype(BF16)
                    dws_ref[g] += _dot(dvmb, vnb, NT)
                    dbs_scr[g] += dvm
                    dvn_scr[j, rs, cs] = _dot(wgt, dvmb)
        width = 4 * 768
        dvh = []
        for j in range(4):
            dvn = dvn_scr[j]
            dlg_ref[j:j + 1, :] += _colsum(dvn * vhat[j])
            dlb_ref[j:j + 1, :] += _colsum(dvn)
            dvh.append(dvn * lg_ref[j:j + 1, :])
        m1 = sum(jnp.sum(d, axis=1, keepdims=True) for d in dvh) / width
        m2 = sum(jnp.sum(dvh[j] * vhat[j], axis=1, keepdims=True) for j in range(4)) / width
        for j in range(4):
            dv = rstd * (dvh[j] - m1 - vhat[j] * m2)
            dpv = dv * gp_ref[4 + j].astype(F32)
            dpu = du_scr[j] * gp_ref[j].astype(F32)
            dpre_ref[4 + j] = dpv.astype(BF16)
            dpre_ref[j] = dpu.astype(BF16)
            dbin_ref[4 + j:5 + j, :] += _colsum(dpv)
            dbin_ref[j:j + 1, :] += _colsum(dpu)

        @pl.when(i == nb - 1)
        def _():
            r_i = lax.broadcasted_iota(jnp.int32, (GM_CHUNK, GM_CHUNK), 0)
            c_i = lax.broadcasted_iota(jnp.int32, (GM_CHUNK, GM_CHUNK), 1)
            for g in range(GM_GROUPS):
                dws_ref[g] = jnp.where(c_i <= r_i, dws_ref[g], 0.0)
                dbs_ref[g] = jnp.broadcast_to(jnp.sum(dbs_scr[g], axis=1, keepdims=True), (GM_CHUNK, GM_CHUNK))

    sq = pl.BlockSpec((GM_GROUPS, GM_CHUNK, GM_CHUNK), lambda i: (0, 0, 0))
    v4 = pl.BlockSpec((4, 768), lambda i: (0, 0))
    return pl.pallas_call(
        body, name=name, grid=(nb,),
        out_shape=(jax.ShapeDtypeStruct((T, 8 * 768), BF16),
                   jax.ShapeDtypeStruct((GM_GROUPS, GM_CHUNK, GM_CHUNK), F32),
                   jax.ShapeDtypeStruct((GM_GROUPS, GM_CHUNK, GM_CHUNK), F32),
                   jax.ShapeDtypeStruct((4, 768), F32), jax.ShapeDtypeStruct((4, 768), F32),
                   jax.ShapeDtypeStruct((8, 768), F32)),
        in_specs=[pl.BlockSpec((tb, 8 * 768), lambda i: (i, 0)),
                  pl.BlockSpec((tb, 8 * 768), lambda i: (i, 0)),
                  pl.BlockSpec((tb, 4 * 768), lambda i: (i, 0)),
                  v4, v4, sq, sq,
                  pl.BlockSpec((GM_GROUPS, GM_CHUNK, GM_GDIM), lambda i: (0, 0, 0))],
        out_specs=(pl.BlockSpec((tb, 8 * 768), lambda i: (i, 0)), sq, sq, v4, v4,
                   pl.BlockSpec((8, 768), lambda i: (0, 0))),
        scratch_shapes=[pltpu.VMEM((GM_GROUPS, GM_CHUNK, GM_GDIM), F32),
                        pltpu.VMEM((4, tb, 768), F32), pltpu.VMEM((4, tb, 768), F32)],
        compiler_params=_params("arbitrary"),
    )(pre, gp, dm, ln_g, ln_b, ws, ws_t, bsb)


def _adamw(slots, w, m, v, name, *, tr=256):
    S, R, C = slots.shape
    tr = next((t for t in (tr, tr // 2, tr // 4, tr // 8, tr // 16) if R % t == 0), R) if R > tr else R
    bc1 = 1.0 - ADAM_B1 ** ADAM_STEP
    bc2 = 1.0 - ADAM_B2 ** ADAM_STEP

    def body(s_ref, w_ref, m_ref, v_ref, g_ref, d_ref, nm_ref, nv_ref):
        g = s_ref[0].astype(F32)
        for s in range(1, S):
            g = g + s_ref[s].astype(F32)
        mn = ADAM_B1 * m_ref[...] + (1.0 - ADAM_B1) * g
        vn = ADAM_B2 * v_ref[...] + (1.0 - ADAM_B2) * (g * g)
        g_ref[...] = g
        nm_ref[...] = mn
        nv_ref[...] = vn
        d_ref[...] = -ADAM_LR * ((mn / bc1) / (jnp.sqrt(vn / bc2) + ADAM_EPS) + ADAM_WD * w_ref[...])

    spec = pl.BlockSpec((tr, C), lambda i: (i, 0))
    return pl.pallas_call(
        body, name=name, grid=(R // tr,),
        out_shape=(jax.ShapeDtypeStruct((R, C), F32),) * 4,
        in_specs=[pl.BlockSpec((S, tr, C), lambda i: (0, i, 0)), spec, spec, spec],
        out_specs=(spec,) * 4,
        compiler_params=_params("parallel"),
    )(slots, w, m, v)


def _update(slots, w, m, v, name):
    shp = w.shape
    C = shp[-1]
    R = math.prod(shp[:-1])
    outs = _adamw(slots.reshape(slots.shape[0], R, C), w.reshape(R, C), m.reshape(R, C), v.reshape(R, C), name)
    return tuple(o.reshape(shp) for o in outs)


def kernel(x, c, ada_w, ada_b, norm_pre, norm_post, ffn_w_in, ffn_w_out, hg_w_in, hg_w_out, hg_out_norm, hg_lb, gm_w_in, gm_b_in, gm_ln_g, gm_ln_b, gm_w_s, gm_b_s, gm_w_out, loss_target, m_ada_w, m_ada_b, m_norm_pre, m_norm_post, m_ffn_w_in, m_ffn_w_out, m_hg_w_in, m_hg_w_out, m_hg_out_norm, m_hg_lb, m_gm_w_in, m_gm_b_in, m_gm_ln_g, m_gm_ln_b, m_gm_w_s, m_gm_b_s, m_gm_w_out, v_ada_w, v_ada_b, v_norm_pre, v_norm_post, v_ffn_w_in, v_ffn_w_out, v_hg_w_in, v_hg_w_out, v_hg_out_norm, v_hg_lb, v_gm_w_in, v_gm_b_in, v_gm_ln_g, v_gm_ln_b, v_gm_w_s, v_gm_b_s, v_gm_w_out):
    me = 4 * lax.axis_index("x") + 2 * lax.axis_index("y") + lax.axis_index("c")
    T = x.shape[1]
    x0 = x.reshape(T, D_MODEL)
    target = loss_target.reshape(T, D_MODEL)
    n_ada = ada_w.shape[-1]

    pack = jnp.concatenate([
        c.reshape(8, 128), norm_pre.reshape(6, 128), norm_post.reshape(6, 128),
        gm_b_in.reshape(6, 128), gm_ln_g.reshape(3, 128), gm_ln_b.reshape(3, 128)], axis=0)
    packs = _all_gather(pack, "gather_small")
    c_all = packs[:, 0:8].reshape(NDEV, D_MODEL)
    npre = packs[:, 8:14].reshape(NDEV, 2, 3, 128).transpose(1, 2, 0, 3).reshape(2, 3, D_MODEL)
    npost = packs[:, 14:20].reshape(NDEV, 2, 3, 128).transpose(1, 2, 0, 3).reshape(2, 3, D_MODEL)
    b_in = packs[:, 20:26].reshape(1, NDEV * 768)
    ln_g = packs[:, 26:29].reshape(4, 768)
    ln_b = packs[:, 29:32].reshape(4, 768)

    ada_b_mine = lax.dynamic_slice_in_dim(ada_b, me * n_ada, n_ada, axis=1).reshape(2, 1, n_ada)
    mod_cols = _ada_fwd(c_all, ada_w, ada_b_mine, "ada_fwd")
    mod_all = _all_gather(mod_cols, "gather_mod")
    mod = lax.dynamic_index_in_dim(mod_all, me, axis=2, keepdims=False)
    mod = mod.transpose(1, 0, 2).reshape(2, 9, 1, D_MODEL)

    sh_fi, sh_fo = ffn_w_in.astype(BF16).swapaxes(-1, -2), ffn_w_out.astype(BF16)
    sh_hi, sh_ho = hg_w_in[0].astype(BF16).T, hg_w_out[0].astype(BF16)
    sh_mi, sh_mo = gm_w_in[0].astype(BF16).T, gm_w_out[0].astype(BF16)

    def whole(gathered):
        return gathered.reshape(-1, D_MODEL)

    w_fi = {(0, 0): whole(_all_gather(sh_fi[0, 0], "gather_ffn_in_first"))}
    w_fo = {}
    riders = {"l0s0": [sh_fo[0, 0], sh_hi, sh_ho], "l0s1": [sh_fi[0, 1], sh_fo[0, 1]], "hg_mix": [sh_mi, sh_mo],
              "l0s2": [sh_fi[1, 0], sh_fo[1, 0]], "l1s0": [sh_fi[1, 1], sh_fo[1, 1]]}

    sm = jax.nn.softmax(hg_lb, axis=0)
    lb0 = sm[0:1]
    on = hg_out_norm.reshape(1, HG_HEAD)
    tril = jnp.tril(jnp.ones((GM_CHUNK, GM_CHUNK), F32))
    ws = gm_w_s[0] * tril[None]
    ws_t = ws.transpose(0, 2, 1)
    bsb = jnp.broadcast_to(gm_b_s[0][:, :, None], (GM_GROUPS, GM_CHUNK, GM_GDIM))

    res_ws = (0.5, 1.0, 0.5)

    def vecs(i, s):
        return (npre[i, s].reshape(1, D_MODEL), npost[i, s].reshape(1, D_MODEL),
                mod[i, 3 * s], mod[i, 3 * s + 1], mod[i, 3 * s + 2])

    order = [(i, s) for i in range(2) for s in range(3)]
    saved = {}
    xs = x0
    for pos, (i, s) in enumerate(order):
        tag = f"l{i}s{s}"
        pre_g, post_g, shift, scale, gate = vecs(i, s)
        rider = _Exchange("gather", riders[tag]) if tag in riders else None
        if s != 1:
            if pos == 0:
                (pq, a, h), got = _ffn_in(xs, w_fi[0, 0], "ffn_in_" + tag, opening=(pre_g, scale, shift), exchange=rider)
            else:
                (pq, a), got = _ffn_in(h, w_fi[i, s // 2], "ffn_in_" + tag, exchange=rider)
            extra = (pq, a)
            if tag == "l0s0":
                w_fo[0, 0], w_hi, w_ho = map(whole, got)
            elif tag == "l0s2":
                w_fi[1, 0], w_fo[1, 0] = map(whole, got)
            elif tag == "l1s0":
                w_fi[1, 1], w_fo[1, 1] = map(whole, got)
            wo = w_fo[i, s // 2]
        elif i == 0:
            proj, *got = _mm_blocks(h, w_hi, "hg_in", out_dtype=F32, exchange=rider)
            w_fi[0, 1], w_fo[0, 1] = map(whole, got)
            o, og, states, got = _hgrn_fwd(proj, lb0, on, "hg_mix", exchange=_Exchange("gather", riders["hg_mix"]))
            w_mi, w_mo = map(whole, got)
            a, wo = og, w_ho
            extra = (proj, o, og, states)
        else:
            pre, gp = _mm_blocks(h, w_mi, "gm_in", bias=b_in, gelu=True)
            a = _gm_spatial_fwd(pre, ln_g, ln_b, ws, bsb, "gm_mix")
            wo = w_mo
            extra = (pre, gp, a)
        if pos + 1 < len(order):
            npre_g, _, nshift, nscale, _ = vecs(*order[pos + 1])
            y, x_next, h_next = _out_proj(a, wo, xs, post_g, gate, res_ws[s], (npre_g, nscale, nshift), "out_" + tag)
            saved[tag] = (xs, h, y) + extra
            xs, h = x_next, h_next
        else:
            dx, dy, dgate, dpost, loss_part = _out_proj_last(a, wo, xs, post_g, gate, res_ws[s], target, "out_" + tag)
            saved[tag] = (xs, h, None) + extra
    loss = lax.psum(loss_part[0, 0], ("x", "y", "c"))

    slots = {}
    d_npre = [[None] * 3, [None] * 3]
    d_npost = [[None] * 3, [None] * 3]
    d_mod = [[None] * 9, [None] * 9]
    for pos in reversed(range(len(order))):
        i, s = order[pos]
        tag = f"l{i}s{s}"
        pre_g, _, _, scale, _ = vecs(i, s)
        xin, h = saved[tag][:2]
        if s != 1:
            w_in, wo = w_fi[i, s // 2], w_fo[i, s // 2]
            pq, g = saved[tag][3:]
            dz = _ffn_dgate(dy, wo, pq, "ffn_dgate_" + tag)
            g_out = _mm_wgrad(g, dy, "ffn_out_wgrad_" + tag, xw=1408)
            g_in = _mm_wgrad(dz, h, "ffn_in_wgrad_" + tag, xw=1408)
        elif i == 0:
            proj, o, og, states = saved[tag][3:]
            dog = _mm_blocks(dy, w_ho, "hg_out_dgrad")
            g_out = _mm_wgrad(og, dy, "hg_out_wgrad")
            dz, d_lb0, d_on = _hgrn_bwd(proj, o, dog, states, lb0, on, "hg_mix_bwd")
            w_in = w_hi
            g_in = _mm_wgrad(h, dz, "hg_in_wgrad", yw=1024, split=2)
        else:
            pre, gp, sp = saved[tag][3:]
            dm = _mm_blocks(dy, w_mo, "gm_out_dgrad")
            g_out = _mm_wgrad(sp, dy, "gm_out_wgrad", xw=768)
            dz, d_ws, d_bs, d_lg, d_lbias, d_bin = _gm_spatial_bwd(pre, gp, dm, ln_g, ln_b, ws, ws_t, bsb, "gm_mix_bwd")
            w_in = w_mi
            g_in = _mm_wgrad(h, dz, "gm_in_wgrad", yw=1536, split=2)
        g_out = g_out.reshape(NDEV, -1, D_MODEL)
        g_in = g_in.reshape(NDEV, -1, D_MODEL) if s != 1 else g_in
        d_npost[i][s] = dpost
        d_mod[i][3 * s + 2] = dgate
        rider = _Exchange("scatter", [g_in, g_out])
        if pos > 0:
            pi, ps = order[pos - 1]
            _, ppost_g, _, _, pgate = vecs(pi, ps)
            prev = (saved[f"l{pi}s{ps}"][2], ppost_g, pgate, res_ws[ps])
            dx, dshift, dscale, dpre_g, dy, dgate, dpost, r_in, r_out = _in_grad(
                dz, w_in, dx, xin, pre_g, scale, prev, "in_grad_" + tag, exchange=rider)
        else:
            dx, dshift, dscale, dpre_g, r_in, r_out = _in_grad(
                dz, w_in, dx, xin, pre_g, scale, None, "in_grad_" + tag, exchange=rider)
        slots[tag] = (r_in, r_out)
        d_npre[i][s] = dpre_g
        d_mod[i][3 * s], d_mod[i][3 * s + 1] = dshift, dscale
    grad_x = dx.reshape(x.shape)

    ffn_tags = ["l0s0", "l0s2", "l1s0", "l1s2"]
    s_fi = jnp.stack([slots[t][0] for t in ffn_tags], axis=1).swapaxes(-1, -2)
    s_fo = jnp.stack([slots[t][1] for t in ffn_tags], axis=1)
    (s_hi, s_ho), (s_mi, s_mo) = slots["l0s1"], slots["l1s1"]
    s_hi, s_ho, s_mi, s_mo = s_hi[:, None], s_ho[:, None], s_mi[:, None], s_mo[:, None]

    gmod = jnp.stack([jnp.concatenate(d_mod[i], axis=0) for i in range(2)])
    d_sm = lb0 * d_lb0
    d_hg_lb = jnp.concatenate([d_sm, jnp.zeros((2, D_MODEL), F32)], axis=0) - sm * d_sm
    small = [gmod, jnp.stack([jnp.concatenate(r, axis=0) for r in d_npre]),
             jnp.stack([jnp.concatenate(r, axis=0) for r in d_npost]),
             d_on, d_hg_lb, d_bin, d_lg, d_lbias, d_ws, d_bs[:, :, 0]]
    sizes = [a.size for a in small]
    flat = jnp.concatenate([a.reshape(-1) for a in small])
    rows = -(-flat.size // (8 * 128)) * 8
    flat = jnp.pad(flat, (0, rows * 128 - flat.size)).reshape(rows, 128)
    flats = _all_gather(flat, "gather_small_grads").reshape(NDEV, rows * 128)
    parts, off = [], 0
    for a, n in zip(small, sizes):
        parts.append(flats[:, off:off + n].reshape((NDEV,) + a.shape))
        off += n
    p_mod, p_npre, p_npost, p_on, p_lb, p_bin, p_lg, p_lbias, p_ws, p_bs = parts

    def mine(p, width):
        return lax.dynamic_slice_in_dim(p, me * width, width, axis=p.ndim - 1)

    gmod_cols = mine(p_mod.reshape(NDEV, 2, 9 * D_MODEL), n_ada).transpose(1, 0, 2)
    g_ada_w = _ada_bwd(jnp.pad(c_all.T, ((0, 0), (0, 120))), jnp.pad(gmod_cols, ((0, 0), (0, 120), (0, 0))), "ada_bwd")

    out = {}
    out["ada_w"] = _update(g_ada_w[None], ada_w, m_ada_w, v_ada_w, "adamw_ada_w")
    out["ada_b"] = _update(p_mod.reshape(NDEV, 2, 9 * D_MODEL), ada_b, m_ada_b, v_ada_b, "adamw_ada_b")
    out["norm_pre"] = _update(mine(p_npre, 128), norm_pre, m_norm_pre, v_norm_pre, "adamw_norm_pre")
    out["norm_post"] = _update(mine(p_npost, 128), norm_post, m_norm_post, v_norm_post, "adamw_norm_post")
    out["ffn_w_in"] = _update(s_fi.reshape((NDEV,) + ffn_w_in.shape), ffn_w_in, m_ffn_w_in, v_ffn_w_in, "adamw_ffn_in")
    out["ffn_w_out"] = _update(s_fo.reshape((NDEV,) + ffn_w_out.shape), ffn_w_out, m_ffn_w_out, v_ffn_w_out, "adamw_ffn_out")
    out["hg_w_in"] = _update(s_hi, hg_w_in, m_hg_w_in, v_hg_w_in, "adamw_hg_in")
    out["hg_w_out"] = _update(s_ho, hg_w_out, m_hg_w_out, v_hg_w_out, "adamw_hg_out")
    out["hg_out_norm"] = _update(p_on, hg_out_norm, m_hg_out_norm, v_hg_out_norm, "adamw_hg_norm")
    out["hg_lb"] = _update(p_lb, hg_lb, m_hg_lb, v_hg_lb, "adamw_hg_lb")
    out["gm_w_in"] = _update(s_mi, gm_w_in, m_gm_w_in, v_gm_w_in, "adamw_gm_in")
    out["gm_b_in"] = _update(mine(p_bin.reshape(NDEV, 1, 8 * 768), 768), gm_b_in, m_gm_b_in, v_gm_b_in, "adamw_gm_b_in")
    out["gm_ln_g"] = _update(mine(p_lg.reshape(NDEV, 1, 4 * 768), 384), gm_ln_g, m_gm_ln_g, v_gm_ln_g, "adamw_gm_ln_g")
    out["gm_ln_b"] = _update(mine(p_lbias.reshape(NDEV, 1, 4 * 768), 384), gm_ln_b, m_gm_ln_b, v_gm_ln_b, "adamw_gm_ln_b")
    out["gm_w_s"] = _update(p_ws[:, None], gm_w_s, m_gm_w_s, v_gm_w_s, "adamw_gm_w_s")
    out["gm_b_s"] = _update(p_bs[:, None], gm_b_s, m_gm_b_s, v_gm_b_s, "adamw_gm_b_s")
    out["gm_w_out"] = _update(s_mo, gm_w_out, m_gm_w_out, v_gm_w_out, "adamw_gm_out")

    names = ["ada_w", "ada_b", "norm_pre", "norm_post", "ffn_w_in", "ffn_w_out", "hg_w_in", "hg_w_out",
             "hg_out_norm", "hg_lb", "gm_w_in", "gm_b_in", "gm_ln_g", "gm_ln_b", "gm_w_s", "gm_b_s", "gm_w_out"]
    return (loss, grad_x, *[out[n][0] for n in names], *[out[n][1] for n in names],
            *[out[n][2] for n in names], *[out[n][3] for n in names])
```

```python
import math

import jax
import jax.numpy as jnp
from jax import lax
from jax.experimental import pallas as pl
from jax.experimental.pallas import tpu as pltpu

F32 = jnp.float32
BF16 = jnp.bfloat16
NDEV = 8
D_MODEL = 1024
EPS = 1e-6
HG_CHUNK = 64
HG_HEAD = 128
HG_HEADS = 8
GM_CHUNK = 128
GM_GDIM = 384
GM_GROUPS = 8
ADAM_LR = 0.001
ADAM_B1 = 0.9
ADAM_B2 = 0.999
ADAM_EPS = 1e-08
ADAM_WD = 0.01
ADAM_STEP = 10
VMEM_LIMIT = 56 * 2 ** 20

NN = (((1,), (0,)), ((), ()))
NT = (((1,), (1,)), ((), ()))
TN = (((0,), (0,)), ((), ()))
MESH = pl.DeviceIdType.MESH
ANY = pl.BlockSpec(memory_space=pl.ANY)


def _dot(a, b, dims=NN, precision=None):
    return lax.dot_general(a, b, dims, preferred_element_type=F32, precision=precision)


def _params(*sem):
    return pltpu.CompilerParams(dimension_semantics=sem, vmem_limit_bytes=VMEM_LIMIT)


def _sigmoid(x):
    return 1.0 / (1.0 + jnp.exp(-x))


def _sigmoid_t(x):
    return 0.5 * jnp.tanh(0.5 * x) + 0.5


def _gelu_and_grad(x):
    c = math.sqrt(2.0 / math.pi)
    m = (c * 0.044715) * (x * x)
    t = jnp.tanh(x * (c + m))
    hp = 0.5 + 0.5 * t
    return x * hp, hp * (1.0 + (x * (1.0 - t)) * (c + 3.0 * m))


def _colsum(x):
    return jnp.sum(x, axis=0, keepdims=True)


def _rowmean(x):
    return jnp.mean(x, axis=-1, keepdims=True)


def _all_gather(shard, name):
    def body(x_ref, out_ref, send_sems, recv_sems, local_sem):
        x, y, c = lax.axis_index("x"), lax.axis_index("y"), lax.axis_index("c")
        me, sibling = (x, y, c), (x, y, 1 - c)
        chips = [(1 - x, y), (x, 1 - y), (1 - x, 1 - y)]

        def slot(p):
            return out_ref.at[4 * p[0] + 2 * p[1] + p[2]]

        def copy(k, block, to, src=None):
            return pltpu.make_async_remote_copy(
                src_ref=slot(block) if src is None else src, dst_ref=slot(block),
                send_sem=send_sems.at[k], recv_sem=recv_sems.at[k],
                device_id=to, device_id_type=MESH)

        mine = pltpu.make_async_copy(x_ref, slot(me), local_sem)
        mine.start()
        first = [copy(0, me, sibling, src=x_ref)]
        first += [copy(1 + j, me, (*chip, c), src=x_ref) for j, chip in enumerate(chips)]
        for cp in first:
            cp.start()
        passed = [copy(4 + j, (*chip, c), sibling) for j, chip in enumerate(chips)]
        for j, chip in enumerate(chips):
            copy(1 + j, (*chip, c), me).wait_recv()
            passed[j].start()
        copy(0, sibling, me).wait_recv()
        for j, chip in enumerate(chips):
            copy(4 + j, (*chip, 1 - c), me).wait_recv()
        for cp in first + passed:
            cp.wait_send()
        mine.wait()

    return pl.pallas_call(
        body, name=name,
        out_shape=jax.ShapeDtypeStruct((NDEV,) + shard.shape, shard.dtype),
        in_specs=[ANY], out_specs=ANY,
        scratch_shapes=[pltpu.SemaphoreType.DMA((7,)), pltpu.SemaphoreType.DMA((7,)),
                        pltpu.SemaphoreType.DMA(())],
    )(shard)


class _Exchange:
    def __init__(self, kind, arrays):
        self.gather = kind == "gather"
        self.arrays = list(arrays)
        self.n = n = len(self.arrays)
        self.out_shape = [jax.ShapeDtypeStruct(((NDEV,) + a.shape) if self.gather else a.shape, a.dtype)
                          for a in self.arrays]
        self.scratch = [pltpu.SemaphoreType.DMA((n, NDEV - 1)), pltpu.SemaphoreType.DMA((n, NDEV - 1)),
                        pltpu.SemaphoreType.DMA((n,))]

    def _copies(self, in_refs, out_refs, sems):
        send_sems, recv_sems, local_sems = sems
        x, y, c = lax.axis_index("x"), lax.axis_index("y"), lax.axis_index("c")
        me = 4 * x + 2 * y + c
        peers = [(1 - x if k & 4 else x, 1 - y if k & 2 else y, 1 - c if k & 1 else c) for k in range(1, NDEV)]
        local, send, recv = [], [], []
        for a in range(self.n):
            src = (lambda pid, a=a: in_refs[a]) if self.gather else (lambda pid, a=a: in_refs[a].at[pid])
            local.append(pltpu.make_async_copy(src(me), out_refs[a].at[me], local_sems.at[a]))
            for k, p in enumerate(peers):
                pid = 4 * p[0] + 2 * p[1] + p[2]
                for lst, slot in ((send, me), (recv, pid)):
                    lst.append(pltpu.make_async_remote_copy(
                        src_ref=src(pid), dst_ref=out_refs[a].at[slot],
                        send_sem=send_sems.at[a, k], recv_sem=recv_sems.at[a, k],
                        device_id=p, device_id_type=MESH))
        return local, send, recv

    def start(self, first, in_refs, out_refs, sems):
        @pl.when(first)
        def _():
            local, send, _ = self._copies(in_refs, out_refs, sems)
            for cp in local + send:
                cp.start()

    def finish(self, last, in_refs, out_refs, sems):
        @pl.when(last)
        def _():
            local, send, recv = self._copies(in_refs, out_refs, sems)
            for cp in send:
                cp.wait_send()
            for cp in recv:
                cp.wait_recv()
            for cp in local:
                cp.wait()


def _host(exchange, n_in, n_out, body, first_last):
    if exchange is None:
        return body, [], [], [], []
    n = exchange.n

    def hosted(*refs):
        ins, refs = refs[:n_in], refs[n_in:]
        xin, refs = refs[:n], refs[n:]
        outs, refs = refs[:n_out], refs[n_out:]
        xout, refs = refs[:n], refs[n:]
        scratch, sems = refs[:len(refs) - 3], refs[len(refs) - 3:]
        first, last = first_last()
        exchange.start(first, xin, xout, sems)
        body(*ins, *outs, *scratch)
        exchange.finish(last, xin, xout, sems)

    return hosted, exchange.arrays, [ANY] * n, exchange.out_shape, exchange.scratch


def _first_last(steps):
    def at():
        i = pl.program_id(0)
        return i == 0, i == steps - 1
    return at


class _ColBlocks:
    def __init__(self, ref, width):
        self.ref, self.width = ref, width

    def _index(self, key):
        key = key if isinstance(key, tuple) else (key,)
        rows = key[1] if len(key) > 1 else slice(None)
        cols = key[2] if len(key) > 2 else slice(0, self.width)
        c0 = key[0] * self.width
        return rows, slice(c0 + cols.start, c0 + cols.stop)

    def __getitem__(self, key):
        return self.ref[self._index(key)]

    def __setitem__(self, key, value):
        self.ref[self._index(key)] = value


def _col_chunks(width, chunk=768):
    return [slice(c, min(c + chunk, width)) for c in range(0, width, chunk)]


def _row_spec(tm, d):
    return pl.BlockSpec((tm, d), lambda m: (m, 0))


def _vec_spec(d):
    return pl.BlockSpec((1, d), lambda m: (0, 0))


def _whole_spec(w):
    nd = w.ndim
    return pl.BlockSpec(w.shape, lambda m: (0,) * nd, pipeline_mode=pl.Buffered(1))


def _mm_blocks(a, w, name, *, bias=None, out_dtype=BF16, tm=512, chunk=768, gelu=False, f32_cols=None,
               exchange=None):
    T, K = a.shape
    C = w.shape[0]
    tm = min(tm, T)
    n_in = 2 + (bias is not None)

    def body(*refs):
        a_ref, w_ref = refs[:2]
        av = a_ref[...]
        for cols in _col_chunks(C, chunk):
            r = _dot(av, w_ref[cols], NT)
            if bias is not None:
                r = r + refs[2][:, cols]
            if gelu:
                z, dz = _gelu_and_grad(r)
                refs[n_in][:, cols] = z.astype(BF16)
                refs[n_in + 1][:, cols] = dz.astype(BF16)
            elif f32_cols is None:
                refs[n_in][:, cols] = r.astype(out_dtype)
            elif f32_cols[0] <= cols.start < f32_cols[1]:
                refs[n_in + 1][:, cols.start - f32_cols[0]:cols.stop - f32_cols[0]] = r
            else:
                shift = 0 if cols.start < f32_cols[0] else f32_cols[1] - f32_cols[0]
                refs[n_in][:, cols.start - shift:cols.stop - shift] = r.astype(BF16)

    in_specs = [_row_spec(tm, K), _whole_spec(w)]
    args = [a, w]
    if bias is not None:
        in_specs.append(_whole_spec(bias))
        args.append(bias)
    if gelu:
        outs = [jax.ShapeDtypeStruct((T, C), BF16)] * 2
    elif f32_cols is not None:
        wide = f32_cols[1] - f32_cols[0]
        assert f32_cols[0] % chunk == 0 and wide % chunk == 0
        outs = [jax.ShapeDtypeStruct((T, C - wide), BF16), jax.ShapeDtypeStruct((T, wide), F32)]
    else:
        outs = [jax.ShapeDtypeStruct((T, C), out_dtype)]
    body, x_args, x_in, x_out, x_scratch = _host(exchange, n_in, len(outs), body, _first_last(T // tm))
    res = pl.pallas_call(
        body, name=name, grid=(T // tm,),
        out_shape=outs + x_out,
        in_specs=in_specs + x_in,
        out_specs=[_row_spec(tm, o.shape[1]) for o in outs] + x_in,
        scratch_shapes=x_scratch,
        compiler_params=_params("arbitrary" if exchange else "parallel"),
    )(*args, *x_args)
    return res if len(res) > 1 else res[0]


def _rms(v):
    return lax.rsqrt(_rowmean(v * v) + EPS)


def _zero_at_start(*refs):
    @pl.when(pl.program_id(0) == 0)
    def _():
        for r in refs:
            r[...] = jnp.zeros_like(r)


def _postnorm_bwd_math(dxo, yv, g, gate, res_w, dgate_ref, dpost_ref):
    r = _rms(yv)
    yh = yv * r
    both = res_w * _colsum(dxo * yh)
    dgate_ref[...] += g * both
    dpost_ref[...] += gate * both
    dyh = dxo * (res_w * gate * g)
    return (r * (dyh - yh * _rowmean(dyh * yh))).astype(BF16)


def _out_proj(a, w, x, post_g, gate, res_w, nxt, name, *, tm=512):
    T, ka = a.shape
    d = w.shape[1]
    tm = min(tm, T)

    def body(a_ref, w_ref, x_ref, pg_ref, gate_ref, ng_ref, nsc_ref, nsh_ref, y_ref, xn_ref, h_ref):
        y = _dot(a_ref[...], w_ref[...])
        y_ref[...] = y
        xn = x_ref[...] + (y * _rms(y)) * (res_w * gate_ref[...] * pg_ref[...])
        xn_ref[...] = xn
        h_ref[...] = ((xn * _rms(xn)) * (ng_ref[...] * (1.0 + nsc_ref[...])) + nsh_ref[...]).astype(BF16)

    return pl.pallas_call(
        body, name=name, grid=(T // tm,),
        out_shape=(jax.ShapeDtypeStruct((T, d), F32), jax.ShapeDtypeStruct((T, d), F32),
                   jax.ShapeDtypeStruct((T, d), BF16)),
        in_specs=[_row_spec(tm, ka), _whole_spec(w), _row_spec(tm, d)] + [_vec_spec(d)] * 5,
        out_specs=(_row_spec(tm, d),) * 3,
        compiler_params=_params("parallel"),
    )(a, w, x, post_g, gate, *nxt)


def _out_proj_last(a, w, x, post_g, gate, res_w, target, name, *, tm=512):
    T, ka = a.shape
    d = w.shape[1]
    tm = min(tm, T)

    def body(a_ref, w_ref, x_ref, pg_ref, gate_ref, t_ref, dx_ref, dy_ref, dgate_ref, dpost_ref, l_ref):
        _zero_at_start(dgate_ref, dpost_ref, l_ref)
        y = _dot(a_ref[...], w_ref[...])
        e = x_ref[...] + res_w * gate_ref[...] * (y * _rms(y) * pg_ref[...]) - t_ref[...]
        l_ref[...] += 0.5 * jnp.sum(_rowmean(e * e), axis=0, keepdims=True)
        dx = e * (1.0 / d)
        dx_ref[...] = dx
        dy_ref[...] = _postnorm_bwd_math(dx, y, pg_ref[...], gate_ref[...], res_w, dgate_ref, dpost_ref)

    return pl.pallas_call(
        body, name=name, grid=(T // tm,),
        out_shape=(jax.ShapeDtypeStruct((T, d), F32), jax.ShapeDtypeStruct((T, d), BF16),
                   jax.ShapeDtypeStruct((1, d), F32), jax.ShapeDtypeStruct((1, d), F32),
                   jax.ShapeDtypeStruct((1, 128), F32)),
        in_specs=[_row_spec(tm, ka), _whole_spec(w), _row_spec(tm, d), _vec_spec(d), _vec_spec(d), _row_spec(tm, d)],
        out_specs=(_row_spec(tm, d), _row_spec(tm, d), _vec_spec(d), _vec_spec(d),
                   pl.BlockSpec((1, 128), lambda m: (0, 0))),
        compiler_params=_params("arbitrary"),
    )(a, w, x, post_g, gate, target)


def _in_grad(dz, w, dxo, x, pre_g, scale, prev, name, *, tm=512, exchange=None):
    T, C = dz.shape
    d = w.shape[1]
    tm = min(tm, T)
    has_prev = prev is not None
    res_w = prev[3] if has_prev else None

    def body(*refs):
        dz_ref, w_ref, dxo_ref, x_ref, g_ref, sc_ref = refs[:6]
        if has_prev:
            yp_ref, ppg_ref, pgate_ref, dx_ref, dsh_ref, dsc_ref, dg_ref, dyp_ref, dgate_ref, dpost_ref = refs[6:]
            _zero_at_start(dsh_ref, dsc_ref, dg_ref, dgate_ref, dpost_ref)
        else:
            dx_ref, dsh_ref, dsc_ref, dg_ref = refs[6:]
            _zero_at_start(dsh_ref, dsc_ref, dg_ref)
        dh = _dot(dz_ref[...], w_ref[...])
        xv = x_ref[...]
        r = _rms(xv)
        xh = xv * r
        gain = 1.0 + sc_ref[...]
        both = _colsum(dh * xh)
        dsh_ref[...] += _colsum(dh)
        dsc_ref[...] += g_ref[...] * both
        dg_ref[...] += gain * both
        dxh = dh * (gain * g_ref[...])
        dx = dxo_ref[...] + r * (dxh - xh * _rowmean(dxh * xh))
        dx_ref[...] = dx
        if has_prev:
            dyp_ref[...] = _postnorm_bwd_math(dx, yp_ref[...], ppg_ref[...], pgate_ref[...], res_w,
                                               dgate_ref, dpost_ref)

    vec = jax.ShapeDtypeStruct((1, d), F32)
    in_specs = [_row_spec(tm, C), _whole_spec(w), _row_spec(tm, d), _row_spec(tm, d), _vec_spec(d), _vec_spec(d)]
    out_shape = [jax.ShapeDtypeStruct((T, d), F32), vec, vec, vec]
    out_specs = [_row_spec(tm, d), _vec_spec(d), _vec_spec(d), _vec_spec(d)]
    args = [dz, w, dxo, x, pre_g, scale]
    if has_prev:
        in_specs += [_row_spec(tm, d), _vec_spec(d), _vec_spec(d)]
        out_shape += [jax.ShapeDtypeStruct((T, d), BF16), vec, vec]
        out_specs += [_row_spec(tm, d), _vec_spec(d), _vec_spec(d)]
        args += list(prev[:3])
    body, x_args, x_in, x_out, x_scratch = _host(exchange, len(args), len(out_shape), body, _first_last(T // tm))
    return pl.pallas_call(
        body, name=name, grid=(T // tm,),
        out_shape=out_shape + x_out, in_specs=in_specs + x_in, out_specs=out_specs + x_in,
        scratch_shapes=x_scratch,
        compiler_params=_params("arbitrary"),
    )(*args, *x_args)


def _mm_wgrad(x, y, name, *, xw=None, yw=None, split=1, tt=2048):
    T, P = x.shape
    Q = y.shape[1]
    xw, yw = xw or P, yw or Q
    jx, jy = P // xw, Q // yw
    assert jx == 1 or jy == 1
    tt = min(tt, T)
    nt = T // tt
    part = yw // split

    def body(x_ref, y_ref, o_ref, acc_ref):
        t = pl.program_id(1)

        @pl.when(t == 0)
        def _():
            acc_ref[...] = jnp.zeros_like(acc_ref)

        acc_ref[...] += _dot(x_ref[...], y_ref[...], TN)

        @pl.when(t == nt - 1)
        def _():
            if jy > 1:
                for k in range(split):
                    o_ref[k] = acc_ref[:, part * k:part * (k + 1)].astype(BF16)
            else:
                o_ref[...] = acc_ref[...].astype(BF16)

    if jy > 1:
        out_shape = jax.ShapeDtypeStruct((jy * split, P, part), BF16)
        out_spec = pl.BlockSpec((split, P, part), lambda j, t: (j, 0, 0))
    else:
        out_shape = jax.ShapeDtypeStruct((P, Q), BF16)
        out_spec = pl.BlockSpec((xw, Q), lambda j, t: (j, 0))
    return pl.pallas_call(
        body, name=name, grid=(max(jx, jy), nt),
        out_shape=out_shape,
        in_specs=[pl.BlockSpec((tt, xw), (lambda j, t: (t, j)) if jx > 1 else (lambda j, t: (t, 0))),
                  pl.BlockSpec((tt, yw), (lambda j, t: (t, j)) if jy > 1 else (lambda j, t: (t, 0)))],
        out_specs=out_spec,
        scratch_shapes=[pltpu.VMEM((xw, yw), F32)],
        compiler_params=_params("parallel", "arbitrary"),
    )(x, y)


def _ffn_in(h, wt, name, *, tm=512, opening=None, exchange=None):
    T, K = h.shape
    F = wt.shape[0] // 2
    tm = min(tm, T)
    n_vec = 3 if opening else 0

    def body(*refs):
        h_ref, w_ref = refs[0], refs[1 + n_vec]
        pq_ref, g_ref = refs[2 + n_vec], refs[3 + n_vec]
        hh = h_ref[...]
        if opening:
            g_vec, sc_vec, sh_vec = refs[1:4]
            hh = (hh * _rms(hh) * (g_vec[...] * (1.0 + sc_vec[...])) + sh_vec[...]).astype(BF16)
            refs[4 + n_vec][...] = hh
        for cols in _col_chunks(F):
            hi = slice(F + cols.start, F + cols.stop)
            a = _dot(hh, w_ref[cols], NT)
            b = _dot(hh, w_ref[hi], NT)
            s = _sigmoid_t(a)
            silu = a * s
            pq_ref[:, cols] = (b * (s * (1.0 + a * (1.0 - s)))).astype(BF16)
            pq_ref[:, hi] = silu.astype(BF16)
            g_ref[:, cols] = (silu * b).astype(BF16)

    outs = [jax.ShapeDtypeStruct((T, 2 * F), BF16), jax.ShapeDtypeStruct((T, F), BF16)]
    out_specs = [_row_spec(tm, 2 * F), _row_spec(tm, F)]
    if opening:
        outs.append(jax.ShapeDtypeStruct((T, K), BF16))
        out_specs.append(_row_spec(tm, K))
    vecs = list(opening) if opening else []
    body, x_args, x_in, x_out, x_scratch = _host(exchange, 2 + n_vec, len(outs), body, _first_last(T // tm))
    res = pl.pallas_call(
        body, name=name, grid=(T // tm,),
        out_shape=outs + x_out,
        in_specs=[_row_spec(tm, K)] + [_vec_spec(K)] * n_vec + [_whole_spec(wt)] + x_in,
        out_specs=out_specs + x_in,
        scratch_shapes=x_scratch,
        compiler_params=_params("arbitrary" if exchange else "parallel"),
    )(h, *vecs, wt, *x_args)
    return res[:len(outs)], res[len(outs):]


def _ffn_dgate(dy, w_out, pq, name, *, tm=512):
    T, N = dy.shape
    F = w_out.shape[0]
    tm = min(tm, T)

    def body(dy_ref, w_ref, pq_ref, dz_ref):
        dyv = dy_ref[...]
        for cols in _col_chunks(F):
            hi = slice(F + cols.start, F + cols.stop)
            dg = _dot(dyv, w_ref[cols], NT)
            dz_ref[:, cols] = (dg * pq_ref[:, cols].astype(F32)).astype(BF16)
            dz_ref[:, hi] = (dg * pq_ref[:, hi].astype(F32)).astype(BF16)

    return pl.pallas_call(
        body, name=name, grid=(T // tm,),
        out_shape=jax.ShapeDtypeStruct((T, 2 * F), BF16),
        in_specs=[_row_spec(tm, N), _whole_spec(w_out), _row_spec(tm, 2 * F)],
        out_specs=_row_spec(tm, 2 * F),
        compiler_params=_params("parallel"),
    )(dy, w_out, pq)


def _ada_fwd(c_all, w, b, name):
    L, K, n = w.shape

    def body(c_ref, w_ref, b_ref, o_ref):
        cv = c_ref[...]
        cond = cv * _sigmoid(cv)
        for l in range(L):
            o_ref[l] = _dot(cond, w_ref[l], precision=lax.Precision.HIGHEST) + b_ref[l]

    return pl.pallas_call(
        body, name=name,
        out_shape=jax.ShapeDtypeStruct((L, NDEV, n), F32),
        compiler_params=pltpu.CompilerParams(vmem_limit_bytes=VMEM_LIMIT),
    )(c_all, w, b)


def _ada_bwd(c_all_t, gmod, name):
    L, _, n = gmod.shape
    K = c_all_t.shape[0]

    def body(c_ref, g_ref, o_ref):
        cv = c_ref[...]
        cond = cv * _sigmoid(cv)
        for l in range(L):
            o_ref[l] = _dot(cond, g_ref[l], precision=lax.Precision.HIGHEST)

    return pl.pallas_call(
        body, name=name,
        out_shape=jax.ShapeDtypeStruct((L, K, n), F32),
        compiler_params=pltpu.CompilerParams(vmem_limit_bytes=VMEM_LIMIT),
    )(c_all_t, gmod)


def _tri(n, upper=False, block=None):
    r = lax.broadcasted_iota(jnp.int32, (n, n), 0)
    c = lax.broadcasted_iota(jnp.int32, (n, n), 1)
    m = (c >= r) if upper else (c <= r)
    if block is not None:
        m = m & ((r // block) == (c // block))
    return m.astype(BF16)


TRI_ROWS = 128


def _tri_dot(tri, x):
    hi = x.astype(BF16)
    lo = (x - hi.astype(F32)).astype(BF16)
    rows = x.shape[0]
    step = min(TRI_ROWS, rows)
    parts = [_dot(tri, hi[r:r + step]) + _dot(tri, lo[r:r + step]) for r in range(0, rows, step)]
    return parts[0] if len(parts) == 1 else jnp.concatenate(parts, axis=0)


def _hgrn_gates(proj_ref, f_ref, lb_ref, jh):
    lb = lb_ref[:, 512 * jh:512 * (jh + 1)]
    qp = _ColBlocks(proj_ref, 512)[jh].astype(F32)
    fx = _ColBlocks(f_ref, 512)[jh]
    sq = _sigmoid_t(qp)
    sig = _sigmoid_t(fx)
    f = lb + (1.0 - lb) * sig
    k = (1.0 - lb) * (1.0 - sig)
    return lb, qp, sq, sig, f, k


def _hgrn_fwd(proj, lb, out_norm, name, *, tb=512, exchange=None):
    T = proj[0].shape[0]
    tb = min(tb, T)
    nc = tb // HG_CHUNK
    lmat = _tri(min(TRI_ROWS, tb), block=HG_CHUNK)

    def body(proj_ref, f_ref, lb_ref, on_ref, l_ref, o_ref, og_ref, st_ref, s_scr, b_scr):
        @pl.when(pl.program_id(0) == 0)
        def _():
            s_scr[...] = jnp.zeros_like(s_scr)

        r_i = lax.broadcasted_iota(jnp.int32, (HG_CHUNK, HG_CHUNK), 0)
        c_i = lax.broadcasted_iota(jnp.int32, (HG_CHUNK, HG_CHUNK), 1)
        causal = c_i <= r_i
        onv = on_ref[...]
        blocks = _ColBlocks(proj_ref, 512)
        for jh in range(2):
            lbv, qp, sq, sig, f, k = _hgrn_gates(proj_ref, f_ref, lb_ref, jh)
            q = qp * sq
            b_half = b_scr.at[jh]
            b_half[...] = _tri_dot(l_ref[...], jnp.log(f))
            v = blocks[2 + jh]
            gp = blocks[4 + jh].astype(F32)
            gs = gp * _sigmoid_t(gp)
            for hh in range(4):
                hd = 4 * jh + hh
                cs = slice(HG_HEAD * hh, HG_HEAD * (hh + 1))
                for ci in range(nc):
                    r0 = HG_CHUNK * ci
                    rs = slice(r0, r0 + HG_CHUNK)
                    bc = b_half[rs, cs]
                    bm = b_half[r0 + HG_CHUNK // 2 - 1:r0 + HG_CHUNK // 2, cs]
                    bl = b_half[r0 + HG_CHUNK - 1:r0 + HG_CHUNK, cs]
                    qc, kc, vc = q[rs, cs], k[rs, cs], v[rs, cs].astype(BF16)
                    e_q, e_k = jnp.exp(bc - bm), jnp.exp(bm - bc)
                    qe = (qc * (e_q * jnp.exp(bm))).astype(BF16)
                    qt = (qc * e_q).astype(BF16)
                    kt = (kc * e_k).astype(BF16)
                    kd = (kc * (e_k * jnp.exp(bl - bm))).astype(BF16)
                    st = s_scr[hd]
                    stb = st.astype(BF16)
                    st_ref[ci, hd] = stb
                    a = jnp.where(causal, _dot(qt, kt, NT), 0.0).astype(BF16)
                    o = _dot(qe, stb, NT) + _dot(a, vc)
                    s_scr[hd] = st * jnp.exp(bl) + _dot(vc, kd, TN)
                    o_ref[rs, HG_HEAD * hd:HG_HEAD * (hd + 1)] = o
                    r = lax.rsqrt(_rowmean(o * o) + EPS)
                    og_ref[rs, HG_HEAD * hd:HG_HEAD * (hd + 1)] = (o * r * onv * gs[rs, cs]).astype(BF16)

    body, x_args, x_in, x_out, x_scratch = _host(exchange, 5, 3, body, _first_last(T // tb))
    res = pl.pallas_call(
        body, name=name, grid=(T // tb,),
        out_shape=[jax.ShapeDtypeStruct((T, D_MODEL), F32), jax.ShapeDtypeStruct((T, D_MODEL), BF16),
                   jax.ShapeDtypeStruct((T // HG_CHUNK, HG_HEADS, HG_HEAD, HG_HEAD), BF16)] + x_out,
        in_specs=[pl.BlockSpec((tb, 3 * D_MODEL), lambda i: (i, 0)),
                  pl.BlockSpec((tb, D_MODEL), lambda i: (i, 0)),
                  pl.BlockSpec((1, D_MODEL), lambda i: (0, 0)),
                  pl.BlockSpec((1, HG_HEAD), lambda i: (0, 0)),
                  pl.BlockSpec(lmat.shape, lambda i: (0, 0))] + x_in,
        out_specs=[pl.BlockSpec((tb, D_MODEL), lambda i: (i, 0)),
                   pl.BlockSpec((tb, D_MODEL), lambda i: (i, 0)),
                   pl.BlockSpec((nc, HG_HEADS, HG_HEAD, HG_HEAD), lambda i: (i, 0, 0, 0))] + x_in,
        scratch_shapes=[pltpu.VMEM((HG_HEADS, HG_HEAD, HG_HEAD), F32), pltpu.VMEM((2, tb, 512), F32)] + x_scratch,
        compiler_params=_params("arbitrary"),
    )(*proj, lb, out_norm, lmat, *x_args)
    return res[0], res[1], res[2], res[3:]


def _hgrn_bwd(proj, o, dog, states, lb, out_norm, name, *, tb=512):
    T = proj[0].shape[0]
    tb = min(tb, T)
    nc = tb // HG_CHUNK
    nb = T // tb
    lmat = _tri(min(TRI_ROWS, tb), block=HG_CHUNK)
    umat = _tri(min(TRI_ROWS, tb), upper=True, block=HG_CHUNK)

    def body(proj_ref, f_ref, o_ref, dog_ref, st_ref, lb_ref, on_ref, l_ref, u_ref,
             dproj_ref, dlb_ref, don_ref, ds_scr, *half_scr):
        @pl.when(pl.program_id(0) == 0)
        def _():
            ds_scr[...] = jnp.zeros_like(ds_scr)
            dlb_ref[...] = jnp.zeros_like(dlb_ref)
            don_ref[...] = jnp.zeros_like(don_ref)

        r_i = lax.broadcasted_iota(jnp.int32, (HG_CHUNK, HG_CHUNK), 0)
        c_i = lax.broadcasted_iota(jnp.int32, (HG_CHUNK, HG_CHUNK), 1)
        causal = c_i <= r_i
        causal_t = r_i <= c_i
        last_row = lax.broadcasted_iota(jnp.int32, (HG_CHUNK, HG_HEAD), 0) == HG_CHUNK - 1
        onv = on_ref[...]
        don_acc = jnp.zeros((1, HG_HEAD), F32)
        blocks = _ColBlocks(proj_ref, 512)
        dproj_ref = _ColBlocks(dproj_ref, 512)
        for jh in range(2):
            b_scr, dq_scr, dk_scr, dv_scr, dg_scr, db_scr = [s.at[jh] for s in half_scr]
            lbv, qp, sq, sig, f, k = _hgrn_gates(proj_ref, f_ref, lb_ref, jh)
            q = qp * sq
            b_scr[...] = _tri_dot(l_ref[...], jnp.log(f))
            v = blocks[2 + jh]
            gp = blocks[4 + jh].astype(F32)
            sg = _sigmoid_t(gp)
            for ci in reversed(range(nc)):
                r0 = HG_CHUNK * ci
                rs = slice(r0, r0 + HG_CHUNK)
                for hh in range(4):
                    hd = 4 * jh + hh
                    cs = slice(HG_HEAD * hh, HG_HEAD * (hh + 1))
                    hs = slice(HG_HEAD * hd, HG_HEAD * (hd + 1))
                    oc = o_ref[rs, hs]
                    r = lax.rsqrt(_rowmean(oc * oc) + EPS)
                    oh = oc * r
                    gc, sgc = gp[rs, cs], sg[rs, cs]
                    dogc = dog_ref[rs, hs].astype(F32)
                    don = dogc * (gc * sgc)
                    dg_scr[rs, cs] = dogc * (oh * onv) * (sgc * (1.0 + gc * (1.0 - sgc)))
                    don_acc += _colsum(don * oh)
                    donh = don * onv
                    do = (r * (donh - oh * _rowmean(donh * oh))).astype(BF16)
                    bc = b_scr[rs, cs]
                    bm = b_scr[r0 + HG_CHUNK // 2 - 1:r0 + HG_CHUNK // 2, cs]
                    bl = b_scr[r0 + HG_CHUNK - 1:r0 + HG_CHUNK, cs]
                    qc, kc, vc = q[rs, cs], k[rs, cs], v[rs, cs].astype(BF16)
                    e_q, e_k = jnp.exp(bc - bm), jnp.exp(bm - bc)
                    e_b, e_d = e_q * jnp.exp(bm), e_k * jnp.exp(bl - bm)
                    qe = (qc * e_b).astype(BF16)
                    qt = (qc * e_q).astype(BF16)
                    kt = (kc * e_k).astype(BF16)
                    kd = (kc * e_d).astype(BF16)
                    stb = st_ref[ci, hd]
                    dst = ds_scr[hd]
                    dstb = dst.astype(BF16)
                    a_t = jnp.where(causal_t, _dot(kt, qt, NT), 0.0).astype(BF16)
                    da = jnp.where(causal, _dot(do, vc, NT), 0.0).astype(BF16)
                    da_t = jnp.where(causal_t, _dot(vc, do, NT), 0.0).astype(BF16)
                    dv_scr[rs, cs] = _dot(a_t, do) + _dot(kd, dstb, NT)
                    dqe, dqt = _dot(do, stb), _dot(da, kt)
                    dkt, dkd = _dot(da_t, qt), _dot(vc, dstb)
                    dq_scr[rs, cs] = dqe * e_b + dqt * e_q
                    dk_scr[rs, cs] = dkt * e_k + dkd * e_d
                    e_l = jnp.exp(bl)
                    kd_dkd = kd.astype(F32) * dkd
                    dbc = qe.astype(F32) * dqe + qt.astype(F32) * dqt - kt.astype(F32) * dkt - kd_dkd
                    handed = e_l * _colsum(dstb.astype(F32) * stb.astype(F32)) + _colsum(kd_dkd)
                    db_scr[rs, cs] = dbc + jnp.where(last_row, handed, 0.0)
                    ds_scr[hd] = dst * e_l + _dot(do, qe, TN)
            dq = dq_scr[...]
            dk = dk_scr[...]
            cols = slice(512 * jh, 512 * (jh + 1))
            dlogf = _tri_dot(u_ref[...], db_scr[...])
            one_m_sig = 1.0 - sig
            dsig = (1.0 - lbv) * sig * one_m_sig
            dboth = dlogf / f - dk
            dproj_ref[jh] = (dq * (sq * (1.0 + qp * (1.0 - sq)))).astype(BF16)
            dproj_ref[2 + jh] = (dboth * dsig).astype(BF16)
            dproj_ref[4 + jh] = dv_scr[...].astype(BF16)
            dproj_ref[6 + jh] = dg_scr[...].astype(BF16)
            dlb_ref[:, cols] += _colsum(dboth * one_m_sig)
        don_ref[...] += don_acc

    rev = lambda i: nb - 1 - i
    return pl.pallas_call(
        body, name=name, grid=(nb,),
        out_shape=(jax.ShapeDtypeStruct((T, 4 * D_MODEL), BF16), jax.ShapeDtypeStruct((1, D_MODEL), F32),
                   jax.ShapeDtypeStruct((1, HG_HEAD), F32)),
        in_specs=[pl.BlockSpec((tb, 3 * D_MODEL), lambda i: (rev(i), 0)),
                  pl.BlockSpec((tb, D_MODEL), lambda i: (rev(i), 0)),
                  pl.BlockSpec((tb, D_MODEL), lambda i: (rev(i), 0)),
                  pl.BlockSpec((tb, D_MODEL), lambda i: (rev(i), 0)),
                  pl.BlockSpec((nc, HG_HEADS, HG_HEAD, HG_HEAD), lambda i: (rev(i), 0, 0, 0)),
                  pl.BlockSpec((1, D_MODEL), lambda i: (0, 0)),
                  pl.BlockSpec((1, HG_HEAD), lambda i: (0, 0)),
                  pl.BlockSpec(lmat.shape, lambda i: (0, 0)),
                  pl.BlockSpec(lmat.shape, lambda i: (0, 0))],
        out_specs=(pl.BlockSpec((tb, 4 * D_MODEL), lambda i: (rev(i), 0)),
                   pl.BlockSpec((1, D_MODEL), lambda i: (0, 0)),
                   pl.BlockSpec((1, HG_HEAD), lambda i: (0, 0))),
        scratch_shapes=[pltpu.VMEM((HG_HEADS, HG_HEAD, HG_HEAD), F32)] + [pltpu.VMEM((2, tb, 512), F32)] * 6,
        compiler_params=_params("arbitrary"),
    )(*proj, o, dog, states, lb, out_norm, lmat, umat)


def _gm_norm(pre_ref, lg_ref, lbias_ref):
    pre_ref = _ColBlocks(pre_ref, 768)
    vs = [pre_ref[4 + j].astype(F32) for j in range(4)]
    width = 4 * vs[0].shape[1]
    mu = sum(jnp.sum(v, axis=1, keepdims=True) for v in vs) / width
    ds = [v - mu for v in vs]
    var = sum(jnp.sum(d * d, axis=1, keepdims=True) for d in ds) / width
    rstd = lax.rsqrt(var + EPS)
    vhat = [d * rstd for d in ds]
    vn = [vhat[j] * lg_ref[j:j + 1, :] + lbias_ref[j:j + 1, :] for j in range(4)]
    return vhat, vn, rstd


def _gm_spatial_fwd(pre, ln_g, ln_b, ws, bsb, name, *, tb=512):
    T = pre.shape[0]
    tb = min(tb, T)
    nc = tb // GM_CHUNK

    def body(pre_ref, lg_ref, lbias_ref, ws_ref, bs_ref, o_ref):
        _, vn, _ = _gm_norm(pre_ref, lg_ref, lbias_ref)
        pre_ref, o_ref = _ColBlocks(pre_ref, 768), _ColBlocks(o_ref, 768)
        for j in range(4):
            u = pre_ref[j].astype(F32)
            for e in range(2):
                g = 2 * j + e
                cs = slice(GM_GDIM * e, GM_GDIM * (e + 1))
                wg = ws_ref[g].astype(BF16)
                for ci in range(nc):
                    rs = slice(GM_CHUNK * ci, GM_CHUNK * (ci + 1))
                    vm = _dot(wg, vn[j][rs, cs].astype(BF16)) + bs_ref[g]
                    o_ref[j, rs, cs] = (u[rs, cs] * vm).astype(BF16)

    return pl.pallas_call(
        body, name=name, grid=(T // tb,),
        out_shape=jax.ShapeDtypeStruct((T, 4 * 768), BF16),
        in_specs=[pl.BlockSpec((tb, 8 * 768), lambda i: (i, 0)),
                  pl.BlockSpec((4, 768), lambda i: (0, 0)),
                  pl.BlockSpec((4, 768), lambda i: (0, 0)),
                  pl.BlockSpec((GM_GROUPS, GM_CHUNK, GM_CHUNK), lambda i: (0, 0, 0)),
                  pl.BlockSpec((GM_GROUPS, GM_CHUNK, GM_GDIM), lambda i: (0, 0, 0))],
        out_specs=pl.BlockSpec((tb, 4 * 768), lambda i: (i, 0)),
        compiler_params=_params("parallel"),
    )(pre, ln_g, ln_b, ws, bsb)


def _gm_spatial_bwd(pre, gp, dm, ln_g, ln_b, ws, ws_t, bsb, name, *, tb=256):
    T = pre.shape[0]
    tb = min(tb, T)
    nc = tb // GM_CHUNK
    nb = T // tb

    def body(pre_ref, gp_ref, dm_ref, lg_ref, lbias_ref, ws_ref, wst_ref, bs_ref,
             dpre_ref, dws_ref, dbs_ref, dlg_ref, dlb_ref, dbin_ref, dbs_scr, dvn_scr, du_scr):
        i = pl.program_id(0)

        @pl.when(i == 0)
        def _():
            dws_ref[...] = jnp.zeros_like(dws_ref)
            dbs_scr[...] = jnp.zeros_like(dbs_scr)
            dlg_ref[...] = jnp.zeros_like(dlg_ref)
            dlb_ref[...] = jnp.zeros_like(dlb_ref)
            dbin_ref[...] = jnp.zeros_like(dbin_ref)

        vhat, vn, rstd = _gm_norm(pre_ref, lg_ref, lbias_ref)
        pre_ref, gp_ref, dm_ref = _ColBlocks(pre_ref, 768), _ColBlocks(gp_ref, 768), _ColBlocks(dm_ref, 768)
        dpre_ref = _ColBlocks(dpre_ref, 768)
        for j in range(4):
            u = pre_ref[j].astype(F32)
            for e in range(2):
                g = 2 * j + e
                cs = slice(GM_GDIM * e, GM_GDIM * (e + 1))
                wg = ws_ref[g].astype(BF16)
                wgt = wst_ref[g].astype(BF16)
                for ci in range(nc):
                    rs = slice(GM_CHUNK * ci, GM_CHUNK * (ci + 1))
                    vnb = vn[j][rs, cs].astype(BF16)
                    vm = _dot(wg, vnb) + bs_ref[g]
                    dmg = dm_ref[j, rs, cs].astype(F32)
                    du_scr[j, rs, cs] = dmg * vm
                    dvm = dmg * u[rs, cs]
                    dvmb = dvm.astype(BF16)
                    dws_ref[g] += _dot(dvmb, vnb, NT)
                    dbs_scr[g] += dvm
                    dvn_scr[j, rs, cs] = _dot(wgt, dvmb)
        width = 4 * 768
        dvh = []
        for j in range(4):
            dvn = dvn_scr[j]
            dlg_ref[j:j + 1, :] += _colsum(dvn * vhat[j])
            dlb_ref[j:j + 1, :] += _colsum(dvn)
            dvh.append(dvn * lg_ref[j:j + 1, :])
        m1 = sum(jnp.sum(d, axis=1, keepdims=True) for d in dvh) / width
        m2 = sum(jnp.sum(dvh[j] * vhat[j], axis=1, keepdims=True) for j in range(4)) / width
        for j in range(4):
            dv = rstd * (dvh[j] - m1 - vhat[j] * m2)
            dpv = dv * gp_ref[4 + j].astype(F32)
            dpu = du_scr[j] * gp_ref[j].astype(F32)
            dpre_ref[4 + j] = dpv.astype(BF16)
            dpre_ref[j] = dpu.astype(BF16)
            dbin_ref[4 + j:5 + j, :] += _colsum(dpv)
            dbin_ref[j:j + 1, :] += _colsum(dpu)

        @pl.when(i == nb - 1)
        def _():
            r_i = lax.broadcasted_iota(jnp.int32, (GM_CHUNK, GM_CHUNK), 0)
            c_i = lax.broadcasted_iota(jnp.int32, (GM_CHUNK, GM_CHUNK), 1)
            for g in range(GM_GROUPS):
                dws_ref[g] = jnp.where(c_i <= r_i, dws_ref[g], 0.0)
                dbs_ref[g] = jnp.broadcast_to(jnp.sum(dbs_scr[g], axis=1, keepdims=True), (GM_CHUNK, GM_CHUNK))

    sq = pl.BlockSpec((GM_GROUPS, GM_CHUNK, GM_CHUNK), lambda i: (0, 0, 0))
    v4 = pl.BlockSpec((4, 768), lambda i: (0, 0))
    return pl.pallas_call(
        body, name=name, grid=(nb,),
        out_shape=(jax.ShapeDtypeStruct((T, 8 * 768), BF16),
                   jax.ShapeDtypeStruct((GM_GROUPS, GM_CHUNK, GM_CHUNK), F32),
                   jax.ShapeDtypeStruct((GM_GROUPS, GM_CHUNK, GM_CHUNK), F32),
                   jax.ShapeDtypeStruct((4, 768), F32), jax.ShapeDtypeStruct((4, 768), F32),
                   jax.ShapeDtypeStruct((8, 768), F32)),
        in_specs=[pl.BlockSpec((tb, 8 * 768), lambda i: (i, 0)),
                  pl.BlockSpec((tb, 8 * 768), lambda i: (i, 0)),
                  pl.BlockSpec((tb, 4 * 768), lambda i: (i, 0)),
                  v4, v4, sq, sq,
                  pl.BlockSpec((GM_GROUPS, GM_CHUNK, GM_GDIM), lambda i: (0, 0, 0))],
        out_specs=(pl.BlockSpec((tb, 8 * 768), lambda i: (i, 0)), sq, sq, v4, v4,
                   pl.BlockSpec((8, 768), lambda i: (0, 0))),
        scratch_shapes=[pltpu.VMEM((GM_GROUPS, GM_CHUNK, GM_GDIM), F32),
                        pltpu.VMEM((4, tb, 768), F32), pltpu.VMEM((4, tb, 768), F32)],
        compiler_params=_params("arbitrary"),
    )(pre, gp, dm, ln_g, ln_b, ws, ws_t, bsb)


def _adamw(slots, w, m, v, name, *, tr=256):
    S, R, C = slots.shape
    tr = next((t for t in (tr, tr // 2, tr // 4, tr // 8, tr // 16) if R % t == 0), R) if R > tr else R
    bc1 = 1.0 - ADAM_B1 ** ADAM_STEP
    bc2 = 1.0 - ADAM_B2 ** ADAM_STEP

    def body(s_ref, w_ref, m_ref, v_ref, g_ref, d_ref, nm_ref, nv_ref):
        g = s_ref[0].astype(F32)
        for s in range(1, S):
            g = g + s_ref[s].astype(F32)
        mn = ADAM_B1 * m_ref[...] + (1.0 - ADAM_B1) * g
        vn = ADAM_B2 * v_ref[...] + (1.0 - ADAM_B2) * (g * g)
        g_ref[...] = g
        nm_ref[...] = mn
        nv_ref[...] = vn
        d_ref[...] = -ADAM_LR * ((mn / bc1) / (jnp.sqrt(vn / bc2) + ADAM_EPS) + ADAM_WD * w_ref[...])

    spec = pl.BlockSpec((tr, C), lambda i: (i, 0))
    return pl.pallas_call(
        body, name=name, grid=(R // tr,),
        out_shape=(jax.ShapeDtypeStruct((R, C), F32),) * 4,
        in_specs=[pl.BlockSpec((S, tr, C), lambda i: (0, i, 0)), spec, spec, spec],
        out_specs=(spec,) * 4,
        compiler_params=_params("parallel"),
    )(slots, w, m, v)


def _update(slots, w, m, v, name):
    shp = w.shape
    C = shp[-1]
    R = math.prod(shp[:-1])
    outs = _adamw(slots.reshape(slots.shape[0], R, C), w.reshape(R, C), m.reshape(R, C), v.reshape(R, C), name)
    return tuple(o.reshape(shp) for o in outs)


def kernel(x, c, ada_w, ada_b, norm_pre, norm_post, ffn_w_in, ffn_w_out, hg_w_in, hg_w_out, hg_out_norm, hg_lb, gm_w_in, gm_b_in, gm_ln_g, gm_ln_b, gm_w_s, gm_b_s, gm_w_out, loss_target, m_ada_w, m_ada_b, m_norm_pre, m_norm_post, m_ffn_w_in, m_ffn_w_out, m_hg_w_in, m_hg_w_out, m_hg_out_norm, m_hg_lb, m_gm_w_in, m_gm_b_in, m_gm_ln_g, m_gm_ln_b, m_gm_w_s, m_gm_b_s, m_gm_w_out, v_ada_w, v_ada_b, v_norm_pre, v_norm_post, v_ffn_w_in, v_ffn_w_out, v_hg_w_in, v_hg_w_out, v_hg_out_norm, v_hg_lb, v_gm_w_in, v_gm_b_in, v_gm_ln_g, v_gm_ln_b, v_gm_w_s, v_gm_b_s, v_gm_w_out):
    me = 4 * lax.axis_index("x") + 2 * lax.axis_index("y") + lax.axis_index("c")
    T = x.shape[1]
    x0 = x.reshape(T, D_MODEL)
    target = loss_target.reshape(T, D_MODEL)
    n_ada = ada_w.shape[-1]

    pack = jnp.concatenate([
        c.reshape(8, 128), norm_pre.reshape(6, 128), norm_post.reshape(6, 128),
        gm_b_in.reshape(6, 128), gm_ln_g.reshape(3, 128), gm_ln_b.reshape(3, 128)], axis=0)
    packs = _all_gather(pack, "gather_small")
    c_all = packs[:, 0:8].reshape(NDEV, D_MODEL)
    npre = packs[:, 8:14].reshape(NDEV, 2, 3, 128).transpose(1, 2, 0, 3).reshape(2, 3, D_MODEL)
    npost = packs[:, 14:20].reshape(NDEV, 2, 3, 128).transpose(1, 2, 0, 3).reshape(2, 3, D_MODEL)
    b_in = packs[:, 20:26].reshape(1, NDEV * 768)
    ln_g = packs[:, 26:29].reshape(4, 768)
    ln_b = packs[:, 29:32].reshape(4, 768)

    ada_b_mine = lax.dynamic_slice_in_dim(ada_b, me * n_ada, n_ada, axis=1).reshape(2, 1, n_ada)
    mod_cols = _ada_fwd(c_all, ada_w, ada_b_mine, "ada_fwd")
    mod_all = _all_gather(mod_cols, "gather_mod")
    mod = lax.dynamic_index_in_dim(mod_all, me, axis=2, keepdims=False)
    mod = mod.transpose(1, 0, 2).reshape(2, 9, 1, D_MODEL)

    sh_fi, sh_fo = ffn_w_in.astype(BF16).swapaxes(-1, -2), ffn_w_out.astype(BF16)
    sh_hi, sh_ho = hg_w_in[0].astype(BF16).T, hg_w_out[0].astype(BF16)
    sh_mi, sh_mo = gm_w_in[0].astype(BF16).T, gm_w_out[0].astype(BF16)

    def whole(gathered):
        return gathered.reshape(-1, D_MODEL)

    w_fi = {(0, 0): whole(_all_gather(sh_fi[0, 0], "gather_ffn_in_first"))}
    w_fo = {}
    riders = {"l0s0": [sh_fo[0, 0], sh_hi, sh_ho], "l0s1": [sh_fi[0, 1], sh_fo[0, 1]], "hg_mix": [sh_mi, sh_mo],
              "l0s2": [sh_fi[1, 0], sh_fo[1, 0]], "l1s0": [sh_fi[1, 1], sh_fo[1, 1]]}

    sm = jax.nn.softmax(hg_lb, axis=0)
    lb0 = sm[0:1]
    on = hg_out_norm.reshape(1, HG_HEAD)
    tril = jnp.tril(jnp.ones((GM_CHUNK, GM_CHUNK), F32))
    ws = gm_w_s[0] * tril[None]
    ws_t = ws.transpose(0, 2, 1)
    bsb = jnp.broadcast_to(gm_b_s[0][:, :, None], (GM_GROUPS, GM_CHUNK, GM_GDIM))

    res_ws = (0.5, 1.0, 0.5)

    def vecs(i, s):
        return (npre[i, s].reshape(1, D_MODEL), npost[i, s].reshape(1, D_MODEL),
                mod[i, 3 * s], mod[i, 3 * s + 1], mod[i, 3 * s + 2])

    order = [(i, s) for i in range(2) for s in range(3)]
    saved = {}
    xs = x0
    for pos, (i, s) in enumerate(order):
        tag = f"l{i}s{s}"
        pre_g, post_g, shift, scale, gate = vecs(i, s)
        rider = _Exchange("gather", riders[tag]) if tag in riders else None
        if s != 1:
            if pos == 0:
                (pq, a, h), got = _ffn_in(xs, w_fi[0, 0], "ffn_in_" + tag, opening=(pre_g, scale, shift), exchange=rider)
            else:
                (pq, a), got = _ffn_in(h, w_fi[i, s // 2], "ffn_in_" + tag, exchange=rider)
            extra = (pq, a)
            if tag == "l0s0":
                w_fo[0, 0], w_hi, w_ho = map(whole, got)
            elif tag == "l0s2":
                w_fi[1, 0], w_fo[1, 0] = map(whole, got)
            elif tag == "l1s0":
                w_fi[1, 1], w_fo[1, 1] = map(whole, got)
            wo = w_fo[i, s // 2]
        elif i == 0:
            p_qig, p_f, *got = _mm_blocks(h, w_hi, "hg_in", chunk=1024, f32_cols=(D_MODEL, 2 * D_MODEL), exchange=rider)
            proj = (p_qig, p_f)
            w_fi[0, 1], w_fo[0, 1] = map(whole, got)
            o, og, states, got = _hgrn_fwd(proj, lb0, on, "hg_mix", exchange=_Exchange("gather", riders["hg_mix"]))
            w_mi, w_mo = map(whole, got)
            a, wo = og, w_ho
            extra = (proj, o, og, states)
        else:
            pre, gp = _mm_blocks(h, w_mi, "gm_in", bias=b_in, gelu=True)
            a = _gm_spatial_fwd(pre, ln_g, ln_b, ws, bsb, "gm_mix")
            wo = w_mo
            extra = (pre, gp, a)
        if pos + 1 < len(order):
            npre_g, _, nshift, nscale, _ = vecs(*order[pos + 1])
            y, x_next, h_next = _out_proj(a, wo, xs, post_g, gate, res_ws[s], (npre_g, nscale, nshift), "out_" + tag)
            saved[tag] = (xs, h, y) + extra
            xs, h = x_next, h_next
        else:
            dx, dy, dgate, dpost, loss_part = _out_proj_last(a, wo, xs, post_g, gate, res_ws[s], target, "out_" + tag)
            saved[tag] = (xs, h, None) + extra
    loss = lax.psum(loss_part[0, 0], ("x", "y", "c"))

    slots = {}
    d_npre = [[None] * 3, [None] * 3]
    d_npost = [[None] * 3, [None] * 3]
    d_mod = [[None] * 9, [None] * 9]
    for pos in reversed(range(len(order))):
        i, s = order[pos]
        tag = f"l{i}s{s}"
        pre_g, _, _, scale, _ = vecs(i, s)
        xin, h = saved[tag][:2]
        if s != 1:
            w_in, wo = w_fi[i, s // 2], w_fo[i, s // 2]
            pq, g = saved[tag][3:]
            dz = _ffn_dgate(dy, wo, pq, "ffn_dgate_" + tag)
            g_out = _mm_wgrad(g, dy, "ffn_out_wgrad_" + tag, xw=1408)
            g_in = _mm_wgrad(dz, h, "ffn_in_wgrad_" + tag, xw=1408)
        elif i == 0:
            proj, o, og, states = saved[tag][3:]
            dog = _mm_blocks(dy, w_ho, "hg_out_dgrad")
            g_out = _mm_wgrad(og, dy, "hg_out_wgrad")
            dz, d_lb0, d_on = _hgrn_bwd(proj, o, dog, states, lb0, on, "hg_mix_bwd")
            w_in = w_hi
            g_in = _mm_wgrad(h, dz, "hg_in_wgrad", yw=1024, split=2)
        else:
            pre, gp, sp = saved[tag][3:]
            dm = _mm_blocks(dy, w_mo, "gm_out_dgrad")
            g_out = _mm_wgrad(sp, dy, "gm_out_wgrad", xw=768)
            dz, d_ws, d_bs, d_lg, d_lbias, d_bin = _gm_spatial_bwd(pre, gp, dm, ln_g, ln_b, ws, ws_t, bsb, "gm_mix_bwd")
            w_in = w_mi
            g_in = _mm_wgrad(h, dz, "gm_in_wgrad", yw=1536, split=2)
        g_out = g_out.reshape(NDEV, -1, D_MODEL)
        g_in = g_in.reshape(NDEV, -1, D_MODEL) if s != 1 else g_in
        d_npost[i][s] = dpost
        d_mod[i][3 * s + 2] = dgate
        rider = _Exchange("scatter", [g_in, g_out])
        if pos > 0:
            pi, ps = order[pos - 1]
            _, ppost_g, _, _, pgate = vecs(pi, ps)
            prev = (saved[f"l{pi}s{ps}"][2], ppost_g, pgate, res_ws[ps])
            dx, dshift, dscale, dpre_g, dy, dgate, dpost, r_in, r_out = _in_grad(
                dz, w_in, dx, xin, pre_g, scale, prev, "in_grad_" + tag, exchange=rider)
        else:
            dx, dshift, dscale, dpre_g, r_in, r_out = _in_grad(
                dz, w_in, dx, xin, pre_g, scale, None, "in_grad_" + tag, exchange=rider)
        slots[tag] = (r_in, r_out)
        d_npre[i][s] = dpre_g
        d_mod[i][3 * s], d_mod[i][3 * s + 1] = dshift, dscale
    grad_x = dx.reshape(x.shape)

    ffn_tags = ["l0s0", "l0s2", "l1s0", "l1s2"]
    s_fi = jnp.stack([slots[t][0] for t in ffn_tags], axis=1).swapaxes(-1, -2)
    s_fo = jnp.stack([slots[t][1] for t in ffn_tags], axis=1)
    (s_hi, s_ho), (s_mi, s_mo) = slots["l0s1"], slots["l1s1"]
    s_hi, s_ho, s_mi, s_mo = s_hi[:, None], s_ho[:, None], s_mi[:, None], s_mo[:, None]

    gmod = jnp.stack([jnp.concatenate(d_mod[i], axis=0) for i in range(2)])
    d_sm = lb0 * d_lb0
    d_hg_lb = jnp.concatenate([d_sm, jnp.zeros((2, D_MODEL), F32)], axis=0) - sm * d_sm
    small = [gmod, jnp.stack([jnp.concatenate(r, axis=0) for r in d_npre]),
             jnp.stack([jnp.concatenate(r, axis=0) for r in d_npost]),
             d_on, d_hg_lb, d_bin, d_lg, d_lbias, d_ws, d_bs[:, :, 0]]
    sizes = [a.size for a in small]
    flat = jnp.concatenate([a.reshape(-1) for a in small])
    rows = -(-flat.size // (8 * 128)) * 8
    flat = jnp.pad(flat, (0, rows * 128 - flat.size)).reshape(rows, 128)
    flats = _all_gather(flat, "gather_small_grads").reshape(NDEV, rows * 128)
    parts, off = [], 0
    for a, n in zip(small, sizes):
        parts.append(flats[:, off:off + n].reshape((NDEV,) + a.shape))
        off += n
    p_mod, p_npre, p_npost, p_on, p_lb, p_bin, p_lg, p_lbias, p_ws, p_bs = parts

    def mine(p, width):
        return lax.dynamic_slice_in_dim(p, me * width, width, axis=p.ndim - 1)

    gmod_cols = mine(p_mod.reshape(NDEV, 2, 9 * D_MODEL), n_ada).transpose(1, 0, 2)
    g_ada_w = _ada_bwd(jnp.pad(c_all.T, ((0, 0), (0, 120))), jnp.pad(gmod_cols, ((0, 0), (0, 120), (0, 0))), "ada_bwd")

    out = {}
    out["ada_w"] = _update(g_ada_w[None], ada_w, m_ada_w, v_ada_w, "adamw_ada_w")
    out["ada_b"] = _update(p_mod.reshape(NDEV, 2, 9 * D_MODEL), ada_b, m_ada_b, v_ada_b, "adamw_ada_b")
    out["norm_pre"] = _update(mine(p_npre, 128), norm_pre, m_norm_pre, v_norm_pre, "adamw_norm_pre")
    out["norm_post"] = _update(mine(p_npost, 128), norm_post, m_norm_post, v_norm_post, "adamw_norm_post")
    out["ffn_w_in"] = _update(s_fi.reshape((NDEV,) + ffn_w_in.shape), ffn_w_in, m_ffn_w_in, v_ffn_w_in, "adamw_ffn_in")
    out["ffn_w_out"] = _update(s_fo.reshape((NDEV,) + ffn_w_out.shape), ffn_w_out, m_ffn_w_out, v_ffn_w_out, "adamw_ffn_out")
    out["hg_w_in"] = _update(s_hi, hg_w_in, m_hg_w_in, v_hg_w_in, "adamw_hg_in")
    out["hg_w_out"] = _update(s_ho, hg_w_out, m_hg_w_out, v_hg_w_out, "adamw_hg_out")
    out["hg_out_norm"] = _update(p_on, hg_out_norm, m_hg_out_norm, v_hg_out_norm, "adamw_hg_norm")
    out["hg_lb"] = _update(p_lb, hg_lb, m_hg_lb, v_hg_lb, "adamw_hg_lb")
    out["gm_w_in"] = _update(s_mi, gm_w_in, m_gm_w_in, v_gm_w_in, "adamw_gm_in")
    out["gm_b_in"] = _update(mine(p_bin.reshape(NDEV, 1, 8 * 768), 768), gm_b_in, m_gm_b_in, v_gm_b_in, "adamw_gm_b_in")
    out["gm_ln_g"] = _update(mine(p_lg.reshape(NDEV, 1, 4 * 768), 384), gm_ln_g, m_gm_ln_g, v_gm_ln_g, "adamw_gm_ln_g")
    out["gm_ln_b"] = _update(mine(p_lbias.reshape(NDEV, 1, 4 * 768), 384), gm_ln_b, m_gm_ln_b, v_gm_ln_b, "adamw_gm_ln_b")
    out["gm_w_s"] = _update(p_ws[:, None], gm_w_s, m_gm_w_s, v_gm_w_s, "adamw_gm_w_s")
    out["gm_b_s"] = _update(p_bs[:, None], gm_b_s, m_gm_b_s, v_gm_b_s, "adamw_gm_b_s")
    out["gm_w_out"] = _update(s_mo, gm_w_out, m_gm_w_out, v_gm_w_out, "adamw_gm_out")

    names = ["ada_w", "ada_b", "norm_pre", "norm_post", "ffn_w_in", "ffn_w_out", "hg_w_in", "hg_w_out",
             "hg_out_norm", "hg_lb", "gm_w_in", "gm_b_in", "gm_ln_g", "gm_ln_b", "gm_w_s", "gm_b_s", "gm_w_out"]
    return (loss, grad_x, *[out[n][0] for n in names], *[out[n][1] for n in names],
            *[out[n][2] for n in names], *[out[n][3] for n in names])
```

```python
import math

import jax
import jax.numpy as jnp
from jax import lax
from jax.experimental import pallas as pl
from jax.experimental.pallas import tpu as pltpu

F32 = jnp.float32
BF16 = jnp.bfloat16
NDEV = 8
D_MODEL = 1024
EPS = 1e-6
HG_CHUNK = 64
HG_HEAD = 128
HG_HEADS = 8
GM_CHUNK = 128
GM_GDIM = 384
GM_GROUPS = 8
ADAM_LR = 0.001
ADAM_B1 = 0.9
ADAM_B2 = 0.999
ADAM_EPS = 1e-08
ADAM_WD = 0.01
ADAM_STEP = 10
VMEM_LIMIT = 56 * 2 ** 20

NN = (((1,), (0,)), ((), ()))
NT = (((1,), (1,)), ((), ()))
TN = (((0,), (0,)), ((), ()))
MESH = pl.DeviceIdType.MESH
ANY = pl.BlockSpec(memory_space=pl.ANY)


def _dot(a, b, dims=NN, precision=None):
    return lax.dot_general(a, b, dims, preferred_element_type=F32, precision=precision)


def _params(*sem):
    return pltpu.CompilerParams(dimension_semantics=sem, vmem_limit_bytes=VMEM_LIMIT)


def _sigmoid(x):
    return 1.0 / (1.0 + jnp.exp(-x))


def _sigmoid_t(x):
    return 0.5 * jnp.tanh(0.5 * x) + 0.5


def _gelu_and_grad(x):
    c = math.sqrt(2.0 / math.pi)
    m = (c * 0.044715) * (x * x)
    t = jnp.tanh(x * (c + m))
    hp = 0.5 + 0.5 * t
    return x * hp, hp * (1.0 + (x * (1.0 - t)) * (c + 3.0 * m))


def _colsum(x):
    return jnp.sum(x, axis=0, keepdims=True)


def _rowmean(x):
    return jnp.mean(x, axis=-1, keepdims=True)


def _all_gather(shard, name):
    def body(x_ref, out_ref, send_sems, recv_sems, local_sem):
        x, y, c = lax.axis_index("x"), lax.axis_index("y"), lax.axis_index("c")
        me, sibling = (x, y, c), (x, y, 1 - c)
        chips = [(1 - x, y), (x, 1 - y), (1 - x, 1 - y)]

        def slot(p):
            return out_ref.at[4 * p[0] + 2 * p[1] + p[2]]

        def copy(k, block, to, src=None):
            return pltpu.make_async_remote_copy(
                src_ref=slot(block) if src is None else src, dst_ref=slot(block),
                send_sem=send_sems.at[k], recv_sem=recv_sems.at[k],
                device_id=to, device_id_type=MESH)

        mine = pltpu.make_async_copy(x_ref, slot(me), local_sem)
        mine.start()
        first = [copy(0, me, sibling, src=x_ref)]
        first += [copy(1 + j, me, (*chip, c), src=x_ref) for j, chip in enumerate(chips)]
        for cp in first:
            cp.start()
        passed = [copy(4 + j, (*chip, c), sibling) for j, chip in enumerate(chips)]
        for j, chip in enumerate(chips):
            copy(1 + j, (*chip, c), me).wait_recv()
            passed[j].start()
        copy(0, sibling, me).wait_recv()
        for j, chip in enumerate(chips):
            copy(4 + j, (*chip, 1 - c), me).wait_recv()
        for cp in first + passed:
            cp.wait_send()
        mine.wait()

    return pl.pallas_call(
        body, name=name,
        out_shape=jax.ShapeDtypeStruct((NDEV,) + shard.shape, shard.dtype),
        in_specs=[ANY], out_specs=ANY,
        scratch_shapes=[pltpu.SemaphoreType.DMA((7,)), pltpu.SemaphoreType.DMA((7,)),
                        pltpu.SemaphoreType.DMA(())],
    )(shard)


class _Exchange:
    def __init__(self, kind, arrays):
        self.gather = kind == "gather"
        self.arrays = list(arrays)
        self.n = n = len(self.arrays)
        self.out_shape = [jax.ShapeDtypeStruct(((NDEV,) + a.shape) if self.gather else a.shape, a.dtype)
                          for a in self.arrays]
        self.scratch = [pltpu.SemaphoreType.DMA((n, NDEV - 1)), pltpu.SemaphoreType.DMA((n, NDEV - 1)),
                        pltpu.SemaphoreType.DMA((n,))]

    def _copies(self, in_refs, out_refs, sems):
        send_sems, recv_sems, local_sems = sems
        x, y, c = lax.axis_index("x"), lax.axis_index("y"), lax.axis_index("c")
        me = 4 * x + 2 * y + c
        peers = [(1 - x if k & 4 else x, 1 - y if k & 2 else y, 1 - c if k & 1 else c) for k in range(1, NDEV)]
        local, send, recv = [], [], []
        for a in range(self.n):
            src = (lambda pid, a=a: in_refs[a]) if self.gather else (lambda pid, a=a: in_refs[a].at[pid])
            local.append(pltpu.make_async_copy(src(me), out_refs[a].at[me], local_sems.at[a]))
            for k, p in enumerate(peers):
                pid = 4 * p[0] + 2 * p[1] + p[2]
                for lst, slot in ((send, me), (recv, pid)):
                    lst.append(pltpu.make_async_remote_copy(
                        src_ref=src(pid), dst_ref=out_refs[a].at[slot],
                        send_sem=send_sems.at[a, k], recv_sem=recv_sems.at[a, k],
                        device_id=p, device_id_type=MESH))
        return local, send, recv

    def start(self, first, in_refs, out_refs, sems):
        @pl.when(first)
        def _():
            local, send, _ = self._copies(in_refs, out_refs, sems)
            for cp in local + send:
                cp.start()

    def finish(self, last, in_refs, out_refs, sems):
        @pl.when(last)
        def _():
            local, send, recv = self._copies(in_refs, out_refs, sems)
            for cp in send:
                cp.wait_send()
            for cp in recv:
                cp.wait_recv()
            for cp in local:
                cp.wait()


def _host(exchange, n_in, n_out, body, first_last):
    if exchange is None:
        return body, [], [], [], []
    n = exchange.n

    def hosted(*refs):
        ins, refs = refs[:n_in], refs[n_in:]
        xin, refs = refs[:n], refs[n:]
        outs, refs = refs[:n_out], refs[n_out:]
        xout, refs = refs[:n], refs[n:]
        scratch, sems = refs[:len(refs) - 3], refs[len(refs) - 3:]
        first, last = first_last()
        exchange.start(first, xin, xout, sems)
        body(*ins, *outs, *scratch)
        exchange.finish(last, xin, xout, sems)

    return hosted, exchange.arrays, [ANY] * n, exchange.out_shape, exchange.scratch


def _first_last(steps):
    def at():
        i = pl.program_id(0)
        return i == 0, i == steps - 1
    return at


class _ColBlocks:
    def __init__(self, ref, width):
        self.ref, self.width = ref, width

    def _index(self, key):
        key = key if isinstance(key, tuple) else (key,)
        rows = key[1] if len(key) > 1 else slice(None)
        cols = key[2] if len(key) > 2 else slice(0, self.width)
        c0 = key[0] * self.width
        return rows, slice(c0 + cols.start, c0 + cols.stop)

    def __getitem__(self, key):
        return self.ref[self._index(key)]

    def __setitem__(self, key, value):
        self.ref[self._index(key)] = value


def _col_chunks(width, chunk=768):
    return [slice(c, min(c + chunk, width)) for c in range(0, width, chunk)]


def _row_spec(tm, d):
    return pl.BlockSpec((tm, d), lambda m: (m, 0))


def _vec_spec(d):
    return pl.BlockSpec((1, d), lambda m: (0, 0))


def _whole_spec(w):
    nd = w.ndim
    return pl.BlockSpec(w.shape, lambda m: (0,) * nd, pipeline_mode=pl.Buffered(1))


def _mm_blocks(a, w, name, *, bias=None, out_dtype=BF16, tm=512, chunk=768, gelu=False, f32_cols=None,
               exchange=None):
    T, K = a.shape
    C = w.shape[0]
    tm = min(tm, T)
    n_in = 2 + (bias is not None)

    def body(*refs):
        a_ref, w_ref = refs[:2]
        av = a_ref[...]
        for cols in _col_chunks(C, chunk):
            r = _dot(av, w_ref[cols], NT)
            if bias is not None:
                r = r + refs[2][:, cols]
            if gelu:
                z, dz = _gelu_and_grad(r)
                refs[n_in][:, cols] = z.astype(BF16)
                refs[n_in + 1][:, cols] = dz.astype(BF16)
            elif f32_cols is None:
                refs[n_in][:, cols] = r.astype(out_dtype)
            elif f32_cols[0] <= cols.start < f32_cols[1]:
                refs[n_in + 1][:, cols.start - f32_cols[0]:cols.stop - f32_cols[0]] = r
            else:
                shift = 0 if cols.start < f32_cols[0] else f32_cols[1] - f32_cols[0]
                refs[n_in][:, cols.start - shift:cols.stop - shift] = r.astype(BF16)

    in_specs = [_row_spec(tm, K), _whole_spec(w)]
    args = [a, w]
    if bias is not None:
        in_specs.append(_whole_spec(bias))
        args.append(bias)
    if gelu:
        outs = [jax.ShapeDtypeStruct((T, C), BF16)] * 2
    elif f32_cols is not None:
        wide = f32_cols[1] - f32_cols[0]
        assert f32_cols[0] % chunk == 0 and wide % chunk == 0
        outs = [jax.ShapeDtypeStruct((T, C - wide), BF16), jax.ShapeDtypeStruct((T, wide), F32)]
    else:
        outs = [jax.ShapeDtypeStruct((T, C), out_dtype)]
    body, x_args, x_in, x_out, x_scratch = _host(exchange, n_in, len(outs), body, _first_last(T // tm))
    res = pl.pallas_call(
        body, name=name, grid=(T // tm,),
        out_shape=outs + x_out,
        in_specs=in_specs + x_in,
        out_specs=[_row_spec(tm, o.shape[1]) for o in outs] + x_in,
        scratch_shapes=x_scratch,
        compiler_params=_params("arbitrary" if exchange else "parallel"),
    )(*args, *x_args)
    return res if len(res) > 1 else res[0]


def _rms(v):
    return lax.rsqrt(_rowmean(v * v) + EPS)


def _zero_at_start(*refs):
    @pl.when(pl.program_id(0) == 0)
    def _():
        for r in refs:
            r[...] = jnp.zeros_like(r)


def _postnorm_bwd_math(dxo, yv, g, gate, res_w, dgate_ref, dpost_ref):
    r = _rms(yv)
    yh = yv * r
    both = res_w * _colsum(dxo * yh)
    dgate_ref[...] += g * both
    dpost_ref[...] += gate * both
    dyh = dxo * (res_w * gate * g)
    return (r * (dyh - yh * _rowmean(dyh * yh))).astype(BF16)


def _out_proj(a, w, x, post_g, gate, res_w, nxt, name, *, tm=512):
    T, ka = a.shape
    d = w.shape[1]
    tm = min(tm, T)

    def body(a_ref, w_ref, x_ref, pg_ref, gate_ref, ng_ref, nsc_ref, nsh_ref, y_ref, xn_ref, h_ref):
        y = _dot(a_ref[...], w_ref[...])
        y_ref[...] = y
        xn = x_ref[...] + (y * _rms(y)) * (res_w * gate_ref[...] * pg_ref[...])
        xn_ref[...] = xn
        h_ref[...] = ((xn * _rms(xn)) * (ng_ref[...] * (1.0 + nsc_ref[...])) + nsh_ref[...]).astype(BF16)

    return pl.pallas_call(
        body, name=name, grid=(T // tm,),
        out_shape=(jax.ShapeDtypeStruct((T, d), F32), jax.ShapeDtypeStruct((T, d), F32),
                   jax.ShapeDtypeStruct((T, d), BF16)),
        in_specs=[_row_spec(tm, ka), _whole_spec(w), _row_spec(tm, d)] + [_vec_spec(d)] * 5,
        out_specs=(_row_spec(tm, d),) * 3,
        compiler_params=_params("parallel"),
    )(a, w, x, post_g, gate, *nxt)


def _out_proj_last(a, w, x, post_g, gate, res_w, target, name, *, tm=512):
    T, ka = a.shape
    d = w.shape[1]
    tm = min(tm, T)

    def body(a_ref, w_ref, x_ref, pg_ref, gate_ref, t_ref, dx_ref, dy_ref, dgate_ref, dpost_ref, l_ref):
        _zero_at_start(dgate_ref, dpost_ref, l_ref)
        y = _dot(a_ref[...], w_ref[...])
        e = x_ref[...] + res_w * gate_ref[...] * (y * _rms(y) * pg_ref[...]) - t_ref[...]
        l_ref[...] += 0.5 * jnp.sum(_rowmean(e * e), axis=0, keepdims=True)
        dx = e * (1.0 / d)
        dx_ref[...] = dx
        dy_ref[...] = _postnorm_bwd_math(dx, y, pg_ref[...], gate_ref[...], res_w, dgate_ref, dpost_ref)

    return pl.pallas_call(
        body, name=name, grid=(T // tm,),
        out_shape=(jax.ShapeDtypeStruct((T, d), F32), jax.ShapeDtypeStruct((T, d), BF16),
                   jax.ShapeDtypeStruct((1, d), F32), jax.ShapeDtypeStruct((1, d), F32),
                   jax.ShapeDtypeStruct((1, 128), F32)),
        in_specs=[_row_spec(tm, ka), _whole_spec(w), _row_spec(tm, d), _vec_spec(d), _vec_spec(d), _row_spec(tm, d)],
        out_specs=(_row_spec(tm, d), _row_spec(tm, d), _vec_spec(d), _vec_spec(d),
                   pl.BlockSpec((1, 128), lambda m: (0, 0))),
        compiler_params=_params("arbitrary"),
    )(a, w, x, post_g, gate, target)


def _in_grad(dz, w, dxo, x, pre_g, scale, prev, name, *, tm=512, exchange=None):
    T, C = dz.shape
    d = w.shape[1]
    tm = min(tm, T)
    has_prev = prev is not None
    res_w = prev[3] if has_prev else None

    def body(*refs):
        dz_ref, w_ref, dxo_ref, x_ref, g_ref, sc_ref = refs[:6]
        if has_prev:
            yp_ref, ppg_ref, pgate_ref, dx_ref, dsh_ref, dsc_ref, dg_ref, dyp_ref, dgate_ref, dpost_ref = refs[6:]
            _zero_at_start(dsh_ref, dsc_ref, dg_ref, dgate_ref, dpost_ref)
        else:
            dx_ref, dsh_ref, dsc_ref, dg_ref = refs[6:]
            _zero_at_start(dsh_ref, dsc_ref, dg_ref)
        dh = _dot(dz_ref[...], w_ref[...])
        xv = x_ref[...]
        r = _rms(xv)
        xh = xv * r
        gain = 1.0 + sc_ref[...]
        both = _colsum(dh * xh)
        dsh_ref[...] += _colsum(dh)
        dsc_ref[...] += g_ref[...] * both
        dg_ref[...] += gain * both
        dxh = dh * (gain * g_ref[...])
        dx = dxo_ref[...] + r * (dxh - xh * _rowmean(dxh * xh))
        dx_ref[...] = dx
        if has_prev:
            dyp_ref[...] = _postnorm_bwd_math(dx, yp_ref[...], ppg_ref[...], pgate_ref[...], res_w,
                                               dgate_ref, dpost_ref)

    vec = jax.ShapeDtypeStruct((1, d), F32)
    in_specs = [_row_spec(tm, C), _whole_spec(w), _row_spec(tm, d), _row_spec(tm, d), _vec_spec(d), _vec_spec(d)]
    out_shape = [jax.ShapeDtypeStruct((T, d), F32), vec, vec, vec]
    out_specs = [_row_spec(tm, d), _vec_spec(d), _vec_spec(d), _vec_spec(d)]
    args = [dz, w, dxo, x, pre_g, scale]
    if has_prev:
        in_specs += [_row_spec(tm, d), _vec_spec(d), _vec_spec(d)]
        out_shape += [jax.ShapeDtypeStruct((T, d), BF16), vec, vec]
        out_specs += [_row_spec(tm, d), _vec_spec(d), _vec_spec(d)]
        args += list(prev[:3])
    body, x_args, x_in, x_out, x_scratch = _host(exchange, len(args), len(out_shape), body, _first_last(T // tm))
    return pl.pallas_call(
        body, name=name, grid=(T // tm,),
        out_shape=out_shape + x_out, in_specs=in_specs + x_in, out_specs=out_specs + x_in,
        scratch_shapes=x_scratch,
        compiler_params=_params("arbitrary"),
    )(*args, *x_args)


def _mm_wgrad(x, y, name, *, xw=None, yw=None, split=1, tt=2048):
    T, P = x.shape
    Q = y.shape[1]
    xw, yw = xw or P, yw or Q
    jx, jy = P // xw, Q // yw
    assert jx == 1 or jy == 1
    tt = min(tt, T)
    nt = T // tt
    part = yw // split

    def body(x_ref, y_ref, o_ref, acc_ref):
        t = pl.program_id(1)

        @pl.when(t == 0)
        def _():
            acc_ref[...] = jnp.zeros_like(acc_ref)

        acc_ref[...] += _dot(x_ref[...], y_ref[...], TN)

        @pl.when(t == nt - 1)
        def _():
            if jy > 1:
                for k in range(split):
                    o_ref[k] = acc_ref[:, part * k:part * (k + 1)].astype(BF16)
            else:
                o_ref[...] = acc_ref[...].astype(BF16)

    if jy > 1:
        out_shape = jax.ShapeDtypeStruct((jy * split, P, part), BF16)
        out_spec = pl.BlockSpec((split, P, part), lambda j, t: (j, 0, 0))
    else:
        out_shape = jax.ShapeDtypeStruct((P, Q), BF16)
        out_spec = pl.BlockSpec((xw, Q), lambda j, t: (j, 0))
    return pl.pallas_call(
        body, name=name, grid=(max(jx, jy), nt),
        out_shape=out_shape,
        in_specs=[pl.BlockSpec((tt, xw), (lambda j, t: (t, j)) if jx > 1 else (lambda j, t: (t, 0))),
                  pl.BlockSpec((tt, yw), (lambda j, t: (t, j)) if jy > 1 else (lambda j, t: (t, 0)))],
        out_specs=out_spec,
        scratch_shapes=[pltpu.VMEM((xw, yw), F32)],
        compiler_params=_params("parallel", "arbitrary"),
    )(x, y)


def _ffn_in(h, wt, name, *, tm=512, opening=None, exchange=None):
    T, K = h.shape
    F = wt.shape[0] // 2
    tm = min(tm, T)
    n_vec = 3 if opening else 0

    def body(*refs):
        h_ref, w_ref = refs[0], refs[1 + n_vec]
        pq_ref, g_ref = refs[2 + n_vec], refs[3 + n_vec]
        hh = h_ref[...]
        if opening:
            g_vec, sc_vec, sh_vec = refs[1:4]
            hh = (hh * _rms(hh) * (g_vec[...] * (1.0 + sc_vec[...])) + sh_vec[...]).astype(BF16)
            refs[4 + n_vec][...] = hh
        for cols in _col_chunks(F):
            hi = slice(F + cols.start, F + cols.stop)
            a = _dot(hh, w_ref[cols], NT)
            b = _dot(hh, w_ref[hi], NT)
            s = _sigmoid_t(a)
            silu = a * s
            pq_ref[:, cols] = (b * (s * (1.0 + a * (1.0 - s)))).astype(BF16)
            pq_ref[:, hi] = silu.astype(BF16)
            g_ref[:, cols] = (silu * b).astype(BF16)

    outs = [jax.ShapeDtypeStruct((T, 2 * F), BF16), jax.ShapeDtypeStruct((T, F), BF16)]
    out_specs = [_row_spec(tm, 2 * F), _row_spec(tm, F)]
    if opening:
        outs.append(jax.ShapeDtypeStruct((T, K), BF16))
        out_specs.append(_row_spec(tm, K))
    vecs = list(opening) if opening else []
    body, x_args, x_in, x_out, x_scratch = _host(exchange, 2 + n_vec, len(outs), body, _first_last(T // tm))
    res = pl.pallas_call(
        body, name=name, grid=(T // tm,),
        out_shape=outs + x_out,
        in_specs=[_row_spec(tm, K)] + [_vec_spec(K)] * n_vec + [_whole_spec(wt)] + x_in,
        out_specs=out_specs + x_in,
        scratch_shapes=x_scratch,
        compiler_params=_params("arbitrary" if exchange else "parallel"),
    )(h, *vecs, wt, *x_args)
    return res[:len(outs)], res[len(outs):]


def _ffn_dgate(dy, w_out, pq, name, *, tm=512):
    T, N = dy.shape
    F = w_out.shape[0]
    tm = min(tm, T)

    def body(dy_ref, w_ref, pq_ref, dz_ref):
        dyv = dy_ref[...]
        for cols in _col_chunks(F):
            hi = slice(F + cols.start, F + cols.stop)
            dg = _dot(dyv, w_ref[cols], NT)
            dz_ref[:, cols] = (dg * pq_ref[:, cols].astype(F32)).astype(BF16)
            dz_ref[:, hi] = (dg * pq_ref[:, hi].astype(F32)).astype(BF16)

    return pl.pallas_call(
        body, name=name, grid=(T // tm,),
        out_shape=jax.ShapeDtypeStruct((T, 2 * F), BF16),
        in_specs=[_row_spec(tm, N), _whole_spec(w_out), _row_spec(tm, 2 * F)],
        out_specs=_row_spec(tm, 2 * F),
        compiler_params=_params("parallel"),
    )(dy, w_out, pq)


def _ada_fwd(c_all, w, b, name):
    L, K, n = w.shape

    def body(c_ref, w_ref, b_ref, o_ref):
        cv = c_ref[...]
        cond = cv * _sigmoid(cv)
        for l in range(L):
            o_ref[l] = _dot(cond, w_ref[l], precision=lax.Precision.HIGHEST) + b_ref[l]

    return pl.pallas_call(
        body, name=name,
        out_shape=jax.ShapeDtypeStruct((L, NDEV, n), F32),
        compiler_params=pltpu.CompilerParams(vmem_limit_bytes=VMEM_LIMIT),
    )(c_all, w, b)


def _ada_bwd(c_all_t, gmod, name):
    L, _, n = gmod.shape
    K = c_all_t.shape[0]

    def body(c_ref, g_ref, o_ref):
        cv = c_ref[...]
        cond = cv * _sigmoid(cv)
        for l in range(L):
            o_ref[l] = _dot(cond, g_ref[l], precision=lax.Precision.HIGHEST)

    return pl.pallas_call(
        body, name=name,
        out_shape=jax.ShapeDtypeStruct((L, K, n), F32),
        compiler_params=pltpu.CompilerParams(vmem_limit_bytes=VMEM_LIMIT),
    )(c_all_t, gmod)


def _tri(n, upper=False, block=None):
    r = lax.broadcasted_iota(jnp.int32, (n, n), 0)
    c = lax.broadcasted_iota(jnp.int32, (n, n), 1)
    m = (c >= r) if upper else (c <= r)
    if block is not None:
        m = m & ((r // block) == (c // block))
    return m.astype(BF16)


TRI_ROWS = 128


def _tri_dot(tri, x):
    hi = x.astype(BF16)
    lo = (x - hi.astype(F32)).astype(BF16)
    rows = x.shape[0]
    step = min(TRI_ROWS, rows)
    parts = [_dot(tri, hi[r:r + step]) + _dot(tri, lo[r:r + step]) for r in range(0, rows, step)]
    return parts[0] if len(parts) == 1 else jnp.concatenate(parts, axis=0)


def _hgrn_gates(proj_ref, f_ref, lb_ref, jh):
    lb = lb_ref[:, 512 * jh:512 * (jh + 1)]
    qp = _ColBlocks(proj_ref, 512)[jh].astype(F32)
    fx = _ColBlocks(f_ref, 512)[jh]
    sq = _sigmoid_t(qp)
    sig = _sigmoid_t(fx)
    f = lb + (1.0 - lb) * sig
    k = (1.0 - lb) * (1.0 - sig)
    return lb, qp, sq, sig, f, k


def _hgrn_fwd(proj, lb, out_norm, name, *, tb=512, exchange=None):
    T = proj[0].shape[0]
    tb = min(tb, T)
    nc = tb // HG_CHUNK
    lmat = _tri(min(TRI_ROWS, tb), block=HG_CHUNK)

    def body(proj_ref, f_ref, lb_ref, on_ref, l_ref, o_ref, og_ref, st_ref, s_scr, b_scr):
        @pl.when(pl.program_id(0) == 0)
        def _():
            s_scr[...] = jnp.zeros_like(s_scr)

        r_i = lax.broadcasted_iota(jnp.int32, (HG_CHUNK, HG_CHUNK), 0)
        c_i = lax.broadcasted_iota(jnp.int32, (HG_CHUNK, HG_CHUNK), 1)
        causal = c_i <= r_i
        onv = on_ref[...]
        blocks = _ColBlocks(proj_ref, 512)
        for jh in range(2):
            lbv, qp, sq, sig, f, k = _hgrn_gates(proj_ref, f_ref, lb_ref, jh)
            q = qp * sq
            b_half = b_scr.at[jh]
            b_half[...] = _tri_dot(l_ref[...], jnp.log(f))
            v = blocks[2 + jh]
            gp = blocks[4 + jh].astype(F32)
            gs = gp * _sigmoid_t(gp)
            for hh in range(4):
                hd = 4 * jh + hh
                cs = slice(HG_HEAD * hh, HG_HEAD * (hh + 1))
                for ci in range(nc):
                    r0 = HG_CHUNK * ci
                    rs = slice(r0, r0 + HG_CHUNK)
                    bc = b_half[rs, cs]
                    bm = b_half[r0 + HG_CHUNK // 2 - 1:r0 + HG_CHUNK // 2, cs]
                    bl = b_half[r0 + HG_CHUNK - 1:r0 + HG_CHUNK, cs]
                    qc, kc, vc = q[rs, cs], k[rs, cs], v[rs, cs].astype(BF16)
                    e_q, e_k = jnp.exp(bc - bm), jnp.exp(bm - bc)
                    qe = (qc * (e_q * jnp.exp(bm))).astype(BF16)
                    qt = (qc * e_q).astype(BF16)
                    kt = (kc * e_k).astype(BF16)
                    kd = (kc * (e_k * jnp.exp(bl - bm))).astype(BF16)
                    st = s_scr[hd]
                    stb = st.astype(BF16)
                    st_ref[ci, hd] = stb
                    a = jnp.where(causal, _dot(qt, kt, NT), 0.0).astype(BF16)
                    o = _dot(qe, stb, NT) + _dot(a, vc)
                    s_scr[hd] = st * jnp.exp(bl) + _dot(vc, kd, TN)
                    o_ref[rs, HG_HEAD * hd:HG_HEAD * (hd + 1)] = o
                    r = lax.rsqrt(_rowmean(o * o) + EPS)
                    og_ref[rs, HG_HEAD * hd:HG_HEAD * (hd + 1)] = (o * r * onv * gs[rs, cs]).astype(BF16)

    body, x_args, x_in, x_out, x_scratch = _host(exchange, 5, 3, body, _first_last(T // tb))
    res = pl.pallas_call(
        body, name=name, grid=(T // tb,),
        out_shape=[jax.ShapeDtypeStruct((T, D_MODEL), F32), jax.ShapeDtypeStruct((T, D_MODEL), BF16),
                   jax.ShapeDtypeStruct((T // HG_CHUNK, HG_HEADS, HG_HEAD, HG_HEAD), BF16)] + x_out,
        in_specs=[pl.BlockSpec((tb, 3 * D_MODEL), lambda i: (i, 0)),
                  pl.BlockSpec((tb, D_MODEL), lambda i: (i, 0)),
                  pl.BlockSpec((1, D_MODEL), lambda i: (0, 0)),
                  pl.BlockSpec((1, HG_HEAD), lambda i: (0, 0)),
                  pl.BlockSpec(lmat.shape, lambda i: (0, 0))] + x_in,
        out_specs=[pl.BlockSpec((tb, D_MODEL), lambda i: (i, 0)),
                   pl.BlockSpec((tb, D_MODEL), lambda i: (i, 0)),
                   pl.BlockSpec((nc, HG_HEADS, HG_HEAD, HG_HEAD), lambda i: (i, 0, 0, 0))] + x_in,
        scratch_shapes=[pltpu.VMEM((HG_HEADS, HG_HEAD, HG_HEAD), F32), pltpu.VMEM((2, tb, 512), F32)] + x_scratch,
        compiler_params=_params("arbitrary"),
    )(*proj, lb, out_norm, lmat, *x_args)
    return res[0], res[1], res[2], res[3:]


def _hgrn_bwd(proj, o, dog, states, lb, out_norm, name, *, tb=512):
    T = proj[0].shape[0]
    tb = min(tb, T)
    nc = tb // HG_CHUNK
    nb = T // tb
    lmat = _tri(min(TRI_ROWS, tb), block=HG_CHUNK)
    umat = _tri(min(TRI_ROWS, tb), upper=True, block=HG_CHUNK)

    def body(proj_ref, f_ref, o_ref, dog_ref, st_ref, lb_ref, on_ref, l_ref, u_ref,
             dproj_ref, dlb_ref, don_ref, ds_scr, *half_scr):
        @pl.when(pl.program_id(0) == 0)
        def _():
            ds_scr[...] = jnp.zeros_like(ds_scr)
            dlb_ref[...] = jnp.zeros_like(dlb_ref)
            don_ref[...] = jnp.zeros_like(don_ref)

        r_i = lax.broadcasted_iota(jnp.int32, (HG_CHUNK, HG_CHUNK), 0)
        c_i = lax.broadcasted_iota(jnp.int32, (HG_CHUNK, HG_CHUNK), 1)
        causal = c_i <= r_i
        causal_t = r_i <= c_i
        last_row = lax.broadcasted_iota(jnp.int32, (HG_CHUNK, HG_HEAD), 0) == HG_CHUNK - 1
        onv = on_ref[...]
        don_acc = jnp.zeros((1, HG_HEAD), F32)
        blocks = _ColBlocks(proj_ref, 512)
        dproj_ref = _ColBlocks(dproj_ref, 512)
        for jh in range(2):
            b_scr, dq_scr, dk_scr, dv_scr, dg_scr, db_scr = [s.at[jh] for s in half_scr]
            lbv, qp, sq, sig, f, k = _hgrn_gates(proj_ref, f_ref, lb_ref, jh)
            q = qp * sq
            b_scr[...] = _tri_dot(l_ref[...], jnp.log(f))
            v = blocks[2 + jh]
            gp = blocks[4 + jh].astype(F32)
            sg = _sigmoid_t(gp)
            for ci in reversed(range(nc)):
                r0 = HG_CHUNK * ci
                rs = slice(r0, r0 + HG_CHUNK)
                for hh in range(4):
                    hd = 4 * jh + hh
                    cs = slice(HG_HEAD * hh, HG_HEAD * (hh + 1))
                    hs = slice(HG_HEAD * hd, HG_HEAD * (hd + 1))
                    oc = o_ref[rs, hs]
                    r = lax.rsqrt(_rowmean(oc * oc) + EPS)
                    oh = oc * r
                    gc, sgc = gp[rs, cs], sg[rs, cs]
                    dogc = dog_ref[rs, hs].astype(F32)
                    don = dogc * (gc * sgc)
                    dg_scr[rs, cs] = dogc * (oh * onv) * (sgc * (1.0 + gc * (1.0 - sgc)))
                    don_acc += _colsum(don * oh)
                    donh = don * onv
                    do = (r * (donh - oh * _rowmean(donh * oh))).astype(BF16)
                    bc = b_scr[rs, cs]
                    bm = b_scr[r0 + HG_CHUNK // 2 - 1:r0 + HG_CHUNK // 2, cs]
                    bl = b_scr[r0 + HG_CHUNK - 1:r0 + HG_CHUNK, cs]
                    qc, kc, vc = q[rs, cs], k[rs, cs], v[rs, cs].astype(BF16)
                    e_q, e_k = jnp.exp(bc - bm), jnp.exp(bm - bc)
                    e_b, e_d = e_q * jnp.exp(bm), e_k * jnp.exp(bl - bm)
                    qe = (qc * e_b).astype(BF16)
                    qt = (qc * e_q).astype(BF16)
                    kt = (kc * e_k).astype(BF16)
                    kd = (kc * e_d).astype(BF16)
                    stb = st_ref[ci, hd]
                    dst = ds_scr[hd]
                    dstb = dst.astype(BF16)
                    a_t = jnp.where(causal_t, _dot(kt, qt, NT), 0.0).astype(BF16)
                    da = jnp.where(causal, _dot(do, vc, NT), 0.0).astype(BF16)
                    da_t = jnp.where(causal_t, _dot(vc, do, NT), 0.0).astype(BF16)
                    dv_scr[rs, cs] = _dot(a_t, do) + _dot(kd, dstb, NT)
                    dqe, dqt = _dot(do, stb), _dot(da, kt)
                    dkt, dkd = _dot(da_t, qt), _dot(vc, dstb)
                    dq_scr[rs, cs] = dqe * e_b + dqt * e_q
                    dk_scr[rs, cs] = dkt * e_k + dkd * e_d
                    e_l = jnp.exp(bl)
                    kd_dkd = kd.astype(F32) * dkd
                    dbc = qe.astype(F32) * dqe + qt.astype(F32) * dqt - kt.astype(F32) * dkt - kd_dkd
                    handed = e_l * _colsum(dstb.astype(F32) * stb.astype(F32)) + _colsum(kd_dkd)
                    db_scr[rs, cs] = dbc + jnp.where(last_row, handed, 0.0)
                    ds_scr[hd] = dst * e_l + _dot(do, qe, TN)
            dq = dq_scr[...]
            dk = dk_scr[...]
            cols = slice(512 * jh, 512 * (jh + 1))
            dlogf = _tri_dot(u_ref[...], db_scr[...])
            one_m_sig = 1.0 - sig
            dsig = (1.0 - lbv) * sig * one_m_sig
            dboth = dlogf / f - dk
            dproj_ref[jh] = (dq * (sq * (1.0 + qp * (1.0 - sq)))).astype(BF16)
            dproj_ref[2 + jh] = (dboth * dsig).astype(BF16)
            dproj_ref[4 + jh] = dv_scr[...].astype(BF16)
            dproj_ref[6 + jh] = dg_scr[...].astype(BF16)
            dlb_ref[:, cols] += _colsum(dboth * one_m_sig)
        don_ref[...] += don_acc

    rev = lambda i: nb - 1 - i
    return pl.pallas_call(
        body, name=name, grid=(nb,),
        out_shape=(jax.ShapeDtypeStruct((T, 4 * D_MODEL), BF16), jax.ShapeDtypeStruct((1, D_MODEL), F32),
                   jax.ShapeDtypeStruct((1, HG_HEAD), F32)),
        in_specs=[pl.BlockSpec((tb, 3 * D_MODEL), lambda i: (rev(i), 0)),
                  pl.BlockSpec((tb, D_MODEL), lambda i: (rev(i), 0)),
                  pl.BlockSpec((tb, D_MODEL), lambda i: (rev(i), 0)),
                  pl.BlockSpec((tb, D_MODEL), lambda i: (rev(i), 0)),
                  pl.BlockSpec((nc, HG_HEADS, HG_HEAD, HG_HEAD), lambda i: (rev(i), 0, 0, 0)),
                  pl.BlockSpec((1, D_MODEL), lambda i: (0, 0)),
                  pl.BlockSpec((1, HG_HEAD), lambda i: (0, 0)),
                  pl.BlockSpec(lmat.shape, lambda i: (0, 0)),
                  pl.BlockSpec(lmat.shape, lambda i: (0, 0))],
        out_specs=(pl.BlockSpec((tb, 4 * D_MODEL), lambda i: (rev(i), 0)),
                   pl.BlockSpec((1, D_MODEL), lambda i: (0, 0)),
                   pl.BlockSpec((1, HG_HEAD), lambda i: (0, 0))),
        scratch_shapes=[pltpu.VMEM((HG_HEADS, HG_HEAD, HG_HEAD), F32)] + [pltpu.VMEM((2, tb, 512), F32)] * 6,
        compiler_params=_params("arbitrary"),
    )(*proj, o, dog, states, lb, out_norm, lmat, umat)


def _gm_norm(pre_ref, lg_ref, lbias_ref):
    pre_ref = _ColBlocks(pre_ref, 768)
    vs = [pre_ref[4 + j].astype(F32) for j in range(4)]
    width = 4 * vs[0].shape[1]
    mu = sum(jnp.sum(v, axis=1, keepdims=True) for v in vs) / width
    ds = [v - mu for v in vs]
    var = sum(jnp.sum(d * d, axis=1, keepdims=True) for d in ds) / width
    rstd = lax.rsqrt(var + EPS)
    vhat = [d * rstd for d in ds]
    vn = [vhat[j] * lg_ref[j:j + 1, :] + lbias_ref[j:j + 1, :] for j in range(4)]
    return vhat, vn, rstd


def _gm_spatial_fwd(pre, ln_g, ln_b, ws, bsb, name, *, tb=512):
    T = pre.shape[0]
    tb = min(tb, T)
    nc = tb // GM_CHUNK

    def body(pre_ref, lg_ref, lbias_ref, ws_ref, bs_ref, o_ref):
        _, vn, _ = _gm_norm(pre_ref, lg_ref, lbias_ref)
        pre_ref, o_ref = _ColBlocks(pre_ref, 768), _ColBlocks(o_ref, 768)
        for j in range(4):
            u = pre_ref[j].astype(F32)
            for e in range(2):
                g = 2 * j + e
                cs = slice(GM_GDIM * e, GM_GDIM * (e + 1))
                wg = ws_ref[g].astype(BF16)
                for ci in range(nc):
                    rs = slice(GM_CHUNK * ci, GM_CHUNK * (ci + 1))
                    vm = _dot(wg, vn[j][rs, cs].astype(BF16)) + bs_ref[g]
                    o_ref[j, rs, cs] = (u[rs, cs] * vm).astype(BF16)

    return pl.pallas_call(
        body, name=name, grid=(T // tb,),
        out_shape=jax.ShapeDtypeStruct((T, 4 * 768), BF16),
        in_specs=[pl.BlockSpec((tb, 8 * 768), lambda i: (i, 0)),
                  pl.BlockSpec((4, 768), lambda i: (0, 0)),
                  pl.BlockSpec((4, 768), lambda i: (0, 0)),
                  pl.BlockSpec((GM_GROUPS, GM_CHUNK, GM_CHUNK), lambda i: (0, 0, 0)),
                  pl.BlockSpec((GM_GROUPS, GM_CHUNK, GM_GDIM), lambda i: (0, 0, 0))],
        out_specs=pl.BlockSpec((tb, 4 * 768), lambda i: (i, 0)),
        compiler_params=_params("parallel"),
    )(pre, ln_g, ln_b, ws, bsb)


def _gm_spatial_bwd(pre, gp, dm, ln_g, ln_b, ws, ws_t, bsb, name, *, tb=256):
    T = pre.shape[0]
    tb = min(tb, T)
    nc = tb // GM_CHUNK
    nb = T // tb

    def body(pre_ref, gp_ref, dm_ref, lg_ref, lbias_ref, ws_ref, wst_ref, bs_ref,
             dpre_ref, dws_ref, dbs_ref, dlg_ref, dlb_ref, dbin_ref, dbs_scr, dvn_scr, du_scr):
        i = pl.program_id(0)

        @pl.when(i == 0)
        def _():
            dws_ref[...] = jnp.zeros_like(dws_ref)
            dbs_scr[...] = jnp.zeros_like(dbs_scr)
            dlg_ref[...] = jnp.zeros_like(dlg_ref)
            dlb_ref[...] = jnp.zeros_like(dlb_ref)
            dbin_ref[...] = jnp.zeros_like(dbin_ref)

        vhat, vn, rstd = _gm_norm(pre_ref, lg_ref, lbias_ref)
        pre_ref, gp_ref, dm_ref = _ColBlocks(pre_ref, 768), _ColBlocks(gp_ref, 768), _ColBlocks(dm_ref, 768)
        dpre_ref = _ColBlocks(dpre_ref, 768)
        for j in range(4):
            u = pre_ref[j].astype(F32)
            for e in range(2):
                g = 2 * j + e
                cs = slice(GM_GDIM * e, GM_GDIM * (e + 1))
                wg = ws_ref[g].astype(BF16)
                wgt = wst_ref[g].astype(BF16)
                for ci in range(nc):
                    rs = slice(GM_CHUNK * ci, GM_CHUNK * (ci + 1))
                    vnb = vn[j][rs, cs].astype(BF16)
                    vm = _dot(wg, vnb) + bs_ref[g]
                    dmg = dm_ref[j, rs, cs].astype(F32)
                    du_scr[j, rs, cs] = dmg * vm
                    dvm = dmg * u[rs, cs]
                    dvmb = dvm.astype(BF16)
                    dws_ref[g] += _dot(dvmb, vnb, NT)
                    dbs_scr[g] += dvm
                    dvn_scr[j, rs, cs] = _dot(wgt, dvmb)
        width = 4 * 768
        dvh = []
        for j in range(4):
            dvn = dvn_scr[j]
            dlg_ref[j:j + 1, :] += _colsum(dvn * vhat[j])
            dlb_ref[j:j + 1, :] += _colsum(dvn)
            dvh.append(dvn * lg_ref[j:j + 1, :])
        m1 = sum(jnp.sum(d, axis=1, keepdims=True) for d in dvh) / width
        m2 = sum(jnp.sum(dvh[j] * vhat[j], axis=1, keepdims=True) for j in range(4)) / width
        for j in range(4):
            dv = rstd * (dvh[j] - m1 - vhat[j] * m2)
            dpv = dv * gp_ref[4 + j].astype(F32)
            dpu = du_scr[j] * gp_ref[j].astype(F32)
            dpre_ref[4 + j] = dpv.astype(BF16)
            dpre_ref[j] = dpu.astype(BF16)
            dbin_ref[4 + j:5 + j, :] += _colsum(dpv)
            dbin_ref[j:j + 1, :] += _colsum(dpu)

        @pl.when(i == nb - 1)
        def _():
            r_i = lax.broadcasted_iota(jnp.int32, (GM_CHUNK, GM_CHUNK), 0)
            c_i = lax.broadcasted_iota(jnp.int32, (GM_CHUNK, GM_CHUNK), 1)
            for g in range(GM_GROUPS):
                dws_ref[g] = jnp.where(c_i <= r_i, dws_ref[g], 0.0)
                dbs_ref[g] = jnp.broadcast_to(jnp.sum(dbs_scr[g], axis=1, keepdims=True), (GM_CHUNK, GM_CHUNK))

    sq = pl.BlockSpec((GM_GROUPS, GM_CHUNK, GM_CHUNK), lambda i: (0, 0, 0))
    v4 = pl.BlockSpec((4, 768), lambda i: (0, 0))
    return pl.pallas_call(
        body, name=name, grid=(nb,),
        out_shape=(jax.ShapeDtypeStruct((T, 8 * 768), BF16),
                   jax.ShapeDtypeStruct((GM_GROUPS, GM_CHUNK, GM_CHUNK), F32),
                   jax.ShapeDtypeStruct((GM_GROUPS, GM_CHUNK, GM_CHUNK), F32),
                   jax.ShapeDtypeStruct((4, 768), F32), jax.ShapeDtypeStruct((4, 768), F32),
                   jax.ShapeDtypeStruct((8, 768), F32)),
        in_specs=[pl.BlockSpec((tb, 8 * 768), lambda i: (i, 0)),
                  pl.BlockSpec((tb, 8 * 768), lambda i: (i, 0)),
                  pl.BlockSpec((tb, 4 * 768), lambda i: (i, 0)),
                  v4, v4, sq, sq,
                  pl.BlockSpec((GM_GROUPS, GM_CHUNK, GM_GDIM), lambda i: (0, 0, 0))],
        out_specs=(pl.BlockSpec((tb, 8 * 768), lambda i: (i, 0)), sq, sq, v4, v4,
                   pl.BlockSpec((8, 768), lambda i: (0, 0))),
        scratch_shapes=[pltpu.VMEM((GM_GROUPS, GM_CHUNK, GM_GDIM), F32),
                        pltpu.VMEM((4, tb, 768), F32), pltpu.VMEM((4, tb, 768), F32)],
        compiler_params=_params("arbitrary"),
    )(pre, gp, dm, ln_g, ln_b, ws, ws_t, bsb)


def _adamw(slots, w, m, v, name, *, tr=256):
    S, R, C = slots.shape
    tr = next((t for t in (tr, tr // 2, tr // 4, tr // 8, tr // 16) if R % t == 0), R) if R > tr else R
    bc1 = 1.0 - ADAM_B1 ** ADAM_STEP
    bc2 = 1.0 - ADAM_B2 ** ADAM_STEP

    def body(s_ref, w_ref, m_ref, v_ref, g_ref, d_ref, nm_ref, nv_ref):
        g = s_ref[0].astype(F32)
        for s in range(1, S):
            g = g + s_ref[s].astype(F32)
        mn = ADAM_B1 * m_ref[...] + (1.0 - ADAM_B1) * g
        vn = ADAM_B2 * v_ref[...] + (1.0 - ADAM_B2) * (g * g)
        g_ref[...] = g
        nm_ref[...] = mn
        nv_ref[...] = vn
        d_ref[...] = -ADAM_LR * ((mn / bc1) / (jnp.sqrt(vn / bc2) + ADAM_EPS) + ADAM_WD * w_ref[...])

    spec = pl.BlockSpec((tr, C), lambda i: (i, 0))
    return pl.pallas_call(
        body, name=name, grid=(R // tr,),
        out_shape=(jax.ShapeDtypeStruct((R, C), F32),) * 4,
        in_specs=[pl.BlockSpec((S, tr, C), lambda i: (0, i, 0)), spec, spec, spec],
        out_specs=(spec,) * 4,
        compiler_params=_params("parallel"),
    )(slots, w, m, v)


def _update(slots, w, m, v, name):
    shp = w.shape
    C = shp[-1]
    R = math.prod(shp[:-1])
    outs = _adamw(slots.reshape(slots.shape[0], R, C), w.reshape(R, C), m.reshape(R, C), v.reshape(R, C), name)
    return tuple(o.reshape(shp) for o in outs)


def kernel(x, c, ada_w, ada_b, norm_pre, norm_post, ffn_w_in, ffn_w_out, hg_w_in, hg_w_out, hg_out_norm, hg_lb, gm_w_in, gm_b_in, gm_ln_g, gm_ln_b, gm_w_s, gm_b_s, gm_w_out, loss_target, m_ada_w, m_ada_b, m_norm_pre, m_norm_post, m_ffn_w_in, m_ffn_w_out, m_hg_w_in, m_hg_w_out, m_hg_out_norm, m_hg_lb, m_gm_w_in, m_gm_b_in, m_gm_ln_g, m_gm_ln_b, m_gm_w_s, m_gm_b_s, m_gm_w_out, v_ada_w, v_ada_b, v_norm_pre, v_norm_post, v_ffn_w_in, v_ffn_w_out, v_hg_w_in, v_hg_w_out, v_hg_out_norm, v_hg_lb, v_gm_w_in, v_gm_b_in, v_gm_ln_g, v_gm_ln_b, v_gm_w_s, v_gm_b_s, v_gm_w_out):
    me = 4 * lax.axis_index("x") + 2 * lax.axis_index("y") + lax.axis_index("c")
    T = x.shape[1]
    x0 = x.reshape(T, D_MODEL)
    target = loss_target.reshape(T, D_MODEL)
    n_ada = ada_w.shape[-1]

    pack = jnp.concatenate([
        c.reshape(8, 128), norm_pre.reshape(6, 128), norm_post.reshape(6, 128),
        gm_b_in.reshape(6, 128), gm_ln_g.reshape(3, 128), gm_ln_b.reshape(3, 128)], axis=0)
    packs = _all_gather(pack, "gather_small")
    c_all = packs[:, 0:8].reshape(NDEV, D_MODEL)
    npre = packs[:, 8:14].reshape(NDEV, 2, 3, 128).transpose(1, 2, 0, 3).reshape(2, 3, D_MODEL)
    npost = packs[:, 14:20].reshape(NDEV, 2, 3, 128).transpose(1, 2, 0, 3).reshape(2, 3, D_MODEL)
    b_in = packs[:, 20:26].reshape(1, NDEV * 768)
    ln_g = packs[:, 26:29].reshape(4, 768)
    ln_b = packs[:, 29:32].reshape(4, 768)

    ada_b_mine = lax.dynamic_slice_in_dim(ada_b, me * n_ada, n_ada, axis=1).reshape(2, 1, n_ada)
    mod_cols = _ada_fwd(c_all, ada_w, ada_b_mine, "ada_fwd")
    mod_all = _all_gather(mod_cols, "gather_mod")
    mod = lax.dynamic_index_in_dim(mod_all, me, axis=2, keepdims=False)
    mod = mod.transpose(1, 0, 2).reshape(2, 9, 1, D_MODEL)

    sh_fi, sh_fo = ffn_w_in.astype(BF16).swapaxes(-1, -2), ffn_w_out.astype(BF16)
    sh_hi, sh_ho = hg_w_in[0].astype(BF16).T, hg_w_out[0].astype(BF16)
    sh_mi, sh_mo = gm_w_in[0].astype(BF16).T, gm_w_out[0].astype(BF16)

    def whole(gathered):
        return gathered.reshape(-1, D_MODEL)

    w_fi = {(0, 0): whole(_all_gather(sh_fi[0, 0], "gather_ffn_in_first"))}
    w_fo = {}
    riders = {"l0s0": [sh_fo[0, 0], sh_hi, sh_ho, sh_fo[0, 1]], "l0s1": [sh_fi[0, 1]], "hg_mix": [sh_mi, sh_mo],
              "l0s2": [sh_fi[1, 0], sh_fo[1, 0]], "l1s0": [sh_fi[1, 1], sh_fo[1, 1]]}

    sm = jax.nn.softmax(hg_lb, axis=0)
    lb0 = sm[0:1]
    on = hg_out_norm.reshape(1, HG_HEAD)
    tril = jnp.tril(jnp.ones((GM_CHUNK, GM_CHUNK), F32))
    ws = gm_w_s[0] * tril[None]
    ws_t = ws.transpose(0, 2, 1)
    bsb = jnp.broadcast_to(gm_b_s[0][:, :, None], (GM_GROUPS, GM_CHUNK, GM_GDIM))

    res_ws = (0.5, 1.0, 0.5)

    def vecs(i, s):
        return (npre[i, s].reshape(1, D_MODEL), npost[i, s].reshape(1, D_MODEL),
                mod[i, 3 * s], mod[i, 3 * s + 1], mod[i, 3 * s + 2])

    order = [(i, s) for i in range(2) for s in range(3)]
    saved = {}
    xs = x0
    for pos, (i, s) in enumerate(order):
        tag = f"l{i}s{s}"
        pre_g, post_g, shift, scale, gate = vecs(i, s)
        rider = _Exchange("gather", riders[tag]) if tag in riders else None
        if s != 1:
            if pos == 0:
                (pq, a, h), got = _ffn_in(xs, w_fi[0, 0], "ffn_in_" + tag, opening=(pre_g, scale, shift), exchange=rider)
            else:
                (pq, a), got = _ffn_in(h, w_fi[i, s // 2], "ffn_in_" + tag, exchange=rider)
            extra = (pq, a)
            if tag == "l0s0":
                w_fo[0, 0], w_hi, w_ho, w_fo[0, 1] = map(whole, got)
            elif tag == "l0s2":
                w_fi[1, 0], w_fo[1, 0] = map(whole, got)
            elif tag == "l1s0":
                w_fi[1, 1], w_fo[1, 1] = map(whole, got)
            wo = w_fo[i, s // 2]
        elif i == 0:
            p_qig, p_f, *got = _mm_blocks(h, w_hi, "hg_in", chunk=1024, f32_cols=(D_MODEL, 2 * D_MODEL), exchange=rider)
            proj = (p_qig, p_f)
            w_fi[0, 1], = map(whole, got)
            o, og, states, got = _hgrn_fwd(proj, lb0, on, "hg_mix", exchange=_Exchange("gather", riders["hg_mix"]))
            w_mi, w_mo = map(whole, got)
            a, wo = og, w_ho
            extra = (proj, o, og, states)
        else:
            pre, gp = _mm_blocks(h, w_mi, "gm_in", bias=b_in, gelu=True)
            a = _gm_spatial_fwd(pre, ln_g, ln_b, ws, bsb, "gm_mix")
            wo = w_mo
            extra = (pre, gp, a)
        if pos + 1 < len(order):
            npre_g, _, nshift, nscale, _ = vecs(*order[pos + 1])
            y, x_next, h_next = _out_proj(a, wo, xs, post_g, gate, res_ws[s], (npre_g, nscale, nshift), "out_" + tag)
            saved[tag] = (xs, h, y) + extra
            xs, h = x_next, h_next
        else:
            dx, dy, dgate, dpost, loss_part = _out_proj_last(a, wo, xs, post_g, gate, res_ws[s], target, "out_" + tag)
            saved[tag] = (xs, h, None) + extra
    loss = lax.psum(loss_part[0, 0], ("x", "y", "c"))

    slots = {}
    d_npre = [[None] * 3, [None] * 3]
    d_npost = [[None] * 3, [None] * 3]
    d_mod = [[None] * 9, [None] * 9]
    for pos in reversed(range(len(order))):
        i, s = order[pos]
        tag = f"l{i}s{s}"
        pre_g, _, _, scale, _ = vecs(i, s)
        xin, h = saved[tag][:2]
        if s != 1:
            w_in, wo = w_fi[i, s // 2], w_fo[i, s // 2]
            pq, g = saved[tag][3:]
            dz = _ffn_dgate(dy, wo, pq, "ffn_dgate_" + tag)
            g_out = _mm_wgrad(g, dy, "ffn_out_wgrad_" + tag, xw=1408)
            g_in = _mm_wgrad(dz, h, "ffn_in_wgrad_" + tag, xw=1408)
        elif i == 0:
            proj, o, og, states = saved[tag][3:]
            dog = _mm_blocks(dy, w_ho, "hg_out_dgrad")
            g_out = _mm_wgrad(og, dy, "hg_out_wgrad")
            dz, d_lb0, d_on = _hgrn_bwd(proj, o, dog, states, lb0, on, "hg_mix_bwd")
            w_in = w_hi
            g_in = _mm_wgrad(h, dz, "hg_in_wgrad", yw=1024, split=2)
        else:
            pre, gp, sp = saved[tag][3:]
            dm = _mm_blocks(dy, w_mo, "gm_out_dgrad")
            g_out = _mm_wgrad(sp, dy, "gm_out_wgrad", xw=768)
            dz, d_ws, d_bs, d_lg, d_lbias, d_bin = _gm_spatial_bwd(pre, gp, dm, ln_g, ln_b, ws, ws_t, bsb, "gm_mix_bwd")
            w_in = w_mi
            g_in = _mm_wgrad(h, dz, "gm_in_wgrad", yw=1536, split=2)
        g_out = g_out.reshape(NDEV, -1, D_MODEL)
        g_in = g_in.reshape(NDEV, -1, D_MODEL) if s != 1 else g_in
        d_npost[i][s] = dpost
        d_mod[i][3 * s + 2] = dgate
        rider = _Exchange("scatter", [g_in, g_out])
        if pos > 0:
            pi, ps = order[pos - 1]
            _, ppost_g, _, _, pgate = vecs(pi, ps)
            prev = (saved[f"l{pi}s{ps}"][2], ppost_g, pgate, res_ws[ps])
            dx, dshift, dscale, dpre_g, dy, dgate, dpost, r_in, r_out = _in_grad(
                dz, w_in, dx, xin, pre_g, scale, prev, "in_grad_" + tag, exchange=rider)
        else:
            dx, dshift, dscale, dpre_g, r_in, r_out = _in_grad(
                dz, w_in, dx, xin, pre_g, scale, None, "in_grad_" + tag, exchange=rider)
        slots[tag] = (r_in, r_out)
        d_npre[i][s] = dpre_g
        d_mod[i][3 * s], d_mod[i][3 * s + 1] = dshift, dscale
    grad_x = dx.reshape(x.shape)

    ffn_tags = ["l0s0", "l0s2", "l1s0", "l1s2"]
    s_fi = jnp.stack([slots[t][0] for t in ffn_tags], axis=1).swapaxes(-1, -2)
    s_fo = jnp.stack([slots[t][1] for t in ffn_tags], axis=1)
    (s_hi, s_ho), (s_mi, s_mo) = slots["l0s1"], slots["l1s1"]
    s_hi, s_ho, s_mi, s_mo = s_hi[:, None], s_ho[:, None], s_mi[:, None], s_mo[:, None]

    gmod = jnp.stack([jnp.concatenate(d_mod[i], axis=0) for i in range(2)])
    d_sm = lb0 * d_lb0
    d_hg_lb = jnp.concatenate([d_sm, jnp.zeros((2, D_MODEL), F32)], axis=0) - sm * d_sm
    small = [gmod, jnp.stack([jnp.concatenate(r, axis=0) for r in d_npre]),
             jnp.stack([jnp.concatenate(r, axis=0) for r in d_npost]),
             d_on, d_hg_lb, d_bin, d_lg, d_lbias, d_ws, d_bs[:, :, 0]]
    sizes = [a.size for a in small]
    flat = jnp.concatenate([a.reshape(-1) for a in small])
    rows = -(-flat.size // (8 * 128)) * 8
    flat = jnp.pad(flat, (0, rows * 128 - flat.size)).reshape(rows, 128)
    flats = _all_gather(flat, "gather_small_grads").reshape(NDEV, rows * 128)
    parts, off = [], 0
    for a, n in zip(small, sizes):
        parts.append(flats[:, off:off + n].reshape((NDEV,) + a.shape))
        off += n
    p_mod, p_npre, p_npost, p_on, p_lb, p_bin, p_lg, p_lbias, p_ws, p_bs = parts

    def mine(p, width):
        return lax.dynamic_slice_in_dim(p, me * width, width, axis=p.ndim - 1)

    gmod_cols = mine(p_mod.reshape(NDEV, 2, 9 * D_MODEL), n_ada).transpose(1, 0, 2)
    g_ada_w = _ada_bwd(jnp.pad(c_all.T, ((0, 0), (0, 120))), jnp.pad(gmod_cols, ((0, 0), (0, 120), (0, 0))), "ada_bwd")

    out = {}
    out["ada_w"] = _update(g_ada_w[None], ada_w, m_ada_w, v_ada_w, "adamw_ada_w")
    out["ada_b"] = _update(p_mod.reshape(NDEV, 2, 9 * D_MODEL), ada_b, m_ada_b, v_ada_b, "adamw_ada_b")
    out["norm_pre"] = _update(mine(p_npre, 128), norm_pre, m_norm_pre, v_norm_pre, "adamw_norm_pre")
    out["norm_post"] = _update(mine(p_npost, 128), norm_post, m_norm_post, v_norm_post, "adamw_norm_post")
    out["ffn_w_in"] = _update(s_fi.reshape((NDEV,) + ffn_w_in.shape), ffn_w_in, m_ffn_w_in, v_ffn_w_in, "adamw_ffn_in")
    out["ffn_w_out"] = _update(s_fo.reshape((NDEV,) + ffn_w_out.shape), ffn_w_out, m_ffn_w_out, v_ffn_w_out, "adamw_ffn_out")
    out["hg_w_in"] = _update(s_hi, hg_w_in, m_hg_w_in, v_hg_w_in, "adamw_hg_in")
    out["hg_w_out"] = _update(s_ho, hg_w_out, m_hg_w_out, v_hg_w_out, "adamw_hg_out")
    out["hg_out_norm"] = _update(p_on, hg_out_norm, m_hg_out_norm, v_hg_out_norm, "adamw_hg_norm")
    out["hg_lb"] = _update(p_lb, hg_lb, m_hg_lb, v_hg_lb, "adamw_hg_lb")
    out["gm_w_in"] = _update(s_mi, gm_w_in, m_gm_w_in, v_gm_w_in, "adamw_gm_in")
    out["gm_b_in"] = _update(mine(p_bin.reshape(NDEV, 1, 8 * 768), 768), gm_b_in, m_gm_b_in, v_gm_b_in, "adamw_gm_b_in")
    out["gm_ln_g"] = _update(mine(p_lg.reshape(NDEV, 1, 4 * 768), 384), gm_ln_g, m_gm_ln_g, v_gm_ln_g, "adamw_gm_ln_g")
    out["gm_ln_b"] = _update(mine(p_lbias.reshape(NDEV, 1, 4 * 768), 384), gm_ln_b, m_gm_ln_b, v_gm_ln_b, "adamw_gm_ln_b")
    out["gm_w_s"] = _update(p_ws[:, None], gm_w_s, m_gm_w_s, v_gm_w_s, "adamw_gm_w_s")
    out["gm_b_s"] = _update(p_bs[:, None], gm_b_s, m_gm_b_s, v_gm_b_s, "adamw_gm_b_s")
    out["gm_w_out"] = _update(s_mo, gm_w_out, m_gm_w_out, v_gm_w_out, "adamw_gm_out")

    names = ["ada_w", "ada_b", "norm_pre", "norm_post", "ffn_w_in", "ffn_w_out", "hg_w_in", "hg_w_out",
             "hg_out_norm", "hg_lb", "gm_w_in", "gm_b_in", "gm_ln_g", "gm_ln_b", "gm_w_s", "gm_b_s", "gm_w_out"]
    return (loss, grad_x, *[out[n][0] for n in names], *[out[n][1] for n in names],
            *[out[n][2] for n in names], *[out[n][3] for n in names])
```

```python
import math

import jax
import jax.numpy as jnp
from jax import lax
from jax.experimental import pallas as pl
from jax.experimental.pallas import tpu as pltpu

F32 = jnp.float32
BF16 = jnp.bfloat16
NDEV = 8
D_MODEL = 1024
EPS = 1e-6
HG_CHUNK = 64
HG_HEAD = 128
HG_HEADS = 8
GM_CHUNK = 128
GM_GDIM = 384
GM_GROUPS = 8
ADAM_LR = 0.001
ADAM_B1 = 0.9
ADAM_B2 = 0.999
ADAM_EPS = 1e-08
ADAM_WD = 0.01
ADAM_STEP = 10
VMEM_LIMIT = 56 * 2 ** 20

NN = (((1,), (0,)), ((), ()))
NT = (((1,), (1,)), ((), ()))
TN = (((0,), (0,)), ((), ()))
MESH = pl.DeviceIdType.MESH
ANY = pl.BlockSpec(memory_space=pl.ANY)


def _dot(a, b, dims=NN, precision=None):
    return lax.dot_general(a, b, dims, preferred_element_type=F32, precision=precision)


def _params(*sem):
    return pltpu.CompilerParams(dimension_semantics=sem, vmem_limit_bytes=VMEM_LIMIT)


def _sigmoid(x):
    return 1.0 / (1.0 + jnp.exp(-x))


def _sigmoid_t(x):
    return 0.5 * jnp.tanh(0.5 * x) + 0.5


def _gelu_and_grad(x):
    c = math.sqrt(2.0 / math.pi)
    m = (c * 0.044715) * (x * x)
    t = jnp.tanh(x * (c + m))
    hp = 0.5 + 0.5 * t
    return x * hp, hp * (1.0 + (x * (1.0 - t)) * (c + 3.0 * m))


def _colsum(x):
    return jnp.sum(x, axis=0, keepdims=True)


def _rowmean(x):
    return jnp.mean(x, axis=-1, keepdims=True)


def _all_gather(shard, name):
    def body(x_ref, out_ref, send_sems, recv_sems, local_sem):
        x, y, c = lax.axis_index("x"), lax.axis_index("y"), lax.axis_index("c")
        me, sibling = (x, y, c), (x, y, 1 - c)
        chips = [(1 - x, y), (x, 1 - y), (1 - x, 1 - y)]

        def slot(p):
            return out_ref.at[4 * p[0] + 2 * p[1] + p[2]]

        def copy(k, block, to, src=None):
            return pltpu.make_async_remote_copy(
                src_ref=slot(block) if src is None else src, dst_ref=slot(block),
                send_sem=send_sems.at[k], recv_sem=recv_sems.at[k],
                device_id=to, device_id_type=MESH)

        mine = pltpu.make_async_copy(x_ref, slot(me), local_sem)
        mine.start()
        first = [copy(0, me, sibling, src=x_ref)]
        first += [copy(1 + j, me, (*chip, c), src=x_ref) for j, chip in enumerate(chips)]
        for cp in first:
            cp.start()
        passed = [copy(4 + j, (*chip, c), sibling) for j, chip in enumerate(chips)]
        for j, chip in enumerate(chips):
            copy(1 + j, (*chip, c), me).wait_recv()
            passed[j].start()
        copy(0, sibling, me).wait_recv()
        for j, chip in enumerate(chips):
            copy(4 + j, (*chip, 1 - c), me).wait_recv()
        for cp in first + passed:
            cp.wait_send()
        mine.wait()

    return pl.pallas_call(
        body, name=name,
        out_shape=jax.ShapeDtypeStruct((NDEV,) + shard.shape, shard.dtype),
        in_specs=[ANY], out_specs=ANY,
        scratch_shapes=[pltpu.SemaphoreType.DMA((7,)), pltpu.SemaphoreType.DMA((7,)),
                        pltpu.SemaphoreType.DMA(())],
    )(shard)


class _Exchange:
    def __init__(self, kind, arrays):
        self.gather = kind == "gather"
        self.arrays = list(arrays)
        self.n = n = len(self.arrays)
        self.out_shape = [jax.ShapeDtypeStruct(((NDEV,) + a.shape) if self.gather else a.shape, a.dtype)
                          for a in self.arrays]
        self.scratch = [pltpu.SemaphoreType.DMA((n, NDEV - 1)), pltpu.SemaphoreType.DMA((n, NDEV - 1)),
                        pltpu.SemaphoreType.DMA((n,))]

    def _copies(self, in_refs, out_refs, sems):
        send_sems, recv_sems, local_sems = sems
        x, y, c = lax.axis_index("x"), lax.axis_index("y"), lax.axis_index("c")
        me = 4 * x + 2 * y + c
        peers = [(1 - x if k & 4 else x, 1 - y if k & 2 else y, 1 - c if k & 1 else c) for k in range(1, NDEV)]
        local, send, recv = [], [], []
        for a in range(self.n):
            src = (lambda pid, a=a: in_refs[a]) if self.gather else (lambda pid, a=a: in_refs[a].at[pid])
            local.append(pltpu.make_async_copy(src(me), out_refs[a].at[me], local_sems.at[a]))
            for k, p in enumerate(peers):
                pid = 4 * p[0] + 2 * p[1] + p[2]
                for lst, slot in ((send, me), (recv, pid)):
                    lst.append(pltpu.make_async_remote_copy(
                        src_ref=src(pid), dst_ref=out_refs[a].at[slot],
                        send_sem=send_sems.at[a, k], recv_sem=recv_sems.at[a, k],
                        device_id=p, device_id_type=MESH))
        return local, send, recv

    def start(self, first, in_refs, out_refs, sems):
        @pl.when(first)
        def _():
            local, send, _ = self._copies(in_refs, out_refs, sems)
            for cp in local + send:
                cp.start()

    def finish(self, last, in_refs, out_refs, sems):
        @pl.when(last)
        def _():
            local, send, recv = self._copies(in_refs, out_refs, sems)
            for cp in send:
                cp.wait_send()
            for cp in recv:
                cp.wait_recv()
            for cp in local:
                cp.wait()


def _host(exchange, n_in, n_out, body, first_last):
    if exchange is None:
        return body, [], [], [], []
    n = exchange.n

    def hosted(*refs):
        ins, refs = refs[:n_in], refs[n_in:]
        xin, refs = refs[:n], refs[n:]
        outs, refs = refs[:n_out], refs[n_out:]
        xout, refs = refs[:n], refs[n:]
        scratch, sems = refs[:len(refs) - 3], refs[len(refs) - 3:]
        first, last = first_last()
        exchange.start(first, xin, xout, sems)
        body(*ins, *outs, *scratch)
        exchange.finish(last, xin, xout, sems)

    return hosted, exchange.arrays, [ANY] * n, exchange.out_shape, exchange.scratch


def _first_last(steps):
    def at():
        i = pl.program_id(0)
        return i == 0, i == steps - 1
    return at


class _ColBlocks:
    def __init__(self, ref, width):
        self.ref, self.width = ref, width

    def _index(self, key):
        key = key if isinstance(key, tuple) else (key,)
        rows = key[1] if len(key) > 1 else slice(None)
        cols = key[2] if len(key) > 2 else slice(0, self.width)
        c0 = key[0] * self.width
        return rows, slice(c0 + cols.start, c0 + cols.stop)

    def __getitem__(self, key):
        return self.ref[self._index(key)]

    def __setitem__(self, key, value):
        self.ref[self._index(key)] = value


def _col_chunks(width, chunk=768):
    return [slice(c, min(c + chunk, width)) for c in range(0, width, chunk)]


def _row_spec(tm, d):
    return pl.BlockSpec((tm, d), lambda m: (m, 0))


def _vec_spec(d):
    return pl.BlockSpec((1, d), lambda m: (0, 0))


def _whole_spec(w):
    nd = w.ndim
    return pl.BlockSpec(w.shape, lambda m: (0,) * nd, pipeline_mode=pl.Buffered(1))


def _mm_blocks(a, w, name, *, bias=None, out_dtype=BF16, tm=512, chunk=768, gelu=False, f32_cols=None,
               exchange=None):
    T, K = a.shape
    C = w.shape[0]
    tm = min(tm, T)
    n_in = 2 + (bias is not None)

    def body(*refs):
        a_ref, w_ref = refs[:2]
        av = a_ref[...]
        for cols in _col_chunks(C, chunk):
            r = _dot(av, w_ref[cols], NT)
            if bias is not None:
                r = r + refs[2][:, cols]
            if gelu:
                z, dz = _gelu_and_grad(r)
                refs[n_in][:, cols] = z.astype(BF16)
                refs[n_in + 1][:, cols] = dz.astype(BF16)
            elif f32_cols is None:
                refs[n_in][:, cols] = r.astype(out_dtype)
            elif f32_cols[0] <= cols.start < f32_cols[1]:
                refs[n_in + 1][:, cols.start - f32_cols[0]:cols.stop - f32_cols[0]] = r
            else:
                shift = 0 if cols.start < f32_cols[0] else f32_cols[1] - f32_cols[0]
                refs[n_in][:, cols.start - shift:cols.stop - shift] = r.astype(BF16)

    in_specs = [_row_spec(tm, K), _whole_spec(w)]
    args = [a, w]
    if bias is not None:
        in_specs.append(_whole_spec(bias))
        args.append(bias)
    if gelu:
        outs = [jax.ShapeDtypeStruct((T, C), BF16)] * 2
    elif f32_cols is not None:
        wide = f32_cols[1] - f32_cols[0]
        assert f32_cols[0] % chunk == 0 and wide % chunk == 0
        outs = [jax.ShapeDtypeStruct((T, C - wide), BF16), jax.ShapeDtypeStruct((T, wide), F32)]
    else:
        outs = [jax.ShapeDtypeStruct((T, C), out_dtype)]
    body, x_args, x_in, x_out, x_scratch = _host(exchange, n_in, len(outs), body, _first_last(T // tm))
    res = pl.pallas_call(
        body, name=name, grid=(T // tm,),
        out_shape=outs + x_out,
        in_specs=in_specs + x_in,
        out_specs=[_row_spec(tm, o.shape[1]) for o in outs] + x_in,
        scratch_shapes=x_scratch,
        compiler_params=_params("arbitrary" if exchange else "parallel"),
    )(*args, *x_args)
    return res if len(res) > 1 else res[0]


def _rms(v):
    return lax.rsqrt(_rowmean(v * v) + EPS)


def _zero_at_start(*refs):
    @pl.when(pl.program_id(0) == 0)
    def _():
        for r in refs:
            r[...] = jnp.zeros_like(r)


def _postnorm_bwd_math(dxo, yv, g, gate, res_w, dgate_ref, dpost_ref):
    r = _rms(yv)
    yh = yv * r
    both = res_w * _colsum(dxo * yh)
    dgate_ref[...] += g * both
    dpost_ref[...] += gate * both
    dyh = dxo * (res_w * gate * g)
    return (r * (dyh - yh * _rowmean(dyh * yh))).astype(BF16)


def _out_proj(a, w, x, post_g, gate, res_w, nxt, name, *, tm=512):
    T, ka = a.shape
    d = w.shape[1]
    tm = min(tm, T)

    def body(a_ref, w_ref, x_ref, pg_ref, gate_ref, ng_ref, nsc_ref, nsh_ref, y_ref, xn_ref, h_ref):
        y = _dot(a_ref[...], w_ref[...])
        y_ref[...] = y.astype(BF16)
        xn = x_ref[...] + (y * _rms(y)) * (res_w * gate_ref[...] * pg_ref[...])
        xn_ref[...] = xn
        h_ref[...] = ((xn * _rms(xn)) * (ng_ref[...] * (1.0 + nsc_ref[...])) + nsh_ref[...]).astype(BF16)

    return pl.pallas_call(
        body, name=name, grid=(T // tm,),
        out_shape=(jax.ShapeDtypeStruct((T, d), BF16), jax.ShapeDtypeStruct((T, d), F32),
                   jax.ShapeDtypeStruct((T, d), BF16)),
        in_specs=[_row_spec(tm, ka), _whole_spec(w), _row_spec(tm, d)] + [_vec_spec(d)] * 5,
        out_specs=(_row_spec(tm, d),) * 3,
        compiler_params=_params("parallel"),
    )(a, w, x, post_g, gate, *nxt)


def _out_proj_last(a, w, x, post_g, gate, res_w, target, name, *, tm=512):
    T, ka = a.shape
    d = w.shape[1]
    tm = min(tm, T)

    def body(a_ref, w_ref, x_ref, pg_ref, gate_ref, t_ref, dx_ref, dy_ref, dgate_ref, dpost_ref, l_ref):
        _zero_at_start(dgate_ref, dpost_ref, l_ref)
        y = _dot(a_ref[...], w_ref[...])
        e = x_ref[...] + res_w * gate_ref[...] * (y * _rms(y) * pg_ref[...]) - t_ref[...]
        l_ref[...] += 0.5 * jnp.sum(_rowmean(e * e), axis=0, keepdims=True)
        dx = e * (1.0 / d)
        dx_ref[...] = dx
        dy_ref[...] = _postnorm_bwd_math(dx, y, pg_ref[...], gate_ref[...], res_w, dgate_ref, dpost_ref)

    return pl.pallas_call(
        body, name=name, grid=(T // tm,),
        out_shape=(jax.ShapeDtypeStruct((T, d), F32), jax.ShapeDtypeStruct((T, d), BF16),
                   jax.ShapeDtypeStruct((1, d), F32), jax.ShapeDtypeStruct((1, d), F32),
                   jax.ShapeDtypeStruct((1, 128), F32)),
        in_specs=[_row_spec(tm, ka), _whole_spec(w), _row_spec(tm, d), _vec_spec(d), _vec_spec(d), _row_spec(tm, d)],
        out_specs=(_row_spec(tm, d), _row_spec(tm, d), _vec_spec(d), _vec_spec(d),
                   pl.BlockSpec((1, 128), lambda m: (0, 0))),
        compiler_params=_params("arbitrary"),
    )(a, w, x, post_g, gate, target)


def _in_grad(dz, w, dxo, x, pre_g, scale, prev, name, *, tm=512, exchange=None):
    T, C = dz.shape
    d = w.shape[1]
    tm = min(tm, T)
    has_prev = prev is not None
    res_w = prev[3] if has_prev else None

    def body(*refs):
        dz_ref, w_ref, dxo_ref, x_ref, g_ref, sc_ref = refs[:6]
        if has_prev:
            yp_ref, ppg_ref, pgate_ref, dx_ref, dsh_ref, dsc_ref, dg_ref, dyp_ref, dgate_ref, dpost_ref = refs[6:]
            _zero_at_start(dsh_ref, dsc_ref, dg_ref, dgate_ref, dpost_ref)
        else:
            dx_ref, dsh_ref, dsc_ref, dg_ref = refs[6:]
            _zero_at_start(dsh_ref, dsc_ref, dg_ref)
        dh = _dot(dz_ref[...], w_ref[...])
        xv = x_ref[...]
        r = _rms(xv)
        xh = xv * r
        gain = 1.0 + sc_ref[...]
        both = _colsum(dh * xh)
        dsh_ref[...] += _colsum(dh)
        dsc_ref[...] += g_ref[...] * both
        dg_ref[...] += gain * both
        dxh = dh * (gain * g_ref[...])
        dx = dxo_ref[...] + r * (dxh - xh * _rowmean(dxh * xh))
        dx_ref[...] = dx
        if has_prev:
            dyp_ref[...] = _postnorm_bwd_math(dx, yp_ref[...].astype(F32), ppg_ref[...], pgate_ref[...], res_w,
                                               dgate_ref, dpost_ref)

    vec = jax.ShapeDtypeStruct((1, d), F32)
    in_specs = [_row_spec(tm, C), _whole_spec(w), _row_spec(tm, d), _row_spec(tm, d), _vec_spec(d), _vec_spec(d)]
    out_shape = [jax.ShapeDtypeStruct((T, d), F32), vec, vec, vec]
    out_specs = [_row_spec(tm, d), _vec_spec(d), _vec_spec(d), _vec_spec(d)]
    args = [dz, w, dxo, x, pre_g, scale]
    if has_prev:
        in_specs += [_row_spec(tm, d), _vec_spec(d), _vec_spec(d)]
        out_shape += [jax.ShapeDtypeStruct((T, d), BF16), vec, vec]
        out_specs += [_row_spec(tm, d), _vec_spec(d), _vec_spec(d)]
        args += list(prev[:3])
    body, x_args, x_in, x_out, x_scratch = _host(exchange, len(args), len(out_shape), body, _first_last(T // tm))
    return pl.pallas_call(
        body, name=name, grid=(T // tm,),
        out_shape=out_shape + x_out, in_specs=in_specs + x_in, out_specs=out_specs + x_in,
        scratch_shapes=x_scratch,
        compiler_params=_params("arbitrary"),
    )(*args, *x_args)


def _mm_wgrad(x, y, name, *, xw=None, yw=None, split=1, tt=2048):
    T, P = x.shape
    Q = y.shape[1]
    xw, yw = xw or P, yw or Q
    jx, jy = P // xw, Q // yw
    assert jx == 1 or jy == 1
    tt = min(tt, T)
    nt = T // tt
    part = yw // split

    def body(x_ref, y_ref, o_ref, acc_ref):
        t = pl.program_id(1)

        @pl.when(t == 0)
        def _():
            acc_ref[...] = jnp.zeros_like(acc_ref)

        acc_ref[...] += _dot(x_ref[...], y_ref[...], TN)

        @pl.when(t == nt - 1)
        def _():
            if jy > 1:
                for k in range(split):
                    o_ref[k] = acc_ref[:, part * k:part * (k + 1)].astype(BF16)
            else:
                o_ref[...] = acc_ref[...].astype(BF16)

    if jy > 1:
        out_shape = jax.ShapeDtypeStruct((jy * split, P, part), BF16)
        out_spec = pl.BlockSpec((split, P, part), lambda j, t: (j, 0, 0))
    else:
        out_shape = jax.ShapeDtypeStruct((P, Q), BF16)
        out_spec = pl.BlockSpec((xw, Q), lambda j, t: (j, 0))
    return pl.pallas_call(
        body, name=name, grid=(max(jx, jy), nt),
        out_shape=out_shape,
        in_specs=[pl.BlockSpec((tt, xw), (lambda j, t: (t, j)) if jx > 1 else (lambda j, t: (t, 0))),
                  pl.BlockSpec((tt, yw), (lambda j, t: (t, j)) if jy > 1 else (lambda j, t: (t, 0)))],
        out_specs=out_spec,
        scratch_shapes=[pltpu.VMEM((xw, yw), F32)],
        compiler_params=_params("parallel", "arbitrary"),
    )(x, y)


def _ffn_in(h, wt, name, *, tm=512, opening=None, exchange=None):
    T, K = h.shape
    F = wt.shape[0] // 2
    tm = min(tm, T)
    n_vec = 3 if opening else 0

    def body(*refs):
        h_ref, w_ref = refs[0], refs[1 + n_vec]
        pq_ref, g_ref = refs[2 + n_vec], refs[3 + n_vec]
        hh = h_ref[...]
        if opening:
            g_vec, sc_vec, sh_vec = refs[1:4]
            hh = (hh * _rms(hh) * (g_vec[...] * (1.0 + sc_vec[...])) + sh_vec[...]).astype(BF16)
            refs[4 + n_vec][...] = hh
        for cols in _col_chunks(F):
            hi = slice(F + cols.start, F + cols.stop)
            a = _dot(hh, w_ref[cols], NT)
            b = _dot(hh, w_ref[hi], NT)
            s = _sigmoid_t(a)
            silu = a * s
            pq_ref[:, cols] = (b * (s * (1.0 + a * (1.0 - s)))).astype(BF16)
            pq_ref[:, hi] = silu.astype(BF16)
            g_ref[:, cols] = (silu * b).astype(BF16)

    outs = [jax.ShapeDtypeStruct((T, 2 * F), BF16), jax.ShapeDtypeStruct((T, F), BF16)]
    out_specs = [_row_spec(tm, 2 * F), _row_spec(tm, F)]
    if opening:
        outs.append(jax.ShapeDtypeStruct((T, K), BF16))
        out_specs.append(_row_spec(tm, K))
    vecs = list(opening) if opening else []
    body, x_args, x_in, x_out, x_scratch = _host(exchange, 2 + n_vec, len(outs), body, _first_last(T // tm))
    res = pl.pallas_call(
        body, name=name, grid=(T // tm,),
        out_shape=outs + x_out,
        in_specs=[_row_spec(tm, K)] + [_vec_spec(K)] * n_vec + [_whole_spec(wt)] + x_in,
        out_specs=out_specs + x_in,
        scratch_shapes=x_scratch,
        compiler_params=_params("arbitrary" if exchange else "parallel"),
    )(h, *vecs, wt, *x_args)
    return res[:len(outs)], res[len(outs):]


def _ffn_dgate(dy, w_out, pq, name, *, tm=512):
    T, N = dy.shape
    F = w_out.shape[0]
    tm = min(tm, T)

    def body(dy_ref, w_ref, pq_ref, dz_ref):
        dyv = dy_ref[...]
        for cols in _col_chunks(F):
            hi = slice(F + cols.start, F + cols.stop)
            dg = _dot(dyv, w_ref[cols], NT)
            dz_ref[:, cols] = (dg * pq_ref[:, cols].astype(F32)).astype(BF16)
            dz_ref[:, hi] = (dg * pq_ref[:, hi].astype(F32)).astype(BF16)

    return pl.pallas_call(
        body, name=name, grid=(T // tm,),
        out_shape=jax.ShapeDtypeStruct((T, 2 * F), BF16),
        in_specs=[_row_spec(tm, N), _whole_spec(w_out), _row_spec(tm, 2 * F)],
        out_specs=_row_spec(tm, 2 * F),
        compiler_params=_params("parallel"),
    )(dy, w_out, pq)


def _ada_fwd(c_all, w, b, name):
    L, K, n = w.shape

    def body(c_ref, w_ref, b_ref, o_ref):
        cv = c_ref[...]
        cond = cv * _sigmoid(cv)
        for l in range(L):
            o_ref[l] = _dot(cond, w_ref[l], precision=lax.Precision.HIGHEST) + b_ref[l]

    return pl.pallas_call(
        body, name=name,
        out_shape=jax.ShapeDtypeStruct((L, NDEV, n), F32),
        compiler_params=pltpu.CompilerParams(vmem_limit_bytes=VMEM_LIMIT),
    )(c_all, w, b)


def _ada_bwd(c_all_t, gmod, name):
    L, _, n = gmod.shape
    K = c_all_t.shape[0]

    def body(c_ref, g_ref, o_ref):
        cv = c_ref[...]
        cond = cv * _sigmoid(cv)
        for l in range(L):
            o_ref[l] = _dot(cond, g_ref[l], precision=lax.Precision.HIGHEST)

    return pl.pallas_call(
        body, name=name,
        out_shape=jax.ShapeDtypeStruct((L, K, n), F32),
        compiler_params=pltpu.CompilerParams(vmem_limit_bytes=VMEM_LIMIT),
    )(c_all_t, gmod)


def _tri(n, upper=False, block=None):
    r = lax.broadcasted_iota(jnp.int32, (n, n), 0)
    c = lax.broadcasted_iota(jnp.int32, (n, n), 1)
    m = (c >= r) if upper else (c <= r)
    if block is not None:
        m = m & ((r // block) == (c // block))
    return m.astype(BF16)


TRI_ROWS = 128


def _tri_dot(tri, x):
    hi = x.astype(BF16)
    lo = (x - hi.astype(F32)).astype(BF16)
    rows = x.shape[0]
    step = min(TRI_ROWS, rows)
    parts = [_dot(tri, hi[r:r + step]) + _dot(tri, lo[r:r + step]) for r in range(0, rows, step)]
    return parts[0] if len(parts) == 1 else jnp.concatenate(parts, axis=0)


def _hgrn_gates(proj_ref, f_ref, lb_ref, jh):
    lb = lb_ref[:, 512 * jh:512 * (jh + 1)]
    qp = _ColBlocks(proj_ref, 512)[jh].astype(F32)
    fx = _ColBlocks(f_ref, 512)[jh]
    sq = _sigmoid_t(qp)
    sig = _sigmoid_t(fx)
    f = lb + (1.0 - lb) * sig
    k = (1.0 - lb) * (1.0 - sig)
    return lb, qp, sq, sig, f, k


def _hgrn_fwd(proj, lb, out_norm, name, *, tb=512, exchange=None):
    T = proj[0].shape[0]
    tb = min(tb, T)
    nc = tb // HG_CHUNK
    lmat = _tri(min(TRI_ROWS, tb), block=HG_CHUNK)

    def body(proj_ref, f_ref, lb_ref, on_ref, l_ref, o_ref, og_ref, st_ref, s_scr, b_scr):
        @pl.when(pl.program_id(0) == 0)
        def _():
            s_scr[...] = jnp.zeros_like(s_scr)

        r_i = lax.broadcasted_iota(jnp.int32, (HG_CHUNK, HG_CHUNK), 0)
        c_i = lax.broadcasted_iota(jnp.int32, (HG_CHUNK, HG_CHUNK), 1)
        causal = c_i <= r_i
        onv = on_ref[...]
        blocks = _ColBlocks(proj_ref, 512)
        for jh in range(2):
            lbv, qp, sq, sig, f, k = _hgrn_gates(proj_ref, f_ref, lb_ref, jh)
            q = qp * sq
            b_half = b_scr.at[jh]
            b_half[...] = _tri_dot(l_ref[...], jnp.log(f))
            v = blocks[2 + jh]
            gp = blocks[4 + jh].astype(F32)
            gs = gp * _sigmoid_t(gp)
            for hh in range(4):
                hd = 4 * jh + hh
                cs = slice(HG_HEAD * hh, HG_HEAD * (hh + 1))
                for ci in range(nc):
                    r0 = HG_CHUNK * ci
                    rs = slice(r0, r0 + HG_CHUNK)
                    bc = b_half[rs, cs]
                    bm = b_half[r0 + HG_CHUNK // 2 - 1:r0 + HG_CHUNK // 2, cs]
                    bl = b_half[r0 + HG_CHUNK - 1:r0 + HG_CHUNK, cs]
                    qc, kc, vc = q[rs, cs], k[rs, cs], v[rs, cs].astype(BF16)
                    e_q, e_k = jnp.exp(bc - bm), jnp.exp(bm - bc)
                    qe = (qc * (e_q * jnp.exp(bm))).astype(BF16)
                    qt = (qc * e_q).astype(BF16)
                    kt = (kc * e_k).astype(BF16)
                    kd = (kc * (e_k * jnp.exp(bl - bm))).astype(BF16)
                    st = s_scr[hd]
                    stb = st.astype(BF16)
                    st_ref[ci, hd] = stb
                    a = jnp.where(causal, _dot(qt, kt, NT), 0.0).astype(BF16)
                    o = _dot(qe, stb, NT) + _dot(a, vc)
                    s_scr[hd] = st * jnp.exp(bl) + _dot(vc, kd, TN)
                    o_ref[rs, HG_HEAD * hd:HG_HEAD * (hd + 1)] = o
                    r = lax.rsqrt(_rowmean(o * o) + EPS)
                    og_ref[rs, HG_HEAD * hd:HG_HEAD * (hd + 1)] = (o * r * onv * gs[rs, cs]).astype(BF16)

    body, x_args, x_in, x_out, x_scratch = _host(exchange, 5, 3, body, _first_last(T // tb))
    res = pl.pallas_call(
        body, name=name, grid=(T // tb,),
        out_shape=[jax.ShapeDtypeStruct((T, D_MODEL), F32), jax.ShapeDtypeStruct((T, D_MODEL), BF16),
                   jax.ShapeDtypeStruct((T // HG_CHUNK, HG_HEADS, HG_HEAD, HG_HEAD), BF16)] + x_out,
        in_specs=[pl.BlockSpec((tb, 3 * D_MODEL), lambda i: (i, 0)),
                  pl.BlockSpec((tb, D_MODEL), lambda i: (i, 0)),
                  pl.BlockSpec((1, D_MODEL), lambda i: (0, 0)),
                  pl.BlockSpec((1, HG_HEAD), lambda i: (0, 0)),
                  pl.BlockSpec(lmat.shape, lambda i: (0, 0))] + x_in,
        out_specs=[pl.BlockSpec((tb, D_MODEL), lambda i: (i, 0)),
                   pl.BlockSpec((tb, D_MODEL), lambda i: (i, 0)),
                   pl.BlockSpec((nc, HG_HEADS, HG_HEAD, HG_HEAD), lambda i: (i, 0, 0, 0))] + x_in,
        scratch_shapes=[pltpu.VMEM((HG_HEADS, HG_HEAD, HG_HEAD), F32), pltpu.VMEM((2, tb, 512), F32)] + x_scratch,
        compiler_params=_params("arbitrary"),
    )(*proj, lb, out_norm, lmat, *x_args)
    return res[0], res[1], res[2], res[3:]


def _hgrn_bwd(proj, o, dog, states, lb, out_norm, name, *, tb=512):
    T = proj[0].shape[0]
    tb = min(tb, T)
    nc = tb // HG_CHUNK
    nb = T // tb
    lmat = _tri(min(TRI_ROWS, tb), block=HG_CHUNK)
    umat = _tri(min(TRI_ROWS, tb), upper=True, block=HG_CHUNK)

    def body(proj_ref, f_ref, o_ref, dog_ref, st_ref, lb_ref, on_ref, l_ref, u_ref,
             dproj_ref, dlb_ref, don_ref, ds_scr, *half_scr):
        @pl.when(pl.program_id(0) == 0)
        def _():
            ds_scr[...] = jnp.zeros_like(ds_scr)
            dlb_ref[...] = jnp.zeros_like(dlb_ref)
            don_ref[...] = jnp.zeros_like(don_ref)

        r_i = lax.broadcasted_iota(jnp.int32, (HG_CHUNK, HG_CHUNK), 0)
        c_i = lax.broadcasted_iota(jnp.int32, (HG_CHUNK, HG_CHUNK), 1)
        causal = c_i <= r_i
        causal_t = r_i <= c_i
        last_row = lax.broadcasted_iota(jnp.int32, (HG_CHUNK, HG_HEAD), 0) == HG_CHUNK - 1
        onv = on_ref[...]
        don_acc = jnp.zeros((1, HG_HEAD), F32)
        blocks = _ColBlocks(proj_ref, 512)
        dproj_ref = _ColBlocks(dproj_ref, 512)
        for jh in range(2):
            b_scr, dq_scr, dk_scr, dv_scr, dg_scr, db_scr = [s.at[jh] for s in half_scr]
            lbv, qp, sq, sig, f, k = _hgrn_gates(proj_ref, f_ref, lb_ref, jh)
            q = qp * sq
            b_scr[...] = _tri_dot(l_ref[...], jnp.log(f))
            v = blocks[2 + jh]
            gp = blocks[4 + jh].astype(F32)
            sg = _sigmoid_t(gp)
            for ci in reversed(range(nc)):
                r0 = HG_CHUNK * ci
                rs = slice(r0, r0 + HG_CHUNK)
                for hh in range(4):
                    hd = 4 * jh + hh
                    cs = slice(HG_HEAD * hh, HG_HEAD * (hh + 1))
                    hs = slice(HG_HEAD * hd, HG_HEAD * (hd + 1))
                    oc = o_ref[rs, hs]
                    r = lax.rsqrt(_rowmean(oc * oc) + EPS)
                    oh = oc * r
                    gc, sgc = gp[rs, cs], sg[rs, cs]
                    dogc = dog_ref[rs, hs].astype(F32)
                    don = dogc * (gc * sgc)
                    dg_scr[rs, cs] = dogc * (oh * onv) * (sgc * (1.0 + gc * (1.0 - sgc)))
                    don_acc += _colsum(don * oh)
                    donh = don * onv
                    do = (r * (donh - oh * _rowmean(donh * oh))).astype(BF16)
                    bc = b_scr[rs, cs]
                    bm = b_scr[r0 + HG_CHUNK // 2 - 1:r0 + HG_CHUNK // 2, cs]
                    bl = b_scr[r0 + HG_CHUNK - 1:r0 + HG_CHUNK, cs]
                    qc, kc, vc = q[rs, cs], k[rs, cs], v[rs, cs].astype(BF16)
                    e_q, e_k = jnp.exp(bc - bm), jnp.exp(bm - bc)
                    e_b, e_d = e_q * jnp.exp(bm), e_k * jnp.exp(bl - bm)
                    qe = (qc * e_b).astype(BF16)
                    qt = (qc * e_q).astype(BF16)
                    kt = (kc * e_k).astype(BF16)
                    kd = (kc * e_d).astype(BF16)
                    stb = st_ref[ci, hd]
                    dst = ds_scr[hd]
                    dstb = dst.astype(BF16)
                    a_t = jnp.where(causal_t, _dot(kt, qt, NT), 0.0).astype(BF16)
                    da = jnp.where(causal, _dot(do, vc, NT), 0.0).astype(BF16)
                    da_t = jnp.where(causal_t, _dot(vc, do, NT), 0.0).astype(BF16)
                    dv_scr[rs, cs] = _dot(a_t, do) + _dot(kd, dstb, NT)
                    dqe, dqt = _dot(do, stb), _dot(da, kt)
                    dkt, dkd = _dot(da_t, qt), _dot(vc, dstb)
                    dq_scr[rs, cs] = dqe * e_b + dqt * e_q
                    dk_scr[rs, cs] = dkt * e_k + dkd * e_d
                    e_l = jnp.exp(bl)
                    kd_dkd = kd.astype(F32) * dkd
                    dbc = qe.astype(F32) * dqe + qt.astype(F32) * dqt - kt.astype(F32) * dkt - kd_dkd
                    handed = e_l * _colsum(dstb.astype(F32) * stb.astype(F32)) + _colsum(kd_dkd)
                    db_scr[rs, cs] = dbc + jnp.where(last_row, handed, 0.0)
                    ds_scr[hd] = dst * e_l + _dot(do, qe, TN)
            dq = dq_scr[...]
            dk = dk_scr[...]
            cols = slice(512 * jh, 512 * (jh + 1))
            dlogf = _tri_dot(u_ref[...], db_scr[...])
            one_m_sig = 1.0 - sig
            dsig = (1.0 - lbv) * sig * one_m_sig
            dboth = dlogf / f - dk
            dproj_ref[jh] = (dq * (sq * (1.0 + qp * (1.0 - sq)))).astype(BF16)
            dproj_ref[2 + jh] = (dboth * dsig).astype(BF16)
            dproj_ref[4 + jh] = dv_scr[...].astype(BF16)
            dproj_ref[6 + jh] = dg_scr[...].astype(BF16)
            dlb_ref[:, cols] += _colsum(dboth * one_m_sig)
        don_ref[...] += don_acc

    rev = lambda i: nb - 1 - i
    return pl.pallas_call(
        body, name=name, grid=(nb,),
        out_shape=(jax.ShapeDtypeStruct((T, 4 * D_MODEL), BF16), jax.ShapeDtypeStruct((1, D_MODEL), F32),
                   jax.ShapeDtypeStruct((1, HG_HEAD), F32)),
        in_specs=[pl.BlockSpec((tb, 3 * D_MODEL), lambda i: (rev(i), 0)),
                  pl.BlockSpec((tb, D_MODEL), lambda i: (rev(i), 0)),
                  pl.BlockSpec((tb, D_MODEL), lambda i: (rev(i), 0)),
                  pl.BlockSpec((tb, D_MODEL), lambda i: (rev(i), 0)),
                  pl.BlockSpec((nc, HG_HEADS, HG_HEAD, HG_HEAD), lambda i: (rev(i), 0, 0, 0)),
                  pl.BlockSpec((1, D_MODEL), lambda i: (0, 0)),
                  pl.BlockSpec((1, HG_HEAD), lambda i: (0, 0)),
                  pl.BlockSpec(lmat.shape, lambda i: (0, 0)),
                  pl.BlockSpec(lmat.shape, lambda i: (0, 0))],
        out_specs=(pl.BlockSpec((tb, 4 * D_MODEL), lambda i: (rev(i), 0)),
                   pl.BlockSpec((1, D_MODEL), lambda i: (0, 0)),
                   pl.BlockSpec((1, HG_HEAD), lambda i: (0, 0))),
        scratch_shapes=[pltpu.VMEM((HG_HEADS, HG_HEAD, HG_HEAD), F32)] + [pltpu.VMEM((2, tb, 512), F32)] * 6,
        compiler_params=_params("arbitrary"),
    )(*proj, o, dog, states, lb, out_norm, lmat, umat)


def _gm_norm(pre_ref, lg_ref, lbias_ref):
    pre_ref = _ColBlocks(pre_ref, 768)
    vs = [pre_ref[4 + j].astype(F32) for j in range(4)]
    width = 4 * vs[0].shape[1]
    mu = sum(jnp.sum(v, axis=1, keepdims=True) for v in vs) / width
    ds = [v - mu for v in vs]
    var = sum(jnp.sum(d * d, axis=1, keepdims=True) for d in ds) / width
    rstd = lax.rsqrt(var + EPS)
    vhat = [d * rstd for d in ds]
    vn = [vhat[j] * lg_ref[j:j + 1, :] + lbias_ref[j:j + 1, :] for j in range(4)]
    return vhat, vn, rstd


def _gm_spatial_fwd(pre, ln_g, ln_b, ws, bsb, name, *, tb=512):
    T = pre.shape[0]
    tb = min(tb, T)
    nc = tb // GM_CHUNK

    def body(pre_ref, lg_ref, lbias_ref, ws_ref, bs_ref, o_ref):
        _, vn, _ = _gm_norm(pre_ref, lg_ref, lbias_ref)
        pre_ref, o_ref = _ColBlocks(pre_ref, 768), _ColBlocks(o_ref, 768)
        for j in range(4):
            u = pre_ref[j].astype(F32)
            for e in range(2):
                g = 2 * j + e
                cs = slice(GM_GDIM * e, GM_GDIM * (e + 1))
                wg = ws_ref[g].astype(BF16)
                for ci in range(nc):
                    rs = slice(GM_CHUNK * ci, GM_CHUNK * (ci + 1))
                    vm = _dot(wg, vn[j][rs, cs].astype(BF16)) + bs_ref[g]
                    o_ref[j, rs, cs] = (u[rs, cs] * vm).astype(BF16)

    return pl.pallas_call(
        body, name=name, grid=(T // tb,),
        out_shape=jax.ShapeDtypeStruct((T, 4 * 768), BF16),
        in_specs=[pl.BlockSpec((tb, 8 * 768), lambda i: (i, 0)),
                  pl.BlockSpec((4, 768), lambda i: (0, 0)),
                  pl.BlockSpec((4, 768), lambda i: (0, 0)),
                  pl.BlockSpec((GM_GROUPS, GM_CHUNK, GM_CHUNK), lambda i: (0, 0, 0)),
                  pl.BlockSpec((GM_GROUPS, GM_CHUNK, GM_GDIM), lambda i: (0, 0, 0))],
        out_specs=pl.BlockSpec((tb, 4 * 768), lambda i: (i, 0)),
        compiler_params=_params("parallel"),
    )(pre, ln_g, ln_b, ws, bsb)


def _gm_spatial_bwd(pre, gp, dm, ln_g, ln_b, ws, ws_t, bsb, name, *, tb=256):
    T = pre.shape[0]
    tb = min(tb, T)
    nc = tb // GM_CHUNK
    nb = T // tb

    def body(pre_ref, gp_ref, dm_ref, lg_ref, lbias_ref, ws_ref, wst_ref, bs_ref,
             dpre_ref, dws_ref, dbs_ref, dlg_ref, dlb_ref, dbin_ref, dbs_scr, dvn_scr, du_scr):
        i = pl.program_id(0)

        @pl.when(i == 0)
        def _():
            dws_ref[...] = jnp.zeros_like(dws_ref)
            dbs_scr[...] = jnp.zeros_like(dbs_scr)
            dlg_ref[...] = jnp.zeros_like(dlg_ref)
            dlb_ref[...] = jnp.zeros_like(dlb_ref)
            dbin_ref[...] = jnp.zeros_like(dbin_ref)

        vhat, vn, rstd = _gm_norm(pre_ref, lg_ref, lbias_ref)
        pre_ref, gp_ref, dm_ref = _ColBlocks(pre_ref, 768), _ColBlocks(gp_ref, 768), _ColBlocks(dm_ref, 768)
        dpre_ref = _ColBlocks(dpre_ref, 768)
        for j in range(4):
            u = pre_ref[j].astype(F32)
            for e in range(2):
                g = 2 * j + e
                cs = slice(GM_GDIM * e, GM_GDIM * (e + 1))
                wg = ws_ref[g].astype(BF16)
                wgt = wst_ref[g].astype(BF16)
                for ci in range(nc):
                    rs = slice(GM_CHUNK * ci, GM_CHUNK * (ci + 1))
                    vnb = vn[j][rs, cs].astype(BF16)
                    vm = _dot(wg, vnb) + bs_ref[g]
                    dmg = dm_ref[j, rs, cs].astype(F32)
                    du_scr[j, rs, cs] = dmg * vm
                    dvm = dmg * u[rs, cs]
                    dvmb = dvm.astype(BF16)
                    dws_ref[g] += _dot(dvmb, vnb, NT)
                    dbs_scr[g] += dvm
                    dvn_scr[j, rs, cs] = _dot(wgt, dvmb)
        width = 4 * 768
        dvh = []
        for j in range(4):
            dvn = dvn_scr[j]
            dlg_ref[j:j + 1, :] += _colsum(dvn * vhat[j])
            dlb_ref[j:j + 1, :] += _colsum(dvn)
            dvh.append(dvn * lg_ref[j:j + 1, :])
        m1 = sum(jnp.sum(d, axis=1, keepdims=True) for d in dvh) / width
        m2 = sum(jnp.sum(dvh[j] * vhat[j], axis=1, keepdims=True) for j in range(4)) / width
        for j in range(4):
            dv = rstd * (dvh[j] - m1 - vhat[j] * m2)
            dpv = dv * gp_ref[4 + j].astype(F32)
            dpu = du_scr[j] * gp_ref[j].astype(F32)
            dpre_ref[4 + j] = dpv.astype(BF16)
            dpre_ref[j] = dpu.astype(BF16)
            dbin_ref[4 + j:5 + j, :] += _colsum(dpv)
            dbin_ref[j:j + 1, :] += _colsum(dpu)

        @pl.when(i == nb - 1)
        def _():
            r_i = lax.broadcasted_iota(jnp.int32, (GM_CHUNK, GM_CHUNK), 0)
            c_i = lax.broadcasted_iota(jnp.int32, (GM_CHUNK, GM_CHUNK), 1)
            for g in range(GM_GROUPS):
                dws_ref[g] = jnp.where(c_i <= r_i, dws_ref[g], 0.0)
                dbs_ref[g] = jnp.broadcast_to(jnp.sum(dbs_scr[g], axis=1, keepdims=True), (GM_CHUNK, GM_CHUNK))

    sq = pl.BlockSpec((GM_GROUPS, GM_CHUNK, GM_CHUNK), lambda i: (0, 0, 0))
    v4 = pl.BlockSpec((4, 768), lambda i: (0, 0))
    return pl.pallas_call(
        body, name=name, grid=(nb,),
        out_shape=(jax.ShapeDtypeStruct((T, 8 * 768), BF16),
                   jax.ShapeDtypeStruct((GM_GROUPS, GM_CHUNK, GM_CHUNK), F32),
                   jax.ShapeDtypeStruct((GM_GROUPS, GM_CHUNK, GM_CHUNK), F32),
                   jax.ShapeDtypeStruct((4, 768), F32), jax.ShapeDtypeStruct((4, 768), F32),
                   jax.ShapeDtypeStruct((8, 768), F32)),
        in_specs=[pl.BlockSpec((tb, 8 * 768), lambda i: (i, 0)),
                  pl.BlockSpec((tb, 8 * 768), lambda i: (i, 0)),
                  pl.BlockSpec((tb, 4 * 768), lambda i: (i, 0)),
                  v4, v4, sq, sq,
                  pl.BlockSpec((GM_GROUPS, GM_CHUNK, GM_GDIM), lambda i: (0, 0, 0))],
        out_specs=(pl.BlockSpec((tb, 8 * 768), lambda i: (i, 0)), sq, sq, v4, v4,
                   pl.BlockSpec((8, 768), lambda i: (0, 0))),
        scratch_shapes=[pltpu.VMEM((GM_GROUPS, GM_CHUNK, GM_GDIM), F32),
                        pltpu.VMEM((4, tb, 768), F32), pltpu.VMEM((4, tb, 768), F32)],
        compiler_params=_params("arbitrary"),
    )(pre, gp, dm, ln_g, ln_b, ws, ws_t, bsb)


def _adamw(slots, w, m, v, name, *, tr=256):
    S, R, C = slots.shape
    tr = next((t for t in (tr, tr // 2, tr // 4, tr // 8, tr // 16) if R % t == 0), R) if R > tr else R
    bc1 = 1.0 - ADAM_B1 ** ADAM_STEP
    bc2 = 1.0 - ADAM_B2 ** ADAM_STEP

    def body(s_ref, w_ref, m_ref, v_ref, g_ref, d_ref, nm_ref, nv_ref):
        g = s_ref[0].astype(F32)
        for s in range(1, S):
            g = g + s_ref[s].astype(F32)
        mn = ADAM_B1 * m_ref[...] + (1.0 - ADAM_B1) * g
        vn = ADAM_B2 * v_ref[...] + (1.0 - ADAM_B2) * (g * g)
        g_ref[...] = g
        nm_ref[...] = mn
        nv_ref[...] = vn
        d_ref[...] = -ADAM_LR * ((mn / bc1) / (jnp.sqrt(vn / bc2) + ADAM_EPS) + ADAM_WD * w_ref[...])

    spec = pl.BlockSpec((tr, C), lambda i: (i, 0))
    return pl.pallas_call(
        body, name=name, grid=(R // tr,),
        out_shape=(jax.ShapeDtypeStruct((R, C), F32),) * 4,
        in_specs=[pl.BlockSpec((S, tr, C), lambda i: (0, i, 0)), spec, spec, spec],
        out_specs=(spec,) * 4,
        compiler_params=_params("parallel"),
    )(slots, w, m, v)


def _update(slots, w, m, v, name):
    shp = w.shape
    C = shp[-1]
    R = math.prod(shp[:-1])
    outs = _adamw(slots.reshape(slots.shape[0], R, C), w.reshape(R, C), m.reshape(R, C), v.reshape(R, C), name)
    return tuple(o.reshape(shp) for o in outs)


def kernel(x, c, ada_w, ada_b, norm_pre, norm_post, ffn_w_in, ffn_w_out, hg_w_in, hg_w_out, hg_out_norm, hg_lb, gm_w_in, gm_b_in, gm_ln_g, gm_ln_b, gm_w_s, gm_b_s, gm_w_out, loss_target, m_ada_w, m_ada_b, m_norm_pre, m_norm_post, m_ffn_w_in, m_ffn_w_out, m_hg_w_in, m_hg_w_out, m_hg_out_norm, m_hg_lb, m_gm_w_in, m_gm_b_in, m_gm_ln_g, m_gm_ln_b, m_gm_w_s, m_gm_b_s, m_gm_w_out, v_ada_w, v_ada_b, v_norm_pre, v_norm_post, v_ffn_w_in, v_ffn_w_out, v_hg_w_in, v_hg_w_out, v_hg_out_norm, v_hg_lb, v_gm_w_in, v_gm_b_in, v_gm_ln_g, v_gm_ln_b, v_gm_w_s, v_gm_b_s, v_gm_w_out):
    me = 4 * lax.axis_index("x") + 2 * lax.axis_index("y") + lax.axis_index("c")
    T = x.shape[1]
    x0 = x.reshape(T, D_MODEL)
    target = loss_target.reshape(T, D_MODEL)
    n_ada = ada_w.shape[-1]

    pack = jnp.concatenate([
        c.reshape(8, 128), norm_pre.reshape(6, 128), norm_post.reshape(6, 128),
        gm_b_in.reshape(6, 128), gm_ln_g.reshape(3, 128), gm_ln_b.reshape(3, 128)], axis=0)
    packs = _all_gather(pack, "gather_small")
    c_all = packs[:, 0:8].reshape(NDEV, D_MODEL)
    npre = packs[:, 8:14].reshape(NDEV, 2, 3, 128).transpose(1, 2, 0, 3).reshape(2, 3, D_MODEL)
    npost = packs[:, 14:20].reshape(NDEV, 2, 3, 128).transpose(1, 2, 0, 3).reshape(2, 3, D_MODEL)
    b_in = packs[:, 20:26].reshape(1, NDEV * 768)
    ln_g = packs[:, 26:29].reshape(4, 768)
    ln_b = packs[:, 29:32].reshape(4, 768)

    ada_b_mine = lax.dynamic_slice_in_dim(ada_b, me * n_ada, n_ada, axis=1).reshape(2, 1, n_ada)
    mod_cols = _ada_fwd(c_all, ada_w, ada_b_mine, "ada_fwd")
    mod_all = _all_gather(mod_cols, "gather_mod")
    mod = lax.dynamic_index_in_dim(mod_all, me, axis=2, keepdims=False)
    mod = mod.transpose(1, 0, 2).reshape(2, 9, 1, D_MODEL)

    sh_fi, sh_fo = ffn_w_in.astype(BF16).swapaxes(-1, -2), ffn_w_out.astype(BF16)
    sh_hi, sh_ho = hg_w_in[0].astype(BF16).T, hg_w_out[0].astype(BF16)
    sh_mi, sh_mo = gm_w_in[0].astype(BF16).T, gm_w_out[0].astype(BF16)

    def whole(gathered):
        return gathered.reshape(-1, D_MODEL)

    w_fi = {(0, 0): whole(_all_gather(sh_fi[0, 0], "gather_ffn_in_first"))}
    w_fo = {}
    riders = {"l0s0": [sh_fo[0, 0], sh_hi, sh_ho, sh_fo[0, 1]], "l0s1": [sh_fi[0, 1]], "hg_mix": [sh_mi, sh_mo],
              "l0s2": [sh_fi[1, 0], sh_fo[1, 0]], "l1s0": [sh_fi[1, 1], sh_fo[1, 1]]}

    sm = jax.nn.softmax(hg_lb, axis=0)
    lb0 = sm[0:1]
    on = hg_out_norm.reshape(1, HG_HEAD)
    tril = jnp.tril(jnp.ones((GM_CHUNK, GM_CHUNK), F32))
    ws = gm_w_s[0] * tril[None]
    ws_t = ws.transpose(0, 2, 1)
    bsb = jnp.broadcast_to(gm_b_s[0][:, :, None], (GM_GROUPS, GM_CHUNK, GM_GDIM))

    res_ws = (0.5, 1.0, 0.5)

    def vecs(i, s):
        return (npre[i, s].reshape(1, D_MODEL), npost[i, s].reshape(1, D_MODEL),
                mod[i, 3 * s], mod[i, 3 * s + 1], mod[i, 3 * s + 2])

    order = [(i, s) for i in range(2) for s in range(3)]
    saved = {}
    xs = x0
    for pos, (i, s) in enumerate(order):
        tag = f"l{i}s{s}"
        pre_g, post_g, shift, scale, gate = vecs(i, s)
        rider = _Exchange("gather", riders[tag]) if tag in riders else None
        if s != 1:
            if pos == 0:
                (pq, a, h), got = _ffn_in(xs, w_fi[0, 0], "ffn_in_" + tag, opening=(pre_g, scale, shift), exchange=rider)
            else:
                (pq, a), got = _ffn_in(h, w_fi[i, s // 2], "ffn_in_" + tag, exchange=rider)
            extra = (pq, a)
            if tag == "l0s0":
                w_fo[0, 0], w_hi, w_ho, w_fo[0, 1] = map(whole, got)
            elif tag == "l0s2":
                w_fi[1, 0], w_fo[1, 0] = map(whole, got)
            elif tag == "l1s0":
                w_fi[1, 1], w_fo[1, 1] = map(whole, got)
            wo = w_fo[i, s // 2]
        elif i == 0:
            p_qig, p_f, *got = _mm_blocks(h, w_hi, "hg_in", chunk=1024, f32_cols=(D_MODEL, 2 * D_MODEL), exchange=rider)
            proj = (p_qig, p_f)
            w_fi[0, 1], = map(whole, got)
            o, og, states, got = _hgrn_fwd(proj, lb0, on, "hg_mix", exchange=_Exchange("gather", riders["hg_mix"]))
            w_mi, w_mo = map(whole, got)
            a, wo = og, w_ho
            extra = (proj, o, og, states)
        else:
            pre, gp = _mm_blocks(h, w_mi, "gm_in", bias=b_in, gelu=True)
            a = _gm_spatial_fwd(pre, ln_g, ln_b, ws, bsb, "gm_mix")
            wo = w_mo
            extra = (pre, gp, a)
        if pos + 1 < len(order):
            npre_g, _, nshift, nscale, _ = vecs(*order[pos + 1])
            y, x_next, h_next = _out_proj(a, wo, xs, post_g, gate, res_ws[s], (npre_g, nscale, nshift), "out_" + tag)
            saved[tag] = (xs, h, y) + extra
            xs, h = x_next, h_next
        else:
            dx, dy, dgate, dpost, loss_part = _out_proj_last(a, wo, xs, post_g, gate, res_ws[s], target, "out_" + tag)
            saved[tag] = (xs, h, None) + extra
    loss = lax.psum(loss_part[0, 0], ("x", "y", "c"))

    slots = {}
    d_npre = [[None] * 3, [None] * 3]
    d_npost = [[None] * 3, [None] * 3]
    d_mod = [[None] * 9, [None] * 9]
    for pos in reversed(range(len(order))):
        i, s = order[pos]
        tag = f"l{i}s{s}"
        pre_g, _, _, scale, _ = vecs(i, s)
        xin, h = saved[tag][:2]
        if s != 1:
            w_in, wo = w_fi[i, s // 2], w_fo[i, s // 2]
            pq, g = saved[tag][3:]
            dz = _ffn_dgate(dy, wo, pq, "ffn_dgate_" + tag)
            g_out = _mm_wgrad(g, dy, "ffn_out_wgrad_" + tag, xw=1408)
            g_in = _mm_wgrad(dz, h, "ffn_in_wgrad_" + tag, xw=1408)
        elif i == 0:
            proj, o, og, states = saved[tag][3:]
            dog = _mm_blocks(dy, w_ho, "hg_out_dgrad")
            g_out = _mm_wgrad(og, dy, "hg_out_wgrad")
            dz, d_lb0, d_on = _hgrn_bwd(proj, o, dog, states, lb0, on, "hg_mix_bwd")
            w_in = w_hi
            g_in = _mm_wgrad(h, dz, "hg_in_wgrad", yw=1024, split=2)
        else:
            pre, gp, sp = saved[tag][3:]
            dm = _mm_blocks(dy, w_mo, "gm_out_dgrad")
            g_out = _mm_wgrad(sp, dy, "gm_out_wgrad", xw=768)
            dz, d_ws, d_bs, d_lg, d_lbias, d_bin = _gm_spatial_bwd(pre, gp, dm, ln_g, ln_b, ws, ws_t, bsb, "gm_mix_bwd")
            w_in = w_mi
            g_in = _mm_wgrad(h, dz, "gm_in_wgrad", yw=1536, split=2)
        g_out = g_out.reshape(NDEV, -1, D_MODEL)
        g_in = g_in.reshape(NDEV, -1, D_MODEL) if s != 1 else g_in
        d_npost[i][s] = dpost
        d_mod[i][3 * s + 2] = dgate
        rider = _Exchange("scatter", [g_in, g_out])
        if pos > 0:
            pi, ps = order[pos - 1]
            _, ppost_g, _, _, pgate = vecs(pi, ps)
            prev = (saved[f"l{pi}s{ps}"][2], ppost_g, pgate, res_ws[ps])
            dx, dshift, dscale, dpre_g, dy, dgate, dpost, r_in, r_out = _in_grad(
                dz, w_in, dx, xin, pre_g, scale, prev, "in_grad_" + tag, exchange=rider)
        else:
            dx, dshift, dscale, dpre_g, r_in, r_out = _in_grad(
                dz, w_in, dx, xin, pre_g, scale, None, "in_grad_" + tag, exchange=rider)
        slots[tag] = (r_in, r_out)
        d_npre[i][s] = dpre_g
        d_mod[i][3 * s], d_mod[i][3 * s + 1] = dshift, dscale
    grad_x = dx.reshape(x.shape)

    ffn_tags = ["l0s0", "l0s2", "l1s0", "l1s2"]
    s_fi = jnp.stack([slots[t][0] for t in ffn_tags], axis=1).swapaxes(-1, -2)
    s_fo = jnp.stack([slots[t][1] for t in ffn_tags], axis=1)
    (s_hi, s_ho), (s_mi, s_mo) = slots["l0s1"], slots["l1s1"]
    s_hi, s_ho, s_mi, s_mo = s_hi[:, None], s_ho[:, None], s_mi[:, None], s_mo[:, None]

    gmod = jnp.stack([jnp.concatenate(d_mod[i], axis=0) for i in range(2)])
    d_sm = lb0 * d_lb0
    d_hg_lb = jnp.concatenate([d_sm, jnp.zeros((2, D_MODEL), F32)], axis=0) - sm * d_sm
    small = [gmod, jnp.stack([jnp.concatenate(r, axis=0) for r in d_npre]),
             jnp.stack([jnp.concatenate(r, axis=0) for r in d_npost]),
             d_on, d_hg_lb, d_bin, d_lg, d_lbias, d_ws, d_bs[:, :, 0]]
    sizes = [a.size for a in small]
    flat = jnp.concatenate([a.reshape(-1) for a in small])
    rows = -(-flat.size // (8 * 128)) * 8
    flat = jnp.pad(flat, (0, rows * 128 - flat.size)).reshape(rows, 128)
    flats = _all_gather(flat, "gather_small_grads").reshape(NDEV, rows * 128)
    parts, off = [], 0
    for a, n in zip(small, sizes):
        parts.append(flats[:, off:off + n].reshape((NDEV,) + a.shape))
        off += n
    p_mod, p_npre, p_npost, p_on, p_lb, p_bin, p_lg, p_lbias, p_ws, p_bs = parts

    def mine(p, width):
        return lax.dynamic_slice_in_dim(p, me * width, width, axis=p.ndim - 1)

    gmod_cols = mine(p_mod.reshape(NDEV, 2, 9 * D_MODEL), n_ada).transpose(1, 0, 2)
    g_ada_w = _ada_bwd(jnp.pad(c_all.T, ((0, 0), (0, 120))), jnp.pad(gmod_cols, ((0, 0), (0, 120), (0, 0))), "ada_bwd")

    out = {}
    out["ada_w"] = _update(g_ada_w[None], ada_w, m_ada_w, v_ada_w, "adamw_ada_w")
    out["ada_b"] = _update(p_mod.reshape(NDEV, 2, 9 * D_MODEL), ada_b, m_ada_b, v_ada_b, "adamw_ada_b")
    out["norm_pre"] = _update(mine(p_npre, 128), norm_pre, m_norm_pre, v_norm_pre, "adamw_norm_pre")
    out["norm_post"] = _update(mine(p_npost, 128), norm_post, m_norm_post, v_norm_post, "adamw_norm_post")
    out["ffn_w_in"] = _update(s_fi.reshape((NDEV,) + ffn_w_in.shape), ffn_w_in, m_ffn_w_in, v_ffn_w_in, "adamw_ffn_in")
    out["ffn_w_out"] = _update(s_fo.reshape((NDEV,) + ffn_w_out.shape), ffn_w_out, m_ffn_w_out, v_ffn_w_out, "adamw_ffn_out")
    out["hg_w_in"] = _update(s_hi, hg_w_in, m_hg_w_in, v_hg_w_in, "adamw_hg_in")
    out["hg_w_out"] = _update(s_ho, hg_w_out, m_hg_w_out, v_hg_w_out, "adamw_hg_out")
    out["hg_out_norm"] = _update(p_on, hg_out_norm, m_hg_out_norm, v_hg_out_norm, "adamw_hg_norm")
    out["hg_lb"] = _update(p_lb, hg_lb, m_hg_lb, v_hg_lb, "adamw_hg_lb")
    out["gm_w_in"] = _update(s_mi, gm_w_in, m_gm_w_in, v_gm_w_in, "adamw_gm_in")
    out["gm_b_in"] = _update(mine(p_bin.reshape(NDEV, 1, 8 * 768), 768), gm_b_in, m_gm_b_in, v_gm_b_in, "adamw_gm_b_in")
    out["gm_ln_g"] = _update(mine(p_lg.reshape(NDEV, 1, 4 * 768), 384), gm_ln_g, m_gm_ln_g, v_gm_ln_g, "adamw_gm_ln_g")
    out["gm_ln_b"] = _update(mine(p_lbias.reshape(NDEV, 1, 4 * 768), 384), gm_ln_b, m_gm_ln_b, v_gm_ln_b, "adamw_gm_ln_b")
    out["gm_w_s"] = _update(p_ws[:, None], gm_w_s, m_gm_w_s, v_gm_w_s, "adamw_gm_w_s")
    out["gm_b_s"] = _update(p_bs[:, None], gm_b_s, m_gm_b_s, v_gm_b_s, "adamw_gm_b_s")
    out["gm_w_out"] = _update(s_mo, gm_w_out, m_gm_w_out, v_gm_w_out, "adamw_gm_out")

    names = ["ada_w", "ada_b", "norm_pre", "norm_post", "ffn_w_in", "ffn_w_out", "hg_w_in", "hg_w_out",
             "hg_out_norm", "hg_lb", "gm_w_in", "gm_b_in", "gm_ln_g", "gm_ln_b", "gm_w_s", "gm_b_s", "gm_w_out"]
    return (loss, grad_x, *[out[n][0] for n in names], *[out[n][1] for n in names],
            *[out[n][2] for n in names], *[out[n][3] for n in names])
```

```python
import math

import jax
import jax.numpy as jnp
from jax import lax
from jax.experimental import pallas as pl
from jax.experimental.pallas import tpu as pltpu

F32 = jnp.float32
BF16 = jnp.bfloat16
NDEV = 8
D_MODEL = 1024
EPS = 1e-6
HG_CHUNK = 64
HG_HEAD = 128
HG_HEADS = 8
GM_CHUNK = 128
GM_GDIM = 384
GM_GROUPS = 8
ADAM_LR = 0.001
ADAM_B1 = 0.9
ADAM_B2 = 0.999
ADAM_EPS = 1e-08
ADAM_WD = 0.01
ADAM_STEP = 10
VMEM_LIMIT = 56 * 2 ** 20

NN = (((1,), (0,)), ((), ()))
NT = (((1,), (1,)), ((), ()))
TN = (((0,), (0,)), ((), ()))
MESH = pl.DeviceIdType.MESH
ANY = pl.BlockSpec(memory_space=pl.ANY)


def _dot(a, b, dims=NN, precision=None):
    return lax.dot_general(a, b, dims, preferred_element_type=F32, precision=precision)


def _params(*sem):
    return pltpu.CompilerParams(dimension_semantics=sem, vmem_limit_bytes=VMEM_LIMIT)


def _sigmoid(x):
    return 1.0 / (1.0 + jnp.exp(-x))


def _sigmoid_t(x):
    return 0.5 * jnp.tanh(0.5 * x) + 0.5


def _gelu_and_grad(x):
    c = math.sqrt(2.0 / math.pi)
    m = (c * 0.044715) * (x * x)
    t = jnp.tanh(x * (c + m))
    hp = 0.5 + 0.5 * t
    return x * hp, hp * (1.0 + (x * (1.0 - t)) * (c + 3.0 * m))


def _colsum(x):
    return jnp.sum(x, axis=0, keepdims=True)


def _rowmean(x):
    return jnp.mean(x, axis=-1, keepdims=True)


def _all_gather(shard, name):
    def body(x_ref, out_ref, send_sems, recv_sems, local_sem):
        x, y, c = lax.axis_index("x"), lax.axis_index("y"), lax.axis_index("c")
        me, sibling = (x, y, c), (x, y, 1 - c)
        chips = [(1 - x, y), (x, 1 - y), (1 - x, 1 - y)]

        def slot(p):
            return out_ref.at[4 * p[0] + 2 * p[1] + p[2]]

        def copy(k, block, to, src=None):
            return pltpu.make_async_remote_copy(
                src_ref=slot(block) if src is None else src, dst_ref=slot(block),
                send_sem=send_sems.at[k], recv_sem=recv_sems.at[k],
                device_id=to, device_id_type=MESH)

        mine = pltpu.make_async_copy(x_ref, slot(me), local_sem)
        mine.start()
        first = [copy(0, me, sibling, src=x_ref)]
        first += [copy(1 + j, me, (*chip, c), src=x_ref) for j, chip in enumerate(chips)]
        for cp in first:
            cp.start()
        passed = [copy(4 + j, (*chip, c), sibling) for j, chip in enumerate(chips)]
        for j, chip in enumerate(chips):
            copy(1 + j, (*chip, c), me).wait_recv()
            passed[j].start()
        copy(0, sibling, me).wait_recv()
        for j, chip in enumerate(chips):
            copy(4 + j, (*chip, 1 - c), me).wait_recv()
        for cp in first + passed:
            cp.wait_send()
        mine.wait()

    return pl.pallas_call(
        body, name=name,
        out_shape=jax.ShapeDtypeStruct((NDEV,) + shard.shape, shard.dtype),
        in_specs=[ANY], out_specs=ANY,
        scratch_shapes=[pltpu.SemaphoreType.DMA((7,)), pltpu.SemaphoreType.DMA((7,)),
                        pltpu.SemaphoreType.DMA(())],
    )(shard)


class _Exchange:
    def __init__(self, kind, arrays):
        self.gather = kind == "gather"
        self.arrays = list(arrays)
        self.n = n = len(self.arrays)
        self.out_shape = [jax.ShapeDtypeStruct(((NDEV,) + a.shape) if self.gather else a.shape, a.dtype)
                          for a in self.arrays]
        self.scratch = [pltpu.SemaphoreType.DMA((n, NDEV - 1)), pltpu.SemaphoreType.DMA((n, NDEV - 1)),
                        pltpu.SemaphoreType.DMA((n,))]

    def _copies(self, in_refs, out_refs, sems):
        send_sems, recv_sems, local_sems = sems
        x, y, c = lax.axis_index("x"), lax.axis_index("y"), lax.axis_index("c")
        me = 4 * x + 2 * y + c
        peers = [(1 - x if k & 4 else x, 1 - y if k & 2 else y, 1 - c if k & 1 else c) for k in range(1, NDEV)]
        local, send, recv = [], [], []
        for a in range(self.n):
            src = (lambda pid, a=a: in_refs[a]) if self.gather else (lambda pid, a=a: in_refs[a].at[pid])
            local.append(pltpu.make_async_copy(src(me), out_refs[a].at[me], local_sems.at[a]))
            for k, p in enumerate(peers):
                pid = 4 * p[0] + 2 * p[1] + p[2]
                for lst, slot in ((send, me), (recv, pid)):
                    lst.append(pltpu.make_async_remote_copy(
                        src_ref=src(pid), dst_ref=out_refs[a].at[slot],
                        send_sem=send_sems.at[a, k], recv_sem=recv_sems.at[a, k],
                        device_id=p, device_id_type=MESH))
        return local, send, recv

    def start(self, first, in_refs, out_refs, sems):
        @pl.when(first)
        def _():
            local, send, _ = self._copies(in_refs, out_refs, sems)
            for cp in local + send:
                cp.start()

    def finish(self, last, in_refs, out_refs, sems):
        @pl.when(last)
        def _():
            local, send, recv = self._copies(in_refs, out_refs, sems)
            for cp in send:
                cp.wait_send()
            for cp in recv:
                cp.wait_recv()
            for cp in local:
                cp.wait()


def _host(exchange, n_in, n_out, body, first_last):
    if exchange is None:
        return body, [], [], [], []
    n = exchange.n

    def hosted(*refs):
        ins, refs = refs[:n_in], refs[n_in:]
        xin, refs = refs[:n], refs[n:]
        outs, refs = refs[:n_out], refs[n_out:]
        xout, refs = refs[:n], refs[n:]
        scratch, sems = refs[:len(refs) - 3], refs[len(refs) - 3:]
        first, last = first_last()
        exchange.start(first, xin, xout, sems)
        body(*ins, *outs, *scratch)
        exchange.finish(last, xin, xout, sems)

    return hosted, exchange.arrays, [ANY] * n, exchange.out_shape, exchange.scratch


def _first_last(steps):
    def at():
        i = pl.program_id(0)
        return i == 0, i == steps - 1
    return at


class _ColBlocks:
    def __init__(self, ref, width):
        self.ref, self.width = ref, width

    def _index(self, key):
        key = key if isinstance(key, tuple) else (key,)
        rows = key[1] if len(key) > 1 else slice(None)
        cols = key[2] if len(key) > 2 else slice(0, self.width)
        c0 = key[0] * self.width
        return rows, slice(c0 + cols.start, c0 + cols.stop)

    def __getitem__(self, key):
        return self.ref[self._index(key)]

    def __setitem__(self, key, value):
        self.ref[self._index(key)] = value


def _col_chunks(width, chunk=768):
    return [slice(c, min(c + chunk, width)) for c in range(0, width, chunk)]


def _row_spec(tm, d):
    return pl.BlockSpec((tm, d), lambda m: (m, 0))


def _vec_spec(d):
    return pl.BlockSpec((1, d), lambda m: (0, 0))


def _whole_spec(w):
    nd = w.ndim
    return pl.BlockSpec(w.shape, lambda m: (0,) * nd, pipeline_mode=pl.Buffered(1))


def _mm_blocks(a, w, name, *, bias=None, out_dtype=BF16, tm=512, chunk=768, gelu=False, f32_cols=None,
               exchange=None):
    T, K = a.shape
    C = w.shape[0]
    tm = min(tm, T)
    n_in = 2 + (bias is not None)

    def body(*refs):
        a_ref, w_ref = refs[:2]
        av = a_ref[...]
        for cols in _col_chunks(C, chunk):
            r = _dot(av, w_ref[cols], NT)
            if bias is not None:
                r = r + refs[2][:, cols]
            if gelu:
                z, dz = _gelu_and_grad(r)
                refs[n_in][:, cols] = z.astype(BF16)
                refs[n_in + 1][:, cols] = dz.astype(BF16)
            elif f32_cols is None:
                refs[n_in][:, cols] = r.astype(out_dtype)
            elif f32_cols[0] <= cols.start < f32_cols[1]:
                refs[n_in + 1][:, cols.start - f32_cols[0]:cols.stop - f32_cols[0]] = r
            else:
                shift = 0 if cols.start < f32_cols[0] else f32_cols[1] - f32_cols[0]
                refs[n_in][:, cols.start - shift:cols.stop - shift] = r.astype(BF16)

    in_specs = [_row_spec(tm, K), _whole_spec(w)]
    args = [a, w]
    if bias is not None:
        in_specs.append(_whole_spec(bias))
        args.append(bias)
    if gelu:
        outs = [jax.ShapeDtypeStruct((T, C), BF16)] * 2
    elif f32_cols is not None:
        wide = f32_cols[1] - f32_cols[0]
        assert f32_cols[0] % chunk == 0 and wide % chunk == 0
        outs = [jax.ShapeDtypeStruct((T, C - wide), BF16), jax.ShapeDtypeStruct((T, wide), F32)]
    else:
        outs = [jax.ShapeDtypeStruct((T, C), out_dtype)]
    body, x_args, x_in, x_out, x_scratch = _host(exchange, n_in, len(outs), body, _first_last(T // tm))
    res = pl.pallas_call(
        body, name=name, grid=(T // tm,),
        out_shape=outs + x_out,
        in_specs=in_specs + x_in,
        out_specs=[_row_spec(tm, o.shape[1]) for o in outs] + x_in,
        scratch_shapes=x_scratch,
        compiler_params=_params("arbitrary" if exchange else "parallel"),
    )(*args, *x_args)
    return res if len(res) > 1 else res[0]


ROW_CHUNK = 64


def _rms(v):
    return lax.rsqrt(_rowmean(v * v) + EPS)


def _zero_at_start(*refs):
    @pl.when(pl.program_id(0) == 0)
    def _():
        for r in refs:
            r[...] = jnp.zeros_like(r)


def _postnorm_bwd_math(dxo, yv, g, gate, res_w, dgate_ref, dpost_ref):
    r = _rms(yv)
    yh = yv * r
    both = res_w * _colsum(dxo * yh)
    dgate_ref[...] += g * both
    dpost_ref[...] += gate * both
    dyh = dxo * (res_w * gate * g)
    return (r * (dyh - yh * _rowmean(dyh * yh))).astype(BF16)


def _out_proj(a, w, x, post_g, gate, res_w, nxt, name, *, tm=512):
    T, ka = a.shape
    d = w.shape[1]
    tm = min(tm, T)

    def body(a_ref, w_ref, x_ref, pg_ref, gate_ref, ng_ref, nsc_ref, nsh_ref, y_ref, xn_ref, h_ref):
        y = _dot(a_ref[...], w_ref[...])
        y_ref[...] = y
        xn = x_ref[...] + (y * _rms(y)) * (res_w * gate_ref[...] * pg_ref[...])
        xn_ref[...] = xn
        h_ref[...] = ((xn * _rms(xn)) * (ng_ref[...] * (1.0 + nsc_ref[...])) + nsh_ref[...]).astype(BF16)

    return pl.pallas_call(
        body, name=name, grid=(T // tm,),
        out_shape=(jax.ShapeDtypeStruct((T, d), F32), jax.ShapeDtypeStruct((T, d), F32),
                   jax.ShapeDtypeStruct((T, d), BF16)),
        in_specs=[_row_spec(tm, ka), _whole_spec(w), _row_spec(tm, d)] + [_vec_spec(d)] * 5,
        out_specs=(_row_spec(tm, d),) * 3,
        compiler_params=_params("parallel"),
    )(a, w, x, post_g, gate, *nxt)


def _out_proj_last(a, w, x, post_g, gate, res_w, target, name, *, tm=512):
    T, ka = a.shape
    d = w.shape[1]
    tm = min(tm, T)

    def body(a_ref, w_ref, x_ref, pg_ref, gate_ref, t_ref, dx_ref, dy_ref, dgate_ref, dpost_ref, l_ref):
        _zero_at_start(dgate_ref, dpost_ref, l_ref)
        y = _dot(a_ref[...], w_ref[...])
        e = x_ref[...] + res_w * gate_ref[...] * (y * _rms(y) * pg_ref[...]) - t_ref[...]
        l_ref[...] += 0.5 * jnp.sum(_rowmean(e * e), axis=0, keepdims=True)
        dx = e * (1.0 / d)
        dx_ref[...] = dx
        dy_ref[...] = _postnorm_bwd_math(dx, y, pg_ref[...], gate_ref[...], res_w, dgate_ref, dpost_ref)

    return pl.pallas_call(
        body, name=name, grid=(T // tm,),
        out_shape=(jax.ShapeDtypeStruct((T, d), F32), jax.ShapeDtypeStruct((T, d), BF16),
                   jax.ShapeDtypeStruct((1, d), F32), jax.ShapeDtypeStruct((1, d), F32),
                   jax.ShapeDtypeStruct((1, 128), F32)),
        in_specs=[_row_spec(tm, ka), _whole_spec(w), _row_spec(tm, d), _vec_spec(d), _vec_spec(d), _row_spec(tm, d)],
        out_specs=(_row_spec(tm, d), _row_spec(tm, d), _vec_spec(d), _vec_spec(d),
                   pl.BlockSpec((1, 128), lambda m: (0, 0))),
        compiler_params=_params("arbitrary"),
    )(a, w, x, post_g, gate, target)


def _in_grad(dz, w, dxo, x, pre_g, scale, prev, name, *, tm=512, exchange=None):
    T, C = dz.shape
    d = w.shape[1]
    tm = min(tm, T)
    has_prev = prev is not None
    res_w = prev[3] if has_prev else None

    def body(*refs):
        dz_ref, w_ref, dxo_ref, x_ref, g_ref, sc_ref = refs[:6]
        dh_scr = refs[-1]
        if has_prev:
            yp_ref, ppg_ref, pgate_ref, dx_ref, dsh_ref, dsc_ref, dg_ref, dyp_ref, dgate_ref, dpost_ref = refs[6:-1]
            _zero_at_start(dsh_ref, dsc_ref, dg_ref, dgate_ref, dpost_ref)
        else:
            dx_ref, dsh_ref, dsc_ref, dg_ref = refs[6:-1]
            _zero_at_start(dsh_ref, dsc_ref, dg_ref)
        dh_scr[...] = _dot(dz_ref[...], w_ref[...])
        gain = 1.0 + sc_ref[...]
        for r0 in range(0, tm, ROW_CHUNK):
            rows = slice(r0, min(r0 + ROW_CHUNK, tm))
            dh = dh_scr[rows]
            xv = x_ref[rows]
            r = _rms(xv)
            xh = xv * r
            both = _colsum(dh * xh)
            dsh_ref[...] += _colsum(dh)
            dsc_ref[...] += g_ref[...] * both
            dg_ref[...] += gain * both
            dxh = dh * (gain * g_ref[...])
            dx = dxo_ref[rows] + r * (dxh - xh * _rowmean(dxh * xh))
            dx_ref[rows] = dx
            if has_prev:
                dyp_ref[rows] = _postnorm_bwd_math(dx, yp_ref[rows], ppg_ref[...], pgate_ref[...], res_w,
                                                   dgate_ref, dpost_ref)

    vec = jax.ShapeDtypeStruct((1, d), F32)
    in_specs = [_row_spec(tm, C), _whole_spec(w), _row_spec(tm, d), _row_spec(tm, d), _vec_spec(d), _vec_spec(d)]
    out_shape = [jax.ShapeDtypeStruct((T, d), F32), vec, vec, vec]
    out_specs = [_row_spec(tm, d), _vec_spec(d), _vec_spec(d), _vec_spec(d)]
    args = [dz, w, dxo, x, pre_g, scale]
    if has_prev:
        in_specs += [_row_spec(tm, d), _vec_spec(d), _vec_spec(d)]
        out_shape += [jax.ShapeDtypeStruct((T, d), BF16), vec, vec]
        out_specs += [_row_spec(tm, d), _vec_spec(d), _vec_spec(d)]
        args += list(prev[:3])
    body, x_args, x_in, x_out, x_scratch = _host(exchange, len(args), len(out_shape), body, _first_last(T // tm))
    return pl.pallas_call(
        body, name=name, grid=(T // tm,),
        out_shape=out_shape + x_out, in_specs=in_specs + x_in, out_specs=out_specs + x_in,
        scratch_shapes=[pltpu.VMEM((tm, d), F32)] + x_scratch,
        compiler_params=_params("arbitrary"),
    )(*args, *x_args)


def _mm_wgrad(x, y, name, *, xw=None, yw=None, split=1, tt=2048):
    T, P = x.shape
    Q = y.shape[1]
    xw, yw = xw or P, yw or Q
    jx, jy = P // xw, Q // yw
    assert jx == 1 or jy == 1
    tt = min(tt, T)
    nt = T // tt
    part = yw // split

    def body(x_ref, y_ref, o_ref, acc_ref):
        t = pl.program_id(1)

        @pl.when(t == 0)
        def _():
            acc_ref[...] = jnp.zeros_like(acc_ref)

        acc_ref[...] += _dot(x_ref[...], y_ref[...], TN)

        @pl.when(t == nt - 1)
        def _():
            if jy > 1:
                for k in range(split):
                    o_ref[k] = acc_ref[:, part * k:part * (k + 1)].astype(BF16)
            else:
                o_ref[...] = acc_ref[...].astype(BF16)

    if jy > 1:
        out_shape = jax.ShapeDtypeStruct((jy * split, P, part), BF16)
        out_spec = pl.BlockSpec((split, P, part), lambda j, t: (j, 0, 0))
    else:
        out_shape = jax.ShapeDtypeStruct((P, Q), BF16)
        out_spec = pl.BlockSpec((xw, Q), lambda j, t: (j, 0))
    return pl.pallas_call(
        body, name=name, grid=(max(jx, jy), nt),
        out_shape=out_shape,
        in_specs=[pl.BlockSpec((tt, xw), (lambda j, t: (t, j)) if jx > 1 else (lambda j, t: (t, 0))),
                  pl.BlockSpec((tt, yw), (lambda j, t: (t, j)) if jy > 1 else (lambda j, t: (t, 0)))],
        out_specs=out_spec,
        scratch_shapes=[pltpu.VMEM((xw, yw), F32)],
        compiler_params=_params("parallel", "arbitrary"),
    )(x, y)


def _ffn_in(h, wt, name, *, tm=512, opening=None, exchange=None):
    T, K = h.shape
    F = wt.shape[0] // 2
    tm = min(tm, T)
    n_vec = 3 if opening else 0

    def body(*refs):
        h_ref, w_ref = refs[0], refs[1 + n_vec]
        pq_ref, g_ref = refs[2 + n_vec], refs[3 + n_vec]
        hh = h_ref[...]
        if opening:
            g_vec, sc_vec, sh_vec = refs[1:4]
            hh = (hh * _rms(hh) * (g_vec[...] * (1.0 + sc_vec[...])) + sh_vec[...]).astype(BF16)
            refs[4 + n_vec][...] = hh
        for cols in _col_chunks(F):
            hi = slice(F + cols.start, F + cols.stop)
            a = _dot(hh, w_ref[cols], NT)
            b = _dot(hh, w_ref[hi], NT)
            s = _sigmoid_t(a)
            silu = a * s
            pq_ref[:, cols] = (b * (s * (1.0 + a * (1.0 - s)))).astype(BF16)
            pq_ref[:, hi] = silu.astype(BF16)
            g_ref[:, cols] = (silu * b).astype(BF16)

    outs = [jax.ShapeDtypeStruct((T, 2 * F), BF16), jax.ShapeDtypeStruct((T, F), BF16)]
    out_specs = [_row_spec(tm, 2 * F), _row_spec(tm, F)]
    if opening:
        outs.append(jax.ShapeDtypeStruct((T, K), BF16))
        out_specs.append(_row_spec(tm, K))
    vecs = list(opening) if opening else []
    body, x_args, x_in, x_out, x_scratch = _host(exchange, 2 + n_vec, len(outs), body, _first_last(T // tm))
    res = pl.pallas_call(
        body, name=name, grid=(T // tm,),
        out_shape=outs + x_out,
        in_specs=[_row_spec(tm, K)] + [_vec_spec(K)] * n_vec + [_whole_spec(wt)] + x_in,
        out_specs=out_specs + x_in,
        scratch_shapes=x_scratch,
        compiler_params=_params("arbitrary" if exchange else "parallel"),
    )(h, *vecs, wt, *x_args)
    return res[:len(outs)], res[len(outs):]


def _ffn_dgate(dy, w_out, pq, name, *, tm=512):
    T, N = dy.shape
    F = w_out.shape[0]
    tm = min(tm, T)

    def body(dy_ref, w_ref, pq_ref, dz_ref):
        dyv = dy_ref[...]
        for cols in _col_chunks(F):
            hi = slice(F + cols.start, F + cols.stop)
            dg = _dot(dyv, w_ref[cols], NT)
            dz_ref[:, cols] = (dg * pq_ref[:, cols].astype(F32)).astype(BF16)
            dz_ref[:, hi] = (dg * pq_ref[:, hi].astype(F32)).astype(BF16)

    return pl.pallas_call(
        body, name=name, grid=(T // tm,),
        out_shape=jax.ShapeDtypeStruct((T, 2 * F), BF16),
        in_specs=[_row_spec(tm, N), _whole_spec(w_out), _row_spec(tm, 2 * F)],
        out_specs=_row_spec(tm, 2 * F),
        compiler_params=_params("parallel"),
    )(dy, w_out, pq)


def _ada_fwd(c_all, w, b, name):
    L, K, n = w.shape

    def body(c_ref, w_ref, b_ref, o_ref):
        cv = c_ref[...]
        cond = cv * _sigmoid(cv)
        for l in range(L):
            o_ref[l] = _dot(cond, w_ref[l], precision=lax.Precision.HIGHEST) + b_ref[l]

    return pl.pallas_call(
        body, name=name,
        out_shape=jax.ShapeDtypeStruct((L, NDEV, n), F32),
        compiler_params=pltpu.CompilerParams(vmem_limit_bytes=VMEM_LIMIT),
    )(c_all, w, b)


def _ada_bwd(c_all_t, gmod, name):
    L, _, n = gmod.shape
    K = c_all_t.shape[0]

    def body(c_ref, g_ref, o_ref):
        cv = c_ref[...]
        cond = cv * _sigmoid(cv)
        for l in range(L):
            o_ref[l] = _dot(cond, g_ref[l], precision=lax.Precision.HIGHEST)

    return pl.pallas_call(
        body, name=name,
        out_shape=jax.ShapeDtypeStruct((L, K, n), F32),
        compiler_params=pltpu.CompilerParams(vmem_limit_bytes=VMEM_LIMIT),
    )(c_all_t, gmod)


def _tri(n, upper=False, block=None):
    r = lax.broadcasted_iota(jnp.int32, (n, n), 0)
    c = lax.broadcasted_iota(jnp.int32, (n, n), 1)
    m = (c >= r) if upper else (c <= r)
    if block is not None:
        m = m & ((r // block) == (c // block))
    return m.astype(BF16)


TRI_ROWS = 128


def _tri_dot(tri, x):
    hi = x.astype(BF16)
    lo = (x - hi.astype(F32)).astype(BF16)
    rows = x.shape[0]
    step = min(TRI_ROWS, rows)
    parts = [_dot(tri, hi[r:r + step]) + _dot(tri, lo[r:r + step]) for r in range(0, rows, step)]
    return parts[0] if len(parts) == 1 else jnp.concatenate(parts, axis=0)


def _hgrn_gates(proj_ref, f_ref, lb_ref, jh):
    lb = lb_ref[:, 512 * jh:512 * (jh + 1)]
    qp = _ColBlocks(proj_ref, 512)[jh].astype(F32)
    fx = _ColBlocks(f_ref, 512)[jh]
    sq = _sigmoid_t(qp)
    sig = _sigmoid_t(fx)
    f = lb + (1.0 - lb) * sig
    k = (1.0 - lb) * (1.0 - sig)
    return lb, qp, sq, sig, f, k


def _hgrn_fwd(proj, lb, out_norm, name, *, tb=512, exchange=None):
    T = proj[0].shape[0]
    tb = min(tb, T)
    nc = tb // HG_CHUNK
    lmat = _tri(min(TRI_ROWS, tb), block=HG_CHUNK)

    def body(proj_ref, f_ref, lb_ref, on_ref, l_ref, o_ref, og_ref, st_ref, s_scr, b_scr):
        @pl.when(pl.program_id(0) == 0)
        def _():
            s_scr[...] = jnp.zeros_like(s_scr)

        r_i = lax.broadcasted_iota(jnp.int32, (HG_CHUNK, HG_CHUNK), 0)
        c_i = lax.broadcasted_iota(jnp.int32, (HG_CHUNK, HG_CHUNK), 1)
        causal = c_i <= r_i
        onv = on_ref[...]
        blocks = _ColBlocks(proj_ref, 512)
        for jh in range(2):
            lbv, qp, sq, sig, f, k = _hgrn_gates(proj_ref, f_ref, lb_ref, jh)
            q = qp * sq
            b_half = b_scr.at[jh]
            b_half[...] = _tri_dot(l_ref[...], jnp.log(f))
            v = blocks[2 + jh]
            gp = blocks[4 + jh].astype(F32)
            gs = gp * _sigmoid_t(gp)
            for hh in range(4):
                hd = 4 * jh + hh
                cs = slice(HG_HEAD * hh, HG_HEAD * (hh + 1))
                for ci in range(nc):
                    r0 = HG_CHUNK * ci
                    rs = slice(r0, r0 + HG_CHUNK)
                    bc = b_half[rs, cs]
                    bm = b_half[r0 + HG_CHUNK // 2 - 1:r0 + HG_CHUNK // 2, cs]
                    bl = b_half[r0 + HG_CHUNK - 1:r0 + HG_CHUNK, cs]
                    qc, kc, vc = q[rs, cs], k[rs, cs], v[rs, cs].astype(BF16)
                    e_q, e_k = jnp.exp(bc - bm), jnp.exp(bm - bc)
                    qe = (qc * (e_q * jnp.exp(bm))).astype(BF16)
                    qt = (qc * e_q).astype(BF16)
                    kt = (kc * e_k).astype(BF16)
                    kd = (kc * (e_k * jnp.exp(bl - bm))).astype(BF16)
                    st = s_scr[hd]
                    stb = st.astype(BF16)
                    st_ref[ci, hd] = stb
                    a = jnp.where(causal, _dot(qt, kt, NT), 0.0).astype(BF16)
                    o = _dot(qe, stb, NT) + _dot(a, vc)
                    s_scr[hd] = st * jnp.exp(bl) + _dot(vc, kd, TN)
                    o_ref[rs, HG_HEAD * hd:HG_HEAD * (hd + 1)] = o
                    r = lax.rsqrt(_rowmean(o * o) + EPS)
                    og_ref[rs, HG_HEAD * hd:HG_HEAD * (hd + 1)] = (o * r * onv * gs[rs, cs]).astype(BF16)

    body, x_args, x_in, x_out, x_scratch = _host(exchange, 5, 3, body, _first_last(T // tb))
    res = pl.pallas_call(
        body, name=name, grid=(T // tb,),
        out_shape=[jax.ShapeDtypeStruct((T, D_MODEL), F32), jax.ShapeDtypeStruct((T, D_MODEL), BF16),
                   jax.ShapeDtypeStruct((T // HG_CHUNK, HG_HEADS, HG_HEAD, HG_HEAD), BF16)] + x_out,
        in_specs=[pl.BlockSpec((tb, 3 * D_MODEL), lambda i: (i, 0)),
                  pl.BlockSpec((tb, D_MODEL), lambda i: (i, 0)),
                  pl.BlockSpec((1, D_MODEL), lambda i: (0, 0)),
                  pl.BlockSpec((1, HG_HEAD), lambda i: (0, 0)),
                  pl.BlockSpec(lmat.shape, lambda i: (0, 0))] + x_in,
        out_specs=[pl.BlockSpec((tb, D_MODEL), lambda i: (i, 0)),
                   pl.BlockSpec((tb, D_MODEL), lambda i: (i, 0)),
                   pl.BlockSpec((nc, HG_HEADS, HG_HEAD, HG_HEAD), lambda i: (i, 0, 0, 0))] + x_in,
        scratch_shapes=[pltpu.VMEM((HG_HEADS, HG_HEAD, HG_HEAD), F32), pltpu.VMEM((2, tb, 512), F32)] + x_scratch,
        compiler_params=_params("arbitrary"),
    )(*proj, lb, out_norm, lmat, *x_args)
    return res[0], res[1], res[2], res[3:]


def _hgrn_bwd(proj, o, dog, states, lb, out_norm, name, *, tb=512):
    T = proj[0].shape[0]
    tb = min(tb, T)
    nc = tb // HG_CHUNK
    nb = T // tb
    lmat = _tri(min(TRI_ROWS, tb), block=HG_CHUNK)
    umat = _tri(min(TRI_ROWS, tb), upper=True, block=HG_CHUNK)

    def body(proj_ref, f_ref, o_ref, dog_ref, st_ref, lb_ref, on_ref, l_ref, u_ref,
             dproj_ref, dlb_ref, don_ref, ds_scr, *half_scr):
        @pl.when(pl.program_id(0) == 0)
        def _():
            ds_scr[...] = jnp.zeros_like(ds_scr)
            dlb_ref[...] = jnp.zeros_like(dlb_ref)
            don_ref[...] = jnp.zeros_like(don_ref)

        r_i = lax.broadcasted_iota(jnp.int32, (HG_CHUNK, HG_CHUNK), 0)
        c_i = lax.broadcasted_iota(jnp.int32, (HG_CHUNK, HG_CHUNK), 1)
        causal = c_i <= r_i
        causal_t = r_i <= c_i
        last_row = lax.broadcasted_iota(jnp.int32, (HG_CHUNK, HG_HEAD), 0) == HG_CHUNK - 1
        onv = on_ref[...]
        don_acc = jnp.zeros((1, HG_HEAD), F32)
        blocks = _ColBlocks(proj_ref, 512)
        dproj_ref = _ColBlocks(dproj_ref, 512)
        for jh in range(2):
            b_scr, dq_scr, dk_scr, dv_scr, dg_scr, db_scr = [s.at[jh] for s in half_scr]
            lbv, qp, sq, sig, f, k = _hgrn_gates(proj_ref, f_ref, lb_ref, jh)
            q = qp * sq
            b_scr[...] = _tri_dot(l_ref[...], jnp.log(f))
            v = blocks[2 + jh]
            gp = blocks[4 + jh].astype(F32)
            sg = _sigmoid_t(gp)
            for ci in reversed(range(nc)):
                r0 = HG_CHUNK * ci
                rs = slice(r0, r0 + HG_CHUNK)
                for hh in range(4):
                    hd = 4 * jh + hh
                    cs = slice(HG_HEAD * hh, HG_HEAD * (hh + 1))
                    hs = slice(HG_HEAD * hd, HG_HEAD * (hd + 1))
                    oc = o_ref[rs, hs]
                    r = lax.rsqrt(_rowmean(oc * oc) + EPS)
                    oh = oc * r
                    gc, sgc = gp[rs, cs], sg[rs, cs]
                    dogc = dog_ref[rs, hs].astype(F32)
                    don = dogc * (gc * sgc)
                    dg_scr[rs, cs] = dogc * (oh * onv) * (sgc * (1.0 + gc * (1.0 - sgc)))
                    don_acc += _colsum(don * oh)
                    donh = don * onv
                    do = (r * (donh - oh * _rowmean(donh * oh))).astype(BF16)
                    bc = b_scr[rs, cs]
                    bm = b_scr[r0 + HG_CHUNK // 2 - 1:r0 + HG_CHUNK // 2, cs]
                    bl = b_scr[r0 + HG_CHUNK - 1:r0 + HG_CHUNK, cs]
                    qc, kc, vc = q[rs, cs], k[rs, cs], v[rs, cs].astype(BF16)
                    e_q, e_k = jnp.exp(bc - bm), jnp.exp(bm - bc)
                    e_b, e_d = e_q * jnp.exp(bm), e_k * jnp.exp(bl - bm)
                    qe = (qc * e_b).astype(BF16)
                    qt = (qc * e_q).astype(BF16)
                    kt = (kc * e_k).astype(BF16)
                    kd = (kc * e_d).astype(BF16)
                    stb = st_ref[ci, hd]
                    dst = ds_scr[hd]
                    dstb = dst.astype(BF16)
                    a_t = jnp.where(causal_t, _dot(kt, qt, NT), 0.0).astype(BF16)
                    da = jnp.where(causal, _dot(do, vc, NT), 0.0).astype(BF16)
                    da_t = jnp.where(causal_t, _dot(vc, do, NT), 0.0).astype(BF16)
                    dv_scr[rs, cs] = _dot(a_t, do) + _dot(kd, dstb, NT)
                    dqe, dqt = _dot(do, stb), _dot(da, kt)
                    dkt, dkd = _dot(da_t, qt), _dot(vc, dstb)
                    dq_scr[rs, cs] = dqe * e_b + dqt * e_q
                    dk_scr[rs, cs] = dkt * e_k + dkd * e_d
                    e_l = jnp.exp(bl)
                    kd_dkd = kd.astype(F32) * dkd
                    dbc = qe.astype(F32) * dqe + qt.astype(F32) * dqt - kt.astype(F32) * dkt - kd_dkd
                    handed = e_l * _colsum(dstb.astype(F32) * stb.astype(F32)) + _colsum(kd_dkd)
                    db_scr[rs, cs] = dbc + jnp.where(last_row, handed, 0.0)
                    ds_scr[hd] = dst * e_l + _dot(do, qe, TN)
            dq = dq_scr[...]
            dk = dk_scr[...]
            cols = slice(512 * jh, 512 * (jh + 1))
            dlogf = _tri_dot(u_ref[...], db_scr[...])
            one_m_sig = 1.0 - sig
            dsig = (1.0 - lbv) * sig * one_m_sig
            dboth = dlogf / f - dk
            dproj_ref[jh] = (dq * (sq * (1.0 + qp * (1.0 - sq)))).astype(BF16)
            dproj_ref[2 + jh] = (dboth * dsig).astype(BF16)
            dproj_ref[4 + jh] = dv_scr[...].astype(BF16)
            dproj_ref[6 + jh] = dg_scr[...].astype(BF16)
            dlb_ref[:, cols] += _colsum(dboth * one_m_sig)
        don_ref[...] += don_acc

    rev = lambda i: nb - 1 - i
    return pl.pallas_call(
        body, name=name, grid=(nb,),
        out_shape=(jax.ShapeDtypeStruct((T, 4 * D_MODEL), BF16), jax.ShapeDtypeStruct((1, D_MODEL), F32),
                   jax.ShapeDtypeStruct((1, HG_HEAD), F32)),
        in_specs=[pl.BlockSpec((tb, 3 * D_MODEL), lambda i: (rev(i), 0)),
                  pl.BlockSpec((tb, D_MODEL), lambda i: (rev(i), 0)),
                  pl.BlockSpec((tb, D_MODEL), lambda i: (rev(i), 0)),
                  pl.BlockSpec((tb, D_MODEL), lambda i: (rev(i), 0)),
                  pl.BlockSpec((nc, HG_HEADS, HG_HEAD, HG_HEAD), lambda i: (rev(i), 0, 0, 0)),
                  pl.BlockSpec((1, D_MODEL), lambda i: (0, 0)),
                  pl.BlockSpec((1, HG_HEAD), lambda i: (0, 0)),
                  pl.BlockSpec(lmat.shape, lambda i: (0, 0)),
                  pl.BlockSpec(lmat.shape, lambda i: (0, 0))],
        out_specs=(pl.BlockSpec((tb, 4 * D_MODEL), lambda i: (rev(i), 0)),
                   pl.BlockSpec((1, D_MODEL), lambda i: (0, 0)),
                   pl.BlockSpec((1, HG_HEAD), lambda i: (0, 0))),
        scratch_shapes=[pltpu.VMEM((HG_HEADS, HG_HEAD, HG_HEAD), F32)] + [pltpu.VMEM((2, tb, 512), F32)] * 6,
        compiler_params=_params("arbitrary"),
    )(*proj, o, dog, states, lb, out_norm, lmat, umat)


def _gm_norm(pre_ref, lg_ref, lbias_ref):
    pre_ref = _ColBlocks(pre_ref, 768)
    vs = [pre_ref[4 + j].astype(F32) for j in range(4)]
    width = 4 * vs[0].shape[1]
    mu = sum(jnp.sum(v, axis=1, keepdims=True) for v in vs) / width
    ds = [v - mu for v in vs]
    var = sum(jnp.sum(d * d, axis=1, keepdims=True) for d in ds) / width
    rstd = lax.rsqrt(var + EPS)
    vhat = [d * rstd for d in ds]
    vn = [vhat[j] * lg_ref[j:j + 1, :] + lbias_ref[j:j + 1, :] for j in range(4)]
    return vhat, vn, rstd


def _gm_spatial_fwd(pre, ln_g, ln_b, ws, bsb, name, *, tb=512):
    T = pre.shape[0]
    tb = min(tb, T)
    nc = tb // GM_CHUNK

    def body(pre_ref, lg_ref, lbias_ref, ws_ref, bs_ref, o_ref):
        _, vn, _ = _gm_norm(pre_ref, lg_ref, lbias_ref)
        pre_ref, o_ref = _ColBlocks(pre_ref, 768), _ColBlocks(o_ref, 768)
        for j in range(4):
            u = pre_ref[j].astype(F32)
            for e in range(2):
                g = 2 * j + e
                cs = slice(GM_GDIM * e, GM_GDIM * (e + 1))
                wg = ws_ref[g].astype(BF16)
                for ci in range(nc):
                    rs = slice(GM_CHUNK * ci, GM_CHUNK * (ci + 1))
                    vm = _dot(wg, vn[j][rs, cs].astype(BF16)) + bs_ref[g]
                    o_ref[j, rs, cs] = (u[rs, cs] * vm).astype(BF16)

    return pl.pallas_call(
        body, name=name, grid=(T // tb,),
        out_shape=jax.ShapeDtypeStruct((T, 4 * 768), BF16),
        in_specs=[pl.BlockSpec((tb, 8 * 768), lambda i: (i, 0)),
                  pl.BlockSpec((4, 768), lambda i: (0, 0)),
                  pl.BlockSpec((4, 768), lambda i: (0, 0)),
                  pl.BlockSpec((GM_GROUPS, GM_CHUNK, GM_CHUNK), lambda i: (0, 0, 0)),
                  pl.BlockSpec((GM_GROUPS, GM_CHUNK, GM_GDIM), lambda i: (0, 0, 0))],
        out_specs=pl.BlockSpec((tb, 4 * 768), lambda i: (i, 0)),
        compiler_params=_params("parallel"),
    )(pre, ln_g, ln_b, ws, bsb)


def _gm_spatial_bwd(pre, gp, dm, ln_g, ln_b, ws, ws_t, bsb, name, *, tb=256):
    T = pre.shape[0]
    tb = min(tb, T)
    nc = tb // GM_CHUNK
    nb = T // tb

    def body(pre_ref, gp_ref, dm_ref, lg_ref, lbias_ref, ws_ref, wst_ref, bs_ref,
             dpre_ref, dws_ref, dbs_ref, dlg_ref, dlb_ref, dbin_ref, dbs_scr, dvn_scr, du_scr):
        i = pl.program_id(0)

        @pl.when(i == 0)
        def _():
            dws_ref[...] = jnp.zeros_like(dws_ref)
            dbs_scr[...] = jnp.zeros_like(dbs_scr)
            dlg_ref[...] = jnp.zeros_like(dlg_ref)
            dlb_ref[...] = jnp.zeros_like(dlb_ref)
            dbin_ref[...] = jnp.zeros_like(dbin_ref)

        vhat, vn, rstd = _gm_norm(pre_ref, lg_ref, lbias_ref)
        pre_ref, gp_ref, dm_ref = _ColBlocks(pre_ref, 768), _ColBlocks(gp_ref, 768), _ColBlocks(dm_ref, 768)
        dpre_ref = _ColBlocks(dpre_ref, 768)
        for j in range(4):
            u = pre_ref[j].astype(F32)
            for e in range(2):
                g = 2 * j + e
                cs = slice(GM_GDIM * e, GM_GDIM * (e + 1))
                wg = ws_ref[g].astype(BF16)
                wgt = wst_ref[g].astype(BF16)
                for ci in range(nc):
                    rs = slice(GM_CHUNK * ci, GM_CHUNK * (ci + 1))
                    vnb = vn[j][rs, cs].astype(BF16)
                    vm = _dot(wg, vnb) + bs_ref[g]
                    dmg = dm_ref[j, rs, cs].astype(F32)
                    du_scr[j, rs, cs] = dmg * vm
                    dvm = dmg * u[rs, cs]
                    dvmb = dvm.astype(BF16)
                    dws_ref[g] += _dot(dvmb, vnb, NT)
                    dbs_scr[g] += dvm
                    dvn_scr[j, rs, cs] = _dot(wgt, dvmb)
        width = 4 * 768
        dvh = []
        for j in range(4):
            dvn = dvn_scr[j]
            dlg_ref[j:j + 1, :] += _colsum(dvn * vhat[j])
            dlb_ref[j:j + 1, :] += _colsum(dvn)
            dvh.append(dvn * lg_ref[j:j + 1, :])
        m1 = sum(jnp.sum(d, axis=1, keepdims=True) for d in dvh) / width
        m2 = sum(jnp.sum(dvh[j] * vhat[j], axis=1, keepdims=True) for j in range(4)) / width
        for j in range(4):
            dv = rstd * (dvh[j] - m1 - vhat[j] * m2)
            dpv = dv * gp_ref[4 + j].astype(F32)
            dpu = du_scr[j] * gp_ref[j].astype(F32)
            dpre_ref[4 + j] = dpv.astype(BF16)
            dpre_ref[j] = dpu.astype(BF16)
            dbin_ref[4 + j:5 + j, :] += _colsum(dpv)
            dbin_ref[j:j + 1, :] += _colsum(dpu)

        @pl.when(i == nb - 1)
        def _():
            r_i = lax.broadcasted_iota(jnp.int32, (GM_CHUNK, GM_CHUNK), 0)
            c_i = lax.broadcasted_iota(jnp.int32, (GM_CHUNK, GM_CHUNK), 1)
            for g in range(GM_GROUPS):
                dws_ref[g] = jnp.where(c_i <= r_i, dws_ref[g], 0.0)
                dbs_ref[g] = jnp.broadcast_to(jnp.sum(dbs_scr[g], axis=1, keepdims=True), (GM_CHUNK, GM_CHUNK))

    sq = pl.BlockSpec((GM_GROUPS, GM_CHUNK, GM_CHUNK), lambda i: (0, 0, 0))
    v4 = pl.BlockSpec((4, 768), lambda i: (0, 0))
    return pl.pallas_call(
        body, name=name, grid=(nb,),
        out_shape=(jax.ShapeDtypeStruct((T, 8 * 768), BF16),
                   jax.ShapeDtypeStruct((GM_GROUPS, GM_CHUNK, GM_CHUNK), F32),
                   jax.ShapeDtypeStruct((GM_GROUPS, GM_CHUNK, GM_CHUNK), F32),
                   jax.ShapeDtypeStruct((4, 768), F32), jax.ShapeDtypeStruct((4, 768), F32),
                   jax.ShapeDtypeStruct((8, 768), F32)),
        in_specs=[pl.BlockSpec((tb, 8 * 768), lambda i: (i, 0)),
                  pl.BlockSpec((tb, 8 * 768), lambda i: (i, 0)),
                  pl.BlockSpec((tb, 4 * 768), lambda i: (i, 0)),
                  v4, v4, sq, sq,
                  pl.BlockSpec((GM_GROUPS, GM_CHUNK, GM_GDIM), lambda i: (0, 0, 0))],
        out_specs=(pl.BlockSpec((tb, 8 * 768), lambda i: (i, 0)), sq, sq, v4, v4,
                   pl.BlockSpec((8, 768), lambda i: (0, 0))),
        scratch_shapes=[pltpu.VMEM((GM_GROUPS, GM_CHUNK, GM_GDIM), F32),
                        pltpu.VMEM((4, tb, 768), F32), pltpu.VMEM((4, tb, 768), F32)],
        compiler_params=_params("arbitrary"),
    )(pre, gp, dm, ln_g, ln_b, ws, ws_t, bsb)


def _adamw(slots, w, m, v, name, *, tr=256):
    S, R, C = slots.shape
    tr = next((t for t in (tr, tr // 2, tr // 4, tr // 8, tr // 16) if R % t == 0), R) if R > tr else R
    bc1 = 1.0 - ADAM_B1 ** ADAM_STEP
    bc2 = 1.0 - ADAM_B2 ** ADAM_STEP

    def body(s_ref, w_ref, m_ref, v_ref, g_ref, d_ref, nm_ref, nv_ref):
        g = s_ref[0].astype(F32)
        for s in range(1, S):
            g = g + s_ref[s].astype(F32)
        mn = ADAM_B1 * m_ref[...] + (1.0 - ADAM_B1) * g
        vn = ADAM_B2 * v_ref[...] + (1.0 - ADAM_B2) * (g * g)
        g_ref[...] = g
        nm_ref[...] = mn
        nv_ref[...] = vn
        d_ref[...] = -ADAM_LR * ((mn / bc1) / (jnp.sqrt(vn / bc2) + ADAM_EPS) + ADAM_WD * w_ref[...])

    spec = pl.BlockSpec((tr, C), lambda i: (i, 0))
    return pl.pallas_call(
        body, name=name, grid=(R // tr,),
        out_shape=(jax.ShapeDtypeStruct((R, C), F32),) * 4,
        in_specs=[pl.BlockSpec((S, tr, C), lambda i: (0, i, 0)), spec, spec, spec],
        out_specs=(spec,) * 4,
        compiler_params=_params("parallel"),
    )(slots, w, m, v)


def _update(slots, w, m, v, name):
    shp = w.shape
    C = shp[-1]
    R = math.prod(shp[:-1])
    outs = _adamw(slots.reshape(slots.shape[0], R, C), w.reshape(R, C), m.reshape(R, C), v.reshape(R, C), name)
    return tuple(o.reshape(shp) for o in outs)


def kernel(x, c, ada_w, ada_b, norm_pre, norm_post, ffn_w_in, ffn_w_out, hg_w_in, hg_w_out, hg_out_norm, hg_lb, gm_w_in, gm_b_in, gm_ln_g, gm_ln_b, gm_w_s, gm_b_s, gm_w_out, loss_target, m_ada_w, m_ada_b, m_norm_pre, m_norm_post, m_ffn_w_in, m_ffn_w_out, m_hg_w_in, m_hg_w_out, m_hg_out_norm, m_hg_lb, m_gm_w_in, m_gm_b_in, m_gm_ln_g, m_gm_ln_b, m_gm_w_s, m_gm_b_s, m_gm_w_out, v_ada_w, v_ada_b, v_norm_pre, v_norm_post, v_ffn_w_in, v_ffn_w_out, v_hg_w_in, v_hg_w_out, v_hg_out_norm, v_hg_lb, v_gm_w_in, v_gm_b_in, v_gm_ln_g, v_gm_ln_b, v_gm_w_s, v_gm_b_s, v_gm_w_out):
    me = 4 * lax.axis_index("x") + 2 * lax.axis_index("y") + lax.axis_index("c")
    T = x.shape[1]
    x0 = x.reshape(T, D_MODEL)
    target = loss_target.reshape(T, D_MODEL)
    n_ada = ada_w.shape[-1]

    pack = jnp.concatenate([
        c.reshape(8, 128), norm_pre.reshape(6, 128), norm_post.reshape(6, 128),
        gm_b_in.reshape(6, 128), gm_ln_g.reshape(3, 128), gm_ln_b.reshape(3, 128)], axis=0)
    packs = _all_gather(pack, "gather_small")
    c_all = packs[:, 0:8].reshape(NDEV, D_MODEL)
    npre = packs[:, 8:14].reshape(NDEV, 2, 3, 128).transpose(1, 2, 0, 3).reshape(2, 3, D_MODEL)
    npost = packs[:, 14:20].reshape(NDEV, 2, 3, 128).transpose(1, 2, 0, 3).reshape(2, 3, D_MODEL)
    b_in = packs[:, 20:26].reshape(1, NDEV * 768)
    ln_g = packs[:, 26:29].reshape(4, 768)
    ln_b = packs[:, 29:32].reshape(4, 768)

    ada_b_mine = lax.dynamic_slice_in_dim(ada_b, me * n_ada, n_ada, axis=1).reshape(2, 1, n_ada)
    mod_cols = _ada_fwd(c_all, ada_w, ada_b_mine, "ada_fwd")
    mod_all = _all_gather(mod_cols, "gather_mod")
    mod = lax.dynamic_index_in_dim(mod_all, me, axis=2, keepdims=False)
    mod = mod.transpose(1, 0, 2).reshape(2, 9, 1, D_MODEL)

    sh_fi, sh_fo = ffn_w_in.astype(BF16).swapaxes(-1, -2), ffn_w_out.astype(BF16)
    sh_hi, sh_ho = hg_w_in[0].astype(BF16).T, hg_w_out[0].astype(BF16)
    sh_mi, sh_mo = gm_w_in[0].astype(BF16).T, gm_w_out[0].astype(BF16)

    def whole(gathered):
        return gathered.reshape(-1, D_MODEL)

    w_fi = {(0, 0): whole(_all_gather(sh_fi[0, 0], "gather_ffn_in_first"))}
    w_fo = {}
    riders = {"l0s0": [sh_fo[0, 0], sh_hi, sh_ho, sh_fo[0, 1]], "l0s1": [sh_fi[0, 1]], "hg_mix": [sh_mi, sh_mo],
              "l0s2": [sh_fi[1, 0], sh_fo[1, 0]], "l1s0": [sh_fi[1, 1], sh_fo[1, 1]]}

    sm = jax.nn.softmax(hg_lb, axis=0)
    lb0 = sm[0:1]
    on = hg_out_norm.reshape(1, HG_HEAD)
    tril = jnp.tril(jnp.ones((GM_CHUNK, GM_CHUNK), F32))
    ws = gm_w_s[0] * tril[None]
    ws_t = ws.transpose(0, 2, 1)
    bsb = jnp.broadcast_to(gm_b_s[0][:, :, None], (GM_GROUPS, GM_CHUNK, GM_GDIM))

    res_ws = (0.5, 1.0, 0.5)

    def vecs(i, s):
        return (npre[i, s].reshape(1, D_MODEL), npost[i, s].reshape(1, D_MODEL),
                mod[i, 3 * s], mod[i, 3 * s + 1], mod[i, 3 * s + 2])

    order = [(i, s) for i in range(2) for s in range(3)]
    saved = {}
    xs = x0
    for pos, (i, s) in enumerate(order):
        tag = f"l{i}s{s}"
        pre_g, post_g, shift, scale, gate = vecs(i, s)
        rider = _Exchange("gather", riders[tag]) if tag in riders else None
        if s != 1:
            if pos == 0:
                (pq, a, h), got = _ffn_in(xs, w_fi[0, 0], "ffn_in_" + tag, opening=(pre_g, scale, shift), exchange=rider)
            else:
                (pq, a), got = _ffn_in(h, w_fi[i, s // 2], "ffn_in_" + tag, exchange=rider)
            extra = (pq, a)
            if tag == "l0s0":
                w_fo[0, 0], w_hi, w_ho, w_fo[0, 1] = map(whole, got)
            elif tag == "l0s2":
                w_fi[1, 0], w_fo[1, 0] = map(whole, got)
            elif tag == "l1s0":
                w_fi[1, 1], w_fo[1, 1] = map(whole, got)
            wo = w_fo[i, s // 2]
        elif i == 0:
            p_qig, p_f, *got = _mm_blocks(h, w_hi, "hg_in", chunk=1024, f32_cols=(D_MODEL, 2 * D_MODEL), exchange=rider)
            proj = (p_qig, p_f)
            w_fi[0, 1], = map(whole, got)
            o, og, states, got = _hgrn_fwd(proj, lb0, on, "hg_mix", exchange=_Exchange("gather", riders["hg_mix"]))
            w_mi, w_mo = map(whole, got)
            a, wo = og, w_ho
            extra = (proj, o, og, states)
        else:
            pre, gp = _mm_blocks(h, w_mi, "gm_in", bias=b_in, gelu=True)
            a = _gm_spatial_fwd(pre, ln_g, ln_b, ws, bsb, "gm_mix")
            wo = w_mo
            extra = (pre, gp, a)
        if pos + 1 < len(order):
            npre_g, _, nshift, nscale, _ = vecs(*order[pos + 1])
            y, x_next, h_next = _out_proj(a, wo, xs, post_g, gate, res_ws[s], (npre_g, nscale, nshift), "out_" + tag)
            saved[tag] = (xs, h, y) + extra
            xs, h = x_next, h_next
        else:
            dx, dy, dgate, dpost, loss_part = _out_proj_last(a, wo, xs, post_g, gate, res_ws[s], target, "out_" + tag)
            saved[tag] = (xs, h, None) + extra
    loss = lax.psum(loss_part[0, 0], ("x", "y", "c"))

    slots = {}
    d_npre = [[None] * 3, [None] * 3]
    d_npost = [[None] * 3, [None] * 3]
    d_mod = [[None] * 9, [None] * 9]
    for pos in reversed(range(len(order))):
        i, s = order[pos]
        tag = f"l{i}s{s}"
        pre_g, _, _, scale, _ = vecs(i, s)
        xin, h = saved[tag][:2]
        if s != 1:
            w_in, wo = w_fi[i, s // 2], w_fo[i, s // 2]
            pq, g = saved[tag][3:]
            dz = _ffn_dgate(dy, wo, pq, "ffn_dgate_" + tag)
            g_out = _mm_wgrad(g, dy, "ffn_out_wgrad_" + tag, xw=1408)
            g_in = _mm_wgrad(dz, h, "ffn_in_wgrad_" + tag, xw=1408)
        elif i == 0:
            proj, o, og, states = saved[tag][3:]
            dog = _mm_blocks(dy, w_ho, "hg_out_dgrad")
            g_out = _mm_wgrad(og, dy, "hg_out_wgrad")
            dz, d_lb0, d_on = _hgrn_bwd(proj, o, dog, states, lb0, on, "hg_mix_bwd")
            w_in = w_hi
            g_in = _mm_wgrad(h, dz, "hg_in_wgrad", yw=1024, split=2)
        else:
            pre, gp, sp = saved[tag][3:]
            dm = _mm_blocks(dy, w_mo, "gm_out_dgrad")
            g_out = _mm_wgrad(sp, dy, "gm_out_wgrad", xw=768)
            dz, d_ws, d_bs, d_lg, d_lbias, d_bin = _gm_spatial_bwd(pre, gp, dm, ln_g, ln_b, ws, ws_t, bsb, "gm_mix_bwd")
            w_in = w_mi
            g_in = _mm_wgrad(h, dz, "gm_in_wgrad", yw=1536, split=2)
        g_out = g_out.reshape(NDEV, -1, D_MODEL)
        g_in = g_in.reshape(NDEV, -1, D_MODEL) if s != 1 else g_in
        d_npost[i][s] = dpost
        d_mod[i][3 * s + 2] = dgate
        rider = _Exchange("scatter", [g_in, g_out])
        if pos > 0:
            pi, ps = order[pos - 1]
            _, ppost_g, _, _, pgate = vecs(pi, ps)
            prev = (saved[f"l{pi}s{ps}"][2], ppost_g, pgate, res_ws[ps])
            dx, dshift, dscale, dpre_g, dy, dgate, dpost, r_in, r_out = _in_grad(
                dz, w_in, dx, xin, pre_g, scale, prev, "in_grad_" + tag, exchange=rider)
        else:
            dx, dshift, dscale, dpre_g, r_in, r_out = _in_grad(
                dz, w_in, dx, xin, pre_g, scale, None, "in_grad_" + tag, exchange=rider)
        slots[tag] = (r_in, r_out)
        d_npre[i][s] = dpre_g
        d_mod[i][3 * s], d_mod[i][3 * s + 1] = dshift, dscale
    grad_x = dx.reshape(x.shape)

    ffn_tags = ["l0s0", "l0s2", "l1s0", "l1s2"]
    s_fi = jnp.stack([slots[t][0] for t in ffn_tags], axis=1).swapaxes(-1, -2)
    s_fo = jnp.stack([slots[t][1] for t in ffn_tags], axis=1)
    (s_hi, s_ho), (s_mi, s_mo) = slots["l0s1"], slots["l1s1"]
    s_hi, s_ho, s_mi, s_mo = s_hi[:, None], s_ho[:, None], s_mi[:, None], s_mo[:, None]

    gmod = jnp.stack([jnp.concatenate(d_mod[i], axis=0) for i in range(2)])
    d_sm = lb0 * d_lb0
    d_hg_lb = jnp.concatenate([d_sm, jnp.zeros((2, D_MODEL), F32)], axis=0) - sm * d_sm
    small = [gmod, jnp.stack([jnp.concatenate(r, axis=0) for r in d_npre]),
             jnp.stack([jnp.concatenate(r, axis=0) for r in d_npost]),
             d_on, d_hg_lb, d_bin, d_lg, d_lbias, d_ws, d_bs[:, :, 0]]
    sizes = [a.size for a in small]
    flat = jnp.concatenate([a.reshape(-1) for a in small])
    rows = -(-flat.size // (8 * 128)) * 8
    flat = jnp.pad(flat, (0, rows * 128 - flat.size)).reshape(rows, 128)
    flats = _all_gather(flat, "gather_small_grads").reshape(NDEV, rows * 128)
    parts, off = [], 0
    for a, n in zip(small, sizes):
        parts.append(flats[:, off:off + n].reshape((NDEV,) + a.shape))
        off += n
    p_mod, p_npre, p_npost, p_on, p_lb, p_bin, p_lg, p_lbias, p_ws, p_bs = parts

    def mine(p, width):
        return lax.dynamic_slice_in_dim(p, me * width, width, axis=p.ndim - 1)

    gmod_cols = mine(p_mod.reshape(NDEV, 2, 9 * D_MODEL), n_ada).transpose(1, 0, 2)
    g_ada_w = _ada_bwd(jnp.pad(c_all.T, ((0, 0), (0, 120))), jnp.pad(gmod_cols, ((0, 0), (0, 120), (0, 0))), "ada_bwd")

    out = {}
    out["ada_w"] = _update(g_ada_w[None], ada_w, m_ada_w, v_ada_w, "adamw_ada_w")
    out["ada_b"] = _update(p_mod.reshape(NDEV, 2, 9 * D_MODEL), ada_b, m_ada_b, v_ada_b, "adamw_ada_b")
    out["norm_pre"] = _update(mine(p_npre, 128), norm_pre, m_norm_pre, v_norm_pre, "adamw_norm_pre")
    out["norm_post"] = _update(mine(p_npost, 128), norm_post, m_norm_post, v_norm_post, "adamw_norm_post")
    out["ffn_w_in"] = _update(s_fi.reshape((NDEV,) + ffn_w_in.shape), ffn_w_in, m_ffn_w_in, v_ffn_w_in, "adamw_ffn_in")
    out["ffn_w_out"] = _update(s_fo.reshape((NDEV,) + ffn_w_out.shape), ffn_w_out, m_ffn_w_out, v_ffn_w_out, "adamw_ffn_out")
    out["hg_w_in"] = _update(s_hi, hg_w_in, m_hg_w_in, v_hg_w_in, "adamw_hg_in")
    out["hg_w_out"] = _update(s_ho, hg_w_out, m_hg_w_out, v_hg_w_out, "adamw_hg_out")
    out["hg_out_norm"] = _update(p_on, hg_out_norm, m_hg_out_norm, v_hg_out_norm, "adamw_hg_norm")
    out["hg_lb"] = _update(p_lb, hg_lb, m_hg_lb, v_hg_lb, "adamw_hg_lb")
    out["gm_w_in"] = _update(s_mi, gm_w_in, m_gm_w_in, v_gm_w_in, "adamw_gm_in")
    out["gm_b_in"] = _update(mine(p_bin.reshape(NDEV, 1, 8 * 768), 768), gm_b_in, m_gm_b_in, v_gm_b_in, "adamw_gm_b_in")
    out["gm_ln_g"] = _update(mine(p_lg.reshape(NDEV, 1, 4 * 768), 384), gm_ln_g, m_gm_ln_g, v_gm_ln_g, "adamw_gm_ln_g")
    out["gm_ln_b"] = _update(mine(p_lbias.reshape(NDEV, 1, 4 * 768), 384), gm_ln_b, m_gm_ln_b, v_gm_ln_b, "adamw_gm_ln_b")
    out["gm_w_s"] = _update(p_ws[:, None], gm_w_s, m_gm_w_s, v_gm_w_s, "adamw_gm_w_s")
    out["gm_b_s"] = _update(p_bs[:, None], gm_b_s, m_gm_b_s, v_gm_b_s, "adamw_gm_b_s")
    out["gm_w_out"] = _update(s_mo, gm_w_out, m_gm_w_out, v_gm_w_out, "adamw_gm_out")

    names = ["ada_w", "ada_b", "norm_pre", "norm_post", "ffn_w_in", "ffn_w_out", "hg_w_in", "hg_w_out",
             "hg_out_norm", "hg_lb", "gm_w_in", "gm_b_in", "gm_ln_g", "gm_ln_b", "gm_w_s", "gm_b_s", "gm_w_out"]
    return (loss, grad_x, *[out[n][0] for n in names], *[out[n][1] for n in names],
            *[out[n][2] for n in names], *[out[n][3] for n in names])
```
